```python
import math
import jax, jax.numpy as jnp
from jax import lax
import numpy as np

D_MODEL = 1024
BATCH = 4
SEQ = 8192
DEPTH = 2

M_HEADS = 4
M_DHV = D_MODEL // M_HEADS
M_DHQK = M_DHV // 2
M_QK = M_HEADS * M_DHQK
M_V = M_HEADS * M_DHV
M_PROJ = 2 * M_QK + 2 * M_V + 2 * M_HEADS
CHUNK = 64
CONV_W = 3
N_MLSTM_LAYERS = (DEPTH + 1) // 2
N_CONV_LAYERS = DEPTH // 2
N_EXPERTS = 16
N_GROUPS = 4
EXPERTS_PER_GROUP = N_EXPERTS // N_GROUPS
TOP_K = 2
D_EXPERT = D_MODEL // 2
EPS = 1e-6

kernel_name = "hybrid_mlstm_shortconv_grouped_moe_adaln"


def _rmsnorm(x, g):
    x32 = x.astype(jnp.float32)
    y = x32 * lax.rsqrt(jnp.mean(x32 * x32, axis=-1, keepdims=True) + EPS)
    return (y * g.astype(jnp.float32)).astype(x.dtype)


def _modulate(h, shift, scale):
    return h * (1 + scale[:, None, :]) + shift[:, None, :]


def _mlstm_mixer(u, w_in, b_gates, norm_g, w_out):
    Bn, S, _ = u.shape
    NC = S // CHUNK
    proj = (u @ w_in).astype(jnp.float32)
    q, k, v, og, gi, gf = jnp.split(
        proj, [M_QK, 2 * M_QK, 2 * M_QK + M_V, 2 * M_QK + 2 * M_V, 2 * M_QK + 2 * M_V + M_HEADS], axis=-1)
    bg = b_gates.astype(jnp.float32)
    ig = gi + bg[:M_HEADS]
    lf = jax.nn.log_sigmoid(gf + bg[M_HEADS:])
    q = q * (M_DHQK ** -0.5)

    def chunks(t, d):
        return t.reshape(Bn, NC, CHUNK, M_HEADS, d).transpose(1, 0, 3, 2, 4)

    def gchunks(t):
        return t.reshape(Bn, NC, CHUNK, M_HEADS).transpose(1, 0, 3, 2)

    qc, kc, vc = chunks(q, M_DHQK), chunks(k, M_DHQK), chunks(v, M_DHV)
    igc, lfc = gchunks(ig), gchunks(lf)
    causal = jnp.tril(jnp.ones((CHUNK, CHUNK), dtype=bool))

    def step(carry, xs):
        C, n, m = carry
        qb, kb, vb, ib, fb = xs
        b = jnp.cumsum(fb, axis=-1)
        logD = jnp.where(causal, b[..., :, None] - b[..., None, :] + ib[..., None, :], -jnp.inf)
        inter = b + m[..., None]
        m_t = jnp.maximum(inter, jnp.max(logD, axis=-1))
        Dm = jnp.exp(logD - m_t[..., None])
        w_inter = jnp.exp(inter - m_t)
        s = jnp.einsum('bhtk,bhsk->bhts', qb, kb) * Dm
        num = w_inter[..., None] * jnp.einsum('bhtk,bhkv->bhtv', qb, C) + jnp.einsum('bhts,bhsv->bhtv', s, vb)
        den = w_inter * jnp.einsum('bhtk,bhk->bht', qb, n) + jnp.sum(s, axis=-1)
        h = num / jnp.maximum(jnp.abs(den), jnp.exp(-m_t))[..., None]
        bL = b[..., -1]
        a = bL[..., None] - b + ib
        m_new = jnp.maximum(bL + m, jnp.max(a, axis=-1))
        wk = jnp.exp(a - m_new[..., None])
        decay = jnp.exp(bL + m - m_new)
        C_new = decay[..., None, None] * C + jnp.einsum('bhs,bhsk,bhsv->bhkv', wk, kb, vb)
        n_new = decay[..., None] * n + jnp.einsum('bhs,bhsk->bhk', wk, kb)
        return (C_new, n_new, m_new), h

    init = (jnp.zeros((Bn, M_HEADS, M_DHQK, M_DHV), jnp.float32),
            jnp.zeros((Bn, M_HEADS, M_DHQK), jnp.float32),
            jnp.zeros((Bn, M_HEADS), jnp.float32))
    _, hc = lax.scan(step, init, (qc, kc, vc, igc, lfc))
    h = hc.transpose(1, 0, 3, 2, 4).reshape(Bn, S, M_HEADS, M_DHV)
    h = h * lax.rsqrt(jnp.mean(h * h, axis=-1, keepdims=True) + EPS)
    h = h.reshape(Bn, S, M_V) * norm_g.astype(jnp.float32) * jax.nn.sigmoid(og)
    return h.astype(u.dtype) @ w_out


def _shortconv_mixer(u, w_in, conv_w, conv_b, w_out):
    bg, cg, hx = jnp.split(u @ w_in, 3, axis=-1)
    z = cg * hx
    zc = lax.conv_general_dilated(
        z, conv_w[:, None, :].astype(z.dtype), window_strides=(1,), padding=[(CONV_W - 1, 0)],
        dimension_numbers=('NWC', 'WIO', 'NWC'), feature_group_count=z.shape[-1]) + conv_b
    return (bg * zc) @ w_out


def _moe(h, w_router, b_router, w_gate, w_up, w_down):
    Bn, S, D = h.shape
    T = Bn * S
    ht = h.reshape(T, D)
    probs = jax.nn.softmax((ht @ w_router).astype(jnp.float32), axis=-1)
    sel = probs + b_router.astype(jnp.float32)
    grp_score = lax.top_k(sel.reshape(T, N_GROUPS, EXPERTS_PER_GROUP), 2)[0].sum(-1)
    best = jnp.argmax(grp_score, axis=-1)
    in_grp = (jnp.arange(N_EXPERTS) // EXPERTS_PER_GROUP)[None, :] == best[:, None]
    _, idx = lax.top_k(jnp.where(in_grp, sel, -jnp.inf), TOP_K)
    wts = jnp.take_along_axis(probs, idx, axis=-1)
    wts = wts / jnp.sum(wts, axis=-1, keepdims=True)
    flat_e = idx.reshape(-1)
    flat_t = jnp.repeat(jnp.arange(T), TOP_K)
    order = jnp.argsort(flat_e)
    tok = flat_t[order]
    sizes = jnp.bincount(flat_e, length=N_EXPERTS).astype(jnp.int32)
    xs = ht[tok]
    a = jax.nn.silu(lax.ragged_dot(xs, w_gate, sizes)) * lax.ragged_dot(xs, w_up, sizes)
    out = lax.ragged_dot(a, w_down, sizes) * wts.reshape(-1)[order][:, None].astype(h.dtype)
    y = jax.ops.segment_sum(out, tok, num_segments=T)
    return y.reshape(Bn, S, D)


def setup_inputs(seed: int = 0) -> dict:
    key = jax.random.key(seed)
    ks = jax.random.split(key, 24)
    D = D_MODEL
    f32 = jnp.float32
    nrm = lambda k, shape, s: jax.random.normal(k, shape, f32) * s
    b_gates = jnp.concatenate([
        nrm(ks[7], (N_MLSTM_LAYERS, M_HEADS), 0.1),
        3.0 + nrm(ks[8], (N_MLSTM_LAYERS, M_HEADS), 0.5)], axis=-1)
    return {
        "x": nrm(ks[0], (BATCH, SEQ, D), 1.0),
        "c": nrm(ks[1], (BATCH, D), 1.0),
        "norm1_g": 1.0 + nrm(ks[2], (DEPTH, D), 0.02),
        "norm2_g": 1.0 + nrm(ks[3], (DEPTH, D), 0.02),
        "w_ada": nrm(ks[4], (DEPTH, D, 6 * D), 0.5 * D ** -0.5),
        "b_ada": nrm(ks[5], (DEPTH, 6 * D), 0.02),
        "m_w_in": nrm(ks[6], (N_MLSTM_LAYERS, D, M_PROJ), D ** -0.5),
        "m_b_gates": b_gates,
        "m_norm_g": 1.0 + nrm(ks[9], (N_MLSTM_LAYERS, M_V), 0.02),
        "m_w_out": nrm(ks[10], (N_MLSTM_LAYERS, M_V, D), M_V ** -0.5),
        "c_w_in": nrm(ks[11], (N_CONV_LAYERS, D, 3 * D), D ** -0.5),
        "c_conv_w": nrm(ks[12], (N_CONV_LAYERS, CONV_W, D), CONV_W ** -0.5),
        "c_conv_b": nrm(ks[13], (N_CONV_LAYERS, D), 0.02),
        "c_w_out": nrm(ks[14], (N_CONV_LAYERS, D, D), D ** -0.5),
        "w_router": nrm(ks[15], (D, N_EXPERTS), D ** -0.5),
        "b_router": nrm(ks[16], (N_EXPERTS,), 0.01),
        "e_w_gate": nrm(ks[17], (DEPTH, N_EXPERTS, D, D_EXPERT), D ** -0.5),
        "e_w_up": nrm(ks[18], (DEPTH, N_EXPERTS, D, D_EXPERT), D ** -0.5),
        "e_w_down": nrm(ks[19], (DEPTH, N_EXPERTS, D_EXPERT, D), D_EXPERT ** -0.5),
        "final_g": 1.0 + nrm(ks[20], (D,), 0.02),
    }


def reference(x, c, norm1_g, norm2_g, w_ada, b_ada, m_w_in, m_b_gates, m_norm_g, m_w_out,
              c_w_in, c_conv_w, c_conv_b, c_w_out, w_router, b_router,
              e_w_gate, e_w_up, e_w_down, final_g):
    cond = jax.nn.silu(c)
    for i in range(DEPTH):
        mod = cond @ w_ada[i] + b_ada[i]
        sh1, sc1, g1, sh2, sc2, g2 = jnp.split(mod, 6, axis=-1)
        u = _modulate(_rmsnorm(x, norm1_g[i]), sh1, sc1)
        j = i // 2
        if i % 2 == 0:
            mix = _mlstm_mixer(u, m_w_in[j], m_b_gates[j], m_norm_g[j], m_w_out[j])
        else:
            mix = _shortconv_mixer(u, c_w_in[j], c_conv_w[j], c_conv_b[j], c_w_out[j])
        x = x + g1[:, None, :] * mix
        u = _modulate(_rmsnorm(x, norm2_g[i]), sh2, sc2)
        x = x + g2[:, None, :] * _moe(u, w_router, b_router, e_w_gate[i], e_w_up[i], e_w_down[i])
    return _rmsnorm(x, final_g)
```

```python
import functools

import jax
import jax.numpy as jnp
import numpy as np
from jax import lax
from jax.experimental import pallas as pl
from jax.experimental.pallas import tpu as pltpu

F32 = jnp.float32
BF16 = jnp.bfloat16

D_MODEL = 1024
BATCH = 4
SEQ = 8192
TOKENS = BATCH * SEQ
N_HEADS = 4
DH_V = 256
DH_QK = 128
QK = N_HEADS * DH_QK
N_EXPERTS = 16
N_GROUPS = 4
EXPERTS_PER_GROUP = 4
D_EXPERT = 512
EPS = 1e-6

LANES = 128
SUBLANES = 8
VMEM_LIMIT_BYTES = 56 * 1024 * 1024

CHUNK = 128
TM_PROJ = 512
TM_MIX = 256
TM_EXPERT = 256
GATE_COLS = LANES
ROUTER_COLS = LANES
PAIRS = ((0, 1), (0, 2), (0, 3), (1, 2), (1, 3), (2, 3))
N_CLASSES = N_GROUPS * len(PAIRS)
CLASS_ROWS = 32
N_EXPERT_TILES = TOKENS // TM_EXPERT + N_CLASSES
PADDED_ROWS = N_EXPERT_TILES * TM_EXPERT
PERMUTE_BLOCK = 1024


def _params(*semantics):
    return pltpu.CompilerParams(dimension_semantics=semantics, vmem_limit_bytes=VMEM_LIMIT_BYTES)


def _dot(a, b):
    return jnp.dot(a, b, preferred_element_type=F32)


def _rms(x):
    return x * lax.rsqrt(jnp.mean(x * x, axis=-1, keepdims=True) + EPS)


def _sigmoid(x):
    return 1.0 / (1.0 + jnp.exp(-x))


def _ada_kernel(c_ref, w_ref, b_ref, o_ref):
    c = c_ref[...]
    cond = c * _sigmoid(c)
    o_ref[0] = jnp.dot(cond, w_ref[0], preferred_element_type=F32,
                       precision=lax.Precision.HIGHEST) + b_ref[0]


def _ada(c, w_ada, b_ada):
    depth, d, n = w_ada.shape
    tn = 1536
    c8 = jnp.zeros((SUBLANES, d), F32).at[:BATCH].set(c)
    out = pl.pallas_call(
        _ada_kernel,
        out_shape=jax.ShapeDtypeStruct((depth, SUBLANES, n), F32),
        grid=(depth, n // tn),
        in_specs=[
            pl.BlockSpec((SUBLANES, d), lambda l, j: (0, 0)),
            pl.BlockSpec((1, d, tn), lambda l, j: (l, 0, j)),
            pl.BlockSpec((1, 1, tn), lambda l, j: (l, 0, j)),
        ],
        out_specs=pl.BlockSpec((1, SUBLANES, tn), lambda l, j: (l, 0, j)),
        compiler_params=_params("arbitrary", "arbitrary"),
        name="ada_mod",
    )(c8, w_ada, b_ada.reshape(depth, 1, n))
    return out[:, :BATCH].reshape(depth, BATCH, 6, d)


def _mlstm_proj_kernel(x_ref, mod_ref, g_ref, w_ref, bg_ref, q_ref, k_ref, v_ref, og_ref,
                       gcol_ref, grow_ref):
    x = x_ref[...]
    m = mod_ref[0]
    u = (_rms(x) * g_ref[...] * (1.0 + m[1:2]) + m[0:1]).astype(BF16)
    q_ref[...] = (_dot(u, w_ref[:, 0:QK]) * (DH_QK ** -0.5)).astype(BF16)
    k_ref[...] = _dot(u, w_ref[:, QK:2 * QK]).astype(BF16)
    v_ref[...] = _dot(u, w_ref[:, 2 * QK:2 * QK + D_MODEL]).astype(BF16)
    og_ref[...] = _dot(u, w_ref[:, 2 * QK + D_MODEL:2 * QK + 2 * D_MODEL]).astype(BF16)
    gt = _dot(u, w_ref[:, 2 * QK + 2 * D_MODEL:]) + bg_ref[...]
    tm = gt.shape[0]
    lane = lax.broadcasted_iota(jnp.int32, gt.shape, 1)
    row = lax.broadcasted_iota(jnp.int32, gt.shape, 0)
    log_f = jnp.minimum(gt, 0.0) - jnp.log(1.0 + jnp.exp(-jnp.abs(gt)))
    z = jnp.where(lane < N_HEADS, 0.0, log_f)
    pos = row % CHUNK
    shift = 1
    while shift < CHUNK:
        z = z + jnp.where(pos >= shift, pltpu.roll(z, shift, axis=0), 0.0)
        shift *= 2
    z = jnp.where(lane < N_HEADS, gt, z)
    gcol_ref[...] = z[:, 0:SUBLANES]
    zr = jnp.where(lane < N_HEADS, z - pltpu.roll(z, LANES - N_HEADS, axis=1), z)
    grow_ref[...] = zr.T[0:SUBLANES, :]
    del tm


def _mlstm_proj(x, mod, g, w, bg):
    t, d = x.shape
    n = w.shape[1]
    tm = TM_PROJ
    tiles_per_batch = SEQ // tm
    return pl.pallas_call(
        _mlstm_proj_kernel,
        out_shape=(
            jax.ShapeDtypeStruct((t, QK), BF16),
            jax.ShapeDtypeStruct((t, QK), BF16),
            jax.ShapeDtypeStruct((t, D_MODEL), BF16),
            jax.ShapeDtypeStruct((t, D_MODEL), BF16),
            jax.ShapeDtypeStruct((t, SUBLANES), F32),
            jax.ShapeDtypeStruct((SUBLANES, t), F32),
        ),
        grid=(t // tm,),
        in_specs=[
            pl.BlockSpec((tm, d), lambda i: (i, 0)),
            pl.BlockSpec((1, 6, d), lambda i: (i // tiles_per_batch, 0, 0)),
            pl.BlockSpec((1, d), lambda i: (0, 0)),
            pl.BlockSpec((d, n), lambda i: (0, 0)),
            pl.BlockSpec((1, GATE_COLS), lambda i: (0, 0)),
        ],
        out_specs=(
            pl.BlockSpec((tm, QK), lambda i: (i, 0)),
            pl.BlockSpec((tm, QK), lambda i: (i, 0)),
            pl.BlockSpec((tm, D_MODEL), lambda i: (i, 0)),
            pl.BlockSpec((tm, D_MODEL), lambda i: (i, 0)),
            pl.BlockSpec((tm, SUBLANES), lambda i: (i, 0)),
            pl.BlockSpec((SUBLANES, tm), lambda i: (0, i)),
        ),
        compiler_params=_params("arbitrary"),
        name="mlstm_proj",
    )(x, mod, g, w, bg)


def _mlstm_core_kernel(q_ref, k_ref, v_ref, og_ref, gcol_ref, grow_ref, ng_ref, o_ref,
                       c_ref, n_ref, m_ref):
    @pl.when(pl.program_id(1) == 0)
    def _():
        c_ref[...] = jnp.zeros_like(c_ref)
        n_ref[...] = jnp.zeros_like(n_ref)
        m_ref[...] = jnp.zeros_like(m_ref)

    ln = CHUNK
    row = lax.broadcasted_iota(jnp.int32, (ln, ln), 0)
    col = lax.broadcasted_iota(jnp.int32, (ln, ln), 1)
    causal = col <= row
    gcol = gcol_ref[...]
    grow = grow_ref[...]
    for h in range(N_HEADS):
        qh = q_ref[:, h * DH_QK:(h + 1) * DH_QK]
        kh = k_ref[:, h * DH_QK:(h + 1) * DH_QK]
        vh = v_ref[:, h * DH_V:(h + 1) * DH_V]
        ig = gcol[:, h:h + 1]
        b = gcol[:, N_HEADS + h:N_HEADS + h + 1]
        imb = grow[h:h + 1, :]
        m_prev = m_ref[h][:, 0:1]
        c_prev = c_ref[h]
        n_prev = n_ref[h]

        log_d = jnp.where(causal, b + imb, -jnp.inf)
        inter = b + m_prev
        m_t = jnp.maximum(inter, jnp.max(log_d, axis=-1, keepdims=True))
        d_mat = jnp.exp(log_d - m_t)
        w_inter = jnp.exp(inter - m_t)
        s = lax.dot_general(qh, kh, (((1,), (1,)), ((), ())), preferred_element_type=F32) * d_mat
        num = w_inter * _dot(qh, c_prev.astype(BF16)) + _dot(s.astype(BF16), vh)
        qn = jnp.sum(qh.astype(F32) * n_prev, axis=-1, keepdims=True)
        den = w_inter * qn + jnp.sum(s, axis=-1, keepdims=True)
        hh = num / jnp.maximum(jnp.abs(den), jnp.exp(-m_t))

        b_last = b[ln - 1:ln, :]
        a = b_last - b + ig
        m_new = jnp.maximum(b_last + m_prev, jnp.max(a, axis=0, keepdims=True))
        kw = (kh.astype(F32) * jnp.exp(a - m_new)).astype(BF16)
        decay = jnp.exp(b_last + m_prev - m_new)
        c_ref[h] = decay * c_prev + lax.dot_general(
            kw, vh, (((0,), (0,)), ((), ())), preferred_element_type=F32)
        n_ref[h] = decay * n_prev + jnp.sum(kw.astype(F32), axis=0, keepdims=True)
        m_ref[h] = jnp.broadcast_to(m_new, (1, LANES))

        sl = slice(h * DH_V, (h + 1) * DH_V)
        gate = _sigmoid(og_ref[:, sl].astype(F32))
        o_ref[:, sl] = (_rms(hh) * ng_ref[:, sl] * gate).astype(BF16)


def _mlstm_core(q, k, v, og, gcol, grow, norm_g):
    t = q.shape[0]
    nc = SEQ // CHUNK
    rows = lambda b, c: (b * nc + c, 0)
    return pl.pallas_call(
        _mlstm_core_kernel,
        out_shape=jax.ShapeDtypeStruct((t, D_MODEL), BF16),
        grid=(BATCH, nc),
        in_specs=[
            pl.BlockSpec((CHUNK, QK), rows),
            pl.BlockSpec((CHUNK, QK), rows),
            pl.BlockSpec((CHUNK, D_MODEL), rows),
            pl.BlockSpec((CHUNK, D_MODEL), rows),
            pl.BlockSpec((CHUNK, SUBLANES), rows),
            pl.BlockSpec((SUBLANES, CHUNK), lambda b, c: (0, b * nc + c)),
            pl.BlockSpec((1, D_MODEL), lambda b, c: (0, 0)),
        ],
        out_specs=pl.BlockSpec((CHUNK, D_MODEL), rows),
        scratch_shapes=[
            pltpu.VMEM((N_HEADS, DH_QK, DH_V), F32),
            pltpu.VMEM((N_HEADS, 1, DH_QK), F32),
            pltpu.VMEM((N_HEADS, 1, LANES), F32),
        ],
        compiler_params=_params("arbitrary", "arbitrary"),
        name="mlstm_core",
    )(q, k, v, og, gcol, grow, norm_g)


def _top2_sum(v0, v1, v2, v3):
    hi1, lo1 = jnp.maximum(v0, v1), jnp.minimum(v0, v1)
    hi2, lo2 = jnp.maximum(v2, v3), jnp.minimum(v2, v3)
    return jnp.maximum(hi1, hi2) + jnp.maximum(jnp.minimum(hi1, hi2), jnp.maximum(lo1, lo2))


def _route_tail(x_new, m, g2_ref, wr_ref, br_ref, tri_ref, u2_ref, cls_ref, rank_ref, cnt_ref,
                run_ref):
    @pl.when(pl.program_id(0) == 0)
    def _():
        run_ref[...] = jnp.zeros_like(run_ref)

    u2 = _rms(x_new) * g2_ref[...] * (1.0 + m[4:5]) + m[3:4]
    u2_ref[...] = u2
    logits = _dot(u2.astype(BF16), wr_ref[...])
    lt = logits.T[0:N_EXPERTS, :]
    e = jnp.exp(lt - jnp.max(lt, axis=0, keepdims=True))
    probs = e / jnp.sum(e, axis=0, keepdims=True)
    sel = probs + br_ref[...]
    sel_rows = [sel[j:j + 1, :] for j in range(N_EXPERTS)]
    best = jnp.zeros_like(sel_rows[0], dtype=jnp.int32)
    best_score = _top2_sum(*sel_rows[0:EXPERTS_PER_GROUP])
    for g in range(1, N_GROUPS):
        score = _top2_sum(*sel_rows[g * EXPERTS_PER_GROUP:(g + 1) * EXPERTS_PER_GROUP])
        better = score > best_score
        best = jnp.where(better, g, best)
        best_score = jnp.where(better, score, best_score)
    s = []
    for j in range(EXPERTS_PER_GROUP):
        sj = sel_rows[j]
        for g in range(1, N_GROUPS):
            sj = jnp.where(best == g, sel_rows[g * EXPERTS_PER_GROUP + j], sj)
        s.append(sj)
    chosen = []
    for j in range(EXPERTS_PER_GROUP):
        beaten = jnp.zeros_like(best)
        for i in range(EXPERTS_PER_GROUP):
            if i == j:
                continue
            wins = (s[i] >= s[j]) if i < j else (s[i] > s[j])
            beaten = beaten + jnp.where(wins, 1, 0)
        chosen.append(beaten < 2)
    pair = jnp.full_like(best, len(PAIRS) - 1)
    for p in range(len(PAIRS) - 2, -1, -1):
        a, b = PAIRS[p]
        pair = jnp.where(jnp.logical_and(chosen[a], chosen[b]), p, pair)
    cls = best * len(PAIRS) + pair
    cls_ref[0] = cls

    class_id = lax.broadcasted_iota(jnp.int32, (CLASS_ROWS, cls.shape[1]), 0)
    onehot = class_id == cls
    before = _dot(jnp.where(onehot, 1.0, 0.0).astype(BF16), tri_ref[...])
    run = run_ref[...]
    rank = jnp.sum(jnp.where(onehot, before + run[:, 0:1], 0.0), axis=0, keepdims=True)
    rank_ref[0] = rank.astype(jnp.int32)
    run = run + jnp.sum(jnp.where(onehot, 1.0, 0.0), axis=1, keepdims=True)
    run_ref[...] = run
    cnt_ref[...] = run.astype(jnp.int32)


def _route_out_shapes(t, tm):
    return (
        jax.ShapeDtypeStruct((t, D_MODEL), F32),
        jax.ShapeDtypeStruct((t, D_MODEL), F32),
        jax.ShapeDtypeStruct((t // tm, 1, tm), jnp.int32),
        jax.ShapeDtypeStruct((t // tm, 1, tm), jnp.int32),
        jax.ShapeDtypeStruct((CLASS_ROWS, LANES), jnp.int32),
    )


def _route_out_specs(tm):
    return (
        pl.BlockSpec((tm, D_MODEL), lambda i: (i, 0)),
        pl.BlockSpec((tm, D_MODEL), lambda i: (i, 0)),
        pl.BlockSpec((1, 1, tm), lambda i: (i, 0, 0)),
        pl.BlockSpec((1, 1, tm), lambda i: (i, 0, 0)),
        pl.BlockSpec((CLASS_ROWS, LANES), lambda i: (0, 0)),
    )


def _const_spec(shape):
    return pl.BlockSpec(shape, lambda i: (0,) * len(shape))


def _mlstm_out_kernel(h_ref, x_ref, mod_ref, wo_ref, g2_ref, wr_ref, br_ref, tri_ref,
                      x1_ref, u2_ref, cls_ref, rank_ref, cnt_ref, run_ref):
    m = mod_ref[0]
    x1 = x_ref[...] + m[2:3] * _dot(h_ref[...], wo_ref[...])
    x1_ref[...] = x1
    _route_tail(x1, m, g2_ref, wr_ref, br_ref, tri_ref, u2_ref, cls_ref, rank_ref, cnt_ref, run_ref)


def _mlstm_out(hg, x, mod, w_out, g2, wr, br, tri):
    t = x.shape[0]
    tm = TM_MIX
    tiles_per_batch = SEQ // tm
    return pl.pallas_call(
        _mlstm_out_kernel,
        out_shape=_route_out_shapes(t, tm),
        grid=(t // tm,),
        in_specs=[
            pl.BlockSpec((tm, D_MODEL), lambda i: (i, 0)),
            pl.BlockSpec((tm, D_MODEL), lambda i: (i, 0)),
            pl.BlockSpec((1, 6, D_MODEL), lambda i: (i // tiles_per_batch, 0, 0)),
            _const_spec((D_MODEL, D_MODEL)),
            _const_spec((1, D_MODEL)),
            _const_spec((D_MODEL, ROUTER_COLS)),
            _const_spec((N_EXPERTS, 1)),
            _const_spec((tm, tm)),
        ],
        out_specs=_route_out_specs(tm),
        scratch_shapes=[pltpu.VMEM((CLASS_ROWS, LANES), F32)],
        compiler_params=_params("arbitrary"),
        name="mlstm_out_route",
    )(hg, x, mod, w_out, g2, wr, br, tri)


def _conv_layer_kernel(x_ref, y_ref, mod0_ref, mod_ref, g1_ref, wi_ref, cw_ref, cb_ref, wo_ref,
                       g2_ref, wr_ref, br_ref, tri_ref,
                       x3_ref, u2_ref, cls_ref, rank_ref, cnt_ref, run_ref, carry_ref):
    tm = x_ref.shape[0]
    tiles_per_batch = SEQ // tm

    @pl.when(pl.program_id(0) % tiles_per_batch == 0)
    def _():
        carry_ref[...] = jnp.zeros_like(carry_ref)

    m = mod_ref[0]
    x2 = x_ref[...] + mod0_ref[0][5:6] * y_ref[...]
    u = (_rms(x2) * g1_ref[...] * (1.0 + m[1:2]) + m[0:1]).astype(BF16)
    bgate = _dot(u, wi_ref[:, 0:D_MODEL])
    z = _dot(u, wi_ref[:, D_MODEL:2 * D_MODEL]) * _dot(u, wi_ref[:, 2 * D_MODEL:])
    prev = carry_ref[...]
    row = lax.broadcasted_iota(jnp.int32, z.shape, 0)
    z1 = jnp.where(row == 0, prev[7:8], pltpu.roll(z, 1, axis=0))
    z2 = jnp.where(row == 0, prev[6:7], jnp.where(row == 1, prev[7:8], pltpu.roll(z, 2, axis=0)))
    carry_ref[...] = z[tm - SUBLANES:, :]
    cw = cw_ref[...]
    zc = cw[0:1] * z2 + cw[1:2] * z1 + cw[2:3] * z + cb_ref[...]
    x3 = x2 + m[2:3] * _dot((bgate * zc).astype(BF16), wo_ref[...])
    x3_ref[...] = x3
    _route_tail(x3, m, g2_ref, wr_ref, br_ref, tri_ref, u2_ref, cls_ref, rank_ref, cnt_ref, run_ref)


def _conv_layer(x, y, mod0, mod, g1, w_in, conv_w, conv_b, w_out, g2, wr, br, tri):
    t = x.shape[0]
    tm = TM_MIX
    tiles_per_batch = SEQ // tm
    mod_spec = pl.BlockSpec((1, 6, D_MODEL), lambda i: (i // tiles_per_batch, 0, 0))
    return pl.pallas_call(
        _conv_layer_kernel,
        out_shape=_route_out_shapes(t, tm),
        grid=(t // tm,),
        in_specs=[
            pl.BlockSpec((tm, D_MODEL), lambda i: (i, 0)),
            pl.BlockSpec((tm, D_MODEL), lambda i: (i, 0)),
            mod_spec,
            mod_spec,
            _const_spec((1, D_MODEL)),
            _const_spec((D_MODEL, 3 * D_MODEL)),
            _const_spec((3, D_MODEL)),
            _const_spec((1, D_MODEL)),
            _const_spec((D_MODEL, D_MODEL)),
            _const_spec((1, D_MODEL)),
            _const_spec((D_MODEL, ROUTER_COLS)),
            _const_spec((N_EXPERTS, 1)),
            _const_spec((tm, tm)),
        ],
        out_specs=_route_out_specs(tm),
        scratch_shapes=[pltpu.VMEM((CLASS_ROWS, LANES), F32), pltpu.VMEM((SUBLANES, D_MODEL), F32)],
        compiler_params=_params("arbitrary"),
        name="conv_layer_route",
    )(x, y, mod0, mod, g1, w_in, conv_w, conv_b, w_out, g2, wr, br, tri)


def _permute_kernel(idx_ref, src_ref, dst_ref, sem, *, scatter, n_rows):
    n_blocks = n_rows // PERMUTE_BLOCK

    def row_copy(t):
        d = idx_ref[t]
        if scatter:
            return pltpu.make_async_copy(src_ref.at[pl.ds(t, 1)], dst_ref.at[pl.ds(d, 1)], sem)
        return pltpu.make_async_copy(src_ref.at[pl.ds(d, 1)], dst_ref.at[pl.ds(t, 1)], sem)

    def wait_block():
        pltpu.make_async_copy(src_ref.at[pl.ds(0, PERMUTE_BLOCK)],
                              dst_ref.at[pl.ds(0, PERMUTE_BLOCK)], sem).wait()

    def block_body(blk, carry):
        def issue(r, c):
            row_copy(blk * PERMUTE_BLOCK + r).start()
            return c

        lax.fori_loop(0, PERMUTE_BLOCK, issue, 0)

        @pl.when(blk > 0)
        def _():
            wait_block()

        return carry

    lax.fori_loop(0, n_blocks, block_body, 0)
    wait_block()


def _permute_rows(idx, src, n_out, scatter, init=None):
    n_rows = idx.shape[0]
    kernel = functools.partial(_permute_kernel, scatter=scatter, n_rows=n_rows)
    any_spec = pl.BlockSpec(memory_space=pl.ANY)
    out_shape = jax.ShapeDtypeStruct((n_out, src.shape[1]), src.dtype)
    if init is None:
        return pl.pallas_call(
            kernel,
            out_shape=out_shape,
            grid_spec=pltpu.PrefetchScalarGridSpec(
                num_scalar_prefetch=1, grid=(1,), in_specs=[any_spec], out_specs=any_spec,
                scratch_shapes=[pltpu.SemaphoreType.DMA(())]),
            compiler_params=_params("arbitrary"),
            name="gather_rows",
        )(idx, src)

    def aliased(idx_ref, src_ref, init_ref, dst_ref, sem):
        del init_ref
        kernel(idx_ref, src_ref, dst_ref, sem)

    return pl.pallas_call(
        aliased,
        out_shape=out_shape,
        grid_spec=pltpu.PrefetchScalarGridSpec(
            num_scalar_prefetch=1, grid=(1,), in_specs=[any_spec, any_spec], out_specs=any_spec,
            scratch_shapes=[pltpu.SemaphoreType.DMA(())]),
        input_output_aliases={2: 0},
        compiler_params=_params("arbitrary"),
        name="scatter_rows",
    )(idx, src, init)


def _expert_kernel(ea_ref, eb_ref, nused_ref, x_ref, wga_ref, wua_ref, wda_ref, wgb_ref, wub_ref,
                   wdb_ref, wra_ref, wrb_ref, y_ref):
    del ea_ref, eb_ref

    @pl.when(pl.program_id(0) < nused_ref[0])
    def _():
        xb = x_ref[...].astype(BF16)
        dl = jnp.sum(xb.astype(F32) * (wra_ref[0] - wrb_ref[0]), axis=-1, keepdims=True)
        w_a = _sigmoid(dl)
        w_b = _sigmoid(-dl)

        def ffn(wg_ref, wu_ref, wd_ref):
            gate = _dot(xb, wg_ref[0])
            hidden = gate * _sigmoid(gate) * _dot(xb, wu_ref[0])
            return _dot(hidden.astype(BF16), wd_ref[0])

        y_ref[...] = w_a * ffn(wga_ref, wua_ref, wda_ref) + w_b * ffn(wgb_ref, wub_ref, wdb_ref)

    @pl.when(pl.program_id(0) >= nused_ref[0])
    def _():
        y_ref[...] = jnp.zeros_like(y_ref)


def _experts(ea, eb, n_used, xs, w_gate, w_up, w_down, wr_rows):
    tm = TM_EXPERT
    row_tile = lambda j, ea, eb, nu: (jnp.minimum(j, nu[0] - 1), 0)
    sel_a = lambda j, ea, eb, nu: (ea[j], 0, 0)
    sel_b = lambda j, ea, eb, nu: (eb[j], 0, 0)
    up_spec = lambda sel: pl.BlockSpec((1, D_MODEL, D_EXPERT), sel)
    down_spec = lambda sel: pl.BlockSpec((1, D_EXPERT, D_MODEL), sel)
    wr_spec = lambda sel: pl.BlockSpec((1, 1, D_MODEL), sel)
    return pl.pallas_call(
        _expert_kernel,
        out_shape=jax.ShapeDtypeStruct((PADDED_ROWS, D_MODEL), F32),
        grid_spec=pltpu.PrefetchScalarGridSpec(
            num_scalar_prefetch=3,
            grid=(N_EXPERT_TILES,),
            in_specs=[
                pl.BlockSpec((tm, D_MODEL), row_tile),
                up_spec(sel_a), up_spec(sel_a), down_spec(sel_a),
                up_spec(sel_b), up_spec(sel_b), down_spec(sel_b),
                wr_spec(sel_a), wr_spec(sel_b),
            ],
            out_specs=pl.BlockSpec((tm, D_MODEL), lambda j, ea, eb, nu: (j, 0)),
        ),
        compiler_params=_params("arbitrary"),
        name="grouped_experts",
    )(ea, eb, n_used, xs, w_gate, w_up, w_down, w_gate, w_up, w_down, wr_rows, wr_rows)


def _final_kernel(x_ref, y_ref, mod_ref, g_ref, o_ref):
    x = x_ref[...] + mod_ref[0][5:6] * y_ref[...]
    o_ref[...] = _rms(x) * g_ref[...]


def _final(x, y, mod, g):
    t = x.shape[0]
    tm = 1024
    tiles_per_batch = SEQ // tm
    return pl.pallas_call(
        _final_kernel,
        out_shape=jax.ShapeDtypeStruct((t, D_MODEL), F32),
        grid=(t // tm,),
        in_specs=[
            pl.BlockSpec((tm, D_MODEL), lambda i: (i, 0)),
            pl.BlockSpec((tm, D_MODEL), lambda i: (i, 0)),
            pl.BlockSpec((1, 6, D_MODEL), lambda i: (i // tiles_per_batch, 0, 0)),
            _const_spec((1, D_MODEL)),
        ],
        out_specs=pl.BlockSpec((tm, D_MODEL), lambda i: (i, 0)),
        compiler_params=_params("arbitrary"),
        name="final_norm",
    )(x, y, mod, g)


_PAIR_A = np.array([EXPERTS_PER_GROUP * (c // len(PAIRS)) + PAIRS[c % len(PAIRS)][0]
                    for c in range(N_CLASSES)], np.int32)
_PAIR_B = np.array([EXPERTS_PER_GROUP * (c // len(PAIRS)) + PAIRS[c % len(PAIRS)][1]
                    for c in range(N_CLASSES)], np.int32)


def _moe(u2, cls, rank, cnt, w_gate, w_up, w_down, wr_rows):
    counts = cnt[:N_CLASSES, 0]
    tiles = (counts + TM_EXPERT - 1) // TM_EXPERT
    tile_end = jnp.cumsum(tiles)
    row_start = (tile_end - tiles) * TM_EXPERT
    n_used = tile_end[-1:]
    cls = cls.reshape(-1)
    dest = row_start[cls] + rank.reshape(-1)
    tile_cls = jnp.sum(jnp.arange(N_EXPERT_TILES)[:, None] >= tile_end[None, :], axis=1)
    tile_cls = jnp.minimum(tile_cls, tile_cls[jnp.maximum(n_used[0] - 1, 0)])
    ea = jnp.asarray(_PAIR_A)[tile_cls]
    eb = jnp.asarray(_PAIR_B)[tile_cls]
    xs = _permute_rows(dest, u2, PADDED_ROWS, scatter=True,
                       init=jnp.zeros((PADDED_ROWS, D_MODEL), F32))
    ys = _experts(ea, eb, n_used.astype(jnp.int32), xs, w_gate, w_up, w_down, wr_rows)
    return _permute_rows(dest, ys, TOKENS, scatter=False)


def kernel(x, c, norm1_g, norm2_g, w_ada, b_ada, m_w_in, m_b_gates, m_norm_g, m_w_out,
           c_w_in, c_conv_w, c_conv_b, c_w_out, w_router, b_router,
           e_w_gate, e_w_up, e_w_down, final_g):
    xf = x.reshape(TOKENS, D_MODEL)
    mod = _ada(c, w_ada, b_ada)

    w_in = jnp.concatenate(
        [m_w_in[0].astype(BF16),
         jnp.zeros((D_MODEL, GATE_COLS - 2 * N_HEADS), BF16)], axis=1)
    bg = jnp.zeros((1, GATE_COLS), F32).at[0, :2 * N_HEADS].set(m_b_gates[0])
    wr = jnp.zeros((D_MODEL, ROUTER_COLS), BF16).at[:, :N_EXPERTS].set(w_router.astype(BF16))
    wr_rows = w_router.astype(BF16).astype(F32).T.reshape(N_EXPERTS, 1, D_MODEL)
    br = b_router.reshape(N_EXPERTS, 1)
    tri = (jnp.arange(TM_MIX)[:, None] < jnp.arange(TM_MIX)[None, :]).astype(BF16)
    row = lambda v: v.reshape(1, -1)

    q, k, v, og, gcol, grow = _mlstm_proj(xf, mod[0], row(norm1_g[0]), w_in, bg)
    hg = _mlstm_core(q, k, v, og, gcol, grow, row(m_norm_g[0]))
    x1, u2, cls, rank, cnt = _mlstm_out(hg, xf, mod[0], m_w_out[0].astype(BF16), row(norm2_g[0]),
                                        wr, br, tri)
    y = _moe(u2, cls, rank, cnt, e_w_gate[0].astype(BF16), e_w_up[0].astype(BF16),
             e_w_down[0].astype(BF16), wr_rows)

    x3, u2, cls, rank, cnt = _conv_layer(
        x1, y, mod[0], mod[1], row(norm1_g[1]), c_w_in[0].astype(BF16), c_conv_w[0],
        row(c_conv_b[0]), c_w_out[0].astype(BF16), row(norm2_g[1]), wr, br, tri)
    y = _moe(u2, cls, rank, cnt, e_w_gate[1].astype(BF16), e_w_up[1].astype(BF16),
             e_w_down[1].astype(BF16), wr_rows)

    out = _final(x3, y, mod[1], row(final_g))
    return out.reshape(BATCH, SEQ, D_MODEL)
```

```python
import functools

import jax
import jax.numpy as jnp
import numpy as np
from jax import lax
from jax.experimental import pallas as pl
from jax.experimental.pallas import tpu as pltpu

F32 = jnp.float32
BF16 = jnp.bfloat16

D_MODEL = 1024
BATCH = 4
SEQ = 8192
TOKENS = BATCH * SEQ
N_HEADS = 4
DH_V = 256
DH_QK = 128
QK = N_HEADS * DH_QK
N_EXPERTS = 16
N_GROUPS = 4
EXPERTS_PER_GROUP = 4
D_EXPERT = 512
EPS = 1e-6

LANES = 128
SUBLANES = 8
VMEM_LIMIT_BYTES = 56 * 1024 * 1024

CHUNK = 128
TM_PROJ = 512
TM_MIX = 256
TM_EXPERT = 256
GATE_COLS = LANES
ROUTER_COLS = LANES
PAIRS = ((0, 1), (0, 2), (0, 3), (1, 2), (1, 3), (2, 3))
N_CLASSES = N_GROUPS * len(PAIRS)
CLASS_ROWS = 32
N_EXPERT_TILES = TOKENS // TM_EXPERT + N_CLASSES
PADDED_ROWS = N_EXPERT_TILES * TM_EXPERT
SCATTER_BLOCK = 512
TM_FINAL = 512


def _params(*semantics):
    return pltpu.CompilerParams(dimension_semantics=semantics, vmem_limit_bytes=VMEM_LIMIT_BYTES)


def _dot(a, b):
    return jnp.dot(a, b, preferred_element_type=F32)


def _rms(x):
    return x * lax.rsqrt(jnp.mean(x * x, axis=-1, keepdims=True) + EPS)


def _sigmoid(x):
    return 1.0 / (1.0 + jnp.exp(-x))


ROW_TILE = D_MODEL // LANES


def _load_token_rows(ref, n):
    return jnp.concatenate([ref[pl.ds(c, n, stride=ROW_TILE), :] for c in range(ROW_TILE)], axis=1)


def _store_token_rows(ref, val, n):
    for c in range(ROW_TILE):
        ref[pl.ds(c, n, stride=ROW_TILE), :] = val[:, c * LANES:(c + 1) * LANES]


def _token_tile(ref, t):
    return ref.at[pl.ds(pl.multiple_of(t * ROW_TILE, ROW_TILE), ROW_TILE)]


def _start_row_gather(idx_ref, base, n, src_hbm, buf, sem):
    unroll = 8

    def body(r8, carry):
        for k in range(unroll):
            r = r8 * unroll + k
            pltpu.make_async_copy(_token_tile(src_hbm, idx_ref[base + r]), _token_tile(buf, r),
                                  sem).start()
        return carry

    lax.fori_loop(0, n // unroll, body, 0)


def _wait_row_gather(src_hbm, buf, sem):
    pltpu.make_async_copy(src_hbm.at[pl.ds(0, buf.shape[0])], buf, sem).wait()


def _gathered_tile(idx_ref, src_hbm, buf_ref, sem_ref, n):
    i = pl.program_id(0)
    slot = i % 2

    @pl.when(i == 0)
    def _():
        _start_row_gather(idx_ref, 0, n, src_hbm, buf_ref.at[0], sem_ref.at[0])

    @pl.when(i + 1 < pl.num_programs(0))
    def _():
        _start_row_gather(idx_ref, (i + 1) * n, n, src_hbm, buf_ref.at[1 - slot], sem_ref.at[1 - slot])

    _wait_row_gather(src_hbm, buf_ref.at[slot], sem_ref.at[slot])
    return _load_token_rows(buf_ref.at[slot], n)


def _ada_kernel(c_ref, w_ref, b_ref, o_ref):
    c = c_ref[...]
    cond = c * _sigmoid(c)
    o_ref[0] = jnp.dot(cond, w_ref[0], preferred_element_type=F32,
                       precision=lax.Precision.HIGHEST) + b_ref[0]


def _ada(c, w_ada, b_ada):
    depth, d, n = w_ada.shape
    tn = 1536
    c8 = jnp.zeros((SUBLANES, d), F32).at[:BATCH].set(c)
    out = pl.pallas_call(
        _ada_kernel,
        out_shape=jax.ShapeDtypeStruct((depth, SUBLANES, n), F32),
        grid=(depth, n // tn),
        in_specs=[
            pl.BlockSpec((SUBLANES, d), lambda l, j: (0, 0)),
            pl.BlockSpec((1, d, tn), lambda l, j: (l, 0, j)),
            pl.BlockSpec((1, 1, tn), lambda l, j: (l, 0, j)),
        ],
        out_specs=pl.BlockSpec((1, SUBLANES, tn), lambda l, j: (l, 0, j)),
        compiler_params=_params("arbitrary", "arbitrary"),
        name="ada_mod",
    )(c8, w_ada, b_ada.reshape(depth, 1, n))
    return out[:, :BATCH].reshape(depth, BATCH, 6, d)


def _mlstm_proj_kernel(x_ref, mod_ref, g_ref, w_ref, bg_ref, q_ref, k_ref, v_ref, og_ref,
                       gcol_ref, grow_ref):
    x = x_ref[...]
    m = mod_ref[0]
    u = (_rms(x) * g_ref[...] * (1.0 + m[1:2]) + m[0:1]).astype(BF16)
    q_ref[...] = (_dot(u, w_ref[:, 0:QK]) * (DH_QK ** -0.5)).astype(BF16)
    k_ref[...] = _dot(u, w_ref[:, QK:2 * QK]).astype(BF16)
    v_ref[...] = _dot(u, w_ref[:, 2 * QK:2 * QK + D_MODEL]).astype(BF16)
    og_ref[...] = _dot(u, w_ref[:, 2 * QK + D_MODEL:2 * QK + 2 * D_MODEL]).astype(BF16)
    gt = _dot(u, w_ref[:, 2 * QK + 2 * D_MODEL:]) + bg_ref[...]
    tm = gt.shape[0]
    lane = lax.broadcasted_iota(jnp.int32, gt.shape, 1)
    row = lax.broadcasted_iota(jnp.int32, gt.shape, 0)
    log_f = jnp.minimum(gt, 0.0) - jnp.log(1.0 + jnp.exp(-jnp.abs(gt)))
    z = jnp.where(lane < N_HEADS, 0.0, log_f)
    pos = row % CHUNK
    shift = 1
    while shift < CHUNK:
        z = z + jnp.where(pos >= shift, pltpu.roll(z, shift, axis=0), 0.0)
        shift *= 2
    z = jnp.where(lane < N_HEADS, gt, z)
    gcol_ref[...] = z[:, 0:SUBLANES]
    zr = jnp.where(lane < N_HEADS, z - pltpu.roll(z, LANES - N_HEADS, axis=1), z)
    grow_ref[...] = zr.T[0:SUBLANES, :]
    del tm


def _mlstm_proj(x, mod, g, w, bg):
    t, d = x.shape
    n = w.shape[1]
    tm = TM_PROJ
    tiles_per_batch = SEQ // tm
    return pl.pallas_call(
        _mlstm_proj_kernel,
        out_shape=(
            jax.ShapeDtypeStruct((t, QK), BF16),
            jax.ShapeDtypeStruct((t, QK), BF16),
            jax.ShapeDtypeStruct((t, D_MODEL), BF16),
            jax.ShapeDtypeStruct((t, D_MODEL), BF16),
            jax.ShapeDtypeStruct((t, SUBLANES), F32),
            jax.ShapeDtypeStruct((SUBLANES, t), F32),
        ),
        grid=(t // tm,),
        in_specs=[
            pl.BlockSpec((tm, d), lambda i, *_: (i, 0)),
            pl.BlockSpec((1, 6, d), lambda i, *_: (i // tiles_per_batch, 0, 0)),
            pl.BlockSpec((1, d), lambda i, *_: (0, 0)),
            pl.BlockSpec((d, n), lambda i, *_: (0, 0)),
            pl.BlockSpec((1, GATE_COLS), lambda i, *_: (0, 0)),
        ],
        out_specs=(
            pl.BlockSpec((tm, QK), lambda i, *_: (i, 0)),
            pl.BlockSpec((tm, QK), lambda i, *_: (i, 0)),
            pl.BlockSpec((tm, D_MODEL), lambda i, *_: (i, 0)),
            pl.BlockSpec((tm, D_MODEL), lambda i, *_: (i, 0)),
            pl.BlockSpec((tm, SUBLANES), lambda i, *_: (i, 0)),
            pl.BlockSpec((SUBLANES, tm), lambda i, *_: (0, i)),
        ),
        compiler_params=_params("arbitrary"),
        name="mlstm_proj",
    )(x, mod, g, w, bg)


def _mlstm_core_kernel(q_ref, k_ref, v_ref, og_ref, gcol_ref, grow_ref, ng_ref, o_ref,
                       c_ref, n_ref, m_ref):
    @pl.when(pl.program_id(1) == 0)
    def _():
        c_ref[...] = jnp.zeros_like(c_ref)
        n_ref[...] = jnp.zeros_like(n_ref)
        m_ref[...] = jnp.zeros_like(m_ref)

    ln = CHUNK
    row = lax.broadcasted_iota(jnp.int32, (ln, ln), 0)
    col = lax.broadcasted_iota(jnp.int32, (ln, ln), 1)
    causal = col <= row
    gcol = gcol_ref[...]
    grow = grow_ref[...]
    for h in range(N_HEADS):
        qh = q_ref[:, h * DH_QK:(h + 1) * DH_QK]
        kh = k_ref[:, h * DH_QK:(h + 1) * DH_QK]
        vh = v_ref[:, h * DH_V:(h + 1) * DH_V]
        ig = gcol[:, h:h + 1]
        b = gcol[:, N_HEADS + h:N_HEADS + h + 1]
        imb = grow[h:h + 1, :]
        m_prev = m_ref[h][:, 0:1]
        c_prev = c_ref[h]
        n_prev = n_ref[h]

        log_d = jnp.where(causal, b + imb, -jnp.inf)
        inter = b + m_prev
        m_t = jnp.maximum(inter, jnp.max(log_d, axis=-1, keepdims=True))
        d_mat = jnp.exp(log_d - m_t)
        w_inter = jnp.exp(inter - m_t)
        s = lax.dot_general(qh, kh, (((1,), (1,)), ((), ())), preferred_element_type=F32) * d_mat
        num = w_inter * _dot(qh, c_prev.astype(BF16)) + _dot(s.astype(BF16), vh)
        qn = jnp.sum(qh.astype(F32) * n_prev, axis=-1, keepdims=True)
        den = w_inter * qn + jnp.sum(s, axis=-1, keepdims=True)
        hh = num / jnp.maximum(jnp.abs(den), jnp.exp(-m_t))

        b_last = b[ln - 1:ln, :]
        a = b_last - b + ig
        m_new = jnp.maximum(b_last + m_prev, jnp.max(a, axis=0, keepdims=True))
        kw = (kh.astype(F32) * jnp.exp(a - m_new)).astype(BF16)
        decay = jnp.exp(b_last + m_prev - m_new)
        c_ref[h] = decay * c_prev + lax.dot_general(
            kw, vh, (((0,), (0,)), ((), ())), preferred_element_type=F32)
        n_ref[h] = decay * n_prev + jnp.sum(kw.astype(F32), axis=0, keepdims=True)
        m_ref[h] = jnp.broadcast_to(m_new, (1, LANES))

        sl = slice(h * DH_V, (h + 1) * DH_V)
        gate = _sigmoid(og_ref[:, sl].astype(F32))
        o_ref[:, sl] = (_rms(hh) * ng_ref[:, sl] * gate).astype(BF16)


def _mlstm_core(q, k, v, og, gcol, grow, norm_g):
    t = q.shape[0]
    nc = SEQ // CHUNK
    rows = lambda b, c: (b * nc + c, 0)
    return pl.pallas_call(
        _mlstm_core_kernel,
        out_shape=jax.ShapeDtypeStruct((t, D_MODEL), BF16),
        grid=(BATCH, nc),
        in_specs=[
            pl.BlockSpec((CHUNK, QK), rows),
            pl.BlockSpec((CHUNK, QK), rows),
            pl.BlockSpec((CHUNK, D_MODEL), rows),
            pl.BlockSpec((CHUNK, D_MODEL), rows),
            pl.BlockSpec((CHUNK, SUBLANES), rows),
            pl.BlockSpec((SUBLANES, CHUNK), lambda b, c: (0, b * nc + c)),
            pl.BlockSpec((1, D_MODEL), lambda b, c: (0, 0)),
        ],
        out_specs=pl.BlockSpec((CHUNK, D_MODEL), rows),
        scratch_shapes=[
            pltpu.VMEM((N_HEADS, DH_QK, DH_V), F32),
            pltpu.VMEM((N_HEADS, 1, DH_QK), F32),
            pltpu.VMEM((N_HEADS, 1, LANES), F32),
        ],
        compiler_params=_params("arbitrary", "arbitrary"),
        name="mlstm_core",
    )(q, k, v, og, gcol, grow, norm_g)


def _top2_sum(v0, v1, v2, v3):
    hi1, lo1 = jnp.maximum(v0, v1), jnp.minimum(v0, v1)
    hi2, lo2 = jnp.maximum(v2, v3), jnp.minimum(v2, v3)
    return jnp.maximum(hi1, hi2) + jnp.maximum(jnp.minimum(hi1, hi2), jnp.maximum(lo1, lo2))


def _route_tail(x_new, m, g2_ref, wr_ref, br_ref, tri_ref, u2_ref, cls_ref, rank_ref, cnt_ref,
                run_ref):
    @pl.when(pl.program_id(0) == 0)
    def _():
        run_ref[...] = jnp.zeros_like(run_ref)

    u2 = _rms(x_new) * g2_ref[...] * (1.0 + m[4:5]) + m[3:4]
    _store_token_rows(u2_ref, u2, u2.shape[0])
    logits = _dot(u2.astype(BF16), wr_ref[...])
    lt = logits.T[0:N_EXPERTS, :]
    e = jnp.exp(lt - jnp.max(lt, axis=0, keepdims=True))
    probs = e / jnp.sum(e, axis=0, keepdims=True)
    sel = probs + br_ref[...]
    sel_rows = [sel[j:j + 1, :] for j in range(N_EXPERTS)]
    best = jnp.zeros_like(sel_rows[0], dtype=jnp.int32)
    best_score = _top2_sum(*sel_rows[0:EXPERTS_PER_GROUP])
    for g in range(1, N_GROUPS):
        score = _top2_sum(*sel_rows[g * EXPERTS_PER_GROUP:(g + 1) * EXPERTS_PER_GROUP])
        better = score > best_score
        best = jnp.where(better, g, best)
        best_score = jnp.where(better, score, best_score)
    s = []
    for j in range(EXPERTS_PER_GROUP):
        sj = sel_rows[j]
        for g in range(1, N_GROUPS):
            sj = jnp.where(best == g, sel_rows[g * EXPERTS_PER_GROUP + j], sj)
        s.append(sj)
    chosen = []
    for j in range(EXPERTS_PER_GROUP):
        beaten = jnp.zeros_like(best)
        for i in range(EXPERTS_PER_GROUP):
            if i == j:
                continue
            wins = (s[i] >= s[j]) if i < j else (s[i] > s[j])
            beaten = beaten + jnp.where(wins, 1, 0)
        chosen.append(beaten < 2)
    pair = jnp.full_like(best, len(PAIRS) - 1)
    for p in range(len(PAIRS) - 2, -1, -1):
        a, b = PAIRS[p]
        pair = jnp.where(jnp.logical_and(chosen[a], chosen[b]), p, pair)
    cls = best * len(PAIRS) + pair
    cls_ref[0] = cls

    class_id = lax.broadcasted_iota(jnp.int32, (CLASS_ROWS, cls.shape[1]), 0)
    onehot = class_id == cls
    before = _dot(jnp.where(onehot, 1.0, 0.0).astype(BF16), tri_ref[...])
    run = run_ref[...]
    rank = jnp.sum(jnp.where(onehot, before + run[:, 0:1], 0.0), axis=0, keepdims=True)
    rank_ref[0] = rank.astype(jnp.int32)
    run = run + jnp.sum(jnp.where(onehot, 1.0, 0.0), axis=1, keepdims=True)
    run_ref[...] = run
    cnt_ref[...] = run.astype(jnp.int32)


def _route_out_shapes(t, tm):
    return (
        jax.ShapeDtypeStruct((t, D_MODEL), F32),
        jax.ShapeDtypeStruct((t * ROW_TILE, LANES), F32),
        jax.ShapeDtypeStruct((t // tm, 1, tm), jnp.int32),
        jax.ShapeDtypeStruct((t // tm, 1, tm), jnp.int32),
        jax.ShapeDtypeStruct((CLASS_ROWS, LANES), jnp.int32),
    )


def _route_out_specs(tm):
    return (
        pl.BlockSpec((tm, D_MODEL), lambda i, *_: (i, 0)),
        pl.BlockSpec((tm * ROW_TILE, LANES), lambda i, *_: (i, 0)),
        pl.BlockSpec((1, 1, tm), lambda i, *_: (i, 0, 0)),
        pl.BlockSpec((1, 1, tm), lambda i, *_: (i, 0, 0)),
        pl.BlockSpec((CLASS_ROWS, LANES), lambda i, *_: (0, 0)),
    )


def _const_spec(shape):
    return pl.BlockSpec(shape, lambda i, *_: (0,) * len(shape))


def _mlstm_out_kernel(h_ref, x_ref, mod_ref, wo_ref, g2_ref, wr_ref, br_ref, tri_ref,
                      x1_ref, u2_ref, cls_ref, rank_ref, cnt_ref, run_ref):
    m = mod_ref[0]
    x1 = x_ref[...] + m[2:3] * _dot(h_ref[...], wo_ref[...])
    x1_ref[...] = x1
    _route_tail(x1, m, g2_ref, wr_ref, br_ref, tri_ref, u2_ref, cls_ref, rank_ref, cnt_ref, run_ref)


def _mlstm_out(hg, x, mod, w_out, g2, wr, br, tri):
    t = x.shape[0]
    tm = TM_MIX
    tiles_per_batch = SEQ // tm
    return pl.pallas_call(
        _mlstm_out_kernel,
        out_shape=_route_out_shapes(t, tm),
        grid=(t // tm,),
        in_specs=[
            pl.BlockSpec((tm, D_MODEL), lambda i, *_: (i, 0)),
            pl.BlockSpec((tm, D_MODEL), lambda i, *_: (i, 0)),
            pl.BlockSpec((1, 6, D_MODEL), lambda i, *_: (i // tiles_per_batch, 0, 0)),
            _const_spec((D_MODEL, D_MODEL)),
            _const_spec((1, D_MODEL)),
            _const_spec((D_MODEL, ROUTER_COLS)),
            _const_spec((N_EXPERTS, 1)),
            _const_spec((tm, tm)),
        ],
        out_specs=_route_out_specs(tm),
        scratch_shapes=[pltpu.VMEM((CLASS_ROWS, LANES), F32)],
        compiler_params=_params("arbitrary"),
        name="mlstm_out_route",
    )(hg, x, mod, w_out, g2, wr, br, tri)


def _conv_layer_kernel(dest_ref, x_ref, ys_ref, mod0_ref, mod_ref, g1_ref, wi_ref, cw_ref, cb_ref,
                       wo_ref, g2_ref, wr_ref, br_ref, tri_ref,
                       x3_ref, u2_ref, cls_ref, rank_ref, cnt_ref,
                       run_ref, carry_ref, ybuf_ref, ysem_ref):
    tm = x_ref.shape[0]
    tiles_per_batch = SEQ // tm

    @pl.when(pl.program_id(0) % tiles_per_batch == 0)
    def _():
        carry_ref[...] = jnp.zeros_like(carry_ref)

    m = mod_ref[0]
    y = _gathered_tile(dest_ref, ys_ref, ybuf_ref, ysem_ref, tm)
    x2 = x_ref[...] + mod0_ref[0][5:6] * y
    u = (_rms(x2) * g1_ref[...] * (1.0 + m[1:2]) + m[0:1]).astype(BF16)
    bgate = _dot(u, wi_ref[:, 0:D_MODEL])
    z = _dot(u, wi_ref[:, D_MODEL:2 * D_MODEL]) * _dot(u, wi_ref[:, 2 * D_MODEL:])
    prev = carry_ref[...]
    row = lax.broadcasted_iota(jnp.int32, z.shape, 0)
    z1 = jnp.where(row == 0, prev[7:8], pltpu.roll(z, 1, axis=0))
    z2 = jnp.where(row == 0, prev[6:7], jnp.where(row == 1, prev[7:8], pltpu.roll(z, 2, axis=0)))
    carry_ref[...] = z[tm - SUBLANES:, :]
    cw = cw_ref[...]
    zc = cw[0:1] * z2 + cw[1:2] * z1 + cw[2:3] * z + cb_ref[...]
    x3 = x2 + m[2:3] * _dot((bgate * zc).astype(BF16), wo_ref[...])
    x3_ref[...] = x3
    _route_tail(x3, m, g2_ref, wr_ref, br_ref, tri_ref, u2_ref, cls_ref, rank_ref, cnt_ref, run_ref)


def _conv_layer(dest, x, ys, mod0, mod, g1, w_in, conv_w, conv_b, w_out, g2, wr, br, tri):
    t = x.shape[0]
    tm = TM_MIX
    tiles_per_batch = SEQ // tm
    mod_spec = pl.BlockSpec((1, 6, D_MODEL), lambda i, *_: (i // tiles_per_batch, 0, 0))
    return pl.pallas_call(
        _conv_layer_kernel,
        out_shape=_route_out_shapes(t, tm),
        grid_spec=pltpu.PrefetchScalarGridSpec(
            num_scalar_prefetch=1,
            grid=(t // tm,),
            in_specs=[
                pl.BlockSpec((tm, D_MODEL), lambda i, *_: (i, 0)),
                pl.BlockSpec(memory_space=pl.ANY),
                mod_spec,
                mod_spec,
                _const_spec((1, D_MODEL)),
                _const_spec((D_MODEL, 3 * D_MODEL)),
                _const_spec((3, D_MODEL)),
                _const_spec((1, D_MODEL)),
                _const_spec((D_MODEL, D_MODEL)),
                _const_spec((1, D_MODEL)),
                _const_spec((D_MODEL, ROUTER_COLS)),
                _const_spec((N_EXPERTS, 1)),
                _const_spec((tm, tm)),
            ],
            out_specs=_route_out_specs(tm),
            scratch_shapes=[
                pltpu.VMEM((CLASS_ROWS, LANES), F32),
                pltpu.VMEM((SUBLANES, D_MODEL), F32),
                pltpu.VMEM((2, tm * ROW_TILE, LANES), F32),
                pltpu.SemaphoreType.DMA((2,)),
            ],
        ),
        compiler_params=_params("arbitrary"),
        name="conv_layer_route",
    )(dest, x, ys, mod0, mod, g1, w_in, conv_w, conv_b, w_out, g2, wr, br, tri)


def _scatter_kernel(idx_ref, src_ref, init_ref, dst_ref, sem):
    del init_ref
    n = src_ref.shape[0] // ROW_TILE
    base = pl.program_id(0) * n
    unroll = 8

    def body(r8, carry):
        for k in range(unroll):
            r = r8 * unroll + k
            pltpu.make_async_copy(_token_tile(src_ref, r), _token_tile(dst_ref, idx_ref[base + r]),
                                  sem).start()
        return carry

    lax.fori_loop(0, n // unroll, body, 0)
    pltpu.make_async_copy(src_ref, dst_ref.at[pl.ds(0, src_ref.shape[0])], sem).wait()


def _scatter_rows(idx, src, init):
    n_rows = idx.shape[0]
    tb = SCATTER_BLOCK
    return pl.pallas_call(
        _scatter_kernel,
        out_shape=jax.ShapeDtypeStruct(init.shape, init.dtype),
        grid_spec=pltpu.PrefetchScalarGridSpec(
            num_scalar_prefetch=1,
            grid=(n_rows // tb,),
            in_specs=[pl.BlockSpec((tb * ROW_TILE, LANES), lambda i, *_: (i, 0)),
                      pl.BlockSpec(memory_space=pl.ANY)],
            out_specs=pl.BlockSpec(memory_space=pl.ANY),
            scratch_shapes=[pltpu.SemaphoreType.DMA(())]),
        input_output_aliases={2: 0},
        compiler_params=_params("arbitrary"),
        name="scatter_rows",
    )(idx, src, init)


def _expert_kernel(ea_ref, eb_ref, nused_ref, x_ref, wga_ref, wua_ref, wda_ref, wgb_ref, wub_ref,
                   wdb_ref, wra_ref, wrb_ref, y_ref):
    del ea_ref, eb_ref

    @pl.when(pl.program_id(0) < nused_ref[0])
    def _():
        tm = x_ref.shape[0] // ROW_TILE
        xb = _load_token_rows(x_ref, tm).astype(BF16)
        dl = jnp.sum(xb.astype(F32) * (wra_ref[0] - wrb_ref[0]), axis=-1, keepdims=True)
        w_a = _sigmoid(dl)
        w_b = _sigmoid(-dl)

        def ffn(wg_ref, wu_ref, wd_ref):
            gate = _dot(xb, wg_ref[0])
            hidden = gate * _sigmoid(gate) * _dot(xb, wu_ref[0])
            return _dot(hidden.astype(BF16), wd_ref[0])

        y = w_a * ffn(wga_ref, wua_ref, wda_ref) + w_b * ffn(wgb_ref, wub_ref, wdb_ref)
        _store_token_rows(y_ref, y, tm)

    @pl.when(pl.program_id(0) >= nused_ref[0])
    def _():
        y_ref[...] = jnp.zeros_like(y_ref)


def _experts(ea, eb, n_used, xs, w_gate, w_up, w_down, wr_rows):
    tm = TM_EXPERT
    row_tile = lambda j, ea, eb, nu: (jnp.minimum(j, nu[0] - 1), 0)
    sel_a = lambda j, ea, eb, nu: (ea[j], 0, 0)
    sel_b = lambda j, ea, eb, nu: (eb[j], 0, 0)
    up_spec = lambda sel: pl.BlockSpec((1, D_MODEL, D_EXPERT), sel)
    down_spec = lambda sel: pl.BlockSpec((1, D_EXPERT, D_MODEL), sel)
    wr_spec = lambda sel: pl.BlockSpec((1, 1, D_MODEL), sel)
    return pl.pallas_call(
        _expert_kernel,
        out_shape=jax.ShapeDtypeStruct((PADDED_ROWS * ROW_TILE, LANES), F32),
        grid_spec=pltpu.PrefetchScalarGridSpec(
            num_scalar_prefetch=3,
            grid=(N_EXPERT_TILES,),
            in_specs=[
                pl.BlockSpec((tm * ROW_TILE, LANES), row_tile),
                up_spec(sel_a), up_spec(sel_a), down_spec(sel_a),
                up_spec(sel_b), up_spec(sel_b), down_spec(sel_b),
                wr_spec(sel_a), wr_spec(sel_b),
            ],
            out_specs=pl.BlockSpec((tm * ROW_TILE, LANES), lambda j, ea, eb, nu: (j, 0)),
        ),
        compiler_params=_params("arbitrary"),
        name="grouped_experts",
    )(ea, eb, n_used, xs, w_gate, w_up, w_down, w_gate, w_up, w_down, wr_rows, wr_rows)


def _final_kernel(dest_ref, x_ref, ys_ref, mod_ref, g_ref, o_ref, ybuf_ref, ysem_ref):
    y = _gathered_tile(dest_ref, ys_ref, ybuf_ref, ysem_ref, x_ref.shape[0])
    x = x_ref[...] + mod_ref[0][5:6] * y
    o_ref[...] = _rms(x) * g_ref[...]


def _final(dest, x, ys, mod, g):
    t = x.shape[0]
    tm = TM_FINAL
    tiles_per_batch = SEQ // tm
    return pl.pallas_call(
        _final_kernel,
        out_shape=jax.ShapeDtypeStruct((t, D_MODEL), F32),
        grid_spec=pltpu.PrefetchScalarGridSpec(
            num_scalar_prefetch=1,
            grid=(t // tm,),
            in_specs=[
                pl.BlockSpec((tm, D_MODEL), lambda i, *_: (i, 0)),
                pl.BlockSpec(memory_space=pl.ANY),
                pl.BlockSpec((1, 6, D_MODEL), lambda i, *_: (i // tiles_per_batch, 0, 0)),
                _const_spec((1, D_MODEL)),
            ],
            out_specs=pl.BlockSpec((tm, D_MODEL), lambda i, *_: (i, 0)),
            scratch_shapes=[pltpu.VMEM((2, tm * ROW_TILE, LANES), F32), pltpu.SemaphoreType.DMA((2,))],
        ),
        compiler_params=_params("arbitrary"),
        name="final_norm",
    )(dest, x, ys, mod, g)


_PAIR_A = np.array([EXPERTS_PER_GROUP * (c // len(PAIRS)) + PAIRS[c % len(PAIRS)][0]
                    for c in range(N_CLASSES)], np.int32)
_PAIR_B = np.array([EXPERTS_PER_GROUP * (c // len(PAIRS)) + PAIRS[c % len(PAIRS)][1]
                    for c in range(N_CLASSES)], np.int32)


def _moe(u2, cls, rank, cnt, w_gate, w_up, w_down, wr_rows):
    counts = cnt[:N_CLASSES, 0]
    tiles = (counts + TM_EXPERT - 1) // TM_EXPERT
    tile_end = jnp.cumsum(tiles)
    row_start = (tile_end - tiles) * TM_EXPERT
    n_used = tile_end[-1:]
    cls = cls.reshape(-1)
    dest = row_start[cls] + rank.reshape(-1)
    tile_cls = jnp.sum(jnp.arange(N_EXPERT_TILES)[:, None] >= tile_end[None, :], axis=1)
    tile_cls = jnp.minimum(tile_cls, tile_cls[jnp.maximum(n_used[0] - 1, 0)])
    ea = jnp.asarray(_PAIR_A)[tile_cls]
    eb = jnp.asarray(_PAIR_B)[tile_cls]
    xs = _scatter_rows(dest, u2, jnp.zeros((PADDED_ROWS * ROW_TILE, LANES), F32))
    ys = _experts(ea, eb, n_used.astype(jnp.int32), xs, w_gate, w_up, w_down, wr_rows)
    return dest, ys


def kernel(x, c, norm1_g, norm2_g, w_ada, b_ada, m_w_in, m_b_gates, m_norm_g, m_w_out,
           c_w_in, c_conv_w, c_conv_b, c_w_out, w_router, b_router,
           e_w_gate, e_w_up, e_w_down, final_g):
    xf = x.reshape(TOKENS, D_MODEL)
    mod = _ada(c, w_ada, b_ada)

    w_in = jnp.concatenate(
        [m_w_in[0].astype(BF16),
         jnp.zeros((D_MODEL, GATE_COLS - 2 * N_HEADS), BF16)], axis=1)
    bg = jnp.zeros((1, GATE_COLS), F32).at[0, :2 * N_HEADS].set(m_b_gates[0])
    wr = jnp.zeros((D_MODEL, ROUTER_COLS), BF16).at[:, :N_EXPERTS].set(w_router.astype(BF16))
    wr_rows = w_router.astype(BF16).astype(F32).T.reshape(N_EXPERTS, 1, D_MODEL)
    br = b_router.reshape(N_EXPERTS, 1)
    tri = (jnp.arange(TM_MIX)[:, None] < jnp.arange(TM_MIX)[None, :]).astype(BF16)
    row = lambda v: v.reshape(1, -1)

    q, k, v, og, gcol, grow = _mlstm_proj(xf, mod[0], row(norm1_g[0]), w_in, bg)
    hg = _mlstm_core(q, k, v, og, gcol, grow, row(m_norm_g[0]))
    x1, u2, cls, rank, cnt = _mlstm_out(hg, xf, mod[0], m_w_out[0].astype(BF16), row(norm2_g[0]),
                                        wr, br, tri)
    dest, ys = _moe(u2, cls, rank, cnt, e_w_gate[0].astype(BF16), e_w_up[0].astype(BF16),
                    e_w_down[0].astype(BF16), wr_rows)

    x3, u2, cls, rank, cnt = _conv_layer(
        dest, x1, ys, mod[0], mod[1], row(norm1_g[1]), c_w_in[0].astype(BF16), c_conv_w[0],
        row(c_conv_b[0]), c_w_out[0].astype(BF16), row(norm2_g[1]), wr, br, tri)
    dest, ys = _moe(u2, cls, rank, cnt, e_w_gate[1].astype(BF16), e_w_up[1].astype(BF16),
                    e_w_down[1].astype(BF16), wr_rows)

    out = _final(dest, x3, ys, mod[1], row(final_g))
    return out.reshape(BATCH, SEQ, D_MODEL)
```

```python
import functools

import jax
import jax.numpy as jnp
import numpy as np
from jax import lax
from jax.experimental import pallas as pl
from jax.experimental.pallas import tpu as pltpu

F32 = jnp.float32
BF16 = jnp.bfloat16

D_MODEL = 1024
BATCH = 4
SEQ = 8192
TOKENS = BATCH * SEQ
N_HEADS = 4
DH_V = 256
DH_QK = 128
QK = N_HEADS * DH_QK
N_EXPERTS = 16
N_GROUPS = 4
EXPERTS_PER_GROUP = 4
D_EXPERT = 512
EPS = 1e-6

LANES = 128
SUBLANES = 8
VMEM_LIMIT_BYTES = 56 * 1024 * 1024

CHUNK = 128
TM_PROJ = 512
TM_MIX = 256
TM_EXPERT = 256
GATE_COLS = LANES
GATE_WIDTH = 16
GATE_KINDS = 3
ROUTER_COLS = LANES
PAIRS = ((0, 1), (0, 2), (0, 3), (1, 2), (1, 3), (2, 3))
N_CLASSES = N_GROUPS * len(PAIRS)
CLASS_ROWS = 32
N_EXPERT_TILES = TOKENS // TM_EXPERT + N_CLASSES
PADDED_ROWS = N_EXPERT_TILES * TM_EXPERT
SCATTER_BLOCK = 512
TM_FINAL = 512
STREAM_GROUP = 4


def _params(*semantics):
    return pltpu.CompilerParams(dimension_semantics=semantics, vmem_limit_bytes=VMEM_LIMIT_BYTES)


def _dot(a, b):
    return jnp.dot(a, b, preferred_element_type=F32)


def _rms(x):
    return x * lax.rsqrt(jnp.mean(x * x, axis=-1, keepdims=True) + EPS)


def _sigmoid(x):
    return 1.0 / (1.0 + jnp.exp(-x))


ROW_TILE = D_MODEL // LANES


def _load_token_rows(ref, n):
    return jnp.concatenate([ref[pl.ds(c, n, stride=ROW_TILE), :] for c in range(ROW_TILE)], axis=1)


def _store_token_rows(ref, val, n):
    for c in range(ROW_TILE):
        ref[pl.ds(c, n, stride=ROW_TILE), :] = val[:, c * LANES:(c + 1) * LANES]


def _token_tile(ref, t):
    return ref.at[pl.ds(pl.multiple_of(t * ROW_TILE, ROW_TILE), ROW_TILE)]


def _start_row_gather(idx_ref, base, n, src_hbm, buf, sem):
    unroll = 8

    def body(r8, carry):
        for k in range(unroll):
            r = r8 * unroll + k
            pltpu.make_async_copy(_token_tile(src_hbm, idx_ref[base + r]), _token_tile(buf, r),
                                  sem).start()
        return carry

    lax.fori_loop(0, n // unroll, body, 0)


def _wait_row_gather(src_hbm, buf, sem):
    pltpu.make_async_copy(src_hbm.at[pl.ds(0, buf.shape[0])], buf, sem).wait()


def _gathered_tile(idx_ref, src_hbm, buf_ref, sem_ref, n):
    i = pl.program_id(0)
    slot = i % 2

    @pl.when(i == 0)
    def _():
        _start_row_gather(idx_ref, 0, n, src_hbm, buf_ref.at[0], sem_ref.at[0])

    @pl.when(i + 1 < pl.num_programs(0))
    def _():
        _start_row_gather(idx_ref, (i + 1) * n, n, src_hbm, buf_ref.at[1 - slot], sem_ref.at[1 - slot])

    _wait_row_gather(src_hbm, buf_ref.at[slot], sem_ref.at[slot])
    return _load_token_rows(buf_ref.at[slot], n)


def _ada_kernel(c_ref, w_ref, b_ref, o_ref):
    c = c_ref[...]
    cond = c * _sigmoid(c)
    o_ref[0] = jnp.dot(cond, w_ref[0], preferred_element_type=F32,
                       precision=lax.Precision.HIGHEST) + b_ref[0]


def _ada(c, w_ada, b_ada):
    depth, d, n = w_ada.shape
    tn = 1536
    c8 = jnp.zeros((SUBLANES, d), F32).at[:BATCH].set(c)
    out = pl.pallas_call(
        _ada_kernel,
        out_shape=jax.ShapeDtypeStruct((depth, SUBLANES, n), F32),
        grid=(depth, n // tn),
        in_specs=[
            pl.BlockSpec((SUBLANES, d), lambda l, j: (0, 0)),
            pl.BlockSpec((1, d, tn), lambda l, j: (l, 0, j)),
            pl.BlockSpec((1, 1, tn), lambda l, j: (l, 0, j)),
        ],
        out_specs=pl.BlockSpec((1, SUBLANES, tn), lambda l, j: (l, 0, j)),
        compiler_params=_params("arbitrary", "arbitrary"),
        name="ada_mod",
    )(c8, w_ada, b_ada.reshape(depth, 1, n))
    return out[:, :BATCH].reshape(depth, BATCH, 6, d)


def _mlstm_proj_kernel(x_ref, mod_ref, g_ref, w_ref, bg_ref, q_ref, k_ref, v_ref, og_ref,
                       gcol_ref, grow_ref):
    x = x_ref[...]
    m = mod_ref[0]
    u = (_rms(x) * g_ref[...] * (1.0 + m[1:2]) + m[0:1]).astype(BF16)
    q_ref[...] = (_dot(u, w_ref[:, 0:QK]) * (DH_QK ** -0.5)).astype(BF16)
    k_ref[...] = _dot(u, w_ref[:, QK:2 * QK]).astype(BF16)
    v_ref[...] = _dot(u, w_ref[:, 2 * QK:2 * QK + D_MODEL]).astype(BF16)
    og_ref[...] = _dot(u, w_ref[:, 2 * QK + D_MODEL:2 * QK + 2 * D_MODEL]).astype(BF16)
    gt = _dot(u, w_ref[:, 2 * QK + 2 * D_MODEL:]) + bg_ref[...]
    lane = lax.broadcasted_iota(jnp.int32, gt.shape, 1)
    pos = lax.broadcasted_iota(jnp.int32, gt.shape, 0) % CHUNK
    log_f = jnp.minimum(gt, 0.0) - jnp.log(1.0 + jnp.exp(-jnp.abs(gt)))
    b = log_f
    shift = 1
    while shift < CHUNK:
        b = b + jnp.where(pos >= shift, pltpu.roll(b, shift, axis=0), 0.0)
        shift *= 2
    imb = gt - pltpu.roll(b, LANES - N_HEADS, axis=1)
    cm = imb
    shift = 1
    while shift < CHUNK:
        cm = jnp.maximum(cm, jnp.where(pos >= shift, pltpu.roll(cm, shift, axis=0), -jnp.inf))
        shift *= 2
    cols = jnp.where(lane < N_HEADS, cm,
                     jnp.where(lane < 2 * N_HEADS, b,
                               jnp.where(lane < 3 * N_HEADS, pltpu.roll(imb, 2 * N_HEADS, axis=1), 0.0)))
    gcol_ref[...] = cols[:, 0:GATE_WIDTH]
    grow_ref[0] = imb.T[0:SUBLANES, :]


def _mlstm_proj(x, mod, g, w, bg):
    t, d = x.shape
    n = w.shape[1]
    tm = TM_PROJ
    tiles_per_batch = SEQ // tm
    return pl.pallas_call(
        _mlstm_proj_kernel,
        out_shape=(
            jax.ShapeDtypeStruct((t, QK), BF16),
            jax.ShapeDtypeStruct((t, QK), BF16),
            jax.ShapeDtypeStruct((t, D_MODEL), BF16),
            jax.ShapeDtypeStruct((t, D_MODEL), BF16),
            jax.ShapeDtypeStruct((t, GATE_WIDTH), F32),
            jax.ShapeDtypeStruct((BATCH, SUBLANES, SEQ), F32),
        ),
        grid=(t // tm,),
        in_specs=[
            pl.BlockSpec((tm, d), lambda i, *_: (i, 0)),
            pl.BlockSpec((1, 6, d), lambda i, *_: (i // tiles_per_batch, 0, 0)),
            pl.BlockSpec((1, d), lambda i, *_: (0, 0)),
            pl.BlockSpec((d, n), lambda i, *_: (0, 0)),
            pl.BlockSpec((1, GATE_COLS), lambda i, *_: (0, 0)),
        ],
        out_specs=(
            pl.BlockSpec((tm, QK), lambda i, *_: (i, 0)),
            pl.BlockSpec((tm, QK), lambda i, *_: (i, 0)),
            pl.BlockSpec((tm, D_MODEL), lambda i, *_: (i, 0)),
            pl.BlockSpec((tm, D_MODEL), lambda i, *_: (i, 0)),
            pl.BlockSpec((tm, GATE_WIDTH), lambda i, *_: (i, 0)),
            pl.BlockSpec((1, SUBLANES, tm),
                         lambda i, *_: (i // tiles_per_batch, 0, i % tiles_per_batch)),
        ),
        compiler_params=_params("arbitrary"),
        name="mlstm_proj",
    )(x, mod, g, w, bg)


def _mlstm_core_kernel(q_ref, k_ref, v_ref, og_ref, gcol_ref, grow_ref, ng_ref, rep_ref, o_ref,
                       m_ref, *c_refs):
    @pl.when(pl.program_id(0) == 0)
    def _():
        m_ref[...] = jnp.zeros_like(m_ref)
        for c_ref in c_refs:
            c_ref[...] = jnp.zeros_like(c_ref)

    ln = CHUNK
    row = lax.broadcasted_iota(jnp.int32, (ln, ln), 0)
    col = lax.broadcasted_iota(jnp.int32, (ln, ln), 1)
    causal = col <= row
    ones = jnp.ones((ln, LANES), BF16)
    n_streams = BATCH * N_HEADS
    m_all = [m_ref[st] for st in range(n_streams)]
    m_out = [None] * n_streams
    for bi in range(BATCH):
        g = gcol_ref[bi]
        g_hi = g.astype(BF16)
        g_r1 = g - g_hi.astype(F32)
        g_mid = g_r1.astype(BF16)
        g_lo = (g_r1 - g_mid.astype(F32)).astype(BF16)
        rep = _dot(g_hi, rep_ref[...]) + _dot(g_mid, rep_ref[...]) + _dot(g_lo, rep_ref[...])
        for h in range(N_HEADS):
            st = bi * N_HEADS + h
            lanes = lambda kind: slice((kind * N_HEADS + h) * LANES, (kind * N_HEADS + h + 1) * LANES)
            cm, b, imb_col = rep[:, lanes(0)], rep[:, lanes(1)], rep[:, lanes(2)]
            imb_row = grow_ref[bi, h:h + 1, :]
            qh = q_ref[bi, :, h * DH_QK:(h + 1) * DH_QK]
            kh = k_ref[bi, :, h * DH_QK:(h + 1) * DH_QK]
            v_ext = jnp.concatenate([v_ref[bi, :, h * DH_V:(h + 1) * DH_V], ones], axis=1)
            c_ref = c_refs[st]
            c_prev = c_ref[...]
            m_prev = m_all[st]

            big_m = jnp.maximum(m_prev, cm)
            d_mat = jnp.exp(jnp.where(causal, imb_row - big_m, -jnp.inf))
            s = lax.dot_general(qh, kh, (((1,), (1,)), ((), ())), preferred_element_type=F32) * d_mat
            q_inter = (qh.astype(F32) * jnp.exp(m_prev - big_m)).astype(BF16)
            lhs = jnp.concatenate([q_inter, s.astype(BF16)], axis=1)
            rhs = jnp.concatenate([c_prev.astype(BF16), v_ext], axis=0)
            nd = _dot(lhs, rhs)
            den = nd[:, DH_V:]
            inv = 1.0 / jnp.maximum(jnp.abs(den), jnp.exp(-(b + big_m)))
            hh = nd[:, :DH_V] * jnp.concatenate([inv, inv], axis=1)

            m_last = big_m[ln - 1:ln, :]
            kw = (kh.astype(F32) * jnp.exp(imb_col - m_last)).astype(BF16)
            decay = jnp.exp(m_prev - m_last)
            c_ref[...] = jnp.concatenate([decay] * 3, axis=1) * c_prev + lax.dot_general(
                kw, v_ext, (((0,), (0,)), ((), ())), preferred_element_type=F32)
            m_out[st] = b[ln - 1:ln, :] + m_last

            sl = slice(h * DH_V, (h + 1) * DH_V)
            gate = _sigmoid(og_ref[bi, :, sl].astype(F32))
            o_ref[bi, :, sl] = (_rms(hh) * ng_ref[:, sl] * gate).astype(BF16)
    for st in range(n_streams):
        m_ref[st] = m_out[st]


def _mlstm_core(q, k, v, og, gcol, grow, norm_g):
    nc = SEQ // CHUNK
    n_streams = BATCH * N_HEADS
    per_batch = lambda a: a.reshape(BATCH, SEQ, a.shape[-1])
    chunk_rows = lambda width: pl.BlockSpec((BATCH, CHUNK, width), lambda c: (0, c, 0))
    n_rep = GATE_KINDS * N_HEADS
    replicate = (jnp.arange(GATE_WIDTH)[:, None] == jnp.arange(n_rep * LANES)[None, :] // LANES
                 ).astype(BF16)
    out = pl.pallas_call(
        _mlstm_core_kernel,
        out_shape=jax.ShapeDtypeStruct((BATCH, SEQ, D_MODEL), BF16),
        grid=(nc,),
        in_specs=[
            chunk_rows(QK),
            chunk_rows(QK),
            chunk_rows(D_MODEL),
            chunk_rows(D_MODEL),
            chunk_rows(GATE_WIDTH),
            pl.BlockSpec((BATCH, SUBLANES, CHUNK), lambda c: (0, 0, c)),
            pl.BlockSpec((1, D_MODEL), lambda c: (0, 0)),
            pl.BlockSpec((GATE_WIDTH, n_rep * LANES), lambda c: (0, 0)),
        ],
        out_specs=chunk_rows(D_MODEL),
        scratch_shapes=[pltpu.VMEM((n_streams, 1, LANES), F32)]
        + [pltpu.VMEM((DH_QK, DH_V + LANES), F32) for _ in range(n_streams)],
        compiler_params=_params("arbitrary"),
        name="mlstm_core",
    )(per_batch(q), per_batch(k), per_batch(v), per_batch(og), per_batch(gcol), grow, norm_g,
      replicate)
    return out.reshape(TOKENS, D_MODEL)


def _top2_sum(v0, v1, v2, v3):
    hi1, lo1 = jnp.maximum(v0, v1), jnp.minimum(v0, v1)
    hi2, lo2 = jnp.maximum(v2, v3), jnp.minimum(v2, v3)
    return jnp.maximum(hi1, hi2) + jnp.maximum(jnp.minimum(hi1, hi2), jnp.maximum(lo1, lo2))


def _route_tail(x_new, m, g2_ref, wr_ref, br_ref, tri_ref, u2_ref, cls_ref, rank_ref, cnt_ref,
                run_ref):
    @pl.when(pl.program_id(0) == 0)
    def _():
        run_ref[...] = jnp.zeros_like(run_ref)

    u2 = _rms(x_new) * g2_ref[...] * (1.0 + m[4:5]) + m[3:4]
    _store_token_rows(u2_ref, u2, u2.shape[0])
    logits = _dot(u2.astype(BF16), wr_ref[...])
    lt = logits.T[0:N_EXPERTS, :]
    e = jnp.exp(lt - jnp.max(lt, axis=0, keepdims=True))
    probs = e / jnp.sum(e, axis=0, keepdims=True)
    sel = probs + br_ref[...]
    sel_rows = [sel[j:j + 1, :] for j in range(N_EXPERTS)]
    best = jnp.zeros_like(sel_rows[0], dtype=jnp.int32)
    best_score = _top2_sum(*sel_rows[0:EXPERTS_PER_GROUP])
    for g in range(1, N_GROUPS):
        score = _top2_sum(*sel_rows[g * EXPERTS_PER_GROUP:(g + 1) * EXPERTS_PER_GROUP])
        better = score > best_score
        best = jnp.where(better, g, best)
        best_score = jnp.where(better, score, best_score)
    s = []
    for j in range(EXPERTS_PER_GROUP):
        sj = sel_rows[j]
        for g in range(1, N_GROUPS):
            sj = jnp.where(best == g, sel_rows[g * EXPERTS_PER_GROUP + j], sj)
        s.append(sj)
    chosen = []
    for j in range(EXPERTS_PER_GROUP):
        beaten = jnp.zeros_like(best)
        for i in range(EXPERTS_PER_GROUP):
            if i == j:
                continue
            wins = (s[i] >= s[j]) if i < j else (s[i] > s[j])
            beaten = beaten + jnp.where(wins, 1, 0)
        chosen.append(beaten < 2)
    pair = jnp.full_like(best, len(PAIRS) - 1)
    for p in range(len(PAIRS) - 2, -1, -1):
        a, b = PAIRS[p]
        pair = jnp.where(jnp.logical_and(chosen[a], chosen[b]), p, pair)
    cls = best * len(PAIRS) + pair
    cls_ref[0] = cls

    class_id = lax.broadcasted_iota(jnp.int32, (CLASS_ROWS, cls.shape[1]), 0)
    onehot = class_id == cls
    before = _dot(jnp.where(onehot, 1.0, 0.0).astype(BF16), tri_ref[...])
    run = run_ref[...]
    rank = jnp.sum(jnp.where(onehot, before + run[:, 0:1], 0.0), axis=0, keepdims=True)
    rank_ref[0] = rank.astype(jnp.int32)
    run = run + jnp.sum(jnp.where(onehot, 1.0, 0.0), axis=1, keepdims=True)
    run_ref[...] = run
    cnt_ref[...] = run.astype(jnp.int32)


def _route_out_shapes(t, tm):
    return (
        jax.ShapeDtypeStruct((t, D_MODEL), F32),
        jax.ShapeDtypeStruct((t * ROW_TILE, LANES), F32),
        jax.ShapeDtypeStruct((t // tm, 1, tm), jnp.int32),
        jax.ShapeDtypeStruct((t // tm, 1, tm), jnp.int32),
        jax.ShapeDtypeStruct((CLASS_ROWS, LANES), jnp.int32),
    )


def _route_out_specs(tm):
    return (
        pl.BlockSpec((tm, D_MODEL), lambda i, *_: (i, 0)),
        pl.BlockSpec((tm * ROW_TILE, LANES), lambda i, *_: (i, 0)),
        pl.BlockSpec((1, 1, tm), lambda i, *_: (i, 0, 0)),
        pl.BlockSpec((1, 1, tm), lambda i, *_: (i, 0, 0)),
        pl.BlockSpec((CLASS_ROWS, LANES), lambda i, *_: (0, 0)),
    )


def _const_spec(shape):
    return pl.BlockSpec(shape, lambda i, *_: (0,) * len(shape))


def _mlstm_out_kernel(h_ref, x_ref, mod_ref, wo_ref, g2_ref, wr_ref, br_ref, tri_ref,
                      x1_ref, u2_ref, cls_ref, rank_ref, cnt_ref, run_ref):
    m = mod_ref[0]
    x1 = x_ref[...] + m[2:3] * _dot(h_ref[...], wo_ref[...])
    x1_ref[...] = x1
    _route_tail(x1, m, g2_ref, wr_ref, br_ref, tri_ref, u2_ref, cls_ref, rank_ref, cnt_ref, run_ref)


def _mlstm_out(hg, x, mod, w_out, g2, wr, br, tri):
    t = x.shape[0]
    tm = TM_MIX
    tiles_per_batch = SEQ // tm
    return pl.pallas_call(
        _mlstm_out_kernel,
        out_shape=_route_out_shapes(t, tm),
        grid=(t // tm,),
        in_specs=[
            pl.BlockSpec((tm, D_MODEL), lambda i, *_: (i, 0)),
            pl.BlockSpec((tm, D_MODEL), lambda i, *_: (i, 0)),
            pl.BlockSpec((1, 6, D_MODEL), lambda i, *_: (i // tiles_per_batch, 0, 0)),
            _const_spec((D_MODEL, D_MODEL)),
            _const_spec((1, D_MODEL)),
            _const_spec((D_MODEL, ROUTER_COLS)),
            _const_spec((N_EXPERTS, 1)),
            _const_spec((tm, tm)),
        ],
        out_specs=_route_out_specs(tm),
        scratch_shapes=[pltpu.VMEM((CLASS_ROWS, LANES), F32)],
        compiler_params=_params("arbitrary"),
        name="mlstm_out_route",
    )(hg, x, mod, w_out, g2, wr, br, tri)


def _conv_layer_kernel(dest_ref, x_ref, ys_ref, mod0_ref, mod_ref, g1_ref, wi_ref, cw_ref, cb_ref,
                       wo_ref, g2_ref, wr_ref, br_ref, tri_ref,
                       x3_ref, u2_ref, cls_ref, rank_ref, cnt_ref,
                       run_ref, carry_ref, ybuf_ref, ysem_ref):
    tm = x_ref.shape[0]
    tiles_per_batch = SEQ // tm

    @pl.when(pl.program_id(0) % tiles_per_batch == 0)
    def _():
        carry_ref[...] = jnp.zeros_like(carry_ref)

    m = mod_ref[0]
    y = _gathered_tile(dest_ref, ys_ref, ybuf_ref, ysem_ref, tm)
    x2 = x_ref[...] + mod0_ref[0][5:6] * y
    u = (_rms(x2) * g1_ref[...] * (1.0 + m[1:2]) + m[0:1]).astype(BF16)
    bgate = _dot(u, wi_ref[:, 0:D_MODEL])
    z = _dot(u, wi_ref[:, D_MODEL:2 * D_MODEL]) * _dot(u, wi_ref[:, 2 * D_MODEL:])
    prev = carry_ref[...]
    row = lax.broadcasted_iota(jnp.int32, z.shape, 0)
    z1 = jnp.where(row == 0, prev[7:8], pltpu.roll(z, 1, axis=0))
    z2 = jnp.where(row == 0, prev[6:7], jnp.where(row == 1, prev[7:8], pltpu.roll(z, 2, axis=0)))
    carry_ref[...] = z[tm - SUBLANES:, :]
    cw = cw_ref[...]
    zc = cw[0:1] * z2 + cw[1:2] * z1 + cw[2:3] * z + cb_ref[...]
    x3 = x2 + m[2:3] * _dot((bgate * zc).astype(BF16), wo_ref[...])
    x3_ref[...] = x3
    _route_tail(x3, m, g2_ref, wr_ref, br_ref, tri_ref, u2_ref, cls_ref, rank_ref, cnt_ref, run_ref)


def _conv_layer(dest, x, ys, mod0, mod, g1, w_in, conv_w, conv_b, w_out, g2, wr, br, tri):
    t = x.shape[0]
    tm = TM_MIX
    tiles_per_batch = SEQ // tm
    mod_spec = pl.BlockSpec((1, 6, D_MODEL), lambda i, *_: (i // tiles_per_batch, 0, 0))
    return pl.pallas_call(
        _conv_layer_kernel,
        out_shape=_route_out_shapes(t, tm),
        grid_spec=pltpu.PrefetchScalarGridSpec(
            num_scalar_prefetch=1,
            grid=(t // tm,),
            in_specs=[
                pl.BlockSpec((tm, D_MODEL), lambda i, *_: (i, 0)),
                pl.BlockSpec(memory_space=pl.ANY),
                mod_spec,
                mod_spec,
                _const_spec((1, D_MODEL)),
                _const_spec((D_MODEL, 3 * D_MODEL)),
                _const_spec((3, D_MODEL)),
                _const_spec((1, D_MODEL)),
                _const_spec((D_MODEL, D_MODEL)),
                _const_spec((1, D_MODEL)),
                _const_spec((D_MODEL, ROUTER_COLS)),
                _const_spec((N_EXPERTS, 1)),
                _const_spec((tm, tm)),
            ],
            out_specs=_route_out_specs(tm),
            scratch_shapes=[
                pltpu.VMEM((CLASS_ROWS, LANES), F32),
                pltpu.VMEM((SUBLANES, D_MODEL), F32),
                pltpu.VMEM((2, tm * ROW_TILE, LANES), F32),
                pltpu.SemaphoreType.DMA((2,)),
            ],
        ),
        compiler_params=_params("arbitrary"),
        name="conv_layer_route",
    )(dest, x, ys, mod0, mod, g1, w_in, conv_w, conv_b, w_out, g2, wr, br, tri)


def _scatter_kernel(idx_ref, src_ref, init_ref, dst_ref, sem):
    del init_ref
    n = src_ref.shape[0] // ROW_TILE
    base = pl.program_id(0) * n
    unroll = 8

    def body(r8, carry):
        for k in range(unroll):
            r = r8 * unroll + k
            pltpu.make_async_copy(_token_tile(src_ref, r), _token_tile(dst_ref, idx_ref[base + r]),
                                  sem).start()
        return carry

    lax.fori_loop(0, n // unroll, body, 0)
    pltpu.make_async_copy(src_ref, dst_ref.at[pl.ds(0, src_ref.shape[0])], sem).wait()


def _scatter_rows(idx, src, init):
    n_rows = idx.shape[0]
    tb = SCATTER_BLOCK
    return pl.pallas_call(
        _scatter_kernel,
        out_shape=jax.ShapeDtypeStruct(init.shape, init.dtype),
        grid_spec=pltpu.PrefetchScalarGridSpec(
            num_scalar_prefetch=1,
            grid=(n_rows // tb,),
            in_specs=[pl.BlockSpec((tb * ROW_TILE, LANES), lambda i, *_: (i, 0)),
                      pl.BlockSpec(memory_space=pl.ANY)],
            out_specs=pl.BlockSpec(memory_space=pl.ANY),
            scratch_shapes=[pltpu.SemaphoreType.DMA(())]),
        input_output_aliases={2: 0},
        compiler_params=_params("arbitrary"),
        name="scatter_rows",
    )(idx, src, init)


def _expert_kernel(ea_ref, eb_ref, nused_ref, x_ref, wga_ref, wua_ref, wda_ref, wgb_ref, wub_ref,
                   wdb_ref, wra_ref, wrb_ref, y_ref):
    del ea_ref, eb_ref

    @pl.when(pl.program_id(0) < nused_ref[0])
    def _():
        tm = x_ref.shape[0] // ROW_TILE
        xb = _load_token_rows(x_ref, tm).astype(BF16)
        dl = jnp.sum(xb.astype(F32) * (wra_ref[0] - wrb_ref[0]), axis=-1, keepdims=True)
        w_a = _sigmoid(dl)
        w_b = _sigmoid(-dl)

        def ffn(wg_ref, wu_ref, wd_ref):
            gate = _dot(xb, wg_ref[0])
            hidden = gate * _sigmoid(gate) * _dot(xb, wu_ref[0])
            return _dot(hidden.astype(BF16), wd_ref[0])

        y = w_a * ffn(wga_ref, wua_ref, wda_ref) + w_b * ffn(wgb_ref, wub_ref, wdb_ref)
        _store_token_rows(y_ref, y, tm)

    @pl.when(pl.program_id(0) >= nused_ref[0])
    def _():
        y_ref[...] = jnp.zeros_like(y_ref)


def _experts(ea, eb, n_used, xs, w_gate, w_up, w_down, wr_rows):
    tm = TM_EXPERT
    row_tile = lambda j, ea, eb, nu: (jnp.minimum(j, nu[0] - 1), 0)
    sel_a = lambda j, ea, eb, nu: (ea[j], 0, 0)
    sel_b = lambda j, ea, eb, nu: (eb[j], 0, 0)
    up_spec = lambda sel: pl.BlockSpec((1, D_MODEL, D_EXPERT), sel)
    down_spec = lambda sel: pl.BlockSpec((1, D_EXPERT, D_MODEL), sel)
    wr_spec = lambda sel: pl.BlockSpec((1, 1, D_MODEL), sel)
    return pl.pallas_call(
        _expert_kernel,
        out_shape=jax.ShapeDtypeStruct((PADDED_ROWS * ROW_TILE, LANES), F32),
        grid_spec=pltpu.PrefetchScalarGridSpec(
            num_scalar_prefetch=3,
            grid=(N_EXPERT_TILES,),
            in_specs=[
                pl.BlockSpec((tm * ROW_TILE, LANES), row_tile),
                up_spec(sel_a), up_spec(sel_a), down_spec(sel_a),
                up_spec(sel_b), up_spec(sel_b), down_spec(sel_b),
                wr_spec(sel_a), wr_spec(sel_b),
            ],
            out_specs=pl.BlockSpec((tm * ROW_TILE, LANES), lambda j, ea, eb, nu: (j, 0)),
        ),
        compiler_params=_params("arbitrary"),
        name="grouped_experts",
    )(ea, eb, n_used, xs, w_gate, w_up, w_down, w_gate, w_up, w_down, wr_rows, wr_rows)


def _final_kernel(dest_ref, x_ref, ys_ref, mod_ref, g_ref, o_ref, ybuf_ref, ysem_ref):
    y = _gathered_tile(dest_ref, ys_ref, ybuf_ref, ysem_ref, x_ref.shape[0])
    x = x_ref[...] + mod_ref[0][5:6] * y
    o_ref[...] = _rms(x) * g_ref[...]


def _final(dest, x, ys, mod, g):
    t = x.shape[0]
    tm = TM_FINAL
    tiles_per_batch = SEQ // tm
    return pl.pallas_call(
        _final_kernel,
        out_shape=jax.ShapeDtypeStruct((t, D_MODEL), F32),
        grid_spec=pltpu.PrefetchScalarGridSpec(
            num_scalar_prefetch=1,
            grid=(t // tm,),
            in_specs=[
                pl.BlockSpec((tm, D_MODEL), lambda i, *_: (i, 0)),
                pl.BlockSpec(memory_space=pl.ANY),
                pl.BlockSpec((1, 6, D_MODEL), lambda i, *_: (i // tiles_per_batch, 0, 0)),
                _const_spec((1, D_MODEL)),
            ],
            out_specs=pl.BlockSpec((tm, D_MODEL), lambda i, *_: (i, 0)),
            scratch_shapes=[pltpu.VMEM((2, tm * ROW_TILE, LANES), F32), pltpu.SemaphoreType.DMA((2,))],
        ),
        compiler_params=_params("arbitrary"),
        name="final_norm",
    )(dest, x, ys, mod, g)


_PAIR_A = np.array([EXPERTS_PER_GROUP * (c // len(PAIRS)) + PAIRS[c % len(PAIRS)][0]
                    for c in range(N_CLASSES)], np.int32)
_PAIR_B = np.array([EXPERTS_PER_GROUP * (c // len(PAIRS)) + PAIRS[c % len(PAIRS)][1]
                    for c in range(N_CLASSES)], np.int32)


def _moe(u2, cls, rank, cnt, w_gate, w_up, w_down, wr_rows):
    counts = cnt[:N_CLASSES, 0]
    tiles = (counts + TM_EXPERT - 1) // TM_EXPERT
    tile_end = jnp.cumsum(tiles)
    row_start = (tile_end - tiles) * TM_EXPERT
    n_used = tile_end[-1:]
    cls = cls.reshape(-1)
    dest = row_start[cls] + rank.reshape(-1)
    tile_cls = jnp.sum(jnp.arange(N_EXPERT_TILES)[:, None] >= tile_end[None, :], axis=1)
    tile_cls = jnp.minimum(tile_cls, tile_cls[jnp.maximum(n_used[0] - 1, 0)])
    ea = jnp.asarray(_PAIR_A)[tile_cls]
    eb = jnp.asarray(_PAIR_B)[tile_cls]
    xs = _scatter_rows(dest, u2, jnp.zeros((PADDED_ROWS * ROW_TILE, LANES), F32))
    ys = _experts(ea, eb, n_used.astype(jnp.int32), xs, w_gate, w_up, w_down, wr_rows)
    return dest, ys


def kernel(x, c, norm1_g, norm2_g, w_ada, b_ada, m_w_in, m_b_gates, m_norm_g, m_w_out,
           c_w_in, c_conv_w, c_conv_b, c_w_out, w_router, b_router,
           e_w_gate, e_w_up, e_w_down, final_g):
    xf = x.reshape(TOKENS, D_MODEL)
    mod = _ada(c, w_ada, b_ada)

    w_in = jnp.concatenate(
        [m_w_in[0].astype(BF16),
         jnp.zeros((D_MODEL, GATE_COLS - 2 * N_HEADS), BF16)], axis=1)
    bg = jnp.zeros((1, GATE_COLS), F32).at[0, :2 * N_HEADS].set(m_b_gates[0])
    wr = jnp.zeros((D_MODEL, ROUTER_COLS), BF16).at[:, :N_EXPERTS].set(w_router.astype(BF16))
    wr_rows = w_router.astype(BF16).astype(F32).T.reshape(N_EXPERTS, 1, D_MODEL)
    br = b_router.reshape(N_EXPERTS, 1)
    tri = (jnp.arange(TM_MIX)[:, None] < jnp.arange(TM_MIX)[None, :]).astype(BF16)
    row = lambda v: v.reshape(1, -1)

    q, k, v, og, gcol, grow = _mlstm_proj(xf, mod[0], row(norm1_g[0]), w_in, bg)
    hg = _mlstm_core(q, k, v, og, gcol, grow, row(m_norm_g[0]))
    x1, u2, cls, rank, cnt = _mlstm_out(hg, xf, mod[0], m_w_out[0].astype(BF16), row(norm2_g[0]),
                                        wr, br, tri)
    dest, ys = _moe(u2, cls, rank, cnt, e_w_gate[0].astype(BF16), e_w_up[0].astype(BF16),
                    e_w_down[0].astype(BF16), wr_rows)

    x3, u2, cls, rank, cnt = _conv_layer(
        dest, x1, ys, mod[0], mod[1], row(norm1_g[1]), c_w_in[0].astype(BF16), c_conv_w[0],
        row(c_conv_b[0]), c_w_out[0].astype(BF16), row(norm2_g[1]), wr, br, tri)
    dest, ys = _moe(u2, cls, rank, cnt, e_w_gate[1].astype(BF16), e_w_up[1].astype(BF16),
                    e_w_down[1].astype(BF16), wr_rows)

    out = _final(dest, x3, ys, mod[1], row(final_g))
    return out.reshape(BATCH, SEQ, D_MODEL)
```

```python
import functools

import jax
import jax.numpy as jnp
import numpy as np
from jax import lax
from jax.experimental import pallas as pl
from jax.experimental.pallas import tpu as pltpu

F32 = jnp.float32
BF16 = jnp.bfloat16

D_MODEL = 1024
BATCH = 4
SEQ = 8192
TOKENS = BATCH * SEQ
N_HEADS = 4
DH_V = 256
DH_QK = 128
QK = N_HEADS * DH_QK
N_EXPERTS = 16
N_GROUPS = 4
EXPERTS_PER_GROUP = 4
D_EXPERT = 512
EPS = 1e-6

LANES = 128
SUBLANES = 8
VMEM_LIMIT_BYTES = 56 * 1024 * 1024

CHUNK = 128
TM_PROJ = 512
TM_MIX = 512
TM_EXPERT = 256
GATE_COLS = LANES
GATE_WIDTH = 16
GATE_KINDS = 3
ROUTER_COLS = LANES
PAIRS = ((0, 1), (0, 2), (0, 3), (1, 2), (1, 3), (2, 3))
N_CLASSES = N_GROUPS * len(PAIRS)
CLASS_ROWS = 32
N_EXPERT_TILES = TOKENS // TM_EXPERT + N_CLASSES
PADDED_ROWS = N_EXPERT_TILES * TM_EXPERT
SCATTER_BLOCK = 512
TM_FINAL = 512
STREAM_GROUP = 4


def _params(*semantics):
    return pltpu.CompilerParams(dimension_semantics=semantics, vmem_limit_bytes=VMEM_LIMIT_BYTES)


def _dot(a, b):
    return jnp.dot(a, b, preferred_element_type=F32)


def _rms(x):
    return x * lax.rsqrt(jnp.mean(x * x, axis=-1, keepdims=True) + EPS)


def _sigmoid(x):
    return 1.0 / (1.0 + jnp.exp(-x))


ROW_TILE = D_MODEL // LANES


def _load_token_rows(ref, n):
    return jnp.concatenate([ref[pl.ds(c, n, stride=ROW_TILE), :] for c in range(ROW_TILE)], axis=1)


def _store_token_rows(ref, val, n):
    for c in range(ROW_TILE):
        ref[pl.ds(c, n, stride=ROW_TILE), :] = val[:, c * LANES:(c + 1) * LANES]


def _token_tile(ref, t):
    return ref.at[pl.ds(pl.multiple_of(t * ROW_TILE, ROW_TILE), ROW_TILE)]


def _start_row_gather(idx_ref, base, n, src_hbm, buf, sem):
    for r in range(n):
        pltpu.make_async_copy(_token_tile(src_hbm, idx_ref[base + r]), _token_tile(buf, r), sem).start()


def _wait_row_gather(src_hbm, buf, sem):
    pltpu.make_async_copy(src_hbm.at[pl.ds(0, buf.shape[0])], buf, sem).wait()


def _gathered_tile(idx_ref, src_hbm, buf_ref, sem_ref, n):
    i = pl.program_id(0)
    last = pl.num_programs(0) - 1
    slot = i % 2

    @pl.when(i == 0)
    def _():
        _start_row_gather(idx_ref, 0, n, src_hbm, buf_ref.at[0], sem_ref.at[0])

    _wait_row_gather(src_hbm, buf_ref.at[slot], sem_ref.at[slot])
    rows = _load_token_rows(buf_ref.at[slot], n)
    _start_row_gather(idx_ref, jnp.minimum(i + 1, last) * n, n, src_hbm, buf_ref.at[1 - slot],
                      sem_ref.at[1 - slot])
    return rows


def _drain_row_gather(src_hbm, buf_ref, sem_ref):
    i = pl.program_id(0)

    @pl.when(i == pl.num_programs(0) - 1)
    def _():
        _wait_row_gather(src_hbm, buf_ref.at[1 - i % 2], sem_ref.at[1 - i % 2])


def _ada_kernel(c_ref, w_ref, b_ref, o_ref):
    c = c_ref[...]
    cond = c * _sigmoid(c)
    o_ref[0] = jnp.dot(cond, w_ref[0], preferred_element_type=F32,
                       precision=lax.Precision.HIGHEST) + b_ref[0]


def _ada(c, w_ada, b_ada):
    depth, d, n = w_ada.shape
    tn = 1536
    c8 = jnp.zeros((SUBLANES, d), F32).at[:BATCH].set(c)
    out = pl.pallas_call(
        _ada_kernel,
        out_shape=jax.ShapeDtypeStruct((depth, SUBLANES, n), F32),
        grid=(depth, n // tn),
        in_specs=[
            pl.BlockSpec((SUBLANES, d), lambda l, j: (0, 0)),
            pl.BlockSpec((1, d, tn), lambda l, j: (l, 0, j)),
            pl.BlockSpec((1, 1, tn), lambda l, j: (l, 0, j)),
        ],
        out_specs=pl.BlockSpec((1, SUBLANES, tn), lambda l, j: (l, 0, j)),
        compiler_params=_params("arbitrary", "arbitrary"),
        name="ada_mod",
    )(c8, w_ada, b_ada.reshape(depth, 1, n))
    return out[:, :BATCH].reshape(depth, BATCH, 6, d)


def _mlstm_proj_kernel(x_ref, mod_ref, g_ref, w_ref, bg_ref, q_ref, k_ref, v_ref, og_ref,
                       gcol_ref, grow_ref):
    x = x_ref[...]
    m = mod_ref[0]
    u = (_rms(x) * g_ref[...] * (1.0 + m[1:2]) + m[0:1]).astype(BF16)
    q_ref[...] = (_dot(u, w_ref[:, 0:QK]) * (DH_QK ** -0.5)).astype(BF16)
    k_ref[...] = _dot(u, w_ref[:, QK:2 * QK]).astype(BF16)
    v_ref[...] = _dot(u, w_ref[:, 2 * QK:2 * QK + D_MODEL]).astype(BF16)
    og_ref[...] = _dot(u, w_ref[:, 2 * QK + D_MODEL:2 * QK + 2 * D_MODEL]).astype(BF16)
    gt = _dot(u, w_ref[:, 2 * QK + 2 * D_MODEL:]) + bg_ref[...]
    lane = lax.broadcasted_iota(jnp.int32, gt.shape, 1)
    pos = lax.broadcasted_iota(jnp.int32, gt.shape, 0) % CHUNK
    log_f = jnp.minimum(gt, 0.0) - jnp.log(1.0 + jnp.exp(-jnp.abs(gt)))
    b = log_f
    shift = 1
    while shift < CHUNK:
        b = b + jnp.where(pos >= shift, pltpu.roll(b, shift, axis=0), 0.0)
        shift *= 2
    imb = gt - pltpu.roll(b, LANES - N_HEADS, axis=1)
    cm = imb
    shift = 1
    while shift < CHUNK:
        cm = jnp.maximum(cm, jnp.where(pos >= shift, pltpu.roll(cm, shift, axis=0), -jnp.inf))
        shift *= 2
    cols = jnp.where(lane < N_HEADS, cm,
                     jnp.where(lane < 2 * N_HEADS, b,
                               jnp.where(lane < 3 * N_HEADS, pltpu.roll(imb, 2 * N_HEADS, axis=1), 0.0)))
    gcol_ref[...] = cols[:, 0:GATE_WIDTH]
    grow_ref[0] = imb.T[0:SUBLANES, :]


def _mlstm_proj(x, mod, g, w, bg):
    t, d = x.shape
    n = w.shape[1]
    tm = TM_PROJ
    tiles_per_batch = SEQ // tm
    return pl.pallas_call(
        _mlstm_proj_kernel,
        out_shape=(
            jax.ShapeDtypeStruct((t, QK), BF16),
            jax.ShapeDtypeStruct((t, QK), BF16),
            jax.ShapeDtypeStruct((t, D_MODEL), BF16),
            jax.ShapeDtypeStruct((t, D_MODEL), BF16),
            jax.ShapeDtypeStruct((t, GATE_WIDTH), F32),
            jax.ShapeDtypeStruct((BATCH, SUBLANES, SEQ), F32),
        ),
        grid=(t // tm,),
        in_specs=[
            pl.BlockSpec((tm, d), lambda i, *_: (i, 0)),
            pl.BlockSpec((1, 6, d), lambda i, *_: (i // tiles_per_batch, 0, 0)),
            pl.BlockSpec((1, d), lambda i, *_: (0, 0)),
            pl.BlockSpec((d, n), lambda i, *_: (0, 0)),
            pl.BlockSpec((1, GATE_COLS), lambda i, *_: (0, 0)),
        ],
        out_specs=(
            pl.BlockSpec((tm, QK), lambda i, *_: (i, 0)),
            pl.BlockSpec((tm, QK), lambda i, *_: (i, 0)),
            pl.BlockSpec((tm, D_MODEL), lambda i, *_: (i, 0)),
            pl.BlockSpec((tm, D_MODEL), lambda i, *_: (i, 0)),
            pl.BlockSpec((tm, GATE_WIDTH), lambda i, *_: (i, 0)),
            pl.BlockSpec((1, SUBLANES, tm),
                         lambda i, *_: (i // tiles_per_batch, 0, i % tiles_per_batch)),
        ),
        compiler_params=_params("arbitrary"),
        name="mlstm_proj",
    )(x, mod, g, w, bg)


def _mlstm_core_kernel(q_ref, k_ref, v_ref, og_ref, gcol_ref, grow_ref, ng_ref, rep_ref, o_ref,
                       m_ref, *c_refs):
    @pl.when(pl.program_id(0) == 0)
    def _():
        m_ref[...] = jnp.zeros_like(m_ref)
        for c_ref in c_refs:
            c_ref[...] = jnp.zeros_like(c_ref)

    ln = CHUNK
    row = lax.broadcasted_iota(jnp.int32, (ln, ln), 0)
    col = lax.broadcasted_iota(jnp.int32, (ln, ln), 1)
    causal = col <= row
    ones = jnp.ones((ln, LANES), BF16)
    n_streams = BATCH * N_HEADS
    m_all = [m_ref[st] for st in range(n_streams)]
    m_out = [None] * n_streams
    for bi in range(BATCH):
        g = gcol_ref[bi]
        g_hi = g.astype(BF16)
        g_r1 = g - g_hi.astype(F32)
        g_mid = g_r1.astype(BF16)
        g_lo = (g_r1 - g_mid.astype(F32)).astype(BF16)
        rep = _dot(g_hi, rep_ref[...]) + _dot(g_mid, rep_ref[...]) + _dot(g_lo, rep_ref[...])
        for h in range(N_HEADS):
            st = bi * N_HEADS + h
            lanes = lambda kind: slice((kind * N_HEADS + h) * LANES, (kind * N_HEADS + h + 1) * LANES)
            cm, b, imb_col = rep[:, lanes(0)], rep[:, lanes(1)], rep[:, lanes(2)]
            imb_row = grow_ref[bi, h:h + 1, :]
            qh = q_ref[bi, :, h * DH_QK:(h + 1) * DH_QK]
            kh = k_ref[bi, :, h * DH_QK:(h + 1) * DH_QK]
            v_ext = jnp.concatenate([v_ref[bi, :, h * DH_V:(h + 1) * DH_V], ones], axis=1)
            c_ref = c_refs[st]
            c_prev = c_ref[...]
            m_prev = m_all[st]

            big_m = jnp.maximum(m_prev, cm)
            d_mat = jnp.exp(jnp.where(causal, imb_row - big_m, -jnp.inf))
            s = lax.dot_general(qh, kh, (((1,), (1,)), ((), ())), preferred_element_type=F32) * d_mat
            q_inter = (qh.astype(F32) * jnp.exp(m_prev - big_m)).astype(BF16)
            lhs = jnp.concatenate([q_inter, s.astype(BF16)], axis=1)
            rhs = jnp.concatenate([c_prev.astype(BF16), v_ext], axis=0)
            nd = _dot(lhs, rhs)
            den = nd[:, DH_V:]
            inv = 1.0 / jnp.maximum(jnp.abs(den), jnp.exp(-(b + big_m)))
            hh = nd[:, :DH_V] * jnp.concatenate([inv, inv], axis=1)

            m_last = big_m[ln - 1:ln, :]
            kw = (kh.astype(F32) * jnp.exp(imb_col - m_last)).astype(BF16)
            decay = jnp.exp(m_prev - m_last)
            c_ref[...] = jnp.concatenate([decay] * 3, axis=1) * c_prev + lax.dot_general(
                kw, v_ext, (((0,), (0,)), ((), ())), preferred_element_type=F32)
            m_out[st] = b[ln - 1:ln, :] + m_last

            sl = slice(h * DH_V, (h + 1) * DH_V)
            gate = _sigmoid(og_ref[bi, :, sl].astype(F32))
            o_ref[bi, :, sl] = (_rms(hh) * ng_ref[:, sl] * gate).astype(BF16)
    for st in range(n_streams):
        m_ref[st] = m_out[st]


def _mlstm_core(q, k, v, og, gcol, grow, norm_g):
    nc = SEQ // CHUNK
    n_streams = BATCH * N_HEADS
    per_batch = lambda a: a.reshape(BATCH, SEQ, a.shape[-1])
    chunk_rows = lambda width: pl.BlockSpec((BATCH, CHUNK, width), lambda c: (0, c, 0))
    n_rep = GATE_KINDS * N_HEADS
    replicate = (jnp.arange(GATE_WIDTH)[:, None] == jnp.arange(n_rep * LANES)[None, :] // LANES
                 ).astype(BF16)
    out = pl.pallas_call(
        _mlstm_core_kernel,
        out_shape=jax.ShapeDtypeStruct((BATCH, SEQ, D_MODEL), BF16),
        grid=(nc,),
        in_specs=[
            chunk_rows(QK),
            chunk_rows(QK),
            chunk_rows(D_MODEL),
            chunk_rows(D_MODEL),
            chunk_rows(GATE_WIDTH),
            pl.BlockSpec((BATCH, SUBLANES, CHUNK), lambda c: (0, 0, c)),
            pl.BlockSpec((1, D_MODEL), lambda c: (0, 0)),
            pl.BlockSpec((GATE_WIDTH, n_rep * LANES), lambda c: (0, 0)),
        ],
        out_specs=chunk_rows(D_MODEL),
        scratch_shapes=[pltpu.VMEM((n_streams, 1, LANES), F32)]
        + [pltpu.VMEM((DH_QK, DH_V + LANES), F32) for _ in range(n_streams)],
        compiler_params=_params("arbitrary"),
        name="mlstm_core",
    )(per_batch(q), per_batch(k), per_batch(v), per_batch(og), per_batch(gcol), grow, norm_g,
      replicate)
    return out.reshape(TOKENS, D_MODEL)


def _top2_sum(v0, v1, v2, v3):
    hi1, lo1 = jnp.maximum(v0, v1), jnp.minimum(v0, v1)
    hi2, lo2 = jnp.maximum(v2, v3), jnp.minimum(v2, v3)
    return jnp.maximum(hi1, hi2) + jnp.maximum(jnp.minimum(hi1, hi2), jnp.maximum(lo1, lo2))


def _route_tail(x_new, m, g2_ref, wr_ref, br_ref, tri_ref, u2_ref, cls_ref, rank_ref, cnt_ref,
                run_ref):
    @pl.when(pl.program_id(0) == 0)
    def _():
        run_ref[...] = jnp.zeros_like(run_ref)

    u2 = _rms(x_new) * g2_ref[...] * (1.0 + m[4:5]) + m[3:4]
    _store_token_rows(u2_ref, u2, u2.shape[0])
    logits = _dot(u2.astype(BF16), wr_ref[...])
    lt = logits.T[0:N_EXPERTS, :]
    e = jnp.exp(lt - jnp.max(lt, axis=0, keepdims=True))
    probs = e / jnp.sum(e, axis=0, keepdims=True)
    sel = probs + br_ref[...]
    sel_rows = [sel[j:j + 1, :] for j in range(N_EXPERTS)]
    best = jnp.zeros_like(sel_rows[0], dtype=jnp.int32)
    best_score = _top2_sum(*sel_rows[0:EXPERTS_PER_GROUP])
    for g in range(1, N_GROUPS):
        score = _top2_sum(*sel_rows[g * EXPERTS_PER_GROUP:(g + 1) * EXPERTS_PER_GROUP])
        better = score > best_score
        best = jnp.where(better, g, best)
        best_score = jnp.where(better, score, best_score)
    s = []
    for j in range(EXPERTS_PER_GROUP):
        sj = sel_rows[j]
        for g in range(1, N_GROUPS):
            sj = jnp.where(best == g, sel_rows[g * EXPERTS_PER_GROUP + j], sj)
        s.append(sj)
    chosen = []
    for j in range(EXPERTS_PER_GROUP):
        beaten = jnp.zeros_like(best)
        for i in range(EXPERTS_PER_GROUP):
            if i == j:
                continue
            wins = (s[i] >= s[j]) if i < j else (s[i] > s[j])
            beaten = beaten + jnp.where(wins, 1, 0)
        chosen.append(beaten < 2)
    pair = jnp.full_like(best, len(PAIRS) - 1)
    for p in range(len(PAIRS) - 2, -1, -1):
        a, b = PAIRS[p]
        pair = jnp.where(jnp.logical_and(chosen[a], chosen[b]), p, pair)
    cls = best * len(PAIRS) + pair
    cls_ref[0] = cls

    class_id = lax.broadcasted_iota(jnp.int32, (CLASS_ROWS, cls.shape[1]), 0)
    onehot = class_id == cls
    before = _dot(jnp.where(onehot, 1.0, 0.0).astype(BF16), tri_ref[...])
    run = run_ref[...]
    rank = jnp.sum(jnp.where(onehot, before + run[:, 0:1], 0.0), axis=0, keepdims=True)
    rank_ref[0] = rank.astype(jnp.int32)
    run = run + jnp.sum(jnp.where(onehot, 1.0, 0.0), axis=1, keepdims=True)
    run_ref[...] = run
    cnt_ref[...] = run.astype(jnp.int32)


def _route_out_shapes(t, tm):
    return (
        jax.ShapeDtypeStruct((t, D_MODEL), F32),
        jax.ShapeDtypeStruct((t * ROW_TILE, LANES), F32),
        jax.ShapeDtypeStruct((t // tm, 1, tm), jnp.int32),
        jax.ShapeDtypeStruct((t // tm, 1, tm), jnp.int32),
        jax.ShapeDtypeStruct((CLASS_ROWS, LANES), jnp.int32),
    )


def _route_out_specs(tm):
    return (
        pl.BlockSpec((tm, D_MODEL), lambda i, *_: (i, 0)),
        pl.BlockSpec((tm * ROW_TILE, LANES), lambda i, *_: (i, 0)),
        pl.BlockSpec((1, 1, tm), lambda i, *_: (i, 0, 0)),
        pl.BlockSpec((1, 1, tm), lambda i, *_: (i, 0, 0)),
        pl.BlockSpec((CLASS_ROWS, LANES), lambda i, *_: (0, 0)),
    )


def _const_spec(shape):
    return pl.BlockSpec(shape, lambda i, *_: (0,) * len(shape))


def _mlstm_out_kernel(h_ref, x_ref, mod_ref, wo_ref, g2_ref, wr_ref, br_ref, tri_ref,
                      x1_ref, u2_ref, cls_ref, rank_ref, cnt_ref, run_ref):
    m = mod_ref[0]
    x1 = x_ref[...] + m[2:3] * _dot(h_ref[...], wo_ref[...])
    x1_ref[...] = x1
    _route_tail(x1, m, g2_ref, wr_ref, br_ref, tri_ref, u2_ref, cls_ref, rank_ref, cnt_ref, run_ref)


def _mlstm_out(hg, x, mod, w_out, g2, wr, br, tri):
    t = x.shape[0]
    tm = TM_MIX
    tiles_per_batch = SEQ // tm
    return pl.pallas_call(
        _mlstm_out_kernel,
        out_shape=_route_out_shapes(t, tm),
        grid=(t // tm,),
        in_specs=[
            pl.BlockSpec((tm, D_MODEL), lambda i, *_: (i, 0)),
            pl.BlockSpec((tm, D_MODEL), lambda i, *_: (i, 0)),
            pl.BlockSpec((1, 6, D_MODEL), lambda i, *_: (i // tiles_per_batch, 0, 0)),
            _const_spec((D_MODEL, D_MODEL)),
            _const_spec((1, D_MODEL)),
            _const_spec((D_MODEL, ROUTER_COLS)),
            _const_spec((N_EXPERTS, 1)),
            _const_spec((tm, tm)),
        ],
        out_specs=_route_out_specs(tm),
        scratch_shapes=[pltpu.VMEM((CLASS_ROWS, LANES), F32)],
        compiler_params=_params("arbitrary"),
        name="mlstm_out_route",
    )(hg, x, mod, w_out, g2, wr, br, tri)


def _conv_layer_kernel(dest_ref, x_ref, ys_ref, mod0_ref, mod_ref, g1_ref, wi_ref, cw_ref, cb_ref,
                       wo_ref, g2_ref, wr_ref, br_ref, tri_ref,
                       x3_ref, u2_ref, cls_ref, rank_ref, cnt_ref,
                       run_ref, carry_ref, ybuf_ref, ysem_ref):
    tm = x_ref.shape[0]
    tiles_per_batch = SEQ // tm

    @pl.when(pl.program_id(0) % tiles_per_batch == 0)
    def _():
        carry_ref[...] = jnp.zeros_like(carry_ref)

    m = mod_ref[0]
    y = _gathered_tile(dest_ref, ys_ref, ybuf_ref, ysem_ref, tm)
    x2 = x_ref[...] + mod0_ref[0][5:6] * y
    u = (_rms(x2) * g1_ref[...] * (1.0 + m[1:2]) + m[0:1]).astype(BF16)
    bgate = _dot(u, wi_ref[:, 0:D_MODEL])
    z = _dot(u, wi_ref[:, D_MODEL:2 * D_MODEL]) * _dot(u, wi_ref[:, 2 * D_MODEL:])
    prev = carry_ref[...]
    row = lax.broadcasted_iota(jnp.int32, z.shape, 0)
    z1 = jnp.where(row == 0, prev[7:8], pltpu.roll(z, 1, axis=0))
    z2 = jnp.where(row == 0, prev[6:7], jnp.where(row == 1, prev[7:8], pltpu.roll(z, 2, axis=0)))
    carry_ref[...] = z[tm - SUBLANES:, :]
    cw = cw_ref[...]
    zc = cw[0:1] * z2 + cw[1:2] * z1 + cw[2:3] * z + cb_ref[...]
    x3 = x2 + m[2:3] * _dot((bgate * zc).astype(BF16), wo_ref[...])
    x3_ref[...] = x3
    _route_tail(x3, m, g2_ref, wr_ref, br_ref, tri_ref, u2_ref, cls_ref, rank_ref, cnt_ref, run_ref)
    _drain_row_gather(ys_ref, ybuf_ref, ysem_ref)


def _conv_layer(dest, x, ys, mod0, mod, g1, w_in, conv_w, conv_b, w_out, g2, wr, br, tri):
    t = x.shape[0]
    tm = TM_MIX
    tiles_per_batch = SEQ // tm
    mod_spec = pl.BlockSpec((1, 6, D_MODEL), lambda i, *_: (i // tiles_per_batch, 0, 0))
    return pl.pallas_call(
        _conv_layer_kernel,
        out_shape=_route_out_shapes(t, tm),
        grid_spec=pltpu.PrefetchScalarGridSpec(
            num_scalar_prefetch=1,
            grid=(t // tm,),
            in_specs=[
                pl.BlockSpec((tm, D_MODEL), lambda i, *_: (i, 0)),
                pl.BlockSpec(memory_space=pl.ANY),
                mod_spec,
                mod_spec,
                _const_spec((1, D_MODEL)),
                _const_spec((D_MODEL, 3 * D_MODEL)),
                _const_spec((3, D_MODEL)),
                _const_spec((1, D_MODEL)),
                _const_spec((D_MODEL, D_MODEL)),
                _const_spec((1, D_MODEL)),
                _const_spec((D_MODEL, ROUTER_COLS)),
                _const_spec((N_EXPERTS, 1)),
                _const_spec((tm, tm)),
            ],
            out_specs=_route_out_specs(tm),
            scratch_shapes=[
                pltpu.VMEM((CLASS_ROWS, LANES), F32),
                pltpu.VMEM((SUBLANES, D_MODEL), F32),
                pltpu.VMEM((2, tm * ROW_TILE, LANES), F32),
                pltpu.SemaphoreType.DMA((2,)),
            ],
        ),
        compiler_params=_params("arbitrary"),
        name="conv_layer_route",
    )(dest, x, ys, mod0, mod, g1, w_in, conv_w, conv_b, w_out, g2, wr, br, tri)


def _scatter_kernel(idx_ref, src_ref, init_ref, dst_ref, sem):
    del init_ref
    n = src_ref.shape[0] // ROW_TILE
    base = pl.program_id(0) * n
    unroll = 8

    def body(r8, carry):
        for k in range(unroll):
            r = r8 * unroll + k
            pltpu.make_async_copy(_token_tile(src_ref, r), _token_tile(dst_ref, idx_ref[base + r]),
                                  sem).start()
        return carry

    lax.fori_loop(0, n // unroll, body, 0)
    pltpu.make_async_copy(src_ref, dst_ref.at[pl.ds(0, src_ref.shape[0])], sem).wait()


def _scatter_rows(idx, src, init):
    n_rows = idx.shape[0]
    tb = SCATTER_BLOCK
    return pl.pallas_call(
        _scatter_kernel,
        out_shape=jax.ShapeDtypeStruct(init.shape, init.dtype),
        grid_spec=pltpu.PrefetchScalarGridSpec(
            num_scalar_prefetch=1,
            grid=(n_rows // tb,),
            in_specs=[pl.BlockSpec((tb * ROW_TILE, LANES), lambda i, *_: (i, 0)),
                      pl.BlockSpec(memory_space=pl.ANY)],
            out_specs=pl.BlockSpec(memory_space=pl.ANY),
            scratch_shapes=[pltpu.SemaphoreType.DMA(())]),
        input_output_aliases={2: 0},
        compiler_params=_params("arbitrary"),
        name="scatter_rows",
    )(idx, src, init)


def _expert_kernel(ea_ref, eb_ref, nused_ref, x_ref, wga_ref, wua_ref, wda_ref, wgb_ref, wub_ref,
                   wdb_ref, wra_ref, wrb_ref, y_ref):
    del ea_ref, eb_ref

    @pl.when(pl.program_id(0) < nused_ref[0])
    def _():
        tm = x_ref.shape[0] // ROW_TILE
        xb = _load_token_rows(x_ref, tm).astype(BF16)
        dl = jnp.sum(xb.astype(F32) * (wra_ref[0] - wrb_ref[0]), axis=-1, keepdims=True)
        w_a = _sigmoid(dl)
        w_b = _sigmoid(-dl)

        def ffn(wg_ref, wu_ref, wd_ref):
            gate = _dot(xb, wg_ref[0])
            hidden = gate * _sigmoid(gate) * _dot(xb, wu_ref[0])
            return _dot(hidden.astype(BF16), wd_ref[0])

        y = w_a * ffn(wga_ref, wua_ref, wda_ref) + w_b * ffn(wgb_ref, wub_ref, wdb_ref)
        _store_token_rows(y_ref, y, tm)

    @pl.when(pl.program_id(0) >= nused_ref[0])
    def _():
        y_ref[...] = jnp.zeros_like(y_ref)


def _experts(ea, eb, n_used, xs, w_gate, w_up, w_down, wr_rows):
    tm = TM_EXPERT
    row_tile = lambda j, ea, eb, nu: (jnp.minimum(j, nu[0] - 1), 0)
    sel_a = lambda j, ea, eb, nu: (ea[j], 0, 0)
    sel_b = lambda j, ea, eb, nu: (eb[j], 0, 0)
    up_spec = lambda sel: pl.BlockSpec((1, D_MODEL, D_EXPERT), sel)
    down_spec = lambda sel: pl.BlockSpec((1, D_EXPERT, D_MODEL), sel)
    wr_spec = lambda sel: pl.BlockSpec((1, 1, D_MODEL), sel)
    return pl.pallas_call(
        _expert_kernel,
        out_shape=jax.ShapeDtypeStruct((PADDED_ROWS * ROW_TILE, LANES), F32),
        grid_spec=pltpu.PrefetchScalarGridSpec(
            num_scalar_prefetch=3,
            grid=(N_EXPERT_TILES,),
            in_specs=[
                pl.BlockSpec((tm * ROW_TILE, LANES), row_tile),
                up_spec(sel_a), up_spec(sel_a), down_spec(sel_a),
                up_spec(sel_b), up_spec(sel_b), down_spec(sel_b),
                wr_spec(sel_a), wr_spec(sel_b),
            ],
            out_specs=pl.BlockSpec((tm * ROW_TILE, LANES), lambda j, ea, eb, nu: (j, 0)),
        ),
        compiler_params=_params("arbitrary"),
        name="grouped_experts",
    )(ea, eb, n_used, xs, w_gate, w_up, w_down, w_gate, w_up, w_down, wr_rows, wr_rows)


def _final_kernel(dest_ref, x_ref, ys_ref, mod_ref, g_ref, o_ref, ybuf_ref, ysem_ref):
    y = _gathered_tile(dest_ref, ys_ref, ybuf_ref, ysem_ref, x_ref.shape[0])
    x = x_ref[...] + mod_ref[0][5:6] * y
    o_ref[...] = _rms(x) * g_ref[...]
    _drain_row_gather(ys_ref, ybuf_ref, ysem_ref)


def _final(dest, x, ys, mod, g):
    t = x.shape[0]
    tm = TM_FINAL
    tiles_per_batch = SEQ // tm
    return pl.pallas_call(
        _final_kernel,
        out_shape=jax.ShapeDtypeStruct((t, D_MODEL), F32),
        grid_spec=pltpu.PrefetchScalarGridSpec(
            num_scalar_prefetch=1,
            grid=(t // tm,),
            in_specs=[
                pl.BlockSpec((tm, D_MODEL), lambda i, *_: (i, 0)),
                pl.BlockSpec(memory_space=pl.ANY),
                pl.BlockSpec((1, 6, D_MODEL), lambda i, *_: (i // tiles_per_batch, 0, 0)),
                _const_spec((1, D_MODEL)),
            ],
            out_specs=pl.BlockSpec((tm, D_MODEL), lambda i, *_: (i, 0)),
            scratch_shapes=[pltpu.VMEM((2, tm * ROW_TILE, LANES), F32), pltpu.SemaphoreType.DMA((2,))],
        ),
        compiler_params=_params("arbitrary"),
        name="final_norm",
    )(dest, x, ys, mod, g)


_PAIR_A = np.array([EXPERTS_PER_GROUP * (c // len(PAIRS)) + PAIRS[c % len(PAIRS)][0]
                    for c in range(N_CLASSES)], np.int32)
_PAIR_B = np.array([EXPERTS_PER_GROUP * (c // len(PAIRS)) + PAIRS[c % len(PAIRS)][1]
                    for c in range(N_CLASSES)], np.int32)


def _moe(u2, cls, rank, cnt, w_gate, w_up, w_down, wr_rows):
    counts = cnt[:N_CLASSES, 0]
    tiles = (counts + TM_EXPERT - 1) // TM_EXPERT
    tile_end = jnp.cumsum(tiles)
    row_start = (tile_end - tiles) * TM_EXPERT
    n_used = tile_end[-1:]
    cls = cls.reshape(-1)
    dest = row_start[cls] + rank.reshape(-1)
    tile_cls = jnp.sum(jnp.arange(N_EXPERT_TILES)[:, None] >= tile_end[None, :], axis=1)
    tile_cls = jnp.minimum(tile_cls, tile_cls[jnp.maximum(n_used[0] - 1, 0)])
    ea = jnp.asarray(_PAIR_A)[tile_cls]
    eb = jnp.asarray(_PAIR_B)[tile_cls]
    xs = _scatter_rows(dest, u2, jnp.zeros((PADDED_ROWS * ROW_TILE, LANES), F32))
    ys = _experts(ea, eb, n_used.astype(jnp.int32), xs, w_gate, w_up, w_down, wr_rows)
    return dest, ys


def kernel(x, c, norm1_g, norm2_g, w_ada, b_ada, m_w_in, m_b_gates, m_norm_g, m_w_out,
           c_w_in, c_conv_w, c_conv_b, c_w_out, w_router, b_router,
           e_w_gate, e_w_up, e_w_down, final_g):
    xf = x.reshape(TOKENS, D_MODEL)
    mod = _ada(c, w_ada, b_ada)

    w_in = jnp.concatenate(
        [m_w_in[0].astype(BF16),
         jnp.zeros((D_MODEL, GATE_COLS - 2 * N_HEADS), BF16)], axis=1)
    bg = jnp.zeros((1, GATE_COLS), F32).at[0, :2 * N_HEADS].set(m_b_gates[0])
    wr = jnp.zeros((D_MODEL, ROUTER_COLS), BF16).at[:, :N_EXPERTS].set(w_router.astype(BF16))
    wr_rows = w_router.astype(BF16).astype(F32).T.reshape(N_EXPERTS, 1, D_MODEL)
    br = b_router.reshape(N_EXPERTS, 1)
    tri = (jnp.arange(TM_MIX)[:, None] < jnp.arange(TM_MIX)[None, :]).astype(BF16)
    row = lambda v: v.reshape(1, -1)

    q, k, v, og, gcol, grow = _mlstm_proj(xf, mod[0], row(norm1_g[0]), w_in, bg)
    hg = _mlstm_core(q, k, v, og, gcol, grow, row(m_norm_g[0]))
    x1, u2, cls, rank, cnt = _mlstm_out(hg, xf, mod[0], m_w_out[0].astype(BF16), row(norm2_g[0]),
                                        wr, br, tri)
    dest, ys = _moe(u2, cls, rank, cnt, e_w_gate[0].astype(BF16), e_w_up[0].astype(BF16),
                    e_w_down[0].astype(BF16), wr_rows)

    x3, u2, cls, rank, cnt = _conv_layer(
        dest, x1, ys, mod[0], mod[1], row(norm1_g[1]), c_w_in[0].astype(BF16), c_conv_w[0],
        row(c_conv_b[0]), c_w_out[0].astype(BF16), row(norm2_g[1]), wr, br, tri)
    dest, ys = _moe(u2, cls, rank, cnt, e_w_gate[1].astype(BF16), e_w_up[1].astype(BF16),
                    e_w_down[1].astype(BF16), wr_rows)

    out = _final(dest, x3, ys, mod[1], row(final_g))
    return out.reshape(BATCH, SEQ, D_MODEL)
```

```python
import functools

import jax
import jax.numpy as jnp
import numpy as np
from jax import lax
from jax.experimental import pallas as pl
from jax.experimental.pallas import tpu as pltpu

F32 = jnp.float32
BF16 = jnp.bfloat16

D_MODEL = 1024
BATCH = 4
SEQ = 8192
TOKENS = BATCH * SEQ
N_HEADS = 4
DH_V = 256
DH_QK = 128
QK = N_HEADS * DH_QK
N_EXPERTS = 16
N_GROUPS = 4
EXPERTS_PER_GROUP = 4
D_EXPERT = 512
EPS = 1e-6

LANES = 128
SUBLANES = 8
VMEM_LIMIT_BYTES = 56 * 1024 * 1024

CHUNK = 128
TM_PROJ = 512
TM_MIX = 512
TM_EXPERT = 256
GATE_COLS = LANES
GATE_WIDTH = 16
GATE_KINDS = 3
ROUTER_COLS = LANES
PAIRS = ((0, 1), (0, 2), (0, 3), (1, 2), (1, 3), (2, 3))
N_CLASSES = N_GROUPS * len(PAIRS)
CLASS_ROWS = 32
N_EXPERT_TILES = TOKENS // TM_EXPERT + N_CLASSES
PADDED_ROWS = N_EXPERT_TILES * TM_EXPERT
SCATTER_BLOCK = 512
TM_FINAL = 512
STREAM_GROUP = 4


def _params(*semantics):
    return pltpu.CompilerParams(dimension_semantics=semantics, vmem_limit_bytes=VMEM_LIMIT_BYTES)


def _dot(a, b):
    return jnp.dot(a, b, preferred_element_type=F32)


def _rms(x):
    return x * lax.rsqrt(jnp.mean(x * x, axis=-1, keepdims=True) + EPS)


def _sigmoid(x):
    return 1.0 / (1.0 + jnp.exp(-x))


ROW_TILE = D_MODEL // LANES


def _load_token_rows(ref, n):
    return jnp.concatenate([ref[pl.ds(c, n, stride=ROW_TILE), :] for c in range(ROW_TILE)], axis=1)


def _store_token_rows(ref, val, n):
    for c in range(ROW_TILE):
        ref[pl.ds(c, n, stride=ROW_TILE), :] = val[:, c * LANES:(c + 1) * LANES]


def _token_tile(ref, t):
    return ref.at[pl.ds(pl.multiple_of(t * ROW_TILE, ROW_TILE), ROW_TILE)]


def _start_row_gather(idx_ref, base, n, src_hbm, buf, sem):
    for r in range(n):
        pltpu.make_async_copy(_token_tile(src_hbm, idx_ref[base + r]), _token_tile(buf, r),
                              sem).start(priority=r % 2)


def _wait_row_gather(src_hbm, buf, sem):
    pltpu.make_async_copy(src_hbm.at[pl.ds(0, buf.shape[0])], buf, sem).wait()


def _gathered_tile(idx_ref, src_hbm, buf_ref, sem_ref, n, issue_first):
    i = pl.program_id(0)
    last = pl.num_programs(0) - 1
    slot = i % 2

    @pl.when(i == 0)
    def _():
        _start_row_gather(idx_ref, 0, n, src_hbm, buf_ref.at[0], sem_ref.at[0])

    start_next = functools.partial(
        _start_row_gather, idx_ref, jnp.minimum(i + 1, last) * n, n, src_hbm, buf_ref.at[1 - slot],
        sem_ref.at[1 - slot])
    if issue_first:
        start_next()
    _wait_row_gather(src_hbm, buf_ref.at[slot], sem_ref.at[slot])
    rows = _load_token_rows(buf_ref.at[slot], n)
    if not issue_first:
        start_next()
    return rows


def _drain_row_gather(src_hbm, buf_ref, sem_ref):
    i = pl.program_id(0)

    @pl.when(i == pl.num_programs(0) - 1)
    def _():
        _wait_row_gather(src_hbm, buf_ref.at[1 - i % 2], sem_ref.at[1 - i % 2])


def _ada_kernel(c_ref, w_ref, b_ref, o_ref):
    c = c_ref[...]
    cond = c * _sigmoid(c)
    o_ref[0] = jnp.dot(cond, w_ref[0], preferred_element_type=F32,
                       precision=lax.Precision.HIGHEST) + b_ref[0]


def _ada(c, w_ada, b_ada):
    depth, d, n = w_ada.shape
    tn = 1536
    c8 = jnp.zeros((SUBLANES, d), F32).at[:BATCH].set(c)
    out = pl.pallas_call(
        _ada_kernel,
        out_shape=jax.ShapeDtypeStruct((depth, SUBLANES, n), F32),
        grid=(depth, n // tn),
        in_specs=[
            pl.BlockSpec((SUBLANES, d), lambda l, j: (0, 0)),
            pl.BlockSpec((1, d, tn), lambda l, j: (l, 0, j)),
            pl.BlockSpec((1, 1, tn), lambda l, j: (l, 0, j)),
        ],
        out_specs=pl.BlockSpec((1, SUBLANES, tn), lambda l, j: (l, 0, j)),
        compiler_params=_params("arbitrary", "arbitrary"),
        name="ada_mod",
    )(c8, w_ada, b_ada.reshape(depth, 1, n))
    return out[:, :BATCH].reshape(depth, BATCH, 6, d)


def _mlstm_proj_kernel(x_ref, mod_ref, g_ref, w_ref, bg_ref, q_ref, k_ref, v_ref, og_ref,
                       gcol_ref, grow_ref):
    x = x_ref[...]
    m = mod_ref[0]
    u = (_rms(x) * g_ref[...] * (1.0 + m[1:2]) + m[0:1]).astype(BF16)
    q_ref[...] = (_dot(u, w_ref[:, 0:QK]) * (DH_QK ** -0.5)).astype(BF16)
    k_ref[...] = _dot(u, w_ref[:, QK:2 * QK]).astype(BF16)
    v_ref[...] = _dot(u, w_ref[:, 2 * QK:2 * QK + D_MODEL]).astype(BF16)
    og_ref[...] = _dot(u, w_ref[:, 2 * QK + D_MODEL:2 * QK + 2 * D_MODEL]).astype(BF16)
    gt = _dot(u, w_ref[:, 2 * QK + 2 * D_MODEL:]) + bg_ref[...]
    lane = lax.broadcasted_iota(jnp.int32, gt.shape, 1)
    pos = lax.broadcasted_iota(jnp.int32, gt.shape, 0) % CHUNK
    log_f = jnp.minimum(gt, 0.0) - jnp.log(1.0 + jnp.exp(-jnp.abs(gt)))
    b = log_f
    shift = 1
    while shift < CHUNK:
        b = b + jnp.where(pos >= shift, pltpu.roll(b, shift, axis=0), 0.0)
        shift *= 2
    imb = gt - pltpu.roll(b, LANES - N_HEADS, axis=1)
    cm = imb
    shift = 1
    while shift < CHUNK:
        cm = jnp.maximum(cm, jnp.where(pos >= shift, pltpu.roll(cm, shift, axis=0), -jnp.inf))
        shift *= 2
    cols = jnp.where(lane < N_HEADS, cm,
                     jnp.where(lane < 2 * N_HEADS, b,
                               jnp.where(lane < 3 * N_HEADS, pltpu.roll(imb, 2 * N_HEADS, axis=1), 0.0)))
    gcol_ref[...] = cols[:, 0:GATE_WIDTH]
    grow_ref[0] = imb.T[0:SUBLANES, :]


def _mlstm_proj(x, mod, g, w, bg):
    t, d = x.shape
    n = w.shape[1]
    tm = TM_PROJ
    tiles_per_batch = SEQ // tm
    return pl.pallas_call(
        _mlstm_proj_kernel,
        out_shape=(
            jax.ShapeDtypeStruct((t, QK), BF16),
            jax.ShapeDtypeStruct((t, QK), BF16),
            jax.ShapeDtypeStruct((t, D_MODEL), BF16),
            jax.ShapeDtypeStruct((t, D_MODEL), BF16),
            jax.ShapeDtypeStruct((t, GATE_WIDTH), F32),
            jax.ShapeDtypeStruct((BATCH, SUBLANES, SEQ), F32),
        ),
        grid=(t // tm,),
        in_specs=[
            pl.BlockSpec((tm, d), lambda i, *_: (i, 0)),
            pl.BlockSpec((1, 6, d), lambda i, *_: (i // tiles_per_batch, 0, 0)),
            pl.BlockSpec((1, d), lambda i, *_: (0, 0)),
            pl.BlockSpec((d, n), lambda i, *_: (0, 0)),
            pl.BlockSpec((1, GATE_COLS), lambda i, *_: (0, 0)),
        ],
        out_specs=(
            pl.BlockSpec((tm, QK), lambda i, *_: (i, 0)),
            pl.BlockSpec((tm, QK), lambda i, *_: (i, 0)),
            pl.BlockSpec((tm, D_MODEL), lambda i, *_: (i, 0)),
            pl.BlockSpec((tm, D_MODEL), lambda i, *_: (i, 0)),
            pl.BlockSpec((tm, GATE_WIDTH), lambda i, *_: (i, 0)),
            pl.BlockSpec((1, SUBLANES, tm),
                         lambda i, *_: (i // tiles_per_batch, 0, i % tiles_per_batch)),
        ),
        compiler_params=_params("arbitrary"),
        name="mlstm_proj",
    )(x, mod, g, w, bg)


def _mlstm_core_kernel(q_ref, k_ref, v_ref, og_ref, gcol_ref, grow_ref, ng_ref, rep_ref, o_ref,
                       m_ref, *c_refs):
    @pl.when(pl.program_id(0) == 0)
    def _():
        m_ref[...] = jnp.zeros_like(m_ref)
        for c_ref in c_refs:
            c_ref[...] = jnp.zeros_like(c_ref)

    ln = CHUNK
    row = lax.broadcasted_iota(jnp.int32, (ln, ln), 0)
    col = lax.broadcasted_iota(jnp.int32, (ln, ln), 1)
    causal = col <= row
    ones = jnp.ones((ln, LANES), BF16)
    n_streams = BATCH * N_HEADS
    m_all = [m_ref[st] for st in range(n_streams)]
    m_out = [None] * n_streams
    for bi in range(BATCH):
        g = gcol_ref[bi]
        g_hi = g.astype(BF16)
        g_r1 = g - g_hi.astype(F32)
        g_mid = g_r1.astype(BF16)
        g_lo = (g_r1 - g_mid.astype(F32)).astype(BF16)
        rep = _dot(g_hi, rep_ref[...]) + _dot(g_mid, rep_ref[...]) + _dot(g_lo, rep_ref[...])
        for h in range(N_HEADS):
            st = bi * N_HEADS + h
            lanes = lambda kind: slice((kind * N_HEADS + h) * LANES, (kind * N_HEADS + h + 1) * LANES)
            cm, b, imb_col = rep[:, lanes(0)], rep[:, lanes(1)], rep[:, lanes(2)]
            imb_row = grow_ref[bi, h:h + 1, :]
            qh = q_ref[bi, :, h * DH_QK:(h + 1) * DH_QK]
            kh = k_ref[bi, :, h * DH_QK:(h + 1) * DH_QK]
            v_ext = jnp.concatenate([v_ref[bi, :, h * DH_V:(h + 1) * DH_V], ones], axis=1)
            c_ref = c_refs[st]
            c_prev = c_ref[...]
            m_prev = m_all[st]

            big_m = jnp.maximum(m_prev, cm)
            d_mat = jnp.exp(jnp.where(causal, imb_row - big_m, -jnp.inf))
            s = lax.dot_general(qh, kh, (((1,), (1,)), ((), ())), preferred_element_type=F32) * d_mat
            q_inter = (qh.astype(F32) * jnp.exp(m_prev - big_m)).astype(BF16)
            lhs = jnp.concatenate([q_inter, s.astype(BF16)], axis=1)
            rhs = jnp.concatenate([c_prev.astype(BF16), v_ext], axis=0)
            nd = _dot(lhs, rhs)
            den = nd[:, DH_V:]
            inv = 1.0 / jnp.maximum(jnp.abs(den), jnp.exp(-(b + big_m)))
            hh = nd[:, :DH_V] * jnp.concatenate([inv, inv], axis=1)

            m_last = big_m[ln - 1:ln, :]
            kw = (kh.astype(F32) * jnp.exp(imb_col - m_last)).astype(BF16)
            decay = jnp.exp(m_prev - m_last)
            c_ref[...] = jnp.concatenate([decay] * 3, axis=1) * c_prev + lax.dot_general(
                kw, v_ext, (((0,), (0,)), ((), ())), preferred_element_type=F32)
            m_out[st] = b[ln - 1:ln, :] + m_last

            sl = slice(h * DH_V, (h + 1) * DH_V)
            gate = _sigmoid(og_ref[bi, :, sl].astype(F32))
            o_ref[bi, :, sl] = (_rms(hh) * ng_ref[:, sl] * gate).astype(BF16)
    for st in range(n_streams):
        m_ref[st] = m_out[st]


def _mlstm_core(q, k, v, og, gcol, grow, norm_g):
    nc = SEQ // CHUNK
    n_streams = BATCH * N_HEADS
    per_batch = lambda a: a.reshape(BATCH, SEQ, a.shape[-1])
    chunk_rows = lambda width: pl.BlockSpec((BATCH, CHUNK, width), lambda c: (0, c, 0))
    n_rep = GATE_KINDS * N_HEADS
    replicate = (jnp.arange(GATE_WIDTH)[:, None] == jnp.arange(n_rep * LANES)[None, :] // LANES
                 ).astype(BF16)
    out = pl.pallas_call(
        _mlstm_core_kernel,
        out_shape=jax.ShapeDtypeStruct((BATCH, SEQ, D_MODEL), BF16),
        grid=(nc,),
        in_specs=[
            chunk_rows(QK),
            chunk_rows(QK),
            chunk_rows(D_MODEL),
            chunk_rows(D_MODEL),
            chunk_rows(GATE_WIDTH),
            pl.BlockSpec((BATCH, SUBLANES, CHUNK), lambda c: (0, 0, c)),
            pl.BlockSpec((1, D_MODEL), lambda c: (0, 0)),
            pl.BlockSpec((GATE_WIDTH, n_rep * LANES), lambda c: (0, 0)),
        ],
        out_specs=chunk_rows(D_MODEL),
        scratch_shapes=[pltpu.VMEM((n_streams, 1, LANES), F32)]
        + [pltpu.VMEM((DH_QK, DH_V + LANES), F32) for _ in range(n_streams)],
        compiler_params=_params("arbitrary"),
        name="mlstm_core",
    )(per_batch(q), per_batch(k), per_batch(v), per_batch(og), per_batch(gcol), grow, norm_g,
      replicate)
    return out.reshape(TOKENS, D_MODEL)


def _top2_sum(v0, v1, v2, v3):
    hi1, lo1 = jnp.maximum(v0, v1), jnp.minimum(v0, v1)
    hi2, lo2 = jnp.maximum(v2, v3), jnp.minimum(v2, v3)
    return jnp.maximum(hi1, hi2) + jnp.maximum(jnp.minimum(hi1, hi2), jnp.maximum(lo1, lo2))


def _route_tail(x_new, m, g2_ref, wr_ref, br_ref, tri_ref, u2_ref, cls_ref, rank_ref, cnt_ref,
                run_ref):
    @pl.when(pl.program_id(0) == 0)
    def _():
        run_ref[...] = jnp.zeros_like(run_ref)

    u2 = _rms(x_new) * g2_ref[...] * (1.0 + m[4:5]) + m[3:4]
    _store_token_rows(u2_ref, u2, u2.shape[0])
    logits = _dot(u2.astype(BF16), wr_ref[...])
    lt = logits.T[0:N_EXPERTS, :]
    e = jnp.exp(lt - jnp.max(lt, axis=0, keepdims=True))
    probs = e / jnp.sum(e, axis=0, keepdims=True)
    sel = probs + br_ref[...]
    sel_rows = [sel[j:j + 1, :] for j in range(N_EXPERTS)]
    best = jnp.zeros_like(sel_rows[0], dtype=jnp.int32)
    best_score = _top2_sum(*sel_rows[0:EXPERTS_PER_GROUP])
    for g in range(1, N_GROUPS):
        score = _top2_sum(*sel_rows[g * EXPERTS_PER_GROUP:(g + 1) * EXPERTS_PER_GROUP])
        better = score > best_score
        best = jnp.where(better, g, best)
        best_score = jnp.where(better, score, best_score)
    s = []
    for j in range(EXPERTS_PER_GROUP):
        sj = sel_rows[j]
        for g in range(1, N_GROUPS):
            sj = jnp.where(best == g, sel_rows[g * EXPERTS_PER_GROUP + j], sj)
        s.append(sj)
    chosen = []
    for j in range(EXPERTS_PER_GROUP):
        beaten = jnp.zeros_like(best)
        for i in range(EXPERTS_PER_GROUP):
            if i == j:
                continue
            wins = (s[i] >= s[j]) if i < j else (s[i] > s[j])
            beaten = beaten + jnp.where(wins, 1, 0)
        chosen.append(beaten < 2)
    pair = jnp.full_like(best, len(PAIRS) - 1)
    for p in range(len(PAIRS) - 2, -1, -1):
        a, b = PAIRS[p]
        pair = jnp.where(jnp.logical_and(chosen[a], chosen[b]), p, pair)
    cls = best * len(PAIRS) + pair
    cls_ref[0] = cls

    class_id = lax.broadcasted_iota(jnp.int32, (CLASS_ROWS, cls.shape[1]), 0)
    onehot = class_id == cls
    before = _dot(jnp.where(onehot, 1.0, 0.0).astype(BF16), tri_ref[...])
    run = run_ref[...]
    rank = jnp.sum(jnp.where(onehot, before + run[:, 0:1], 0.0), axis=0, keepdims=True)
    rank_ref[0] = rank.astype(jnp.int32)
    run = run + jnp.sum(jnp.where(onehot, 1.0, 0.0), axis=1, keepdims=True)
    run_ref[...] = run
    cnt_ref[...] = run.astype(jnp.int32)


def _route_out_shapes(t, tm):
    return (
        jax.ShapeDtypeStruct((t, D_MODEL), F32),
        jax.ShapeDtypeStruct((t * ROW_TILE, LANES), F32),
        jax.ShapeDtypeStruct((t // tm, 1, tm), jnp.int32),
        jax.ShapeDtypeStruct((t // tm, 1, tm), jnp.int32),
        jax.ShapeDtypeStruct((CLASS_ROWS, LANES), jnp.int32),
    )


def _route_out_specs(tm):
    return (
        pl.BlockSpec((tm, D_MODEL), lambda i, *_: (i, 0)),
        pl.BlockSpec((tm * ROW_TILE, LANES), lambda i, *_: (i, 0)),
        pl.BlockSpec((1, 1, tm), lambda i, *_: (i, 0, 0)),
        pl.BlockSpec((1, 1, tm), lambda i, *_: (i, 0, 0)),
        pl.BlockSpec((CLASS_ROWS, LANES), lambda i, *_: (0, 0)),
    )


def _const_spec(shape):
    return pl.BlockSpec(shape, lambda i, *_: (0,) * len(shape))


def _mlstm_out_kernel(h_ref, x_ref, mod_ref, wo_ref, g2_ref, wr_ref, br_ref, tri_ref,
                      x1_ref, u2_ref, cls_ref, rank_ref, cnt_ref, run_ref):
    m = mod_ref[0]
    x1 = x_ref[...] + m[2:3] * _dot(h_ref[...], wo_ref[...])
    x1_ref[...] = x1
    _route_tail(x1, m, g2_ref, wr_ref, br_ref, tri_ref, u2_ref, cls_ref, rank_ref, cnt_ref, run_ref)


def _mlstm_out(hg, x, mod, w_out, g2, wr, br, tri):
    t = x.shape[0]
    tm = TM_MIX
    tiles_per_batch = SEQ // tm
    return pl.pallas_call(
        _mlstm_out_kernel,
        out_shape=_route_out_shapes(t, tm),
        grid=(t // tm,),
        in_specs=[
            pl.BlockSpec((tm, D_MODEL), lambda i, *_: (i, 0)),
            pl.BlockSpec((tm, D_MODEL), lambda i, *_: (i, 0)),
            pl.BlockSpec((1, 6, D_MODEL), lambda i, *_: (i // tiles_per_batch, 0, 0)),
            _const_spec((D_MODEL, D_MODEL)),
            _const_spec((1, D_MODEL)),
            _const_spec((D_MODEL, ROUTER_COLS)),
            _const_spec((N_EXPERTS, 1)),
            _const_spec((tm, tm)),
        ],
        out_specs=_route_out_specs(tm),
        scratch_shapes=[pltpu.VMEM((CLASS_ROWS, LANES), F32)],
        compiler_params=_params("arbitrary"),
        name="mlstm_out_route",
    )(hg, x, mod, w_out, g2, wr, br, tri)


def _conv_layer_kernel(dest_ref, x_ref, ys_ref, mod0_ref, mod_ref, g1_ref, wi_ref, cw_ref, cb_ref,
                       wo_ref, g2_ref, wr_ref, br_ref, tri_ref,
                       x3_ref, u2_ref, cls_ref, rank_ref, cnt_ref,
                       run_ref, carry_ref, ybuf_ref, ysem_ref):
    tm = x_ref.shape[0]
    tiles_per_batch = SEQ // tm

    @pl.when(pl.program_id(0) % tiles_per_batch == 0)
    def _():
        carry_ref[...] = jnp.zeros_like(carry_ref)

    m = mod_ref[0]
    y = _gathered_tile(dest_ref, ys_ref, ybuf_ref, ysem_ref, tm, issue_first=False)
    x2 = x_ref[...] + mod0_ref[0][5:6] * y
    u = (_rms(x2) * g1_ref[...] * (1.0 + m[1:2]) + m[0:1]).astype(BF16)
    bgate = _dot(u, wi_ref[:, 0:D_MODEL])
    z = _dot(u, wi_ref[:, D_MODEL:2 * D_MODEL]) * _dot(u, wi_ref[:, 2 * D_MODEL:])
    prev = carry_ref[...]
    row = lax.broadcasted_iota(jnp.int32, z.shape, 0)
    z1 = jnp.where(row == 0, prev[7:8], pltpu.roll(z, 1, axis=0))
    z2 = jnp.where(row == 0, prev[6:7], jnp.where(row == 1, prev[7:8], pltpu.roll(z, 2, axis=0)))
    carry_ref[...] = z[tm - SUBLANES:, :]
    cw = cw_ref[...]
    zc = cw[0:1] * z2 + cw[1:2] * z1 + cw[2:3] * z + cb_ref[...]
    x3 = x2 + m[2:3] * _dot((bgate * zc).astype(BF16), wo_ref[...])
    x3_ref[...] = x3
    _route_tail(x3, m, g2_ref, wr_ref, br_ref, tri_ref, u2_ref, cls_ref, rank_ref, cnt_ref, run_ref)
    _drain_row_gather(ys_ref, ybuf_ref, ysem_ref)


def _conv_layer(dest, x, ys, mod0, mod, g1, w_in, conv_w, conv_b, w_out, g2, wr, br, tri):
    t = x.shape[0]
    tm = TM_MIX
    tiles_per_batch = SEQ // tm
    mod_spec = pl.BlockSpec((1, 6, D_MODEL), lambda i, *_: (i // tiles_per_batch, 0, 0))
    return pl.pallas_call(
        _conv_layer_kernel,
        out_shape=_route_out_shapes(t, tm),
        grid_spec=pltpu.PrefetchScalarGridSpec(
            num_scalar_prefetch=1,
            grid=(t // tm,),
            in_specs=[
                pl.BlockSpec((tm, D_MODEL), lambda i, *_: (i, 0)),
                pl.BlockSpec(memory_space=pl.ANY),
                mod_spec,
                mod_spec,
                _const_spec((1, D_MODEL)),
                _const_spec((D_MODEL, 3 * D_MODEL)),
                _const_spec((3, D_MODEL)),
                _const_spec((1, D_MODEL)),
                _const_spec((D_MODEL, D_MODEL)),
                _const_spec((1, D_MODEL)),
                _const_spec((D_MODEL, ROUTER_COLS)),
                _const_spec((N_EXPERTS, 1)),
                _const_spec((tm, tm)),
            ],
            out_specs=_route_out_specs(tm),
            scratch_shapes=[
                pltpu.VMEM((CLASS_ROWS, LANES), F32),
                pltpu.VMEM((SUBLANES, D_MODEL), F32),
                pltpu.VMEM((2, tm * ROW_TILE, LANES), F32),
                pltpu.SemaphoreType.DMA((2,)),
            ],
        ),
        compiler_params=_params("arbitrary"),
        name="conv_layer_route",
    )(dest, x, ys, mod0, mod, g1, w_in, conv_w, conv_b, w_out, g2, wr, br, tri)


def _scatter_kernel(idx_ref, src_ref, init_ref, dst_ref, sem):
    del init_ref
    n = src_ref.shape[0] // ROW_TILE
    base = pl.program_id(0) * n
    unroll = 8

    def body(r8, carry):
        for k in range(unroll):
            r = r8 * unroll + k
            pltpu.make_async_copy(_token_tile(src_ref, r), _token_tile(dst_ref, idx_ref[base + r]),
                                  sem).start(priority=k % 2)
        return carry

    lax.fori_loop(0, n // unroll, body, 0)
    pltpu.make_async_copy(src_ref, dst_ref.at[pl.ds(0, src_ref.shape[0])], sem).wait()


def _scatter_rows(idx, src, init):
    n_rows = idx.shape[0]
    tb = SCATTER_BLOCK
    return pl.pallas_call(
        _scatter_kernel,
        out_shape=jax.ShapeDtypeStruct(init.shape, init.dtype),
        grid_spec=pltpu.PrefetchScalarGridSpec(
            num_scalar_prefetch=1,
            grid=(n_rows // tb,),
            in_specs=[pl.BlockSpec((tb * ROW_TILE, LANES), lambda i, *_: (i, 0)),
                      pl.BlockSpec(memory_space=pl.ANY)],
            out_specs=pl.BlockSpec(memory_space=pl.ANY),
            scratch_shapes=[pltpu.SemaphoreType.DMA(())]),
        input_output_aliases={2: 0},
        compiler_params=_params("arbitrary"),
        name="scatter_rows",
    )(idx, src, init)


def _expert_kernel(ea_ref, eb_ref, chg_ref, nused_ref, x_ref, wga_ref, wua_ref, wda_ref, wgb_ref,
                   wub_ref, wdb_ref, wra_ref, wrb_ref, y_ref, *wbf_refs):
    del ea_ref, eb_ref
    j = pl.program_id(0)

    @pl.when(chg_ref[j] == 1)
    def _():
        for src, dst in zip((wga_ref, wua_ref, wda_ref, wgb_ref, wub_ref, wdb_ref), wbf_refs):
            dst[...] = src[0, 0].astype(BF16)

    @pl.when(j < nused_ref[0])
    def _():
        tm = x_ref.shape[0] // ROW_TILE
        xb = _load_token_rows(x_ref, tm).astype(BF16)
        dl = jnp.sum(xb.astype(F32) * (wra_ref[0] - wrb_ref[0]), axis=-1, keepdims=True)
        w_a = _sigmoid(dl)
        w_b = _sigmoid(-dl)

        def ffn(wg_ref, wu_ref, wd_ref):
            gate = _dot(xb, wg_ref[...])
            hidden = gate * _sigmoid(gate) * _dot(xb, wu_ref[...])
            return _dot(hidden.astype(BF16), wd_ref[...])

        y = w_a * ffn(*wbf_refs[0:3]) + w_b * ffn(*wbf_refs[3:6])
        _store_token_rows(y_ref, y, tm)

    @pl.when(j >= nused_ref[0])
    def _():
        y_ref[...] = jnp.zeros_like(y_ref)


def _experts(layer, ea, eb, chg, n_used, xs, w_gate, w_up, w_down, wr_rows):
    tm = TM_EXPERT
    row_tile = lambda j, ea, eb, chg, nu: (jnp.minimum(j, nu[0] - 1), 0)
    sel_a = lambda j, ea, eb, chg, nu: (layer, ea[j], 0, 0)
    sel_b = lambda j, ea, eb, chg, nu: (layer, eb[j], 0, 0)
    up_spec = lambda sel: pl.BlockSpec((1, 1, D_MODEL, D_EXPERT), sel)
    down_spec = lambda sel: pl.BlockSpec((1, 1, D_EXPERT, D_MODEL), sel)
    wr_spec = lambda sel: pl.BlockSpec((1, 1, D_MODEL), lambda *a: sel(*a)[1:])
    up_scratch = pltpu.VMEM((D_MODEL, D_EXPERT), BF16)
    down_scratch = pltpu.VMEM((D_EXPERT, D_MODEL), BF16)
    return pl.pallas_call(
        _expert_kernel,
        out_shape=jax.ShapeDtypeStruct((PADDED_ROWS * ROW_TILE, LANES), F32),
        grid_spec=pltpu.PrefetchScalarGridSpec(
            num_scalar_prefetch=4,
            grid=(N_EXPERT_TILES,),
            in_specs=[
                pl.BlockSpec((tm * ROW_TILE, LANES), row_tile),
                up_spec(sel_a), up_spec(sel_a), down_spec(sel_a),
                up_spec(sel_b), up_spec(sel_b), down_spec(sel_b),
                wr_spec(sel_a), wr_spec(sel_b),
            ],
            out_specs=pl.BlockSpec((tm * ROW_TILE, LANES), lambda j, *_: (j, 0)),
            scratch_shapes=[up_scratch, up_scratch, down_scratch, up_scratch, up_scratch, down_scratch],
        ),
        compiler_params=_params("arbitrary"),
        name="grouped_experts",
    )(ea, eb, chg, n_used, xs, w_gate, w_up, w_down, w_gate, w_up, w_down, wr_rows, wr_rows)


def _final_kernel(dest_ref, x_ref, ys_ref, mod_ref, g_ref, o_ref, ybuf_ref, ysem_ref):
    y = _gathered_tile(dest_ref, ys_ref, ybuf_ref, ysem_ref, x_ref.shape[0], issue_first=True)
    x = x_ref[...] + mod_ref[0][5:6] * y
    o_ref[...] = _rms(x) * g_ref[...]
    _drain_row_gather(ys_ref, ybuf_ref, ysem_ref)


def _final(dest, x, ys, mod, g):
    t = x.shape[0]
    tm = TM_FINAL
    tiles_per_batch = SEQ // tm
    return pl.pallas_call(
        _final_kernel,
        out_shape=jax.ShapeDtypeStruct((t, D_MODEL), F32),
        grid_spec=pltpu.PrefetchScalarGridSpec(
            num_scalar_prefetch=1,
            grid=(t // tm,),
            in_specs=[
                pl.BlockSpec((tm, D_MODEL), lambda i, *_: (i, 0)),
                pl.BlockSpec(memory_space=pl.ANY),
                pl.BlockSpec((1, 6, D_MODEL), lambda i, *_: (i // tiles_per_batch, 0, 0)),
                _const_spec((1, D_MODEL)),
            ],
            out_specs=pl.BlockSpec((tm, D_MODEL), lambda i, *_: (i, 0)),
            scratch_shapes=[pltpu.VMEM((2, tm * ROW_TILE, LANES), F32), pltpu.SemaphoreType.DMA((2,))],
        ),
        compiler_params=_params("arbitrary"),
        name="final_norm",
    )(dest, x, ys, mod, g)


_PAIR_A = np.array([EXPERTS_PER_GROUP * (c // len(PAIRS)) + PAIRS[c % len(PAIRS)][0]
                    for c in range(N_CLASSES)], np.int32)
_PAIR_B = np.array([EXPERTS_PER_GROUP * (c // len(PAIRS)) + PAIRS[c % len(PAIRS)][1]
                    for c in range(N_CLASSES)], np.int32)


def _moe(layer, u2, cls, rank, cnt, w_gate, w_up, w_down, wr_rows):
    counts = cnt[:N_CLASSES, 0]
    tiles = (counts + TM_EXPERT - 1) // TM_EXPERT
    tile_end = jnp.cumsum(tiles)
    row_start = (tile_end - tiles) * TM_EXPERT
    n_used = tile_end[-1:]
    cls = cls.reshape(-1)
    dest = row_start[cls] + rank.reshape(-1)
    tile_cls = jnp.sum(jnp.arange(N_EXPERT_TILES)[:, None] >= tile_end[None, :], axis=1)
    tile_cls = jnp.minimum(tile_cls, tile_cls[jnp.maximum(n_used[0] - 1, 0)])
    ea = jnp.asarray(_PAIR_A)[tile_cls]
    eb = jnp.asarray(_PAIR_B)[tile_cls]
    chg = jnp.concatenate([jnp.ones((1,), jnp.int32),
                           (tile_cls[1:] != tile_cls[:-1]).astype(jnp.int32)])
    xs = _scatter_rows(dest, u2, jnp.zeros((PADDED_ROWS * ROW_TILE, LANES), F32))
    ys = _experts(layer, ea, eb, chg, n_used.astype(jnp.int32), xs, w_gate, w_up, w_down, wr_rows)
    return dest, ys


def kernel(x, c, norm1_g, norm2_g, w_ada, b_ada, m_w_in, m_b_gates, m_norm_g, m_w_out,
           c_w_in, c_conv_w, c_conv_b, c_w_out, w_router, b_router,
           e_w_gate, e_w_up, e_w_down, final_g):
    xf = x.reshape(TOKENS, D_MODEL)
    mod = _ada(c, w_ada, b_ada)

    w_in = jnp.concatenate(
        [m_w_in[0].astype(BF16),
         jnp.zeros((D_MODEL, GATE_COLS - 2 * N_HEADS), BF16)], axis=1)
    bg = jnp.zeros((1, GATE_COLS), F32).at[0, :2 * N_HEADS].set(m_b_gates[0])
    wr = jnp.zeros((D_MODEL, ROUTER_COLS), BF16).at[:, :N_EXPERTS].set(w_router.astype(BF16))
    wr_rows = w_router.astype(BF16).astype(F32).T.reshape(N_EXPERTS, 1, D_MODEL)
    br = b_router.reshape(N_EXPERTS, 1)
    tri = (jnp.arange(TM_MIX)[:, None] < jnp.arange(TM_MIX)[None, :]).astype(BF16)
    row = lambda v: v.reshape(1, -1)

    q, k, v, og, gcol, grow = _mlstm_proj(xf, mod[0], row(norm1_g[0]), w_in, bg)
    hg = _mlstm_core(q, k, v, og, gcol, grow, row(m_norm_g[0]))
    x1, u2, cls, rank, cnt = _mlstm_out(hg, xf, mod[0], m_w_out[0].astype(BF16), row(norm2_g[0]),
                                        wr, br, tri)
    dest, ys = _moe(0, u2, cls, rank, cnt, e_w_gate, e_w_up, e_w_down, wr_rows)

    x3, u2, cls, rank, cnt = _conv_layer(
        dest, x1, ys, mod[0], mod[1], row(norm1_g[1]), c_w_in[0].astype(BF16), c_conv_w[0],
        row(c_conv_b[0]), c_w_out[0].astype(BF16), row(norm2_g[1]), wr, br, tri)
    dest, ys = _moe(1, u2, cls, rank, cnt, e_w_gate, e_w_up, e_w_down, wr_rows)

    out = _final(dest, x3, ys, mod[1], row(final_g))
    return out.reshape(BATCH, SEQ, D_MODEL)
```

```python
import functools

import jax
import jax.numpy as jnp
import numpy as np
from jax import lax
from jax.experimental import pallas as pl
from jax.experimental.pallas import tpu as pltpu

F32 = jnp.float32
BF16 = jnp.bfloat16

D_MODEL = 1024
BATCH = 4
SEQ = 8192
TOKENS = BATCH * SEQ
N_HEADS = 4
DH_V = 256
DH_QK = 128
QK = N_HEADS * DH_QK
N_EXPERTS = 16
N_GROUPS = 4
EXPERTS_PER_GROUP = 4
D_EXPERT = 512
EPS = 1e-6

LANES = 128
SUBLANES = 8
VMEM_LIMIT_BYTES = 56 * 1024 * 1024

CHUNK = 128
TM_PROJ = 512
TM_MIX = 512
TM_EXPERT = 256
GATE_COLS = LANES
GATE_WIDTH = 16
GATE_KINDS = 3
ROUTER_COLS = LANES
PAIRS = ((0, 1), (0, 2), (0, 3), (1, 2), (1, 3), (2, 3))
N_CLASSES = N_GROUPS * len(PAIRS)
CLASS_ROWS = 32
N_EXPERT_TILES = TOKENS // TM_EXPERT + N_CLASSES
PADDED_ROWS = N_EXPERT_TILES * TM_EXPERT
TM_FINAL = 512
STREAM_GROUP = 4


def _params(*semantics):
    return pltpu.CompilerParams(dimension_semantics=semantics, vmem_limit_bytes=VMEM_LIMIT_BYTES)


def _dot(a, b):
    return jnp.dot(a, b, preferred_element_type=F32)


def _rms(x):
    return x * lax.rsqrt(jnp.mean(x * x, axis=-1, keepdims=True) + EPS)


def _sigmoid(x):
    return 1.0 / (1.0 + jnp.exp(-x))


ROW_TILE = D_MODEL // LANES


def _load_token_rows(ref, n):
    return jnp.concatenate([ref[pl.ds(c, n, stride=ROW_TILE), :] for c in range(ROW_TILE)], axis=1)


def _store_token_rows(ref, val, n):
    for c in range(ROW_TILE):
        ref[pl.ds(c, n, stride=ROW_TILE), :] = val[:, c * LANES:(c + 1) * LANES]


def _token_tile(ref, t):
    return ref.at[pl.ds(pl.multiple_of(t * ROW_TILE, ROW_TILE), ROW_TILE)]


def _start_row_gather(idx_ref, base, n, src_hbm, buf, sem):
    for r in range(n):
        pltpu.make_async_copy(_token_tile(src_hbm, idx_ref[base + r]), _token_tile(buf, r),
                              sem).start(priority=r % 2)


def _wait_row_gather(src_hbm, buf, sem):
    pltpu.make_async_copy(src_hbm.at[pl.ds(0, buf.shape[0])], buf, sem).wait()


def _gathered_tile(idx_ref, src_hbm, buf_ref, sem_ref, n, issue_first):
    i = pl.program_id(0)
    last = pl.num_programs(0) - 1
    slot = i % 2

    @pl.when(i == 0)
    def _():
        _start_row_gather(idx_ref, 0, n, src_hbm, buf_ref.at[0], sem_ref.at[0])

    start_next = functools.partial(
        _start_row_gather, idx_ref, jnp.minimum(i + 1, last) * n, n, src_hbm, buf_ref.at[1 - slot],
        sem_ref.at[1 - slot])
    if issue_first:
        start_next()
    _wait_row_gather(src_hbm, buf_ref.at[slot], sem_ref.at[slot])
    rows = _load_token_rows(buf_ref.at[slot], n)
    if issue_first is None:
        return rows, start_next
    if not issue_first:
        start_next()
    return rows


def _drain_row_gather(src_hbm, buf_ref, sem_ref):
    i = pl.program_id(0)

    @pl.when(i == pl.num_programs(0) - 1)
    def _():
        _wait_row_gather(src_hbm, buf_ref.at[1 - i % 2], sem_ref.at[1 - i % 2])


def _ada_kernel(c_ref, w_ref, b_ref, o_ref):
    c = c_ref[...]
    cond = c * _sigmoid(c)
    o_ref[0] = jnp.dot(cond, w_ref[0], preferred_element_type=F32,
                       precision=lax.Precision.HIGHEST) + b_ref[0]


def _ada(c, w_ada, b_ada):
    depth, d, n = w_ada.shape
    tn = 1536
    c8 = jnp.zeros((SUBLANES, d), F32).at[:BATCH].set(c)
    out = pl.pallas_call(
        _ada_kernel,
        out_shape=jax.ShapeDtypeStruct((depth, SUBLANES, n), F32),
        grid=(depth, n // tn),
        in_specs=[
            pl.BlockSpec((SUBLANES, d), lambda l, j: (0, 0)),
            pl.BlockSpec((1, d, tn), lambda l, j: (l, 0, j)),
            pl.BlockSpec((1, 1, tn), lambda l, j: (l, 0, j)),
        ],
        out_specs=pl.BlockSpec((1, SUBLANES, tn), lambda l, j: (l, 0, j)),
        compiler_params=_params("arbitrary", "arbitrary"),
        name="ada_mod",
    )(c8, w_ada, b_ada.reshape(depth, 1, n))
    return out[:, :BATCH].reshape(depth, BATCH, 6, d)


def _mlstm_proj_kernel(x_ref, mod_ref, g_ref, w_ref, bg_ref, q_ref, k_ref, v_ref, og_ref,
                       gcol_ref, grow_ref):
    x = x_ref[...]
    m = mod_ref[0]
    u = (_rms(x) * g_ref[...] * (1.0 + m[1:2]) + m[0:1]).astype(BF16)
    q_ref[...] = (_dot(u, w_ref[:, 0:QK]) * (DH_QK ** -0.5)).astype(BF16)
    k_ref[...] = _dot(u, w_ref[:, QK:2 * QK]).astype(BF16)
    v_ref[...] = _dot(u, w_ref[:, 2 * QK:2 * QK + D_MODEL]).astype(BF16)
    og_ref[...] = _dot(u, w_ref[:, 2 * QK + D_MODEL:2 * QK + 2 * D_MODEL]).astype(BF16)
    gt = _dot(u, w_ref[:, 2 * QK + 2 * D_MODEL:]) + bg_ref[...]
    lane = lax.broadcasted_iota(jnp.int32, gt.shape, 1)
    pos = lax.broadcasted_iota(jnp.int32, gt.shape, 0) % CHUNK
    log_f = jnp.minimum(gt, 0.0) - jnp.log(1.0 + jnp.exp(-jnp.abs(gt)))
    b = log_f
    shift = 1
    while shift < CHUNK:
        b = b + jnp.where(pos >= shift, pltpu.roll(b, shift, axis=0), 0.0)
        shift *= 2
    imb = gt - pltpu.roll(b, LANES - N_HEADS, axis=1)
    cm = imb
    shift = 1
    while shift < CHUNK:
        cm = jnp.maximum(cm, jnp.where(pos >= shift, pltpu.roll(cm, shift, axis=0), -jnp.inf))
        shift *= 2
    cols = jnp.where(lane < N_HEADS, cm,
                     jnp.where(lane < 2 * N_HEADS, b,
                               jnp.where(lane < 3 * N_HEADS, pltpu.roll(imb, 2 * N_HEADS, axis=1), 0.0)))
    gcol_ref[...] = cols[:, 0:GATE_WIDTH]
    grow_ref[0] = imb.T[0:SUBLANES, :]


def _mlstm_proj(x, mod, g, w, bg):
    t, d = x.shape
    n = w.shape[1]
    tm = TM_PROJ
    tiles_per_batch = SEQ // tm
    return pl.pallas_call(
        _mlstm_proj_kernel,
        out_shape=(
            jax.ShapeDtypeStruct((t, QK), BF16),
            jax.ShapeDtypeStruct((t, QK), BF16),
            jax.ShapeDtypeStruct((t, D_MODEL), BF16),
            jax.ShapeDtypeStruct((t, D_MODEL), BF16),
            jax.ShapeDtypeStruct((t, GATE_WIDTH), F32),
            jax.ShapeDtypeStruct((BATCH, SUBLANES, SEQ), F32),
        ),
        grid=(t // tm,),
        in_specs=[
            pl.BlockSpec((tm, d), lambda i, *_: (i, 0)),
            pl.BlockSpec((1, 6, d), lambda i, *_: (i // tiles_per_batch, 0, 0)),
            pl.BlockSpec((1, d), lambda i, *_: (0, 0)),
            pl.BlockSpec((d, n), lambda i, *_: (0, 0)),
            pl.BlockSpec((1, GATE_COLS), lambda i, *_: (0, 0)),
        ],
        out_specs=(
            pl.BlockSpec((tm, QK), lambda i, *_: (i, 0)),
            pl.BlockSpec((tm, QK), lambda i, *_: (i, 0)),
            pl.BlockSpec((tm, D_MODEL), lambda i, *_: (i, 0)),
            pl.BlockSpec((tm, D_MODEL), lambda i, *_: (i, 0)),
            pl.BlockSpec((tm, GATE_WIDTH), lambda i, *_: (i, 0)),
            pl.BlockSpec((1, SUBLANES, tm),
                         lambda i, *_: (i // tiles_per_batch, 0, i % tiles_per_batch)),
        ),
        compiler_params=_params("arbitrary"),
        name="mlstm_proj",
    )(x, mod, g, w, bg)


def _mlstm_core_kernel(q_ref, k_ref, v_ref, og_ref, gcol_ref, grow_ref, ng_ref, rep_ref, o_ref,
                       m_ref, *c_refs):
    @pl.when(pl.program_id(0) == 0)
    def _():
        m_ref[...] = jnp.zeros_like(m_ref)
        for c_ref in c_refs:
            c_ref[...] = jnp.zeros_like(c_ref)

    ln = CHUNK
    row = lax.broadcasted_iota(jnp.int32, (ln, ln), 0)
    col = lax.broadcasted_iota(jnp.int32, (ln, ln), 1)
    causal = col <= row
    ones = jnp.ones((ln, LANES), BF16)
    n_streams = BATCH * N_HEADS
    m_all = [m_ref[st] for st in range(n_streams)]
    m_out = [None] * n_streams
    for bi in range(BATCH):
        g = gcol_ref[bi]
        g_hi = g.astype(BF16)
        g_r1 = g - g_hi.astype(F32)
        g_mid = g_r1.astype(BF16)
        g_lo = (g_r1 - g_mid.astype(F32)).astype(BF16)
        rep = _dot(g_hi, rep_ref[...]) + _dot(g_mid, rep_ref[...]) + _dot(g_lo, rep_ref[...])
        for h in range(N_HEADS):
            st = bi * N_HEADS + h
            lanes = lambda kind: slice((kind * N_HEADS + h) * LANES, (kind * N_HEADS + h + 1) * LANES)
            cm, b, imb_col = rep[:, lanes(0)], rep[:, lanes(1)], rep[:, lanes(2)]
            imb_row = grow_ref[bi, h:h + 1, :]
            qh = q_ref[bi, :, h * DH_QK:(h + 1) * DH_QK]
            kh = k_ref[bi, :, h * DH_QK:(h + 1) * DH_QK]
            v_ext = jnp.concatenate([v_ref[bi, :, h * DH_V:(h + 1) * DH_V], ones], axis=1)
            c_ref = c_refs[st]
            c_prev = c_ref[...]
            m_prev = m_all[st]

            big_m = jnp.maximum(m_prev, cm)
            d_mat = jnp.exp(jnp.where(causal, imb_row - big_m, -jnp.inf))
            s = lax.dot_general(qh, kh, (((1,), (1,)), ((), ())), preferred_element_type=F32) * d_mat
            q_inter = (qh.astype(F32) * jnp.exp(m_prev - big_m)).astype(BF16)
            lhs = jnp.concatenate([q_inter, s.astype(BF16)], axis=1)
            rhs = jnp.concatenate([c_prev.astype(BF16), v_ext], axis=0)
            nd = _dot(lhs, rhs)
            den = nd[:, DH_V:]
            inv = 1.0 / jnp.maximum(jnp.abs(den), jnp.exp(-(b + big_m)))
            hh = nd[:, :DH_V] * jnp.concatenate([inv, inv], axis=1)

            m_last = big_m[ln - 1:ln, :]
            kw = (kh.astype(F32) * jnp.exp(imb_col - m_last)).astype(BF16)
            decay = jnp.exp(m_prev - m_last)
            c_ref[...] = jnp.concatenate([decay] * 3, axis=1) * c_prev + lax.dot_general(
                kw, v_ext, (((0,), (0,)), ((), ())), preferred_element_type=F32)
            m_out[st] = b[ln - 1:ln, :] + m_last

            sl = slice(h * DH_V, (h + 1) * DH_V)
            gate = _sigmoid(og_ref[bi, :, sl].astype(F32))
            o_ref[bi, :, sl] = (_rms(hh) * ng_ref[:, sl] * gate).astype(BF16)
    for st in range(n_streams):
        m_ref[st] = m_out[st]


def _mlstm_core(q, k, v, og, gcol, grow, norm_g):
    nc = SEQ // CHUNK
    n_streams = BATCH * N_HEADS
    per_batch = lambda a: a.reshape(BATCH, SEQ, a.shape[-1])
    chunk_rows = lambda width: pl.BlockSpec((BATCH, CHUNK, width), lambda c: (0, c, 0))
    n_rep = GATE_KINDS * N_HEADS
    replicate = (jnp.arange(GATE_WIDTH)[:, None] == jnp.arange(n_rep * LANES)[None, :] // LANES
                 ).astype(BF16)
    out = pl.pallas_call(
        _mlstm_core_kernel,
        out_shape=jax.ShapeDtypeStruct((BATCH, SEQ, D_MODEL), BF16),
        grid=(nc,),
        in_specs=[
            chunk_rows(QK),
            chunk_rows(QK),
            chunk_rows(D_MODEL),
            chunk_rows(D_MODEL),
            chunk_rows(GATE_WIDTH),
            pl.BlockSpec((BATCH, SUBLANES, CHUNK), lambda c: (0, 0, c)),
            pl.BlockSpec((1, D_MODEL), lambda c: (0, 0)),
            pl.BlockSpec((GATE_WIDTH, n_rep * LANES), lambda c: (0, 0)),
        ],
        out_specs=chunk_rows(D_MODEL),
        scratch_shapes=[pltpu.VMEM((n_streams, 1, LANES), F32)]
        + [pltpu.VMEM((DH_QK, DH_V + LANES), F32) for _ in range(n_streams)],
        compiler_params=_params("arbitrary"),
        name="mlstm_core",
    )(per_batch(q), per_batch(k), per_batch(v), per_batch(og), per_batch(gcol), grow, norm_g,
      replicate)
    return out.reshape(TOKENS, D_MODEL)


def _top2_sum(v0, v1, v2, v3):
    hi1, lo1 = jnp.maximum(v0, v1), jnp.minimum(v0, v1)
    hi2, lo2 = jnp.maximum(v2, v3), jnp.minimum(v2, v3)
    return jnp.maximum(hi1, hi2) + jnp.maximum(jnp.minimum(hi1, hi2), jnp.maximum(lo1, lo2))


def _route_tail(x_new, m, g2_ref, wr_ref, br_ref, tri_ref, u2_ref, cls_ref, rank_ref, cnt_ref,
                run_ref):
    @pl.when(pl.program_id(0) == 0)
    def _():
        run_ref[...] = jnp.zeros_like(run_ref)

    u2 = _rms(x_new) * g2_ref[...] * (1.0 + m[4:5]) + m[3:4]
    _store_token_rows(u2_ref, u2, u2.shape[0])
    logits = _dot(u2.astype(BF16), wr_ref[...])
    lt = logits.T[0:N_EXPERTS, :]
    e = jnp.exp(lt - jnp.max(lt, axis=0, keepdims=True))
    probs = e / jnp.sum(e, axis=0, keepdims=True)
    sel = probs + br_ref[...]
    sel_rows = [sel[j:j + 1, :] for j in range(N_EXPERTS)]
    best = jnp.zeros_like(sel_rows[0], dtype=jnp.int32)
    best_score = _top2_sum(*sel_rows[0:EXPERTS_PER_GROUP])
    for g in range(1, N_GROUPS):
        score = _top2_sum(*sel_rows[g * EXPERTS_PER_GROUP:(g + 1) * EXPERTS_PER_GROUP])
        better = score > best_score
        best = jnp.where(better, g, best)
        best_score = jnp.where(better, score, best_score)
    s = []
    for j in range(EXPERTS_PER_GROUP):
        sj = sel_rows[j]
        for g in range(1, N_GROUPS):
            sj = jnp.where(best == g, sel_rows[g * EXPERTS_PER_GROUP + j], sj)
        s.append(sj)
    chosen = []
    for j in range(EXPERTS_PER_GROUP):
        beaten = jnp.zeros_like(best)
        for i in range(EXPERTS_PER_GROUP):
            if i == j:
                continue
            wins = (s[i] >= s[j]) if i < j else (s[i] > s[j])
            beaten = beaten + jnp.where(wins, 1, 0)
        chosen.append(beaten < 2)
    pair = jnp.full_like(best, len(PAIRS) - 1)
    for p in range(len(PAIRS) - 2, -1, -1):
        a, b = PAIRS[p]
        pair = jnp.where(jnp.logical_and(chosen[a], chosen[b]), p, pair)
    cls = best * len(PAIRS) + pair
    cls_ref[0] = cls

    class_id = lax.broadcasted_iota(jnp.int32, (CLASS_ROWS, cls.shape[1]), 0)
    onehot = class_id == cls
    before = _dot(jnp.where(onehot, 1.0, 0.0).astype(BF16), tri_ref[...])
    run = run_ref[...]
    rank = jnp.sum(jnp.where(onehot, before + run[:, 0:1], 0.0), axis=0, keepdims=True)
    rank_ref[0] = rank.astype(jnp.int32)
    run = run + jnp.sum(jnp.where(onehot, 1.0, 0.0), axis=1, keepdims=True)
    run_ref[...] = run
    cnt_ref[...] = run.astype(jnp.int32)


def _route_out_shapes(t, tm):
    return (
        jax.ShapeDtypeStruct((t, D_MODEL), F32),
        jax.ShapeDtypeStruct((t * ROW_TILE, LANES), F32),
        jax.ShapeDtypeStruct((t // tm, 1, tm), jnp.int32),
        jax.ShapeDtypeStruct((t // tm, 1, tm), jnp.int32),
        jax.ShapeDtypeStruct((CLASS_ROWS, LANES), jnp.int32),
    )


def _route_out_specs(tm):
    return (
        pl.BlockSpec((tm, D_MODEL), lambda i, *_: (i, 0)),
        pl.BlockSpec((tm * ROW_TILE, LANES), lambda i, *_: (i, 0)),
        pl.BlockSpec((1, 1, tm), lambda i, *_: (i, 0, 0)),
        pl.BlockSpec((1, 1, tm), lambda i, *_: (i, 0, 0)),
        pl.BlockSpec((CLASS_ROWS, LANES), lambda i, *_: (0, 0)),
    )


def _const_spec(shape):
    return pl.BlockSpec(shape, lambda i, *_: (0,) * len(shape))


def _mlstm_out_kernel(h_ref, x_ref, mod_ref, wo_ref, g2_ref, wr_ref, br_ref, tri_ref,
                      x1_ref, u2_ref, cls_ref, rank_ref, cnt_ref, run_ref):
    m = mod_ref[0]
    x1 = x_ref[...] + m[2:3] * _dot(h_ref[...], wo_ref[...])
    x1_ref[...] = x1
    _route_tail(x1, m, g2_ref, wr_ref, br_ref, tri_ref, u2_ref, cls_ref, rank_ref, cnt_ref, run_ref)


def _mlstm_out(hg, x, mod, w_out, g2, wr, br, tri):
    t = x.shape[0]
    tm = TM_MIX
    tiles_per_batch = SEQ // tm
    return pl.pallas_call(
        _mlstm_out_kernel,
        out_shape=_route_out_shapes(t, tm),
        grid=(t // tm,),
        in_specs=[
            pl.BlockSpec((tm, D_MODEL), lambda i, *_: (i, 0)),
            pl.BlockSpec((tm, D_MODEL), lambda i, *_: (i, 0)),
            pl.BlockSpec((1, 6, D_MODEL), lambda i, *_: (i // tiles_per_batch, 0, 0)),
            _const_spec((D_MODEL, D_MODEL)),
            _const_spec((1, D_MODEL)),
            _const_spec((D_MODEL, ROUTER_COLS)),
            _const_spec((N_EXPERTS, 1)),
            _const_spec((tm, tm)),
        ],
        out_specs=_route_out_specs(tm),
        scratch_shapes=[pltpu.VMEM((CLASS_ROWS, LANES), F32)],
        compiler_params=_params("arbitrary"),
        name="mlstm_out_route",
    )(hg, x, mod, w_out, g2, wr, br, tri)


def _conv_layer_kernel(dest_ref, x_ref, ys_ref, mod0_ref, mod_ref, g1_ref, wi_ref, cw_ref, cb_ref,
                       wo_ref, g2_ref, wr_ref, br_ref, tri_ref,
                       x3_ref, u2_ref, cls_ref, rank_ref, cnt_ref,
                       run_ref, carry_ref, ybuf_ref, ysem_ref):
    tm = x_ref.shape[0]
    tiles_per_batch = SEQ // tm

    @pl.when(pl.program_id(0) % tiles_per_batch == 0)
    def _():
        carry_ref[...] = jnp.zeros_like(carry_ref)

    m = mod_ref[0]
    y = _gathered_tile(dest_ref, ys_ref, ybuf_ref, ysem_ref, tm, issue_first=False)
    x2 = x_ref[...] + mod0_ref[0][5:6] * y
    u = (_rms(x2) * g1_ref[...] * (1.0 + m[1:2]) + m[0:1]).astype(BF16)
    bgate = _dot(u, wi_ref[:, 0:D_MODEL])
    z = _dot(u, wi_ref[:, D_MODEL:2 * D_MODEL]) * _dot(u, wi_ref[:, 2 * D_MODEL:])
    prev = carry_ref[...]
    row = lax.broadcasted_iota(jnp.int32, z.shape, 0)
    z1 = jnp.where(row == 0, prev[7:8], pltpu.roll(z, 1, axis=0))
    z2 = jnp.where(row == 0, prev[6:7], jnp.where(row == 1, prev[7:8], pltpu.roll(z, 2, axis=0)))
    carry_ref[...] = z[tm - SUBLANES:, :]
    cw = cw_ref[...]
    zc = cw[0:1] * z2 + cw[1:2] * z1 + cw[2:3] * z + cb_ref[...]
    x3 = x2 + m[2:3] * _dot((bgate * zc).astype(BF16), wo_ref[...])
    x3_ref[...] = x3
    _route_tail(x3, m, g2_ref, wr_ref, br_ref, tri_ref, u2_ref, cls_ref, rank_ref, cnt_ref, run_ref)
    _drain_row_gather(ys_ref, ybuf_ref, ysem_ref)


def _conv_layer(dest, x, ys, mod0, mod, g1, w_in, conv_w, conv_b, w_out, g2, wr, br, tri):
    t = x.shape[0]
    tm = TM_MIX
    tiles_per_batch = SEQ // tm
    mod_spec = pl.BlockSpec((1, 6, D_MODEL), lambda i, *_: (i // tiles_per_batch, 0, 0))
    return pl.pallas_call(
        _conv_layer_kernel,
        out_shape=_route_out_shapes(t, tm),
        grid_spec=pltpu.PrefetchScalarGridSpec(
            num_scalar_prefetch=1,
            grid=(t // tm,),
            in_specs=[
                pl.BlockSpec((tm, D_MODEL), lambda i, *_: (i, 0)),
                pl.BlockSpec(memory_space=pl.ANY),
                mod_spec,
                mod_spec,
                _const_spec((1, D_MODEL)),
                _const_spec((D_MODEL, 3 * D_MODEL)),
                _const_spec((3, D_MODEL)),
                _const_spec((1, D_MODEL)),
                _const_spec((D_MODEL, D_MODEL)),
                _const_spec((1, D_MODEL)),
                _const_spec((D_MODEL, ROUTER_COLS)),
                _const_spec((N_EXPERTS, 1)),
                _const_spec((tm, tm)),
            ],
            out_specs=_route_out_specs(tm),
            scratch_shapes=[
                pltpu.VMEM((CLASS_ROWS, LANES), F32),
                pltpu.VMEM((SUBLANES, D_MODEL), F32),
                pltpu.VMEM((2, tm * ROW_TILE, LANES), F32),
                pltpu.SemaphoreType.DMA((2,)),
            ],
        ),
        compiler_params=_params("arbitrary"),
        name="conv_layer_route",
    )(dest, x, ys, mod0, mod, g1, w_in, conv_w, conv_b, w_out, g2, wr, br, tri)


def _invert_kernel(dest_ref, zeros_ref, src_ref, sem):
    fill = pltpu.make_async_copy(zeros_ref, src_ref, sem)
    fill.start()
    fill.wait()
    unroll = 8

    def body(t8, carry):
        for k in range(unroll):
            t = t8 * unroll + k
            src_ref[dest_ref[t]] = t
        return carry

    lax.fori_loop(0, dest_ref.shape[0] // unroll, body, 0)


def _invert_permutation(dest, n_out):
    smem = pl.BlockSpec(memory_space=pltpu.SMEM)
    return pl.pallas_call(
        _invert_kernel,
        out_shape=jax.ShapeDtypeStruct((n_out,), jnp.int32),
        in_specs=[smem, pl.BlockSpec(memory_space=pl.ANY)],
        out_specs=smem,
        scratch_shapes=[pltpu.SemaphoreType.DMA(())],
        name="invert_permutation",
    )(dest, jnp.zeros((n_out,), jnp.int32))


def _expert_kernel(ea_ref, eb_ref, chg_ref, nused_ref, src_ref, u2_ref, wga_ref, wua_ref, wda_ref,
                   wgb_ref, wub_ref, wdb_ref, wra_ref, wrb_ref, y_ref, xbuf_ref, xsem_ref, *wbf_refs):
    del ea_ref, eb_ref
    j = pl.program_id(0)
    tm = TM_EXPERT

    @pl.when(chg_ref[j] == 1)
    def _():
        for src, dst in zip((wga_ref, wua_ref, wda_ref, wgb_ref, wub_ref, wdb_ref), wbf_refs):
            dst[...] = src[0, 0].astype(BF16)

    rows, start_next = _gathered_tile(src_ref, u2_ref, xbuf_ref, xsem_ref, tm, issue_first=None)

    @pl.when(j < nused_ref[0])
    def _():
        start_next()
        xb = rows.astype(BF16)
        dl = jnp.sum(xb.astype(F32) * (wra_ref[0] - wrb_ref[0]), axis=-1, keepdims=True)
        w_a = _sigmoid(dl)
        w_b = _sigmoid(-dl)

        def ffn(wg_ref, wu_ref, wd_ref):
            gate = _dot(xb, wg_ref[...])
            hidden = gate * _sigmoid(gate) * _dot(xb, wu_ref[...])
            return _dot(hidden.astype(BF16), wd_ref[...])

        y = w_a * ffn(*wbf_refs[0:3]) + w_b * ffn(*wbf_refs[3:6])
        _store_token_rows(y_ref, y, tm)

    @pl.when(j >= nused_ref[0])
    def _():
        start_next()
        y_ref[...] = jnp.zeros_like(y_ref)

    _drain_row_gather(u2_ref, xbuf_ref, xsem_ref)


def _experts(layer, ea, eb, chg, n_used, src, u2, w_gate, w_up, w_down, wr_rows):
    tm = TM_EXPERT
    sel_a = lambda j, ea, eb, *_: (layer, ea[j], 0, 0)
    sel_b = lambda j, ea, eb, *_: (layer, eb[j], 0, 0)
    up_spec = lambda sel: pl.BlockSpec((1, 1, D_MODEL, D_EXPERT), sel)
    down_spec = lambda sel: pl.BlockSpec((1, 1, D_EXPERT, D_MODEL), sel)
    wr_spec = lambda sel: pl.BlockSpec((1, 1, D_MODEL), lambda *a: sel(*a)[1:])
    up_scratch = pltpu.VMEM((D_MODEL, D_EXPERT), BF16)
    down_scratch = pltpu.VMEM((D_EXPERT, D_MODEL), BF16)
    return pl.pallas_call(
        _expert_kernel,
        out_shape=jax.ShapeDtypeStruct((PADDED_ROWS * ROW_TILE, LANES), F32),
        grid_spec=pltpu.PrefetchScalarGridSpec(
            num_scalar_prefetch=5,
            grid=(N_EXPERT_TILES,),
            in_specs=[
                pl.BlockSpec(memory_space=pl.ANY),
                up_spec(sel_a), up_spec(sel_a), down_spec(sel_a),
                up_spec(sel_b), up_spec(sel_b), down_spec(sel_b),
                wr_spec(sel_a), wr_spec(sel_b),
            ],
            out_specs=pl.BlockSpec((tm * ROW_TILE, LANES), lambda j, *_: (j, 0)),
            scratch_shapes=[pltpu.VMEM((2, tm * ROW_TILE, LANES), F32), pltpu.SemaphoreType.DMA((2,)),
                            up_scratch, up_scratch, down_scratch, up_scratch, up_scratch, down_scratch],
        ),
        compiler_params=_params("arbitrary"),
        name="grouped_experts",
    )(ea, eb, chg, n_used, src, u2, w_gate, w_up, w_down, w_gate, w_up, w_down, wr_rows, wr_rows)


def _final_kernel(dest_ref, x_ref, ys_ref, mod_ref, g_ref, o_ref, ybuf_ref, ysem_ref):
    y = _gathered_tile(dest_ref, ys_ref, ybuf_ref, ysem_ref, x_ref.shape[0], issue_first=True)
    x = x_ref[...] + mod_ref[0][5:6] * y
    o_ref[...] = _rms(x) * g_ref[...]
    _drain_row_gather(ys_ref, ybuf_ref, ysem_ref)


def _final(dest, x, ys, mod, g):
    t = x.shape[0]
    tm = TM_FINAL
    tiles_per_batch = SEQ // tm
    return pl.pallas_call(
        _final_kernel,
        out_shape=jax.ShapeDtypeStruct((t, D_MODEL), F32),
        grid_spec=pltpu.PrefetchScalarGridSpec(
            num_scalar_prefetch=1,
            grid=(t // tm,),
            in_specs=[
                pl.BlockSpec((tm, D_MODEL), lambda i, *_: (i, 0)),
                pl.BlockSpec(memory_space=pl.ANY),
                pl.BlockSpec((1, 6, D_MODEL), lambda i, *_: (i // tiles_per_batch, 0, 0)),
                _const_spec((1, D_MODEL)),
            ],
            out_specs=pl.BlockSpec((tm, D_MODEL), lambda i, *_: (i, 0)),
            scratch_shapes=[pltpu.VMEM((2, tm * ROW_TILE, LANES), F32), pltpu.SemaphoreType.DMA((2,))],
        ),
        compiler_params=_params("arbitrary"),
        name="final_norm",
    )(dest, x, ys, mod, g)


_PAIR_A = np.array([EXPERTS_PER_GROUP * (c // len(PAIRS)) + PAIRS[c % len(PAIRS)][0]
                    for c in range(N_CLASSES)], np.int32)
_PAIR_B = np.array([EXPERTS_PER_GROUP * (c // len(PAIRS)) + PAIRS[c % len(PAIRS)][1]
                    for c in range(N_CLASSES)], np.int32)


def _moe(layer, u2, cls, rank, cnt, w_gate, w_up, w_down, wr_rows):
    counts = cnt[:N_CLASSES, 0]
    tiles = (counts + TM_EXPERT - 1) // TM_EXPERT
    tile_end = jnp.cumsum(tiles)
    row_start = (tile_end - tiles) * TM_EXPERT
    n_used = tile_end[-1:]
    cls = cls.reshape(-1)
    dest = row_start[cls] + rank.reshape(-1)
    tile_cls = jnp.sum(jnp.arange(N_EXPERT_TILES)[:, None] >= tile_end[None, :], axis=1)
    tile_cls = jnp.minimum(tile_cls, tile_cls[jnp.maximum(n_used[0] - 1, 0)])
    ea = jnp.asarray(_PAIR_A)[tile_cls]
    eb = jnp.asarray(_PAIR_B)[tile_cls]
    chg = jnp.concatenate([jnp.ones((1,), jnp.int32),
                           (tile_cls[1:] != tile_cls[:-1]).astype(jnp.int32)])
    src = _invert_permutation(dest, PADDED_ROWS)
    ys = _experts(layer, ea, eb, chg, n_used.astype(jnp.int32), src, u2, w_gate, w_up, w_down, wr_rows)
    return dest, ys


def kernel(x, c, norm1_g, norm2_g, w_ada, b_ada, m_w_in, m_b_gates, m_norm_g, m_w_out,
           c_w_in, c_conv_w, c_conv_b, c_w_out, w_router, b_router,
           e_w_gate, e_w_up, e_w_down, final_g):
    xf = x.reshape(TOKENS, D_MODEL)
    mod = _ada(c, w_ada, b_ada)

    w_in = jnp.concatenate(
        [m_w_in[0].astype(BF16),
         jnp.zeros((D_MODEL, GATE_COLS - 2 * N_HEADS), BF16)], axis=1)
    bg = jnp.zeros((1, GATE_COLS), F32).at[0, :2 * N_HEADS].set(m_b_gates[0])
    wr = jnp.zeros((D_MODEL, ROUTER_COLS), BF16).at[:, :N_EXPERTS].set(w_router.astype(BF16))
    wr_rows = w_router.astype(BF16).astype(F32).T.reshape(N_EXPERTS, 1, D_MODEL)
    br = b_router.reshape(N_EXPERTS, 1)
    tri = (jnp.arange(TM_MIX)[:, None] < jnp.arange(TM_MIX)[None, :]).astype(BF16)
    row = lambda v: v.reshape(1, -1)

    q, k, v, og, gcol, grow = _mlstm_proj(xf, mod[0], row(norm1_g[0]), w_in, bg)
    hg = _mlstm_core(q, k, v, og, gcol, grow, row(m_norm_g[0]))
    x1, u2, cls, rank, cnt = _mlstm_out(hg, xf, mod[0], m_w_out[0].astype(BF16), row(norm2_g[0]),
                                        wr, br, tri)
    dest, ys = _moe(0, u2, cls, rank, cnt, e_w_gate, e_w_up, e_w_down, wr_rows)

    x3, u2, cls, rank, cnt = _conv_layer(
        dest, x1, ys, mod[0], mod[1], row(norm1_g[1]), c_w_in[0].astype(BF16), c_conv_w[0],
        row(c_conv_b[0]), c_w_out[0].astype(BF16), row(norm2_g[1]), wr, br, tri)
    dest, ys = _moe(1, u2, cls, rank, cnt, e_w_gate, e_w_up, e_w_down, wr_rows)

    out = _final(dest, x3, ys, mod[1], row(final_g))
    return out.reshape(BATCH, SEQ, D_MODEL)
```

```python
import functools

import jax
import jax.numpy as jnp
import numpy as np
from jax import lax
from jax.experimental import pallas as pl
from jax.experimental.pallas import tpu as pltpu

F32 = jnp.float32
BF16 = jnp.bfloat16

D_MODEL = 1024
BATCH = 4
SEQ = 8192
TOKENS = BATCH * SEQ
N_HEADS = 4
DH_V = 256
DH_QK = 128
QK = N_HEADS * DH_QK
N_EXPERTS = 16
N_GROUPS = 4
EXPERTS_PER_GROUP = 4
D_EXPERT = 512
EPS = 1e-6

LANES = 128
SUBLANES = 8
VMEM_LIMIT_BYTES = 56 * 1024 * 1024

CHUNK = 128
TM_PROJ = 512
TM_MIX = 512
TM_EXPERT = 256
GATE_COLS = LANES
GATE_WIDTH = 16
GATE_KINDS = 3
ROUTER_COLS = LANES
PAIRS = ((0, 1), (0, 2), (0, 3), (1, 2), (1, 3), (2, 3))
N_CLASSES = N_GROUPS * len(PAIRS)
CLASS_ROWS = 32
N_EXPERT_TILES = TOKENS // TM_EXPERT + N_CLASSES
PADDED_ROWS = N_EXPERT_TILES * TM_EXPERT
TM_FINAL = 512
EXPERT_GATHER_BUFFERS = 3
STREAM_GROUP = 4


def _params(*semantics):
    return pltpu.CompilerParams(dimension_semantics=semantics, vmem_limit_bytes=VMEM_LIMIT_BYTES)


def _dot(a, b):
    return jnp.dot(a, b, preferred_element_type=F32)


def _rms(x):
    return x * lax.rsqrt(jnp.mean(x * x, axis=-1, keepdims=True) + EPS)


def _sigmoid(x):
    return 1.0 / (1.0 + jnp.exp(-x))


ROW_TILE = D_MODEL // LANES


def _load_token_rows(ref, n):
    return jnp.concatenate([ref[pl.ds(c, n, stride=ROW_TILE), :] for c in range(ROW_TILE)], axis=1)


def _store_token_rows(ref, val, n):
    for c in range(ROW_TILE):
        ref[pl.ds(c, n, stride=ROW_TILE), :] = val[:, c * LANES:(c + 1) * LANES]


def _token_tile(ref, t):
    return ref.at[pl.ds(pl.multiple_of(t * ROW_TILE, ROW_TILE), ROW_TILE)]


def _start_row_gather(idx_ref, base, n, src_hbm, buf, sem):
    for r in range(n):
        pltpu.make_async_copy(_token_tile(src_hbm, idx_ref[base + r]), _token_tile(buf, r),
                              sem).start(priority=r % 2)


def _wait_row_gather(src_hbm, buf, sem):
    pltpu.make_async_copy(src_hbm.at[pl.ds(0, buf.shape[0])], buf, sem).wait()


def _gathered_tile(idx_ref, src_hbm, buf_ref, sem_ref, n, issue_first):
    i = pl.program_id(0)
    last = pl.num_programs(0) - 1
    nb = buf_ref.shape[0]
    ahead = nb - 1

    @pl.when(i == 0)
    def _():
        for k in range(ahead):
            _start_row_gather(idx_ref, jnp.minimum(k, last) * n, n, src_hbm, buf_ref.at[k], sem_ref.at[k])

    nxt = (i + ahead) % nb
    start_next = functools.partial(
        _start_row_gather, idx_ref, jnp.minimum(i + ahead, last) * n, n, src_hbm, buf_ref.at[nxt],
        sem_ref.at[nxt])
    if issue_first:
        start_next()
    slot = i % nb
    _wait_row_gather(src_hbm, buf_ref.at[slot], sem_ref.at[slot])
    rows = _load_token_rows(buf_ref.at[slot], n)
    if issue_first is None:
        return rows, start_next
    if not issue_first:
        start_next()
    return rows


def _drain_row_gather(src_hbm, buf_ref, sem_ref):
    i = pl.program_id(0)
    nb = buf_ref.shape[0]

    @pl.when(i == pl.num_programs(0) - 1)
    def _():
        for k in range(1, nb):
            _wait_row_gather(src_hbm, buf_ref.at[(i + k) % nb], sem_ref.at[(i + k) % nb])


def _ada_kernel(c_ref, w_ref, b_ref, o_ref):
    c = c_ref[...]
    cond = c * _sigmoid(c)
    o_ref[0] = jnp.dot(cond, w_ref[0], preferred_element_type=F32,
                       precision=lax.Precision.HIGHEST) + b_ref[0]


def _ada(c, w_ada, b_ada):
    depth, d, n = w_ada.shape
    tn = 1536
    c8 = jnp.zeros((SUBLANES, d), F32).at[:BATCH].set(c)
    out = pl.pallas_call(
        _ada_kernel,
        out_shape=jax.ShapeDtypeStruct((depth, SUBLANES, n), F32),
        grid=(depth, n // tn),
        in_specs=[
            pl.BlockSpec((SUBLANES, d), lambda l, j: (0, 0)),
            pl.BlockSpec((1, d, tn), lambda l, j: (l, 0, j)),
            pl.BlockSpec((1, 1, tn), lambda l, j: (l, 0, j)),
        ],
        out_specs=pl.BlockSpec((1, SUBLANES, tn), lambda l, j: (l, 0, j)),
        compiler_params=_params("arbitrary", "arbitrary"),
        name="ada_mod",
    )(c8, w_ada, b_ada.reshape(depth, 1, n))
    return out[:, :BATCH].reshape(depth, BATCH, 6, d)


def _mlstm_proj_kernel(x_ref, mod_ref, g_ref, w_ref, bg_ref, q_ref, k_ref, v_ref, og_ref,
                       gcol_ref, grow_ref):
    x = x_ref[...]
    m = mod_ref[0]
    u = (_rms(x) * g_ref[...] * (1.0 + m[1:2]) + m[0:1]).astype(BF16)
    q_ref[...] = (_dot(u, w_ref[:, 0:QK]) * (DH_QK ** -0.5)).astype(BF16)
    k_ref[...] = _dot(u, w_ref[:, QK:2 * QK]).astype(BF16)
    v_ref[...] = _dot(u, w_ref[:, 2 * QK:2 * QK + D_MODEL]).astype(BF16)
    og_ref[...] = _dot(u, w_ref[:, 2 * QK + D_MODEL:2 * QK + 2 * D_MODEL]).astype(BF16)
    gt = _dot(u, w_ref[:, 2 * QK + 2 * D_MODEL:]) + bg_ref[...]
    lane = lax.broadcasted_iota(jnp.int32, gt.shape, 1)
    pos = lax.broadcasted_iota(jnp.int32, gt.shape, 0) % CHUNK
    log_f = jnp.minimum(gt, 0.0) - jnp.log(1.0 + jnp.exp(-jnp.abs(gt)))
    b = log_f
    shift = 1
    while shift < CHUNK:
        b = b + jnp.where(pos >= shift, pltpu.roll(b, shift, axis=0), 0.0)
        shift *= 2
    imb = gt - pltpu.roll(b, LANES - N_HEADS, axis=1)
    cm = imb
    shift = 1
    while shift < CHUNK:
        cm = jnp.maximum(cm, jnp.where(pos >= shift, pltpu.roll(cm, shift, axis=0), -jnp.inf))
        shift *= 2
    cols = jnp.where(lane < N_HEADS, cm,
                     jnp.where(lane < 2 * N_HEADS, b,
                               jnp.where(lane < 3 * N_HEADS, pltpu.roll(imb, 2 * N_HEADS, axis=1), 0.0)))
    gcol_ref[...] = cols[:, 0:GATE_WIDTH]
    grow_ref[0] = imb.T[0:SUBLANES, :]


def _mlstm_proj(x, mod, g, w, bg):
    t, d = x.shape
    n = w.shape[1]
    tm = TM_PROJ
    tiles_per_batch = SEQ // tm
    return pl.pallas_call(
        _mlstm_proj_kernel,
        out_shape=(
            jax.ShapeDtypeStruct((t, QK), BF16),
            jax.ShapeDtypeStruct((t, QK), BF16),
            jax.ShapeDtypeStruct((t, D_MODEL), BF16),
            jax.ShapeDtypeStruct((t, D_MODEL), BF16),
            jax.ShapeDtypeStruct((t, GATE_WIDTH), F32),
            jax.ShapeDtypeStruct((BATCH, SUBLANES, SEQ), F32),
        ),
        grid=(t // tm,),
        in_specs=[
            pl.BlockSpec((tm, d), lambda i, *_: (i, 0)),
            pl.BlockSpec((1, 6, d), lambda i, *_: (i // tiles_per_batch, 0, 0)),
            pl.BlockSpec((1, d), lambda i, *_: (0, 0)),
            pl.BlockSpec((d, n), lambda i, *_: (0, 0)),
            pl.BlockSpec((1, GATE_COLS), lambda i, *_: (0, 0)),
        ],
        out_specs=(
            pl.BlockSpec((tm, QK), lambda i, *_: (i, 0)),
            pl.BlockSpec((tm, QK), lambda i, *_: (i, 0)),
            pl.BlockSpec((tm, D_MODEL), lambda i, *_: (i, 0)),
            pl.BlockSpec((tm, D_MODEL), lambda i, *_: (i, 0)),
            pl.BlockSpec((tm, GATE_WIDTH), lambda i, *_: (i, 0)),
            pl.BlockSpec((1, SUBLANES, tm),
                         lambda i, *_: (i // tiles_per_batch, 0, i % tiles_per_batch)),
        ),
        compiler_params=_params("arbitrary"),
        name="mlstm_proj",
    )(x, mod, g, w, bg)


def _mlstm_core_kernel(q_ref, k_ref, v_ref, og_ref, gcol_ref, grow_ref, ng_ref, rep_ref, o_ref,
                       m_ref, *c_refs):
    @pl.when(pl.program_id(0) == 0)
    def _():
        m_ref[...] = jnp.zeros_like(m_ref)
        for c_ref in c_refs:
            c_ref[...] = jnp.zeros_like(c_ref)

    ln = CHUNK
    row = lax.broadcasted_iota(jnp.int32, (ln, ln), 0)
    col = lax.broadcasted_iota(jnp.int32, (ln, ln), 1)
    causal = col <= row
    ones = jnp.ones((ln, LANES), BF16)
    n_streams = BATCH * N_HEADS
    m_all = [m_ref[st] for st in range(n_streams)]
    m_out = [None] * n_streams
    for bi in range(BATCH):
        g = gcol_ref[bi]
        g_hi = g.astype(BF16)
        g_r1 = g - g_hi.astype(F32)
        g_mid = g_r1.astype(BF16)
        g_lo = (g_r1 - g_mid.astype(F32)).astype(BF16)
        rep = _dot(g_hi, rep_ref[...]) + _dot(g_mid, rep_ref[...]) + _dot(g_lo, rep_ref[...])
        for h in range(N_HEADS):
            st = bi * N_HEADS + h
            lanes = lambda kind: slice((kind * N_HEADS + h) * LANES, (kind * N_HEADS + h + 1) * LANES)
            cm, b, imb_col = rep[:, lanes(0)], rep[:, lanes(1)], rep[:, lanes(2)]
            imb_row = grow_ref[bi, h:h + 1, :]
            qh = q_ref[bi, :, h * DH_QK:(h + 1) * DH_QK]
            kh = k_ref[bi, :, h * DH_QK:(h + 1) * DH_QK]
            v_ext = jnp.concatenate([v_ref[bi, :, h * DH_V:(h + 1) * DH_V], ones], axis=1)
            c_ref = c_refs[st]
            c_prev = c_ref[...]
            m_prev = m_all[st]

            big_m = jnp.maximum(m_prev, cm)
            d_mat = jnp.exp(jnp.where(causal, imb_row - big_m, -jnp.inf))
            s = lax.dot_general(qh, kh, (((1,), (1,)), ((), ())), preferred_element_type=F32) * d_mat
            q_inter = (qh.astype(F32) * jnp.exp(m_prev - big_m)).astype(BF16)
            lhs = jnp.concatenate([q_inter, s.astype(BF16)], axis=1)
            rhs = jnp.concatenate([c_prev.astype(BF16), v_ext], axis=0)
            nd = _dot(lhs, rhs)
            den = nd[:, DH_V:]
            inv = 1.0 / jnp.maximum(jnp.abs(den), jnp.exp(-(b + big_m)))
            hh = nd[:, :DH_V] * jnp.concatenate([inv, inv], axis=1)

            m_last = big_m[ln - 1:ln, :]
            kw = (kh.astype(F32) * jnp.exp(imb_col - m_last)).astype(BF16)
            decay = jnp.exp(m_prev - m_last)
            c_ref[...] = jnp.concatenate([decay] * 3, axis=1) * c_prev + lax.dot_general(
                kw, v_ext, (((0,), (0,)), ((), ())), preferred_element_type=F32)
            m_out[st] = b[ln - 1:ln, :] + m_last

            sl = slice(h * DH_V, (h + 1) * DH_V)
            gate = _sigmoid(og_ref[bi, :, sl].astype(F32))
            o_ref[bi, :, sl] = (_rms(hh) * ng_ref[:, sl] * gate).astype(BF16)
    for st in range(n_streams):
        m_ref[st] = m_out[st]


def _mlstm_core(q, k, v, og, gcol, grow, norm_g):
    nc = SEQ // CHUNK
    n_streams = BATCH * N_HEADS
    per_batch = lambda a: a.reshape(BATCH, SEQ, a.shape[-1])
    chunk_rows = lambda width: pl.BlockSpec((BATCH, CHUNK, width), lambda c: (0, c, 0))
    n_rep = GATE_KINDS * N_HEADS
    replicate = (jnp.arange(GATE_WIDTH)[:, None] == jnp.arange(n_rep * LANES)[None, :] // LANES
                 ).astype(BF16)
    out = pl.pallas_call(
        _mlstm_core_kernel,
        out_shape=jax.ShapeDtypeStruct((BATCH, SEQ, D_MODEL), BF16),
        grid=(nc,),
        in_specs=[
            chunk_rows(QK),
            chunk_rows(QK),
            chunk_rows(D_MODEL),
            chunk_rows(D_MODEL),
            chunk_rows(GATE_WIDTH),
            pl.BlockSpec((BATCH, SUBLANES, CHUNK), lambda c: (0, 0, c)),
            pl.BlockSpec((1, D_MODEL), lambda c: (0, 0)),
            pl.BlockSpec((GATE_WIDTH, n_rep * LANES), lambda c: (0, 0)),
        ],
        out_specs=chunk_rows(D_MODEL),
        scratch_shapes=[pltpu.VMEM((n_streams, 1, LANES), F32)]
        + [pltpu.VMEM((DH_QK, DH_V + LANES), F32) for _ in range(n_streams)],
        compiler_params=_params("arbitrary"),
        name="mlstm_core",
    )(per_batch(q), per_batch(k), per_batch(v), per_batch(og), per_batch(gcol), grow, norm_g,
      replicate)
    return out.reshape(TOKENS, D_MODEL)


def _top2_sum(v0, v1, v2, v3):
    hi1, lo1 = jnp.maximum(v0, v1), jnp.minimum(v0, v1)
    hi2, lo2 = jnp.maximum(v2, v3), jnp.minimum(v2, v3)
    return jnp.maximum(hi1, hi2) + jnp.maximum(jnp.minimum(hi1, hi2), jnp.maximum(lo1, lo2))


def _route_tail(x_new, m, g2_ref, wr_ref, br_ref, tri_ref, u2_ref, cls_ref, rank_ref, cnt_ref,
                run_ref):
    @pl.when(pl.program_id(0) == 0)
    def _():
        run_ref[...] = jnp.zeros_like(run_ref)

    u2 = _rms(x_new) * g2_ref[...] * (1.0 + m[4:5]) + m[3:4]
    _store_token_rows(u2_ref, u2, u2.shape[0])
    logits = _dot(u2.astype(BF16), wr_ref[...])
    lt = logits.T[0:N_EXPERTS, :]
    e = jnp.exp(lt - jnp.max(lt, axis=0, keepdims=True))
    probs = e / jnp.sum(e, axis=0, keepdims=True)
    sel = probs + br_ref[...]
    sel_rows = [sel[j:j + 1, :] for j in range(N_EXPERTS)]
    best = jnp.zeros_like(sel_rows[0], dtype=jnp.int32)
    best_score = _top2_sum(*sel_rows[0:EXPERTS_PER_GROUP])
    for g in range(1, N_GROUPS):
        score = _top2_sum(*sel_rows[g * EXPERTS_PER_GROUP:(g + 1) * EXPERTS_PER_GROUP])
        better = score > best_score
        best = jnp.where(better, g, best)
        best_score = jnp.where(better, score, best_score)
    s = []
    for j in range(EXPERTS_PER_GROUP):
        sj = sel_rows[j]
        for g in range(1, N_GROUPS):
            sj = jnp.where(best == g, sel_rows[g * EXPERTS_PER_GROUP + j], sj)
        s.append(sj)
    chosen = []
    for j in range(EXPERTS_PER_GROUP):
        beaten = jnp.zeros_like(best)
        for i in range(EXPERTS_PER_GROUP):
            if i == j:
                continue
            wins = (s[i] >= s[j]) if i < j else (s[i] > s[j])
            beaten = beaten + jnp.where(wins, 1, 0)
        chosen.append(beaten < 2)
    pair = jnp.full_like(best, len(PAIRS) - 1)
    for p in range(len(PAIRS) - 2, -1, -1):
        a, b = PAIRS[p]
        pair = jnp.where(jnp.logical_and(chosen[a], chosen[b]), p, pair)
    cls = best * len(PAIRS) + pair
    cls_ref[0] = cls

    class_id = lax.broadcasted_iota(jnp.int32, (CLASS_ROWS, cls.shape[1]), 0)
    onehot = class_id == cls
    before = _dot(jnp.where(onehot, 1.0, 0.0).astype(BF16), tri_ref[...])
    run = run_ref[...]
    rank = jnp.sum(jnp.where(onehot, before + run[:, 0:1], 0.0), axis=0, keepdims=True)
    rank_ref[0] = rank.astype(jnp.int32)
    run = run + jnp.sum(jnp.where(onehot, 1.0, 0.0), axis=1, keepdims=True)
    run_ref[...] = run
    cnt_ref[...] = run.astype(jnp.int32)


def _route_out_shapes(t, tm):
    return (
        jax.ShapeDtypeStruct((t, D_MODEL), F32),
        jax.ShapeDtypeStruct((t * ROW_TILE, LANES), F32),
        jax.ShapeDtypeStruct((t // tm, 1, tm), jnp.int32),
        jax.ShapeDtypeStruct((t // tm, 1, tm), jnp.int32),
        jax.ShapeDtypeStruct((CLASS_ROWS, LANES), jnp.int32),
    )


def _route_out_specs(tm):
    return (
        pl.BlockSpec((tm, D_MODEL), lambda i, *_: (i, 0)),
        pl.BlockSpec((tm * ROW_TILE, LANES), lambda i, *_: (i, 0)),
        pl.BlockSpec((1, 1, tm), lambda i, *_: (i, 0, 0)),
        pl.BlockSpec((1, 1, tm), lambda i, *_: (i, 0, 0)),
        pl.BlockSpec((CLASS_ROWS, LANES), lambda i, *_: (0, 0)),
    )


def _const_spec(shape):
    return pl.BlockSpec(shape, lambda i, *_: (0,) * len(shape))


def _mlstm_out_kernel(h_ref, x_ref, mod_ref, wo_ref, g2_ref, wr_ref, br_ref, tri_ref,
                      x1_ref, u2_ref, cls_ref, rank_ref, cnt_ref, run_ref):
    m = mod_ref[0]
    x1 = x_ref[...] + m[2:3] * _dot(h_ref[...], wo_ref[...])
    x1_ref[...] = x1
    _route_tail(x1, m, g2_ref, wr_ref, br_ref, tri_ref, u2_ref, cls_ref, rank_ref, cnt_ref, run_ref)


def _mlstm_out(hg, x, mod, w_out, g2, wr, br, tri):
    t = x.shape[0]
    tm = TM_MIX
    tiles_per_batch = SEQ // tm
    return pl.pallas_call(
        _mlstm_out_kernel,
        out_shape=_route_out_shapes(t, tm),
        grid=(t // tm,),
        in_specs=[
            pl.BlockSpec((tm, D_MODEL), lambda i, *_: (i, 0)),
            pl.BlockSpec((tm, D_MODEL), lambda i, *_: (i, 0)),
            pl.BlockSpec((1, 6, D_MODEL), lambda i, *_: (i // tiles_per_batch, 0, 0)),
            _const_spec((D_MODEL, D_MODEL)),
            _const_spec((1, D_MODEL)),
            _const_spec((D_MODEL, ROUTER_COLS)),
            _const_spec((N_EXPERTS, 1)),
            _const_spec((tm, tm)),
        ],
        out_specs=_route_out_specs(tm),
        scratch_shapes=[pltpu.VMEM((CLASS_ROWS, LANES), F32)],
        compiler_params=_params("arbitrary"),
        name="mlstm_out_route",
    )(hg, x, mod, w_out, g2, wr, br, tri)


def _conv_layer_kernel(dest_ref, x_ref, ys_ref, mod0_ref, mod_ref, g1_ref, wi_ref, cw_ref, cb_ref,
                       wo_ref, g2_ref, wr_ref, br_ref, tri_ref,
                       x3_ref, u2_ref, cls_ref, rank_ref, cnt_ref,
                       run_ref, carry_ref, ybuf_ref, ysem_ref):
    tm = x_ref.shape[0]
    tiles_per_batch = SEQ // tm

    @pl.when(pl.program_id(0) % tiles_per_batch == 0)
    def _():
        carry_ref[...] = jnp.zeros_like(carry_ref)

    m = mod_ref[0]
    y = _gathered_tile(dest_ref, ys_ref, ybuf_ref, ysem_ref, tm, issue_first=False)
    x2 = x_ref[...] + mod0_ref[0][5:6] * y
    u = (_rms(x2) * g1_ref[...] * (1.0 + m[1:2]) + m[0:1]).astype(BF16)
    bgate = _dot(u, wi_ref[:, 0:D_MODEL])
    z = _dot(u, wi_ref[:, D_MODEL:2 * D_MODEL]) * _dot(u, wi_ref[:, 2 * D_MODEL:])
    prev = carry_ref[...]
    row = lax.broadcasted_iota(jnp.int32, z.shape, 0)
    z1 = jnp.where(row == 0, prev[7:8], pltpu.roll(z, 1, axis=0))
    z2 = jnp.where(row == 0, prev[6:7], jnp.where(row == 1, prev[7:8], pltpu.roll(z, 2, axis=0)))
    carry_ref[...] = z[tm - SUBLANES:, :]
    cw = cw_ref[...]
    zc = cw[0:1] * z2 + cw[1:2] * z1 + cw[2:3] * z + cb_ref[...]
    x3 = x2 + m[2:3] * _dot((bgate * zc).astype(BF16), wo_ref[...])
    x3_ref[...] = x3
    _route_tail(x3, m, g2_ref, wr_ref, br_ref, tri_ref, u2_ref, cls_ref, rank_ref, cnt_ref, run_ref)
    _drain_row_gather(ys_ref, ybuf_ref, ysem_ref)


def _conv_layer(dest, x, ys, mod0, mod, g1, w_in, conv_w, conv_b, w_out, g2, wr, br, tri):
    t = x.shape[0]
    tm = TM_MIX
    tiles_per_batch = SEQ // tm
    mod_spec = pl.BlockSpec((1, 6, D_MODEL), lambda i, *_: (i // tiles_per_batch, 0, 0))
    return pl.pallas_call(
        _conv_layer_kernel,
        out_shape=_route_out_shapes(t, tm),
        grid_spec=pltpu.PrefetchScalarGridSpec(
            num_scalar_prefetch=1,
            grid=(t // tm,),
            in_specs=[
                pl.BlockSpec((tm, D_MODEL), lambda i, *_: (i, 0)),
                pl.BlockSpec(memory_space=pl.ANY),
                mod_spec,
                mod_spec,
                _const_spec((1, D_MODEL)),
                _const_spec((D_MODEL, 3 * D_MODEL)),
                _const_spec((3, D_MODEL)),
                _const_spec((1, D_MODEL)),
                _const_spec((D_MODEL, D_MODEL)),
                _const_spec((1, D_MODEL)),
                _const_spec((D_MODEL, ROUTER_COLS)),
                _const_spec((N_EXPERTS, 1)),
                _const_spec((tm, tm)),
            ],
            out_specs=_route_out_specs(tm),
            scratch_shapes=[
                pltpu.VMEM((CLASS_ROWS, LANES), F32),
                pltpu.VMEM((SUBLANES, D_MODEL), F32),
                pltpu.VMEM((2, tm * ROW_TILE, LANES), F32),
                pltpu.SemaphoreType.DMA((2,)),
            ],
        ),
        compiler_params=_params("arbitrary"),
        name="conv_layer_route",
    )(dest, x, ys, mod0, mod, g1, w_in, conv_w, conv_b, w_out, g2, wr, br, tri)


def _invert_kernel(dest_ref, zeros_ref, src_ref, sem):
    fill = pltpu.make_async_copy(zeros_ref, src_ref, sem)
    fill.start()
    fill.wait()
    unroll = 8

    def body(t8, carry):
        for k in range(unroll):
            t = t8 * unroll + k
            src_ref[dest_ref[t]] = t
        return carry

    lax.fori_loop(0, dest_ref.shape[0] // unroll, body, 0)


def _invert_permutation(dest, n_out):
    smem = pl.BlockSpec(memory_space=pltpu.SMEM)
    return pl.pallas_call(
        _invert_kernel,
        out_shape=jax.ShapeDtypeStruct((n_out,), jnp.int32),
        in_specs=[smem, pl.BlockSpec(memory_space=pl.ANY)],
        out_specs=smem,
        scratch_shapes=[pltpu.SemaphoreType.DMA(())],
        name="invert_permutation",
    )(dest, jnp.zeros((n_out,), jnp.int32))


def _expert_kernel(ea_ref, eb_ref, chg_ref, nused_ref, src_ref, u2_ref, wga_ref, wua_ref, wda_ref,
                   wgb_ref, wub_ref, wdb_ref, wra_ref, wrb_ref, y_ref, xbuf_ref, xsem_ref, *wbf_refs):
    del ea_ref, eb_ref
    j = pl.program_id(0)
    tm = TM_EXPERT

    @pl.when(chg_ref[j] == 1)
    def _():
        for src, dst in zip((wga_ref, wua_ref, wda_ref, wgb_ref, wub_ref, wdb_ref), wbf_refs):
            dst[...] = src[0, 0].astype(BF16)

    rows, start_next = _gathered_tile(src_ref, u2_ref, xbuf_ref, xsem_ref, tm, issue_first=None)

    @pl.when(j < nused_ref[0])
    def _():
        start_next()
        xb = rows.astype(BF16)
        dl = jnp.sum(xb.astype(F32) * (wra_ref[0] - wrb_ref[0]), axis=-1, keepdims=True)
        w_a = _sigmoid(dl)
        w_b = _sigmoid(-dl)

        def ffn(wg_ref, wu_ref, wd_ref):
            gate = _dot(xb, wg_ref[...])
            hidden = gate * _sigmoid(gate) * _dot(xb, wu_ref[...])
            return _dot(hidden.astype(BF16), wd_ref[...])

        y = w_a * ffn(*wbf_refs[0:3]) + w_b * ffn(*wbf_refs[3:6])
        _store_token_rows(y_ref, y, tm)

    @pl.when(j >= nused_ref[0])
    def _():
        start_next()
        y_ref[...] = jnp.zeros_like(y_ref)

    _drain_row_gather(u2_ref, xbuf_ref, xsem_ref)


def _experts(layer, ea, eb, chg, n_used, src, u2, w_gate, w_up, w_down, wr_rows):
    tm = TM_EXPERT
    sel_a = lambda j, ea, eb, *_: (layer, ea[j], 0, 0)
    sel_b = lambda j, ea, eb, *_: (layer, eb[j], 0, 0)
    up_spec = lambda sel: pl.BlockSpec((1, 1, D_MODEL, D_EXPERT), sel)
    down_spec = lambda sel: pl.BlockSpec((1, 1, D_EXPERT, D_MODEL), sel)
    wr_spec = lambda sel: pl.BlockSpec((1, 1, D_MODEL), lambda *a: sel(*a)[1:])
    up_scratch = pltpu.VMEM((D_MODEL, D_EXPERT), BF16)
    down_scratch = pltpu.VMEM((D_EXPERT, D_MODEL), BF16)
    return pl.pallas_call(
        _expert_kernel,
        out_shape=jax.ShapeDtypeStruct((PADDED_ROWS * ROW_TILE, LANES), F32),
        grid_spec=pltpu.PrefetchScalarGridSpec(
            num_scalar_prefetch=5,
            grid=(N_EXPERT_TILES,),
            in_specs=[
                pl.BlockSpec(memory_space=pl.ANY),
                up_spec(sel_a), up_spec(sel_a), down_spec(sel_a),
                up_spec(sel_b), up_spec(sel_b), down_spec(sel_b),
                wr_spec(sel_a), wr_spec(sel_b),
            ],
            out_specs=pl.BlockSpec((tm * ROW_TILE, LANES), lambda j, *_: (j, 0)),
            scratch_shapes=[pltpu.VMEM((EXPERT_GATHER_BUFFERS, tm * ROW_TILE, LANES), F32),
                            pltpu.SemaphoreType.DMA((EXPERT_GATHER_BUFFERS,)),
                            up_scratch, up_scratch, down_scratch, up_scratch, up_scratch, down_scratch],
        ),
        compiler_params=_params("arbitrary"),
        name="grouped_experts",
    )(ea, eb, chg, n_used, src, u2, w_gate, w_up, w_down, w_gate, w_up, w_down, wr_rows, wr_rows)


def _final_kernel(dest_ref, x_ref, ys_ref, mod_ref, g_ref, o_ref, ybuf_ref, ysem_ref):
    y = _gathered_tile(dest_ref, ys_ref, ybuf_ref, ysem_ref, x_ref.shape[0], issue_first=True)
    x = x_ref[...] + mod_ref[0][5:6] * y
    o_ref[...] = _rms(x) * g_ref[...]
    _drain_row_gather(ys_ref, ybuf_ref, ysem_ref)


def _final(dest, x, ys, mod, g):
    t = x.shape[0]
    tm = TM_FINAL
    tiles_per_batch = SEQ // tm
    return pl.pallas_call(
        _final_kernel,
        out_shape=jax.ShapeDtypeStruct((t, D_MODEL), F32),
        grid_spec=pltpu.PrefetchScalarGridSpec(
            num_scalar_prefetch=1,
            grid=(t // tm,),
            in_specs=[
                pl.BlockSpec((tm, D_MODEL), lambda i, *_: (i, 0)),
                pl.BlockSpec(memory_space=pl.ANY),
                pl.BlockSpec((1, 6, D_MODEL), lambda i, *_: (i // tiles_per_batch, 0, 0)),
                _const_spec((1, D_MODEL)),
            ],
            out_specs=pl.BlockSpec((tm, D_MODEL), lambda i, *_: (i, 0)),
            scratch_shapes=[pltpu.VMEM((2, tm * ROW_TILE, LANES), F32), pltpu.SemaphoreType.DMA((2,))],
        ),
        compiler_params=_params("arbitrary"),
        name="final_norm",
    )(dest, x, ys, mod, g)


_PAIR_A = np.array([EXPERTS_PER_GROUP * (c // len(PAIRS)) + PAIRS[c % len(PAIRS)][0]
                    for c in range(N_CLASSES)], np.int32)
_PAIR_B = np.array([EXPERTS_PER_GROUP * (c // len(PAIRS)) + PAIRS[c % len(PAIRS)][1]
                    for c in range(N_CLASSES)], np.int32)


def _moe(layer, u2, cls, rank, cnt, w_gate, w_up, w_down, wr_rows):
    counts = cnt[:N_CLASSES, 0]
    tiles = (counts + TM_EXPERT - 1) // TM_EXPERT
    tile_end = jnp.cumsum(tiles)
    row_start = (tile_end - tiles) * TM_EXPERT
    n_used = tile_end[-1:]
    cls = cls.reshape(-1)
    dest = row_start[cls] + rank.reshape(-1)
    tile_cls = jnp.sum(jnp.arange(N_EXPERT_TILES)[:, None] >= tile_end[None, :], axis=1)
    tile_cls = jnp.minimum(tile_cls, tile_cls[jnp.maximum(n_used[0] - 1, 0)])
    ea = jnp.asarray(_PAIR_A)[tile_cls]
    eb = jnp.asarray(_PAIR_B)[tile_cls]
    chg = jnp.concatenate([jnp.ones((1,), jnp.int32),
                           (tile_cls[1:] != tile_cls[:-1]).astype(jnp.int32)])
    src = _invert_permutation(dest, PADDED_ROWS)
    ys = _experts(layer, ea, eb, chg, n_used.astype(jnp.int32), src, u2, w_gate, w_up, w_down, wr_rows)
    return dest, ys


def kernel(x, c, norm1_g, norm2_g, w_ada, b_ada, m_w_in, m_b_gates, m_norm_g, m_w_out,
           c_w_in, c_conv_w, c_conv_b, c_w_out, w_router, b_router,
           e_w_gate, e_w_up, e_w_down, final_g):
    xf = x.reshape(TOKENS, D_MODEL)
    mod = _ada(c, w_ada, b_ada)

    w_in = jnp.concatenate(
        [m_w_in[0].astype(BF16),
         jnp.zeros((D_MODEL, GATE_COLS - 2 * N_HEADS), BF16)], axis=1)
    bg = jnp.zeros((1, GATE_COLS), F32).at[0, :2 * N_HEADS].set(m_b_gates[0])
    wr = jnp.zeros((D_MODEL, ROUTER_COLS), BF16).at[:, :N_EXPERTS].set(w_router.astype(BF16))
    wr_rows = w_router.astype(BF16).astype(F32).T.reshape(N_EXPERTS, 1, D_MODEL)
    br = b_router.reshape(N_EXPERTS, 1)
    tri = (jnp.arange(TM_MIX)[:, None] < jnp.arange(TM_MIX)[None, :]).astype(BF16)
    row = lambda v: v.reshape(1, -1)

    q, k, v, og, gcol, grow = _mlstm_proj(xf, mod[0], row(norm1_g[0]), w_in, bg)
    hg = _mlstm_core(q, k, v, og, gcol, grow, row(m_norm_g[0]))
    x1, u2, cls, rank, cnt = _mlstm_out(hg, xf, mod[0], m_w_out[0].astype(BF16), row(norm2_g[0]),
                                        wr, br, tri)
    dest, ys = _moe(0, u2, cls, rank, cnt, e_w_gate, e_w_up, e_w_down, wr_rows)

    x3, u2, cls, rank, cnt = _conv_layer(
        dest, x1, ys, mod[0], mod[1], row(norm1_g[1]), c_w_in[0].astype(BF16), c_conv_w[0],
        row(c_conv_b[0]), c_w_out[0].astype(BF16), row(norm2_g[1]), wr, br, tri)
    dest, ys = _moe(1, u2, cls, rank, cnt, e_w_gate, e_w_up, e_w_down, wr_rows)

    out = _final(dest, x3, ys, mod[1], row(final_g))
    return out.reshape(BATCH, SEQ, D_MODEL)
```

```python
import functools

import jax
import jax.numpy as jnp
import numpy as np
from jax import lax
from jax.experimental import pallas as pl
from jax.experimental.pallas import tpu as pltpu

F32 = jnp.float32
BF16 = jnp.bfloat16

D_MODEL = 1024
BATCH = 4
SEQ = 8192
TOKENS = BATCH * SEQ
N_HEADS = 4
DH_V = 256
DH_QK = 128
QK = N_HEADS * DH_QK
N_EXPERTS = 16
N_GROUPS = 4
EXPERTS_PER_GROUP = 4
D_EXPERT = 512
EPS = 1e-6

LANES = 128
SUBLANES = 8
VMEM_LIMIT_BYTES = 56 * 1024 * 1024

CHUNK = 128
TM_PROJ = 512
TM_MIX = 512
TM_EXPERT = 256
GATE_COLS = LANES
GATE_WIDTH = 16
GATE_KINDS = 3
ROUTER_COLS = LANES
PAIRS = ((0, 1), (0, 2), (0, 3), (1, 2), (1, 3), (2, 3))
N_CLASSES = N_GROUPS * len(PAIRS)
CLASS_ROWS = 32
N_EXPERT_TILES = TOKENS // TM_EXPERT + N_CLASSES
PADDED_ROWS = N_EXPERT_TILES * TM_EXPERT
TM_FINAL = 512
EXPERT_GATHER_BUFFERS = 3
STREAM_GROUP = 4


def _params(*semantics):
    return pltpu.CompilerParams(dimension_semantics=semantics, vmem_limit_bytes=VMEM_LIMIT_BYTES)


def _dot(a, b):
    return jnp.dot(a, b, preferred_element_type=F32)


def _rms(x):
    return x * lax.rsqrt(jnp.mean(x * x, axis=-1, keepdims=True) + EPS)


def _sigmoid(x):
    return 1.0 / (1.0 + jnp.exp(-x))


ROW_TILE = D_MODEL // LANES


def _load_token_rows(ref, n):
    return jnp.concatenate([ref[pl.ds(c, n, stride=ROW_TILE), :] for c in range(ROW_TILE)], axis=1)


def _store_token_rows(ref, val, n):
    for c in range(ROW_TILE):
        ref[pl.ds(c, n, stride=ROW_TILE), :] = val[:, c * LANES:(c + 1) * LANES]


def _token_tile(ref, t):
    return ref.at[pl.ds(pl.multiple_of(t * ROW_TILE, ROW_TILE), ROW_TILE)]


def _start_row_gather(idx_ref, base, n, src_hbm, buf, sem):
    for r in range(n):
        pltpu.make_async_copy(_token_tile(src_hbm, idx_ref[base + r]), _token_tile(buf, r),
                              sem).start(priority=r % 2)


def _wait_row_gather(src_hbm, buf, sem):
    pltpu.make_async_copy(src_hbm.at[pl.ds(0, buf.shape[0])], buf, sem).wait()


def _gathered_tile(idx_ref, src_hbm, buf_ref, sem_ref, n, issue_first):
    i = pl.program_id(0)
    last = pl.num_programs(0) - 1
    nb = buf_ref.shape[0]
    ahead = nb - 1

    @pl.when(i == 0)
    def _():
        for k in range(ahead):
            _start_row_gather(idx_ref, jnp.minimum(k, last) * n, n, src_hbm, buf_ref.at[k], sem_ref.at[k])

    nxt = (i + ahead) % nb
    start_next = functools.partial(
        _start_row_gather, idx_ref, jnp.minimum(i + ahead, last) * n, n, src_hbm, buf_ref.at[nxt],
        sem_ref.at[nxt])
    if issue_first:
        start_next()
    slot = i % nb
    _wait_row_gather(src_hbm, buf_ref.at[slot], sem_ref.at[slot])
    rows = _load_token_rows(buf_ref.at[slot], n)
    if issue_first is None:
        return rows, start_next
    if not issue_first:
        start_next()
    return rows


def _drain_row_gather(src_hbm, buf_ref, sem_ref):
    i = pl.program_id(0)
    nb = buf_ref.shape[0]

    @pl.when(i == pl.num_programs(0) - 1)
    def _():
        for k in range(1, nb):
            _wait_row_gather(src_hbm, buf_ref.at[(i + k) % nb], sem_ref.at[(i + k) % nb])


def _ada_kernel(c_ref, w_ref, b_ref, o_ref):
    c = c_ref[...]
    cond = c * _sigmoid(c)
    o_ref[0] = jnp.dot(cond, w_ref[0], preferred_element_type=F32,
                       precision=lax.Precision.HIGHEST) + b_ref[0]


def _ada(c, w_ada, b_ada):
    depth, d, n = w_ada.shape
    tn = 1536
    c8 = jnp.zeros((SUBLANES, d), F32).at[:BATCH].set(c)
    out = pl.pallas_call(
        _ada_kernel,
        out_shape=jax.ShapeDtypeStruct((depth, SUBLANES, n), F32),
        grid=(depth, n // tn),
        in_specs=[
            pl.BlockSpec((SUBLANES, d), lambda l, j: (0, 0)),
            pl.BlockSpec((1, d, tn), lambda l, j: (l, 0, j)),
            pl.BlockSpec((1, 1, tn), lambda l, j: (l, 0, j)),
        ],
        out_specs=pl.BlockSpec((1, SUBLANES, tn), lambda l, j: (l, 0, j)),
        compiler_params=_params("arbitrary", "arbitrary"),
        name="ada_mod",
    )(c8, w_ada, b_ada.reshape(depth, 1, n))
    return out[:, :BATCH].reshape(depth, BATCH, 6, d)


def _mlstm_proj_kernel(x_ref, mod_ref, g_ref, w_ref, bg_ref, q_ref, k_ref, v_ref, og_ref,
                       gcol_ref, grow_ref):
    x = x_ref[...]
    m = mod_ref[0]
    u = (_rms(x) * g_ref[...] * (1.0 + m[1:2]) + m[0:1]).astype(BF16)
    q_ref[...] = (_dot(u, w_ref[:, 0:QK]) * (DH_QK ** -0.5)).astype(BF16)
    k_ref[...] = _dot(u, w_ref[:, QK:2 * QK]).astype(BF16)
    v_ref[...] = _dot(u, w_ref[:, 2 * QK:2 * QK + D_MODEL]).astype(BF16)
    og_ref[...] = _dot(u, w_ref[:, 2 * QK + D_MODEL:2 * QK + 2 * D_MODEL]).astype(BF16)
    gt = _dot(u, w_ref[:, 2 * QK + 2 * D_MODEL:]) + bg_ref[...]
    lane = lax.broadcasted_iota(jnp.int32, gt.shape, 1)
    pos = lax.broadcasted_iota(jnp.int32, gt.shape, 0) % CHUNK
    log_f = jnp.minimum(gt, 0.0) - jnp.log(1.0 + jnp.exp(-jnp.abs(gt)))
    b = log_f
    shift = 1
    while shift < CHUNK:
        b = b + jnp.where(pos >= shift, pltpu.roll(b, shift, axis=0), 0.0)
        shift *= 2
    imb = gt - pltpu.roll(b, LANES - N_HEADS, axis=1)
    cm = imb
    shift = 1
    while shift < CHUNK:
        cm = jnp.maximum(cm, jnp.where(pos >= shift, pltpu.roll(cm, shift, axis=0), -jnp.inf))
        shift *= 2
    cols = jnp.where(lane < N_HEADS, cm,
                     jnp.where(lane < 2 * N_HEADS, b,
                               jnp.where(lane < 3 * N_HEADS, pltpu.roll(imb, 2 * N_HEADS, axis=1), 0.0)))
    gcol_ref[...] = cols[:, 0:GATE_WIDTH]
    grow_ref[0] = imb.T[0:SUBLANES, :]


def _mlstm_proj(x, mod, g, w, bg):
    t, d = x.shape
    n = w.shape[1]
    tm = TM_PROJ
    tiles_per_batch = SEQ // tm
    return pl.pallas_call(
        _mlstm_proj_kernel,
        out_shape=(
            jax.ShapeDtypeStruct((t, QK), BF16),
            jax.ShapeDtypeStruct((t, QK), BF16),
            jax.ShapeDtypeStruct((t, D_MODEL), BF16),
            jax.ShapeDtypeStruct((t, D_MODEL), BF16),
            jax.ShapeDtypeStruct((t, GATE_WIDTH), F32),
            jax.ShapeDtypeStruct((BATCH, SUBLANES, SEQ), F32),
        ),
        grid=(t // tm,),
        in_specs=[
            pl.BlockSpec((tm, d), lambda i, *_: (i, 0)),
            pl.BlockSpec((1, 6, d), lambda i, *_: (i // tiles_per_batch, 0, 0)),
            pl.BlockSpec((1, d), lambda i, *_: (0, 0)),
            pl.BlockSpec((d, n), lambda i, *_: (0, 0)),
            pl.BlockSpec((1, GATE_COLS), lambda i, *_: (0, 0)),
        ],
        out_specs=(
            pl.BlockSpec((tm, QK), lambda i, *_: (i, 0)),
            pl.BlockSpec((tm, QK), lambda i, *_: (i, 0)),
            pl.BlockSpec((tm, D_MODEL), lambda i, *_: (i, 0)),
            pl.BlockSpec((tm, D_MODEL), lambda i, *_: (i, 0)),
            pl.BlockSpec((tm, GATE_WIDTH), lambda i, *_: (i, 0)),
            pl.BlockSpec((1, SUBLANES, tm),
                         lambda i, *_: (i // tiles_per_batch, 0, i % tiles_per_batch)),
        ),
        compiler_params=_params("arbitrary"),
        name="mlstm_proj",
    )(x, mod, g, w, bg)


def _mlstm_core_kernel(q_ref, k_ref, v_ref, og_ref, gcol_ref, grow_ref, ng_ref, rep_ref, o_ref,
                       m_ref, *c_refs):
    @pl.when(pl.program_id(0) == 0)
    def _():
        m_ref[...] = jnp.zeros_like(m_ref)
        for c_ref in c_refs:
            c_ref[...] = jnp.zeros_like(c_ref)

    ln = CHUNK
    row = lax.broadcasted_iota(jnp.int32, (ln, ln), 0)
    col = lax.broadcasted_iota(jnp.int32, (ln, ln), 1)
    causal = col <= row
    ones = jnp.ones((ln, LANES), BF16)
    n_streams = BATCH * N_HEADS
    m_all = [m_ref[st] for st in range(n_streams)]
    m_out = [None] * n_streams
    for bi in range(BATCH):
        g = gcol_ref[bi]
        g_hi = g.astype(BF16)
        g_r1 = g - g_hi.astype(F32)
        g_mid = g_r1.astype(BF16)
        g_lo = (g_r1 - g_mid.astype(F32)).astype(BF16)
        rep = _dot(g_hi, rep_ref[...]) + _dot(g_mid, rep_ref[...]) + _dot(g_lo, rep_ref[...])
        for h in range(N_HEADS):
            st = bi * N_HEADS + h
            lanes = lambda kind: slice((kind * N_HEADS + h) * LANES, (kind * N_HEADS + h + 1) * LANES)
            cm, b, imb_col = rep[:, lanes(0)], rep[:, lanes(1)], rep[:, lanes(2)]
            imb_row = grow_ref[bi, h:h + 1, :]
            qh = q_ref[bi, :, h * DH_QK:(h + 1) * DH_QK]
            kh = k_ref[bi, :, h * DH_QK:(h + 1) * DH_QK]
            v_ext = jnp.concatenate([v_ref[bi, :, h * DH_V:(h + 1) * DH_V], ones], axis=1)
            c_ref = c_refs[st]
            c_prev = c_ref[...]
            m_prev = m_all[st]

            big_m = jnp.maximum(m_prev, cm)
            d_mat = jnp.exp(jnp.where(causal, imb_row - big_m, -jnp.inf))
            s = lax.dot_general(qh, kh, (((1,), (1,)), ((), ())), preferred_element_type=F32) * d_mat
            q_inter = (qh.astype(F32) * jnp.exp(m_prev - big_m)).astype(BF16)
            lhs = jnp.concatenate([q_inter, s.astype(BF16)], axis=1)
            rhs = jnp.concatenate([c_prev.astype(BF16), v_ext], axis=0)
            nd = _dot(lhs, rhs)
            den = nd[:, DH_V:]
            inv = 1.0 / jnp.maximum(jnp.abs(den), jnp.exp(-(b + big_m)))
            hh = nd[:, :DH_V] * jnp.concatenate([inv, inv], axis=1)

            m_last = big_m[ln - 1:ln, :]
            kw = (kh.astype(F32) * jnp.exp(imb_col - m_last)).astype(BF16)
            decay = jnp.exp(m_prev - m_last)
            c_ref[...] = jnp.concatenate([decay] * 3, axis=1) * c_prev + lax.dot_general(
                kw, v_ext, (((0,), (0,)), ((), ())), preferred_element_type=F32)
            m_out[st] = b[ln - 1:ln, :] + m_last

            sl = slice(h * DH_V, (h + 1) * DH_V)
            gate = _sigmoid(og_ref[bi, :, sl].astype(F32))
            o_ref[bi, :, sl] = (_rms(hh) * ng_ref[:, sl] * gate).astype(BF16)
    for st in range(n_streams):
        m_ref[st] = m_out[st]


def _mlstm_core(q, k, v, og, gcol, grow, norm_g):
    nc = SEQ // CHUNK
    n_streams = BATCH * N_HEADS
    per_batch = lambda a: a.reshape(BATCH, SEQ, a.shape[-1])
    chunk_rows = lambda width: pl.BlockSpec((BATCH, CHUNK, width), lambda c: (0, c, 0))
    n_rep = GATE_KINDS * N_HEADS
    replicate = (jnp.arange(GATE_WIDTH)[:, None] == jnp.arange(n_rep * LANES)[None, :] // LANES
                 ).astype(BF16)
    out = pl.pallas_call(
        _mlstm_core_kernel,
        out_shape=jax.ShapeDtypeStruct((BATCH, SEQ, D_MODEL), BF16),
        grid=(nc,),
        in_specs=[
            chunk_rows(QK),
            chunk_rows(QK),
            chunk_rows(D_MODEL),
            chunk_rows(D_MODEL),
            chunk_rows(GATE_WIDTH),
            pl.BlockSpec((BATCH, SUBLANES, CHUNK), lambda c: (0, 0, c)),
            pl.BlockSpec((1, D_MODEL), lambda c: (0, 0)),
            pl.BlockSpec((GATE_WIDTH, n_rep * LANES), lambda c: (0, 0)),
        ],
        out_specs=chunk_rows(D_MODEL),
        scratch_shapes=[pltpu.VMEM((n_streams, 1, LANES), F32)]
        + [pltpu.VMEM((DH_QK, DH_V + LANES), F32) for _ in range(n_streams)],
        compiler_params=_params("arbitrary"),
        name="mlstm_core",
    )(per_batch(q), per_batch(k), per_batch(v), per_batch(og), per_batch(gcol), grow, norm_g,
      replicate)
    return out.reshape(TOKENS, D_MODEL)


def _top2_sum(v0, v1, v2, v3):
    hi1, lo1 = jnp.maximum(v0, v1), jnp.minimum(v0, v1)
    hi2, lo2 = jnp.maximum(v2, v3), jnp.minimum(v2, v3)
    return jnp.maximum(hi1, hi2) + jnp.maximum(jnp.minimum(hi1, hi2), jnp.maximum(lo1, lo2))


def _route_tail(x_new, m, g2_ref, wr_ref, br_ref, tri_ref, u2_ref, cls_ref, rank_ref, cnt_ref,
                run_ref):
    @pl.when(pl.program_id(0) == 0)
    def _():
        run_ref[...] = jnp.zeros_like(run_ref)

    u2 = _rms(x_new) * g2_ref[...] * (1.0 + m[4:5]) + m[3:4]
    _store_token_rows(u2_ref, u2, u2.shape[0])
    logits = _dot(u2.astype(BF16), wr_ref[...])
    lt = logits.T[0:N_EXPERTS, :]
    e = jnp.exp(lt - jnp.max(lt, axis=0, keepdims=True))
    probs = e / jnp.sum(e, axis=0, keepdims=True)
    sel = probs + br_ref[...]
    sel_rows = [sel[j:j + 1, :] for j in range(N_EXPERTS)]
    best = jnp.zeros_like(sel_rows[0], dtype=jnp.int32)
    best_score = _top2_sum(*sel_rows[0:EXPERTS_PER_GROUP])
    for g in range(1, N_GROUPS):
        score = _top2_sum(*sel_rows[g * EXPERTS_PER_GROUP:(g + 1) * EXPERTS_PER_GROUP])
        better = score > best_score
        best = jnp.where(better, g, best)
        best_score = jnp.where(better, score, best_score)
    s = []
    for j in range(EXPERTS_PER_GROUP):
        sj = sel_rows[j]
        for g in range(1, N_GROUPS):
            sj = jnp.where(best == g, sel_rows[g * EXPERTS_PER_GROUP + j], sj)
        s.append(sj)
    chosen = []
    for j in range(EXPERTS_PER_GROUP):
        beaten = jnp.zeros_like(best)
        for i in range(EXPERTS_PER_GROUP):
            if i == j:
                continue
            wins = (s[i] >= s[j]) if i < j else (s[i] > s[j])
            beaten = beaten + jnp.where(wins, 1, 0)
        chosen.append(beaten < 2)
    pair = jnp.full_like(best, len(PAIRS) - 1)
    for p in range(len(PAIRS) - 2, -1, -1):
        a, b = PAIRS[p]
        pair = jnp.where(jnp.logical_and(chosen[a], chosen[b]), p, pair)
    cls = best * len(PAIRS) + pair
    cls_ref[0] = cls

    class_id = lax.broadcasted_iota(jnp.int32, (CLASS_ROWS, cls.shape[1]), 0)
    onehot = class_id == cls
    before = _dot(jnp.where(onehot, 1.0, 0.0).astype(BF16), tri_ref[...])
    run = run_ref[...]
    rank = jnp.sum(jnp.where(onehot, before + run[:, 0:1], 0.0), axis=0, keepdims=True)
    rank_ref[0] = rank.astype(jnp.int32)
    run = run + jnp.sum(jnp.where(onehot, 1.0, 0.0), axis=1, keepdims=True)
    run_ref[...] = run
    cnt_ref[...] = run.astype(jnp.int32)


def _route_out_shapes(t, tm):
    return (
        jax.ShapeDtypeStruct((t, D_MODEL), F32),
        jax.ShapeDtypeStruct((t * ROW_TILE, LANES), F32),
        jax.ShapeDtypeStruct((t // tm, 1, tm), jnp.int32),
        jax.ShapeDtypeStruct((t // tm, 1, tm), jnp.int32),
        jax.ShapeDtypeStruct((CLASS_ROWS, LANES), jnp.int32),
    )


def _route_out_specs(tm):
    return (
        pl.BlockSpec((tm, D_MODEL), lambda i, *_: (i, 0)),
        pl.BlockSpec((tm * ROW_TILE, LANES), lambda i, *_: (i, 0)),
        pl.BlockSpec((1, 1, tm), lambda i, *_: (i, 0, 0)),
        pl.BlockSpec((1, 1, tm), lambda i, *_: (i, 0, 0)),
        pl.BlockSpec((CLASS_ROWS, LANES), lambda i, *_: (0, 0)),
    )


def _const_spec(shape):
    return pl.BlockSpec(shape, lambda i, *_: (0,) * len(shape))


def _mlstm_out_kernel(h_ref, x_ref, mod_ref, wo_ref, g2_ref, wr_ref, br_ref, tri_ref,
                      x1_ref, u2_ref, cls_ref, rank_ref, cnt_ref, run_ref):
    m = mod_ref[0]
    x1 = x_ref[...] + m[2:3] * _dot(h_ref[...], wo_ref[...])
    x1_ref[...] = x1
    _route_tail(x1, m, g2_ref, wr_ref, br_ref, tri_ref, u2_ref, cls_ref, rank_ref, cnt_ref, run_ref)


def _mlstm_out(hg, x, mod, w_out, g2, wr, br, tri):
    t = x.shape[0]
    tm = TM_MIX
    tiles_per_batch = SEQ // tm
    return pl.pallas_call(
        _mlstm_out_kernel,
        out_shape=_route_out_shapes(t, tm),
        grid=(t // tm,),
        in_specs=[
            pl.BlockSpec((tm, D_MODEL), lambda i, *_: (i, 0)),
            pl.BlockSpec((tm, D_MODEL), lambda i, *_: (i, 0)),
            pl.BlockSpec((1, 6, D_MODEL), lambda i, *_: (i // tiles_per_batch, 0, 0)),
            _const_spec((D_MODEL, D_MODEL)),
            _const_spec((1, D_MODEL)),
            _const_spec((D_MODEL, ROUTER_COLS)),
            _const_spec((N_EXPERTS, 1)),
            _const_spec((tm, tm)),
        ],
        out_specs=_route_out_specs(tm),
        scratch_shapes=[pltpu.VMEM((CLASS_ROWS, LANES), F32)],
        compiler_params=_params("arbitrary"),
        name="mlstm_out_route",
    )(hg, x, mod, w_out, g2, wr, br, tri)


def _conv_layer_kernel(dest_ref, x_ref, ys_ref, mod0_ref, mod_ref, g1_ref, wi_ref, cw_ref, cb_ref,
                       wo_ref, g2_ref, wr_ref, br_ref, tri_ref,
                       x3_ref, u2_ref, cls_ref, rank_ref, cnt_ref,
                       run_ref, carry_ref, ybuf_ref, ysem_ref):
    tm = x_ref.shape[0]
    tiles_per_batch = SEQ // tm

    @pl.when(pl.program_id(0) % tiles_per_batch == 0)
    def _():
        carry_ref[...] = jnp.zeros_like(carry_ref)

    m = mod_ref[0]
    y = _gathered_tile(dest_ref, ys_ref, ybuf_ref, ysem_ref, tm, issue_first=False)
    x2 = x_ref[...] + mod0_ref[0][5:6] * y
    u = (_rms(x2) * g1_ref[...] * (1.0 + m[1:2]) + m[0:1]).astype(BF16)
    bgate = _dot(u, wi_ref[:, 0:D_MODEL])
    z = _dot(u, wi_ref[:, D_MODEL:2 * D_MODEL]) * _dot(u, wi_ref[:, 2 * D_MODEL:])
    prev = carry_ref[...]
    row = lax.broadcasted_iota(jnp.int32, z.shape, 0)
    z1 = jnp.where(row == 0, prev[7:8], pltpu.roll(z, 1, axis=0))
    z2 = jnp.where(row == 0, prev[6:7], jnp.where(row == 1, prev[7:8], pltpu.roll(z, 2, axis=0)))
    carry_ref[...] = z[tm - SUBLANES:, :]
    cw = cw_ref[...]
    zc = cw[0:1] * z2 + cw[1:2] * z1 + cw[2:3] * z + cb_ref[...]
    x3 = x2 + m[2:3] * _dot((bgate * zc).astype(BF16), wo_ref[...])
    x3_ref[...] = x3
    _route_tail(x3, m, g2_ref, wr_ref, br_ref, tri_ref, u2_ref, cls_ref, rank_ref, cnt_ref, run_ref)
    _drain_row_gather(ys_ref, ybuf_ref, ysem_ref)


def _conv_layer(dest, x, ys, mod0, mod, g1, w_in, conv_w, conv_b, w_out, g2, wr, br, tri):
    t = x.shape[0]
    tm = TM_MIX
    tiles_per_batch = SEQ // tm
    mod_spec = pl.BlockSpec((1, 6, D_MODEL), lambda i, *_: (i // tiles_per_batch, 0, 0))
    return pl.pallas_call(
        _conv_layer_kernel,
        out_shape=_route_out_shapes(t, tm),
        grid_spec=pltpu.PrefetchScalarGridSpec(
            num_scalar_prefetch=1,
            grid=(t // tm,),
            in_specs=[
                pl.BlockSpec((tm, D_MODEL), lambda i, *_: (i, 0)),
                pl.BlockSpec(memory_space=pl.ANY),
                mod_spec,
                mod_spec,
                _const_spec((1, D_MODEL)),
                _const_spec((D_MODEL, 3 * D_MODEL)),
                _const_spec((3, D_MODEL)),
                _const_spec((1, D_MODEL)),
                _const_spec((D_MODEL, D_MODEL)),
                _const_spec((1, D_MODEL)),
                _const_spec((D_MODEL, ROUTER_COLS)),
                _const_spec((N_EXPERTS, 1)),
                _const_spec((tm, tm)),
            ],
            out_specs=_route_out_specs(tm),
            scratch_shapes=[
                pltpu.VMEM((CLASS_ROWS, LANES), F32),
                pltpu.VMEM((SUBLANES, D_MODEL), F32),
                pltpu.VMEM((2, tm * ROW_TILE, LANES), F32),
                pltpu.SemaphoreType.DMA((2,)),
            ],
        ),
        compiler_params=_params("arbitrary"),
        name="conv_layer_route",
    )(dest, x, ys, mod0, mod, g1, w_in, conv_w, conv_b, w_out, g2, wr, br, tri)


def _invert_kernel(dest_ref, zeros_ref, src_ref, sem):
    fill = pltpu.make_async_copy(zeros_ref, src_ref, sem)
    fill.start()
    fill.wait()
    unroll = 8

    def body(t8, carry):
        for k in range(unroll):
            t = t8 * unroll + k
            src_ref[dest_ref[t]] = t
        return carry

    lax.fori_loop(0, dest_ref.shape[0] // unroll, body, 0)


def _invert_permutation(dest, n_out):
    smem = pl.BlockSpec(memory_space=pltpu.SMEM)
    return pl.pallas_call(
        _invert_kernel,
        out_shape=jax.ShapeDtypeStruct((n_out,), jnp.int32),
        in_specs=[smem, pl.BlockSpec(memory_space=pl.ANY)],
        out_specs=smem,
        scratch_shapes=[pltpu.SemaphoreType.DMA(())],
        name="invert_permutation",
    )(dest, jnp.zeros((n_out,), jnp.int32))


def _expert_kernel(ea_ref, eb_ref, chg_ref, nused_ref, src_ref, u2_ref, wga_ref, wua_ref, wda_ref,
                   wgb_ref, wub_ref, wdb_ref, wra_ref, wrb_ref, y_ref, xbuf_ref, xsem_ref, *wbf_refs):
    del ea_ref, eb_ref
    j = pl.program_id(0)
    tm = TM_EXPERT

    @pl.when(chg_ref[j] == 1)
    def _():
        for src, dst in zip((wga_ref, wua_ref, wda_ref, wgb_ref, wub_ref, wdb_ref), wbf_refs):
            dst[...] = src[0, 0].astype(BF16)

    rows = _gathered_tile(src_ref, u2_ref, xbuf_ref, xsem_ref, tm, issue_first=True)

    @pl.when(j < nused_ref[0])
    def _():
        xb = rows.astype(BF16)
        dl = jnp.sum(xb.astype(F32) * (wra_ref[0] - wrb_ref[0]), axis=-1, keepdims=True)
        w_a = _sigmoid(dl)
        w_b = _sigmoid(-dl)

        def ffn(wg_ref, wu_ref, wd_ref):
            gate = _dot(xb, wg_ref[...])
            hidden = gate * _sigmoid(gate) * _dot(xb, wu_ref[...])
            return _dot(hidden.astype(BF16), wd_ref[...])

        y = w_a * ffn(*wbf_refs[0:3]) + w_b * ffn(*wbf_refs[3:6])
        _store_token_rows(y_ref, y, tm)

    @pl.when(j >= nused_ref[0])
    def _():
        y_ref[...] = jnp.zeros_like(y_ref)

    _drain_row_gather(u2_ref, xbuf_ref, xsem_ref)


def _experts(layer, ea, eb, chg, n_used, src, u2, w_gate, w_up, w_down, wr_rows):
    tm = TM_EXPERT
    sel_a = lambda j, ea, eb, *_: (layer, ea[j], 0, 0)
    sel_b = lambda j, ea, eb, *_: (layer, eb[j], 0, 0)
    up_spec = lambda sel: pl.BlockSpec((1, 1, D_MODEL, D_EXPERT), sel)
    down_spec = lambda sel: pl.BlockSpec((1, 1, D_EXPERT, D_MODEL), sel)
    wr_spec = lambda sel: pl.BlockSpec((1, 1, D_MODEL), lambda *a: sel(*a)[1:])
    up_scratch = pltpu.VMEM((D_MODEL, D_EXPERT), BF16)
    down_scratch = pltpu.VMEM((D_EXPERT, D_MODEL), BF16)
    return pl.pallas_call(
        _expert_kernel,
        out_shape=jax.ShapeDtypeStruct((PADDED_ROWS * ROW_TILE, LANES), F32),
        grid_spec=pltpu.PrefetchScalarGridSpec(
            num_scalar_prefetch=5,
            grid=(N_EXPERT_TILES,),
            in_specs=[
                pl.BlockSpec(memory_space=pl.ANY),
                up_spec(sel_a), up_spec(sel_a), down_spec(sel_a),
                up_spec(sel_b), up_spec(sel_b), down_spec(sel_b),
                wr_spec(sel_a), wr_spec(sel_b),
            ],
            out_specs=pl.BlockSpec((tm * ROW_TILE, LANES), lambda j, *_: (j, 0)),
            scratch_shapes=[pltpu.VMEM((EXPERT_GATHER_BUFFERS, tm * ROW_TILE, LANES), F32),
                            pltpu.SemaphoreType.DMA((EXPERT_GATHER_BUFFERS,)),
                            up_scratch, up_scratch, down_scratch, up_scratch, up_scratch, down_scratch],
        ),
        compiler_params=_params("arbitrary"),
        name="grouped_experts",
    )(ea, eb, chg, n_used, src, u2, w_gate, w_up, w_down, w_gate, w_up, w_down, wr_rows, wr_rows)


def _final_kernel(dest_ref, x_ref, ys_ref, mod_ref, g_ref, o_ref, ybuf_ref, ysem_ref):
    y = _gathered_tile(dest_ref, ys_ref, ybuf_ref, ysem_ref, x_ref.shape[0], issue_first=True)
    x = x_ref[...] + mod_ref[0][5:6] * y
    o_ref[...] = _rms(x) * g_ref[...]
    _drain_row_gather(ys_ref, ybuf_ref, ysem_ref)


def _final(dest, x, ys, mod, g):
    t = x.shape[0]
    tm = TM_FINAL
    tiles_per_batch = SEQ // tm
    return pl.pallas_call(
        _final_kernel,
        out_shape=jax.ShapeDtypeStruct((t, D_MODEL), F32),
        grid_spec=pltpu.PrefetchScalarGridSpec(
            num_scalar_prefetch=1,
            grid=(t // tm,),
            in_specs=[
                pl.BlockSpec((tm, D_MODEL), lambda i, *_: (i, 0)),
                pl.BlockSpec(memory_space=pl.ANY),
                pl.BlockSpec((1, 6, D_MODEL), lambda i, *_: (i // tiles_per_batch, 0, 0)),
                _const_spec((1, D_MODEL)),
            ],
            out_specs=pl.BlockSpec((tm, D_MODEL), lambda i, *_: (i, 0)),
            scratch_shapes=[pltpu.VMEM((2, tm * ROW_TILE, LANES), F32), pltpu.SemaphoreType.DMA((2,))],
        ),
        compiler_params=_params("arbitrary"),
        name="final_norm",
    )(dest, x, ys, mod, g)


_PAIR_A = np.array([EXPERTS_PER_GROUP * (c // len(PAIRS)) + PAIRS[c % len(PAIRS)][0]
                    for c in range(N_CLASSES)], np.int32)
_PAIR_B = np.array([EXPERTS_PER_GROUP * (c // len(PAIRS)) + PAIRS[c % len(PAIRS)][1]
                    for c in range(N_CLASSES)], np.int32)


def _moe(layer, u2, cls, rank, cnt, w_gate, w_up, w_down, wr_rows):
    counts = cnt[:N_CLASSES, 0]
    tiles = (counts + TM_EXPERT - 1) // TM_EXPERT
    tile_end = jnp.cumsum(tiles)
    row_start = (tile_end - tiles) * TM_EXPERT
    n_used = tile_end[-1:]
    cls = cls.reshape(-1)
    dest = row_start[cls] + rank.reshape(-1)
    tile_cls = jnp.sum(jnp.arange(N_EXPERT_TILES)[:, None] >= tile_end[None, :], axis=1)
    tile_cls = jnp.minimum(tile_cls, tile_cls[jnp.maximum(n_used[0] - 1, 0)])
    ea = jnp.asarray(_PAIR_A)[tile_cls]
    eb = jnp.asarray(_PAIR_B)[tile_cls]
    chg = jnp.concatenate([jnp.ones((1,), jnp.int32),
                           (tile_cls[1:] != tile_cls[:-1]).astype(jnp.int32)])
    src = _invert_permutation(dest, PADDED_ROWS)
    ys = _experts(layer, ea, eb, chg, n_used.astype(jnp.int32), src, u2, w_gate, w_up, w_down, wr_rows)
    return dest, ys


def kernel(x, c, norm1_g, norm2_g, w_ada, b_ada, m_w_in, m_b_gates, m_norm_g, m_w_out,
           c_w_in, c_conv_w, c_conv_b, c_w_out, w_router, b_router,
           e_w_gate, e_w_up, e_w_down, final_g):
    xf = x.reshape(TOKENS, D_MODEL)
    mod = _ada(c, w_ada, b_ada)

    w_in = jnp.concatenate(
        [m_w_in[0].astype(BF16),
         jnp.zeros((D_MODEL, GATE_COLS - 2 * N_HEADS), BF16)], axis=1)
    bg = jnp.zeros((1, GATE_COLS), F32).at[0, :2 * N_HEADS].set(m_b_gates[0])
    wr = jnp.zeros((D_MODEL, ROUTER_COLS), BF16).at[:, :N_EXPERTS].set(w_router.astype(BF16))
    wr_rows = w_router.astype(BF16).astype(F32).T.reshape(N_EXPERTS, 1, D_MODEL)
    br = b_router.reshape(N_EXPERTS, 1)
    tri = (jnp.arange(TM_MIX)[:, None] < jnp.arange(TM_MIX)[None, :]).astype(BF16)
    row = lambda v: v.reshape(1, -1)

    q, k, v, og, gcol, grow = _mlstm_proj(xf, mod[0], row(norm1_g[0]), w_in, bg)
    hg = _mlstm_core(q, k, v, og, gcol, grow, row(m_norm_g[0]))
    x1, u2, cls, rank, cnt = _mlstm_out(hg, xf, mod[0], m_w_out[0].astype(BF16), row(norm2_g[0]),
                                        wr, br, tri)
    dest, ys = _moe(0, u2, cls, rank, cnt, e_w_gate, e_w_up, e_w_down, wr_rows)

    x3, u2, cls, rank, cnt = _conv_layer(
        dest, x1, ys, mod[0], mod[1], row(norm1_g[1]), c_w_in[0].astype(BF16), c_conv_w[0],
        row(c_conv_b[0]), c_w_out[0].astype(BF16), row(norm2_g[1]), wr, br, tri)
    dest, ys = _moe(1, u2, cls, rank, cnt, e_w_gate, e_w_up, e_w_down, wr_rows)

    out = _final(dest, x3, ys, mod[1], row(final_g))
    return out.reshape(BATCH, SEQ, D_MODEL)
```

```python
import functools

import jax
import jax.numpy as jnp
import numpy as np
from jax import lax
from jax.experimental import pallas as pl
from jax.experimental.pallas import tpu as pltpu

F32 = jnp.float32
BF16 = jnp.bfloat16

D_MODEL = 1024
BATCH = 4
SEQ = 8192
TOKENS = BATCH * SEQ
N_HEADS = 4
DH_V = 256
DH_QK = 128
QK = N_HEADS * DH_QK
N_EXPERTS = 16
N_GROUPS = 4
EXPERTS_PER_GROUP = 4
D_EXPERT = 512
EPS = 1e-6

LANES = 128
SUBLANES = 8
VMEM_LIMIT_BYTES = 56 * 1024 * 1024

CHUNK = 128
TM_PROJ = 512
TM_MIX = 512
TM_EXPERT = 256
GATE_COLS = LANES
GATE_WIDTH = 16
GATE_KINDS = 3
ROUTER_COLS = LANES
PAIRS = ((0, 1), (0, 2), (0, 3), (1, 2), (1, 3), (2, 3))
N_CLASSES = N_GROUPS * len(PAIRS)
CLASS_ROWS = 32
N_EXPERT_TILES = TOKENS // TM_EXPERT + N_CLASSES
PADDED_ROWS = N_EXPERT_TILES * TM_EXPERT
TM_FINAL = 512
EXPERT_GATHER_BUFFERS = 3
STREAM_GROUP = 4


def _params(*semantics):
    return pltpu.CompilerParams(dimension_semantics=semantics, vmem_limit_bytes=VMEM_LIMIT_BYTES)


def _dot(a, b):
    return jnp.dot(a, b, preferred_element_type=F32)


def _rms(x):
    return x * lax.rsqrt(jnp.mean(x * x, axis=-1, keepdims=True) + EPS)


def _sigmoid(x):
    return 1.0 / (1.0 + jnp.exp(-x))


ROW_TILE = D_MODEL // LANES


def _load_token_rows(ref, n):
    return jnp.concatenate([ref[pl.ds(c, n, stride=ROW_TILE), :] for c in range(ROW_TILE)], axis=1)


def _store_token_rows(ref, val, n):
    for c in range(ROW_TILE):
        ref[pl.ds(c, n, stride=ROW_TILE), :] = val[:, c * LANES:(c + 1) * LANES]


def _token_tile(ref, t):
    return ref.at[pl.ds(pl.multiple_of(t * ROW_TILE, ROW_TILE), ROW_TILE)]


def _start_row_gather(idx_ref, base, n, src_hbm, buf, sem):
    for r in range(n):
        pltpu.make_async_copy(_token_tile(src_hbm, idx_ref[base + r]), _token_tile(buf, r),
                              sem).start(priority=r % 2)


def _wait_row_gather(src_hbm, buf, sem):
    pltpu.make_async_copy(src_hbm.at[pl.ds(0, buf.shape[0])], buf, sem).wait()


def _gathered_tile(idx_ref, src_hbm, buf_ref, sem_ref, n, issue_first):
    i = pl.program_id(0)
    last = pl.num_programs(0) - 1
    nb = buf_ref.shape[0]
    ahead = nb - 1

    @pl.when(i == 0)
    def _():
        for k in range(ahead):
            _start_row_gather(idx_ref, jnp.minimum(k, last) * n, n, src_hbm, buf_ref.at[k], sem_ref.at[k])

    nxt = (i + ahead) % nb
    start_next = functools.partial(
        _start_row_gather, idx_ref, jnp.minimum(i + ahead, last) * n, n, src_hbm, buf_ref.at[nxt],
        sem_ref.at[nxt])
    if issue_first:
        start_next()
    slot = i % nb
    _wait_row_gather(src_hbm, buf_ref.at[slot], sem_ref.at[slot])
    rows = _load_token_rows(buf_ref.at[slot], n)
    if issue_first is None:
        return rows, start_next
    if not issue_first:
        start_next()
    return rows


def _drain_row_gather(src_hbm, buf_ref, sem_ref):
    i = pl.program_id(0)
    nb = buf_ref.shape[0]

    @pl.when(i == pl.num_programs(0) - 1)
    def _():
        for k in range(1, nb):
            _wait_row_gather(src_hbm, buf_ref.at[(i + k) % nb], sem_ref.at[(i + k) % nb])


def _ada_kernel(c_ref, w_ref, b_ref, o_ref):
    c = c_ref[...]
    cond = c * _sigmoid(c)
    o_ref[0] = jnp.dot(cond, w_ref[0], preferred_element_type=F32,
                       precision=lax.Precision.HIGHEST) + b_ref[0]


def _ada(c, w_ada, b_ada):
    depth, d, n = w_ada.shape
    tn = 1536
    c8 = jnp.zeros((SUBLANES, d), F32).at[:BATCH].set(c)
    out = pl.pallas_call(
        _ada_kernel,
        out_shape=jax.ShapeDtypeStruct((depth, SUBLANES, n), F32),
        grid=(depth, n // tn),
        in_specs=[
            pl.BlockSpec((SUBLANES, d), lambda l, j: (0, 0)),
            pl.BlockSpec((1, d, tn), lambda l, j: (l, 0, j)),
            pl.BlockSpec((1, 1, tn), lambda l, j: (l, 0, j)),
        ],
        out_specs=pl.BlockSpec((1, SUBLANES, tn), lambda l, j: (l, 0, j)),
        compiler_params=_params("arbitrary", "arbitrary"),
        name="ada_mod",
    )(c8, w_ada, b_ada.reshape(depth, 1, n))
    return out[:, :BATCH].reshape(depth, BATCH, 6, d)


def _mlstm_proj_kernel(x_ref, mod_ref, g_ref, w_ref, bg_ref, q_ref, k_ref, v_ref, og_ref,
                       gcol_ref, grow_ref):
    x = x_ref[...]
    m = mod_ref[0]
    u = (_rms(x) * g_ref[...] * (1.0 + m[1:2]) + m[0:1]).astype(BF16)
    q_ref[...] = (_dot(u, w_ref[:, 0:QK]) * (DH_QK ** -0.5)).astype(BF16)
    k_ref[...] = _dot(u, w_ref[:, QK:2 * QK]).astype(BF16)
    v_ref[...] = _dot(u, w_ref[:, 2 * QK:2 * QK + D_MODEL]).astype(BF16)
    og_ref[...] = _dot(u, w_ref[:, 2 * QK + D_MODEL:2 * QK + 2 * D_MODEL]).astype(BF16)
    gt = _dot(u, w_ref[:, 2 * QK + 2 * D_MODEL:]) + bg_ref[...]
    lane = lax.broadcasted_iota(jnp.int32, gt.shape, 1)
    pos = lax.broadcasted_iota(jnp.int32, gt.shape, 0) % CHUNK
    log_f = jnp.minimum(gt, 0.0) - jnp.log(1.0 + jnp.exp(-jnp.abs(gt)))
    b = log_f
    shift = 1
    while shift < CHUNK:
        b = b + jnp.where(pos >= shift, pltpu.roll(b, shift, axis=0), 0.0)
        shift *= 2
    imb = gt - pltpu.roll(b, LANES - N_HEADS, axis=1)
    cm = imb
    shift = 1
    while shift < CHUNK:
        cm = jnp.maximum(cm, jnp.where(pos >= shift, pltpu.roll(cm, shift, axis=0), -jnp.inf))
        shift *= 2
    cols = jnp.where(lane < N_HEADS, cm,
                     jnp.where(lane < 2 * N_HEADS, b,
                               jnp.where(lane < 3 * N_HEADS, pltpu.roll(imb, 2 * N_HEADS, axis=1), 0.0)))
    gcol_ref[...] = cols[:, 0:GATE_WIDTH]
    grow_ref[0] = imb.T[0:SUBLANES, :]


def _mlstm_proj(x, mod, g, w, bg):
    t, d = x.shape
    n = w.shape[1]
    tm = TM_PROJ
    tiles_per_batch = SEQ // tm
    return pl.pallas_call(
        _mlstm_proj_kernel,
        out_shape=(
            jax.ShapeDtypeStruct((t, QK), BF16),
            jax.ShapeDtypeStruct((t, QK), BF16),
            jax.ShapeDtypeStruct((t, D_MODEL), BF16),
            jax.ShapeDtypeStruct((t, D_MODEL), BF16),
            jax.ShapeDtypeStruct((t, GATE_WIDTH), F32),
            jax.ShapeDtypeStruct((BATCH, SUBLANES, SEQ), F32),
        ),
        grid=(t // tm,),
        in_specs=[
            pl.BlockSpec((tm, d), lambda i, *_: (i, 0)),
            pl.BlockSpec((1, 6, d), lambda i, *_: (i // tiles_per_batch, 0, 0)),
            pl.BlockSpec((1, d), lambda i, *_: (0, 0)),
            pl.BlockSpec((d, n), lambda i, *_: (0, 0)),
            pl.BlockSpec((1, GATE_COLS), lambda i, *_: (0, 0)),
        ],
        out_specs=(
            pl.BlockSpec((tm, QK), lambda i, *_: (i, 0)),
            pl.BlockSpec((tm, QK), lambda i, *_: (i, 0)),
            pl.BlockSpec((tm, D_MODEL), lambda i, *_: (i, 0)),
            pl.BlockSpec((tm, D_MODEL), lambda i, *_: (i, 0)),
            pl.BlockSpec((tm, GATE_WIDTH), lambda i, *_: (i, 0)),
            pl.BlockSpec((1, SUBLANES, tm),
                         lambda i, *_: (i // tiles_per_batch, 0, i % tiles_per_batch)),
        ),
        compiler_params=_params("arbitrary"),
        name="mlstm_proj",
    )(x, mod, g, w, bg)


def _mlstm_core_kernel(q_ref, k_ref, v_ref, og_ref, gcol_ref, grow_ref, ng_ref, rep_ref, o_ref,
                       m_ref, *c_refs):
    @pl.when(pl.program_id(0) == 0)
    def _():
        m_ref[...] = jnp.zeros_like(m_ref)
        for c_ref in c_refs:
            c_ref[...] = jnp.zeros_like(c_ref)

    ln = CHUNK
    row = lax.broadcasted_iota(jnp.int32, (ln, ln), 0)
    col = lax.broadcasted_iota(jnp.int32, (ln, ln), 1)
    causal = col <= row
    ones = jnp.ones((ln, LANES), BF16)
    n_streams = BATCH * N_HEADS
    m_all = [m_ref[st] for st in range(n_streams)]
    m_out = [None] * n_streams
    for bi in range(BATCH):
        g = gcol_ref[bi]
        g_hi = g.astype(BF16)
        g_r1 = g - g_hi.astype(F32)
        g_mid = g_r1.astype(BF16)
        g_lo = (g_r1 - g_mid.astype(F32)).astype(BF16)
        rep = _dot(g_hi, rep_ref[...]) + _dot(g_mid, rep_ref[...]) + _dot(g_lo, rep_ref[...])
        for h in range(N_HEADS):
            st = bi * N_HEADS + h
            lanes = lambda kind: slice((kind * N_HEADS + h) * LANES, (kind * N_HEADS + h + 1) * LANES)
            cm, b, imb_col = rep[:, lanes(0)], rep[:, lanes(1)], rep[:, lanes(2)]
            imb_row = grow_ref[bi, h:h + 1, :]
            qh = q_ref[bi, :, h * DH_QK:(h + 1) * DH_QK]
            kh = k_ref[bi, :, h * DH_QK:(h + 1) * DH_QK]
            v_ext = jnp.concatenate([v_ref[bi, :, h * DH_V:(h + 1) * DH_V], ones], axis=1)
            c_ref = c_refs[st]
            c_prev = c_ref[...]
            m_prev = m_all[st]

            big_m = jnp.maximum(m_prev, cm)
            d_mat = jnp.exp(jnp.where(causal, imb_row - big_m, -jnp.inf))
            s = lax.dot_general(qh, kh, (((1,), (1,)), ((), ())), preferred_element_type=F32) * d_mat
            q_inter = (qh.astype(F32) * jnp.exp(m_prev - big_m)).astype(BF16)
            lhs = jnp.concatenate([q_inter, s.astype(BF16)], axis=1)
            rhs = jnp.concatenate([c_prev.astype(BF16), v_ext], axis=0)
            nd = _dot(lhs, rhs)
            den = nd[:, DH_V:]
            inv = 1.0 / jnp.maximum(jnp.abs(den), jnp.exp(-(b + big_m)))
            hh = nd[:, :DH_V] * jnp.concatenate([inv, inv], axis=1)

            m_last = big_m[ln - 1:ln, :]
            kw = (kh.astype(F32) * jnp.exp(imb_col - m_last)).astype(BF16)
            decay = jnp.exp(m_prev - m_last)
            c_ref[...] = jnp.concatenate([decay] * 3, axis=1) * c_prev + lax.dot_general(
                kw, v_ext, (((0,), (0,)), ((), ())), preferred_element_type=F32)
            m_out[st] = b[ln - 1:ln, :] + m_last

            sl = slice(h * DH_V, (h + 1) * DH_V)
            gate = _sigmoid(og_ref[bi, :, sl].astype(F32))
            o_ref[bi, :, sl] = (_rms(hh) * ng_ref[:, sl] * gate).astype(BF16)
    for st in range(n_streams):
        m_ref[st] = m_out[st]


def _mlstm_core(q, k, v, og, gcol, grow, norm_g):
    nc = SEQ // CHUNK
    n_streams = BATCH * N_HEADS
    per_batch = lambda a: a.reshape(BATCH, SEQ, a.shape[-1])
    chunk_rows = lambda width: pl.BlockSpec((BATCH, CHUNK, width), lambda c: (0, c, 0))
    n_rep = GATE_KINDS * N_HEADS
    replicate = (jnp.arange(GATE_WIDTH)[:, None] == jnp.arange(n_rep * LANES)[None, :] // LANES
                 ).astype(BF16)
    out = pl.pallas_call(
        _mlstm_core_kernel,
        out_shape=jax.ShapeDtypeStruct((BATCH, SEQ, D_MODEL), BF16),
        grid=(nc,),
        in_specs=[
            chunk_rows(QK),
            chunk_rows(QK),
            chunk_rows(D_MODEL),
            chunk_rows(D_MODEL),
            chunk_rows(GATE_WIDTH),
            pl.BlockSpec((BATCH, SUBLANES, CHUNK), lambda c: (0, 0, c)),
            pl.BlockSpec((1, D_MODEL), lambda c: (0, 0)),
            pl.BlockSpec((GATE_WIDTH, n_rep * LANES), lambda c: (0, 0)),
        ],
        out_specs=chunk_rows(D_MODEL),
        scratch_shapes=[pltpu.VMEM((n_streams, 1, LANES), F32)]
        + [pltpu.VMEM((DH_QK, DH_V + LANES), F32) for _ in range(n_streams)],
        compiler_params=_params("arbitrary"),
        name="mlstm_core",
    )(per_batch(q), per_batch(k), per_batch(v), per_batch(og), per_batch(gcol), grow, norm_g,
      replicate)
    return out.reshape(TOKENS, D_MODEL)


def _top2_sum(v0, v1, v2, v3):
    hi1, lo1 = jnp.maximum(v0, v1), jnp.minimum(v0, v1)
    hi2, lo2 = jnp.maximum(v2, v3), jnp.minimum(v2, v3)
    return jnp.maximum(hi1, hi2) + jnp.maximum(jnp.minimum(hi1, hi2), jnp.maximum(lo1, lo2))


def _route_tail(x_new, m, g2_ref, wr_ref, br_ref, tri_ref, u2_ref, cls_ref, rank_ref, cnt_ref,
                run_ref):
    @pl.when(pl.program_id(0) == 0)
    def _():
        run_ref[...] = jnp.zeros_like(run_ref)

    u2 = _rms(x_new) * g2_ref[...] * (1.0 + m[4:5]) + m[3:4]
    _store_token_rows(u2_ref, u2, u2.shape[0])
    logits = _dot(u2.astype(BF16), wr_ref[...])
    lt = logits.T[0:N_EXPERTS, :]
    e = jnp.exp(lt - jnp.max(lt, axis=0, keepdims=True))
    probs = e / jnp.sum(e, axis=0, keepdims=True)
    sel = probs + br_ref[...]
    sel_rows = [sel[j:j + 1, :] for j in range(N_EXPERTS)]
    best = jnp.zeros_like(sel_rows[0], dtype=jnp.int32)
    best_score = _top2_sum(*sel_rows[0:EXPERTS_PER_GROUP])
    for g in range(1, N_GROUPS):
        score = _top2_sum(*sel_rows[g * EXPERTS_PER_GROUP:(g + 1) * EXPERTS_PER_GROUP])
        better = score > best_score
        best = jnp.where(better, g, best)
        best_score = jnp.where(better, score, best_score)
    s = []
    for j in range(EXPERTS_PER_GROUP):
        sj = sel_rows[j]
        for g in range(1, N_GROUPS):
            sj = jnp.where(best == g, sel_rows[g * EXPERTS_PER_GROUP + j], sj)
        s.append(sj)
    chosen = []
    for j in range(EXPERTS_PER_GROUP):
        beaten = jnp.zeros_like(best)
        for i in range(EXPERTS_PER_GROUP):
            if i == j:
                continue
            wins = (s[i] >= s[j]) if i < j else (s[i] > s[j])
            beaten = beaten + jnp.where(wins, 1, 0)
        chosen.append(beaten < 2)
    pair = jnp.full_like(best, len(PAIRS) - 1)
    for p in range(len(PAIRS) - 2, -1, -1):
        a, b = PAIRS[p]
        pair = jnp.where(jnp.logical_and(chosen[a], chosen[b]), p, pair)
    cls = best * len(PAIRS) + pair
    cls_ref[0] = cls

    class_id = lax.broadcasted_iota(jnp.int32, (CLASS_ROWS, cls.shape[1]), 0)
    onehot = class_id == cls
    before = _dot(jnp.where(onehot, 1.0, 0.0).astype(BF16), tri_ref[...])
    run = run_ref[...]
    rank = jnp.sum(jnp.where(onehot, before + run[:, 0:1], 0.0), axis=0, keepdims=True)
    rank_ref[0] = rank.astype(jnp.int32)
    run = run + jnp.sum(jnp.where(onehot, 1.0, 0.0), axis=1, keepdims=True)
    run_ref[...] = run
    cnt_ref[...] = run.astype(jnp.int32)


def _route_out_shapes(t, tm):
    return (
        jax.ShapeDtypeStruct((t, D_MODEL), F32),
        jax.ShapeDtypeStruct((t * ROW_TILE, LANES), F32),
        jax.ShapeDtypeStruct((t // tm, 1, tm), jnp.int32),
        jax.ShapeDtypeStruct((t // tm, 1, tm), jnp.int32),
        jax.ShapeDtypeStruct((CLASS_ROWS, LANES), jnp.int32),
    )


def _route_out_specs(tm):
    return (
        pl.BlockSpec((tm, D_MODEL), lambda i, *_: (i, 0)),
        pl.BlockSpec((tm * ROW_TILE, LANES), lambda i, *_: (i, 0)),
        pl.BlockSpec((1, 1, tm), lambda i, *_: (i, 0, 0)),
        pl.BlockSpec((1, 1, tm), lambda i, *_: (i, 0, 0)),
        pl.BlockSpec((CLASS_ROWS, LANES), lambda i, *_: (0, 0)),
    )


def _const_spec(shape):
    return pl.BlockSpec(shape, lambda i, *_: (0,) * len(shape))


def _mlstm_out_kernel(h_ref, x_ref, mod_ref, wo_ref, g2_ref, wr_ref, br_ref, tri_ref,
                      x1_ref, u2_ref, cls_ref, rank_ref, cnt_ref, run_ref):
    m = mod_ref[0]
    x1 = x_ref[...] + m[2:3] * _dot(h_ref[...], wo_ref[...])
    x1_ref[...] = x1
    _route_tail(x1, m, g2_ref, wr_ref, br_ref, tri_ref, u2_ref, cls_ref, rank_ref, cnt_ref, run_ref)


def _mlstm_out(hg, x, mod, w_out, g2, wr, br, tri):
    t = x.shape[0]
    tm = TM_MIX
    tiles_per_batch = SEQ // tm
    return pl.pallas_call(
        _mlstm_out_kernel,
        out_shape=_route_out_shapes(t, tm),
        grid=(t // tm,),
        in_specs=[
            pl.BlockSpec((tm, D_MODEL), lambda i, *_: (i, 0)),
            pl.BlockSpec((tm, D_MODEL), lambda i, *_: (i, 0)),
            pl.BlockSpec((1, 6, D_MODEL), lambda i, *_: (i // tiles_per_batch, 0, 0)),
            _const_spec((D_MODEL, D_MODEL)),
            _const_spec((1, D_MODEL)),
            _const_spec((D_MODEL, ROUTER_COLS)),
            _const_spec((N_EXPERTS, 1)),
            _const_spec((tm, tm)),
        ],
        out_specs=_route_out_specs(tm),
        scratch_shapes=[pltpu.VMEM((CLASS_ROWS, LANES), F32)],
        compiler_params=_params("arbitrary"),
        name="mlstm_out_route",
    )(hg, x, mod, w_out, g2, wr, br, tri)


def _conv_layer_kernel(dest_ref, x_ref, ys_ref, mod0_ref, mod_ref, g1_ref, wi_ref, cw_ref, cb_ref,
                       wo_ref, g2_ref, wr_ref, br_ref, tri_ref,
                       x3_ref, u2_ref, cls_ref, rank_ref, cnt_ref,
                       run_ref, carry_ref, ybuf_ref, ysem_ref):
    tm = x_ref.shape[0]
    tiles_per_batch = SEQ // tm

    @pl.when(pl.program_id(0) % tiles_per_batch == 0)
    def _():
        carry_ref[...] = jnp.zeros_like(carry_ref)

    m = mod_ref[0]
    y = _gathered_tile(dest_ref, ys_ref, ybuf_ref, ysem_ref, tm, issue_first=False)
    x2 = x_ref[...] + mod0_ref[0][5:6] * y
    u = (_rms(x2) * g1_ref[...] * (1.0 + m[1:2]) + m[0:1]).astype(BF16)
    bgate = _dot(u, wi_ref[:, 0:D_MODEL])
    z = _dot(u, wi_ref[:, D_MODEL:2 * D_MODEL]) * _dot(u, wi_ref[:, 2 * D_MODEL:])
    prev = carry_ref[...]
    row = lax.broadcasted_iota(jnp.int32, z.shape, 0)
    z1 = jnp.where(row == 0, prev[7:8], pltpu.roll(z, 1, axis=0))
    z2 = jnp.where(row == 0, prev[6:7], jnp.where(row == 1, prev[7:8], pltpu.roll(z, 2, axis=0)))
    carry_ref[...] = z[tm - SUBLANES:, :]
    cw = cw_ref[...]
    zc = cw[0:1] * z2 + cw[1:2] * z1 + cw[2:3] * z + cb_ref[...]
    x3 = x2 + m[2:3] * _dot((bgate * zc).astype(BF16), wo_ref[...])
    x3_ref[...] = x3
    _route_tail(x3, m, g2_ref, wr_ref, br_ref, tri_ref, u2_ref, cls_ref, rank_ref, cnt_ref, run_ref)
    _drain_row_gather(ys_ref, ybuf_ref, ysem_ref)


def _conv_layer(dest, x, ys, mod0, mod, g1, w_in, conv_w, conv_b, w_out, g2, wr, br, tri):
    t = x.shape[0]
    tm = TM_MIX
    tiles_per_batch = SEQ // tm
    mod_spec = pl.BlockSpec((1, 6, D_MODEL), lambda i, *_: (i // tiles_per_batch, 0, 0))
    return pl.pallas_call(
        _conv_layer_kernel,
        out_shape=_route_out_shapes(t, tm),
        grid_spec=pltpu.PrefetchScalarGridSpec(
            num_scalar_prefetch=1,
            grid=(t // tm,),
            in_specs=[
                pl.BlockSpec((tm, D_MODEL), lambda i, *_: (i, 0)),
                pl.BlockSpec(memory_space=pl.ANY),
                mod_spec,
                mod_spec,
                _const_spec((1, D_MODEL)),
                _const_spec((D_MODEL, 3 * D_MODEL)),
                _const_spec((3, D_MODEL)),
                _const_spec((1, D_MODEL)),
                _const_spec((D_MODEL, D_MODEL)),
                _const_spec((1, D_MODEL)),
                _const_spec((D_MODEL, ROUTER_COLS)),
                _const_spec((N_EXPERTS, 1)),
                _const_spec((tm, tm)),
            ],
            out_specs=_route_out_specs(tm),
            scratch_shapes=[
                pltpu.VMEM((CLASS_ROWS, LANES), F32),
                pltpu.VMEM((SUBLANES, D_MODEL), F32),
                pltpu.VMEM((2, tm * ROW_TILE, LANES), F32),
                pltpu.SemaphoreType.DMA((2,)),
            ],
        ),
        compiler_params=_params("arbitrary"),
        name="conv_layer_route",
    )(dest, x, ys, mod0, mod, g1, w_in, conv_w, conv_b, w_out, g2, wr, br, tri)


def _invert_kernel(dest_ref, zeros_ref, src_ref, sem):
    fill = pltpu.make_async_copy(zeros_ref, src_ref, sem)
    fill.start()
    fill.wait()
    unroll = 8

    def body(t8, carry):
        for k in range(unroll):
            t = t8 * unroll + k
            src_ref[dest_ref[t]] = t
        return carry

    lax.fori_loop(0, dest_ref.shape[0] // unroll, body, 0)


def _invert_permutation(dest, n_out):
    smem = pl.BlockSpec(memory_space=pltpu.SMEM)
    return pl.pallas_call(
        _invert_kernel,
        out_shape=jax.ShapeDtypeStruct((n_out,), jnp.int32),
        in_specs=[smem, pl.BlockSpec(memory_space=pl.ANY)],
        out_specs=smem,
        scratch_shapes=[pltpu.SemaphoreType.DMA(())],
        name="invert_permutation",
    )(dest, jnp.zeros((n_out,), jnp.int32))


def _expert_kernel(ea_ref, eb_ref, chg_ref, nused_ref, src_ref, u2_ref, wga_ref, wua_ref, wda_ref,
                   wgb_ref, wub_ref, wdb_ref, wra_ref, wrb_ref, y_ref, xbuf_ref, xsem_ref, *wbf_refs):
    del ea_ref, eb_ref
    j = pl.program_id(0)
    tm = TM_EXPERT

    @pl.when(chg_ref[j] == 1)
    def _():
        for src, dst in zip((wga_ref, wua_ref, wda_ref, wgb_ref, wub_ref, wdb_ref), wbf_refs):
            dst[...] = src[0, 0].astype(BF16)

    rows = _gathered_tile(src_ref, u2_ref, xbuf_ref, xsem_ref, tm, issue_first=True)

    @pl.when(j < nused_ref[0])
    def _():
        xb = rows.astype(BF16)
        dl = jnp.sum(xb.astype(F32) * (wra_ref[0] - wrb_ref[0]), axis=-1, keepdims=True)
        w_a = _sigmoid(dl)
        w_b = _sigmoid(-dl)

        def ffn(wg_ref, wu_ref, wd_ref):
            gate = _dot(xb, wg_ref[...])
            hidden = gate * _sigmoid(gate) * _dot(xb, wu_ref[...])
            return _dot(hidden.astype(BF16), wd_ref[...])

        y = w_a * ffn(*wbf_refs[0:3]) + w_b * ffn(*wbf_refs[3:6])
        _store_token_rows(y_ref, y, tm)

    @pl.when(j >= nused_ref[0])
    def _():
        y_ref[...] = jnp.zeros_like(y_ref)

    _drain_row_gather(u2_ref, xbuf_ref, xsem_ref)


def _experts(layer, ea, eb, chg, n_used, src, u2, w_gate, w_up, w_down, wr_rows):
    tm = TM_EXPERT
    sel_a = lambda j, ea, eb, *_: (layer, ea[j], 0, 0)
    sel_b = lambda j, ea, eb, *_: (layer, eb[j], 0, 0)
    up_spec = lambda sel: pl.BlockSpec((1, 1, D_MODEL, D_EXPERT), sel)
    down_spec = lambda sel: pl.BlockSpec((1, 1, D_EXPERT, D_MODEL), sel)
    wr_spec = lambda sel: pl.BlockSpec((1, 1, D_MODEL), lambda *a: sel(*a)[1:])
    up_scratch = pltpu.VMEM((D_MODEL, D_EXPERT), BF16)
    down_scratch = pltpu.VMEM((D_EXPERT, D_MODEL), BF16)
    return pl.pallas_call(
        _expert_kernel,
        out_shape=jax.ShapeDtypeStruct((PADDED_ROWS * ROW_TILE, LANES), F32),
        grid_spec=pltpu.PrefetchScalarGridSpec(
            num_scalar_prefetch=4,
            grid=(N_EXPERT_TILES,),
            in_specs=[
                pl.BlockSpec(memory_space=pltpu.SMEM),
                pl.BlockSpec(memory_space=pl.ANY),
                up_spec(sel_a), up_spec(sel_a), down_spec(sel_a),
                up_spec(sel_b), up_spec(sel_b), down_spec(sel_b),
                wr_spec(sel_a), wr_spec(sel_b),
            ],
            out_specs=pl.BlockSpec((tm * ROW_TILE, LANES), lambda j, *_: (j, 0)),
            scratch_shapes=[pltpu.VMEM((EXPERT_GATHER_BUFFERS, tm * ROW_TILE, LANES), F32),
                            pltpu.SemaphoreType.DMA((EXPERT_GATHER_BUFFERS,)),
                            up_scratch, up_scratch, down_scratch, up_scratch, up_scratch, down_scratch],
        ),
        compiler_params=_params("arbitrary"),
        name="grouped_experts",
    )(ea, eb, chg, n_used, src, u2, w_gate, w_up, w_down, w_gate, w_up, w_down, wr_rows, wr_rows)


def _final_kernel(dest_ref, x_ref, ys_ref, mod_ref, g_ref, o_ref, ybuf_ref, ysem_ref):
    y = _gathered_tile(dest_ref, ys_ref, ybuf_ref, ysem_ref, x_ref.shape[0], issue_first=True)
    x = x_ref[...] + mod_ref[0][5:6] * y
    o_ref[...] = _rms(x) * g_ref[...]
    _drain_row_gather(ys_ref, ybuf_ref, ysem_ref)


def _final(dest, x, ys, mod, g):
    t = x.shape[0]
    tm = TM_FINAL
    tiles_per_batch = SEQ // tm
    return pl.pallas_call(
        _final_kernel,
        out_shape=jax.ShapeDtypeStruct((t, D_MODEL), F32),
        grid_spec=pltpu.PrefetchScalarGridSpec(
            num_scalar_prefetch=1,
            grid=(t // tm,),
            in_specs=[
                pl.BlockSpec((tm, D_MODEL), lambda i, *_: (i, 0)),
                pl.BlockSpec(memory_space=pl.ANY),
                pl.BlockSpec((1, 6, D_MODEL), lambda i, *_: (i // tiles_per_batch, 0, 0)),
                _const_spec((1, D_MODEL)),
            ],
            out_specs=pl.BlockSpec((tm, D_MODEL), lambda i, *_: (i, 0)),
            scratch_shapes=[pltpu.VMEM((2, tm * ROW_TILE, LANES), F32), pltpu.SemaphoreType.DMA((2,))],
        ),
        compiler_params=_params("arbitrary"),
        name="final_norm",
    )(dest, x, ys, mod, g)


_PAIR_A = np.array([EXPERTS_PER_GROUP * (c // len(PAIRS)) + PAIRS[c % len(PAIRS)][0]
                    for c in range(N_CLASSES)], np.int32)
_PAIR_B = np.array([EXPERTS_PER_GROUP * (c // len(PAIRS)) + PAIRS[c % len(PAIRS)][1]
                    for c in range(N_CLASSES)], np.int32)


def _moe(layer, u2, cls, rank, cnt, w_gate, w_up, w_down, wr_rows):
    counts = cnt[:N_CLASSES, 0]
    tiles = (counts + TM_EXPERT - 1) // TM_EXPERT
    tile_end = jnp.cumsum(tiles)
    row_start = (tile_end - tiles) * TM_EXPERT
    n_used = tile_end[-1:]
    cls = cls.reshape(-1)
    dest = row_start[cls] + rank.reshape(-1)
    tile_cls = jnp.sum(jnp.arange(N_EXPERT_TILES)[:, None] >= tile_end[None, :], axis=1)
    tile_cls = jnp.minimum(tile_cls, tile_cls[jnp.maximum(n_used[0] - 1, 0)])
    ea = jnp.asarray(_PAIR_A)[tile_cls]
    eb = jnp.asarray(_PAIR_B)[tile_cls]
    chg = jnp.concatenate([jnp.ones((1,), jnp.int32),
                           (tile_cls[1:] != tile_cls[:-1]).astype(jnp.int32)])
    src = _invert_permutation(dest, PADDED_ROWS)
    ys = _experts(layer, ea, eb, chg, n_used.astype(jnp.int32), src, u2, w_gate, w_up, w_down, wr_rows)
    return dest, ys


def kernel(x, c, norm1_g, norm2_g, w_ada, b_ada, m_w_in, m_b_gates, m_norm_g, m_w_out,
           c_w_in, c_conv_w, c_conv_b, c_w_out, w_router, b_router,
           e_w_gate, e_w_up, e_w_down, final_g):
    xf = x.reshape(TOKENS, D_MODEL)
    mod = _ada(c, w_ada, b_ada)

    w_in = jnp.concatenate(
        [m_w_in[0].astype(BF16),
         jnp.zeros((D_MODEL, GATE_COLS - 2 * N_HEADS), BF16)], axis=1)
    bg = jnp.zeros((1, GATE_COLS), F32).at[0, :2 * N_HEADS].set(m_b_gates[0])
    wr = jnp.zeros((D_MODEL, ROUTER_COLS), BF16).at[:, :N_EXPERTS].set(w_router.astype(BF16))
    wr_rows = w_router.astype(BF16).astype(F32).T.reshape(N_EXPERTS, 1, D_MODEL)
    br = b_router.reshape(N_EXPERTS, 1)
    tri = (jnp.arange(TM_MIX)[:, None] < jnp.arange(TM_MIX)[None, :]).astype(BF16)
    row = lambda v: v.reshape(1, -1)

    q, k, v, og, gcol, grow = _mlstm_proj(xf, mod[0], row(norm1_g[0]), w_in, bg)
    hg = _mlstm_core(q, k, v, og, gcol, grow, row(m_norm_g[0]))
    x1, u2, cls, rank, cnt = _mlstm_out(hg, xf, mod[0], m_w_out[0].astype(BF16), row(norm2_g[0]),
                                        wr, br, tri)
    dest, ys = _moe(0, u2, cls, rank, cnt, e_w_gate, e_w_up, e_w_down, wr_rows)

    x3, u2, cls, rank, cnt = _conv_layer(
        dest, x1, ys, mod[0], mod[1], row(norm1_g[1]), c_w_in[0].astype(BF16), c_conv_w[0],
        row(c_conv_b[0]), c_w_out[0].astype(BF16), row(norm2_g[1]), wr, br, tri)
    dest, ys = _moe(1, u2, cls, rank, cnt, e_w_gate, e_w_up, e_w_down, wr_rows)

    out = _final(dest, x3, ys, mod[1], row(final_g))
    return out.reshape(BATCH, SEQ, D_MODEL)
```

```python
import functools

import jax
import jax.numpy as jnp
import numpy as np
from jax import lax
from jax.experimental import pallas as pl
from jax.experimental.pallas import tpu as pltpu

F32 = jnp.float32
BF16 = jnp.bfloat16

D_MODEL = 1024
BATCH = 4
SEQ = 8192
TOKENS = BATCH * SEQ
N_HEADS = 4
DH_V = 256
DH_QK = 128
QK = N_HEADS * DH_QK
N_EXPERTS = 16
N_GROUPS = 4
EXPERTS_PER_GROUP = 4
D_EXPERT = 512
EPS = 1e-6

LANES = 128
SUBLANES = 8
VMEM_LIMIT_BYTES = 56 * 1024 * 1024

CHUNK = 128
TM_PROJ = 512
TM_MIX = 512
TM_EXPERT = 256
GATE_COLS = LANES
GATE_WIDTH = 16
GATE_KINDS = 3
ROUTER_COLS = LANES
PAIRS = ((0, 1), (0, 2), (0, 3), (1, 2), (1, 3), (2, 3))
N_CLASSES = N_GROUPS * len(PAIRS)
CLASS_ROWS = 32
N_EXPERT_TILES = TOKENS // TM_EXPERT + N_CLASSES
PADDED_ROWS = N_EXPERT_TILES * TM_EXPERT
TM_FINAL = 512
EXPERT_GATHER_BUFFERS = 3
STREAM_GROUP = 4


def _params(*semantics):
    return pltpu.CompilerParams(dimension_semantics=semantics, vmem_limit_bytes=VMEM_LIMIT_BYTES)


def _dot(a, b):
    return jnp.dot(a, b, preferred_element_type=F32)


def _rms(x):
    return x * lax.rsqrt(jnp.mean(x * x, axis=-1, keepdims=True) + EPS)


def _sigmoid(x):
    return 1.0 / (1.0 + jnp.exp(-x))


ROW_TILE = D_MODEL // LANES


def _load_token_rows(ref, n):
    return jnp.concatenate([ref[pl.ds(c, n, stride=ROW_TILE), :] for c in range(ROW_TILE)], axis=1)


def _store_token_rows(ref, val, n):
    for c in range(ROW_TILE):
        ref[pl.ds(c, n, stride=ROW_TILE), :] = val[:, c * LANES:(c + 1) * LANES]


def _token_tile(ref, t):
    return ref.at[pl.ds(pl.multiple_of(t * ROW_TILE, ROW_TILE), ROW_TILE)]


def _start_row_gather(idx_ref, base, n, src_hbm, buf, sem):
    for r in range(n):
        pltpu.make_async_copy(_token_tile(src_hbm, idx_ref[base + r]), _token_tile(buf, r),
                              sem).start(priority=r % 2)


def _wait_row_gather(src_hbm, buf, sem):
    pltpu.make_async_copy(src_hbm.at[pl.ds(0, buf.shape[0])], buf, sem).wait()


def _gathered_tile(idx_ref, src_hbm, buf_ref, sem_ref, n, issue_first):
    i = pl.program_id(0)
    last = pl.num_programs(0) - 1
    nb = buf_ref.shape[0]
    ahead = nb - 1

    @pl.when(i == 0)
    def _():
        for k in range(ahead):
            _start_row_gather(idx_ref, jnp.minimum(k, last) * n, n, src_hbm, buf_ref.at[k], sem_ref.at[k])

    nxt = (i + ahead) % nb
    start_next = functools.partial(
        _start_row_gather, idx_ref, jnp.minimum(i + ahead, last) * n, n, src_hbm, buf_ref.at[nxt],
        sem_ref.at[nxt])
    if issue_first:
        start_next()
    slot = i % nb
    _wait_row_gather(src_hbm, buf_ref.at[slot], sem_ref.at[slot])
    rows = _load_token_rows(buf_ref.at[slot], n)
    if issue_first is None:
        return rows, start_next
    if not issue_first:
        start_next()
    return rows


def _drain_row_gather(src_hbm, buf_ref, sem_ref):
    i = pl.program_id(0)
    nb = buf_ref.shape[0]

    @pl.when(i == pl.num_programs(0) - 1)
    def _():
        for k in range(1, nb):
            _wait_row_gather(src_hbm, buf_ref.at[(i + k) % nb], sem_ref.at[(i + k) % nb])


def _ada_kernel(c_ref, w_ref, b_ref, o_ref):
    c = c_ref[...]
    cond = c * _sigmoid(c)
    o_ref[0] = jnp.dot(cond, w_ref[0], preferred_element_type=F32,
                       precision=lax.Precision.HIGHEST) + b_ref[0]


def _ada(c, w_ada, b_ada):
    depth, d, n = w_ada.shape
    tn = 1536
    c8 = jnp.zeros((SUBLANES, d), F32).at[:BATCH].set(c)
    out = pl.pallas_call(
        _ada_kernel,
        out_shape=jax.ShapeDtypeStruct((depth, SUBLANES, n), F32),
        grid=(depth, n // tn),
        in_specs=[
            pl.BlockSpec((SUBLANES, d), lambda l, j: (0, 0)),
            pl.BlockSpec((1, d, tn), lambda l, j: (l, 0, j)),
            pl.BlockSpec((1, 1, tn), lambda l, j: (l, 0, j)),
        ],
        out_specs=pl.BlockSpec((1, SUBLANES, tn), lambda l, j: (l, 0, j)),
        compiler_params=_params("arbitrary", "arbitrary"),
        name="ada_mod",
    )(c8, w_ada, b_ada.reshape(depth, 1, n))
    return out[:, :BATCH].reshape(depth, BATCH, 6, d)


def _mlstm_proj_kernel(x_ref, mod_ref, g_ref, w_ref, bg_ref, q_ref, k_ref, v_ref, og_ref,
                       gcol_ref, grow_ref):
    x = x_ref[...]
    m = mod_ref[0]
    u = (_rms(x) * g_ref[...] * (1.0 + m[1:2]) + m[0:1]).astype(BF16)
    q_ref[...] = (_dot(u, w_ref[:, 0:QK]) * (DH_QK ** -0.5)).astype(BF16)
    k_ref[...] = _dot(u, w_ref[:, QK:2 * QK]).astype(BF16)
    v_ref[...] = _dot(u, w_ref[:, 2 * QK:2 * QK + D_MODEL]).astype(BF16)
    og_ref[...] = _dot(u, w_ref[:, 2 * QK + D_MODEL:2 * QK + 2 * D_MODEL]).astype(BF16)
    gt = _dot(u, w_ref[:, 2 * QK + 2 * D_MODEL:]) + bg_ref[...]
    lane = lax.broadcasted_iota(jnp.int32, gt.shape, 1)
    pos = lax.broadcasted_iota(jnp.int32, gt.shape, 0) % CHUNK
    log_f = jnp.minimum(gt, 0.0) - jnp.log(1.0 + jnp.exp(-jnp.abs(gt)))
    b = log_f
    shift = 1
    while shift < CHUNK:
        b = b + jnp.where(pos >= shift, pltpu.roll(b, shift, axis=0), 0.0)
        shift *= 2
    imb = gt - pltpu.roll(b, LANES - N_HEADS, axis=1)
    cm = imb
    shift = 1
    while shift < CHUNK:
        cm = jnp.maximum(cm, jnp.where(pos >= shift, pltpu.roll(cm, shift, axis=0), -jnp.inf))
        shift *= 2
    cols = jnp.where(lane < N_HEADS, cm,
                     jnp.where(lane < 2 * N_HEADS, b,
                               jnp.where(lane < 3 * N_HEADS, pltpu.roll(imb, 2 * N_HEADS, axis=1), 0.0)))
    gcol_ref[...] = cols[:, 0:GATE_WIDTH]
    grow_ref[0] = imb.T[0:SUBLANES, :]


def _mlstm_proj(x, mod, g, w, bg):
    t, d = x.shape
    n = w.shape[1]
    tm = TM_PROJ
    tiles_per_batch = SEQ // tm
    return pl.pallas_call(
        _mlstm_proj_kernel,
        out_shape=(
            jax.ShapeDtypeStruct((t, QK), BF16),
            jax.ShapeDtypeStruct((t, QK), BF16),
            jax.ShapeDtypeStruct((t, D_MODEL), BF16),
            jax.ShapeDtypeStruct((t, D_MODEL), BF16),
            jax.ShapeDtypeStruct((t, GATE_WIDTH), F32),
            jax.ShapeDtypeStruct((BATCH, SUBLANES, SEQ), F32),
        ),
        grid=(t // tm,),
        in_specs=[
            pl.BlockSpec((tm, d), lambda i, *_: (i, 0)),
            pl.BlockSpec((1, 6, d), lambda i, *_: (i // tiles_per_batch, 0, 0)),
            pl.BlockSpec((1, d), lambda i, *_: (0, 0)),
            pl.BlockSpec((d, n), lambda i, *_: (0, 0)),
            pl.BlockSpec((1, GATE_COLS), lambda i, *_: (0, 0)),
        ],
        out_specs=(
            pl.BlockSpec((tm, QK), lambda i, *_: (i, 0)),
            pl.BlockSpec((tm, QK), lambda i, *_: (i, 0)),
            pl.BlockSpec((tm, D_MODEL), lambda i, *_: (i, 0)),
            pl.BlockSpec((tm, D_MODEL), lambda i, *_: (i, 0)),
            pl.BlockSpec((tm, GATE_WIDTH), lambda i, *_: (i, 0)),
            pl.BlockSpec((1, SUBLANES, tm),
                         lambda i, *_: (i // tiles_per_batch, 0, i % tiles_per_batch)),
        ),
        compiler_params=_params("arbitrary"),
        name="mlstm_proj",
    )(x, mod, g, w, bg)


def _mlstm_core_kernel(q_ref, k_ref, v_ref, og_ref, gcol_ref, grow_ref, ng_ref, rep_ref, o_ref,
                       m_ref, *c_refs):
    @pl.when(pl.program_id(0) == 0)
    def _():
        m_ref[...] = jnp.zeros_like(m_ref)
        for c_ref in c_refs:
            c_ref[...] = jnp.zeros_like(c_ref)

    ln = CHUNK
    row = lax.broadcasted_iota(jnp.int32, (ln, ln), 0)
    col = lax.broadcasted_iota(jnp.int32, (ln, ln), 1)
    causal = col <= row
    ones = jnp.ones((ln, LANES), BF16)
    n_streams = BATCH * N_HEADS
    m_all = [m_ref[st] for st in range(n_streams)]
    m_out = [None] * n_streams
    for bi in range(BATCH):
        g = gcol_ref[bi]
        g_hi = g.astype(BF16)
        g_r1 = g - g_hi.astype(F32)
        g_mid = g_r1.astype(BF16)
        g_lo = (g_r1 - g_mid.astype(F32)).astype(BF16)
        rep = _dot(g_hi, rep_ref[...]) + _dot(g_mid, rep_ref[...]) + _dot(g_lo, rep_ref[...])
        for h in range(N_HEADS):
            st = bi * N_HEADS + h
            lanes = lambda kind: slice((kind * N_HEADS + h) * LANES, (kind * N_HEADS + h + 1) * LANES)
            cm, b, imb_col = rep[:, lanes(0)], rep[:, lanes(1)], rep[:, lanes(2)]
            imb_row = grow_ref[bi, h:h + 1, :]
            qh = q_ref[bi, :, h * DH_QK:(h + 1) * DH_QK]
            kh = k_ref[bi, :, h * DH_QK:(h + 1) * DH_QK]
            v_ext = jnp.concatenate([v_ref[bi, :, h * DH_V:(h + 1) * DH_V], ones], axis=1)
            c_ref = c_refs[st]
            c_prev = c_ref[...]
            m_prev = m_all[st]

            big_m = jnp.maximum(m_prev, cm)
            d_mat = jnp.exp(jnp.where(causal, imb_row - big_m, -jnp.inf))
            s = lax.dot_general(qh, kh, (((1,), (1,)), ((), ())), preferred_element_type=F32) * d_mat
            q_inter = (qh.astype(F32) * jnp.exp(m_prev - big_m)).astype(BF16)
            lhs = jnp.concatenate([q_inter, s.astype(BF16)], axis=1)
            rhs = jnp.concatenate([c_prev.astype(BF16), v_ext], axis=0)
            nd = _dot(lhs, rhs)
            den = nd[:, DH_V:]
            inv = 1.0 / jnp.maximum(jnp.abs(den), jnp.exp(-(b + big_m)))
            hh = nd[:, :DH_V] * jnp.concatenate([inv, inv], axis=1)

            m_last = big_m[ln - 1:ln, :]
            kw = (kh.astype(F32) * jnp.exp(imb_col - m_last)).astype(BF16)
            decay = jnp.exp(m_prev - m_last)
            c_ref[...] = jnp.concatenate([decay] * 3, axis=1) * c_prev + lax.dot_general(
                kw, v_ext, (((0,), (0,)), ((), ())), preferred_element_type=F32)
            m_out[st] = b[ln - 1:ln, :] + m_last

            sl = slice(h * DH_V, (h + 1) * DH_V)
            gate = _sigmoid(og_ref[bi, :, sl].astype(F32))
            o_ref[bi, :, sl] = (_rms(hh) * ng_ref[:, sl] * gate).astype(BF16)
    for st in range(n_streams):
        m_ref[st] = m_out[st]


def _mlstm_core(q, k, v, og, gcol, grow, norm_g):
    nc = SEQ // CHUNK
    n_streams = BATCH * N_HEADS
    per_batch = lambda a: a.reshape(BATCH, SEQ, a.shape[-1])
    chunk_rows = lambda width: pl.BlockSpec((BATCH, CHUNK, width), lambda c: (0, c, 0))
    n_rep = GATE_KINDS * N_HEADS
    replicate = (jnp.arange(GATE_WIDTH)[:, None] == jnp.arange(n_rep * LANES)[None, :] // LANES
                 ).astype(BF16)
    out = pl.pallas_call(
        _mlstm_core_kernel,
        out_shape=jax.ShapeDtypeStruct((BATCH, SEQ, D_MODEL), BF16),
        grid=(nc,),
        in_specs=[
            chunk_rows(QK),
            chunk_rows(QK),
            chunk_rows(D_MODEL),
            chunk_rows(D_MODEL),
            chunk_rows(GATE_WIDTH),
            pl.BlockSpec((BATCH, SUBLANES, CHUNK), lambda c: (0, 0, c)),
            pl.BlockSpec((1, D_MODEL), lambda c: (0, 0)),
            pl.BlockSpec((GATE_WIDTH, n_rep * LANES), lambda c: (0, 0)),
        ],
        out_specs=chunk_rows(D_MODEL),
        scratch_shapes=[pltpu.VMEM((n_streams, 1, LANES), F32)]
        + [pltpu.VMEM((DH_QK, DH_V + LANES), F32) for _ in range(n_streams)],
        compiler_params=_params("arbitrary"),
        name="mlstm_core",
    )(per_batch(q), per_batch(k), per_batch(v), per_batch(og), per_batch(gcol), grow, norm_g,
      replicate)
    return out.reshape(TOKENS, D_MODEL)


def _top2_sum(v0, v1, v2, v3):
    hi1, lo1 = jnp.maximum(v0, v1), jnp.minimum(v0, v1)
    hi2, lo2 = jnp.maximum(v2, v3), jnp.minimum(v2, v3)
    return jnp.maximum(hi1, hi2) + jnp.maximum(jnp.minimum(hi1, hi2), jnp.maximum(lo1, lo2))


def _route_tail(x_new, m, g2_ref, wr_ref, br_ref, tri_ref, u2_ref, cls_ref, rank_ref, cnt_ref,
                run_ref):
    @pl.when(pl.program_id(0) == 0)
    def _():
        run_ref[...] = jnp.zeros_like(run_ref)

    u2 = _rms(x_new) * g2_ref[...] * (1.0 + m[4:5]) + m[3:4]
    _store_token_rows(u2_ref, u2, u2.shape[0])
    logits = _dot(u2.astype(BF16), wr_ref[...])
    lt = logits.T[0:N_EXPERTS, :]
    e = jnp.exp(lt - jnp.max(lt, axis=0, keepdims=True))
    probs = e / jnp.sum(e, axis=0, keepdims=True)
    sel = probs + br_ref[...]
    sel_rows = [sel[j:j + 1, :] for j in range(N_EXPERTS)]
    best = jnp.zeros_like(sel_rows[0], dtype=jnp.int32)
    best_score = _top2_sum(*sel_rows[0:EXPERTS_PER_GROUP])
    for g in range(1, N_GROUPS):
        score = _top2_sum(*sel_rows[g * EXPERTS_PER_GROUP:(g + 1) * EXPERTS_PER_GROUP])
        better = score > best_score
        best = jnp.where(better, g, best)
        best_score = jnp.where(better, score, best_score)
    s = []
    for j in range(EXPERTS_PER_GROUP):
        sj = sel_rows[j]
        for g in range(1, N_GROUPS):
            sj = jnp.where(best == g, sel_rows[g * EXPERTS_PER_GROUP + j], sj)
        s.append(sj)
    chosen = []
    for j in range(EXPERTS_PER_GROUP):
        beaten = jnp.zeros_like(best)
        for i in range(EXPERTS_PER_GROUP):
            if i == j:
                continue
            wins = (s[i] >= s[j]) if i < j else (s[i] > s[j])
            beaten = beaten + jnp.where(wins, 1, 0)
        chosen.append(beaten < 2)
    pair = jnp.full_like(best, len(PAIRS) - 1)
    for p in range(len(PAIRS) - 2, -1, -1):
        a, b = PAIRS[p]
        pair = jnp.where(jnp.logical_and(chosen[a], chosen[b]), p, pair)
    cls = best * len(PAIRS) + pair
    cls_ref[0] = cls

    class_id = lax.broadcasted_iota(jnp.int32, (CLASS_ROWS, cls.shape[1]), 0)
    onehot = class_id == cls
    before = _dot(jnp.where(onehot, 1.0, 0.0).astype(BF16), tri_ref[...])
    run = run_ref[...]
    rank = jnp.sum(jnp.where(onehot, before + run[:, 0:1], 0.0), axis=0, keepdims=True)
    rank_ref[0] = rank.astype(jnp.int32)
    run = run + jnp.sum(jnp.where(onehot, 1.0, 0.0), axis=1, keepdims=True)
    run_ref[...] = run
    cnt_ref[...] = run.astype(jnp.int32)


def _route_out_shapes(t, tm):
    return (
        jax.ShapeDtypeStruct((t, D_MODEL), F32),
        jax.ShapeDtypeStruct((t * ROW_TILE, LANES), F32),
        jax.ShapeDtypeStruct((t // tm, 1, tm), jnp.int32),
        jax.ShapeDtypeStruct((t // tm, 1, tm), jnp.int32),
        jax.ShapeDtypeStruct((CLASS_ROWS, LANES), jnp.int32),
    )


def _route_out_specs(tm):
    return (
        pl.BlockSpec((tm, D_MODEL), lambda i, *_: (i, 0)),
        pl.BlockSpec((tm * ROW_TILE, LANES), lambda i, *_: (i, 0)),
        pl.BlockSpec((1, 1, tm), lambda i, *_: (i, 0, 0)),
        pl.BlockSpec((1, 1, tm), lambda i, *_: (i, 0, 0)),
        pl.BlockSpec((CLASS_ROWS, LANES), lambda i, *_: (0, 0)),
    )


def _const_spec(shape):
    return pl.BlockSpec(shape, lambda i, *_: (0,) * len(shape))


def _mlstm_out_kernel(h_ref, x_ref, mod_ref, wo_ref, g2_ref, wr_ref, br_ref, tri_ref,
                      x1_ref, u2_ref, cls_ref, rank_ref, cnt_ref, run_ref):
    m = mod_ref[0]
    x1 = x_ref[...] + m[2:3] * _dot(h_ref[...], wo_ref[...])
    x1_ref[...] = x1
    _route_tail(x1, m, g2_ref, wr_ref, br_ref, tri_ref, u2_ref, cls_ref, rank_ref, cnt_ref, run_ref)


def _mlstm_out(hg, x, mod, w_out, g2, wr, br, tri):
    t = x.shape[0]
    tm = TM_MIX
    tiles_per_batch = SEQ // tm
    return pl.pallas_call(
        _mlstm_out_kernel,
        out_shape=_route_out_shapes(t, tm),
        grid=(t // tm,),
        in_specs=[
            pl.BlockSpec((tm, D_MODEL), lambda i, *_: (i, 0)),
            pl.BlockSpec((tm, D_MODEL), lambda i, *_: (i, 0)),
            pl.BlockSpec((1, 6, D_MODEL), lambda i, *_: (i // tiles_per_batch, 0, 0)),
            _const_spec((D_MODEL, D_MODEL)),
            _const_spec((1, D_MODEL)),
            _const_spec((D_MODEL, ROUTER_COLS)),
            _const_spec((N_EXPERTS, 1)),
            _const_spec((tm, tm)),
        ],
        out_specs=_route_out_specs(tm),
        scratch_shapes=[pltpu.VMEM((CLASS_ROWS, LANES), F32)],
        compiler_params=_params("arbitrary"),
        name="mlstm_out_route",
    )(hg, x, mod, w_out, g2, wr, br, tri)


def _conv_layer_kernel(dest_ref, x_ref, ys_ref, mod0_ref, mod_ref, g1_ref, wi_ref, cw_ref, cb_ref,
                       wo_ref, g2_ref, wr_ref, br_ref, tri_ref,
                       x3_ref, u2_ref, cls_ref, rank_ref, cnt_ref,
                       run_ref, carry_ref, ybuf_ref, ysem_ref):
    tm = x_ref.shape[0]
    tiles_per_batch = SEQ // tm

    @pl.when(pl.program_id(0) % tiles_per_batch == 0)
    def _():
        carry_ref[...] = jnp.zeros_like(carry_ref)

    m = mod_ref[0]
    y = _gathered_tile(dest_ref, ys_ref, ybuf_ref, ysem_ref, tm, issue_first=False)
    x2 = x_ref[...] + mod0_ref[0][5:6] * y
    u = (_rms(x2) * g1_ref[...] * (1.0 + m[1:2]) + m[0:1]).astype(BF16)
    bgate = _dot(u, wi_ref[:, 0:D_MODEL])
    z = _dot(u, wi_ref[:, D_MODEL:2 * D_MODEL]) * _dot(u, wi_ref[:, 2 * D_MODEL:])
    prev = carry_ref[...]
    row = lax.broadcasted_iota(jnp.int32, z.shape, 0)
    z1 = jnp.where(row == 0, prev[7:8], pltpu.roll(z, 1, axis=0))
    z2 = jnp.where(row == 0, prev[6:7], jnp.where(row == 1, prev[7:8], pltpu.roll(z, 2, axis=0)))
    carry_ref[...] = z[tm - SUBLANES:, :]
    cw = cw_ref[...]
    zc = cw[0:1] * z2 + cw[1:2] * z1 + cw[2:3] * z + cb_ref[...]
    x3 = x2 + m[2:3] * _dot((bgate * zc).astype(BF16), wo_ref[...])
    x3_ref[...] = x3
    _route_tail(x3, m, g2_ref, wr_ref, br_ref, tri_ref, u2_ref, cls_ref, rank_ref, cnt_ref, run_ref)
    _drain_row_gather(ys_ref, ybuf_ref, ysem_ref)


def _conv_layer(dest, x, ys, mod0, mod, g1, w_in, conv_w, conv_b, w_out, g2, wr, br, tri):
    t = x.shape[0]
    tm = TM_MIX
    tiles_per_batch = SEQ // tm
    mod_spec = pl.BlockSpec((1, 6, D_MODEL), lambda i, *_: (i // tiles_per_batch, 0, 0))
    return pl.pallas_call(
        _conv_layer_kernel,
        out_shape=_route_out_shapes(t, tm),
        grid_spec=pltpu.PrefetchScalarGridSpec(
            num_scalar_prefetch=1,
            grid=(t // tm,),
            in_specs=[
                pl.BlockSpec((tm, D_MODEL), lambda i, *_: (i, 0)),
                pl.BlockSpec(memory_space=pl.ANY),
                mod_spec,
                mod_spec,
                _const_spec((1, D_MODEL)),
                _const_spec((D_MODEL, 3 * D_MODEL)),
                _const_spec((3, D_MODEL)),
                _const_spec((1, D_MODEL)),
                _const_spec((D_MODEL, D_MODEL)),
                _const_spec((1, D_MODEL)),
                _const_spec((D_MODEL, ROUTER_COLS)),
                _const_spec((N_EXPERTS, 1)),
                _const_spec((tm, tm)),
            ],
            out_specs=_route_out_specs(tm),
            scratch_shapes=[
                pltpu.VMEM((CLASS_ROWS, LANES), F32),
                pltpu.VMEM((SUBLANES, D_MODEL), F32),
                pltpu.VMEM((2, tm * ROW_TILE, LANES), F32),
                pltpu.SemaphoreType.DMA((2,)),
            ],
        ),
        compiler_params=_params("arbitrary"),
        name="conv_layer_route",
    )(dest, x, ys, mod0, mod, g1, w_in, conv_w, conv_b, w_out, g2, wr, br, tri)


def _invert_kernel(dest_ref, zeros_ref, src_ref, sem):
    fill = pltpu.make_async_copy(zeros_ref, src_ref, sem)
    fill.start()
    fill.wait()
    unroll = 8

    def body(t8, carry):
        for k in range(unroll):
            t = t8 * unroll + k
            src_ref[dest_ref[t]] = t
        return carry

    lax.fori_loop(0, dest_ref.shape[0] // unroll, body, 0)


def _invert_permutation(dest, n_out):
    smem = pl.BlockSpec(memory_space=pltpu.SMEM)
    return pl.pallas_call(
        _invert_kernel,
        out_shape=jax.ShapeDtypeStruct((n_out,), jnp.int32),
        in_specs=[smem, pl.BlockSpec(memory_space=pl.ANY)],
        out_specs=smem,
        scratch_shapes=[pltpu.SemaphoreType.DMA(())],
        name="invert_permutation",
    )(dest, jnp.arange(n_out, dtype=jnp.int32) % dest.shape[0])


def _expert_kernel(ea_ref, eb_ref, chg_ref, nused_ref, src_ref, u2_ref, wga_ref, wua_ref, wda_ref,
                   wgb_ref, wub_ref, wdb_ref, wra_ref, wrb_ref, y_ref, xbuf_ref, xsem_ref, *wbf_refs):
    del ea_ref, eb_ref
    j = pl.program_id(0)
    tm = TM_EXPERT

    @pl.when(chg_ref[j] == 1)
    def _():
        for src, dst in zip((wga_ref, wua_ref, wda_ref, wgb_ref, wub_ref, wdb_ref), wbf_refs):
            dst[...] = src[0, 0].astype(BF16)

    rows = _gathered_tile(src_ref, u2_ref, xbuf_ref, xsem_ref, tm, issue_first=True)

    @pl.when(j < nused_ref[0])
    def _():
        xb = rows.astype(BF16)
        dl = jnp.sum(xb.astype(F32) * (wra_ref[0] - wrb_ref[0]), axis=-1, keepdims=True)
        w_a = _sigmoid(dl)
        w_b = _sigmoid(-dl)

        def ffn(wg_ref, wu_ref, wd_ref):
            gate = _dot(xb, wg_ref[...])
            hidden = gate * _sigmoid(gate) * _dot(xb, wu_ref[...])
            return _dot(hidden.astype(BF16), wd_ref[...])

        y = w_a * ffn(*wbf_refs[0:3]) + w_b * ffn(*wbf_refs[3:6])
        _store_token_rows(y_ref, y, tm)

    @pl.when(j >= nused_ref[0])
    def _():
        y_ref[...] = jnp.zeros_like(y_ref)

    _drain_row_gather(u2_ref, xbuf_ref, xsem_ref)


def _experts(layer, ea, eb, chg, n_used, src, u2, w_gate, w_up, w_down, wr_rows):
    tm = TM_EXPERT
    sel_a = lambda j, ea, eb, *_: (layer, ea[j], 0, 0)
    sel_b = lambda j, ea, eb, *_: (layer, eb[j], 0, 0)
    up_spec = lambda sel: pl.BlockSpec((1, 1, D_MODEL, D_EXPERT), sel)
    down_spec = lambda sel: pl.BlockSpec((1, 1, D_EXPERT, D_MODEL), sel)
    wr_spec = lambda sel: pl.BlockSpec((1, 1, D_MODEL), lambda *a: sel(*a)[1:])
    up_scratch = pltpu.VMEM((D_MODEL, D_EXPERT), BF16)
    down_scratch = pltpu.VMEM((D_EXPERT, D_MODEL), BF16)
    return pl.pallas_call(
        _expert_kernel,
        out_shape=jax.ShapeDtypeStruct((PADDED_ROWS * ROW_TILE, LANES), F32),
        grid_spec=pltpu.PrefetchScalarGridSpec(
            num_scalar_prefetch=4,
            grid=(N_EXPERT_TILES,),
            in_specs=[
                pl.BlockSpec(memory_space=pltpu.SMEM),
                pl.BlockSpec(memory_space=pl.ANY),
                up_spec(sel_a), up_spec(sel_a), down_spec(sel_a),
                up_spec(sel_b), up_spec(sel_b), down_spec(sel_b),
                wr_spec(sel_a), wr_spec(sel_b),
            ],
            out_specs=pl.BlockSpec((tm * ROW_TILE, LANES), lambda j, *_: (j, 0)),
            scratch_shapes=[pltpu.VMEM((EXPERT_GATHER_BUFFERS, tm * ROW_TILE, LANES), F32),
                            pltpu.SemaphoreType.DMA((EXPERT_GATHER_BUFFERS,)),
                            up_scratch, up_scratch, down_scratch, up_scratch, up_scratch, down_scratch],
        ),
        compiler_params=_params("arbitrary"),
        name="grouped_experts",
    )(ea, eb, chg, n_used, src, u2, w_gate, w_up, w_down, w_gate, w_up, w_down, wr_rows, wr_rows)


def _final_kernel(dest_ref, x_ref, ys_ref, mod_ref, g_ref, o_ref, ybuf_ref, ysem_ref):
    y = _gathered_tile(dest_ref, ys_ref, ybuf_ref, ysem_ref, x_ref.shape[0], issue_first=True)
    x = x_ref[...] + mod_ref[0][5:6] * y
    o_ref[...] = _rms(x) * g_ref[...]
    _drain_row_gather(ys_ref, ybuf_ref, ysem_ref)


def _final(dest, x, ys, mod, g):
    t = x.shape[0]
    tm = TM_FINAL
    tiles_per_batch = SEQ // tm
    return pl.pallas_call(
        _final_kernel,
        out_shape=jax.ShapeDtypeStruct((t, D_MODEL), F32),
        grid_spec=pltpu.PrefetchScalarGridSpec(
            num_scalar_prefetch=1,
            grid=(t // tm,),
            in_specs=[
                pl.BlockSpec((tm, D_MODEL), lambda i, *_: (i, 0)),
                pl.BlockSpec(memory_space=pl.ANY),
                pl.BlockSpec((1, 6, D_MODEL), lambda i, *_: (i // tiles_per_batch, 0, 0)),
                _const_spec((1, D_MODEL)),
            ],
            out_specs=pl.BlockSpec((tm, D_MODEL), lambda i, *_: (i, 0)),
            scratch_shapes=[pltpu.VMEM((2, tm * ROW_TILE, LANES), F32), pltpu.SemaphoreType.DMA((2,))],
        ),
        compiler_params=_params("arbitrary"),
        name="final_norm",
    )(dest, x, ys, mod, g)


_PAIR_A = np.array([EXPERTS_PER_GROUP * (c // len(PAIRS)) + PAIRS[c % len(PAIRS)][0]
                    for c in range(N_CLASSES)], np.int32)
_PAIR_B = np.array([EXPERTS_PER_GROUP * (c // len(PAIRS)) + PAIRS[c % len(PAIRS)][1]
                    for c in range(N_CLASSES)], np.int32)


def _moe(layer, u2, cls, rank, cnt, w_gate, w_up, w_down, wr_rows):
    counts = cnt[:N_CLASSES, 0]
    tiles = (counts + TM_EXPERT - 1) // TM_EXPERT
    tile_end = jnp.cumsum(tiles)
    row_start = (tile_end - tiles) * TM_EXPERT
    n_used = tile_end[-1:]
    cls = cls.reshape(-1)
    dest = row_start[cls] + rank.reshape(-1)
    tile_cls = jnp.sum(jnp.arange(N_EXPERT_TILES)[:, None] >= tile_end[None, :], axis=1)
    tile_cls = jnp.minimum(tile_cls, tile_cls[jnp.maximum(n_used[0] - 1, 0)])
    ea = jnp.asarray(_PAIR_A)[tile_cls]
    eb = jnp.asarray(_PAIR_B)[tile_cls]
    chg = jnp.concatenate([jnp.ones((1,), jnp.int32),
                           (tile_cls[1:] != tile_cls[:-1]).astype(jnp.int32)])
    src = _invert_permutation(dest, PADDED_ROWS)
    ys = _experts(layer, ea, eb, chg, n_used.astype(jnp.int32), src, u2, w_gate, w_up, w_down, wr_rows)
    return dest, ys


def kernel(x, c, norm1_g, norm2_g, w_ada, b_ada, m_w_in, m_b_gates, m_norm_g, m_w_out,
           c_w_in, c_conv_w, c_conv_b, c_w_out, w_router, b_router,
           e_w_gate, e_w_up, e_w_down, final_g):
    xf = x.reshape(TOKENS, D_MODEL)
    mod = _ada(c, w_ada, b_ada)

    w_in = jnp.concatenate(
        [m_w_in[0].astype(BF16),
         jnp.zeros((D_MODEL, GATE_COLS - 2 * N_HEADS), BF16)], axis=1)
    bg = jnp.zeros((1, GATE_COLS), F32).at[0, :2 * N_HEADS].set(m_b_gates[0])
    wr = jnp.zeros((D_MODEL, ROUTER_COLS), BF16).at[:, :N_EXPERTS].set(w_router.astype(BF16))
    wr_rows = w_router.astype(BF16).astype(F32).T.reshape(N_EXPERTS, 1, D_MODEL)
    br = b_router.reshape(N_EXPERTS, 1)
    tri = (jnp.arange(TM_MIX)[:, None] < jnp.arange(TM_MIX)[None, :]).astype(BF16)
    row = lambda v: v.reshape(1, -1)

    q, k, v, og, gcol, grow = _mlstm_proj(xf, mod[0], row(norm1_g[0]), w_in, bg)
    hg = _mlstm_core(q, k, v, og, gcol, grow, row(m_norm_g[0]))
    x1, u2, cls, rank, cnt = _mlstm_out(hg, xf, mod[0], m_w_out[0].astype(BF16), row(norm2_g[0]),
                                        wr, br, tri)
    dest, ys = _moe(0, u2, cls, rank, cnt, e_w_gate, e_w_up, e_w_down, wr_rows)

    x3, u2, cls, rank, cnt = _conv_layer(
        dest, x1, ys, mod[0], mod[1], row(norm1_g[1]), c_w_in[0].astype(BF16), c_conv_w[0],
        row(c_conv_b[0]), c_w_out[0].astype(BF16), row(norm2_g[1]), wr, br, tri)
    dest, ys = _moe(1, u2, cls, rank, cnt, e_w_gate, e_w_up, e_w_down, wr_rows)

    out = _final(dest, x3, ys, mod[1], row(final_g))
    return out.reshape(BATCH, SEQ, D_MODEL)
```

```python
import functools

import jax
import jax.numpy as jnp
import numpy as np
from jax import lax
from jax.experimental import pallas as pl
from jax.experimental.pallas import tpu as pltpu

F32 = jnp.float32
BF16 = jnp.bfloat16

D_MODEL = 1024
BATCH = 4
SEQ = 8192
TOKENS = BATCH * SEQ
N_HEADS = 4
DH_V = 256
DH_QK = 128
QK = N_HEADS * DH_QK
N_EXPERTS = 16
N_GROUPS = 4
EXPERTS_PER_GROUP = 4
D_EXPERT = 512
EPS = 1e-6

LANES = 128
SUBLANES = 8
VMEM_LIMIT_BYTES = 56 * 1024 * 1024

CHUNK = 128
TM_PROJ = 512
TM_MIX = 512
TM_EXPERT = 256
GATE_COLS = LANES
GATE_WIDTH = 16
GATE_KINDS = 3
ROUTER_COLS = LANES
PAIRS = ((0, 1), (0, 2), (0, 3), (1, 2), (1, 3), (2, 3))
N_CLASSES = N_GROUPS * len(PAIRS)
CLASS_ROWS = 32
N_EXPERT_TILES = TOKENS // TM_EXPERT + N_CLASSES
PADDED_ROWS = N_EXPERT_TILES * TM_EXPERT
TM_FINAL = 512
EXPERT_GATHER_BUFFERS = 3
STREAM_GROUP = 4


def _params(*semantics):
    return pltpu.CompilerParams(dimension_semantics=semantics, vmem_limit_bytes=VMEM_LIMIT_BYTES)


def _dot(a, b):
    return jnp.dot(a, b, preferred_element_type=F32)


def _rms(x):
    return x * lax.rsqrt(jnp.mean(x * x, axis=-1, keepdims=True) + EPS)


def _sigmoid(x):
    return 1.0 / (1.0 + jnp.exp(-x))


ROW_TILE = D_MODEL // LANES


def _load_token_rows(ref, n):
    return jnp.concatenate([ref[pl.ds(c, n, stride=ROW_TILE), :] for c in range(ROW_TILE)], axis=1)


def _store_token_rows(ref, val, n):
    for c in range(ROW_TILE):
        ref[pl.ds(c, n, stride=ROW_TILE), :] = val[:, c * LANES:(c + 1) * LANES]


def _token_tile(ref, t):
    return ref.at[pl.ds(pl.multiple_of(t * ROW_TILE, ROW_TILE), ROW_TILE)]


def _start_row_gather(idx_ref, base, n, src_hbm, buf, sem):
    for r in range(n):
        pltpu.make_async_copy(_token_tile(src_hbm, idx_ref[base + r]), _token_tile(buf, r),
                              sem).start(priority=r % 2)


def _wait_row_gather(src_hbm, buf, sem):
    pltpu.make_async_copy(src_hbm.at[pl.ds(0, buf.shape[0])], buf, sem).wait()


def _gathered_tile(idx_ref, src_hbm, buf_ref, sem_ref, n, issue_first):
    i = pl.program_id(0)
    last = pl.num_programs(0) - 1
    nb = buf_ref.shape[0]
    ahead = nb - 1

    @pl.when(i == 0)
    def _():
        for k in range(ahead):
            _start_row_gather(idx_ref, jnp.minimum(k, last) * n, n, src_hbm, buf_ref.at[k], sem_ref.at[k])

    nxt = (i + ahead) % nb
    start_next = functools.partial(
        _start_row_gather, idx_ref, jnp.minimum(i + ahead, last) * n, n, src_hbm, buf_ref.at[nxt],
        sem_ref.at[nxt])
    if issue_first:
        start_next()
    slot = i % nb
    _wait_row_gather(src_hbm, buf_ref.at[slot], sem_ref.at[slot])
    rows = _load_token_rows(buf_ref.at[slot], n)
    if issue_first is None:
        return rows, start_next
    if not issue_first:
        start_next()
    return rows


def _drain_row_gather(src_hbm, buf_ref, sem_ref):
    i = pl.program_id(0)
    nb = buf_ref.shape[0]

    @pl.when(i == pl.num_programs(0) - 1)
    def _():
        for k in range(1, nb):
            _wait_row_gather(src_hbm, buf_ref.at[(i + k) % nb], sem_ref.at[(i + k) % nb])


def _ada_kernel(c_ref, w_ref, b_ref, o_ref):
    c = c_ref[...]
    cond = c * _sigmoid(c)
    o_ref[0] = jnp.dot(cond, w_ref[0], preferred_element_type=F32,
                       precision=lax.Precision.HIGHEST) + b_ref[0]


def _ada(c, w_ada, b_ada):
    depth, d, n = w_ada.shape
    tn = 1536
    c8 = jnp.zeros((SUBLANES, d), F32).at[:BATCH].set(c)
    out = pl.pallas_call(
        _ada_kernel,
        out_shape=jax.ShapeDtypeStruct((depth, SUBLANES, n), F32),
        grid=(depth, n // tn),
        in_specs=[
            pl.BlockSpec((SUBLANES, d), lambda l, j: (0, 0)),
            pl.BlockSpec((1, d, tn), lambda l, j: (l, 0, j)),
            pl.BlockSpec((1, 1, tn), lambda l, j: (l, 0, j)),
        ],
        out_specs=pl.BlockSpec((1, SUBLANES, tn), lambda l, j: (l, 0, j)),
        compiler_params=_params("arbitrary", "arbitrary"),
        name="ada_mod",
    )(c8, w_ada, b_ada.reshape(depth, 1, n))
    return out[:, :BATCH].reshape(depth, BATCH, 6, d)


def _mlstm_proj_kernel(x_ref, mod_ref, g_ref, w_ref, bg_ref, q_ref, k_ref, v_ref, og_ref,
                       gcol_ref, grow_ref):
    x = x_ref[...]
    m = mod_ref[0]
    u = (_rms(x) * g_ref[...] * (1.0 + m[1:2]) + m[0:1]).astype(BF16)
    q_ref[...] = (_dot(u, w_ref[:, 0:QK]) * (DH_QK ** -0.5)).astype(BF16)
    k_ref[...] = _dot(u, w_ref[:, QK:2 * QK]).astype(BF16)
    v_ref[...] = _dot(u, w_ref[:, 2 * QK:2 * QK + D_MODEL]).astype(BF16)
    og_ref[...] = _dot(u, w_ref[:, 2 * QK + D_MODEL:2 * QK + 2 * D_MODEL]).astype(BF16)
    gt = _dot(u, w_ref[:, 2 * QK + 2 * D_MODEL:]) + bg_ref[...]
    lane = lax.broadcasted_iota(jnp.int32, gt.shape, 1)
    pos = lax.broadcasted_iota(jnp.int32, gt.shape, 0) % CHUNK
    log_f = jnp.minimum(gt, 0.0) - jnp.log(1.0 + jnp.exp(-jnp.abs(gt)))
    b = log_f
    shift = 1
    while shift < CHUNK:
        b = b + jnp.where(pos >= shift, pltpu.roll(b, shift, axis=0), 0.0)
        shift *= 2
    imb = gt - pltpu.roll(b, LANES - N_HEADS, axis=1)
    cm = imb
    shift = 1
    while shift < CHUNK:
        cm = jnp.maximum(cm, jnp.where(pos >= shift, pltpu.roll(cm, shift, axis=0), -jnp.inf))
        shift *= 2
    cols = jnp.where(lane < N_HEADS, cm,
                     jnp.where(lane < 2 * N_HEADS, b,
                               jnp.where(lane < 3 * N_HEADS, pltpu.roll(imb, 2 * N_HEADS, axis=1), 0.0)))
    gcol_ref[...] = cols[:, 0:GATE_WIDTH]
    grow_ref[0] = imb.T[0:SUBLANES, :]


def _mlstm_proj(x, mod, g, w, bg):
    t, d = x.shape
    n = w.shape[1]
    tm = TM_PROJ
    tiles_per_batch = SEQ // tm
    return pl.pallas_call(
        _mlstm_proj_kernel,
        out_shape=(
            jax.ShapeDtypeStruct((t, QK), BF16),
            jax.ShapeDtypeStruct((t, QK), BF16),
            jax.ShapeDtypeStruct((t, D_MODEL), BF16),
            jax.ShapeDtypeStruct((t, D_MODEL), BF16),
            jax.ShapeDtypeStruct((t, GATE_WIDTH), F32),
            jax.ShapeDtypeStruct((BATCH, SUBLANES, SEQ), F32),
        ),
        grid=(t // tm,),
        in_specs=[
            pl.BlockSpec((tm, d), lambda i, *_: (i, 0)),
            pl.BlockSpec((1, 6, d), lambda i, *_: (i // tiles_per_batch, 0, 0)),
            pl.BlockSpec((1, d), lambda i, *_: (0, 0)),
            pl.BlockSpec((d, n), lambda i, *_: (0, 0)),
            pl.BlockSpec((1, GATE_COLS), lambda i, *_: (0, 0)),
        ],
        out_specs=(
            pl.BlockSpec((tm, QK), lambda i, *_: (i, 0)),
            pl.BlockSpec((tm, QK), lambda i, *_: (i, 0)),
            pl.BlockSpec((tm, D_MODEL), lambda i, *_: (i, 0)),
            pl.BlockSpec((tm, D_MODEL), lambda i, *_: (i, 0)),
            pl.BlockSpec((tm, GATE_WIDTH), lambda i, *_: (i, 0)),
            pl.BlockSpec((1, SUBLANES, tm),
                         lambda i, *_: (i // tiles_per_batch, 0, i % tiles_per_batch)),
        ),
        compiler_params=_params("arbitrary"),
        name="mlstm_proj",
    )(x, mod, g, w, bg)


def _mlstm_core_kernel(q_ref, k_ref, v_ref, og_ref, gcol_ref, grow_ref, ng_ref, rep_ref, o_ref,
                       m_ref, *c_refs):
    @pl.when(pl.program_id(0) == 0)
    def _():
        m_ref[...] = jnp.zeros_like(m_ref)
        for c_ref in c_refs:
            c_ref[...] = jnp.zeros_like(c_ref)

    ln = CHUNK
    row = lax.broadcasted_iota(jnp.int32, (ln, ln), 0)
    col = lax.broadcasted_iota(jnp.int32, (ln, ln), 1)
    causal = col <= row
    ones = jnp.ones((ln, LANES), BF16)
    n_streams = BATCH * N_HEADS
    m_all = [m_ref[st] for st in range(n_streams)]
    m_out = [None] * n_streams
    for bi in range(BATCH):
        g = gcol_ref[bi]
        g_hi = g.astype(BF16)
        g_r1 = g - g_hi.astype(F32)
        g_mid = g_r1.astype(BF16)
        g_lo = (g_r1 - g_mid.astype(F32)).astype(BF16)
        rep = _dot(g_hi, rep_ref[...]) + _dot(g_mid, rep_ref[...]) + _dot(g_lo, rep_ref[...])
        for h in range(N_HEADS):
            st = bi * N_HEADS + h
            lanes = lambda kind: slice((kind * N_HEADS + h) * LANES, (kind * N_HEADS + h + 1) * LANES)
            cm, b, imb_col = rep[:, lanes(0)], rep[:, lanes(1)], rep[:, lanes(2)]
            imb_row = grow_ref[bi, h:h + 1, :]
            qh = q_ref[bi, :, h * DH_QK:(h + 1) * DH_QK]
            kh = k_ref[bi, :, h * DH_QK:(h + 1) * DH_QK]
            v_ext = jnp.concatenate([v_ref[bi, :, h * DH_V:(h + 1) * DH_V], ones], axis=1)
            c_ref = c_refs[st]
            c_prev = c_ref[...]
            m_prev = m_all[st]

            big_m = jnp.maximum(m_prev, cm)
            d_mat = jnp.exp(jnp.where(causal, imb_row - big_m, -jnp.inf))
            s = lax.dot_general(qh, kh, (((1,), (1,)), ((), ())), preferred_element_type=F32) * d_mat
            q_inter = (qh.astype(F32) * jnp.exp(m_prev - big_m)).astype(BF16)
            lhs = jnp.concatenate([q_inter, s.astype(BF16)], axis=1)
            rhs = jnp.concatenate([c_prev.astype(BF16), v_ext], axis=0)
            nd = _dot(lhs, rhs)
            den = nd[:, DH_V:]
            inv = 1.0 / jnp.maximum(jnp.abs(den), jnp.exp(-(b + big_m)))
            hh = nd[:, :DH_V] * jnp.concatenate([inv, inv], axis=1)

            m_last = big_m[ln - 1:ln, :]
            kw = (kh.astype(F32) * jnp.exp(imb_col - m_last)).astype(BF16)
            decay = jnp.exp(m_prev - m_last)
            c_ref[...] = jnp.concatenate([decay] * 3, axis=1) * c_prev + lax.dot_general(
                kw, v_ext, (((0,), (0,)), ((), ())), preferred_element_type=F32)
            m_out[st] = b[ln - 1:ln, :] + m_last

            sl = slice(h * DH_V, (h + 1) * DH_V)
            gate = _sigmoid(og_ref[bi, :, sl].astype(F32))
            o_ref[bi, :, sl] = (_rms(hh) * ng_ref[:, sl] * gate).astype(BF16)
    for st in range(n_streams):
        m_ref[st] = m_out[st]


def _mlstm_core(q, k, v, og, gcol, grow, norm_g):
    nc = SEQ // CHUNK
    n_streams = BATCH * N_HEADS
    per_batch = lambda a: a.reshape(BATCH, SEQ, a.shape[-1])
    chunk_rows = lambda width: pl.BlockSpec((BATCH, CHUNK, width), lambda c: (0, c, 0))
    n_rep = GATE_KINDS * N_HEADS
    replicate = (jnp.arange(GATE_WIDTH)[:, None] == jnp.arange(n_rep * LANES)[None, :] // LANES
                 ).astype(BF16)
    out = pl.pallas_call(
        _mlstm_core_kernel,
        out_shape=jax.ShapeDtypeStruct((BATCH, SEQ, D_MODEL), BF16),
        grid=(nc,),
        in_specs=[
            chunk_rows(QK),
            chunk_rows(QK),
            chunk_rows(D_MODEL),
            chunk_rows(D_MODEL),
            chunk_rows(GATE_WIDTH),
            pl.BlockSpec((BATCH, SUBLANES, CHUNK), lambda c: (0, 0, c)),
            pl.BlockSpec((1, D_MODEL), lambda c: (0, 0)),
            pl.BlockSpec((GATE_WIDTH, n_rep * LANES), lambda c: (0, 0)),
        ],
        out_specs=chunk_rows(D_MODEL),
        scratch_shapes=[pltpu.VMEM((n_streams, 1, LANES), F32)]
        + [pltpu.VMEM((DH_QK, DH_V + LANES), F32) for _ in range(n_streams)],
        compiler_params=_params("arbitrary"),
        name="mlstm_core",
    )(per_batch(q), per_batch(k), per_batch(v), per_batch(og), per_batch(gcol), grow, norm_g,
      replicate)
    return out.reshape(TOKENS, D_MODEL)


def _top2_sum(v0, v1, v2, v3):
    hi1, lo1 = jnp.maximum(v0, v1), jnp.minimum(v0, v1)
    hi2, lo2 = jnp.maximum(v2, v3), jnp.minimum(v2, v3)
    return jnp.maximum(hi1, hi2) + jnp.maximum(jnp.minimum(hi1, hi2), jnp.maximum(lo1, lo2))


def _route_tail(x_new, m, g2_ref, wr_ref, br_ref, tri_ref, u2_ref, cls_ref, rank_ref, cnt_ref,
                run_ref):
    @pl.when(pl.program_id(0) == 0)
    def _():
        run_ref[...] = jnp.zeros_like(run_ref)

    u2 = _rms(x_new) * g2_ref[...] * (1.0 + m[4:5]) + m[3:4]
    _store_token_rows(u2_ref, u2, u2.shape[0])
    logits = _dot(u2.astype(BF16), wr_ref[...])
    lt = logits.T[0:N_EXPERTS, :]
    e = jnp.exp(lt - jnp.max(lt, axis=0, keepdims=True))
    probs = e / jnp.sum(e, axis=0, keepdims=True)
    sel = probs + br_ref[...]
    sel_rows = [sel[j:j + 1, :] for j in range(N_EXPERTS)]
    best = jnp.zeros_like(sel_rows[0], dtype=jnp.int32)
    best_score = _top2_sum(*sel_rows[0:EXPERTS_PER_GROUP])
    for g in range(1, N_GROUPS):
        score = _top2_sum(*sel_rows[g * EXPERTS_PER_GROUP:(g + 1) * EXPERTS_PER_GROUP])
        better = score > best_score
        best = jnp.where(better, g, best)
        best_score = jnp.where(better, score, best_score)
    s = []
    for j in range(EXPERTS_PER_GROUP):
        sj = sel_rows[j]
        for g in range(1, N_GROUPS):
            sj = jnp.where(best == g, sel_rows[g * EXPERTS_PER_GROUP + j], sj)
        s.append(sj)
    chosen = []
    for j in range(EXPERTS_PER_GROUP):
        beaten = jnp.zeros_like(best)
        for i in range(EXPERTS_PER_GROUP):
            if i == j:
                continue
            wins = (s[i] >= s[j]) if i < j else (s[i] > s[j])
            beaten = beaten + jnp.where(wins, 1, 0)
        chosen.append(beaten < 2)
    pair = jnp.full_like(best, len(PAIRS) - 1)
    for p in range(len(PAIRS) - 2, -1, -1):
        a, b = PAIRS[p]
        pair = jnp.where(jnp.logical_and(chosen[a], chosen[b]), p, pair)
    cls = best * len(PAIRS) + pair
    cls_ref[0] = cls

    class_id = lax.broadcasted_iota(jnp.int32, (CLASS_ROWS, cls.shape[1]), 0)
    onehot = class_id == cls
    before = _dot(jnp.where(onehot, 1.0, 0.0).astype(BF16), tri_ref[...])
    run = run_ref[...]
    rank = jnp.sum(jnp.where(onehot, before + run[:, 0:1], 0.0), axis=0, keepdims=True)
    rank_ref[0] = rank.astype(jnp.int32)
    run = run + jnp.sum(jnp.where(onehot, 1.0, 0.0), axis=1, keepdims=True)
    run_ref[...] = run
    cnt_ref[...] = run.astype(jnp.int32)


def _route_out_shapes(t, tm):
    return (
        jax.ShapeDtypeStruct((t, D_MODEL), F32),
        jax.ShapeDtypeStruct((t * ROW_TILE, LANES), F32),
        jax.ShapeDtypeStruct((t // tm, 1, tm), jnp.int32),
        jax.ShapeDtypeStruct((t // tm, 1, tm), jnp.int32),
        jax.ShapeDtypeStruct((CLASS_ROWS, LANES), jnp.int32),
    )


def _route_out_specs(tm):
    return (
        pl.BlockSpec((tm, D_MODEL), lambda i, *_: (i, 0)),
        pl.BlockSpec((tm * ROW_TILE, LANES), lambda i, *_: (i, 0)),
        pl.BlockSpec((1, 1, tm), lambda i, *_: (i, 0, 0)),
        pl.BlockSpec((1, 1, tm), lambda i, *_: (i, 0, 0)),
        pl.BlockSpec((CLASS_ROWS, LANES), lambda i, *_: (0, 0)),
    )


def _const_spec(shape):
    return pl.BlockSpec(shape, lambda i, *_: (0,) * len(shape))


def _mlstm_out_kernel(h_ref, x_ref, mod_ref, wo_ref, g2_ref, wr_ref, br_ref, tri_ref,
                      x1_ref, u2_ref, cls_ref, rank_ref, cnt_ref, run_ref):
    m = mod_ref[0]
    x1 = x_ref[...] + m[2:3] * _dot(h_ref[...], wo_ref[...])
    x1_ref[...] = x1
    _route_tail(x1, m, g2_ref, wr_ref, br_ref, tri_ref, u2_ref, cls_ref, rank_ref, cnt_ref, run_ref)


def _mlstm_out(hg, x, mod, w_out, g2, wr, br, tri):
    t = x.shape[0]
    tm = TM_MIX
    tiles_per_batch = SEQ // tm
    return pl.pallas_call(
        _mlstm_out_kernel,
        out_shape=_route_out_shapes(t, tm),
        grid=(t // tm,),
        in_specs=[
            pl.BlockSpec((tm, D_MODEL), lambda i, *_: (i, 0)),
            pl.BlockSpec((tm, D_MODEL), lambda i, *_: (i, 0)),
            pl.BlockSpec((1, 6, D_MODEL), lambda i, *_: (i // tiles_per_batch, 0, 0)),
            _const_spec((D_MODEL, D_MODEL)),
            _const_spec((1, D_MODEL)),
            _const_spec((D_MODEL, ROUTER_COLS)),
            _const_spec((N_EXPERTS, 1)),
            _const_spec((tm, tm)),
        ],
        out_specs=_route_out_specs(tm),
        scratch_shapes=[pltpu.VMEM((CLASS_ROWS, LANES), F32)],
        compiler_params=_params("arbitrary"),
        name="mlstm_out_route",
    )(hg, x, mod, w_out, g2, wr, br, tri)


def _conv_layer_kernel(dest_ref, x_ref, ys_ref, mod0_ref, mod_ref, g1_ref, wi_ref, cw_ref, cb_ref,
                       wo_ref, g2_ref, wr_ref, br_ref, tri_ref,
                       x3_ref, u2_ref, cls_ref, rank_ref, cnt_ref,
                       run_ref, carry_ref, ybuf_ref, ysem_ref):
    tm = x_ref.shape[0]
    tiles_per_batch = SEQ // tm

    @pl.when(pl.program_id(0) % tiles_per_batch == 0)
    def _():
        carry_ref[...] = jnp.zeros_like(carry_ref)

    m = mod_ref[0]
    y = _gathered_tile(dest_ref, ys_ref, ybuf_ref, ysem_ref, tm, issue_first=False)
    x2 = x_ref[...] + mod0_ref[0][5:6] * y
    u = (_rms(x2) * g1_ref[...] * (1.0 + m[1:2]) + m[0:1]).astype(BF16)
    bgate = _dot(u, wi_ref[:, 0:D_MODEL])
    z = _dot(u, wi_ref[:, D_MODEL:2 * D_MODEL]) * _dot(u, wi_ref[:, 2 * D_MODEL:])
    prev = carry_ref[...]
    row = lax.broadcasted_iota(jnp.int32, z.shape, 0)
    z1 = jnp.where(row == 0, prev[7:8], pltpu.roll(z, 1, axis=0))
    z2 = jnp.where(row == 0, prev[6:7], jnp.where(row == 1, prev[7:8], pltpu.roll(z, 2, axis=0)))
    carry_ref[...] = z[tm - SUBLANES:, :]
    cw = cw_ref[...]
    zc = cw[0:1] * z2 + cw[1:2] * z1 + cw[2:3] * z + cb_ref[...]
    x3 = x2 + m[2:3] * _dot((bgate * zc).astype(BF16), wo_ref[...])
    x3_ref[...] = x3
    _route_tail(x3, m, g2_ref, wr_ref, br_ref, tri_ref, u2_ref, cls_ref, rank_ref, cnt_ref, run_ref)
    _drain_row_gather(ys_ref, ybuf_ref, ysem_ref)


def _conv_layer(dest, x, ys, mod0, mod, g1, w_in, conv_w, conv_b, w_out, g2, wr, br, tri):
    t = x.shape[0]
    tm = TM_MIX
    tiles_per_batch = SEQ // tm
    mod_spec = pl.BlockSpec((1, 6, D_MODEL), lambda i, *_: (i // tiles_per_batch, 0, 0))
    return pl.pallas_call(
        _conv_layer_kernel,
        out_shape=_route_out_shapes(t, tm),
        grid_spec=pltpu.PrefetchScalarGridSpec(
            num_scalar_prefetch=1,
            grid=(t // tm,),
            in_specs=[
                pl.BlockSpec((tm, D_MODEL), lambda i, *_: (i, 0)),
                pl.BlockSpec(memory_space=pl.ANY),
                mod_spec,
                mod_spec,
                _const_spec((1, D_MODEL)),
                _const_spec((D_MODEL, 3 * D_MODEL)),
                _const_spec((3, D_MODEL)),
                _const_spec((1, D_MODEL)),
                _const_spec((D_MODEL, D_MODEL)),
                _const_spec((1, D_MODEL)),
                _const_spec((D_MODEL, ROUTER_COLS)),
                _const_spec((N_EXPERTS, 1)),
                _const_spec((tm, tm)),
            ],
            out_specs=_route_out_specs(tm),
            scratch_shapes=[
                pltpu.VMEM((CLASS_ROWS, LANES), F32),
                pltpu.VMEM((SUBLANES, D_MODEL), F32),
                pltpu.VMEM((2, tm * ROW_TILE, LANES), F32),
                pltpu.SemaphoreType.DMA((2,)),
            ],
        ),
        compiler_params=_params("arbitrary"),
        name="conv_layer_route",
    )(dest, x, ys, mod0, mod, g1, w_in, conv_w, conv_b, w_out, g2, wr, br, tri)


def _invert_kernel(dest_ref, fill_ref, src_ref, sem):
    fill = pltpu.make_async_copy(fill_ref, src_ref, sem)
    fill.start()
    fill.wait()
    unroll = 8

    def body(t8, carry):
        for k in range(unroll):
            t = t8 * unroll + k
            src_ref[dest_ref[t]] = t
        return carry

    lax.fori_loop(0, dest_ref.shape[0] // unroll, body, 0)


def _invert_permutation(dest, n_out):
    smem = pl.BlockSpec(memory_space=pltpu.SMEM)
    return pl.pallas_call(
        _invert_kernel,
        out_shape=jax.ShapeDtypeStruct((n_out,), jnp.int32),
        in_specs=[smem, pl.BlockSpec(memory_space=pl.ANY)],
        out_specs=smem,
        scratch_shapes=[pltpu.SemaphoreType.DMA(())],
        name="invert_permutation",
    )(dest, jnp.arange(n_out, dtype=jnp.int32) % dest.shape[0])


def _expert_kernel(ea_ref, eb_ref, chg_ref, nused_ref, src_ref, u2_ref, wga_ref, wua_ref, wda_ref,
                   wgb_ref, wub_ref, wdb_ref, wra_ref, wrb_ref, y_ref, xbuf_ref, xsem_ref, *wbf_refs):
    del ea_ref, eb_ref
    j = pl.program_id(0)
    tm = TM_EXPERT

    @pl.when(chg_ref[j] == 1)
    def _():
        for src, dst in zip((wga_ref, wua_ref, wda_ref, wgb_ref, wub_ref, wdb_ref), wbf_refs):
            dst[...] = src[0, 0].astype(BF16)

    rows, start_next = _gathered_tile(src_ref, u2_ref, xbuf_ref, xsem_ref, tm, issue_first=None)

    @pl.when(j < nused_ref[0])
    def _():
        start_next()
        xb = rows.astype(BF16)
        dl = jnp.sum(xb.astype(F32) * (wra_ref[0] - wrb_ref[0]), axis=-1, keepdims=True)
        w_a = _sigmoid(dl)
        w_b = _sigmoid(-dl)

        def ffn(wg_ref, wu_ref, wd_ref):
            gate = _dot(xb, wg_ref[...])
            hidden = gate * _sigmoid(gate) * _dot(xb, wu_ref[...])
            return _dot(hidden.astype(BF16), wd_ref[...])

        y = w_a * ffn(*wbf_refs[0:3]) + w_b * ffn(*wbf_refs[3:6])
        _store_token_rows(y_ref, y, tm)

    @pl.when(j >= nused_ref[0])
    def _():
        start_next()
        y_ref[...] = jnp.zeros_like(y_ref)

    _drain_row_gather(u2_ref, xbuf_ref, xsem_ref)


def _experts(layer, ea, eb, chg, n_used, src, u2, w_gate, w_up, w_down, wr_rows):
    tm = TM_EXPERT
    sel_a = lambda j, ea, eb, *_: (layer, ea[j], 0, 0)
    sel_b = lambda j, ea, eb, *_: (layer, eb[j], 0, 0)
    up_spec = lambda sel: pl.BlockSpec((1, 1, D_MODEL, D_EXPERT), sel)
    down_spec = lambda sel: pl.BlockSpec((1, 1, D_EXPERT, D_MODEL), sel)
    wr_spec = lambda sel: pl.BlockSpec((1, 1, D_MODEL), lambda *a: sel(*a)[1:])
    up_scratch = pltpu.VMEM((D_MODEL, D_EXPERT), BF16)
    down_scratch = pltpu.VMEM((D_EXPERT, D_MODEL), BF16)
    return pl.pallas_call(
        _expert_kernel,
        out_shape=jax.ShapeDtypeStruct((PADDED_ROWS * ROW_TILE, LANES), F32),
        grid_spec=pltpu.PrefetchScalarGridSpec(
            num_scalar_prefetch=4,
            grid=(N_EXPERT_TILES,),
            in_specs=[
                pl.BlockSpec(memory_space=pltpu.SMEM),
                pl.BlockSpec(memory_space=pl.ANY),
                up_spec(sel_a), up_spec(sel_a), down_spec(sel_a),
                up_spec(sel_b), up_spec(sel_b), down_spec(sel_b),
                wr_spec(sel_a), wr_spec(sel_b),
            ],
            out_specs=pl.BlockSpec((tm * ROW_TILE, LANES), lambda j, *_: (j, 0)),
            scratch_shapes=[pltpu.VMEM((EXPERT_GATHER_BUFFERS, tm * ROW_TILE, LANES), F32),
                            pltpu.SemaphoreType.DMA((EXPERT_GATHER_BUFFERS,)),
                            up_scratch, up_scratch, down_scratch, up_scratch, up_scratch, down_scratch],
        ),
        compiler_params=_params("arbitrary"),
        name="grouped_experts",
    )(ea, eb, chg, n_used, src, u2, w_gate, w_up, w_down, w_gate, w_up, w_down, wr_rows, wr_rows)


def _final_kernel(dest_ref, x_ref, ys_ref, mod_ref, g_ref, o_ref, ybuf_ref, ysem_ref):
    y = _gathered_tile(dest_ref, ys_ref, ybuf_ref, ysem_ref, x_ref.shape[0], issue_first=True)
    x = x_ref[...] + mod_ref[0][5:6] * y
    o_ref[...] = _rms(x) * g_ref[...]
    _drain_row_gather(ys_ref, ybuf_ref, ysem_ref)


def _final(dest, x, ys, mod, g):
    t = x.shape[0]
    tm = TM_FINAL
    tiles_per_batch = SEQ // tm
    return pl.pallas_call(
        _final_kernel,
        out_shape=jax.ShapeDtypeStruct((t, D_MODEL), F32),
        grid_spec=pltpu.PrefetchScalarGridSpec(
            num_scalar_prefetch=1,
            grid=(t // tm,),
            in_specs=[
                pl.BlockSpec((tm, D_MODEL), lambda i, *_: (i, 0)),
                pl.BlockSpec(memory_space=pl.ANY),
                pl.BlockSpec((1, 6, D_MODEL), lambda i, *_: (i // tiles_per_batch, 0, 0)),
                _const_spec((1, D_MODEL)),
            ],
            out_specs=pl.BlockSpec((tm, D_MODEL), lambda i, *_: (i, 0)),
            scratch_shapes=[pltpu.VMEM((2, tm * ROW_TILE, LANES), F32), pltpu.SemaphoreType.DMA((2,))],
        ),
        compiler_params=_params("arbitrary"),
        name="final_norm",
    )(dest, x, ys, mod, g)


_PAIR_A = np.array([EXPERTS_PER_GROUP * (c // len(PAIRS)) + PAIRS[c % len(PAIRS)][0]
                    for c in range(N_CLASSES)], np.int32)
_PAIR_B = np.array([EXPERTS_PER_GROUP * (c // len(PAIRS)) + PAIRS[c % len(PAIRS)][1]
                    for c in range(N_CLASSES)], np.int32)


def _moe(layer, u2, cls, rank, cnt, w_gate, w_up, w_down, wr_rows):
    counts = cnt[:N_CLASSES, 0]
    tiles = (counts + TM_EXPERT - 1) // TM_EXPERT
    tile_end = jnp.cumsum(tiles)
    row_start = (tile_end - tiles) * TM_EXPERT
    n_used = tile_end[-1:]
    cls = cls.reshape(-1)
    dest = row_start[cls] + rank.reshape(-1)
    tile_cls = jnp.sum(jnp.arange(N_EXPERT_TILES)[:, None] >= tile_end[None, :], axis=1)
    tile_cls = jnp.minimum(tile_cls, tile_cls[jnp.maximum(n_used[0] - 1, 0)])
    ea = jnp.asarray(_PAIR_A)[tile_cls]
    eb = jnp.asarray(_PAIR_B)[tile_cls]
    chg = jnp.concatenate([jnp.ones((1,), jnp.int32),
                           (tile_cls[1:] != tile_cls[:-1]).astype(jnp.int32)])
    src = _invert_permutation(dest, PADDED_ROWS)
    ys = _experts(layer, ea, eb, chg, n_used.astype(jnp.int32), src, u2, w_gate, w_up, w_down, wr_rows)
    return dest, ys


def kernel(x, c, norm1_g, norm2_g, w_ada, b_ada, m_w_in, m_b_gates, m_norm_g, m_w_out,
           c_w_in, c_conv_w, c_conv_b, c_w_out, w_router, b_router,
           e_w_gate, e_w_up, e_w_down, final_g):
    xf = x.reshape(TOKENS, D_MODEL)
    mod = _ada(c, w_ada, b_ada)

    w_in = jnp.concatenate(
        [m_w_in[0].astype(BF16),
         jnp.zeros((D_MODEL, GATE_COLS - 2 * N_HEADS), BF16)], axis=1)
    bg = jnp.zeros((1, GATE_COLS), F32).at[0, :2 * N_HEADS].set(m_b_gates[0])
    wr = jnp.zeros((D_MODEL, ROUTER_COLS), BF16).at[:, :N_EXPERTS].set(w_router.astype(BF16))
    wr_rows = w_router.astype(BF16).astype(F32).T.reshape(N_EXPERTS, 1, D_MODEL)
    br = b_router.reshape(N_EXPERTS, 1)
    tri = (jnp.arange(TM_MIX)[:, None] < jnp.arange(TM_MIX)[None, :]).astype(BF16)
    row = lambda v: v.reshape(1, -1)

    q, k, v, og, gcol, grow = _mlstm_proj(xf, mod[0], row(norm1_g[0]), w_in, bg)
    hg = _mlstm_core(q, k, v, og, gcol, grow, row(m_norm_g[0]))
    x1, u2, cls, rank, cnt = _mlstm_out(hg, xf, mod[0], m_w_out[0].astype(BF16), row(norm2_g[0]),
                                        wr, br, tri)
    dest, ys = _moe(0, u2, cls, rank, cnt, e_w_gate, e_w_up, e_w_down, wr_rows)

    x3, u2, cls, rank, cnt = _conv_layer(
        dest, x1, ys, mod[0], mod[1], row(norm1_g[1]), c_w_in[0].astype(BF16), c_conv_w[0],
        row(c_conv_b[0]), c_w_out[0].astype(BF16), row(norm2_g[1]), wr, br, tri)
    dest, ys = _moe(1, u2, cls, rank, cnt, e_w_gate, e_w_up, e_w_down, wr_rows)

    out = _final(dest, x3, ys, mod[1], row(final_g))
    return out.reshape(BATCH, SEQ, D_MODEL)
```

```python
import functools

import jax
import jax.numpy as jnp
import numpy as np
from jax import lax
from jax.experimental import pallas as pl
from jax.experimental.pallas import tpu as pltpu

F32 = jnp.float32
BF16 = jnp.bfloat16

D_MODEL = 1024
BATCH = 4
SEQ = 8192
TOKENS = BATCH * SEQ
N_HEADS = 4
DH_V = 256
DH_QK = 128
QK = N_HEADS * DH_QK
N_EXPERTS = 16
N_GROUPS = 4
EXPERTS_PER_GROUP = 4
D_EXPERT = 512
EPS = 1e-6

LANES = 128
SUBLANES = 8
VMEM_LIMIT_BYTES = 56 * 1024 * 1024

CHUNK = 128
TM_PROJ = 1024
TM_OUT = 1024
TM_CONV = 512
TM_EXPERT = 512
GATE_COLS = LANES
GATE_WIDTH = 16
GATE_KINDS = 3
ROUTER_COLS = LANES
PAIRS = ((0, 1), (0, 2), (0, 3), (1, 2), (1, 3), (2, 3))
N_CLASSES = N_GROUPS * len(PAIRS)
CLASS_ROWS = 32
N_EXPERT_TILES = TOKENS // TM_EXPERT + N_CLASSES
PADDED_ROWS = N_EXPERT_TILES * TM_EXPERT
TM_FINAL = 512
EXPERT_GATHER_BUFFERS = 3
STREAM_GROUP = 4


def _params(*semantics):
    return pltpu.CompilerParams(dimension_semantics=semantics, vmem_limit_bytes=VMEM_LIMIT_BYTES)


def _dot(a, b):
    return jnp.dot(a, b, preferred_element_type=F32)


def _rms(x):
    return x * lax.rsqrt(jnp.mean(x * x, axis=-1, keepdims=True) + EPS)


def _sigmoid(x):
    return 1.0 / (1.0 + jnp.exp(-x))


ROW_TILE = D_MODEL // LANES


def _load_token_rows(ref, n):
    return jnp.concatenate([ref[pl.ds(c, n, stride=ROW_TILE), :] for c in range(ROW_TILE)], axis=1)


def _store_token_rows(ref, val, n):
    for c in range(ROW_TILE):
        ref[pl.ds(c, n, stride=ROW_TILE), :] = val[:, c * LANES:(c + 1) * LANES]


def _token_tile(ref, t):
    return ref.at[pl.ds(pl.multiple_of(t * ROW_TILE, ROW_TILE), ROW_TILE)]


def _start_row_gather(idx_ref, base, n, src_hbm, buf, sem):
    for r in range(n):
        pltpu.make_async_copy(_token_tile(src_hbm, idx_ref[base + r]), _token_tile(buf, r),
                              sem).start(priority=r % 2)


def _wait_row_gather(src_hbm, buf, sem):
    pltpu.make_async_copy(src_hbm.at[pl.ds(0, buf.shape[0])], buf, sem).wait()


def _gathered_tile(idx_ref, src_hbm, buf_ref, sem_ref, n, issue_first):
    i = pl.program_id(0)
    last = pl.num_programs(0) - 1
    nb = buf_ref.shape[0]
    ahead = nb - 1

    @pl.when(i == 0)
    def _():
        for k in range(ahead):
            _start_row_gather(idx_ref, jnp.minimum(k, last) * n, n, src_hbm, buf_ref.at[k], sem_ref.at[k])

    nxt = (i + ahead) % nb
    start_next = functools.partial(
        _start_row_gather, idx_ref, jnp.minimum(i + ahead, last) * n, n, src_hbm, buf_ref.at[nxt],
        sem_ref.at[nxt])
    if issue_first:
        start_next()
    slot = i % nb
    _wait_row_gather(src_hbm, buf_ref.at[slot], sem_ref.at[slot])
    rows = _load_token_rows(buf_ref.at[slot], n)
    if issue_first is None:
        return rows, start_next
    if not issue_first:
        start_next()
    return rows


def _drain_row_gather(src_hbm, buf_ref, sem_ref):
    i = pl.program_id(0)
    nb = buf_ref.shape[0]

    @pl.when(i == pl.num_programs(0) - 1)
    def _():
        for k in range(1, nb):
            _wait_row_gather(src_hbm, buf_ref.at[(i + k) % nb], sem_ref.at[(i + k) % nb])


def _ada_kernel(c_ref, w_ref, b_ref, o_ref):
    c = c_ref[...]
    cond = c * _sigmoid(c)
    o_ref[0] = jnp.dot(cond, w_ref[0], preferred_element_type=F32,
                       precision=lax.Precision.HIGHEST) + b_ref[0]


def _ada(c, w_ada, b_ada):
    depth, d, n = w_ada.shape
    tn = 1536
    c8 = jnp.zeros((SUBLANES, d), F32).at[:BATCH].set(c)
    out = pl.pallas_call(
        _ada_kernel,
        out_shape=jax.ShapeDtypeStruct((depth, SUBLANES, n), F32),
        grid=(depth, n // tn),
        in_specs=[
            pl.BlockSpec((SUBLANES, d), lambda l, j: (0, 0)),
            pl.BlockSpec((1, d, tn), lambda l, j: (l, 0, j)),
            pl.BlockSpec((1, 1, tn), lambda l, j: (l, 0, j)),
        ],
        out_specs=pl.BlockSpec((1, SUBLANES, tn), lambda l, j: (l, 0, j)),
        compiler_params=_params("arbitrary", "arbitrary"),
        name="ada_mod",
    )(c8, w_ada, b_ada.reshape(depth, 1, n))
    return out[:, :BATCH].reshape(depth, BATCH, 6, d)


def _mlstm_proj_kernel(x_ref, mod_ref, g_ref, w_ref, bg_ref, q_ref, k_ref, v_ref, og_ref,
                       gcol_ref, grow_ref):
    x = x_ref[...]
    m = mod_ref[0]
    u = (_rms(x) * g_ref[...] * (1.0 + m[1:2]) + m[0:1]).astype(BF16)
    q_ref[...] = (_dot(u, w_ref[:, 0:QK]) * (DH_QK ** -0.5)).astype(BF16)
    k_ref[...] = _dot(u, w_ref[:, QK:2 * QK]).astype(BF16)
    v_ref[...] = _dot(u, w_ref[:, 2 * QK:2 * QK + D_MODEL]).astype(BF16)
    og_ref[...] = _dot(u, w_ref[:, 2 * QK + D_MODEL:2 * QK + 2 * D_MODEL]).astype(BF16)
    gt = _dot(u, w_ref[:, 2 * QK + 2 * D_MODEL:]) + bg_ref[...]
    lane = lax.broadcasted_iota(jnp.int32, gt.shape, 1)
    pos = lax.broadcasted_iota(jnp.int32, gt.shape, 0) % CHUNK
    log_f = jnp.minimum(gt, 0.0) - jnp.log(1.0 + jnp.exp(-jnp.abs(gt)))
    b = log_f
    shift = 1
    while shift < CHUNK:
        b = b + jnp.where(pos >= shift, pltpu.roll(b, shift, axis=0), 0.0)
        shift *= 2
    imb = gt - pltpu.roll(b, LANES - N_HEADS, axis=1)
    cm = imb
    shift = 1
    while shift < CHUNK:
        cm = jnp.maximum(cm, jnp.where(pos >= shift, pltpu.roll(cm, shift, axis=0), -jnp.inf))
        shift *= 2
    cols = jnp.where(lane < N_HEADS, cm,
                     jnp.where(lane < 2 * N_HEADS, b,
                               jnp.where(lane < 3 * N_HEADS, pltpu.roll(imb, 2 * N_HEADS, axis=1), 0.0)))
    gcol_ref[...] = cols[:, 0:GATE_WIDTH]
    grow_ref[0] = imb.T[0:SUBLANES, :]


def _mlstm_proj(x, mod, g, w, bg):
    t, d = x.shape
    n = w.shape[1]
    tm = TM_PROJ
    tiles_per_batch = SEQ // tm
    return pl.pallas_call(
        _mlstm_proj_kernel,
        out_shape=(
            jax.ShapeDtypeStruct((t, QK), BF16),
            jax.ShapeDtypeStruct((t, QK), BF16),
            jax.ShapeDtypeStruct((t, D_MODEL), BF16),
            jax.ShapeDtypeStruct((t, D_MODEL), BF16),
            jax.ShapeDtypeStruct((t, GATE_WIDTH), F32),
            jax.ShapeDtypeStruct((BATCH, SUBLANES, SEQ), F32),
        ),
        grid=(t // tm,),
        in_specs=[
            pl.BlockSpec((tm, d), lambda i, *_: (i, 0)),
            pl.BlockSpec((1, 6, d), lambda i, *_: (i // tiles_per_batch, 0, 0)),
            pl.BlockSpec((1, d), lambda i, *_: (0, 0)),
            pl.BlockSpec((d, n), lambda i, *_: (0, 0)),
            pl.BlockSpec((1, GATE_COLS), lambda i, *_: (0, 0)),
        ],
        out_specs=(
            pl.BlockSpec((tm, QK), lambda i, *_: (i, 0)),
            pl.BlockSpec((tm, QK), lambda i, *_: (i, 0)),
            pl.BlockSpec((tm, D_MODEL), lambda i, *_: (i, 0)),
            pl.BlockSpec((tm, D_MODEL), lambda i, *_: (i, 0)),
            pl.BlockSpec((tm, GATE_WIDTH), lambda i, *_: (i, 0)),
            pl.BlockSpec((1, SUBLANES, tm),
                         lambda i, *_: (i // tiles_per_batch, 0, i % tiles_per_batch)),
        ),
        compiler_params=_params("arbitrary"),
        name="mlstm_proj",
    )(x, mod, g, w, bg)


def _mlstm_core_kernel(q_ref, k_ref, v_ref, og_ref, gcol_ref, grow_ref, ng_ref, rep_ref, o_ref,
                       m_ref, *c_refs):
    @pl.when(pl.program_id(0) == 0)
    def _():
        m_ref[...] = jnp.zeros_like(m_ref)
        for c_ref in c_refs:
            c_ref[...] = jnp.zeros_like(c_ref)

    ln = CHUNK
    row = lax.broadcasted_iota(jnp.int32, (ln, ln), 0)
    col = lax.broadcasted_iota(jnp.int32, (ln, ln), 1)
    causal = col <= row
    ones = jnp.ones((ln, LANES), BF16)
    n_streams = BATCH * N_HEADS
    m_all = [m_ref[st] for st in range(n_streams)]
    m_out = [None] * n_streams
    for bi in range(BATCH):
        g = gcol_ref[bi]
        g_hi = g.astype(BF16)
        g_r1 = g - g_hi.astype(F32)
        g_mid = g_r1.astype(BF16)
        g_lo = (g_r1 - g_mid.astype(F32)).astype(BF16)
        rep = _dot(g_hi, rep_ref[...]) + _dot(g_mid, rep_ref[...]) + _dot(g_lo, rep_ref[...])
        for h in range(N_HEADS):
            st = bi * N_HEADS + h
            lanes = lambda kind: slice((kind * N_HEADS + h) * LANES, (kind * N_HEADS + h + 1) * LANES)
            cm, b, imb_col = rep[:, lanes(0)], rep[:, lanes(1)], rep[:, lanes(2)]
            imb_row = grow_ref[bi, h:h + 1, :]
            qh = q_ref[bi, :, h * DH_QK:(h + 1) * DH_QK]
            kh = k_ref[bi, :, h * DH_QK:(h + 1) * DH_QK]
            v_ext = jnp.concatenate([v_ref[bi, :, h * DH_V:(h + 1) * DH_V], ones], axis=1)
            c_ref = c_refs[st]
            c_prev = c_ref[...]
            m_prev = m_all[st]

            big_m = jnp.maximum(m_prev, cm)
            d_mat = jnp.exp(jnp.where(causal, imb_row - big_m, -jnp.inf))
            s = lax.dot_general(qh, kh, (((1,), (1,)), ((), ())), preferred_element_type=F32) * d_mat
            q_inter = (qh.astype(F32) * jnp.exp(m_prev - big_m)).astype(BF16)
            lhs = jnp.concatenate([q_inter, s.astype(BF16)], axis=1)
            rhs = jnp.concatenate([c_prev.astype(BF16), v_ext], axis=0)
            nd = _dot(lhs, rhs)
            den = nd[:, DH_V:]
            inv = 1.0 / jnp.maximum(jnp.abs(den), jnp.exp(-(b + big_m)))
            hh = nd[:, :DH_V] * jnp.concatenate([inv, inv], axis=1)

            m_last = big_m[ln - 1:ln, :]
            kw = (kh.astype(F32) * jnp.exp(imb_col - m_last)).astype(BF16)
            decay = jnp.exp(m_prev - m_last)
            c_ref[...] = jnp.concatenate([decay] * 3, axis=1) * c_prev + lax.dot_general(
                kw, v_ext, (((0,), (0,)), ((), ())), preferred_element_type=F32)
            m_out[st] = b[ln - 1:ln, :] + m_last

            sl = slice(h * DH_V, (h + 1) * DH_V)
            gate = _sigmoid(og_ref[bi, :, sl].astype(F32))
            o_ref[bi, :, sl] = (_rms(hh) * ng_ref[:, sl] * gate).astype(BF16)
    for st in range(n_streams):
        m_ref[st] = m_out[st]


def _mlstm_core(q, k, v, og, gcol, grow, norm_g):
    nc = SEQ // CHUNK
    n_streams = BATCH * N_HEADS
    per_batch = lambda a: a.reshape(BATCH, SEQ, a.shape[-1])
    chunk_rows = lambda width: pl.BlockSpec((BATCH, CHUNK, width), lambda c: (0, c, 0))
    n_rep = GATE_KINDS * N_HEADS
    replicate = (jnp.arange(GATE_WIDTH)[:, None] == jnp.arange(n_rep * LANES)[None, :] // LANES
                 ).astype(BF16)
    out = pl.pallas_call(
        _mlstm_core_kernel,
        out_shape=jax.ShapeDtypeStruct((BATCH, SEQ, D_MODEL), BF16),
        grid=(nc,),
        in_specs=[
            chunk_rows(QK),
            chunk_rows(QK),
            chunk_rows(D_MODEL),
            chunk_rows(D_MODEL),
            chunk_rows(GATE_WIDTH),
            pl.BlockSpec((BATCH, SUBLANES, CHUNK), lambda c: (0, 0, c)),
            pl.BlockSpec((1, D_MODEL), lambda c: (0, 0)),
            pl.BlockSpec((GATE_WIDTH, n_rep * LANES), lambda c: (0, 0)),
        ],
        out_specs=chunk_rows(D_MODEL),
        scratch_shapes=[pltpu.VMEM((n_streams, 1, LANES), F32)]
        + [pltpu.VMEM((DH_QK, DH_V + LANES), F32) for _ in range(n_streams)],
        compiler_params=_params("arbitrary"),
        name="mlstm_core",
    )(per_batch(q), per_batch(k), per_batch(v), per_batch(og), per_batch(gcol), grow, norm_g,
      replicate)
    return out.reshape(TOKENS, D_MODEL)


def _top2_sum(v0, v1, v2, v3):
    hi1, lo1 = jnp.maximum(v0, v1), jnp.minimum(v0, v1)
    hi2, lo2 = jnp.maximum(v2, v3), jnp.minimum(v2, v3)
    return jnp.maximum(hi1, hi2) + jnp.maximum(jnp.minimum(hi1, hi2), jnp.maximum(lo1, lo2))


def _route_tail(x_new, m, g2_ref, wr_ref, br_ref, tri_ref, u2_ref, cls_ref, rank_ref, cnt_ref,
                run_ref):
    @pl.when(pl.program_id(0) == 0)
    def _():
        run_ref[...] = jnp.zeros_like(run_ref)

    u2 = _rms(x_new) * g2_ref[...] * (1.0 + m[4:5]) + m[3:4]
    _store_token_rows(u2_ref, u2, u2.shape[0])
    logits = _dot(u2.astype(BF16), wr_ref[...])
    lt = logits.T[0:N_EXPERTS, :]
    e = jnp.exp(lt - jnp.max(lt, axis=0, keepdims=True))
    probs = e / jnp.sum(e, axis=0, keepdims=True)
    sel = probs + br_ref[...]
    sel_rows = [sel[j:j + 1, :] for j in range(N_EXPERTS)]
    best = jnp.zeros_like(sel_rows[0], dtype=jnp.int32)
    best_score = _top2_sum(*sel_rows[0:EXPERTS_PER_GROUP])
    for g in range(1, N_GROUPS):
        score = _top2_sum(*sel_rows[g * EXPERTS_PER_GROUP:(g + 1) * EXPERTS_PER_GROUP])
        better = score > best_score
        best = jnp.where(better, g, best)
        best_score = jnp.where(better, score, best_score)
    s = []
    for j in range(EXPERTS_PER_GROUP):
        sj = sel_rows[j]
        for g in range(1, N_GROUPS):
            sj = jnp.where(best == g, sel_rows[g * EXPERTS_PER_GROUP + j], sj)
        s.append(sj)
    chosen = []
    for j in range(EXPERTS_PER_GROUP):
        beaten = jnp.zeros_like(best)
        for i in range(EXPERTS_PER_GROUP):
            if i == j:
                continue
            wins = (s[i] >= s[j]) if i < j else (s[i] > s[j])
            beaten = beaten + jnp.where(wins, 1, 0)
        chosen.append(beaten < 2)
    pair = jnp.full_like(best, len(PAIRS) - 1)
    for p in range(len(PAIRS) - 2, -1, -1):
        a, b = PAIRS[p]
        pair = jnp.where(jnp.logical_and(chosen[a], chosen[b]), p, pair)
    cls = best * len(PAIRS) + pair
    cls_ref[0] = cls

    class_id = lax.broadcasted_iota(jnp.int32, (CLASS_ROWS, cls.shape[1]), 0)
    onehot = class_id == cls
    before = _dot(jnp.where(onehot, 1.0, 0.0).astype(BF16), tri_ref[...])
    run = run_ref[...]
    rank = jnp.sum(jnp.where(onehot, before + run[:, 0:1], 0.0), axis=0, keepdims=True)
    rank_ref[0] = rank.astype(jnp.int32)
    run = run + jnp.sum(jnp.where(onehot, 1.0, 0.0), axis=1, keepdims=True)
    run_ref[...] = run
    cnt_ref[...] = run.astype(jnp.int32)


def _strict_upper(n):
    return (jnp.arange(n)[:, None] < jnp.arange(n)[None, :]).astype(BF16)


def _route_out_shapes(t, tm):
    return (
        jax.ShapeDtypeStruct((t, D_MODEL), F32),
        jax.ShapeDtypeStruct((t * ROW_TILE, LANES), F32),
        jax.ShapeDtypeStruct((t // tm, 1, tm), jnp.int32),
        jax.ShapeDtypeStruct((t // tm, 1, tm), jnp.int32),
        jax.ShapeDtypeStruct((CLASS_ROWS, LANES), jnp.int32),
    )


def _route_out_specs(tm):
    return (
        pl.BlockSpec((tm, D_MODEL), lambda i, *_: (i, 0)),
        pl.BlockSpec((tm * ROW_TILE, LANES), lambda i, *_: (i, 0)),
        pl.BlockSpec((1, 1, tm), lambda i, *_: (i, 0, 0)),
        pl.BlockSpec((1, 1, tm), lambda i, *_: (i, 0, 0)),
        pl.BlockSpec((CLASS_ROWS, LANES), lambda i, *_: (0, 0)),
    )


def _const_spec(shape):
    return pl.BlockSpec(shape, lambda i, *_: (0,) * len(shape))


def _mlstm_out_kernel(h_ref, x_ref, mod_ref, wo_ref, g2_ref, wr_ref, br_ref, tri_ref,
                      x1_ref, u2_ref, cls_ref, rank_ref, cnt_ref, run_ref):
    m = mod_ref[0]
    x1 = x_ref[...] + m[2:3] * _dot(h_ref[...], wo_ref[...])
    x1_ref[...] = x1
    _route_tail(x1, m, g2_ref, wr_ref, br_ref, tri_ref, u2_ref, cls_ref, rank_ref, cnt_ref, run_ref)


def _mlstm_out(hg, x, mod, w_out, g2, wr, br):
    t = x.shape[0]
    tm = TM_OUT
    tri = _strict_upper(tm)
    tiles_per_batch = SEQ // tm
    return pl.pallas_call(
        _mlstm_out_kernel,
        out_shape=_route_out_shapes(t, tm),
        grid=(t // tm,),
        in_specs=[
            pl.BlockSpec((tm, D_MODEL), lambda i, *_: (i, 0)),
            pl.BlockSpec((tm, D_MODEL), lambda i, *_: (i, 0)),
            pl.BlockSpec((1, 6, D_MODEL), lambda i, *_: (i // tiles_per_batch, 0, 0)),
            _const_spec((D_MODEL, D_MODEL)),
            _const_spec((1, D_MODEL)),
            _const_spec((D_MODEL, ROUTER_COLS)),
            _const_spec((N_EXPERTS, 1)),
            _const_spec((tm, tm)),
        ],
        out_specs=_route_out_specs(tm),
        scratch_shapes=[pltpu.VMEM((CLASS_ROWS, LANES), F32)],
        compiler_params=_params("arbitrary"),
        name="mlstm_out_route",
    )(hg, x, mod, w_out, g2, wr, br, tri)


def _conv_layer_kernel(dest_ref, x_ref, ys_ref, mod0_ref, mod_ref, g1_ref, wi_ref, cw_ref, cb_ref,
                       wo_ref, g2_ref, wr_ref, br_ref, tri_ref,
                       x3_ref, u2_ref, cls_ref, rank_ref, cnt_ref,
                       run_ref, carry_ref, ybuf_ref, ysem_ref):
    tm = x_ref.shape[0]
    tiles_per_batch = SEQ // tm

    @pl.when(pl.program_id(0) % tiles_per_batch == 0)
    def _():
        carry_ref[...] = jnp.zeros_like(carry_ref)

    m = mod_ref[0]
    y = _gathered_tile(dest_ref, ys_ref, ybuf_ref, ysem_ref, tm, issue_first=False)
    x2 = x_ref[...] + mod0_ref[0][5:6] * y
    u = (_rms(x2) * g1_ref[...] * (1.0 + m[1:2]) + m[0:1]).astype(BF16)
    bgate = _dot(u, wi_ref[:, 0:D_MODEL])
    z = _dot(u, wi_ref[:, D_MODEL:2 * D_MODEL]) * _dot(u, wi_ref[:, 2 * D_MODEL:])
    prev = carry_ref[...]
    row = lax.broadcasted_iota(jnp.int32, z.shape, 0)
    z1 = jnp.where(row == 0, prev[7:8], pltpu.roll(z, 1, axis=0))
    z2 = jnp.where(row == 0, prev[6:7], jnp.where(row == 1, prev[7:8], pltpu.roll(z, 2, axis=0)))
    carry_ref[...] = z[tm - SUBLANES:, :]
    cw = cw_ref[...]
    zc = cw[0:1] * z2 + cw[1:2] * z1 + cw[2:3] * z + cb_ref[...]
    x3 = x2 + m[2:3] * _dot((bgate * zc).astype(BF16), wo_ref[...])
    x3_ref[...] = x3
    _route_tail(x3, m, g2_ref, wr_ref, br_ref, tri_ref, u2_ref, cls_ref, rank_ref, cnt_ref, run_ref)
    _drain_row_gather(ys_ref, ybuf_ref, ysem_ref)


def _conv_layer(dest, x, ys, mod0, mod, g1, w_in, conv_w, conv_b, w_out, g2, wr, br):
    t = x.shape[0]
    tm = TM_CONV
    tri = _strict_upper(tm)
    tiles_per_batch = SEQ // tm
    mod_spec = pl.BlockSpec((1, 6, D_MODEL), lambda i, *_: (i // tiles_per_batch, 0, 0))
    return pl.pallas_call(
        _conv_layer_kernel,
        out_shape=_route_out_shapes(t, tm),
        grid_spec=pltpu.PrefetchScalarGridSpec(
            num_scalar_prefetch=1,
            grid=(t // tm,),
            in_specs=[
                pl.BlockSpec((tm, D_MODEL), lambda i, *_: (i, 0)),
                pl.BlockSpec(memory_space=pl.ANY),
                mod_spec,
                mod_spec,
                _const_spec((1, D_MODEL)),
                _const_spec((D_MODEL, 3 * D_MODEL)),
                _const_spec((3, D_MODEL)),
                _const_spec((1, D_MODEL)),
                _const_spec((D_MODEL, D_MODEL)),
                _const_spec((1, D_MODEL)),
                _const_spec((D_MODEL, ROUTER_COLS)),
                _const_spec((N_EXPERTS, 1)),
                _const_spec((tm, tm)),
            ],
            out_specs=_route_out_specs(tm),
            scratch_shapes=[
                pltpu.VMEM((CLASS_ROWS, LANES), F32),
                pltpu.VMEM((SUBLANES, D_MODEL), F32),
                pltpu.VMEM((2, tm * ROW_TILE, LANES), F32),
                pltpu.SemaphoreType.DMA((2,)),
            ],
        ),
        compiler_params=_params("arbitrary"),
        name="conv_layer_route",
    )(dest, x, ys, mod0, mod, g1, w_in, conv_w, conv_b, w_out, g2, wr, br, tri)


def _invert_kernel(dest_ref, fill_ref, src_ref, sem):
    fill = pltpu.make_async_copy(fill_ref, src_ref, sem)
    fill.start()
    fill.wait()
    unroll = 8

    def body(t8, carry):
        for k in range(unroll):
            t = t8 * unroll + k
            src_ref[dest_ref[t]] = t
        return carry

    lax.fori_loop(0, dest_ref.shape[0] // unroll, body, 0)


def _invert_permutation(dest, n_out):
    smem = pl.BlockSpec(memory_space=pltpu.SMEM)
    return pl.pallas_call(
        _invert_kernel,
        out_shape=jax.ShapeDtypeStruct((n_out,), jnp.int32),
        in_specs=[smem, pl.BlockSpec(memory_space=pl.ANY)],
        out_specs=smem,
        scratch_shapes=[pltpu.SemaphoreType.DMA(())],
        name="invert_permutation",
    )(dest, jnp.arange(n_out, dtype=jnp.int32) % dest.shape[0])


def _expert_kernel(ea_ref, eb_ref, chg_ref, nused_ref, src_ref, u2_ref, wga_ref, wua_ref, wda_ref,
                   wgb_ref, wub_ref, wdb_ref, wra_ref, wrb_ref, y_ref, xbuf_ref, xsem_ref, *wbf_refs):
    del ea_ref, eb_ref
    j = pl.program_id(0)
    tm = TM_EXPERT

    @pl.when(chg_ref[j] == 1)
    def _():
        for src, dst in zip((wga_ref, wua_ref, wda_ref, wgb_ref, wub_ref, wdb_ref), wbf_refs):
            dst[...] = src[0, 0].astype(BF16)

    rows, start_next = _gathered_tile(src_ref, u2_ref, xbuf_ref, xsem_ref, tm, issue_first=None)

    @pl.when(j < nused_ref[0])
    def _():
        start_next()
        xb = rows.astype(BF16)
        dl = jnp.sum(xb.astype(F32) * (wra_ref[0] - wrb_ref[0]), axis=-1, keepdims=True)
        w_a = _sigmoid(dl)
        w_b = _sigmoid(-dl)

        def ffn(wg_ref, wu_ref, wd_ref):
            gate = _dot(xb, wg_ref[...])
            hidden = gate * _sigmoid(gate) * _dot(xb, wu_ref[...])
            return _dot(hidden.astype(BF16), wd_ref[...])

        y = w_a * ffn(*wbf_refs[0:3]) + w_b * ffn(*wbf_refs[3:6])
        _store_token_rows(y_ref, y, tm)

    @pl.when(j >= nused_ref[0])
    def _():
        start_next()
        y_ref[...] = jnp.zeros_like(y_ref)

    _drain_row_gather(u2_ref, xbuf_ref, xsem_ref)


def _experts(layer, ea, eb, chg, n_used, src, u2, w_gate, w_up, w_down, wr_rows):
    tm = TM_EXPERT
    sel_a = lambda j, ea, eb, *_: (layer, ea[j], 0, 0)
    sel_b = lambda j, ea, eb, *_: (layer, eb[j], 0, 0)
    up_spec = lambda sel: pl.BlockSpec((1, 1, D_MODEL, D_EXPERT), sel)
    down_spec = lambda sel: pl.BlockSpec((1, 1, D_EXPERT, D_MODEL), sel)
    wr_spec = lambda sel: pl.BlockSpec((1, 1, D_MODEL), lambda *a: sel(*a)[1:])
    up_scratch = pltpu.VMEM((D_MODEL, D_EXPERT), BF16)
    down_scratch = pltpu.VMEM((D_EXPERT, D_MODEL), BF16)
    return pl.pallas_call(
        _expert_kernel,
        out_shape=jax.ShapeDtypeStruct((PADDED_ROWS * ROW_TILE, LANES), F32),
        grid_spec=pltpu.PrefetchScalarGridSpec(
            num_scalar_prefetch=4,
            grid=(N_EXPERT_TILES,),
            in_specs=[
                pl.BlockSpec(memory_space=pltpu.SMEM),
                pl.BlockSpec(memory_space=pl.ANY),
                up_spec(sel_a), up_spec(sel_a), down_spec(sel_a),
                up_spec(sel_b), up_spec(sel_b), down_spec(sel_b),
                wr_spec(sel_a), wr_spec(sel_b),
            ],
            out_specs=pl.BlockSpec((tm * ROW_TILE, LANES), lambda j, *_: (j, 0)),
            scratch_shapes=[pltpu.VMEM((EXPERT_GATHER_BUFFERS, tm * ROW_TILE, LANES), F32),
                            pltpu.SemaphoreType.DMA((EXPERT_GATHER_BUFFERS,)),
                            up_scratch, up_scratch, down_scratch, up_scratch, up_scratch, down_scratch],
        ),
        compiler_params=_params("arbitrary"),
        name="grouped_experts",
    )(ea, eb, chg, n_used, src, u2, w_gate, w_up, w_down, w_gate, w_up, w_down, wr_rows, wr_rows)


def _final_kernel(dest_ref, x_ref, ys_ref, mod_ref, g_ref, o_ref, ybuf_ref, ysem_ref):
    y = _gathered_tile(dest_ref, ys_ref, ybuf_ref, ysem_ref, x_ref.shape[0], issue_first=True)
    x = x_ref[...] + mod_ref[0][5:6] * y
    o_ref[...] = _rms(x) * g_ref[...]
    _drain_row_gather(ys_ref, ybuf_ref, ysem_ref)


def _final(dest, x, ys, mod, g):
    t = x.shape[0]
    tm = TM_FINAL
    tiles_per_batch = SEQ // tm
    return pl.pallas_call(
        _final_kernel,
        out_shape=jax.ShapeDtypeStruct((t, D_MODEL), F32),
        grid_spec=pltpu.PrefetchScalarGridSpec(
            num_scalar_prefetch=1,
            grid=(t // tm,),
            in_specs=[
                pl.BlockSpec((tm, D_MODEL), lambda i, *_: (i, 0)),
                pl.BlockSpec(memory_space=pl.ANY),
                pl.BlockSpec((1, 6, D_MODEL), lambda i, *_: (i // tiles_per_batch, 0, 0)),
                _const_spec((1, D_MODEL)),
            ],
            out_specs=pl.BlockSpec((tm, D_MODEL), lambda i, *_: (i, 0)),
            scratch_shapes=[pltpu.VMEM((2, tm * ROW_TILE, LANES), F32), pltpu.SemaphoreType.DMA((2,))],
        ),
        compiler_params=_params("arbitrary"),
        name="final_norm",
    )(dest, x, ys, mod, g)


_PAIR_A = np.array([EXPERTS_PER_GROUP * (c // len(PAIRS)) + PAIRS[c % len(PAIRS)][0]
                    for c in range(N_CLASSES)], np.int32)
_PAIR_B = np.array([EXPERTS_PER_GROUP * (c // len(PAIRS)) + PAIRS[c % len(PAIRS)][1]
                    for c in range(N_CLASSES)], np.int32)


def _moe(layer, u2, cls, rank, cnt, w_gate, w_up, w_down, wr_rows):
    counts = cnt[:N_CLASSES, 0]
    tiles = (counts + TM_EXPERT - 1) // TM_EXPERT
    tile_end = jnp.cumsum(tiles)
    row_start = (tile_end - tiles) * TM_EXPERT
    n_used = tile_end[-1:]
    cls = cls.reshape(-1)
    dest = row_start[cls] + rank.reshape(-1)
    tile_cls = jnp.sum(jnp.arange(N_EXPERT_TILES)[:, None] >= tile_end[None, :], axis=1)
    tile_cls = jnp.minimum(tile_cls, tile_cls[jnp.maximum(n_used[0] - 1, 0)])
    ea = jnp.asarray(_PAIR_A)[tile_cls]
    eb = jnp.asarray(_PAIR_B)[tile_cls]
    chg = jnp.concatenate([jnp.ones((1,), jnp.int32),
                           (tile_cls[1:] != tile_cls[:-1]).astype(jnp.int32)])
    src = _invert_permutation(dest, PADDED_ROWS)
    ys = _experts(layer, ea, eb, chg, n_used.astype(jnp.int32), src, u2, w_gate, w_up, w_down, wr_rows)
    return dest, ys


def kernel(x, c, norm1_g, norm2_g, w_ada, b_ada, m_w_in, m_b_gates, m_norm_g, m_w_out,
           c_w_in, c_conv_w, c_conv_b, c_w_out, w_router, b_router,
           e_w_gate, e_w_up, e_w_down, final_g):
    xf = x.reshape(TOKENS, D_MODEL)
    mod = _ada(c, w_ada, b_ada)

    w_in = jnp.concatenate(
        [m_w_in[0].astype(BF16),
         jnp.zeros((D_MODEL, GATE_COLS - 2 * N_HEADS), BF16)], axis=1)
    bg = jnp.zeros((1, GATE_COLS), F32).at[0, :2 * N_HEADS].set(m_b_gates[0])
    wr = jnp.zeros((D_MODEL, ROUTER_COLS), BF16).at[:, :N_EXPERTS].set(w_router.astype(BF16))
    wr_rows = w_router.astype(BF16).astype(F32).T.reshape(N_EXPERTS, 1, D_MODEL)
    br = b_router.reshape(N_EXPERTS, 1)
    row = lambda v: v.reshape(1, -1)

    q, k, v, og, gcol, grow = _mlstm_proj(xf, mod[0], row(norm1_g[0]), w_in, bg)
    hg = _mlstm_core(q, k, v, og, gcol, grow, row(m_norm_g[0]))
    x1, u2, cls, rank, cnt = _mlstm_out(hg, xf, mod[0], m_w_out[0].astype(BF16), row(norm2_g[0]),
                                        wr, br)
    dest, ys = _moe(0, u2, cls, rank, cnt, e_w_gate, e_w_up, e_w_down, wr_rows)

    x3, u2, cls, rank, cnt = _conv_layer(
        dest, x1, ys, mod[0], mod[1], row(norm1_g[1]), c_w_in[0].astype(BF16), c_conv_w[0],
        row(c_conv_b[0]), c_w_out[0].astype(BF16), row(norm2_g[1]), wr, br)
    dest, ys = _moe(1, u2, cls, rank, cnt, e_w_gate, e_w_up, e_w_down, wr_rows)

    out = _final(dest, x3, ys, mod[1], row(final_g))
    return out.reshape(BATCH, SEQ, D_MODEL)
```

```python
import functools

import jax
import jax.numpy as jnp
import numpy as np
from jax import lax
from jax.experimental import pallas as pl
from jax.experimental.pallas import tpu as pltpu

F32 = jnp.float32
BF16 = jnp.bfloat16

D_MODEL = 1024
BATCH = 4
SEQ = 8192
TOKENS = BATCH * SEQ
N_HEADS = 4
DH_V = 256
DH_QK = 128
QK = N_HEADS * DH_QK
N_EXPERTS = 16
N_GROUPS = 4
EXPERTS_PER_GROUP = 4
D_EXPERT = 512
EPS = 1e-6

LANES = 128
SUBLANES = 8
VMEM_LIMIT_BYTES = 56 * 1024 * 1024

CHUNK = 128
TM_PROJ = 1024
TM_OUT = 1024
TM_CONV = 512
TM_EXPERT = 512
GATE_COLS = LANES
GATE_WIDTH = 16
GATE_KINDS = 3
ROUTER_COLS = LANES
PAIRS = ((0, 1), (0, 2), (0, 3), (1, 2), (1, 3), (2, 3))
N_CLASSES = N_GROUPS * len(PAIRS)
CLASS_ROWS = 32
N_EXPERT_TILES = TOKENS // TM_EXPERT + N_CLASSES
PADDED_ROWS = N_EXPERT_TILES * TM_EXPERT
TM_FINAL = 512
EXPERT_GATHER_BUFFERS = 3
STREAM_GROUP = 4


def _params(*semantics):
    return pltpu.CompilerParams(dimension_semantics=semantics, vmem_limit_bytes=VMEM_LIMIT_BYTES)


def _dot(a, b):
    return jnp.dot(a, b, preferred_element_type=F32)


def _rms(x):
    return x * lax.rsqrt(jnp.mean(x * x, axis=-1, keepdims=True) + EPS)


def _sigmoid(x):
    return 1.0 / (1.0 + jnp.exp(-x))


ROW_TILE = D_MODEL // LANES


def _load_token_rows(ref, n):
    return jnp.concatenate([ref[pl.ds(c, n, stride=ROW_TILE), :] for c in range(ROW_TILE)], axis=1)


def _store_token_rows(ref, val, n):
    for c in range(ROW_TILE):
        ref[pl.ds(c, n, stride=ROW_TILE), :] = val[:, c * LANES:(c + 1) * LANES]


def _token_tile(ref, t):
    return ref.at[pl.ds(pl.multiple_of(t * ROW_TILE, ROW_TILE), ROW_TILE)]


def _start_row_gather(idx_ref, base, n, src_hbm, buf, sem):
    for r in range(n):
        pltpu.make_async_copy(_token_tile(src_hbm, idx_ref[base + r]), _token_tile(buf, r),
                              sem).start(priority=r % 2)


def _wait_row_gather(src_hbm, buf, sem):
    pltpu.make_async_copy(src_hbm.at[pl.ds(0, buf.shape[0])], buf, sem).wait()


def _gathered_tile(idx_ref, src_hbm, buf_ref, sem_ref, n, issue_first, n_tiles=None):
    i = pl.program_id(0)
    last = (pl.num_programs(0) if n_tiles is None else n_tiles) - 1
    nb = buf_ref.shape[0]
    ahead = nb - 1

    @pl.when(i == 0)
    def _():
        for k in range(ahead):
            _start_row_gather(idx_ref, jnp.minimum(k, last) * n, n, src_hbm, buf_ref.at[k], sem_ref.at[k])

    nxt = (i + ahead) % nb
    start_next = functools.partial(
        _start_row_gather, idx_ref, jnp.minimum(i + ahead, last) * n, n, src_hbm, buf_ref.at[nxt],
        sem_ref.at[nxt])
    if issue_first:
        start_next()
    slot = i % nb
    _wait_row_gather(src_hbm, buf_ref.at[slot], sem_ref.at[slot])
    rows = _load_token_rows(buf_ref.at[slot], n)
    if issue_first is None:
        return rows, start_next
    if not issue_first:
        start_next()
    return rows


def _drain_row_gather(src_hbm, buf_ref, sem_ref):
    i = pl.program_id(0)
    nb = buf_ref.shape[0]

    @pl.when(i == pl.num_programs(0) - 1)
    def _():
        for k in range(1, nb):
            _wait_row_gather(src_hbm, buf_ref.at[(i + k) % nb], sem_ref.at[(i + k) % nb])


def _ada_kernel(c_ref, w_ref, b_ref, o_ref):
    c = c_ref[...]
    cond = c * _sigmoid(c)
    o_ref[0] = jnp.dot(cond, w_ref[0], preferred_element_type=F32,
                       precision=lax.Precision.HIGHEST) + b_ref[0]


def _ada(c, w_ada, b_ada):
    depth, d, n = w_ada.shape
    tn = 1536
    c8 = jnp.zeros((SUBLANES, d), F32).at[:BATCH].set(c)
    out = pl.pallas_call(
        _ada_kernel,
        out_shape=jax.ShapeDtypeStruct((depth, SUBLANES, n), F32),
        grid=(depth, n // tn),
        in_specs=[
            pl.BlockSpec((SUBLANES, d), lambda l, j: (0, 0)),
            pl.BlockSpec((1, d, tn), lambda l, j: (l, 0, j)),
            pl.BlockSpec((1, 1, tn), lambda l, j: (l, 0, j)),
        ],
        out_specs=pl.BlockSpec((1, SUBLANES, tn), lambda l, j: (l, 0, j)),
        compiler_params=_params("arbitrary", "arbitrary"),
        name="ada_mod",
    )(c8, w_ada, b_ada.reshape(depth, 1, n))
    return out[:, :BATCH].reshape(depth, BATCH, 6, d)


def _mlstm_proj_kernel(x_ref, mod_ref, g_ref, w_ref, bg_ref, q_ref, k_ref, v_ref, og_ref,
                       gcol_ref, grow_ref):
    x = x_ref[...]
    m = mod_ref[0]
    u = (_rms(x) * g_ref[...] * (1.0 + m[1:2]) + m[0:1]).astype(BF16)
    q_ref[...] = (_dot(u, w_ref[:, 0:QK]) * (DH_QK ** -0.5)).astype(BF16)
    k_ref[...] = _dot(u, w_ref[:, QK:2 * QK]).astype(BF16)
    v_ref[...] = _dot(u, w_ref[:, 2 * QK:2 * QK + D_MODEL]).astype(BF16)
    og_ref[...] = _dot(u, w_ref[:, 2 * QK + D_MODEL:2 * QK + 2 * D_MODEL]).astype(BF16)
    gt = _dot(u, w_ref[:, 2 * QK + 2 * D_MODEL:]) + bg_ref[...]
    lane = lax.broadcasted_iota(jnp.int32, gt.shape, 1)
    pos = lax.broadcasted_iota(jnp.int32, gt.shape, 0) % CHUNK
    log_f = jnp.minimum(gt, 0.0) - jnp.log(1.0 + jnp.exp(-jnp.abs(gt)))
    b = log_f
    shift = 1
    while shift < CHUNK:
        b = b + jnp.where(pos >= shift, pltpu.roll(b, shift, axis=0), 0.0)
        shift *= 2
    imb = gt - pltpu.roll(b, LANES - N_HEADS, axis=1)
    cm = imb
    shift = 1
    while shift < CHUNK:
        cm = jnp.maximum(cm, jnp.where(pos >= shift, pltpu.roll(cm, shift, axis=0), -jnp.inf))
        shift *= 2
    cols = jnp.where(lane < N_HEADS, cm,
                     jnp.where(lane < 2 * N_HEADS, b,
                               jnp.where(lane < 3 * N_HEADS, pltpu.roll(imb, 2 * N_HEADS, axis=1), 0.0)))
    gcol_ref[...] = cols[:, 0:GATE_WIDTH]
    grow_ref[0] = imb.T[0:SUBLANES, :]


def _mlstm_proj(x, mod, g, w, bg):
    t, d = x.shape
    n = w.shape[1]
    tm = TM_PROJ
    tiles_per_batch = SEQ // tm
    return pl.pallas_call(
        _mlstm_proj_kernel,
        out_shape=(
            jax.ShapeDtypeStruct((t, QK), BF16),
            jax.ShapeDtypeStruct((t, QK), BF16),
            jax.ShapeDtypeStruct((t, D_MODEL), BF16),
            jax.ShapeDtypeStruct((t, D_MODEL), BF16),
            jax.ShapeDtypeStruct((t, GATE_WIDTH), F32),
            jax.ShapeDtypeStruct((BATCH, SUBLANES, SEQ), F32),
        ),
        grid=(t // tm,),
        in_specs=[
            pl.BlockSpec((tm, d), lambda i, *_: (i, 0)),
            pl.BlockSpec((1, 6, d), lambda i, *_: (i // tiles_per_batch, 0, 0)),
            pl.BlockSpec((1, d), lambda i, *_: (0, 0)),
            pl.BlockSpec((d, n), lambda i, *_: (0, 0)),
            pl.BlockSpec((1, GATE_COLS), lambda i, *_: (0, 0)),
        ],
        out_specs=(
            pl.BlockSpec((tm, QK), lambda i, *_: (i, 0)),
            pl.BlockSpec((tm, QK), lambda i, *_: (i, 0)),
            pl.BlockSpec((tm, D_MODEL), lambda i, *_: (i, 0)),
            pl.BlockSpec((tm, D_MODEL), lambda i, *_: (i, 0)),
            pl.BlockSpec((tm, GATE_WIDTH), lambda i, *_: (i, 0)),
            pl.BlockSpec((1, SUBLANES, tm),
                         lambda i, *_: (i // tiles_per_batch, 0, i % tiles_per_batch)),
        ),
        compiler_params=_params("arbitrary"),
        name="mlstm_proj",
    )(x, mod, g, w, bg)


def _mlstm_core_kernel(q_ref, k_ref, v_ref, og_ref, gcol_ref, grow_ref, ng_ref, rep_ref, o_ref,
                       m_ref, *c_refs):
    @pl.when(pl.program_id(0) == 0)
    def _():
        m_ref[...] = jnp.zeros_like(m_ref)
        for c_ref in c_refs:
            c_ref[...] = jnp.zeros_like(c_ref)

    ln = CHUNK
    row = lax.broadcasted_iota(jnp.int32, (ln, ln), 0)
    col = lax.broadcasted_iota(jnp.int32, (ln, ln), 1)
    causal = col <= row
    ones = jnp.ones((ln, LANES), BF16)
    n_streams = BATCH * N_HEADS
    m_all = [m_ref[st] for st in range(n_streams)]
    m_out = [None] * n_streams
    for bi in range(BATCH):
        g = gcol_ref[bi]
        g_hi = g.astype(BF16)
        g_r1 = g - g_hi.astype(F32)
        g_mid = g_r1.astype(BF16)
        g_lo = (g_r1 - g_mid.astype(F32)).astype(BF16)
        rep = _dot(g_hi, rep_ref[...]) + _dot(g_mid, rep_ref[...]) + _dot(g_lo, rep_ref[...])
        for h in range(N_HEADS):
            st = bi * N_HEADS + h
            lanes = lambda kind: slice((kind * N_HEADS + h) * LANES, (kind * N_HEADS + h + 1) * LANES)
            cm, b, imb_col = rep[:, lanes(0)], rep[:, lanes(1)], rep[:, lanes(2)]
            imb_row = grow_ref[bi, h:h + 1, :]
            qh = q_ref[bi, :, h * DH_QK:(h + 1) * DH_QK]
            kh = k_ref[bi, :, h * DH_QK:(h + 1) * DH_QK]
            v_ext = jnp.concatenate([v_ref[bi, :, h * DH_V:(h + 1) * DH_V], ones], axis=1)
            c_ref = c_refs[st]
            c_prev = c_ref[...]
            m_prev = m_all[st]

            big_m = jnp.maximum(m_prev, cm)
            d_mat = jnp.exp(jnp.where(causal, imb_row - big_m, -jnp.inf))
            s = lax.dot_general(qh, kh, (((1,), (1,)), ((), ())), preferred_element_type=F32) * d_mat
            q_inter = (qh.astype(F32) * jnp.exp(m_prev - big_m)).astype(BF16)
            lhs = jnp.concatenate([q_inter, s.astype(BF16)], axis=1)
            rhs = jnp.concatenate([c_prev.astype(BF16), v_ext], axis=0)
            nd = _dot(lhs, rhs)
            den = nd[:, DH_V:]
            inv = 1.0 / jnp.maximum(jnp.abs(den), jnp.exp(-(b + big_m)))
            hh = nd[:, :DH_V] * jnp.concatenate([inv, inv], axis=1)

            m_last = big_m[ln - 1:ln, :]
            kw = (kh.astype(F32) * jnp.exp(imb_col - m_last)).astype(BF16)
            decay = jnp.exp(m_prev - m_last)
            c_ref[...] = jnp.concatenate([decay] * 3, axis=1) * c_prev + lax.dot_general(
                kw, v_ext, (((0,), (0,)), ((), ())), preferred_element_type=F32)
            m_out[st] = b[ln - 1:ln, :] + m_last

            sl = slice(h * DH_V, (h + 1) * DH_V)
            gate = _sigmoid(og_ref[bi, :, sl].astype(F32))
            o_ref[bi, :, sl] = (_rms(hh) * ng_ref[:, sl] * gate).astype(BF16)
    for st in range(n_streams):
        m_ref[st] = m_out[st]


def _mlstm_core(q, k, v, og, gcol, grow, norm_g):
    nc = SEQ // CHUNK
    n_streams = BATCH * N_HEADS
    per_batch = lambda a: a.reshape(BATCH, SEQ, a.shape[-1])
    chunk_rows = lambda width: pl.BlockSpec((BATCH, CHUNK, width), lambda c: (0, c, 0))
    n_rep = GATE_KINDS * N_HEADS
    replicate = (jnp.arange(GATE_WIDTH)[:, None] == jnp.arange(n_rep * LANES)[None, :] // LANES
                 ).astype(BF16)
    out = pl.pallas_call(
        _mlstm_core_kernel,
        out_shape=jax.ShapeDtypeStruct((BATCH, SEQ, D_MODEL), BF16),
        grid=(nc,),
        in_specs=[
            chunk_rows(QK),
            chunk_rows(QK),
            chunk_rows(D_MODEL),
            chunk_rows(D_MODEL),
            chunk_rows(GATE_WIDTH),
            pl.BlockSpec((BATCH, SUBLANES, CHUNK), lambda c: (0, 0, c)),
            pl.BlockSpec((1, D_MODEL), lambda c: (0, 0)),
            pl.BlockSpec((GATE_WIDTH, n_rep * LANES), lambda c: (0, 0)),
        ],
        out_specs=chunk_rows(D_MODEL),
        scratch_shapes=[pltpu.VMEM((n_streams, 1, LANES), F32)]
        + [pltpu.VMEM((DH_QK, DH_V + LANES), F32) for _ in range(n_streams)],
        compiler_params=_params("arbitrary"),
        name="mlstm_core",
    )(per_batch(q), per_batch(k), per_batch(v), per_batch(og), per_batch(gcol), grow, norm_g,
      replicate)
    return out.reshape(TOKENS, D_MODEL)


def _top2_sum(v0, v1, v2, v3):
    hi1, lo1 = jnp.maximum(v0, v1), jnp.minimum(v0, v1)
    hi2, lo2 = jnp.maximum(v2, v3), jnp.minimum(v2, v3)
    return jnp.maximum(hi1, hi2) + jnp.maximum(jnp.minimum(hi1, hi2), jnp.maximum(lo1, lo2))


def _route_tail(x_new, m, valid, g2_ref, wr_ref, br_ref, tri_ref, u2_ref, cls_ref, rank_ref, cnt_ref,
                run_ref, after_router=None):
    u2 = _rms(x_new) * g2_ref[...] * (1.0 + m[4:5]) + m[3:4]
    _store_token_rows(u2_ref, u2, u2.shape[0])
    logits = _dot(u2.astype(BF16), wr_ref[...])
    if after_router is not None:
        after_router()
    lt = logits.T[0:N_EXPERTS, :]
    e = jnp.exp(lt - jnp.max(lt, axis=0, keepdims=True))
    probs = e / jnp.sum(e, axis=0, keepdims=True)
    sel = probs + br_ref[...]
    sel_rows = [sel[j:j + 1, :] for j in range(N_EXPERTS)]
    best = jnp.zeros_like(sel_rows[0], dtype=jnp.int32)
    best_score = _top2_sum(*sel_rows[0:EXPERTS_PER_GROUP])
    for g in range(1, N_GROUPS):
        score = _top2_sum(*sel_rows[g * EXPERTS_PER_GROUP:(g + 1) * EXPERTS_PER_GROUP])
        better = score > best_score
        best = jnp.where(better, g, best)
        best_score = jnp.where(better, score, best_score)
    s = []
    for j in range(EXPERTS_PER_GROUP):
        sj = sel_rows[j]
        for g in range(1, N_GROUPS):
            sj = jnp.where(best == g, sel_rows[g * EXPERTS_PER_GROUP + j], sj)
        s.append(sj)
    chosen = []
    for j in range(EXPERTS_PER_GROUP):
        beaten = jnp.zeros_like(best)
        for i in range(EXPERTS_PER_GROUP):
            if i == j:
                continue
            wins = (s[i] >= s[j]) if i < j else (s[i] > s[j])
            beaten = beaten + jnp.where(wins, 1, 0)
        chosen.append(beaten < 2)
    pair = jnp.full_like(best, len(PAIRS) - 1)
    for p in range(len(PAIRS) - 2, -1, -1):
        a, b = PAIRS[p]
        pair = jnp.where(jnp.logical_and(chosen[a], chosen[b]), p, pair)
    cls = best * len(PAIRS) + pair
    cls_ref[0] = cls

    class_id = lax.broadcasted_iota(jnp.int32, (CLASS_ROWS, cls.shape[1]), 0)
    onehot = class_id == cls
    before = _dot(jnp.where(onehot, 1.0, 0.0).astype(BF16), tri_ref[...])
    run = run_ref[...]
    rank = jnp.sum(jnp.where(onehot, before + run[:, 0:1], 0.0), axis=0, keepdims=True)
    rank_ref[0] = rank.astype(jnp.int32)
    run = run + valid * jnp.sum(jnp.where(onehot, 1.0, 0.0), axis=1, keepdims=True)
    run_ref[...] = run
    cnt_ref[...] = run.astype(jnp.int32)


def _strict_upper(n):
    return (jnp.arange(n)[:, None] < jnp.arange(n)[None, :]).astype(BF16)


def _route_out_shapes(t, tm):
    return (
        jax.ShapeDtypeStruct((t, D_MODEL), F32),
        jax.ShapeDtypeStruct((t * ROW_TILE, LANES), F32),
        jax.ShapeDtypeStruct((t // tm, 1, tm), jnp.int32),
        jax.ShapeDtypeStruct((t // tm, 1, tm), jnp.int32),
        jax.ShapeDtypeStruct((CLASS_ROWS, LANES), jnp.int32),
    )


def _cur_tile(n_tiles):
    return lambda i: jnp.minimum(i, n_tiles - 1)


def _prev_tile(i):
    return jnp.maximum(i - 1, 0)


def _route_out_specs(tm, n_tiles):
    cur = _cur_tile(n_tiles)
    return (
        pl.BlockSpec((tm, D_MODEL), lambda i, *_: (cur(i), 0)),
        pl.BlockSpec((tm * ROW_TILE, LANES), lambda i, *_: (_prev_tile(i), 0)),
        pl.BlockSpec((1, 1, tm), lambda i, *_: (_prev_tile(i), 0, 0)),
        pl.BlockSpec((1, 1, tm), lambda i, *_: (_prev_tile(i), 0, 0)),
        pl.BlockSpec((CLASS_ROWS, LANES), lambda i, *_: (0, 0)),
    )


def _init_deferred_tail(xprev_ref, run_ref):
    @pl.when(pl.program_id(0) == 0)
    def _():
        xprev_ref[...] = jnp.zeros_like(xprev_ref)
        run_ref[...] = jnp.zeros_like(run_ref)


def _deferred_tail(xprev_ref, modp_ref, *tail_refs, after_router=None):
    valid = jnp.where(pl.program_id(0) > 0, 1.0, 0.0)
    _route_tail(xprev_ref[...], modp_ref[0], valid, *tail_refs, after_router=after_router)


def _const_spec(shape):
    return pl.BlockSpec(shape, lambda i, *_: (0,) * len(shape))


def _mlstm_out_kernel(h_ref, x_ref, mod_ref, modp_ref, wo_ref, g2_ref, wr_ref, br_ref, tri_ref,
                      x1_ref, u2_ref, cls_ref, rank_ref, cnt_ref, run_ref, xprev_ref):
    _init_deferred_tail(xprev_ref, run_ref)
    quarter = D_MODEL // 4
    parts = []

    def project(*quarters):
        for j in quarters:
            parts.append(_dot(h_ref[...], wo_ref[:, j * quarter:(j + 1) * quarter]))

    project(0)
    _deferred_tail(xprev_ref, modp_ref, g2_ref, wr_ref, br_ref, tri_ref, u2_ref, cls_ref, rank_ref,
                   cnt_ref, run_ref, after_router=functools.partial(project, 1, 2))
    project(3)
    x1 = x_ref[...] + mod_ref[0][2:3] * jnp.concatenate(parts, axis=1)
    x1_ref[...] = x1
    xprev_ref[...] = x1


def _mlstm_out(hg, x, mod, w_out, g2, wr, br):
    t = x.shape[0]
    tm = TM_OUT
    tri = _strict_upper(tm)
    tiles_per_batch = SEQ // tm
    n_tiles = t // tm
    cur = _cur_tile(n_tiles)
    return pl.pallas_call(
        _mlstm_out_kernel,
        out_shape=_route_out_shapes(t, tm),
        grid=(n_tiles + 1,),
        in_specs=[
            pl.BlockSpec((tm, D_MODEL), lambda i, *_: (cur(i), 0)),
            pl.BlockSpec((tm, D_MODEL), lambda i, *_: (cur(i), 0)),
            pl.BlockSpec((1, 6, D_MODEL), lambda i, *_: (cur(i) // tiles_per_batch, 0, 0)),
            pl.BlockSpec((1, 6, D_MODEL), lambda i, *_: (_prev_tile(i) // tiles_per_batch, 0, 0)),
            _const_spec((D_MODEL, D_MODEL)),
            _const_spec((1, D_MODEL)),
            _const_spec((D_MODEL, ROUTER_COLS)),
            _const_spec((N_EXPERTS, 1)),
            _const_spec((tm, tm)),
        ],
        out_specs=_route_out_specs(tm, n_tiles),
        scratch_shapes=[pltpu.VMEM((CLASS_ROWS, LANES), F32), pltpu.VMEM((tm, D_MODEL), F32)],
        compiler_params=_params("arbitrary"),
        name="mlstm_out_route",
    )(hg, x, mod, mod, w_out, g2, wr, br, tri)


def _conv_layer_kernel(dest_ref, x_ref, ys_ref, mod0_ref, mod_ref, modp_ref, g1_ref, wi_ref, cw_ref,
                       cb_ref, wo_ref, g2_ref, wr_ref, br_ref, tri_ref,
                       x3_ref, u2_ref, cls_ref, rank_ref, cnt_ref,
                       run_ref, carry_ref, ybuf_ref, ysem_ref, xprev_ref):
    tm = x_ref.shape[0]
    tiles_per_batch = SEQ // tm
    n_tiles = pl.num_programs(0) - 1
    i = pl.program_id(0)

    @pl.when(i % tiles_per_batch == 0)
    def _():
        carry_ref[...] = jnp.zeros_like(carry_ref)

    _init_deferred_tail(xprev_ref, run_ref)
    m = mod_ref[0]
    y = _gathered_tile(dest_ref, ys_ref, ybuf_ref, ysem_ref, tm, issue_first=False, n_tiles=n_tiles)
    x2 = x_ref[...] + mod0_ref[0][5:6] * y
    u = (_rms(x2) * g1_ref[...] * (1.0 + m[1:2]) + m[0:1]).astype(BF16)
    bgate = _dot(u, wi_ref[:, 0:D_MODEL])
    gated = []

    def project_gated():
        gated.append(_dot(u, wi_ref[:, D_MODEL:2 * D_MODEL]) * _dot(u, wi_ref[:, 2 * D_MODEL:]))

    _deferred_tail(xprev_ref, modp_ref, g2_ref, wr_ref, br_ref, tri_ref, u2_ref, cls_ref, rank_ref,
                   cnt_ref, run_ref, after_router=project_gated)
    z = gated[0]
    prev = carry_ref[...]
    row = lax.broadcasted_iota(jnp.int32, z.shape, 0)
    z1 = jnp.where(row == 0, prev[7:8], pltpu.roll(z, 1, axis=0))
    z2 = jnp.where(row == 0, prev[6:7], jnp.where(row == 1, prev[7:8], pltpu.roll(z, 2, axis=0)))
    carry_ref[...] = z[tm - SUBLANES:, :]
    cw = cw_ref[...]
    zc = cw[0:1] * z2 + cw[1:2] * z1 + cw[2:3] * z + cb_ref[...]
    x3 = x2 + m[2:3] * _dot((bgate * zc).astype(BF16), wo_ref[...])
    xprev_ref[...] = x3

    @pl.when(i < n_tiles)
    def _():
        x3_ref[...] = x3

    _drain_row_gather(ys_ref, ybuf_ref, ysem_ref)


def _conv_layer(dest, x, ys, mod0, mod, g1, w_in, conv_w, conv_b, w_out, g2, wr, br):
    t = x.shape[0]
    tm = TM_CONV
    tri = _strict_upper(tm)
    tiles_per_batch = SEQ // tm
    n_tiles = t // tm
    cur = _cur_tile(n_tiles)
    mod_spec = pl.BlockSpec((1, 6, D_MODEL), lambda i, *_: (cur(i) // tiles_per_batch, 0, 0))
    modp_spec = pl.BlockSpec((1, 6, D_MODEL), lambda i, *_: (_prev_tile(i) // tiles_per_batch, 0, 0))
    return pl.pallas_call(
        _conv_layer_kernel,
        out_shape=_route_out_shapes(t, tm),
        grid_spec=pltpu.PrefetchScalarGridSpec(
            num_scalar_prefetch=1,
            grid=(n_tiles + 1,),
            in_specs=[
                pl.BlockSpec((tm, D_MODEL), lambda i, *_: (cur(i), 0)),
                pl.BlockSpec(memory_space=pl.ANY),
                mod_spec,
                mod_spec,
                modp_spec,
                _const_spec((1, D_MODEL)),
                _const_spec((D_MODEL, 3 * D_MODEL)),
                _const_spec((3, D_MODEL)),
                _const_spec((1, D_MODEL)),
                _const_spec((D_MODEL, D_MODEL)),
                _const_spec((1, D_MODEL)),
                _const_spec((D_MODEL, ROUTER_COLS)),
                _const_spec((N_EXPERTS, 1)),
                _const_spec((tm, tm)),
            ],
            out_specs=_route_out_specs(tm, n_tiles),
            scratch_shapes=[
                pltpu.VMEM((CLASS_ROWS, LANES), F32),
                pltpu.VMEM((SUBLANES, D_MODEL), F32),
                pltpu.VMEM((2, tm * ROW_TILE, LANES), F32),
                pltpu.SemaphoreType.DMA((2,)),
                pltpu.VMEM((tm, D_MODEL), F32),
            ],
        ),
        compiler_params=_params("arbitrary"),
        name="conv_layer_route",
    )(dest, x, ys, mod0, mod, mod, g1, w_in, conv_w, conv_b, w_out, g2, wr, br, tri)


def _invert_kernel(dest_ref, fill_ref, src_ref, sem):
    fill = pltpu.make_async_copy(fill_ref, src_ref, sem)
    fill.start()
    fill.wait()
    unroll = 8

    def body(t8, carry):
        for k in range(unroll):
            t = t8 * unroll + k
            src_ref[dest_ref[t]] = t
        return carry

    lax.fori_loop(0, dest_ref.shape[0] // unroll, body, 0)


def _invert_permutation(dest, n_out):
    smem = pl.BlockSpec(memory_space=pltpu.SMEM)
    return pl.pallas_call(
        _invert_kernel,
        out_shape=jax.ShapeDtypeStruct((n_out,), jnp.int32),
        in_specs=[smem, pl.BlockSpec(memory_space=pl.ANY)],
        out_specs=smem,
        scratch_shapes=[pltpu.SemaphoreType.DMA(())],
        name="invert_permutation",
    )(dest, jnp.arange(n_out, dtype=jnp.int32) % dest.shape[0])


def _expert_kernel(ea_ref, eb_ref, chg_ref, nused_ref, src_ref, u2_ref, wga_ref, wua_ref, wda_ref,
                   wgb_ref, wub_ref, wdb_ref, wra_ref, wrb_ref, y_ref, xbuf_ref, xsem_ref, *wbf_refs):
    del ea_ref, eb_ref
    j = pl.program_id(0)
    tm = TM_EXPERT

    @pl.when(chg_ref[j] == 1)
    def _():
        for src, dst in zip((wga_ref, wua_ref, wda_ref, wgb_ref, wub_ref, wdb_ref), wbf_refs):
            dst[...] = src[0, 0].astype(BF16)

    rows, start_next = _gathered_tile(src_ref, u2_ref, xbuf_ref, xsem_ref, tm, issue_first=None)

    @pl.when(j < nused_ref[0])
    def _():
        start_next()
        xb = rows.astype(BF16)
        dl = jnp.sum(xb.astype(F32) * (wra_ref[0] - wrb_ref[0]), axis=-1, keepdims=True)
        w_a = _sigmoid(dl)
        w_b = _sigmoid(-dl)

        def ffn(wg_ref, wu_ref, wd_ref):
            gate = _dot(xb, wg_ref[...])
            hidden = gate * _sigmoid(gate) * _dot(xb, wu_ref[...])
            return _dot(hidden.astype(BF16), wd_ref[...])

        y = w_a * ffn(*wbf_refs[0:3]) + w_b * ffn(*wbf_refs[3:6])
        _store_token_rows(y_ref, y, tm)

    @pl.when(j >= nused_ref[0])
    def _():
        start_next()
        y_ref[...] = jnp.zeros_like(y_ref)

    _drain_row_gather(u2_ref, xbuf_ref, xsem_ref)


def _experts(layer, ea, eb, chg, n_used, src, u2, w_gate, w_up, w_down, wr_rows):
    tm = TM_EXPERT
    sel_a = lambda j, ea, eb, *_: (layer, ea[j], 0, 0)
    sel_b = lambda j, ea, eb, *_: (layer, eb[j], 0, 0)
    up_spec = lambda sel: pl.BlockSpec((1, 1, D_MODEL, D_EXPERT), sel)
    down_spec = lambda sel: pl.BlockSpec((1, 1, D_EXPERT, D_MODEL), sel)
    wr_spec = lambda sel: pl.BlockSpec((1, 1, D_MODEL), lambda *a: sel(*a)[1:])
    up_scratch = pltpu.VMEM((D_MODEL, D_EXPERT), BF16)
    down_scratch = pltpu.VMEM((D_EXPERT, D_MODEL), BF16)
    return pl.pallas_call(
        _expert_kernel,
        out_shape=jax.ShapeDtypeStruct((PADDED_ROWS * ROW_TILE, LANES), F32),
        grid_spec=pltpu.PrefetchScalarGridSpec(
            num_scalar_prefetch=4,
            grid=(N_EXPERT_TILES,),
            in_specs=[
                pl.BlockSpec(memory_space=pltpu.SMEM),
                pl.BlockSpec(memory_space=pl.ANY),
                up_spec(sel_a), up_spec(sel_a), down_spec(sel_a),
                up_spec(sel_b), up_spec(sel_b), down_spec(sel_b),
                wr_spec(sel_a), wr_spec(sel_b),
            ],
            out_specs=pl.BlockSpec((tm * ROW_TILE, LANES), lambda j, *_: (j, 0)),
            scratch_shapes=[pltpu.VMEM((EXPERT_GATHER_BUFFERS, tm * ROW_TILE, LANES), F32),
                            pltpu.SemaphoreType.DMA((EXPERT_GATHER_BUFFERS,)),
                            up_scratch, up_scratch, down_scratch, up_scratch, up_scratch, down_scratch],
        ),
        compiler_params=_params("arbitrary"),
        name="grouped_experts",
    )(ea, eb, chg, n_used, src, u2, w_gate, w_up, w_down, w_gate, w_up, w_down, wr_rows, wr_rows)


def _final_kernel(dest_ref, x_ref, ys_ref, mod_ref, g_ref, o_ref, ybuf_ref, ysem_ref):
    y = _gathered_tile(dest_ref, ys_ref, ybuf_ref, ysem_ref, x_ref.shape[0], issue_first=True)
    x = x_ref[...] + mod_ref[0][5:6] * y
    o_ref[...] = _rms(x) * g_ref[...]
    _drain_row_gather(ys_ref, ybuf_ref, ysem_ref)


def _final(dest, x, ys, mod, g):
    t = x.shape[0]
    tm = TM_FINAL
    tiles_per_batch = SEQ // tm
    return pl.pallas_call(
        _final_kernel,
        out_shape=jax.ShapeDtypeStruct((t, D_MODEL), F32),
        grid_spec=pltpu.PrefetchScalarGridSpec(
            num_scalar_prefetch=1,
            grid=(t // tm,),
            in_specs=[
                pl.BlockSpec((tm, D_MODEL), lambda i, *_: (i, 0)),
                pl.BlockSpec(memory_space=pl.ANY),
                pl.BlockSpec((1, 6, D_MODEL), lambda i, *_: (i // tiles_per_batch, 0, 0)),
                _const_spec((1, D_MODEL)),
            ],
            out_specs=pl.BlockSpec((tm, D_MODEL), lambda i, *_: (i, 0)),
            scratch_shapes=[pltpu.VMEM((2, tm * ROW_TILE, LANES), F32), pltpu.SemaphoreType.DMA((2,))],
        ),
        compiler_params=_params("arbitrary"),
        name="final_norm",
    )(dest, x, ys, mod, g)


_PAIR_A = np.array([EXPERTS_PER_GROUP * (c // len(PAIRS)) + PAIRS[c % len(PAIRS)][0]
                    for c in range(N_CLASSES)], np.int32)
_PAIR_B = np.array([EXPERTS_PER_GROUP * (c // len(PAIRS)) + PAIRS[c % len(PAIRS)][1]
                    for c in range(N_CLASSES)], np.int32)


def _moe(layer, u2, cls, rank, cnt, w_gate, w_up, w_down, wr_rows):
    counts = cnt[:N_CLASSES, 0]
    tiles = (counts + TM_EXPERT - 1) // TM_EXPERT
    tile_end = jnp.cumsum(tiles)
    row_start = (tile_end - tiles) * TM_EXPERT
    n_used = tile_end[-1:]
    cls = cls.reshape(-1)
    dest = row_start[cls] + rank.reshape(-1)
    tile_cls = jnp.sum(jnp.arange(N_EXPERT_TILES)[:, None] >= tile_end[None, :], axis=1)
    tile_cls = jnp.minimum(tile_cls, tile_cls[jnp.maximum(n_used[0] - 1, 0)])
    ea = jnp.asarray(_PAIR_A)[tile_cls]
    eb = jnp.asarray(_PAIR_B)[tile_cls]
    chg = jnp.concatenate([jnp.ones((1,), jnp.int32),
                           (tile_cls[1:] != tile_cls[:-1]).astype(jnp.int32)])
    src = _invert_permutation(dest, PADDED_ROWS)
    ys = _experts(layer, ea, eb, chg, n_used.astype(jnp.int32), src, u2, w_gate, w_up, w_down, wr_rows)
    return dest, ys


def kernel(x, c, norm1_g, norm2_g, w_ada, b_ada, m_w_in, m_b_gates, m_norm_g, m_w_out,
           c_w_in, c_conv_w, c_conv_b, c_w_out, w_router, b_router,
           e_w_gate, e_w_up, e_w_down, final_g):
    xf = x.reshape(TOKENS, D_MODEL)
    mod = _ada(c, w_ada, b_ada)

    w_in = jnp.concatenate(
        [m_w_in[0].astype(BF16),
         jnp.zeros((D_MODEL, GATE_COLS - 2 * N_HEADS), BF16)], axis=1)
    bg = jnp.zeros((1, GATE_COLS), F32).at[0, :2 * N_HEADS].set(m_b_gates[0])
    wr = jnp.zeros((D_MODEL, ROUTER_COLS), BF16).at[:, :N_EXPERTS].set(w_router.astype(BF16))
    wr_rows = w_router.astype(BF16).astype(F32).T.reshape(N_EXPERTS, 1, D_MODEL)
    br = b_router.reshape(N_EXPERTS, 1)
    row = lambda v: v.reshape(1, -1)

    q, k, v, og, gcol, grow = _mlstm_proj(xf, mod[0], row(norm1_g[0]), w_in, bg)
    hg = _mlstm_core(q, k, v, og, gcol, grow, row(m_norm_g[0]))
    x1, u2, cls, rank, cnt = _mlstm_out(hg, xf, mod[0], m_w_out[0].astype(BF16), row(norm2_g[0]),
                                        wr, br)
    dest, ys = _moe(0, u2, cls, rank, cnt, e_w_gate, e_w_up, e_w_down, wr_rows)

    x3, u2, cls, rank, cnt = _conv_layer(
        dest, x1, ys, mod[0], mod[1], row(norm1_g[1]), c_w_in[0].astype(BF16), c_conv_w[0],
        row(c_conv_b[0]), c_w_out[0].astype(BF16), row(norm2_g[1]), wr, br)
    dest, ys = _moe(1, u2, cls, rank, cnt, e_w_gate, e_w_up, e_w_down, wr_rows)

    out = _final(dest, x3, ys, mod[1], row(final_g))
    return out.reshape(BATCH, SEQ, D_MODEL)
```

```python
import functools

import jax
import jax.numpy as jnp
import numpy as np
from jax import lax
from jax.experimental import pallas as pl
from jax.experimental.pallas import tpu as pltpu

F32 = jnp.float32
BF16 = jnp.bfloat16

D_MODEL = 1024
BATCH = 4
SEQ = 8192
TOKENS = BATCH * SEQ
N_HEADS = 4
DH_V = 256
DH_QK = 128
QK = N_HEADS * DH_QK
N_EXPERTS = 16
N_GROUPS = 4
EXPERTS_PER_GROUP = 4
D_EXPERT = 512
EPS = 1e-6

LANES = 128
SUBLANES = 8
VMEM_LIMIT_BYTES = 56 * 1024 * 1024

CHUNK = 128
TM_PROJ = 1024
TM_OUT = 1024
TM_CONV = 512
TM_EXPERT = 512
GATE_COLS = LANES
GATE_WIDTH = 16
GATE_KINDS = 3
ROUTER_COLS = LANES
PAIRS = ((0, 1), (0, 2), (0, 3), (1, 2), (1, 3), (2, 3))
N_CLASSES = N_GROUPS * len(PAIRS)
CLASS_ROWS = 32
N_EXPERT_TILES = TOKENS // TM_EXPERT + N_CLASSES
PADDED_ROWS = N_EXPERT_TILES * TM_EXPERT
TM_FINAL = 512
EXPERT_GATHER_BUFFERS = 3
STREAM_GROUP = 4


def _params(*semantics):
    return pltpu.CompilerParams(dimension_semantics=semantics, vmem_limit_bytes=VMEM_LIMIT_BYTES)


def _dot(a, b):
    return jnp.dot(a, b, preferred_element_type=F32)


def _rms(x):
    return x * lax.rsqrt(jnp.mean(x * x, axis=-1, keepdims=True) + EPS)


def _sigmoid(x):
    return 1.0 / (1.0 + jnp.exp(-x))


ROW_TILE = D_MODEL // LANES


def _load_token_rows(ref, n):
    return jnp.concatenate([ref[pl.ds(c, n, stride=ROW_TILE), :] for c in range(ROW_TILE)], axis=1)


def _store_token_rows(ref, val, n):
    for c in range(ROW_TILE):
        ref[pl.ds(c, n, stride=ROW_TILE), :] = val[:, c * LANES:(c + 1) * LANES]


def _token_tile(ref, t):
    return ref.at[pl.ds(pl.multiple_of(t * ROW_TILE, ROW_TILE), ROW_TILE)]


def _start_row_gather(idx_ref, base, n, src_hbm, buf, sem):
    for r in range(n):
        pltpu.make_async_copy(_token_tile(src_hbm, idx_ref[base + r]), _token_tile(buf, r),
                              sem).start(priority=r % 2)


def _wait_row_gather(src_hbm, buf, sem):
    pltpu.make_async_copy(src_hbm.at[pl.ds(0, buf.shape[0])], buf, sem).wait()


def _gathered_tile(idx_ref, src_hbm, buf_ref, sem_ref, n, issue_first, n_tiles=None):
    i = pl.program_id(0)
    last = (pl.num_programs(0) if n_tiles is None else n_tiles) - 1
    nb = buf_ref.shape[0]
    ahead = nb - 1

    @pl.when(i == 0)
    def _():
        for k in range(ahead):
            _start_row_gather(idx_ref, jnp.minimum(k, last) * n, n, src_hbm, buf_ref.at[k], sem_ref.at[k])

    nxt = (i + ahead) % nb
    start_next = functools.partial(
        _start_row_gather, idx_ref, jnp.minimum(i + ahead, last) * n, n, src_hbm, buf_ref.at[nxt],
        sem_ref.at[nxt])
    if issue_first:
        start_next()
    slot = i % nb
    _wait_row_gather(src_hbm, buf_ref.at[slot], sem_ref.at[slot])
    rows = _load_token_rows(buf_ref.at[slot], n)
    if issue_first is None:
        return rows, start_next
    if not issue_first:
        start_next()
    return rows


def _drain_row_gather(src_hbm, buf_ref, sem_ref):
    i = pl.program_id(0)
    nb = buf_ref.shape[0]

    @pl.when(i == pl.num_programs(0) - 1)
    def _():
        for k in range(1, nb):
            _wait_row_gather(src_hbm, buf_ref.at[(i + k) % nb], sem_ref.at[(i + k) % nb])


def _ada_kernel(c_ref, w_ref, b_ref, o_ref):
    c = c_ref[...]
    cond = c * _sigmoid(c)
    o_ref[0] = jnp.dot(cond, w_ref[0], preferred_element_type=F32,
                       precision=lax.Precision.HIGHEST) + b_ref[0]


def _ada(c, w_ada, b_ada):
    depth, d, n = w_ada.shape
    tn = 1536
    c8 = jnp.zeros((SUBLANES, d), F32).at[:BATCH].set(c)
    out = pl.pallas_call(
        _ada_kernel,
        out_shape=jax.ShapeDtypeStruct((depth, SUBLANES, n), F32),
        grid=(depth, n // tn),
        in_specs=[
            pl.BlockSpec((SUBLANES, d), lambda l, j: (0, 0)),
            pl.BlockSpec((1, d, tn), lambda l, j: (l, 0, j)),
            pl.BlockSpec((1, 1, tn), lambda l, j: (l, 0, j)),
        ],
        out_specs=pl.BlockSpec((1, SUBLANES, tn), lambda l, j: (l, 0, j)),
        compiler_params=_params("arbitrary", "arbitrary"),
        name="ada_mod",
    )(c8, w_ada, b_ada.reshape(depth, 1, n))
    return out[:, :BATCH].reshape(depth, BATCH, 6, d)


def _mlstm_proj_kernel(x_ref, mod_ref, g_ref, w_ref, bg_ref, q_ref, k_ref, v_ref, og_ref,
                       gcol_ref, grow_ref):
    x = x_ref[...]
    m = mod_ref[0]
    u = (_rms(x) * g_ref[...] * (1.0 + m[1:2]) + m[0:1]).astype(BF16)
    q_ref[...] = (_dot(u, w_ref[:, 0:QK]) * (DH_QK ** -0.5)).astype(BF16)
    k_ref[...] = _dot(u, w_ref[:, QK:2 * QK]).astype(BF16)
    v_ref[...] = _dot(u, w_ref[:, 2 * QK:2 * QK + D_MODEL]).astype(BF16)
    og_ref[...] = _dot(u, w_ref[:, 2 * QK + D_MODEL:2 * QK + 2 * D_MODEL]).astype(BF16)
    gt = _dot(u, w_ref[:, 2 * QK + 2 * D_MODEL:]) + bg_ref[...]
    lane = lax.broadcasted_iota(jnp.int32, gt.shape, 1)
    pos = lax.broadcasted_iota(jnp.int32, gt.shape, 0) % CHUNK
    log_f = jnp.minimum(gt, 0.0) - jnp.log(1.0 + jnp.exp(-jnp.abs(gt)))
    b = log_f
    shift = 1
    while shift < CHUNK:
        b = b + jnp.where(pos >= shift, pltpu.roll(b, shift, axis=0), 0.0)
        shift *= 2
    imb = gt - pltpu.roll(b, LANES - N_HEADS, axis=1)
    cm = imb
    shift = 1
    while shift < CHUNK:
        cm = jnp.maximum(cm, jnp.where(pos >= shift, pltpu.roll(cm, shift, axis=0), -jnp.inf))
        shift *= 2
    cols = jnp.where(lane < N_HEADS, cm,
                     jnp.where(lane < 2 * N_HEADS, b,
                               jnp.where(lane < 3 * N_HEADS, pltpu.roll(imb, 2 * N_HEADS, axis=1), 0.0)))
    gcol_ref[...] = cols[:, 0:GATE_WIDTH]
    grow_ref[0] = imb.T[0:SUBLANES, :]


def _mlstm_proj(x, mod, g, w, bg):
    t, d = x.shape
    n = w.shape[1]
    tm = TM_PROJ
    tiles_per_batch = SEQ // tm
    return pl.pallas_call(
        _mlstm_proj_kernel,
        out_shape=(
            jax.ShapeDtypeStruct((t, QK), BF16),
            jax.ShapeDtypeStruct((t, QK), BF16),
            jax.ShapeDtypeStruct((t, D_MODEL), BF16),
            jax.ShapeDtypeStruct((t, D_MODEL), BF16),
            jax.ShapeDtypeStruct((t, GATE_WIDTH), F32),
            jax.ShapeDtypeStruct((BATCH, SUBLANES, SEQ), F32),
        ),
        grid=(t // tm,),
        in_specs=[
            pl.BlockSpec((tm, d), lambda i, *_: (i, 0)),
            pl.BlockSpec((1, 6, d), lambda i, *_: (i // tiles_per_batch, 0, 0)),
            pl.BlockSpec((1, d), lambda i, *_: (0, 0)),
            pl.BlockSpec((d, n), lambda i, *_: (0, 0)),
            pl.BlockSpec((1, GATE_COLS), lambda i, *_: (0, 0)),
        ],
        out_specs=(
            pl.BlockSpec((tm, QK), lambda i, *_: (i, 0)),
            pl.BlockSpec((tm, QK), lambda i, *_: (i, 0)),
            pl.BlockSpec((tm, D_MODEL), lambda i, *_: (i, 0)),
            pl.BlockSpec((tm, D_MODEL), lambda i, *_: (i, 0)),
            pl.BlockSpec((tm, GATE_WIDTH), lambda i, *_: (i, 0)),
            pl.BlockSpec((1, SUBLANES, tm),
                         lambda i, *_: (i // tiles_per_batch, 0, i % tiles_per_batch)),
        ),
        compiler_params=_params("arbitrary"),
        name="mlstm_proj",
    )(x, mod, g, w, bg)


def _mlstm_core_kernel(q_ref, k_ref, v_ref, og_ref, gcol_ref, grow_ref, ng_ref, rep_ref, o_ref,
                       m_ref, *c_refs):
    @pl.when(pl.program_id(0) == 0)
    def _():
        m_ref[...] = jnp.zeros_like(m_ref)
        for c_ref in c_refs:
            c_ref[...] = jnp.zeros_like(c_ref)

    ln = CHUNK
    row = lax.broadcasted_iota(jnp.int32, (ln, ln), 0)
    col = lax.broadcasted_iota(jnp.int32, (ln, ln), 1)
    causal = col <= row
    ones = jnp.ones((ln, LANES), BF16)
    n_streams = BATCH * N_HEADS
    m_all = [m_ref[st] for st in range(n_streams)]
    m_out = [None] * n_streams
    for bi in range(BATCH):
        g = gcol_ref[bi]
        g_hi = g.astype(BF16)
        g_r1 = g - g_hi.astype(F32)
        g_mid = g_r1.astype(BF16)
        g_lo = (g_r1 - g_mid.astype(F32)).astype(BF16)
        rep = _dot(g_hi, rep_ref[...]) + _dot(g_mid, rep_ref[...]) + _dot(g_lo, rep_ref[...])
        for h in range(N_HEADS):
            st = bi * N_HEADS + h
            lanes = lambda kind: slice((kind * N_HEADS + h) * LANES, (kind * N_HEADS + h + 1) * LANES)
            cm, b, imb_col = rep[:, lanes(0)], rep[:, lanes(1)], rep[:, lanes(2)]
            imb_row = grow_ref[bi, h:h + 1, :]
            qh = q_ref[bi, :, h * DH_QK:(h + 1) * DH_QK]
            kh = k_ref[bi, :, h * DH_QK:(h + 1) * DH_QK]
            v_ext = jnp.concatenate([v_ref[bi, :, h * DH_V:(h + 1) * DH_V], ones], axis=1)
            c_ref = c_refs[st]
            c_prev = c_ref[...]
            m_prev = m_all[st]

            big_m = jnp.maximum(m_prev, cm)
            d_mat = jnp.exp(jnp.where(causal, imb_row - big_m, -jnp.inf))
            s = lax.dot_general(qh, kh, (((1,), (1,)), ((), ())), preferred_element_type=F32) * d_mat
            q_inter = (qh.astype(F32) * jnp.exp(m_prev - big_m)).astype(BF16)
            lhs = jnp.concatenate([q_inter, s.astype(BF16)], axis=1)
            rhs = jnp.concatenate([c_prev.astype(BF16), v_ext], axis=0)
            nd = _dot(lhs, rhs)
            den = nd[:, DH_V:]
            inv = 1.0 / jnp.maximum(jnp.abs(den), jnp.exp(-(b + big_m)))
            hh = nd[:, :DH_V] * jnp.concatenate([inv, inv], axis=1)

            m_last = big_m[ln - 1:ln, :]
            kw = (kh.astype(F32) * jnp.exp(imb_col - m_last)).astype(BF16)
            decay = jnp.exp(m_prev - m_last)
            c_ref[...] = jnp.concatenate([decay] * 3, axis=1) * c_prev + lax.dot_general(
                kw, v_ext, (((0,), (0,)), ((), ())), preferred_element_type=F32)
            m_out[st] = b[ln - 1:ln, :] + m_last

            sl = slice(h * DH_V, (h + 1) * DH_V)
            gate = _sigmoid(og_ref[bi, :, sl].astype(F32))
            o_ref[bi, :, sl] = (_rms(hh) * ng_ref[:, sl] * gate).astype(BF16)
    for st in range(n_streams):
        m_ref[st] = m_out[st]


def _mlstm_core(q, k, v, og, gcol, grow, norm_g):
    nc = SEQ // CHUNK
    n_streams = BATCH * N_HEADS
    per_batch = lambda a: a.reshape(BATCH, SEQ, a.shape[-1])
    chunk_rows = lambda width: pl.BlockSpec((BATCH, CHUNK, width), lambda c: (0, c, 0))
    n_rep = GATE_KINDS * N_HEADS
    replicate = (jnp.arange(GATE_WIDTH)[:, None] == jnp.arange(n_rep * LANES)[None, :] // LANES
                 ).astype(BF16)
    out = pl.pallas_call(
        _mlstm_core_kernel,
        out_shape=jax.ShapeDtypeStruct((BATCH, SEQ, D_MODEL), BF16),
        grid=(nc,),
        in_specs=[
            chunk_rows(QK),
            chunk_rows(QK),
            chunk_rows(D_MODEL),
            chunk_rows(D_MODEL),
            chunk_rows(GATE_WIDTH),
            pl.BlockSpec((BATCH, SUBLANES, CHUNK), lambda c: (0, 0, c)),
            pl.BlockSpec((1, D_MODEL), lambda c: (0, 0)),
            pl.BlockSpec((GATE_WIDTH, n_rep * LANES), lambda c: (0, 0)),
        ],
        out_specs=chunk_rows(D_MODEL),
        scratch_shapes=[pltpu.VMEM((n_streams, 1, LANES), F32)]
        + [pltpu.VMEM((DH_QK, DH_V + LANES), F32) for _ in range(n_streams)],
        compiler_params=_params("arbitrary"),
        name="mlstm_core",
    )(per_batch(q), per_batch(k), per_batch(v), per_batch(og), per_batch(gcol), grow, norm_g,
      replicate)
    return out.reshape(TOKENS, D_MODEL)


def _top2_sum(v0, v1, v2, v3):
    hi1, lo1 = jnp.maximum(v0, v1), jnp.minimum(v0, v1)
    hi2, lo2 = jnp.maximum(v2, v3), jnp.minimum(v2, v3)
    return jnp.maximum(hi1, hi2) + jnp.maximum(jnp.minimum(hi1, hi2), jnp.maximum(lo1, lo2))


def _route_tail(x_new, m, valid, g2_ref, wr_ref, br_ref, tri_ref, u2_ref, cls_ref, rank_ref, cnt_ref,
                run_ref, after_router=None):
    u2 = _rms(x_new) * g2_ref[...] * (1.0 + m[4:5]) + m[3:4]
    _store_token_rows(u2_ref, u2, u2.shape[0])
    logits = _dot(u2.astype(BF16), wr_ref[...])
    if after_router is not None:
        after_router()
    lt = logits.T[0:N_EXPERTS, :]
    e = jnp.exp(lt - jnp.max(lt, axis=0, keepdims=True))
    probs = e / jnp.sum(e, axis=0, keepdims=True)
    sel = probs + br_ref[...]
    sel_rows = [sel[j:j + 1, :] for j in range(N_EXPERTS)]
    best = jnp.zeros_like(sel_rows[0], dtype=jnp.int32)
    best_score = _top2_sum(*sel_rows[0:EXPERTS_PER_GROUP])
    for g in range(1, N_GROUPS):
        score = _top2_sum(*sel_rows[g * EXPERTS_PER_GROUP:(g + 1) * EXPERTS_PER_GROUP])
        better = score > best_score
        best = jnp.where(better, g, best)
        best_score = jnp.where(better, score, best_score)
    s = []
    for j in range(EXPERTS_PER_GROUP):
        sj = sel_rows[j]
        for g in range(1, N_GROUPS):
            sj = jnp.where(best == g, sel_rows[g * EXPERTS_PER_GROUP + j], sj)
        s.append(sj)
    chosen = []
    for j in range(EXPERTS_PER_GROUP):
        beaten = jnp.zeros_like(best)
        for i in range(EXPERTS_PER_GROUP):
            if i == j:
                continue
            wins = (s[i] >= s[j]) if i < j else (s[i] > s[j])
            beaten = beaten + jnp.where(wins, 1, 0)
        chosen.append(beaten < 2)
    pair = jnp.full_like(best, len(PAIRS) - 1)
    for p in range(len(PAIRS) - 2, -1, -1):
        a, b = PAIRS[p]
        pair = jnp.where(jnp.logical_and(chosen[a], chosen[b]), p, pair)
    cls = best * len(PAIRS) + pair
    cls_ref[0] = cls

    class_id = lax.broadcasted_iota(jnp.int32, (CLASS_ROWS, cls.shape[1]), 0)
    onehot = class_id == cls
    before = _dot(jnp.where(onehot, 1.0, 0.0).astype(BF16), tri_ref[...])
    run = run_ref[...]
    rank = jnp.sum(jnp.where(onehot, before + run[:, 0:1], 0.0), axis=0, keepdims=True)
    rank_ref[0] = rank.astype(jnp.int32)
    run = run + valid * jnp.sum(jnp.where(onehot, 1.0, 0.0), axis=1, keepdims=True)
    run_ref[...] = run
    cnt_ref[...] = run.astype(jnp.int32)


def _strict_upper(n):
    return (jnp.arange(n)[:, None] < jnp.arange(n)[None, :]).astype(BF16)


def _route_out_shapes(t, tm):
    return (
        jax.ShapeDtypeStruct((t, D_MODEL), F32),
        jax.ShapeDtypeStruct((t * ROW_TILE, LANES), F32),
        jax.ShapeDtypeStruct((t // tm, 1, tm), jnp.int32),
        jax.ShapeDtypeStruct((t // tm, 1, tm), jnp.int32),
        jax.ShapeDtypeStruct((CLASS_ROWS, LANES), jnp.int32),
    )


def _cur_tile(n_tiles):
    return lambda i: jnp.minimum(i, n_tiles - 1)


def _prev_tile(i):
    return jnp.maximum(i - 1, 0)


def _route_out_specs(tm, n_tiles):
    cur = _cur_tile(n_tiles)
    return (
        pl.BlockSpec((tm, D_MODEL), lambda i, *_: (cur(i), 0)),
        pl.BlockSpec((tm * ROW_TILE, LANES), lambda i, *_: (_prev_tile(i), 0)),
        pl.BlockSpec((1, 1, tm), lambda i, *_: (_prev_tile(i), 0, 0)),
        pl.BlockSpec((1, 1, tm), lambda i, *_: (_prev_tile(i), 0, 0)),
        pl.BlockSpec((CLASS_ROWS, LANES), lambda i, *_: (0, 0)),
    )


def _init_deferred_tail(xprev_ref, run_ref):
    @pl.when(pl.program_id(0) == 0)
    def _():
        xprev_ref[...] = jnp.zeros_like(xprev_ref)
        run_ref[...] = jnp.zeros_like(run_ref)


def _deferred_tail(xprev_ref, modp_ref, *tail_refs, after_router=None):
    valid = jnp.where(pl.program_id(0) > 0, 1.0, 0.0)
    _route_tail(xprev_ref[...], modp_ref[0], valid, *tail_refs, after_router=after_router)


def _const_spec(shape):
    return pl.BlockSpec(shape, lambda i, *_: (0,) * len(shape))


def _mlstm_out_kernel(h_ref, x_ref, mod_ref, modp_ref, wo_ref, g2_ref, wr_ref, br_ref, tri_ref,
                      x1_ref, u2_ref, cls_ref, rank_ref, cnt_ref, run_ref, xprev_ref):
    _init_deferred_tail(xprev_ref, run_ref)
    quarter = D_MODEL // 4
    parts = []

    def project(*quarters):
        for j in quarters:
            parts.append(_dot(h_ref[...], wo_ref[:, j * quarter:(j + 1) * quarter]))

    project(0)
    _deferred_tail(xprev_ref, modp_ref, g2_ref, wr_ref, br_ref, tri_ref, u2_ref, cls_ref, rank_ref,
                   cnt_ref, run_ref, after_router=functools.partial(project, 1, 2))
    project(3)
    x1 = x_ref[...] + mod_ref[0][2:3] * jnp.concatenate(parts, axis=1)
    x1_ref[...] = x1
    xprev_ref[...] = x1


def _mlstm_out(hg, x, mod, w_out, g2, wr, br):
    t = x.shape[0]
    tm = TM_OUT
    tri = _strict_upper(tm)
    tiles_per_batch = SEQ // tm
    n_tiles = t // tm
    cur = _cur_tile(n_tiles)
    return pl.pallas_call(
        _mlstm_out_kernel,
        out_shape=_route_out_shapes(t, tm),
        grid=(n_tiles + 1,),
        in_specs=[
            pl.BlockSpec((tm, D_MODEL), lambda i, *_: (cur(i), 0)),
            pl.BlockSpec((tm, D_MODEL), lambda i, *_: (cur(i), 0)),
            pl.BlockSpec((1, 6, D_MODEL), lambda i, *_: (cur(i) // tiles_per_batch, 0, 0)),
            pl.BlockSpec((1, 6, D_MODEL), lambda i, *_: (_prev_tile(i) // tiles_per_batch, 0, 0)),
            _const_spec((D_MODEL, D_MODEL)),
            _const_spec((1, D_MODEL)),
            _const_spec((D_MODEL, ROUTER_COLS)),
            _const_spec((N_EXPERTS, 1)),
            _const_spec((tm, tm)),
        ],
        out_specs=_route_out_specs(tm, n_tiles),
        scratch_shapes=[pltpu.VMEM((CLASS_ROWS, LANES), F32), pltpu.VMEM((tm, D_MODEL), F32)],
        compiler_params=_params("arbitrary"),
        name="mlstm_out_route",
    )(hg, x, mod, mod, w_out, g2, wr, br, tri)


def _conv_layer_kernel(dest_ref, x_ref, ys_ref, mod0_ref, mod_ref, modp_ref, g1_ref, wi_ref, cw_ref,
                       cb_ref, wo_ref, g2_ref, wr_ref, br_ref, tri_ref,
                       x3_ref, u2_ref, cls_ref, rank_ref, cnt_ref,
                       run_ref, carry_ref, ybuf_ref, ysem_ref, xprev_ref):
    tm = x_ref.shape[0]
    tiles_per_batch = SEQ // tm
    n_tiles = pl.num_programs(0) - 1
    i = pl.program_id(0)

    @pl.when(i % tiles_per_batch == 0)
    def _():
        carry_ref[...] = jnp.zeros_like(carry_ref)

    _init_deferred_tail(xprev_ref, run_ref)
    m = mod_ref[0]
    y = _gathered_tile(dest_ref, ys_ref, ybuf_ref, ysem_ref, tm, issue_first=False, n_tiles=n_tiles)
    x2 = x_ref[...] + mod0_ref[0][5:6] * y
    u = (_rms(x2) * g1_ref[...] * (1.0 + m[1:2]) + m[0:1]).astype(BF16)
    bgate = _dot(u, wi_ref[:, 0:D_MODEL])
    gated = []

    def project_gated():
        gated.append(_dot(u, wi_ref[:, D_MODEL:2 * D_MODEL]) * _dot(u, wi_ref[:, 2 * D_MODEL:]))

    _deferred_tail(xprev_ref, modp_ref, g2_ref, wr_ref, br_ref, tri_ref, u2_ref, cls_ref, rank_ref,
                   cnt_ref, run_ref, after_router=project_gated)
    z = gated[0]
    prev = carry_ref[...]
    row = lax.broadcasted_iota(jnp.int32, z.shape, 0)
    z1 = jnp.where(row == 0, prev[7:8], pltpu.roll(z, 1, axis=0))
    z2 = jnp.where(row == 0, prev[6:7], jnp.where(row == 1, prev[7:8], pltpu.roll(z, 2, axis=0)))
    carry_ref[...] = z[tm - SUBLANES:, :]
    cw = cw_ref[...]
    zc = cw[0:1] * z2 + cw[1:2] * z1 + cw[2:3] * z + cb_ref[...]
    x3 = x2 + m[2:3] * _dot((bgate * zc).astype(BF16), wo_ref[...])
    xprev_ref[...] = x3

    @pl.when(i < n_tiles)
    def _():
        x3_ref[...] = x3

    _drain_row_gather(ys_ref, ybuf_ref, ysem_ref)


def _conv_layer(dest, x, ys, mod0, mod, g1, w_in, conv_w, conv_b, w_out, g2, wr, br):
    t = x.shape[0]
    tm = TM_CONV
    tri = _strict_upper(tm)
    tiles_per_batch = SEQ // tm
    n_tiles = t // tm
    cur = _cur_tile(n_tiles)
    mod_spec = pl.BlockSpec((1, 6, D_MODEL), lambda i, *_: (cur(i) // tiles_per_batch, 0, 0))
    modp_spec = pl.BlockSpec((1, 6, D_MODEL), lambda i, *_: (_prev_tile(i) // tiles_per_batch, 0, 0))
    return pl.pallas_call(
        _conv_layer_kernel,
        out_shape=_route_out_shapes(t, tm),
        grid_spec=pltpu.PrefetchScalarGridSpec(
            num_scalar_prefetch=1,
            grid=(n_tiles + 1,),
            in_specs=[
                pl.BlockSpec((tm, D_MODEL), lambda i, *_: (cur(i), 0)),
                pl.BlockSpec(memory_space=pl.ANY),
                mod_spec,
                mod_spec,
                modp_spec,
                _const_spec((1, D_MODEL)),
                _const_spec((D_MODEL, 3 * D_MODEL)),
                _const_spec((3, D_MODEL)),
                _const_spec((1, D_MODEL)),
                _const_spec((D_MODEL, D_MODEL)),
                _const_spec((1, D_MODEL)),
                _const_spec((D_MODEL, ROUTER_COLS)),
                _const_spec((N_EXPERTS, 1)),
                _const_spec((tm, tm)),
            ],
            out_specs=_route_out_specs(tm, n_tiles),
            scratch_shapes=[
                pltpu.VMEM((CLASS_ROWS, LANES), F32),
                pltpu.VMEM((SUBLANES, D_MODEL), F32),
                pltpu.VMEM((2, tm * ROW_TILE, LANES), F32),
                pltpu.SemaphoreType.DMA((2,)),
                pltpu.VMEM((tm, D_MODEL), F32),
            ],
        ),
        compiler_params=_params("arbitrary"),
        name="conv_layer_route",
    )(dest, x, ys, mod0, mod, mod, g1, w_in, conv_w, conv_b, w_out, g2, wr, br, tri)


def _invert_kernel(dest_ref, fill_ref, src_ref, sem):
    fill = pltpu.make_async_copy(fill_ref, src_ref, sem)
    fill.start()
    fill.wait()
    unroll = 8

    def body(t8, carry):
        for k in range(unroll):
            t = t8 * unroll + k
            src_ref[dest_ref[t]] = t
        return carry

    lax.fori_loop(0, dest_ref.shape[0] // unroll, body, 0)


def _invert_permutation(dest, n_out):
    smem = pl.BlockSpec(memory_space=pltpu.SMEM)
    return pl.pallas_call(
        _invert_kernel,
        out_shape=jax.ShapeDtypeStruct((n_out,), jnp.int32),
        in_specs=[smem, pl.BlockSpec(memory_space=pl.ANY)],
        out_specs=smem,
        scratch_shapes=[pltpu.SemaphoreType.DMA(())],
        name="invert_permutation",
    )(dest, jnp.arange(n_out, dtype=jnp.int32) % dest.shape[0])


def _expert_kernel(ea_ref, eb_ref, nused_ref, src_ref, u2_ref, wga_ref, wua_ref, wda_ref,
                   wgb_ref, wub_ref, wdb_ref, wra_ref, wrb_ref, y_ref, xbuf_ref, xsem_ref):
    del ea_ref, eb_ref
    j = pl.program_id(0)
    tm = TM_EXPERT

    rows, start_next = _gathered_tile(src_ref, u2_ref, xbuf_ref, xsem_ref, tm, issue_first=None)

    @pl.when(j < nused_ref[0])
    def _():
        start_next()
        xb = rows.astype(BF16)
        dl = jnp.sum(xb.astype(F32) * (wra_ref[0] - wrb_ref[0]), axis=-1, keepdims=True)
        w_a = _sigmoid(dl)
        w_b = _sigmoid(-dl)

        def ffn(wg_ref, wu_ref, wd_ref):
            gate = _dot(xb, wg_ref[0, 0])
            hidden = gate * _sigmoid(gate) * _dot(xb, wu_ref[0, 0])
            return _dot(hidden.astype(BF16), wd_ref[0, 0])

        y = w_a * ffn(wga_ref, wua_ref, wda_ref) + w_b * ffn(wgb_ref, wub_ref, wdb_ref)
        _store_token_rows(y_ref, y, tm)

    @pl.when(j >= nused_ref[0])
    def _():
        start_next()
        y_ref[...] = jnp.zeros_like(y_ref)

    _drain_row_gather(u2_ref, xbuf_ref, xsem_ref)


def _experts(layer, ea, eb, n_used, src, u2, w_gate, w_up, w_down, wr_rows):
    tm = TM_EXPERT
    sel_a = lambda j, ea, eb, *_: (layer, ea[j], 0, 0)
    sel_b = lambda j, ea, eb, *_: (layer, eb[j], 0, 0)
    up_spec = lambda sel: pl.BlockSpec((1, 1, D_MODEL, D_EXPERT), sel)
    down_spec = lambda sel: pl.BlockSpec((1, 1, D_EXPERT, D_MODEL), sel)
    wr_spec = lambda sel: pl.BlockSpec((1, 1, D_MODEL), lambda *a: sel(*a)[1:])
    return pl.pallas_call(
        _expert_kernel,
        out_shape=jax.ShapeDtypeStruct((PADDED_ROWS * ROW_TILE, LANES), F32),
        grid_spec=pltpu.PrefetchScalarGridSpec(
            num_scalar_prefetch=3,
            grid=(N_EXPERT_TILES,),
            in_specs=[
                pl.BlockSpec(memory_space=pltpu.SMEM),
                pl.BlockSpec(memory_space=pl.ANY),
                up_spec(sel_a), up_spec(sel_a), down_spec(sel_a),
                up_spec(sel_b), up_spec(sel_b), down_spec(sel_b),
                wr_spec(sel_a), wr_spec(sel_b),
            ],
            out_specs=pl.BlockSpec((tm * ROW_TILE, LANES), lambda j, *_: (j, 0)),
            scratch_shapes=[pltpu.VMEM((EXPERT_GATHER_BUFFERS, tm * ROW_TILE, LANES), F32),
                            pltpu.SemaphoreType.DMA((EXPERT_GATHER_BUFFERS,))],
        ),
        compiler_params=_params("arbitrary"),
        name="grouped_experts",
    )(ea, eb, n_used, src, u2, w_gate, w_up, w_down, w_gate, w_up, w_down, wr_rows, wr_rows)


def _final_kernel(dest_ref, x_ref, ys_ref, mod_ref, g_ref, o_ref, ybuf_ref, ysem_ref):
    y = _gathered_tile(dest_ref, ys_ref, ybuf_ref, ysem_ref, x_ref.shape[0], issue_first=True)
    x = x_ref[...] + mod_ref[0][5:6] * y
    o_ref[...] = _rms(x) * g_ref[...]
    _drain_row_gather(ys_ref, ybuf_ref, ysem_ref)


def _final(dest, x, ys, mod, g):
    t = x.shape[0]
    tm = TM_FINAL
    tiles_per_batch = SEQ // tm
    return pl.pallas_call(
        _final_kernel,
        out_shape=jax.ShapeDtypeStruct((t, D_MODEL), F32),
        grid_spec=pltpu.PrefetchScalarGridSpec(
            num_scalar_prefetch=1,
            grid=(t // tm,),
            in_specs=[
                pl.BlockSpec((tm, D_MODEL), lambda i, *_: (i, 0)),
                pl.BlockSpec(memory_space=pl.ANY),
                pl.BlockSpec((1, 6, D_MODEL), lambda i, *_: (i // tiles_per_batch, 0, 0)),
                _const_spec((1, D_MODEL)),
            ],
            out_specs=pl.BlockSpec((tm, D_MODEL), lambda i, *_: (i, 0)),
            scratch_shapes=[pltpu.VMEM((2, tm * ROW_TILE, LANES), F32), pltpu.SemaphoreType.DMA((2,))],
        ),
        compiler_params=_params("arbitrary"),
        name="final_norm",
    )(dest, x, ys, mod, g)


_PAIR_A = np.array([EXPERTS_PER_GROUP * (c // len(PAIRS)) + PAIRS[c % len(PAIRS)][0]
                    for c in range(N_CLASSES)], np.int32)
_PAIR_B = np.array([EXPERTS_PER_GROUP * (c // len(PAIRS)) + PAIRS[c % len(PAIRS)][1]
                    for c in range(N_CLASSES)], np.int32)


def _moe(layer, u2, cls, rank, cnt, w_gate, w_up, w_down, wr_rows):
    counts = cnt[:N_CLASSES, 0]
    tiles = (counts + TM_EXPERT - 1) // TM_EXPERT
    tile_end = jnp.cumsum(tiles)
    row_start = (tile_end - tiles) * TM_EXPERT
    n_used = tile_end[-1:]
    cls = cls.reshape(-1)
    dest = row_start[cls] + rank.reshape(-1)
    tile_cls = jnp.sum(jnp.arange(N_EXPERT_TILES)[:, None] >= tile_end[None, :], axis=1)
    tile_cls = jnp.minimum(tile_cls, tile_cls[jnp.maximum(n_used[0] - 1, 0)])
    ea = jnp.asarray(_PAIR_A)[tile_cls]
    eb = jnp.asarray(_PAIR_B)[tile_cls]
    src = _invert_permutation(dest, PADDED_ROWS)
    ys = _experts(layer, ea, eb, n_used.astype(jnp.int32), src, u2, w_gate, w_up, w_down, wr_rows)
    return dest, ys


def kernel(x, c, norm1_g, norm2_g, w_ada, b_ada, m_w_in, m_b_gates, m_norm_g, m_w_out,
           c_w_in, c_conv_w, c_conv_b, c_w_out, w_router, b_router,
           e_w_gate, e_w_up, e_w_down, final_g):
    xf = x.reshape(TOKENS, D_MODEL)
    mod = _ada(c, w_ada, b_ada)

    w_in = jnp.concatenate(
        [m_w_in[0].astype(BF16),
         jnp.zeros((D_MODEL, GATE_COLS - 2 * N_HEADS), BF16)], axis=1)
    bg = jnp.zeros((1, GATE_COLS), F32).at[0, :2 * N_HEADS].set(m_b_gates[0])
    wr = jnp.zeros((D_MODEL, ROUTER_COLS), BF16).at[:, :N_EXPERTS].set(w_router.astype(BF16))
    wr_rows = w_router.astype(BF16).astype(F32).T.reshape(N_EXPERTS, 1, D_MODEL)
    br = b_router.reshape(N_EXPERTS, 1)
    row = lambda v: v.reshape(1, -1)

    q, k, v, og, gcol, grow = _mlstm_proj(xf, mod[0], row(norm1_g[0]), w_in, bg)
    hg = _mlstm_core(q, k, v, og, gcol, grow, row(m_norm_g[0]))
    x1, u2, cls, rank, cnt = _mlstm_out(hg, xf, mod[0], m_w_out[0].astype(BF16), row(norm2_g[0]),
                                        wr, br)
    w_gate, w_up, w_down = e_w_gate.astype(BF16), e_w_up.astype(BF16), e_w_down.astype(BF16)
    dest, ys = _moe(0, u2, cls, rank, cnt, w_gate, w_up, w_down, wr_rows)

    x3, u2, cls, rank, cnt = _conv_layer(
        dest, x1, ys, mod[0], mod[1], row(norm1_g[1]), c_w_in[0].astype(BF16), c_conv_w[0],
        row(c_conv_b[0]), c_w_out[0].astype(BF16), row(norm2_g[1]), wr, br)
    dest, ys = _moe(1, u2, cls, rank, cnt, w_gate, w_up, w_down, wr_rows)

    out = _final(dest, x3, ys, mod[1], row(final_g))
    return out.reshape(BATCH, SEQ, D_MODEL)
```

```python
import functools

import jax
import jax.numpy as jnp
import numpy as np
from jax import lax
from jax.experimental import pallas as pl
from jax.experimental.pallas import tpu as pltpu

F32 = jnp.float32
BF16 = jnp.bfloat16

D_MODEL = 1024
BATCH = 4
SEQ = 8192
TOKENS = BATCH * SEQ
N_HEADS = 4
DH_V = 256
DH_QK = 128
QK = N_HEADS * DH_QK
N_EXPERTS = 16
N_GROUPS = 4
EXPERTS_PER_GROUP = 4
D_EXPERT = 512
EPS = 1e-6

LANES = 128
SUBLANES = 8
VMEM_LIMIT_BYTES = 56 * 1024 * 1024

CHUNK = 128
TM_PROJ = 1024
TM_CONV = 512
TM_EXPERT = 512
GATE_COLS = LANES
GATE_WIDTH = 16
GATE_KINDS = 3
ROUTER_COLS = LANES
PAIRS = ((0, 1), (0, 2), (0, 3), (1, 2), (1, 3), (2, 3))
N_CLASSES = N_GROUPS * len(PAIRS)
CLASS_ROWS = 32
N_EXPERT_TILES = TOKENS // TM_EXPERT + N_CLASSES
PADDED_ROWS = N_EXPERT_TILES * TM_EXPERT
TM_FINAL = 512
EXPERT_GATHER_BUFFERS = 3
STREAM_GROUP = 16


def _params(*semantics):
    return pltpu.CompilerParams(dimension_semantics=semantics, vmem_limit_bytes=VMEM_LIMIT_BYTES)


def _dot(a, b):
    return jnp.dot(a, b, preferred_element_type=F32)


def _rms(x):
    return x * lax.rsqrt(jnp.mean(x * x, axis=-1, keepdims=True) + EPS)


def _sigmoid(x):
    return 1.0 / (1.0 + jnp.exp(-x))


ROW_TILE = D_MODEL // LANES


def _load_token_rows(ref, n):
    return jnp.concatenate([ref[pl.ds(c, n, stride=ROW_TILE), :] for c in range(ROW_TILE)], axis=1)


def _store_token_rows(ref, val, n):
    for c in range(ROW_TILE):
        ref[pl.ds(c, n, stride=ROW_TILE), :] = val[:, c * LANES:(c + 1) * LANES]


def _token_tile(ref, t):
    return ref.at[pl.ds(pl.multiple_of(t * ROW_TILE, ROW_TILE), ROW_TILE)]


def _start_row_gather(idx_ref, base, n, src_hbm, buf, sem):
    for r in range(n):
        pltpu.make_async_copy(_token_tile(src_hbm, idx_ref[base + r]), _token_tile(buf, r),
                              sem).start(priority=r % 2)


def _wait_row_gather(src_hbm, buf, sem):
    pltpu.make_async_copy(src_hbm.at[pl.ds(0, buf.shape[0])], buf, sem).wait()


def _gathered_tile(idx_ref, src_hbm, buf_ref, sem_ref, n, issue_first, n_tiles=None):
    i = pl.program_id(0)
    last = (pl.num_programs(0) if n_tiles is None else n_tiles) - 1
    nb = buf_ref.shape[0]
    ahead = nb - 1

    @pl.when(i == 0)
    def _():
        for k in range(ahead):
            _start_row_gather(idx_ref, jnp.minimum(k, last) * n, n, src_hbm, buf_ref.at[k], sem_ref.at[k])

    nxt = (i + ahead) % nb
    start_next = functools.partial(
        _start_row_gather, idx_ref, jnp.minimum(i + ahead, last) * n, n, src_hbm, buf_ref.at[nxt],
        sem_ref.at[nxt])
    if issue_first:
        start_next()
    slot = i % nb
    _wait_row_gather(src_hbm, buf_ref.at[slot], sem_ref.at[slot])
    rows = _load_token_rows(buf_ref.at[slot], n)
    if issue_first is None:
        return rows, start_next
    if not issue_first:
        start_next()
    return rows


def _drain_row_gather(src_hbm, buf_ref, sem_ref):
    i = pl.program_id(0)
    nb = buf_ref.shape[0]

    @pl.when(i == pl.num_programs(0) - 1)
    def _():
        for k in range(1, nb):
            _wait_row_gather(src_hbm, buf_ref.at[(i + k) % nb], sem_ref.at[(i + k) % nb])


def _ada_kernel(c_ref, w_ref, b_ref, o_ref):
    c = c_ref[...]
    cond = c * _sigmoid(c)
    o_ref[0] = jnp.dot(cond, w_ref[0], preferred_element_type=F32,
                       precision=lax.Precision.HIGHEST) + b_ref[0]


def _ada(c, w_ada, b_ada):
    depth, d, n = w_ada.shape
    tn = 1536
    c8 = jnp.zeros((SUBLANES, d), F32).at[:BATCH].set(c)
    out = pl.pallas_call(
        _ada_kernel,
        out_shape=jax.ShapeDtypeStruct((depth, SUBLANES, n), F32),
        grid=(depth, n // tn),
        in_specs=[
            pl.BlockSpec((SUBLANES, d), lambda l, j: (0, 0)),
            pl.BlockSpec((1, d, tn), lambda l, j: (l, 0, j)),
            pl.BlockSpec((1, 1, tn), lambda l, j: (l, 0, j)),
        ],
        out_specs=pl.BlockSpec((1, SUBLANES, tn), lambda l, j: (l, 0, j)),
        compiler_params=_params("arbitrary", "arbitrary"),
        name="ada_mod",
    )(c8, w_ada, b_ada.reshape(depth, 1, n))
    return out[:, :BATCH].reshape(depth, BATCH, 6, d)


def _mlstm_proj_kernel(x_ref, mod_ref, g_ref, w_ref, bg_ref, q_ref, k_ref, v_ref, og_ref,
                       gcol_ref, grow_ref):
    x = x_ref[...]
    m = mod_ref[0]
    u = (_rms(x) * g_ref[...] * (1.0 + m[1:2]) + m[0:1]).astype(BF16)
    q_ref[...] = (_dot(u, w_ref[:, 0:QK]) * (DH_QK ** -0.5)).astype(BF16)
    k_ref[...] = _dot(u, w_ref[:, QK:2 * QK]).astype(BF16)
    v_ref[...] = _dot(u, w_ref[:, 2 * QK:2 * QK + D_MODEL]).astype(BF16)
    og_ref[...] = _dot(u, w_ref[:, 2 * QK + D_MODEL:2 * QK + 2 * D_MODEL]).astype(BF16)
    gt = _dot(u, w_ref[:, 2 * QK + 2 * D_MODEL:]) + bg_ref[...]
    lane = lax.broadcasted_iota(jnp.int32, gt.shape, 1)
    pos = lax.broadcasted_iota(jnp.int32, gt.shape, 0) % CHUNK
    log_f = jnp.minimum(gt, 0.0) - jnp.log(1.0 + jnp.exp(-jnp.abs(gt)))
    b = log_f
    shift = 1
    while shift < CHUNK:
        b = b + jnp.where(pos >= shift, pltpu.roll(b, shift, axis=0), 0.0)
        shift *= 2
    imb = gt - pltpu.roll(b, LANES - N_HEADS, axis=1)
    cm = imb
    shift = 1
    while shift < CHUNK:
        cm = jnp.maximum(cm, jnp.where(pos >= shift, pltpu.roll(cm, shift, axis=0), -jnp.inf))
        shift *= 2
    cols = jnp.where(lane < N_HEADS, cm,
                     jnp.where(lane < 2 * N_HEADS, b,
                               jnp.where(lane < 3 * N_HEADS, pltpu.roll(imb, 2 * N_HEADS, axis=1), 0.0)))
    gcol_ref[...] = cols[:, 0:GATE_WIDTH]
    grow_ref[0] = imb.T[0:SUBLANES, :]


def _mlstm_proj(x, mod, g, w, bg):
    t, d = x.shape
    n = w.shape[1]
    tm = TM_PROJ
    tiles_per_batch = SEQ // tm
    return pl.pallas_call(
        _mlstm_proj_kernel,
        out_shape=(
            jax.ShapeDtypeStruct((t, QK), BF16),
            jax.ShapeDtypeStruct((t, QK), BF16),
            jax.ShapeDtypeStruct((t, D_MODEL), BF16),
            jax.ShapeDtypeStruct((t, D_MODEL), BF16),
            jax.ShapeDtypeStruct((t, GATE_WIDTH), F32),
            jax.ShapeDtypeStruct((BATCH, SUBLANES, SEQ), F32),
        ),
        grid=(t // tm,),
        in_specs=[
            pl.BlockSpec((tm, d), lambda i, *_: (i, 0)),
            pl.BlockSpec((1, 6, d), lambda i, *_: (i // tiles_per_batch, 0, 0)),
            pl.BlockSpec((1, d), lambda i, *_: (0, 0)),
            pl.BlockSpec((d, n), lambda i, *_: (0, 0)),
            pl.BlockSpec((1, GATE_COLS), lambda i, *_: (0, 0)),
        ],
        out_specs=(
            pl.BlockSpec((tm, QK), lambda i, *_: (i, 0)),
            pl.BlockSpec((tm, QK), lambda i, *_: (i, 0)),
            pl.BlockSpec((tm, D_MODEL), lambda i, *_: (i, 0)),
            pl.BlockSpec((tm, D_MODEL), lambda i, *_: (i, 0)),
            pl.BlockSpec((tm, GATE_WIDTH), lambda i, *_: (i, 0)),
            pl.BlockSpec((1, SUBLANES, tm),
                         lambda i, *_: (i // tiles_per_batch, 0, i % tiles_per_batch)),
        ),
        compiler_params=_params("arbitrary"),
        name="mlstm_proj",
    )(x, mod, g, w, bg)


def _mlstm_layer_kernel(q_ref, k_ref, v_ref, og_ref, gcol_ref, grow_ref, ng_ref, rep_ref,
                        x_ref, mod_ref, wo_ref, g2_ref, wr_ref, br_ref, tri_ref,
                        x1_ref, u2_ref, cls_ref, rank_ref, cnt_ref,
                        m_ref, h_ref, run_ref, *c_refs):
    step = pl.program_id(0)

    @pl.when(step == 0)
    def _():
        m_ref[...] = jnp.zeros_like(m_ref)
        h_ref[...] = jnp.zeros_like(h_ref)
        run_ref[...] = jnp.zeros_like(run_ref)
        for c_ref in c_refs:
            c_ref[...] = jnp.zeros_like(c_ref)

    h_prev = h_ref[...]
    ln = CHUNK
    row = lax.broadcasted_iota(jnp.int32, (ln, ln), 0)
    col = lax.broadcasted_iota(jnp.int32, (ln, ln), 1)
    causal = col <= row
    ones = jnp.ones((ln, LANES), BF16)
    n_streams = BATCH * N_HEADS
    m_all = [m_ref[st] for st in range(n_streams)]
    m_out = [None] * n_streams

    reps = {}

    def replicated_gates(bi):
        if bi not in reps:
            g = gcol_ref[bi]
            g_hi = g.astype(BF16)
            g_r1 = g - g_hi.astype(F32)
            g_mid = g_r1.astype(BF16)
            g_lo = (g_r1 - g_mid.astype(F32)).astype(BF16)
            reps[bi] = _dot(g_hi, rep_ref[...]) + _dot(g_mid, rep_ref[...]) + _dot(g_lo, rep_ref[...])
        return reps[bi]

    def stream(st):
        bi, h = divmod(st, N_HEADS)
        rep = replicated_gates(bi)
        lanes = lambda kind: slice((kind * N_HEADS + h) * LANES, (kind * N_HEADS + h + 1) * LANES)
        cm, b, imb_col = rep[:, lanes(0)], rep[:, lanes(1)], rep[:, lanes(2)]
        imb_row = grow_ref[bi, h:h + 1, :]
        qh = q_ref[bi, :, h * DH_QK:(h + 1) * DH_QK]
        kh = k_ref[bi, :, h * DH_QK:(h + 1) * DH_QK]
        v_ext = jnp.concatenate([v_ref[bi, :, h * DH_V:(h + 1) * DH_V], ones], axis=1)
        c_ref = c_refs[st]
        c_prev = c_ref[...]
        m_prev = m_all[st]
        big_m = jnp.maximum(m_prev, cm)
        d_mat = jnp.exp(jnp.where(causal, imb_row - big_m, -jnp.inf))
        s_raw = lax.dot_general(qh, kh, (((1,), (1,)), ((), ())), preferred_element_type=F32)
        yield
        q_inter = (qh.astype(F32) * jnp.exp(m_prev - big_m)).astype(BF16)
        lhs = jnp.concatenate([q_inter, (s_raw * d_mat).astype(BF16)], axis=1)
        rhs = jnp.concatenate([c_prev.astype(BF16), v_ext], axis=0)
        nd = _dot(lhs, rhs)
        m_last = big_m[ln - 1:ln, :]
        kw = (kh.astype(F32) * jnp.exp(imb_col - m_last)).astype(BF16)
        update = lax.dot_general(kw, v_ext, (((0,), (0,)), ((), ())), preferred_element_type=F32)
        yield
        decay = jnp.exp(m_prev - m_last)
        c_ref[...] = jnp.concatenate([decay] * 3, axis=1) * c_prev + update
        m_out[st] = b[ln - 1:ln, :] + m_last
        den = nd[:, DH_V:]
        inv = 1.0 / jnp.maximum(jnp.abs(den), jnp.exp(-(b + big_m)))
        hh = nd[:, :DH_V] * jnp.concatenate([inv, inv], axis=1)
        sl = slice(h * DH_V, (h + 1) * DH_V)
        gate = _sigmoid(og_ref[bi, :, sl].astype(F32))
        h_ref[bi * ln:(bi + 1) * ln, sl] = (_rms(hh) * ng_ref[:, sl] * gate).astype(BF16)

    def previous_chunk():
        half = D_MODEL // 2
        mix0 = _dot(h_prev, wo_ref[:, 0:half])
        yield
        mix1 = _dot(h_prev, wo_ref[:, half:])
        yield
        mod = mod_ref[...]
        u2_rows = []
        for bi in range(BATCH):
            mb = mod[bi]
            rows = slice(bi * ln, (bi + 1) * ln)
            x1 = x_ref[bi] + mb[2:3] * jnp.concatenate([mix0[rows], mix1[rows]], axis=1)
            x1_ref[bi] = x1
            u2_b = _moe_input(x1, mb, g2_ref)
            _store_token_rows(u2_ref.at[bi], u2_b, ln)
            u2_rows.append(u2_b)
        valid = jnp.where(step > 0, 1.0, 0.0)
        yield from _route_stages(jnp.concatenate(u2_rows, axis=0), valid, wr_ref, br_ref, tri_ref,
                                 cls_ref, rank_ref, cnt_ref, run_ref)

    for g0 in range(0, n_streams, STREAM_GROUP):
        chains = [stream(st) for st in range(g0, g0 + STREAM_GROUP)]
        if g0 == 0:
            chains.insert(0, previous_chunk())
        _round_robin(chains)
    for st in range(n_streams):
        m_ref[st] = m_out[st]


def _mlstm_layer(q, k, v, og, gcol, grow, norm_g, x, mod, w_out, g2, wr, br):
    nc = SEQ // CHUNK
    n_streams = BATCH * N_HEADS
    rows = BATCH * CHUNK
    per_batch = lambda a: a.reshape(BATCH, SEQ, a.shape[-1])
    cur = _cur_tile(nc)
    chunk_rows = lambda width: pl.BlockSpec((BATCH, CHUNK, width), lambda c: (0, cur(c), 0))
    prev_rows = lambda height, width: pl.BlockSpec((BATCH, height, width), lambda c: (0, _prev_tile(c), 0))
    n_rep = GATE_KINDS * N_HEADS
    replicate = (jnp.arange(GATE_WIDTH)[:, None] == jnp.arange(n_rep * LANES)[None, :] // LANES
                 ).astype(BF16)
    x1, u2, cls, rank, cnt = pl.pallas_call(
        _mlstm_layer_kernel,
        out_shape=(
            jax.ShapeDtypeStruct((BATCH, SEQ, D_MODEL), F32),
            jax.ShapeDtypeStruct((BATCH, SEQ * ROW_TILE, LANES), F32),
            jax.ShapeDtypeStruct((nc, 1, rows), jnp.int32),
            jax.ShapeDtypeStruct((nc, 1, rows), jnp.int32),
            jax.ShapeDtypeStruct((CLASS_ROWS, LANES), jnp.int32),
        ),
        grid=(nc + 1,),
        in_specs=[
            chunk_rows(QK),
            chunk_rows(QK),
            chunk_rows(D_MODEL),
            chunk_rows(D_MODEL),
            chunk_rows(GATE_WIDTH),
            pl.BlockSpec((BATCH, SUBLANES, CHUNK), lambda c: (0, 0, cur(c))),
            _const_spec((1, D_MODEL)),
            _const_spec((GATE_WIDTH, n_rep * LANES)),
            prev_rows(CHUNK, D_MODEL),
            _const_spec((BATCH, 6, D_MODEL)),
            _const_spec((D_MODEL, D_MODEL)),
            _const_spec((1, D_MODEL)),
            _const_spec((D_MODEL, ROUTER_COLS)),
            _const_spec((N_EXPERTS, 1)),
            _const_spec((rows, rows)),
        ],
        out_specs=(
            prev_rows(CHUNK, D_MODEL),
            prev_rows(CHUNK * ROW_TILE, LANES),
            pl.BlockSpec((1, 1, rows), lambda c: (_prev_tile(c), 0, 0)),
            pl.BlockSpec((1, 1, rows), lambda c: (_prev_tile(c), 0, 0)),
            _const_spec((CLASS_ROWS, LANES)),
        ),
        scratch_shapes=[pltpu.VMEM((n_streams, 1, LANES), F32),
                        pltpu.VMEM((rows, D_MODEL), BF16),
                        pltpu.VMEM((CLASS_ROWS, LANES), F32)]
        + [pltpu.VMEM((DH_QK, DH_V + LANES), F32) for _ in range(n_streams)],
        compiler_params=_params("arbitrary"),
        name="mlstm_layer",
    )(per_batch(q), per_batch(k), per_batch(v), per_batch(og), per_batch(gcol), grow, norm_g,
      replicate, per_batch(x), mod, w_out, g2, wr, br, _strict_upper(rows))
    token_order = lambda a: a.reshape(nc, BATCH, CHUNK).transpose(1, 0, 2).reshape(TOKENS)
    return (x1.reshape(TOKENS, D_MODEL), u2.reshape(TOKENS * ROW_TILE, LANES), token_order(cls),
            token_order(rank), cnt)


def _top2_sum(v0, v1, v2, v3):
    hi1, lo1 = jnp.maximum(v0, v1), jnp.minimum(v0, v1)
    hi2, lo2 = jnp.maximum(v2, v3), jnp.minimum(v2, v3)
    return jnp.maximum(hi1, hi2) + jnp.maximum(jnp.minimum(hi1, hi2), jnp.maximum(lo1, lo2))


def _route_tail(x_new, m, valid, g2_ref, wr_ref, br_ref, tri_ref, u2_ref, cls_ref, rank_ref, cnt_ref,
                run_ref, after_router=None):
    u2 = _moe_input(x_new, m, g2_ref)
    _store_token_rows(u2_ref, u2, u2.shape[0])
    _route(u2, valid, wr_ref, br_ref, tri_ref, cls_ref, rank_ref, cnt_ref, run_ref, after_router)


def _moe_input(x_new, m, g2_ref):
    return _rms(x_new) * g2_ref[...] * (1.0 + m[4:5]) + m[3:4]


def _round_robin(chains):
    chains = list(chains)
    while chains:
        alive = []
        for chain in chains:
            try:
                next(chain)
                alive.append(chain)
            except StopIteration:
                pass
        chains = alive


def _route(u2, valid, wr_ref, br_ref, tri_ref, cls_ref, rank_ref, cnt_ref, run_ref, after_router=None):
    stages = _route_stages(u2, valid, wr_ref, br_ref, tri_ref, cls_ref, rank_ref, cnt_ref, run_ref)
    next(stages)
    if after_router is not None:
        after_router()
    for _ in stages:
        pass


def _route_stages(u2, valid, wr_ref, br_ref, tri_ref, cls_ref, rank_ref, cnt_ref, run_ref):
    logits = _dot(u2.astype(BF16), wr_ref[...])
    yield
    lt = logits.T[0:N_EXPERTS, :]
    e = jnp.exp(lt - jnp.max(lt, axis=0, keepdims=True))
    probs = e / jnp.sum(e, axis=0, keepdims=True)
    sel = probs + br_ref[...]
    sel_rows = [sel[j:j + 1, :] for j in range(N_EXPERTS)]
    best = jnp.zeros_like(sel_rows[0], dtype=jnp.int32)
    best_score = _top2_sum(*sel_rows[0:EXPERTS_PER_GROUP])
    for g in range(1, N_GROUPS):
        score = _top2_sum(*sel_rows[g * EXPERTS_PER_GROUP:(g + 1) * EXPERTS_PER_GROUP])
        better = score > best_score
        best = jnp.where(better, g, best)
        best_score = jnp.where(better, score, best_score)
    s = []
    for j in range(EXPERTS_PER_GROUP):
        sj = sel_rows[j]
        for g in range(1, N_GROUPS):
            sj = jnp.where(best == g, sel_rows[g * EXPERTS_PER_GROUP + j], sj)
        s.append(sj)
    chosen = []
    for j in range(EXPERTS_PER_GROUP):
        beaten = jnp.zeros_like(best)
        for i in range(EXPERTS_PER_GROUP):
            if i == j:
                continue
            wins = (s[i] >= s[j]) if i < j else (s[i] > s[j])
            beaten = beaten + jnp.where(wins, 1, 0)
        chosen.append(beaten < 2)
    pair = jnp.full_like(best, len(PAIRS) - 1)
    for p in range(len(PAIRS) - 2, -1, -1):
        a, b = PAIRS[p]
        pair = jnp.where(jnp.logical_and(chosen[a], chosen[b]), p, pair)
    cls = best * len(PAIRS) + pair
    cls_ref[0] = cls

    class_id = lax.broadcasted_iota(jnp.int32, (CLASS_ROWS, cls.shape[1]), 0)
    onehot = class_id == cls
    before = _dot(jnp.where(onehot, 1.0, 0.0).astype(BF16), tri_ref[...])
    run = run_ref[...]
    rank = jnp.sum(jnp.where(onehot, before + run[:, 0:1], 0.0), axis=0, keepdims=True)
    rank_ref[0] = rank.astype(jnp.int32)
    run = run + valid * jnp.sum(jnp.where(onehot, 1.0, 0.0), axis=1, keepdims=True)
    run_ref[...] = run
    cnt_ref[...] = run.astype(jnp.int32)


def _strict_upper(n):
    return (jnp.arange(n)[:, None] < jnp.arange(n)[None, :]).astype(BF16)


def _route_out_shapes(t, tm):
    return (
        jax.ShapeDtypeStruct((t, D_MODEL), F32),
        jax.ShapeDtypeStruct((t * ROW_TILE, LANES), F32),
        jax.ShapeDtypeStruct((t // tm, 1, tm), jnp.int32),
        jax.ShapeDtypeStruct((t // tm, 1, tm), jnp.int32),
        jax.ShapeDtypeStruct((CLASS_ROWS, LANES), jnp.int32),
    )


def _cur_tile(n_tiles):
    return lambda i: jnp.minimum(i, n_tiles - 1)


def _prev_tile(i):
    return jnp.maximum(i - 1, 0)


def _route_out_specs(tm, n_tiles):
    cur = _cur_tile(n_tiles)
    return (
        pl.BlockSpec((tm, D_MODEL), lambda i, *_: (cur(i), 0)),
        pl.BlockSpec((tm * ROW_TILE, LANES), lambda i, *_: (_prev_tile(i), 0)),
        pl.BlockSpec((1, 1, tm), lambda i, *_: (_prev_tile(i), 0, 0)),
        pl.BlockSpec((1, 1, tm), lambda i, *_: (_prev_tile(i), 0, 0)),
        pl.BlockSpec((CLASS_ROWS, LANES), lambda i, *_: (0, 0)),
    )


def _init_deferred_tail(xprev_ref, run_ref):
    @pl.when(pl.program_id(0) == 0)
    def _():
        xprev_ref[...] = jnp.zeros_like(xprev_ref)
        run_ref[...] = jnp.zeros_like(run_ref)


def _deferred_tail(xprev_ref, modp_ref, *tail_refs, after_router=None):
    valid = jnp.where(pl.program_id(0) > 0, 1.0, 0.0)
    _route_tail(xprev_ref[...], modp_ref[0], valid, *tail_refs, after_router=after_router)


def _const_spec(shape):
    return pl.BlockSpec(shape, lambda i, *_: (0,) * len(shape))


def _conv_layer_kernel(dest_ref, x_ref, ys_ref, mod0_ref, mod_ref, modp_ref, g1_ref, wi_ref, cw_ref,
                       cb_ref, wo_ref, g2_ref, wr_ref, br_ref, tri_ref,
                       x3_ref, u2_ref, cls_ref, rank_ref, cnt_ref,
                       run_ref, carry_ref, ybuf_ref, ysem_ref, xprev_ref):
    tm = x_ref.shape[0]
    tiles_per_batch = SEQ // tm
    n_tiles = pl.num_programs(0) - 1
    i = pl.program_id(0)

    @pl.when(i % tiles_per_batch == 0)
    def _():
        carry_ref[...] = jnp.zeros_like(carry_ref)

    _init_deferred_tail(xprev_ref, run_ref)
    m = mod_ref[0]
    y = _gathered_tile(dest_ref, ys_ref, ybuf_ref, ysem_ref, tm, issue_first=False, n_tiles=n_tiles)
    x2 = x_ref[...] + mod0_ref[0][5:6] * y
    u = (_rms(x2) * g1_ref[...] * (1.0 + m[1:2]) + m[0:1]).astype(BF16)
    bgate = _dot(u, wi_ref[:, 0:D_MODEL])
    gated = []

    def project_gated():
        gated.append(_dot(u, wi_ref[:, D_MODEL:2 * D_MODEL]) * _dot(u, wi_ref[:, 2 * D_MODEL:]))

    _deferred_tail(xprev_ref, modp_ref, g2_ref, wr_ref, br_ref, tri_ref, u2_ref, cls_ref, rank_ref,
                   cnt_ref, run_ref, after_router=project_gated)
    z = gated[0]
    prev = carry_ref[...]
    row = lax.broadcasted_iota(jnp.int32, z.shape, 0)
    z1 = jnp.where(row == 0, prev[7:8], pltpu.roll(z, 1, axis=0))
    z2 = jnp.where(row == 0, prev[6:7], jnp.where(row == 1, prev[7:8], pltpu.roll(z, 2, axis=0)))
    carry_ref[...] = z[tm - SUBLANES:, :]
    cw = cw_ref[...]
    zc = cw[0:1] * z2 + cw[1:2] * z1 + cw[2:3] * z + cb_ref[...]
    x3 = x2 + m[2:3] * _dot((bgate * zc).astype(BF16), wo_ref[...])
    xprev_ref[...] = x3

    @pl.when(i < n_tiles)
    def _():
        x3_ref[...] = x3

    _drain_row_gather(ys_ref, ybuf_ref, ysem_ref)


def _conv_layer(dest, x, ys, mod0, mod, g1, w_in, conv_w, conv_b, w_out, g2, wr, br):
    t = x.shape[0]
    tm = TM_CONV
    tri = _strict_upper(tm)
    tiles_per_batch = SEQ // tm
    n_tiles = t // tm
    cur = _cur_tile(n_tiles)
    mod_spec = pl.BlockSpec((1, 6, D_MODEL), lambda i, *_: (cur(i) // tiles_per_batch, 0, 0))
    modp_spec = pl.BlockSpec((1, 6, D_MODEL), lambda i, *_: (_prev_tile(i) // tiles_per_batch, 0, 0))
    return pl.pallas_call(
        _conv_layer_kernel,
        out_shape=_route_out_shapes(t, tm),
        grid_spec=pltpu.PrefetchScalarGridSpec(
            num_scalar_prefetch=1,
            grid=(n_tiles + 1,),
            in_specs=[
                pl.BlockSpec((tm, D_MODEL), lambda i, *_: (cur(i), 0)),
                pl.BlockSpec(memory_space=pl.ANY),
                mod_spec,
                mod_spec,
                modp_spec,
                _const_spec((1, D_MODEL)),
                _const_spec((D_MODEL, 3 * D_MODEL)),
                _const_spec((3, D_MODEL)),
                _const_spec((1, D_MODEL)),
                _const_spec((D_MODEL, D_MODEL)),
                _const_spec((1, D_MODEL)),
                _const_spec((D_MODEL, ROUTER_COLS)),
                _const_spec((N_EXPERTS, 1)),
                _const_spec((tm, tm)),
            ],
            out_specs=_route_out_specs(tm, n_tiles),
            scratch_shapes=[
                pltpu.VMEM((CLASS_ROWS, LANES), F32),
                pltpu.VMEM((SUBLANES, D_MODEL), F32),
                pltpu.VMEM((2, tm * ROW_TILE, LANES), F32),
                pltpu.SemaphoreType.DMA((2,)),
                pltpu.VMEM((tm, D_MODEL), F32),
            ],
        ),
        compiler_params=_params("arbitrary"),
        name="conv_layer_route",
    )(dest, x, ys, mod0, mod, mod, g1, w_in, conv_w, conv_b, w_out, g2, wr, br, tri)


def _invert_kernel(dest_ref, fill_ref, src_ref, sem):
    fill = pltpu.make_async_copy(fill_ref, src_ref, sem)
    fill.start()
    fill.wait()
    unroll = 8

    def body(t8, carry):
        for k in range(unroll):
            t = t8 * unroll + k
            src_ref[dest_ref[t]] = t
        return carry

    lax.fori_loop(0, dest_ref.shape[0] // unroll, body, 0)


def _invert_permutation(dest, n_out):
    smem = pl.BlockSpec(memory_space=pltpu.SMEM)
    return pl.pallas_call(
        _invert_kernel,
        out_shape=jax.ShapeDtypeStruct((n_out,), jnp.int32),
        in_specs=[smem, pl.BlockSpec(memory_space=pl.ANY)],
        out_specs=smem,
        scratch_shapes=[pltpu.SemaphoreType.DMA(())],
        name="invert_permutation",
    )(dest, jnp.arange(n_out, dtype=jnp.int32) % dest.shape[0])


def _expert_kernel(ea_ref, eb_ref, chg_ref, nused_ref, src_ref, u2_ref, wga_ref, wua_ref, wda_ref,
                   wgb_ref, wub_ref, wdb_ref, wra_ref, wrb_ref, y_ref, xbuf_ref, xsem_ref, *wbf_refs):
    del ea_ref, eb_ref
    j = pl.program_id(0)
    tm = TM_EXPERT

    @pl.when(chg_ref[j] == 1)
    def _():
        for src, dst in zip((wga_ref, wua_ref, wda_ref, wgb_ref, wub_ref, wdb_ref), wbf_refs):
            dst[...] = src[0, 0].astype(BF16)

    rows, start_next = _gathered_tile(src_ref, u2_ref, xbuf_ref, xsem_ref, tm, issue_first=None)

    @pl.when(j < nused_ref[0])
    def _():
        start_next()
        xb = rows.astype(BF16)
        dl = jnp.sum(xb.astype(F32) * (wra_ref[0] - wrb_ref[0]), axis=-1, keepdims=True)
        w_a = _sigmoid(dl)
        w_b = _sigmoid(-dl)

        def ffn(wg_ref, wu_ref, wd_ref):
            gate = _dot(xb, wg_ref[...])
            hidden = gate * _sigmoid(gate) * _dot(xb, wu_ref[...])
            return _dot(hidden.astype(BF16), wd_ref[...])

        y = w_a * ffn(*wbf_refs[0:3]) + w_b * ffn(*wbf_refs[3:6])
        _store_token_rows(y_ref, y, tm)

    @pl.when(j >= nused_ref[0])
    def _():
        start_next()
        y_ref[...] = jnp.zeros_like(y_ref)

    _drain_row_gather(u2_ref, xbuf_ref, xsem_ref)


def _experts(layer, ea, eb, chg, n_used, src, u2, w_gate, w_up, w_down, wr_rows):
    tm = TM_EXPERT
    sel_a = lambda j, ea, eb, *_: (layer, ea[j], 0, 0)
    sel_b = lambda j, ea, eb, *_: (layer, eb[j], 0, 0)
    up_spec = lambda sel: pl.BlockSpec((1, 1, D_MODEL, D_EXPERT), sel)
    down_spec = lambda sel: pl.BlockSpec((1, 1, D_EXPERT, D_MODEL), sel)
    wr_spec = lambda sel: pl.BlockSpec((1, 1, D_MODEL), lambda *a: sel(*a)[1:])
    up_scratch = pltpu.VMEM((D_MODEL, D_EXPERT), BF16)
    down_scratch = pltpu.VMEM((D_EXPERT, D_MODEL), BF16)
    return pl.pallas_call(
        _expert_kernel,
        out_shape=jax.ShapeDtypeStruct((PADDED_ROWS * ROW_TILE, LANES), F32),
        grid_spec=pltpu.PrefetchScalarGridSpec(
            num_scalar_prefetch=4,
            grid=(N_EXPERT_TILES,),
            in_specs=[
                pl.BlockSpec(memory_space=pltpu.SMEM),
                pl.BlockSpec(memory_space=pl.ANY),
                up_spec(sel_a), up_spec(sel_a), down_spec(sel_a),
                up_spec(sel_b), up_spec(sel_b), down_spec(sel_b),
                wr_spec(sel_a), wr_spec(sel_b),
            ],
            out_specs=pl.BlockSpec((tm * ROW_TILE, LANES), lambda j, *_: (j, 0)),
            scratch_shapes=[pltpu.VMEM((EXPERT_GATHER_BUFFERS, tm * ROW_TILE, LANES), F32),
                            pltpu.SemaphoreType.DMA((EXPERT_GATHER_BUFFERS,)),
                            up_scratch, up_scratch, down_scratch, up_scratch, up_scratch, down_scratch],
        ),
        compiler_params=_params("arbitrary"),
        name="grouped_experts",
    )(ea, eb, chg, n_used, src, u2, w_gate, w_up, w_down, w_gate, w_up, w_down, wr_rows, wr_rows)


def _final_kernel(dest_ref, x_ref, ys_ref, mod_ref, g_ref, o_ref, ybuf_ref, ysem_ref):
    y = _gathered_tile(dest_ref, ys_ref, ybuf_ref, ysem_ref, x_ref.shape[0], issue_first=True)
    x = x_ref[...] + mod_ref[0][5:6] * y
    o_ref[...] = _rms(x) * g_ref[...]
    _drain_row_gather(ys_ref, ybuf_ref, ysem_ref)


def _final(dest, x, ys, mod, g):
    t = x.shape[0]
    tm = TM_FINAL
    tiles_per_batch = SEQ // tm
    return pl.pallas_call(
        _final_kernel,
        out_shape=jax.ShapeDtypeStruct((t, D_MODEL), F32),
        grid_spec=pltpu.PrefetchScalarGridSpec(
            num_scalar_prefetch=1,
            grid=(t // tm,),
            in_specs=[
                pl.BlockSpec((tm, D_MODEL), lambda i, *_: (i, 0)),
                pl.BlockSpec(memory_space=pl.ANY),
                pl.BlockSpec((1, 6, D_MODEL), lambda i, *_: (i // tiles_per_batch, 0, 0)),
                _const_spec((1, D_MODEL)),
            ],
            out_specs=pl.BlockSpec((tm, D_MODEL), lambda i, *_: (i, 0)),
            scratch_shapes=[pltpu.VMEM((2, tm * ROW_TILE, LANES), F32), pltpu.SemaphoreType.DMA((2,))],
        ),
        compiler_params=_params("arbitrary"),
        name="final_norm",
    )(dest, x, ys, mod, g)


_PAIR_A = np.array([EXPERTS_PER_GROUP * (c // len(PAIRS)) + PAIRS[c % len(PAIRS)][0]
                    for c in range(N_CLASSES)], np.int32)
_PAIR_B = np.array([EXPERTS_PER_GROUP * (c // len(PAIRS)) + PAIRS[c % len(PAIRS)][1]
                    for c in range(N_CLASSES)], np.int32)


def _moe(layer, u2, cls, rank, cnt, w_gate, w_up, w_down, wr_rows):
    counts = cnt[:N_CLASSES, 0]
    tiles = (counts + TM_EXPERT - 1) // TM_EXPERT
    tile_end = jnp.cumsum(tiles)
    row_start = (tile_end - tiles) * TM_EXPERT
    n_used = tile_end[-1:]
    cls = cls.reshape(-1)
    dest = row_start[cls] + rank.reshape(-1)
    tile_cls = jnp.sum(jnp.arange(N_EXPERT_TILES)[:, None] >= tile_end[None, :], axis=1)
    tile_cls = jnp.minimum(tile_cls, tile_cls[jnp.maximum(n_used[0] - 1, 0)])
    ea = jnp.asarray(_PAIR_A)[tile_cls]
    eb = jnp.asarray(_PAIR_B)[tile_cls]
    chg = jnp.concatenate([jnp.ones((1,), jnp.int32),
                           (tile_cls[1:] != tile_cls[:-1]).astype(jnp.int32)])
    src = _invert_permutation(dest, PADDED_ROWS)
    ys = _experts(layer, ea, eb, chg, n_used.astype(jnp.int32), src, u2, w_gate, w_up, w_down, wr_rows)
    return dest, ys


def kernel(x, c, norm1_g, norm2_g, w_ada, b_ada, m_w_in, m_b_gates, m_norm_g, m_w_out,
           c_w_in, c_conv_w, c_conv_b, c_w_out, w_router, b_router,
           e_w_gate, e_w_up, e_w_down, final_g):
    xf = x.reshape(TOKENS, D_MODEL)
    mod = _ada(c, w_ada, b_ada)

    w_in = jnp.concatenate(
        [m_w_in[0].astype(BF16),
         jnp.zeros((D_MODEL, GATE_COLS - 2 * N_HEADS), BF16)], axis=1)
    bg = jnp.zeros((1, GATE_COLS), F32).at[0, :2 * N_HEADS].set(m_b_gates[0])
    wr = jnp.zeros((D_MODEL, ROUTER_COLS), BF16).at[:, :N_EXPERTS].set(w_router.astype(BF16))
    wr_rows = w_router.astype(BF16).astype(F32).T.reshape(N_EXPERTS, 1, D_MODEL)
    br = b_router.reshape(N_EXPERTS, 1)
    row = lambda v: v.reshape(1, -1)

    q, k, v, og, gcol, grow = _mlstm_proj(xf, mod[0], row(norm1_g[0]), w_in, bg)
    x1, u2, cls, rank, cnt = _mlstm_layer(q, k, v, og, gcol, grow, row(m_norm_g[0]), xf, mod[0],
                                          m_w_out[0].astype(BF16), row(norm2_g[0]), wr, br)
    dest, ys = _moe(0, u2, cls, rank, cnt, e_w_gate, e_w_up, e_w_down, wr_rows)

    x3, u2, cls, rank, cnt = _conv_layer(
        dest, x1, ys, mod[0], mod[1], row(norm1_g[1]), c_w_in[0].astype(BF16), c_conv_w[0],
        row(c_conv_b[0]), c_w_out[0].astype(BF16), row(norm2_g[1]), wr, br)
    dest, ys = _moe(1, u2, cls, rank, cnt, e_w_gate, e_w_up, e_w_down, wr_rows)

    out = _final(dest, x3, ys, mod[1], row(final_g))
    return out.reshape(BATCH, SEQ, D_MODEL)
```

```python
import functools

import jax
import jax.numpy as jnp
import numpy as np
from jax import lax
from jax.experimental import pallas as pl
from jax.experimental.pallas import tpu as pltpu

F32 = jnp.float32
BF16 = jnp.bfloat16

D_MODEL = 1024
BATCH = 4
SEQ = 8192
TOKENS = BATCH * SEQ
N_HEADS = 4
DH_V = 256
DH_QK = 128
QK = N_HEADS * DH_QK
N_EXPERTS = 16
N_GROUPS = 4
EXPERTS_PER_GROUP = 4
D_EXPERT = 512
EPS = 1e-6

LANES = 128
SUBLANES = 8
VMEM_LIMIT_BYTES = 56 * 1024 * 1024

CHUNK = 128
TM_CONV = 512
TM_EXPERT = 512
GATE_COLS = LANES
GATE_WIDTH = 16
GATE_KINDS = 3
ROUTER_COLS = LANES
PAIRS = ((0, 1), (0, 2), (0, 3), (1, 2), (1, 3), (2, 3))
N_CLASSES = N_GROUPS * len(PAIRS)
CLASS_ROWS = 32
N_EXPERT_TILES = TOKENS // TM_EXPERT + N_CLASSES
PADDED_ROWS = N_EXPERT_TILES * TM_EXPERT
TM_FINAL = 512
EXPERT_GATHER_BUFFERS = 3
PROJ_COLS = 256
PROJ_EVERY = 1


def _params(*semantics):
    return pltpu.CompilerParams(dimension_semantics=semantics, vmem_limit_bytes=VMEM_LIMIT_BYTES)


def _dot(a, b):
    return jnp.dot(a, b, preferred_element_type=F32)


def _rms(x):
    return x * lax.rsqrt(jnp.mean(x * x, axis=-1, keepdims=True) + EPS)


def _sigmoid(x):
    return 1.0 / (1.0 + jnp.exp(-x))


ROW_TILE = D_MODEL // LANES


def _load_token_rows(ref, n):
    return jnp.concatenate([ref[pl.ds(c, n, stride=ROW_TILE), :] for c in range(ROW_TILE)], axis=1)


def _store_token_rows(ref, val, n):
    for c in range(ROW_TILE):
        ref[pl.ds(c, n, stride=ROW_TILE), :] = val[:, c * LANES:(c + 1) * LANES]


def _token_tile(ref, t):
    return ref.at[pl.ds(pl.multiple_of(t * ROW_TILE, ROW_TILE), ROW_TILE)]


def _start_row_gather(idx_ref, base, n, src_hbm, buf, sem):
    for r in range(n):
        pltpu.make_async_copy(_token_tile(src_hbm, idx_ref[base + r]), _token_tile(buf, r),
                              sem).start(priority=r % 2)


def _wait_row_gather(src_hbm, buf, sem):
    pltpu.make_async_copy(src_hbm.at[pl.ds(0, buf.shape[0])], buf, sem).wait()


def _gathered_tile(idx_ref, src_hbm, buf_ref, sem_ref, n, issue_first, n_tiles=None):
    i = pl.program_id(0)
    last = (pl.num_programs(0) if n_tiles is None else n_tiles) - 1
    nb = buf_ref.shape[0]
    ahead = nb - 1

    @pl.when(i == 0)
    def _():
        for k in range(ahead):
            _start_row_gather(idx_ref, jnp.minimum(k, last) * n, n, src_hbm, buf_ref.at[k], sem_ref.at[k])

    nxt = (i + ahead) % nb
    start_next = functools.partial(
        _start_row_gather, idx_ref, jnp.minimum(i + ahead, last) * n, n, src_hbm, buf_ref.at[nxt],
        sem_ref.at[nxt])
    if issue_first:
        start_next()
    slot = i % nb
    _wait_row_gather(src_hbm, buf_ref.at[slot], sem_ref.at[slot])
    rows = _load_token_rows(buf_ref.at[slot], n)
    if issue_first is None:
        return rows, start_next
    if not issue_first:
        start_next()
    return rows


def _drain_row_gather(src_hbm, buf_ref, sem_ref):
    i = pl.program_id(0)
    nb = buf_ref.shape[0]

    @pl.when(i == pl.num_programs(0) - 1)
    def _():
        for k in range(1, nb):
            _wait_row_gather(src_hbm, buf_ref.at[(i + k) % nb], sem_ref.at[(i + k) % nb])


def _ada_kernel(c_ref, w_ref, b_ref, o_ref):
    c = c_ref[...]
    cond = c * _sigmoid(c)
    o_ref[0] = jnp.dot(cond, w_ref[0], preferred_element_type=F32,
                       precision=lax.Precision.HIGHEST) + b_ref[0]


def _ada(c, w_ada, b_ada):
    depth, d, n = w_ada.shape
    tn = 1536
    c8 = jnp.zeros((SUBLANES, d), F32).at[:BATCH].set(c)
    out = pl.pallas_call(
        _ada_kernel,
        out_shape=jax.ShapeDtypeStruct((depth, SUBLANES, n), F32),
        grid=(depth, n // tn),
        in_specs=[
            pl.BlockSpec((SUBLANES, d), lambda l, j: (0, 0)),
            pl.BlockSpec((1, d, tn), lambda l, j: (l, 0, j)),
            pl.BlockSpec((1, 1, tn), lambda l, j: (l, 0, j)),
        ],
        out_specs=pl.BlockSpec((1, SUBLANES, tn), lambda l, j: (l, 0, j)),
        compiler_params=_params("arbitrary", "arbitrary"),
        name="ada_mod",
    )(c8, w_ada, b_ada.reshape(depth, 1, n))
    return out[:, :BATCH].reshape(depth, BATCH, 6, d)


def _mlstm_layer_kernel(xn_ref, x_ref, mod_ref, g1_ref, wi_ref, bg_ref, ng_ref, rep_ref,
                        wo_ref, g2_ref, wr_ref, br_ref, tri_ref,
                        x1_ref, u2_ref, cls_ref, rank_ref, cnt_ref,
                        q_ref, k_ref, v_ref, og_ref, gcol_ref, grow_ref, m_ref, h_ref, run_ref, *c_refs):
    step = pl.program_id(0)

    @pl.when(step == 0)
    def _():
        for ref in (q_ref, k_ref, v_ref, og_ref, gcol_ref, grow_ref, m_ref, h_ref, run_ref) + c_refs:
            ref[...] = jnp.zeros_like(ref)

    h_prev = h_ref[...]
    ln = CHUNK
    row = lax.broadcasted_iota(jnp.int32, (ln, ln), 0)
    col = lax.broadcasted_iota(jnp.int32, (ln, ln), 1)
    causal = col <= row
    ones = jnp.ones((ln, LANES), BF16)
    n_streams = BATCH * N_HEADS
    m_all = [m_ref[st] for st in range(n_streams)]
    m_out = [None] * n_streams

    reps = {}

    def replicated_gates(bi):
        if bi not in reps:
            g = gcol_ref[bi]
            g_hi = g.astype(BF16)
            g_r1 = g - g_hi.astype(F32)
            g_mid = g_r1.astype(BF16)
            g_lo = (g_r1 - g_mid.astype(F32)).astype(BF16)
            reps[bi] = _dot(g_hi, rep_ref[...]) + _dot(g_mid, rep_ref[...]) + _dot(g_lo, rep_ref[...])
        return reps[bi]

    def stream(st):
        bi, h = divmod(st, N_HEADS)
        rep = replicated_gates(bi)
        lanes = lambda kind: slice((kind * N_HEADS + h) * LANES, (kind * N_HEADS + h + 1) * LANES)
        cm, b, imb_col = rep[:, lanes(0)], rep[:, lanes(1)], rep[:, lanes(2)]
        imb_row = grow_ref[bi, h:h + 1, :]
        qh = q_ref[bi, :, h * DH_QK:(h + 1) * DH_QK]
        kh = k_ref[bi, :, h * DH_QK:(h + 1) * DH_QK]
        v_ext = jnp.concatenate([v_ref[bi, :, h * DH_V:(h + 1) * DH_V], ones], axis=1)
        c_ref = c_refs[st]
        c_prev = c_ref[...]
        m_prev = m_all[st]
        big_m = jnp.maximum(m_prev, cm)
        d_mat = jnp.exp(jnp.where(causal, imb_row - big_m, -jnp.inf))
        s_raw = lax.dot_general(qh, kh, (((1,), (1,)), ((), ())), preferred_element_type=F32)
        yield
        q_inter = (qh.astype(F32) * jnp.exp(m_prev - big_m)).astype(BF16)
        lhs = jnp.concatenate([q_inter, (s_raw * d_mat).astype(BF16)], axis=1)
        rhs = jnp.concatenate([c_prev.astype(BF16), v_ext], axis=0)
        nd = _dot(lhs, rhs)
        m_last = big_m[ln - 1:ln, :]
        kw = (kh.astype(F32) * jnp.exp(imb_col - m_last)).astype(BF16)
        update = lax.dot_general(kw, v_ext, (((0,), (0,)), ((), ())), preferred_element_type=F32)
        yield
        decay = jnp.exp(m_prev - m_last)
        c_ref[...] = jnp.concatenate([decay] * 3, axis=1) * c_prev + update
        m_out[st] = b[ln - 1:ln, :] + m_last
        den = nd[:, DH_V:]
        inv = 1.0 / jnp.maximum(jnp.abs(den), jnp.exp(-(b + big_m)))
        hh = nd[:, :DH_V] * jnp.concatenate([inv, inv], axis=1)
        sl = slice(h * DH_V, (h + 1) * DH_V)
        gate = _sigmoid(og_ref[bi, :, sl].astype(F32))
        h_ref[bi * ln:(bi + 1) * ln, sl] = (_rms(hh) * ng_ref[:, sl] * gate).astype(BF16)

    deferred_stores = []

    def projection():
        mod = mod_ref[...]
        u = jnp.concatenate(
            [_rms(xn_ref[bi]) * g1_ref[...] * (1.0 + mod[bi][1:2]) + mod[bi][0:1] for bi in range(BATCH)],
            axis=0).astype(BF16)
        gt = _dot(u, wi_ref[:, 2 * QK + 2 * D_MODEL:]) + bg_ref[...]
        cols, imb = _gate_columns(gt)
        imb_rows = imb.T[0:SUBLANES, :]
        for bi in range(BATCH):
            gcol_ref[bi] = cols[bi * ln:(bi + 1) * ln, 0:GATE_WIDTH]
            grow_ref[bi] = imb_rows[:, bi * ln:(bi + 1) * ln]
        yield
        lo = 0
        for ref, width, scale in ((q_ref, QK, DH_QK ** -0.5), (k_ref, QK, None), (v_ref, D_MODEL, None),
                                  (og_ref, D_MODEL, None)):
            for off in range(0, width, PROJ_COLS):
                part = _dot(u, wi_ref[:, lo + off:lo + off + PROJ_COLS])
                if scale is not None:
                    part = part * scale
                part = part.astype(BF16)

                def store(ref=ref, off=off, part=part):
                    for bi in range(BATCH):
                        ref[bi, :, off:off + PROJ_COLS] = part[bi * ln:(bi + 1) * ln]

                if ref is og_ref:
                    deferred_stores.append(store)
                else:
                    store()
                yield
            lo += width

    def previous_chunk():
        half = D_MODEL // 2
        mix0 = _dot(h_prev, wo_ref[:, 0:half])
        yield
        mix1 = _dot(h_prev, wo_ref[:, half:])
        yield
        mod = mod_ref[...]
        u2_rows = []
        for bi in range(BATCH):
            mb = mod[bi]
            rows = slice(bi * ln, (bi + 1) * ln)
            x1 = x_ref[bi] + mb[2:3] * jnp.concatenate([mix0[rows], mix1[rows]], axis=1)
            x1_ref[bi] = x1
            u2_b = _moe_input(x1, mb, g2_ref)
            _store_token_rows(u2_ref.at[bi], u2_b, ln)
            u2_rows.append(u2_b)
        valid = jnp.where(step > 1, 1.0, 0.0)
        yield from _route_stages(jnp.concatenate(u2_rows, axis=0), valid, wr_ref, br_ref, tri_ref,
                                 cls_ref, rank_ref, cnt_ref, run_ref)

    _round_robin([previous_chunk()] + [stream(st) for st in range(n_streams)],
                 background=projection(), every=PROJ_EVERY, start_after=n_streams + 1)
    for store in deferred_stores:
        store()
    for st in range(n_streams):
        m_ref[st] = m_out[st]


def _gate_columns(gt):
    lane = lax.broadcasted_iota(jnp.int32, gt.shape, 1)
    pos = lax.broadcasted_iota(jnp.int32, gt.shape, 0) % CHUNK
    b = jnp.minimum(gt, 0.0) - jnp.log(1.0 + jnp.exp(-jnp.abs(gt)))
    shift = 1
    while shift < CHUNK:
        b = b + jnp.where(pos >= shift, pltpu.roll(b, shift, axis=0), 0.0)
        shift *= 2
    imb = gt - pltpu.roll(b, LANES - N_HEADS, axis=1)
    cm = imb
    shift = 1
    while shift < CHUNK:
        cm = jnp.maximum(cm, jnp.where(pos >= shift, pltpu.roll(cm, shift, axis=0), -jnp.inf))
        shift *= 2
    cols = jnp.where(lane < N_HEADS, cm,
                     jnp.where(lane < 2 * N_HEADS, b,
                               jnp.where(lane < 3 * N_HEADS, pltpu.roll(imb, 2 * N_HEADS, axis=1), 0.0)))
    return cols, imb


def _mlstm_layer(x, mod, g1, w_in, bg, norm_g, w_out, g2, wr, br):
    nc = SEQ // CHUNK
    n_streams = BATCH * N_HEADS
    rows = BATCH * CHUNK
    n_rep = GATE_KINDS * N_HEADS
    replicate = (jnp.arange(GATE_WIDTH)[:, None] == jnp.arange(n_rep * LANES)[None, :] // LANES
                 ).astype(BF16)
    newest = lambda c: jnp.minimum(c, nc - 1)
    oldest = lambda c: jnp.maximum(c - 2, 0)
    old_rows = lambda height, width: pl.BlockSpec((BATCH, height, width), lambda c: (0, oldest(c), 0))
    x3d = x.reshape(BATCH, SEQ, D_MODEL)
    x1, u2, cls, rank, cnt = pl.pallas_call(
        _mlstm_layer_kernel,
        out_shape=(
            jax.ShapeDtypeStruct((BATCH, SEQ, D_MODEL), F32),
            jax.ShapeDtypeStruct((BATCH, SEQ * ROW_TILE, LANES), F32),
            jax.ShapeDtypeStruct((nc, 1, rows), jnp.int32),
            jax.ShapeDtypeStruct((nc, 1, rows), jnp.int32),
            jax.ShapeDtypeStruct((CLASS_ROWS, LANES), jnp.int32),
        ),
        grid=(nc + 2,),
        in_specs=[
            pl.BlockSpec((BATCH, CHUNK, D_MODEL), lambda c: (0, newest(c), 0)),
            old_rows(CHUNK, D_MODEL),
            _const_spec((BATCH, 6, D_MODEL)),
            _const_spec((1, D_MODEL)),
            _const_spec(w_in.shape),
            _const_spec((1, GATE_COLS)),
            _const_spec((1, D_MODEL)),
            _const_spec((GATE_WIDTH, n_rep * LANES)),
            _const_spec((D_MODEL, D_MODEL)),
            _const_spec((1, D_MODEL)),
            _const_spec((D_MODEL, ROUTER_COLS)),
            _const_spec((N_EXPERTS, 1)),
            _const_spec((rows, rows)),
        ],
        out_specs=(
            old_rows(CHUNK, D_MODEL),
            old_rows(CHUNK * ROW_TILE, LANES),
            pl.BlockSpec((1, 1, rows), lambda c: (oldest(c), 0, 0)),
            pl.BlockSpec((1, 1, rows), lambda c: (oldest(c), 0, 0)),
            _const_spec((CLASS_ROWS, LANES)),
        ),
        scratch_shapes=[pltpu.VMEM((BATCH, CHUNK, QK), BF16),
                        pltpu.VMEM((BATCH, CHUNK, QK), BF16),
                        pltpu.VMEM((BATCH, CHUNK, D_MODEL), BF16),
                        pltpu.VMEM((BATCH, CHUNK, D_MODEL), BF16),
                        pltpu.VMEM((BATCH, CHUNK, GATE_WIDTH), F32),
                        pltpu.VMEM((BATCH, SUBLANES, CHUNK), F32),
                        pltpu.VMEM((n_streams, 1, LANES), F32),
                        pltpu.VMEM((rows, D_MODEL), BF16),
                        pltpu.VMEM((CLASS_ROWS, LANES), F32)]
        + [pltpu.VMEM((DH_QK, DH_V + LANES), F32) for _ in range(n_streams)],
        compiler_params=_params("arbitrary"),
        name="mlstm_layer",
    )(x3d, x3d, mod, g1, w_in, bg, norm_g, replicate, w_out, g2, wr, br, _strict_upper(rows))
    token_order = lambda a: a.reshape(nc, BATCH, CHUNK).transpose(1, 0, 2).reshape(TOKENS)
    return (x1.reshape(TOKENS, D_MODEL), u2.reshape(TOKENS * ROW_TILE, LANES), token_order(cls),
            token_order(rank), cnt)


def _top2_sum(v0, v1, v2, v3):
    hi1, lo1 = jnp.maximum(v0, v1), jnp.minimum(v0, v1)
    hi2, lo2 = jnp.maximum(v2, v3), jnp.minimum(v2, v3)
    return jnp.maximum(hi1, hi2) + jnp.maximum(jnp.minimum(hi1, hi2), jnp.maximum(lo1, lo2))


def _route_tail(x_new, m, valid, g2_ref, wr_ref, br_ref, tri_ref, u2_ref, cls_ref, rank_ref, cnt_ref,
                run_ref, after_router=None):
    u2 = _moe_input(x_new, m, g2_ref)
    _store_token_rows(u2_ref, u2, u2.shape[0])
    _route(u2, valid, wr_ref, br_ref, tri_ref, cls_ref, rank_ref, cnt_ref, run_ref, after_router)


def _moe_input(x_new, m, g2_ref):
    return _rms(x_new) * g2_ref[...] * (1.0 + m[4:5]) + m[3:4]


def _round_robin(chains, background=None, every=1, start_after=0):
    chains = list(chains)
    advanced = 0
    while chains:
        alive = []
        for chain in chains:
            try:
                next(chain)
            except StopIteration:
                continue
            alive.append(chain)
            advanced += 1
            if background is not None and advanced >= start_after and advanced % every == 0:
                next(background, None)
        chains = alive
    if background is not None:
        for _ in background:
            pass


def _route(u2, valid, wr_ref, br_ref, tri_ref, cls_ref, rank_ref, cnt_ref, run_ref, after_router=None):
    stages = _route_stages(u2, valid, wr_ref, br_ref, tri_ref, cls_ref, rank_ref, cnt_ref, run_ref)
    next(stages)
    if after_router is not None:
        after_router()
    for _ in stages:
        pass


def _route_stages(u2, valid, wr_ref, br_ref, tri_ref, cls_ref, rank_ref, cnt_ref, run_ref):
    logits = _dot(u2.astype(BF16), wr_ref[...])
    yield
    lt = logits.T[0:N_EXPERTS, :]
    e = jnp.exp(lt - jnp.max(lt, axis=0, keepdims=True))
    probs = e / jnp.sum(e, axis=0, keepdims=True)
    sel = probs + br_ref[...]
    sel_rows = [sel[j:j + 1, :] for j in range(N_EXPERTS)]
    best = jnp.zeros_like(sel_rows[0], dtype=jnp.int32)
    best_score = _top2_sum(*sel_rows[0:EXPERTS_PER_GROUP])
    for g in range(1, N_GROUPS):
        score = _top2_sum(*sel_rows[g * EXPERTS_PER_GROUP:(g + 1) * EXPERTS_PER_GROUP])
        better = score > best_score
        best = jnp.where(better, g, best)
        best_score = jnp.where(better, score, best_score)
    s = []
    for j in range(EXPERTS_PER_GROUP):
        sj = sel_rows[j]
        for g in range(1, N_GROUPS):
            sj = jnp.where(best == g, sel_rows[g * EXPERTS_PER_GROUP + j], sj)
        s.append(sj)
    chosen = []
    for j in range(EXPERTS_PER_GROUP):
        beaten = jnp.zeros_like(best)
        for i in range(EXPERTS_PER_GROUP):
            if i == j:
                continue
            wins = (s[i] >= s[j]) if i < j else (s[i] > s[j])
            beaten = beaten + jnp.where(wins, 1, 0)
        chosen.append(beaten < 2)
    pair = jnp.full_like(best, len(PAIRS) - 1)
    for p in range(len(PAIRS) - 2, -1, -1):
        a, b = PAIRS[p]
        pair = jnp.where(jnp.logical_and(chosen[a], chosen[b]), p, pair)
    cls = best * len(PAIRS) + pair
    cls_ref[0] = cls

    class_id = lax.broadcasted_iota(jnp.int32, (CLASS_ROWS, cls.shape[1]), 0)
    onehot = class_id == cls
    before = _dot(jnp.where(onehot, 1.0, 0.0).astype(BF16), tri_ref[...])
    run = run_ref[...]
    rank = jnp.sum(jnp.where(onehot, before + run[:, 0:1], 0.0), axis=0, keepdims=True)
    rank_ref[0] = rank.astype(jnp.int32)
    run = run + valid * jnp.sum(jnp.where(onehot, 1.0, 0.0), axis=1, keepdims=True)
    run_ref[...] = run
    cnt_ref[...] = run.astype(jnp.int32)


def _strict_upper(n):
    return (jnp.arange(n)[:, None] < jnp.arange(n)[None, :]).astype(BF16)


def _route_out_shapes(t, tm):
    return (
        jax.ShapeDtypeStruct((t, D_MODEL), F32),
        jax.ShapeDtypeStruct((t * ROW_TILE, LANES), F32),
        jax.ShapeDtypeStruct((t // tm, 1, tm), jnp.int32),
        jax.ShapeDtypeStruct((t // tm, 1, tm), jnp.int32),
        jax.ShapeDtypeStruct((CLASS_ROWS, LANES), jnp.int32),
    )


def _cur_tile(n_tiles):
    return lambda i: jnp.minimum(i, n_tiles - 1)


def _prev_tile(i):
    return jnp.maximum(i - 1, 0)


def _route_out_specs(tm, n_tiles):
    cur = _cur_tile(n_tiles)
    return (
        pl.BlockSpec((tm, D_MODEL), lambda i, *_: (cur(i), 0)),
        pl.BlockSpec((tm * ROW_TILE, LANES), lambda i, *_: (_prev_tile(i), 0)),
        pl.BlockSpec((1, 1, tm), lambda i, *_: (_prev_tile(i), 0, 0)),
        pl.BlockSpec((1, 1, tm), lambda i, *_: (_prev_tile(i), 0, 0)),
        pl.BlockSpec((CLASS_ROWS, LANES), lambda i, *_: (0, 0)),
    )


def _init_deferred_tail(xprev_ref, run_ref):
    @pl.when(pl.program_id(0) == 0)
    def _():
        xprev_ref[...] = jnp.zeros_like(xprev_ref)
        run_ref[...] = jnp.zeros_like(run_ref)


def _deferred_tail(xprev_ref, modp_ref, *tail_refs, after_router=None):
    valid = jnp.where(pl.program_id(0) > 0, 1.0, 0.0)
    _route_tail(xprev_ref[...], modp_ref[0], valid, *tail_refs, after_router=after_router)


def _const_spec(shape):
    return pl.BlockSpec(shape, lambda i, *_: (0,) * len(shape))


def _conv_layer_kernel(dest_ref, x_ref, ys_ref, mod0_ref, mod_ref, modp_ref, g1_ref, wi_ref, cw_ref,
                       cb_ref, wo_ref, g2_ref, wr_ref, br_ref, tri_ref,
                       x3_ref, u2_ref, cls_ref, rank_ref, cnt_ref,
                       run_ref, carry_ref, ybuf_ref, ysem_ref, xprev_ref):
    tm = x_ref.shape[0]
    tiles_per_batch = SEQ // tm
    n_tiles = pl.num_programs(0) - 1
    i = pl.program_id(0)

    @pl.when(i % tiles_per_batch == 0)
    def _():
        carry_ref[...] = jnp.zeros_like(carry_ref)

    _init_deferred_tail(xprev_ref, run_ref)
    m = mod_ref[0]
    y = _gathered_tile(dest_ref, ys_ref, ybuf_ref, ysem_ref, tm, issue_first=False, n_tiles=n_tiles)
    x2 = x_ref[...] + mod0_ref[0][5:6] * y
    u = (_rms(x2) * g1_ref[...] * (1.0 + m[1:2]) + m[0:1]).astype(BF16)
    bgate = _dot(u, wi_ref[:, 0:D_MODEL])
    gated = []

    def project_gated():
        gated.append(_dot(u, wi_ref[:, D_MODEL:2 * D_MODEL]) * _dot(u, wi_ref[:, 2 * D_MODEL:]))

    _deferred_tail(xprev_ref, modp_ref, g2_ref, wr_ref, br_ref, tri_ref, u2_ref, cls_ref, rank_ref,
                   cnt_ref, run_ref, after_router=project_gated)
    z = gated[0]
    prev = carry_ref[...]
    row = lax.broadcasted_iota(jnp.int32, z.shape, 0)
    z1 = jnp.where(row == 0, prev[7:8], pltpu.roll(z, 1, axis=0))
    z2 = jnp.where(row == 0, prev[6:7], jnp.where(row == 1, prev[7:8], pltpu.roll(z, 2, axis=0)))
    carry_ref[...] = z[tm - SUBLANES:, :]
    cw = cw_ref[...]
    zc = cw[0:1] * z2 + cw[1:2] * z1 + cw[2:3] * z + cb_ref[...]
    x3 = x2 + m[2:3] * _dot((bgate * zc).astype(BF16), wo_ref[...])
    xprev_ref[...] = x3

    @pl.when(i < n_tiles)
    def _():
        x3_ref[...] = x3

    _drain_row_gather(ys_ref, ybuf_ref, ysem_ref)


def _conv_layer(dest, x, ys, mod0, mod, g1, w_in, conv_w, conv_b, w_out, g2, wr, br):
    t = x.shape[0]
    tm = TM_CONV
    tri = _strict_upper(tm)
    tiles_per_batch = SEQ // tm
    n_tiles = t // tm
    cur = _cur_tile(n_tiles)
    mod_spec = pl.BlockSpec((1, 6, D_MODEL), lambda i, *_: (cur(i) // tiles_per_batch, 0, 0))
    modp_spec = pl.BlockSpec((1, 6, D_MODEL), lambda i, *_: (_prev_tile(i) // tiles_per_batch, 0, 0))
    return pl.pallas_call(
        _conv_layer_kernel,
        out_shape=_route_out_shapes(t, tm),
        grid_spec=pltpu.PrefetchScalarGridSpec(
            num_scalar_prefetch=1,
            grid=(n_tiles + 1,),
            in_specs=[
                pl.BlockSpec((tm, D_MODEL), lambda i, *_: (cur(i), 0)),
                pl.BlockSpec(memory_space=pl.ANY),
                mod_spec,
                mod_spec,
                modp_spec,
                _const_spec((1, D_MODEL)),
                _const_spec((D_MODEL, 3 * D_MODEL)),
                _const_spec((3, D_MODEL)),
                _const_spec((1, D_MODEL)),
                _const_spec((D_MODEL, D_MODEL)),
                _const_spec((1, D_MODEL)),
                _const_spec((D_MODEL, ROUTER_COLS)),
                _const_spec((N_EXPERTS, 1)),
                _const_spec((tm, tm)),
            ],
            out_specs=_route_out_specs(tm, n_tiles),
            scratch_shapes=[
                pltpu.VMEM((CLASS_ROWS, LANES), F32),
                pltpu.VMEM((SUBLANES, D_MODEL), F32),
                pltpu.VMEM((2, tm * ROW_TILE, LANES), F32),
                pltpu.SemaphoreType.DMA((2,)),
                pltpu.VMEM((tm, D_MODEL), F32),
            ],
        ),
        compiler_params=_params("arbitrary"),
        name="conv_layer_route",
    )(dest, x, ys, mod0, mod, mod, g1, w_in, conv_w, conv_b, w_out, g2, wr, br, tri)


def _invert_kernel(dest_ref, fill_ref, src_ref, sem):
    fill = pltpu.make_async_copy(fill_ref, src_ref, sem)
    fill.start()
    fill.wait()
    unroll = 8

    def body(t8, carry):
        for k in range(unroll):
            t = t8 * unroll + k
            src_ref[dest_ref[t]] = t
        return carry

    lax.fori_loop(0, dest_ref.shape[0] // unroll, body, 0)


def _invert_permutation(dest, n_out):
    smem = pl.BlockSpec(memory_space=pltpu.SMEM)
    return pl.pallas_call(
        _invert_kernel,
        out_shape=jax.ShapeDtypeStruct((n_out,), jnp.int32),
        in_specs=[smem, pl.BlockSpec(memory_space=pl.ANY)],
        out_specs=smem,
        scratch_shapes=[pltpu.SemaphoreType.DMA(())],
        name="invert_permutation",
    )(dest, jnp.arange(n_out, dtype=jnp.int32) % dest.shape[0])


def _expert_kernel(ea_ref, eb_ref, chg_ref, nused_ref, src_ref, u2_ref, wga_ref, wua_ref, wda_ref,
                   wgb_ref, wub_ref, wdb_ref, wra_ref, wrb_ref, y_ref, xbuf_ref, xsem_ref, *wbf_refs):
    del ea_ref, eb_ref
    j = pl.program_id(0)
    tm = TM_EXPERT

    @pl.when(chg_ref[j] == 1)
    def _():
        for src, dst in zip((wga_ref, wua_ref, wda_ref, wgb_ref, wub_ref, wdb_ref), wbf_refs):
            dst[...] = src[0, 0].astype(BF16)

    rows, start_next = _gathered_tile(src_ref, u2_ref, xbuf_ref, xsem_ref, tm, issue_first=None)

    @pl.when(j < nused_ref[0])
    def _():
        start_next()
        xb = rows.astype(BF16)
        dl = jnp.sum(xb.astype(F32) * (wra_ref[0] - wrb_ref[0]), axis=-1, keepdims=True)
        w_a = _sigmoid(dl)
        w_b = _sigmoid(-dl)

        def ffn(wg_ref, wu_ref, wd_ref):
            gate = _dot(xb, wg_ref[...])
            hidden = gate * _sigmoid(gate) * _dot(xb, wu_ref[...])
            return _dot(hidden.astype(BF16), wd_ref[...])

        y = w_a * ffn(*wbf_refs[0:3]) + w_b * ffn(*wbf_refs[3:6])
        _store_token_rows(y_ref, y, tm)

    @pl.when(j >= nused_ref[0])
    def _():
        start_next()
        y_ref[...] = jnp.zeros_like(y_ref)

    _drain_row_gather(u2_ref, xbuf_ref, xsem_ref)


def _experts(layer, ea, eb, chg, n_used, src, u2, w_gate, w_up, w_down, wr_rows):
    tm = TM_EXPERT
    sel_a = lambda j, ea, eb, *_: (layer, ea[j], 0, 0)
    sel_b = lambda j, ea, eb, *_: (layer, eb[j], 0, 0)
    up_spec = lambda sel: pl.BlockSpec((1, 1, D_MODEL, D_EXPERT), sel)
    down_spec = lambda sel: pl.BlockSpec((1, 1, D_EXPERT, D_MODEL), sel)
    wr_spec = lambda sel: pl.BlockSpec((1, 1, D_MODEL), lambda *a: sel(*a)[1:])
    up_scratch = pltpu.VMEM((D_MODEL, D_EXPERT), BF16)
    down_scratch = pltpu.VMEM((D_EXPERT, D_MODEL), BF16)
    return pl.pallas_call(
        _expert_kernel,
        out_shape=jax.ShapeDtypeStruct((PADDED_ROWS * ROW_TILE, LANES), F32),
        grid_spec=pltpu.PrefetchScalarGridSpec(
            num_scalar_prefetch=4,
            grid=(N_EXPERT_TILES,),
            in_specs=[
                pl.BlockSpec(memory_space=pltpu.SMEM),
                pl.BlockSpec(memory_space=pl.ANY),
                up_spec(sel_a), up_spec(sel_a), down_spec(sel_a),
                up_spec(sel_b), up_spec(sel_b), down_spec(sel_b),
                wr_spec(sel_a), wr_spec(sel_b),
            ],
            out_specs=pl.BlockSpec((tm * ROW_TILE, LANES), lambda j, *_: (j, 0)),
            scratch_shapes=[pltpu.VMEM((EXPERT_GATHER_BUFFERS, tm * ROW_TILE, LANES), F32),
                            pltpu.SemaphoreType.DMA((EXPERT_GATHER_BUFFERS,)),
                            up_scratch, up_scratch, down_scratch, up_scratch, up_scratch, down_scratch],
        ),
        compiler_params=_params("arbitrary"),
        name="grouped_experts",
    )(ea, eb, chg, n_used, src, u2, w_gate, w_up, w_down, w_gate, w_up, w_down, wr_rows, wr_rows)


def _final_kernel(dest_ref, x_ref, ys_ref, mod_ref, g_ref, o_ref, ybuf_ref, ysem_ref):
    y = _gathered_tile(dest_ref, ys_ref, ybuf_ref, ysem_ref, x_ref.shape[0], issue_first=True)
    x = x_ref[...] + mod_ref[0][5:6] * y
    o_ref[...] = _rms(x) * g_ref[...]
    _drain_row_gather(ys_ref, ybuf_ref, ysem_ref)


def _final(dest, x, ys, mod, g):
    t = x.shape[0]
    tm = TM_FINAL
    tiles_per_batch = SEQ // tm
    return pl.pallas_call(
        _final_kernel,
        out_shape=jax.ShapeDtypeStruct((t, D_MODEL), F32),
        grid_spec=pltpu.PrefetchScalarGridSpec(
            num_scalar_prefetch=1,
            grid=(t // tm,),
            in_specs=[
                pl.BlockSpec((tm, D_MODEL), lambda i, *_: (i, 0)),
                pl.BlockSpec(memory_space=pl.ANY),
                pl.BlockSpec((1, 6, D_MODEL), lambda i, *_: (i // tiles_per_batch, 0, 0)),
                _const_spec((1, D_MODEL)),
            ],
            out_specs=pl.BlockSpec((tm, D_MODEL), lambda i, *_: (i, 0)),
            scratch_shapes=[pltpu.VMEM((2, tm * ROW_TILE, LANES), F32), pltpu.SemaphoreType.DMA((2,))],
        ),
        compiler_params=_params("arbitrary"),
        name="final_norm",
    )(dest, x, ys, mod, g)


_PAIR_A = np.array([EXPERTS_PER_GROUP * (c // len(PAIRS)) + PAIRS[c % len(PAIRS)][0]
                    for c in range(N_CLASSES)], np.int32)
_PAIR_B = np.array([EXPERTS_PER_GROUP * (c // len(PAIRS)) + PAIRS[c % len(PAIRS)][1]
                    for c in range(N_CLASSES)], np.int32)


def _moe(layer, u2, cls, rank, cnt, w_gate, w_up, w_down, wr_rows):
    counts = cnt[:N_CLASSES, 0]
    tiles = (counts + TM_EXPERT - 1) // TM_EXPERT
    tile_end = jnp.cumsum(tiles)
    row_start = (tile_end - tiles) * TM_EXPERT
    n_used = tile_end[-1:]
    cls = cls.reshape(-1)
    dest = row_start[cls] + rank.reshape(-1)
    tile_cls = jnp.sum(jnp.arange(N_EXPERT_TILES)[:, None] >= tile_end[None, :], axis=1)
    tile_cls = jnp.minimum(tile_cls, tile_cls[jnp.maximum(n_used[0] - 1, 0)])
    ea = jnp.asarray(_PAIR_A)[tile_cls]
    eb = jnp.asarray(_PAIR_B)[tile_cls]
    chg = jnp.concatenate([jnp.ones((1,), jnp.int32),
                           (tile_cls[1:] != tile_cls[:-1]).astype(jnp.int32)])
    src = _invert_permutation(dest, PADDED_ROWS)
    ys = _experts(layer, ea, eb, chg, n_used.astype(jnp.int32), src, u2, w_gate, w_up, w_down, wr_rows)
    return dest, ys


def kernel(x, c, norm1_g, norm2_g, w_ada, b_ada, m_w_in, m_b_gates, m_norm_g, m_w_out,
           c_w_in, c_conv_w, c_conv_b, c_w_out, w_router, b_router,
           e_w_gate, e_w_up, e_w_down, final_g):
    xf = x.reshape(TOKENS, D_MODEL)
    mod = _ada(c, w_ada, b_ada)

    w_in = jnp.concatenate(
        [m_w_in[0].astype(BF16),
         jnp.zeros((D_MODEL, GATE_COLS - 2 * N_HEADS), BF16)], axis=1)
    bg = jnp.zeros((1, GATE_COLS), F32).at[0, :2 * N_HEADS].set(m_b_gates[0])
    wr = jnp.zeros((D_MODEL, ROUTER_COLS), BF16).at[:, :N_EXPERTS].set(w_router.astype(BF16))
    wr_rows = w_router.astype(BF16).astype(F32).T.reshape(N_EXPERTS, 1, D_MODEL)
    br = b_router.reshape(N_EXPERTS, 1)
    row = lambda v: v.reshape(1, -1)

    x1, u2, cls, rank, cnt = _mlstm_layer(xf, mod[0], row(norm1_g[0]), w_in, bg, row(m_norm_g[0]),
                                          m_w_out[0].astype(BF16), row(norm2_g[0]), wr, br)
    dest, ys = _moe(0, u2, cls, rank, cnt, e_w_gate, e_w_up, e_w_down, wr_rows)

    x3, u2, cls, rank, cnt = _conv_layer(
        dest, x1, ys, mod[0], mod[1], row(norm1_g[1]), c_w_in[0].astype(BF16), c_conv_w[0],
        row(c_conv_b[0]), c_w_out[0].astype(BF16), row(norm2_g[1]), wr, br)
    dest, ys = _moe(1, u2, cls, rank, cnt, e_w_gate, e_w_up, e_w_down, wr_rows)

    out = _final(dest, x3, ys, mod[1], row(final_g))
    return out.reshape(BATCH, SEQ, D_MODEL)
```

```python
import functools

import jax
import jax.numpy as jnp
import numpy as np
from jax import lax
from jax.experimental import pallas as pl
from jax.experimental.pallas import tpu as pltpu

F32 = jnp.float32
BF16 = jnp.bfloat16

D_MODEL = 1024
BATCH = 4
SEQ = 8192
TOKENS = BATCH * SEQ
N_HEADS = 4
DH_V = 256
DH_QK = 128
QK = N_HEADS * DH_QK
N_EXPERTS = 16
N_GROUPS = 4
EXPERTS_PER_GROUP = 4
D_EXPERT = 512
EPS = 1e-6

LANES = 128
SUBLANES = 8
VMEM_LIMIT_BYTES = 56 * 1024 * 1024

CHUNK = 128
TM_CONV = 512
TM_EXPERT = 512
GATE_COLS = LANES
GATE_WIDTH = 16
GATE_KINDS = 3
ROUTER_COLS = LANES
PAIRS = ((0, 1), (0, 2), (0, 3), (1, 2), (1, 3), (2, 3))
N_CLASSES = N_GROUPS * len(PAIRS)
CLASS_ROWS = 32
N_EXPERT_TILES = TOKENS // TM_EXPERT + N_CLASSES
PADDED_ROWS = N_EXPERT_TILES * TM_EXPERT
TM_FINAL = 512
EXPERT_GATHER_BUFFERS = 3
PROJ_COLS = 256
PROJ_EVERY = 1


def _params(*semantics):
    return pltpu.CompilerParams(dimension_semantics=semantics, vmem_limit_bytes=VMEM_LIMIT_BYTES)


def _dot(a, b):
    return jnp.dot(a, b, preferred_element_type=F32)


def _rms(x):
    return x * lax.rsqrt(jnp.mean(x * x, axis=-1, keepdims=True) + EPS)


def _sigmoid(x):
    return 1.0 / (1.0 + jnp.exp(-x))


ROW_TILE = D_MODEL // LANES


def _load_token_rows(ref, n):
    return jnp.concatenate([ref[pl.ds(c, n, stride=ROW_TILE), :] for c in range(ROW_TILE)], axis=1)


def _store_token_rows(ref, val, n):
    for c in range(ROW_TILE):
        ref[pl.ds(c, n, stride=ROW_TILE), :] = val[:, c * LANES:(c + 1) * LANES]


def _token_tile(ref, t):
    return ref.at[pl.ds(pl.multiple_of(t * ROW_TILE, ROW_TILE), ROW_TILE)]


def _start_row_gather(idx_ref, base, n, src_hbm, buf, sem):
    for r in range(n):
        pltpu.make_async_copy(_token_tile(src_hbm, idx_ref[base + r]), _token_tile(buf, r),
                              sem).start(priority=r % 2)


def _wait_row_gather(src_hbm, buf, sem):
    pltpu.make_async_copy(src_hbm.at[pl.ds(0, buf.shape[0])], buf, sem).wait()


def _gathered_tile(idx_ref, src_hbm, buf_ref, sem_ref, n, issue_first, n_tiles=None):
    i = pl.program_id(0)
    last = (pl.num_programs(0) if n_tiles is None else n_tiles) - 1
    nb = buf_ref.shape[0]
    ahead = nb - 1

    @pl.when(i == 0)
    def _():
        for k in range(ahead):
            _start_row_gather(idx_ref, jnp.minimum(k, last) * n, n, src_hbm, buf_ref.at[k], sem_ref.at[k])

    nxt = (i + ahead) % nb
    start_next = functools.partial(
        _start_row_gather, idx_ref, jnp.minimum(i + ahead, last) * n, n, src_hbm, buf_ref.at[nxt],
        sem_ref.at[nxt])
    if issue_first:
        start_next()
    slot = i % nb
    _wait_row_gather(src_hbm, buf_ref.at[slot], sem_ref.at[slot])
    rows = _load_token_rows(buf_ref.at[slot], n)
    if issue_first is None:
        return rows, start_next
    if not issue_first:
        start_next()
    return rows


def _drain_row_gather(src_hbm, buf_ref, sem_ref):
    i = pl.program_id(0)
    nb = buf_ref.shape[0]

    @pl.when(i == pl.num_programs(0) - 1)
    def _():
        for k in range(1, nb):
            _wait_row_gather(src_hbm, buf_ref.at[(i + k) % nb], sem_ref.at[(i + k) % nb])


def _ada_kernel(c_ref, w_ref, b_ref, o_ref):
    c = c_ref[...]
    cond = c * _sigmoid(c)
    o_ref[0] = jnp.dot(cond, w_ref[0], preferred_element_type=F32,
                       precision=lax.Precision.HIGHEST) + b_ref[0]


def _ada(c, w_ada, b_ada):
    depth, d, n = w_ada.shape
    tn = 1536
    c8 = jnp.zeros((SUBLANES, d), F32).at[:BATCH].set(c)
    out = pl.pallas_call(
        _ada_kernel,
        out_shape=jax.ShapeDtypeStruct((depth, SUBLANES, n), F32),
        grid=(depth, n // tn),
        in_specs=[
            pl.BlockSpec((SUBLANES, d), lambda l, j: (0, 0)),
            pl.BlockSpec((1, d, tn), lambda l, j: (l, 0, j)),
            pl.BlockSpec((1, 1, tn), lambda l, j: (l, 0, j)),
        ],
        out_specs=pl.BlockSpec((1, SUBLANES, tn), lambda l, j: (l, 0, j)),
        compiler_params=_params("arbitrary", "arbitrary"),
        name="ada_mod",
    )(c8, w_ada, b_ada.reshape(depth, 1, n))
    return out[:, :BATCH].reshape(depth, BATCH, 6, d)


def _mlstm_layer_kernel(xn_ref, x_ref, mod_ref, g1_ref, wi_ref, bg_ref, ng_ref, rep_ref,
                        wo_ref, g2_ref, wr_ref, br_ref, tri_ref,
                        x1_ref, u2_ref, cls_ref, rank_ref, cnt_ref,
                        q_ref, k_ref, v_ref, og_ref, gcol_ref, grow_ref, m_ref, h_ref, run_ref, *c_refs):
    step = pl.program_id(0)

    @pl.when(step == 0)
    def _():
        for ref in (q_ref, k_ref, v_ref, og_ref, gcol_ref, grow_ref, m_ref, h_ref, run_ref) + c_refs:
            ref[...] = jnp.zeros_like(ref)

    h_prev = h_ref[...]
    ln = CHUNK
    row = lax.broadcasted_iota(jnp.int32, (ln, ln), 0)
    col = lax.broadcasted_iota(jnp.int32, (ln, ln), 1)
    causal = col <= row
    ones = jnp.ones((ln, LANES), BF16)
    n_streams = BATCH * N_HEADS
    m_all = [m_ref[st] for st in range(n_streams)]
    m_out = [None] * n_streams

    reps = {}

    def replicated_gates(bi):
        if bi not in reps:
            g = gcol_ref[bi]
            g_hi = g.astype(BF16)
            g_r1 = g - g_hi.astype(F32)
            g_mid = g_r1.astype(BF16)
            g_lo = (g_r1 - g_mid.astype(F32)).astype(BF16)
            reps[bi] = _dot(g_hi, rep_ref[...]) + _dot(g_mid, rep_ref[...]) + _dot(g_lo, rep_ref[...])
        return reps[bi]

    def stream(st):
        bi, h = divmod(st, N_HEADS)
        rep = replicated_gates(bi)
        lanes = lambda kind: slice((kind * N_HEADS + h) * LANES, (kind * N_HEADS + h + 1) * LANES)
        cm, b, imb_col = rep[:, lanes(0)], rep[:, lanes(1)], rep[:, lanes(2)]
        imb_row = grow_ref[bi, h:h + 1, :]
        qh = q_ref[bi, :, h * DH_QK:(h + 1) * DH_QK]
        kh = k_ref[bi, :, h * DH_QK:(h + 1) * DH_QK]
        v_ext = jnp.concatenate([v_ref[bi, :, h * DH_V:(h + 1) * DH_V], ones], axis=1)
        c_ref = c_refs[st]
        c_prev = c_ref[...]
        m_prev = m_all[st]
        big_m = jnp.maximum(m_prev, cm)
        d_mat = jnp.exp(jnp.where(causal, imb_row - big_m, -jnp.inf))
        s_raw = lax.dot_general(qh, kh, (((1,), (1,)), ((), ())), preferred_element_type=F32)
        yield
        q_inter = (qh.astype(F32) * jnp.exp(m_prev - big_m)).astype(BF16)
        lhs = jnp.concatenate([q_inter, (s_raw * d_mat).astype(BF16)], axis=1)
        rhs = jnp.concatenate([c_prev.astype(BF16), v_ext], axis=0)
        nd = _dot(lhs, rhs)
        m_last = big_m[ln - 1:ln, :]
        kw = (kh.astype(F32) * jnp.exp(imb_col - m_last)).astype(BF16)
        update = lax.dot_general(kw, v_ext, (((0,), (0,)), ((), ())), preferred_element_type=F32)
        yield
        decay = jnp.exp(m_prev - m_last)
        c_ref[...] = jnp.concatenate([decay] * 3, axis=1) * c_prev + update
        m_out[st] = b[ln - 1:ln, :] + m_last
        den = nd[:, DH_V:]
        inv = 1.0 / jnp.maximum(jnp.abs(den), jnp.exp(-(b + big_m)))
        hh = nd[:, :DH_V] * jnp.concatenate([inv, inv], axis=1)
        sl = slice(h * DH_V, (h + 1) * DH_V)
        gate = _sigmoid(og_ref[bi, :, sl].astype(F32))
        h_ref[bi * ln:(bi + 1) * ln, sl] = (_rms(hh) * ng_ref[:, sl] * gate).astype(BF16)

    deferred_stores = []

    def projection():
        mod = mod_ref[...]
        u = jnp.concatenate(
            [_rms(xn_ref[bi]) * g1_ref[...] * (1.0 + mod[bi][1:2]) + mod[bi][0:1] for bi in range(BATCH)],
            axis=0).astype(BF16)
        gt = _dot(u, wi_ref[:, 2 * QK + 2 * D_MODEL:]) + bg_ref[...]
        cols, imb = _gate_columns(gt)
        imb_rows = imb.T[0:SUBLANES, :]
        for bi in range(BATCH):
            gcol_ref[bi] = cols[bi * ln:(bi + 1) * ln, 0:GATE_WIDTH]
            grow_ref[bi] = imb_rows[:, bi * ln:(bi + 1) * ln]
        yield
        lo = 0
        for ref, width, scale in ((q_ref, QK, DH_QK ** -0.5), (k_ref, QK, None), (v_ref, D_MODEL, None),
                                  (og_ref, D_MODEL, None)):
            for off in range(0, width, PROJ_COLS):
                part = _dot(u, wi_ref[:, lo + off:lo + off + PROJ_COLS])
                if scale is not None:
                    part = part * scale
                part = part.astype(BF16)

                def store(ref=ref, off=off, part=part):
                    for bi in range(BATCH):
                        ref[bi, :, off:off + PROJ_COLS] = part[bi * ln:(bi + 1) * ln]

                if ref is og_ref:
                    deferred_stores.append(store)
                else:
                    store()
                yield
            lo += width

    def previous_chunk():
        half = D_MODEL // 2
        mix0 = _dot(h_prev, wo_ref[:, 0:half])
        yield
        mix1 = _dot(h_prev, wo_ref[:, half:])
        yield
        mod = mod_ref[...]
        u2_rows = []
        for bi in range(BATCH):
            mb = mod[bi]
            rows = slice(bi * ln, (bi + 1) * ln)
            x1 = x_ref[bi] + mb[2:3] * jnp.concatenate([mix0[rows], mix1[rows]], axis=1)
            x1_ref[bi] = x1
            u2_b = _moe_input(x1, mb, g2_ref)
            _store_token_rows(u2_ref.at[bi], u2_b, ln)
            u2_rows.append(u2_b)
        valid = jnp.where(step > 1, 1.0, 0.0)
        yield from _route_stages(jnp.concatenate(u2_rows, axis=0), valid, wr_ref, br_ref, tri_ref,
                                 cls_ref, rank_ref, cnt_ref, run_ref)

    _round_robin([previous_chunk()] + [stream(st) for st in range(n_streams)],
                 background=projection(), every=PROJ_EVERY, start_after=n_streams + 1)
    for store in deferred_stores:
        store()
    for st in range(n_streams):
        m_ref[st] = m_out[st]


def _gate_columns(gt):
    lane = lax.broadcasted_iota(jnp.int32, gt.shape, 1)
    pos = lax.broadcasted_iota(jnp.int32, gt.shape, 0) % CHUNK
    b = jnp.minimum(gt, 0.0) - jnp.log(1.0 + jnp.exp(-jnp.abs(gt)))
    shift = 1
    while shift < CHUNK:
        b = b + jnp.where(pos >= shift, pltpu.roll(b, shift, axis=0), 0.0)
        shift *= 2
    imb = gt - pltpu.roll(b, LANES - N_HEADS, axis=1)
    cm = imb
    shift = 1
    while shift < CHUNK:
        cm = jnp.maximum(cm, jnp.where(pos >= shift, pltpu.roll(cm, shift, axis=0), -jnp.inf))
        shift *= 2
    cols = jnp.where(lane < N_HEADS, cm,
                     jnp.where(lane < 2 * N_HEADS, b,
                               jnp.where(lane < 3 * N_HEADS, pltpu.roll(imb, 2 * N_HEADS, axis=1), 0.0)))
    return cols, imb


def _mlstm_layer(x, mod, g1, w_in, bg, norm_g, w_out, g2, wr, br):
    nc = SEQ // CHUNK
    n_streams = BATCH * N_HEADS
    rows = BATCH * CHUNK
    n_rep = GATE_KINDS * N_HEADS
    replicate = (jnp.arange(GATE_WIDTH)[:, None] == jnp.arange(n_rep * LANES)[None, :] // LANES
                 ).astype(BF16)
    newest = lambda c: jnp.minimum(c, nc - 1)
    oldest = lambda c: jnp.maximum(c - 2, 0)
    old_rows = lambda height, width: pl.BlockSpec((BATCH, height, width), lambda c: (0, oldest(c), 0))
    x3d = x.reshape(BATCH, SEQ, D_MODEL)
    x1, u2, cls, rank, cnt = pl.pallas_call(
        _mlstm_layer_kernel,
        out_shape=(
            jax.ShapeDtypeStruct((BATCH, SEQ, D_MODEL), F32),
            jax.ShapeDtypeStruct((BATCH, SEQ * ROW_TILE, LANES), F32),
            jax.ShapeDtypeStruct((nc, 1, rows), jnp.int32),
            jax.ShapeDtypeStruct((nc, 1, rows), jnp.int32),
            jax.ShapeDtypeStruct((CLASS_ROWS, LANES), jnp.int32),
        ),
        grid=(nc + 2,),
        in_specs=[
            pl.BlockSpec((BATCH, CHUNK, D_MODEL), lambda c: (0, newest(c), 0)),
            old_rows(CHUNK, D_MODEL),
            _const_spec((BATCH, 6, D_MODEL)),
            _const_spec((1, D_MODEL)),
            _const_spec(w_in.shape),
            _const_spec((1, GATE_COLS)),
            _const_spec((1, D_MODEL)),
            _const_spec((GATE_WIDTH, n_rep * LANES)),
            _const_spec((D_MODEL, D_MODEL)),
            _const_spec((1, D_MODEL)),
            _const_spec((D_MODEL, ROUTER_COLS)),
            _const_spec((N_EXPERTS, 1)),
            _const_spec((rows, rows)),
        ],
        out_specs=(
            old_rows(CHUNK, D_MODEL),
            old_rows(CHUNK * ROW_TILE, LANES),
            pl.BlockSpec((1, 1, rows), lambda c: (oldest(c), 0, 0)),
            pl.BlockSpec((1, 1, rows), lambda c: (oldest(c), 0, 0)),
            _const_spec((CLASS_ROWS, LANES)),
        ),
        scratch_shapes=[pltpu.VMEM((BATCH, CHUNK, QK), BF16),
                        pltpu.VMEM((BATCH, CHUNK, QK), BF16),
                        pltpu.VMEM((BATCH, CHUNK, D_MODEL), BF16),
                        pltpu.VMEM((BATCH, CHUNK, D_MODEL), BF16),
                        pltpu.VMEM((BATCH, CHUNK, GATE_WIDTH), F32),
                        pltpu.VMEM((BATCH, SUBLANES, CHUNK), F32),
                        pltpu.VMEM((n_streams, 1, LANES), F32),
                        pltpu.VMEM((rows, D_MODEL), BF16),
                        pltpu.VMEM((CLASS_ROWS, LANES), F32)]
        + [pltpu.VMEM((DH_QK, DH_V + LANES), F32) for _ in range(n_streams)],
        compiler_params=_params("arbitrary"),
        name="mlstm_layer",
    )(x3d, x3d, mod, g1, w_in, bg, norm_g, replicate, w_out, g2, wr, br, _strict_upper(rows))
    token_order = lambda a: a.reshape(nc, BATCH, CHUNK).transpose(1, 0, 2).reshape(TOKENS)
    return (x1.reshape(TOKENS, D_MODEL), u2.reshape(TOKENS * ROW_TILE, LANES), token_order(cls),
            token_order(rank), cnt)


def _top2_sum(v0, v1, v2, v3):
    hi1, lo1 = jnp.maximum(v0, v1), jnp.minimum(v0, v1)
    hi2, lo2 = jnp.maximum(v2, v3), jnp.minimum(v2, v3)
    return jnp.maximum(hi1, hi2) + jnp.maximum(jnp.minimum(hi1, hi2), jnp.maximum(lo1, lo2))


def _route_tail(x_new, m, valid, g2_ref, wr_ref, br_ref, tri_ref, u2_ref, cls_ref, rank_ref, cnt_ref,
                run_ref, after_router=None):
    u2 = _moe_input(x_new, m, g2_ref)
    _store_token_rows(u2_ref, u2, u2.shape[0])
    _route(u2, valid, wr_ref, br_ref, tri_ref, cls_ref, rank_ref, cnt_ref, run_ref, after_router)


def _moe_input(x_new, m, g2_ref):
    return _rms(x_new) * g2_ref[...] * (1.0 + m[4:5]) + m[3:4]


def _round_robin(chains, background=None, every=1, start_after=0):
    chains = list(chains)
    advanced = 0
    while chains:
        alive = []
        for chain in chains:
            try:
                next(chain)
            except StopIteration:
                continue
            alive.append(chain)
            advanced += 1
            if background is not None and advanced >= start_after and advanced % every == 0:
                next(background, None)
        chains = alive
    if background is not None:
        for _ in background:
            pass


def _route(u2, valid, wr_ref, br_ref, tri_ref, cls_ref, rank_ref, cnt_ref, run_ref, after_router=None):
    stages = _route_stages(u2, valid, wr_ref, br_ref, tri_ref, cls_ref, rank_ref, cnt_ref, run_ref)
    next(stages)
    if after_router is not None:
        after_router()
    for _ in stages:
        pass


def _route_stages(u2, valid, wr_ref, br_ref, tri_ref, cls_ref, rank_ref, cnt_ref, run_ref):
    logits = _dot(u2.astype(BF16), wr_ref[...])
    yield
    lt = logits.T[0:N_EXPERTS, :]
    e = jnp.exp(lt - jnp.max(lt, axis=0, keepdims=True))
    probs = e / jnp.sum(e, axis=0, keepdims=True)
    sel = probs + br_ref[...]
    sel_rows = [sel[j:j + 1, :] for j in range(N_EXPERTS)]
    best = jnp.zeros_like(sel_rows[0], dtype=jnp.int32)
    best_score = _top2_sum(*sel_rows[0:EXPERTS_PER_GROUP])
    for g in range(1, N_GROUPS):
        score = _top2_sum(*sel_rows[g * EXPERTS_PER_GROUP:(g + 1) * EXPERTS_PER_GROUP])
        better = score > best_score
        best = jnp.where(better, g, best)
        best_score = jnp.where(better, score, best_score)
    s = []
    for j in range(EXPERTS_PER_GROUP):
        sj = sel_rows[j]
        for g in range(1, N_GROUPS):
            sj = jnp.where(best == g, sel_rows[g * EXPERTS_PER_GROUP + j], sj)
        s.append(sj)
    chosen = []
    for j in range(EXPERTS_PER_GROUP):
        beaten = jnp.zeros_like(best)
        for i in range(EXPERTS_PER_GROUP):
            if i == j:
                continue
            wins = (s[i] >= s[j]) if i < j else (s[i] > s[j])
            beaten = beaten + jnp.where(wins, 1, 0)
        chosen.append(beaten < 2)
    pair = jnp.full_like(best, len(PAIRS) - 1)
    for p in range(len(PAIRS) - 2, -1, -1):
        a, b = PAIRS[p]
        pair = jnp.where(jnp.logical_and(chosen[a], chosen[b]), p, pair)
    cls = best * len(PAIRS) + pair
    cls_ref[0] = cls

    class_id = lax.broadcasted_iota(jnp.int32, (CLASS_ROWS, cls.shape[1]), 0)
    onehot = class_id == cls
    before = _dot(jnp.where(onehot, 1.0, 0.0).astype(BF16), tri_ref[...])
    run = run_ref[...]
    rank = jnp.sum(jnp.where(onehot, before + run[:, 0:1], 0.0), axis=0, keepdims=True)
    rank_ref[0] = rank.astype(jnp.int32)
    run = run + valid * jnp.sum(jnp.where(onehot, 1.0, 0.0), axis=1, keepdims=True)
    run_ref[...] = run
    cnt_ref[...] = run.astype(jnp.int32)


def _strict_upper(n):
    return (jnp.arange(n)[:, None] < jnp.arange(n)[None, :]).astype(BF16)


def _route_out_shapes(t, tm):
    return (
        jax.ShapeDtypeStruct((t, D_MODEL), F32),
        jax.ShapeDtypeStruct((t * ROW_TILE, LANES), F32),
        jax.ShapeDtypeStruct((t // tm, 1, tm), jnp.int32),
        jax.ShapeDtypeStruct((t // tm, 1, tm), jnp.int32),
        jax.ShapeDtypeStruct((CLASS_ROWS, LANES), jnp.int32),
    )


def _cur_tile(n_tiles):
    return lambda i: jnp.minimum(i, n_tiles - 1)


def _prev_tile(i):
    return jnp.maximum(i - 1, 0)


def _route_out_specs(tm, n_tiles):
    cur = _cur_tile(n_tiles)
    return (
        pl.BlockSpec((tm, D_MODEL), lambda i, *_: (cur(i), 0)),
        pl.BlockSpec((tm * ROW_TILE, LANES), lambda i, *_: (_prev_tile(i), 0)),
        pl.BlockSpec((1, 1, tm), lambda i, *_: (_prev_tile(i), 0, 0)),
        pl.BlockSpec((1, 1, tm), lambda i, *_: (_prev_tile(i), 0, 0)),
        pl.BlockSpec((CLASS_ROWS, LANES), lambda i, *_: (0, 0)),
    )


def _init_deferred_tail(xprev_ref, run_ref):
    @pl.when(pl.program_id(0) == 0)
    def _():
        xprev_ref[...] = jnp.zeros_like(xprev_ref)
        run_ref[...] = jnp.zeros_like(run_ref)


def _deferred_tail(xprev_ref, modp_ref, *tail_refs, after_router=None):
    valid = jnp.where(pl.program_id(0) > 0, 1.0, 0.0)
    _route_tail(xprev_ref[...], modp_ref[0], valid, *tail_refs, after_router=after_router)


def _const_spec(shape):
    return pl.BlockSpec(shape, lambda i, *_: (0,) * len(shape))


def _conv_layer_kernel(dest_ref, x_ref, ys_ref, mod0_ref, mod_ref, modp_ref, g1_ref, wi_ref, cw_ref,
                       cb_ref, wo_ref, g2_ref, wr_ref, br_ref, tri_ref,
                       x3_ref, u2_ref, cls_ref, rank_ref, cnt_ref,
                       run_ref, carry_ref, ybuf_ref, ysem_ref, xprev_ref):
    tm = x_ref.shape[0]
    tiles_per_batch = SEQ // tm
    n_tiles = pl.num_programs(0) - 1
    i = pl.program_id(0)

    @pl.when(i % tiles_per_batch == 0)
    def _():
        carry_ref[...] = jnp.zeros_like(carry_ref)

    _init_deferred_tail(xprev_ref, run_ref)
    m = mod_ref[0]
    y = _gathered_tile(dest_ref, ys_ref, ybuf_ref, ysem_ref, tm, issue_first=False, n_tiles=n_tiles)
    x2 = x_ref[...] + mod0_ref[0][5:6] * y
    u = (_rms(x2) * g1_ref[...] * (1.0 + m[1:2]) + m[0:1]).astype(BF16)
    bgate = _dot(u, wi_ref[:, 0:D_MODEL])
    gated = []

    def project_gated():
        gated.append(_dot(u, wi_ref[:, D_MODEL:2 * D_MODEL]) * _dot(u, wi_ref[:, 2 * D_MODEL:]))

    _deferred_tail(xprev_ref, modp_ref, g2_ref, wr_ref, br_ref, tri_ref, u2_ref, cls_ref, rank_ref,
                   cnt_ref, run_ref, after_router=project_gated)
    z = gated[0]
    prev = carry_ref[...]
    row = lax.broadcasted_iota(jnp.int32, z.shape, 0)
    z1 = jnp.where(row == 0, prev[7:8], pltpu.roll(z, 1, axis=0))
    z2 = jnp.where(row == 0, prev[6:7], jnp.where(row == 1, prev[7:8], pltpu.roll(z, 2, axis=0)))
    carry_ref[...] = z[tm - SUBLANES:, :]
    cw = cw_ref[...]
    zc = cw[0:1] * z2 + cw[1:2] * z1 + cw[2:3] * z + cb_ref[...]
    x3 = x2 + m[2:3] * _dot((bgate * zc).astype(BF16), wo_ref[...])
    xprev_ref[...] = x3

    @pl.when(i < n_tiles)
    def _():
        x3_ref[...] = x3

    _drain_row_gather(ys_ref, ybuf_ref, ysem_ref)


def _conv_layer(dest, x, ys, mod0, mod, g1, w_in, conv_w, conv_b, w_out, g2, wr, br):
    t = x.shape[0]
    tm = TM_CONV
    tri = _strict_upper(tm)
    tiles_per_batch = SEQ // tm
    n_tiles = t // tm
    cur = _cur_tile(n_tiles)
    mod_spec = pl.BlockSpec((1, 6, D_MODEL), lambda i, *_: (cur(i) // tiles_per_batch, 0, 0))
    modp_spec = pl.BlockSpec((1, 6, D_MODEL), lambda i, *_: (_prev_tile(i) // tiles_per_batch, 0, 0))
    return pl.pallas_call(
        _conv_layer_kernel,
        out_shape=_route_out_shapes(t, tm),
        grid_spec=pltpu.PrefetchScalarGridSpec(
            num_scalar_prefetch=1,
            grid=(n_tiles + 1,),
            in_specs=[
                pl.BlockSpec((tm, D_MODEL), lambda i, *_: (cur(i), 0)),
                pl.BlockSpec(memory_space=pl.ANY),
                mod_spec,
                mod_spec,
                modp_spec,
                _const_spec((1, D_MODEL)),
                _const_spec((D_MODEL, 3 * D_MODEL)),
                _const_spec((3, D_MODEL)),
                _const_spec((1, D_MODEL)),
                _const_spec((D_MODEL, D_MODEL)),
                _const_spec((1, D_MODEL)),
                _const_spec((D_MODEL, ROUTER_COLS)),
                _const_spec((N_EXPERTS, 1)),
                _const_spec((tm, tm)),
            ],
            out_specs=_route_out_specs(tm, n_tiles),
            scratch_shapes=[
                pltpu.VMEM((CLASS_ROWS, LANES), F32),
                pltpu.VMEM((SUBLANES, D_MODEL), F32),
                pltpu.VMEM((2, tm * ROW_TILE, LANES), F32),
                pltpu.SemaphoreType.DMA((2,)),
                pltpu.VMEM((tm, D_MODEL), F32),
            ],
        ),
        compiler_params=_params("arbitrary"),
        name="conv_layer_route",
    )(dest, x, ys, mod0, mod, mod, g1, w_in, conv_w, conv_b, w_out, g2, wr, br, tri)


def _invert_kernel(dest_ref, fill_ref, src_ref, sem):
    fill = pltpu.make_async_copy(fill_ref, src_ref, sem)
    fill.start()
    fill.wait()
    unroll = 8

    def body(t8, carry):
        for k in range(unroll):
            t = t8 * unroll + k
            src_ref[dest_ref[t]] = t
        return carry

    lax.fori_loop(0, dest_ref.shape[0] // unroll, body, 0)


def _invert_permutation(dest, n_out):
    smem = pl.BlockSpec(memory_space=pltpu.SMEM)
    return pl.pallas_call(
        _invert_kernel,
        out_shape=jax.ShapeDtypeStruct((n_out,), jnp.int32),
        in_specs=[smem, pl.BlockSpec(memory_space=pl.ANY)],
        out_specs=smem,
        scratch_shapes=[pltpu.SemaphoreType.DMA(())],
        name="invert_permutation",
    )(dest, jnp.arange(n_out, dtype=jnp.int32) % dest.shape[0])


def _expert_kernel(ea_ref, eb_ref, chg_ref, run_ref, nxa_ref, nxb_ref, hasn_ref, nused_ref,
                   src_ref, u2_ref, wg_hbm, wu_hbm, wd_hbm, wra_ref, wrb_ref, y_ref,
                   xbuf_ref, xsem_ref, wup_ref, wdn_ref, wsem_ref, *wbf_refs, layer):
    j = pl.program_id(0)
    tm = TM_EXPERT

    def weight_copies(slot, a, b):
        sem = wsem_ref.at[slot]
        return [pltpu.make_async_copy(wg_hbm.at[layer, a], wup_ref.at[slot, 0], sem),
                pltpu.make_async_copy(wu_hbm.at[layer, a], wup_ref.at[slot, 1], sem),
                pltpu.make_async_copy(wd_hbm.at[layer, a], wdn_ref.at[slot, 0], sem),
                pltpu.make_async_copy(wg_hbm.at[layer, b], wup_ref.at[slot, 2], sem),
                pltpu.make_async_copy(wu_hbm.at[layer, b], wup_ref.at[slot, 3], sem),
                pltpu.make_async_copy(wd_hbm.at[layer, b], wdn_ref.at[slot, 1], sem)]

    @pl.when(j == 0)
    def _():
        for copy in weight_copies(0, ea_ref[0], eb_ref[0]):
            copy.start()

    @pl.when(chg_ref[j] == 1)
    def _():
        slot = run_ref[j] % 2
        for copy in weight_copies(slot, ea_ref[j], eb_ref[j]):
            copy.wait()
        staged = (wup_ref.at[slot, 0], wup_ref.at[slot, 1], wdn_ref.at[slot, 0],
                  wup_ref.at[slot, 2], wup_ref.at[slot, 3], wdn_ref.at[slot, 1])
        for src, dst in zip(staged, wbf_refs):
            dst[...] = src[...].astype(BF16)

        @pl.when(hasn_ref[j] == 1)
        def _():
            for copy in weight_copies(1 - slot, nxa_ref[j], nxb_ref[j]):
                copy.start()

    rows, start_next = _gathered_tile(src_ref, u2_ref, xbuf_ref, xsem_ref, tm, issue_first=None)

    @pl.when(j < nused_ref[0])
    def _():
        start_next()
        xb = rows.astype(BF16)
        dl = jnp.sum(xb.astype(F32) * (wra_ref[0] - wrb_ref[0]), axis=-1, keepdims=True)
        w_a = _sigmoid(dl)
        w_b = _sigmoid(-dl)

        def ffn(wg_ref, wu_ref, wd_ref):
            gate = _dot(xb, wg_ref[...])
            hidden = gate * _sigmoid(gate) * _dot(xb, wu_ref[...])
            return _dot(hidden.astype(BF16), wd_ref[...])

        y = w_a * ffn(*wbf_refs[0:3]) + w_b * ffn(*wbf_refs[3:6])
        _store_token_rows(y_ref, y, tm)

    @pl.when(j >= nused_ref[0])
    def _():
        start_next()
        y_ref[...] = jnp.zeros_like(y_ref)

    _drain_row_gather(u2_ref, xbuf_ref, xsem_ref)


def _experts(layer, plan, src, u2, w_gate, w_up, w_down, wr_rows):
    tm = TM_EXPERT
    any_spec = pl.BlockSpec(memory_space=pl.ANY)
    wr_spec = lambda which: pl.BlockSpec((1, 1, D_MODEL), lambda j, *p: (p[which][j], 0, 0))
    up_scratch = pltpu.VMEM((D_MODEL, D_EXPERT), BF16)
    down_scratch = pltpu.VMEM((D_EXPERT, D_MODEL), BF16)
    return pl.pallas_call(
        functools.partial(_expert_kernel, layer=layer),
        out_shape=jax.ShapeDtypeStruct((PADDED_ROWS * ROW_TILE, LANES), F32),
        grid_spec=pltpu.PrefetchScalarGridSpec(
            num_scalar_prefetch=len(plan),
            grid=(N_EXPERT_TILES,),
            in_specs=[
                pl.BlockSpec(memory_space=pltpu.SMEM),
                any_spec,
                any_spec, any_spec, any_spec,
                wr_spec(0), wr_spec(1),
            ],
            out_specs=pl.BlockSpec((tm * ROW_TILE, LANES), lambda j, *_: (j, 0)),
            scratch_shapes=[pltpu.VMEM((EXPERT_GATHER_BUFFERS, tm * ROW_TILE, LANES), F32),
                            pltpu.SemaphoreType.DMA((EXPERT_GATHER_BUFFERS,)),
                            pltpu.VMEM((2, 4, D_MODEL, D_EXPERT), F32),
                            pltpu.VMEM((2, 2, D_EXPERT, D_MODEL), F32),
                            pltpu.SemaphoreType.DMA((2,)),
                            up_scratch, up_scratch, down_scratch, up_scratch, up_scratch, down_scratch],
        ),
        compiler_params=_params("arbitrary"),
        name="grouped_experts",
    )(*plan, src, u2, w_gate, w_up, w_down, wr_rows, wr_rows)


def _final_kernel(dest_ref, x_ref, ys_ref, mod_ref, g_ref, o_ref, ybuf_ref, ysem_ref):
    y = _gathered_tile(dest_ref, ys_ref, ybuf_ref, ysem_ref, x_ref.shape[0], issue_first=True)
    x = x_ref[...] + mod_ref[0][5:6] * y
    o_ref[...] = _rms(x) * g_ref[...]
    _drain_row_gather(ys_ref, ybuf_ref, ysem_ref)


def _final(dest, x, ys, mod, g):
    t = x.shape[0]
    tm = TM_FINAL
    tiles_per_batch = SEQ // tm
    return pl.pallas_call(
        _final_kernel,
        out_shape=jax.ShapeDtypeStruct((t, D_MODEL), F32),
        grid_spec=pltpu.PrefetchScalarGridSpec(
            num_scalar_prefetch=1,
            grid=(t // tm,),
            in_specs=[
                pl.BlockSpec((tm, D_MODEL), lambda i, *_: (i, 0)),
                pl.BlockSpec(memory_space=pl.ANY),
                pl.BlockSpec((1, 6, D_MODEL), lambda i, *_: (i // tiles_per_batch, 0, 0)),
                _const_spec((1, D_MODEL)),
            ],
            out_specs=pl.BlockSpec((tm, D_MODEL), lambda i, *_: (i, 0)),
            scratch_shapes=[pltpu.VMEM((2, tm * ROW_TILE, LANES), F32), pltpu.SemaphoreType.DMA((2,))],
        ),
        compiler_params=_params("arbitrary"),
        name="final_norm",
    )(dest, x, ys, mod, g)


_PAIR_A = np.array([EXPERTS_PER_GROUP * (c // len(PAIRS)) + PAIRS[c % len(PAIRS)][0]
                    for c in range(N_CLASSES)], np.int32)
_PAIR_B = np.array([EXPERTS_PER_GROUP * (c // len(PAIRS)) + PAIRS[c % len(PAIRS)][1]
                    for c in range(N_CLASSES)], np.int32)


def _moe(layer, u2, cls, rank, cnt, w_gate, w_up, w_down, wr_rows):
    counts = cnt[:N_CLASSES, 0]
    tiles = (counts + TM_EXPERT - 1) // TM_EXPERT
    tile_end = jnp.cumsum(tiles)
    row_start = (tile_end - tiles) * TM_EXPERT
    n_used = tile_end[-1:]
    cls = cls.reshape(-1)
    dest = row_start[cls] + rank.reshape(-1)
    tile_cls = jnp.sum(jnp.arange(N_EXPERT_TILES)[:, None] >= tile_end[None, :], axis=1)
    tile_cls = jnp.minimum(tile_cls, tile_cls[jnp.maximum(n_used[0] - 1, 0)])
    pair_a, pair_b = jnp.asarray(_PAIR_A), jnp.asarray(_PAIR_B)
    chg = jnp.concatenate([jnp.ones((1,), jnp.int32),
                           (tile_cls[1:] != tile_cls[:-1]).astype(jnp.int32)])
    next_start = tile_end[tile_cls]
    next_cls = tile_cls[jnp.minimum(next_start, N_EXPERT_TILES - 1)]
    plan = (pair_a[tile_cls], pair_b[tile_cls], chg, jnp.cumsum(chg) - 1,
            pair_a[next_cls], pair_b[next_cls], (next_start < n_used[0]).astype(jnp.int32),
            n_used.astype(jnp.int32))
    src = _invert_permutation(dest, PADDED_ROWS)
    ys = _experts(layer, plan, src, u2, w_gate, w_up, w_down, wr_rows)
    return dest, ys


def kernel(x, c, norm1_g, norm2_g, w_ada, b_ada, m_w_in, m_b_gates, m_norm_g, m_w_out,
           c_w_in, c_conv_w, c_conv_b, c_w_out, w_router, b_router,
           e_w_gate, e_w_up, e_w_down, final_g):
    xf = x.reshape(TOKENS, D_MODEL)
    mod = _ada(c, w_ada, b_ada)

    w_in = jnp.concatenate(
        [m_w_in[0].astype(BF16),
         jnp.zeros((D_MODEL, GATE_COLS - 2 * N_HEADS), BF16)], axis=1)
    bg = jnp.zeros((1, GATE_COLS), F32).at[0, :2 * N_HEADS].set(m_b_gates[0])
    wr = jnp.zeros((D_MODEL, ROUTER_COLS), BF16).at[:, :N_EXPERTS].set(w_router.astype(BF16))
    wr_rows = w_router.astype(BF16).astype(F32).T.reshape(N_EXPERTS, 1, D_MODEL)
    br = b_router.reshape(N_EXPERTS, 1)
    row = lambda v: v.reshape(1, -1)

    x1, u2, cls, rank, cnt = _mlstm_layer(xf, mod[0], row(norm1_g[0]), w_in, bg, row(m_norm_g[0]),
                                          m_w_out[0].astype(BF16), row(norm2_g[0]), wr, br)
    dest, ys = _moe(0, u2, cls, rank, cnt, e_w_gate, e_w_up, e_w_down, wr_rows)

    x3, u2, cls, rank, cnt = _conv_layer(
        dest, x1, ys, mod[0], mod[1], row(norm1_g[1]), c_w_in[0].astype(BF16), c_conv_w[0],
        row(c_conv_b[0]), c_w_out[0].astype(BF16), row(norm2_g[1]), wr, br)
    dest, ys = _moe(1, u2, cls, rank, cnt, e_w_gate, e_w_up, e_w_down, wr_rows)

    out = _final(dest, x3, ys, mod[1], row(final_g))
    return out.reshape(BATCH, SEQ, D_MODEL)
```

```python
import functools

import jax
import jax.numpy as jnp
import numpy as np
from jax import lax
from jax.experimental import pallas as pl
from jax.experimental.pallas import tpu as pltpu

F32 = jnp.float32
BF16 = jnp.bfloat16

D_MODEL = 1024
BATCH = 4
SEQ = 8192
TOKENS = BATCH * SEQ
N_HEADS = 4
DH_V = 256
DH_QK = 128
QK = N_HEADS * DH_QK
N_EXPERTS = 16
N_GROUPS = 4
EXPERTS_PER_GROUP = 4
D_EXPERT = 512
EPS = 1e-6

LANES = 128
SUBLANES = 8
VMEM_LIMIT_BYTES = 56 * 1024 * 1024

CHUNK = 128
TM_CONV = 512
TM_EXPERT = 512
GATE_COLS = LANES
GATE_WIDTH = 16
GATE_KINDS = 3
ROUTER_COLS = LANES
PAIRS = ((0, 1), (0, 2), (0, 3), (1, 2), (1, 3), (2, 3))
N_CLASSES = N_GROUPS * len(PAIRS)
CLASS_ROWS = 32
N_EXPERT_TILES = TOKENS // TM_EXPERT + N_CLASSES
PADDED_ROWS = N_EXPERT_TILES * TM_EXPERT
TM_FINAL = 512
EXPERT_GATHER_BUFFERS = 3
PROJ_COLS = 256
PROJ_EVERY = 1


def _params(*semantics):
    return pltpu.CompilerParams(dimension_semantics=semantics, vmem_limit_bytes=VMEM_LIMIT_BYTES)


def _dot(a, b):
    return jnp.dot(a, b, preferred_element_type=F32)


def _rms(x):
    return x * lax.rsqrt(jnp.mean(x * x, axis=-1, keepdims=True) + EPS)


def _sigmoid(x):
    return 1.0 / (1.0 + jnp.exp(-x))


ROW_TILE = D_MODEL // LANES


def _load_token_rows(ref, n):
    return jnp.concatenate([ref[pl.ds(c, n, stride=ROW_TILE), :] for c in range(ROW_TILE)], axis=1)


def _store_token_rows(ref, val, n):
    for c in range(ROW_TILE):
        ref[pl.ds(c, n, stride=ROW_TILE), :] = val[:, c * LANES:(c + 1) * LANES]


def _token_tile(ref, t):
    return ref.at[pl.ds(pl.multiple_of(t * ROW_TILE, ROW_TILE), ROW_TILE)]


def _zero_after(v):
    bits = pltpu.bitcast(v[0:SUBLANES, 0:LANES], jnp.int32)
    return lax.shift_right_logical(lax.shift_right_logical(bits, 16), 16)[0, 0]


def _start_row_gather(idx_ref, base, n, src_hbm, buf, sem, part=0, n_parts=1, after=None):
    if after is not None:
        base = base + _zero_after(after)
    for r in range(part * n // n_parts, (part + 1) * n // n_parts):
        pltpu.make_async_copy(_token_tile(src_hbm, idx_ref[base + r]), _token_tile(buf, r),
                              sem).start(priority=r % 2)


def _wait_row_gather(src_hbm, buf, sem):
    pltpu.make_async_copy(src_hbm.at[pl.ds(0, buf.shape[0])], buf, sem).wait()


def _gathered_tile(idx_ref, src_hbm, buf_ref, sem_ref, n, issue_first, n_tiles=None):
    i = pl.program_id(0)
    last = (pl.num_programs(0) if n_tiles is None else n_tiles) - 1
    nb = buf_ref.shape[0]
    ahead = nb - 1

    @pl.when(i == 0)
    def _():
        for k in range(ahead):
            _start_row_gather(idx_ref, jnp.minimum(k, last) * n, n, src_hbm, buf_ref.at[k], sem_ref.at[k])

    nxt = (i + ahead) % nb
    start_next = functools.partial(
        _start_row_gather, idx_ref, jnp.minimum(i + ahead, last) * n, n, src_hbm, buf_ref.at[nxt],
        sem_ref.at[nxt])
    if issue_first:
        start_next()
    slot = i % nb
    _wait_row_gather(src_hbm, buf_ref.at[slot], sem_ref.at[slot])
    rows = _load_token_rows(buf_ref.at[slot], n)
    if issue_first is None:
        return rows, start_next
    if not issue_first:
        start_next()
    return rows


def _drain_row_gather(src_hbm, buf_ref, sem_ref):
    i = pl.program_id(0)
    nb = buf_ref.shape[0]

    @pl.when(i == pl.num_programs(0) - 1)
    def _():
        for k in range(1, nb):
            _wait_row_gather(src_hbm, buf_ref.at[(i + k) % nb], sem_ref.at[(i + k) % nb])


def _ada_kernel(c_ref, w_ref, b_ref, o_ref):
    c = c_ref[...]
    cond = c * _sigmoid(c)
    o_ref[0] = jnp.dot(cond, w_ref[0], preferred_element_type=F32,
                       precision=lax.Precision.HIGHEST) + b_ref[0]


def _ada(c, w_ada, b_ada):
    depth, d, n = w_ada.shape
    tn = 1536
    c8 = jnp.zeros((SUBLANES, d), F32).at[:BATCH].set(c)
    out = pl.pallas_call(
        _ada_kernel,
        out_shape=jax.ShapeDtypeStruct((depth, SUBLANES, n), F32),
        grid=(depth, n // tn),
        in_specs=[
            pl.BlockSpec((SUBLANES, d), lambda l, j: (0, 0)),
            pl.BlockSpec((1, d, tn), lambda l, j: (l, 0, j)),
            pl.BlockSpec((1, 1, tn), lambda l, j: (l, 0, j)),
        ],
        out_specs=pl.BlockSpec((1, SUBLANES, tn), lambda l, j: (l, 0, j)),
        compiler_params=_params("arbitrary", "arbitrary"),
        name="ada_mod",
    )(c8, w_ada, b_ada.reshape(depth, 1, n))
    return out[:, :BATCH].reshape(depth, BATCH, 6, d)


def _mlstm_layer_kernel(xn_ref, x_ref, mod_ref, g1_ref, wi_ref, bg_ref, ng_ref, rep_ref,
                        wo_ref, g2_ref, wr_ref, br_ref, tri_ref,
                        x1_ref, u2_ref, cls_ref, rank_ref, cnt_ref,
                        q_ref, k_ref, v_ref, og_ref, gcol_ref, grow_ref, m_ref, h_ref, run_ref, *c_refs):
    step = pl.program_id(0)

    @pl.when(step == 0)
    def _():
        for ref in (q_ref, k_ref, v_ref, og_ref, gcol_ref, grow_ref, m_ref, h_ref, run_ref) + c_refs:
            ref[...] = jnp.zeros_like(ref)

    h_prev = h_ref[...]
    ln = CHUNK
    row = lax.broadcasted_iota(jnp.int32, (ln, ln), 0)
    col = lax.broadcasted_iota(jnp.int32, (ln, ln), 1)
    causal = col <= row
    ones = jnp.ones((ln, LANES), BF16)
    n_streams = BATCH * N_HEADS
    m_all = [m_ref[st] for st in range(n_streams)]
    m_out = [None] * n_streams

    reps = {}

    def replicated_gates(bi):
        if bi not in reps:
            g = gcol_ref[bi]
            g_hi = g.astype(BF16)
            g_r1 = g - g_hi.astype(F32)
            g_mid = g_r1.astype(BF16)
            g_lo = (g_r1 - g_mid.astype(F32)).astype(BF16)
            reps[bi] = _dot(g_hi, rep_ref[...]) + _dot(g_mid, rep_ref[...]) + _dot(g_lo, rep_ref[...])
        return reps[bi]

    def stream(st):
        bi, h = divmod(st, N_HEADS)
        rep = replicated_gates(bi)
        lanes = lambda kind: slice((kind * N_HEADS + h) * LANES, (kind * N_HEADS + h + 1) * LANES)
        cm, b, imb_col = rep[:, lanes(0)], rep[:, lanes(1)], rep[:, lanes(2)]
        imb_row = grow_ref[bi, h:h + 1, :]
        qh = q_ref[bi, :, h * DH_QK:(h + 1) * DH_QK]
        kh = k_ref[bi, :, h * DH_QK:(h + 1) * DH_QK]
        v_ext = jnp.concatenate([v_ref[bi, :, h * DH_V:(h + 1) * DH_V], ones], axis=1)
        c_ref = c_refs[st]
        c_prev = c_ref[...]
        m_prev = m_all[st]
        big_m = jnp.maximum(m_prev, cm)
        d_mat = jnp.exp(jnp.where(causal, imb_row - big_m, -jnp.inf))
        s_raw = lax.dot_general(qh, kh, (((1,), (1,)), ((), ())), preferred_element_type=F32)
        yield
        q_inter = (qh.astype(F32) * jnp.exp(m_prev - big_m)).astype(BF16)
        lhs = jnp.concatenate([q_inter, (s_raw * d_mat).astype(BF16)], axis=1)
        rhs = jnp.concatenate([c_prev.astype(BF16), v_ext], axis=0)
        nd = _dot(lhs, rhs)
        m_last = big_m[ln - 1:ln, :]
        kw = (kh.astype(F32) * jnp.exp(imb_col - m_last)).astype(BF16)
        update = lax.dot_general(kw, v_ext, (((0,), (0,)), ((), ())), preferred_element_type=F32)
        yield
        decay = jnp.exp(m_prev - m_last)
        c_ref[...] = jnp.concatenate([decay] * 3, axis=1) * c_prev + update
        m_out[st] = b[ln - 1:ln, :] + m_last
        den = nd[:, DH_V:]
        inv = 1.0 / jnp.maximum(jnp.abs(den), jnp.exp(-(b + big_m)))
        hh = nd[:, :DH_V] * jnp.concatenate([inv, inv], axis=1)
        sl = slice(h * DH_V, (h + 1) * DH_V)
        gate = _sigmoid(og_ref[bi, :, sl].astype(F32))
        h_ref[bi * ln:(bi + 1) * ln, sl] = (_rms(hh) * ng_ref[:, sl] * gate).astype(BF16)

    deferred_stores = []

    def projection():
        mod = mod_ref[...]
        u = jnp.concatenate(
            [_rms(xn_ref[bi]) * g1_ref[...] * (1.0 + mod[bi][1:2]) + mod[bi][0:1] for bi in range(BATCH)],
            axis=0).astype(BF16)
        gt = _dot(u, wi_ref[:, 2 * QK + 2 * D_MODEL:]) + bg_ref[...]
        cols, imb = _gate_columns(gt)
        imb_rows = imb.T[0:SUBLANES, :]
        for bi in range(BATCH):
            gcol_ref[bi] = cols[bi * ln:(bi + 1) * ln, 0:GATE_WIDTH]
            grow_ref[bi] = imb_rows[:, bi * ln:(bi + 1) * ln]
        yield
        lo = 0
        for ref, width, scale in ((q_ref, QK, DH_QK ** -0.5), (k_ref, QK, None), (v_ref, D_MODEL, None),
                                  (og_ref, D_MODEL, None)):
            for off in range(0, width, PROJ_COLS):
                part = _dot(u, wi_ref[:, lo + off:lo + off + PROJ_COLS])
                if scale is not None:
                    part = part * scale
                part = part.astype(BF16)

                def store(ref=ref, off=off, part=part):
                    for bi in range(BATCH):
                        ref[bi, :, off:off + PROJ_COLS] = part[bi * ln:(bi + 1) * ln]

                if ref is og_ref:
                    deferred_stores.append(store)
                else:
                    store()
                yield
            lo += width

    def previous_chunk():
        half = D_MODEL // 2
        mix0 = _dot(h_prev, wo_ref[:, 0:half])
        yield
        mix1 = _dot(h_prev, wo_ref[:, half:])
        yield
        mod = mod_ref[...]
        u2_rows = []
        for bi in range(BATCH):
            mb = mod[bi]
            rows = slice(bi * ln, (bi + 1) * ln)
            x1 = x_ref[bi] + mb[2:3] * jnp.concatenate([mix0[rows], mix1[rows]], axis=1)
            x1_ref[bi] = x1
            u2_b = _moe_input(x1, mb, g2_ref)
            _store_token_rows(u2_ref.at[bi], u2_b, ln)
            u2_rows.append(u2_b)
        valid = jnp.where(step > 1, 1.0, 0.0)
        yield from _route_stages(jnp.concatenate(u2_rows, axis=0), valid, wr_ref, br_ref, tri_ref,
                                 cls_ref, rank_ref, cnt_ref, run_ref)

    _round_robin([previous_chunk()] + [stream(st) for st in range(n_streams)],
                 background=projection(), every=PROJ_EVERY, start_after=n_streams + 1)
    for store in deferred_stores:
        store()
    for st in range(n_streams):
        m_ref[st] = m_out[st]


def _gate_columns(gt):
    lane = lax.broadcasted_iota(jnp.int32, gt.shape, 1)
    pos = lax.broadcasted_iota(jnp.int32, gt.shape, 0) % CHUNK
    b = jnp.minimum(gt, 0.0) - jnp.log(1.0 + jnp.exp(-jnp.abs(gt)))
    shift = 1
    while shift < CHUNK:
        b = b + jnp.where(pos >= shift, pltpu.roll(b, shift, axis=0), 0.0)
        shift *= 2
    imb = gt - pltpu.roll(b, LANES - N_HEADS, axis=1)
    cm = imb
    shift = 1
    while shift < CHUNK:
        cm = jnp.maximum(cm, jnp.where(pos >= shift, pltpu.roll(cm, shift, axis=0), -jnp.inf))
        shift *= 2
    cols = jnp.where(lane < N_HEADS, cm,
                     jnp.where(lane < 2 * N_HEADS, b,
                               jnp.where(lane < 3 * N_HEADS, pltpu.roll(imb, 2 * N_HEADS, axis=1), 0.0)))
    return cols, imb


def _mlstm_layer(x, mod, g1, w_in, bg, norm_g, w_out, g2, wr, br):
    nc = SEQ // CHUNK
    n_streams = BATCH * N_HEADS
    rows = BATCH * CHUNK
    n_rep = GATE_KINDS * N_HEADS
    replicate = (jnp.arange(GATE_WIDTH)[:, None] == jnp.arange(n_rep * LANES)[None, :] // LANES
                 ).astype(BF16)
    newest = lambda c: jnp.minimum(c, nc - 1)
    oldest = lambda c: jnp.maximum(c - 2, 0)
    old_rows = lambda height, width: pl.BlockSpec((BATCH, height, width), lambda c: (0, oldest(c), 0))
    x3d = x.reshape(BATCH, SEQ, D_MODEL)
    x1, u2, cls, rank, cnt = pl.pallas_call(
        _mlstm_layer_kernel,
        out_shape=(
            jax.ShapeDtypeStruct((BATCH, SEQ, D_MODEL), F32),
            jax.ShapeDtypeStruct((BATCH, SEQ * ROW_TILE, LANES), F32),
            jax.ShapeDtypeStruct((nc, 1, rows), jnp.int32),
            jax.ShapeDtypeStruct((nc, 1, rows), jnp.int32),
            jax.ShapeDtypeStruct((CLASS_ROWS, LANES), jnp.int32),
        ),
        grid=(nc + 2,),
        in_specs=[
            pl.BlockSpec((BATCH, CHUNK, D_MODEL), lambda c: (0, newest(c), 0)),
            old_rows(CHUNK, D_MODEL),
            _const_spec((BATCH, 6, D_MODEL)),
            _const_spec((1, D_MODEL)),
            _const_spec(w_in.shape),
            _const_spec((1, GATE_COLS)),
            _const_spec((1, D_MODEL)),
            _const_spec((GATE_WIDTH, n_rep * LANES)),
            _const_spec((D_MODEL, D_MODEL)),
            _const_spec((1, D_MODEL)),
            _const_spec((D_MODEL, ROUTER_COLS)),
            _const_spec((N_EXPERTS, 1)),
            _const_spec((rows, rows)),
        ],
        out_specs=(
            old_rows(CHUNK, D_MODEL),
            old_rows(CHUNK * ROW_TILE, LANES),
            pl.BlockSpec((1, 1, rows), lambda c: (oldest(c), 0, 0)),
            pl.BlockSpec((1, 1, rows), lambda c: (oldest(c), 0, 0)),
            _const_spec((CLASS_ROWS, LANES)),
        ),
        scratch_shapes=[pltpu.VMEM((BATCH, CHUNK, QK), BF16),
                        pltpu.VMEM((BATCH, CHUNK, QK), BF16),
                        pltpu.VMEM((BATCH, CHUNK, D_MODEL), BF16),
                        pltpu.VMEM((BATCH, CHUNK, D_MODEL), BF16),
                        pltpu.VMEM((BATCH, CHUNK, GATE_WIDTH), F32),
                        pltpu.VMEM((BATCH, SUBLANES, CHUNK), F32),
                        pltpu.VMEM((n_streams, 1, LANES), F32),
                        pltpu.VMEM((rows, D_MODEL), BF16),
                        pltpu.VMEM((CLASS_ROWS, LANES), F32)]
        + [pltpu.VMEM((DH_QK, DH_V + LANES), F32) for _ in range(n_streams)],
        compiler_params=_params("arbitrary"),
        name="mlstm_layer",
    )(x3d, x3d, mod, g1, w_in, bg, norm_g, replicate, w_out, g2, wr, br, _strict_upper(rows))
    token_order = lambda a: a.reshape(nc, BATCH, CHUNK).transpose(1, 0, 2).reshape(TOKENS)
    return (x1.reshape(TOKENS, D_MODEL), u2.reshape(TOKENS * ROW_TILE, LANES), token_order(cls),
            token_order(rank), cnt)


def _top2_sum(v0, v1, v2, v3):
    hi1, lo1 = jnp.maximum(v0, v1), jnp.minimum(v0, v1)
    hi2, lo2 = jnp.maximum(v2, v3), jnp.minimum(v2, v3)
    return jnp.maximum(hi1, hi2) + jnp.maximum(jnp.minimum(hi1, hi2), jnp.maximum(lo1, lo2))


def _route_tail(x_new, m, valid, g2_ref, wr_ref, br_ref, tri_ref, u2_ref, cls_ref, rank_ref, cnt_ref,
                run_ref, after_router=None):
    u2 = _moe_input(x_new, m, g2_ref)
    _store_token_rows(u2_ref, u2, u2.shape[0])
    _route(u2, valid, wr_ref, br_ref, tri_ref, cls_ref, rank_ref, cnt_ref, run_ref, after_router)


def _moe_input(x_new, m, g2_ref):
    return _rms(x_new) * g2_ref[...] * (1.0 + m[4:5]) + m[3:4]


def _round_robin(chains, background=None, every=1, start_after=0):
    chains = list(chains)
    advanced = 0
    while chains:
        alive = []
        for chain in chains:
            try:
                next(chain)
            except StopIteration:
                continue
            alive.append(chain)
            advanced += 1
            if background is not None and advanced >= start_after and advanced % every == 0:
                next(background, None)
        chains = alive
    if background is not None:
        for _ in background:
            pass


def _route(u2, valid, wr_ref, br_ref, tri_ref, cls_ref, rank_ref, cnt_ref, run_ref, after_router=None):
    stages = _route_stages(u2, valid, wr_ref, br_ref, tri_ref, cls_ref, rank_ref, cnt_ref, run_ref)
    next(stages)
    if after_router is not None:
        after_router()
    for _ in stages:
        pass


def _route_stages(u2, valid, wr_ref, br_ref, tri_ref, cls_ref, rank_ref, cnt_ref, run_ref):
    logits = _dot(u2.astype(BF16), wr_ref[...])
    yield
    lt = logits.T[0:N_EXPERTS, :]
    e = jnp.exp(lt - jnp.max(lt, axis=0, keepdims=True))
    probs = e / jnp.sum(e, axis=0, keepdims=True)
    sel = probs + br_ref[...]
    sel_rows = [sel[j:j + 1, :] for j in range(N_EXPERTS)]
    best = jnp.zeros_like(sel_rows[0], dtype=jnp.int32)
    best_score = _top2_sum(*sel_rows[0:EXPERTS_PER_GROUP])
    for g in range(1, N_GROUPS):
        score = _top2_sum(*sel_rows[g * EXPERTS_PER_GROUP:(g + 1) * EXPERTS_PER_GROUP])
        better = score > best_score
        best = jnp.where(better, g, best)
        best_score = jnp.where(better, score, best_score)
    s = []
    for j in range(EXPERTS_PER_GROUP):
        sj = sel_rows[j]
        for g in range(1, N_GROUPS):
            sj = jnp.where(best == g, sel_rows[g * EXPERTS_PER_GROUP + j], sj)
        s.append(sj)
    chosen = []
    for j in range(EXPERTS_PER_GROUP):
        beaten = jnp.zeros_like(best)
        for i in range(EXPERTS_PER_GROUP):
            if i == j:
                continue
            wins = (s[i] >= s[j]) if i < j else (s[i] > s[j])
            beaten = beaten + jnp.where(wins, 1, 0)
        chosen.append(beaten < 2)
    pair = jnp.full_like(best, len(PAIRS) - 1)
    for p in range(len(PAIRS) - 2, -1, -1):
        a, b = PAIRS[p]
        pair = jnp.where(jnp.logical_and(chosen[a], chosen[b]), p, pair)
    cls = best * len(PAIRS) + pair
    cls_ref[0] = cls

    class_id = lax.broadcasted_iota(jnp.int32, (CLASS_ROWS, cls.shape[1]), 0)
    onehot = class_id == cls
    before = _dot(jnp.where(onehot, 1.0, 0.0).astype(BF16), tri_ref[...])
    run = run_ref[...]
    rank = jnp.sum(jnp.where(onehot, before + run[:, 0:1], 0.0), axis=0, keepdims=True)
    rank_ref[0] = rank.astype(jnp.int32)
    run = run + valid * jnp.sum(jnp.where(onehot, 1.0, 0.0), axis=1, keepdims=True)
    run_ref[...] = run
    cnt_ref[...] = run.astype(jnp.int32)


def _strict_upper(n):
    return (jnp.arange(n)[:, None] < jnp.arange(n)[None, :]).astype(BF16)


def _route_out_shapes(t, tm):
    return (
        jax.ShapeDtypeStruct((t, D_MODEL), F32),
        jax.ShapeDtypeStruct((t * ROW_TILE, LANES), F32),
        jax.ShapeDtypeStruct((t // tm, 1, tm), jnp.int32),
        jax.ShapeDtypeStruct((t // tm, 1, tm), jnp.int32),
        jax.ShapeDtypeStruct((CLASS_ROWS, LANES), jnp.int32),
    )


def _cur_tile(n_tiles):
    return lambda i: jnp.minimum(i, n_tiles - 1)


def _prev_tile(i):
    return jnp.maximum(i - 1, 0)


def _route_out_specs(tm, n_tiles):
    cur = _cur_tile(n_tiles)
    return (
        pl.BlockSpec((tm, D_MODEL), lambda i, *_: (cur(i), 0)),
        pl.BlockSpec((tm * ROW_TILE, LANES), lambda i, *_: (_prev_tile(i), 0)),
        pl.BlockSpec((1, 1, tm), lambda i, *_: (_prev_tile(i), 0, 0)),
        pl.BlockSpec((1, 1, tm), lambda i, *_: (_prev_tile(i), 0, 0)),
        pl.BlockSpec((CLASS_ROWS, LANES), lambda i, *_: (0, 0)),
    )


def _init_deferred_tail(xprev_ref, run_ref):
    @pl.when(pl.program_id(0) == 0)
    def _():
        xprev_ref[...] = jnp.zeros_like(xprev_ref)
        run_ref[...] = jnp.zeros_like(run_ref)


def _deferred_tail(xprev_ref, modp_ref, *tail_refs, after_router=None):
    valid = jnp.where(pl.program_id(0) > 0, 1.0, 0.0)
    _route_tail(xprev_ref[...], modp_ref[0], valid, *tail_refs, after_router=after_router)


def _const_spec(shape):
    return pl.BlockSpec(shape, lambda i, *_: (0,) * len(shape))


def _conv_layer_kernel(dest_ref, x_ref, ys_ref, mod0_ref, mod_ref, modp_ref, g1_ref, wi_ref, cw_ref,
                       cb_ref, wo_ref, g2_ref, wr_ref, br_ref, tri_ref,
                       x3_ref, u2_ref, cls_ref, rank_ref, cnt_ref,
                       run_ref, carry_ref, ybuf_ref, ysem_ref, xprev_ref):
    tm = x_ref.shape[0]
    tiles_per_batch = SEQ // tm
    n_tiles = pl.num_programs(0) - 1
    i = pl.program_id(0)

    @pl.when(i % tiles_per_batch == 0)
    def _():
        carry_ref[...] = jnp.zeros_like(carry_ref)

    _init_deferred_tail(xprev_ref, run_ref)
    m = mod_ref[0]
    y, start_next = _gathered_tile(dest_ref, ys_ref, ybuf_ref, ysem_ref, tm, issue_first=None,
                                   n_tiles=n_tiles)
    start_next(part=0, n_parts=4)
    x2 = x_ref[...] + mod0_ref[0][5:6] * y
    u = (_rms(x2) * g1_ref[...] * (1.0 + m[1:2]) + m[0:1]).astype(BF16)
    bgate = _dot(u, wi_ref[:, 0:D_MODEL])
    start_next(part=1, n_parts=4)
    gated = []

    def project_gated():
        gated.append(_dot(u, wi_ref[:, D_MODEL:2 * D_MODEL]) * _dot(u, wi_ref[:, 2 * D_MODEL:]))
        start_next(part=2, n_parts=4)

    _deferred_tail(xprev_ref, modp_ref, g2_ref, wr_ref, br_ref, tri_ref, u2_ref, cls_ref, rank_ref,
                   cnt_ref, run_ref, after_router=project_gated)
    z = gated[0]
    prev = carry_ref[...]
    row = lax.broadcasted_iota(jnp.int32, z.shape, 0)
    z1 = jnp.where(row == 0, prev[7:8], pltpu.roll(z, 1, axis=0))
    z2 = jnp.where(row == 0, prev[6:7], jnp.where(row == 1, prev[7:8], pltpu.roll(z, 2, axis=0)))
    carry_ref[...] = z[tm - SUBLANES:, :]
    cw = cw_ref[...]
    zc = cw[0:1] * z2 + cw[1:2] * z1 + cw[2:3] * z + cb_ref[...]
    start_next(part=3, n_parts=4)
    x3 = x2 + m[2:3] * _dot((bgate * zc).astype(BF16), wo_ref[...])
    xprev_ref[...] = x3

    @pl.when(i < n_tiles)
    def _():
        x3_ref[...] = x3

    _drain_row_gather(ys_ref, ybuf_ref, ysem_ref)


def _conv_layer(dest, x, ys, mod0, mod, g1, w_in, conv_w, conv_b, w_out, g2, wr, br):
    t = x.shape[0]
    tm = TM_CONV
    tri = _strict_upper(tm)
    tiles_per_batch = SEQ // tm
    n_tiles = t // tm
    cur = _cur_tile(n_tiles)
    mod_spec = pl.BlockSpec((1, 6, D_MODEL), lambda i, *_: (cur(i) // tiles_per_batch, 0, 0))
    modp_spec = pl.BlockSpec((1, 6, D_MODEL), lambda i, *_: (_prev_tile(i) // tiles_per_batch, 0, 0))
    return pl.pallas_call(
        _conv_layer_kernel,
        out_shape=_route_out_shapes(t, tm),
        grid_spec=pltpu.PrefetchScalarGridSpec(
            num_scalar_prefetch=1,
            grid=(n_tiles + 1,),
            in_specs=[
                pl.BlockSpec((tm, D_MODEL), lambda i, *_: (cur(i), 0)),
                pl.BlockSpec(memory_space=pl.ANY),
                mod_spec,
                mod_spec,
                modp_spec,
                _const_spec((1, D_MODEL)),
                _const_spec((D_MODEL, 3 * D_MODEL)),
                _const_spec((3, D_MODEL)),
                _const_spec((1, D_MODEL)),
                _const_spec((D_MODEL, D_MODEL)),
                _const_spec((1, D_MODEL)),
                _const_spec((D_MODEL, ROUTER_COLS)),
                _const_spec((N_EXPERTS, 1)),
                _const_spec((tm, tm)),
            ],
            out_specs=_route_out_specs(tm, n_tiles),
            scratch_shapes=[
                pltpu.VMEM((CLASS_ROWS, LANES), F32),
                pltpu.VMEM((SUBLANES, D_MODEL), F32),
                pltpu.VMEM((2, tm * ROW_TILE, LANES), F32),
                pltpu.SemaphoreType.DMA((2,)),
                pltpu.VMEM((tm, D_MODEL), F32),
            ],
        ),
        compiler_params=_params("arbitrary"),
        name="conv_layer_route",
    )(dest, x, ys, mod0, mod, mod, g1, w_in, conv_w, conv_b, w_out, g2, wr, br, tri)


def _invert_kernel(dest_ref, fill_ref, src_ref, sem):
    fill = pltpu.make_async_copy(fill_ref, src_ref, sem)
    fill.start()
    fill.wait()
    unroll = 8

    def body(t8, carry):
        for k in range(unroll):
            t = t8 * unroll + k
            src_ref[dest_ref[t]] = t
        return carry

    lax.fori_loop(0, dest_ref.shape[0] // unroll, body, 0)


def _invert_permutation(dest, n_out):
    smem = pl.BlockSpec(memory_space=pltpu.SMEM)
    return pl.pallas_call(
        _invert_kernel,
        out_shape=jax.ShapeDtypeStruct((n_out,), jnp.int32),
        in_specs=[smem, pl.BlockSpec(memory_space=pl.ANY)],
        out_specs=smem,
        scratch_shapes=[pltpu.SemaphoreType.DMA(())],
        name="invert_permutation",
    )(dest, jnp.arange(n_out, dtype=jnp.int32) % dest.shape[0])


def _expert_kernel(ea_ref, eb_ref, chg_ref, nused_ref, src_ref, u2_ref, wga_ref, wua_ref, wda_ref,
                   wgb_ref, wub_ref, wdb_ref, wra_ref, wrb_ref, y_ref, xbuf_ref, xsem_ref, *wbf_refs):
    del ea_ref, eb_ref
    j = pl.program_id(0)
    tm = TM_EXPERT

    @pl.when(chg_ref[j] == 1)
    def _():
        for src, dst in zip((wga_ref, wua_ref, wda_ref, wgb_ref, wub_ref, wdb_ref), wbf_refs):
            dst[...] = src[0, 0].astype(BF16)

    rows, start_next = _gathered_tile(src_ref, u2_ref, xbuf_ref, xsem_ref, tm, issue_first=None)

    @pl.when(j < nused_ref[0])
    def _():
        xb = rows.astype(BF16)
        dl = jnp.sum(xb.astype(F32) * (wra_ref[0] - wrb_ref[0]), axis=-1, keepdims=True)
        w_a = _sigmoid(dl)
        w_b = _sigmoid(-dl)
        issued = iter(range(6))

        def ffn(wg_ref, wu_ref, wd_ref, first):
            gate = _dot(xb, wg_ref[...])
            start_next(part=next(issued), n_parts=6, after=first)
            up = _dot(xb, wu_ref[...])
            start_next(part=next(issued), n_parts=6, after=gate)
            out = _dot((gate * _sigmoid(gate) * up).astype(BF16), wd_ref[...])
            start_next(part=next(issued), n_parts=6, after=up)
            return out

        y_a = ffn(*wbf_refs[0:3], first=None)
        y = w_a * y_a + w_b * ffn(*wbf_refs[3:6], first=y_a)
        _store_token_rows(y_ref, y, tm)

    @pl.when(j >= nused_ref[0])
    def _():
        start_next()
        y_ref[...] = jnp.zeros_like(y_ref)

    _drain_row_gather(u2_ref, xbuf_ref, xsem_ref)


def _experts(layer, ea, eb, chg, n_used, src, u2, w_gate, w_up, w_down, wr_rows):
    tm = TM_EXPERT
    sel_a = lambda j, ea, eb, *_: (layer, ea[j], 0, 0)
    sel_b = lambda j, ea, eb, *_: (layer, eb[j], 0, 0)
    up_spec = lambda sel: pl.BlockSpec((1, 1, D_MODEL, D_EXPERT), sel)
    down_spec = lambda sel: pl.BlockSpec((1, 1, D_EXPERT, D_MODEL), sel)
    wr_spec = lambda sel: pl.BlockSpec((1, 1, D_MODEL), lambda *a: sel(*a)[1:])
    up_scratch = pltpu.VMEM((D_MODEL, D_EXPERT), BF16)
    down_scratch = pltpu.VMEM((D_EXPERT, D_MODEL), BF16)
    return pl.pallas_call(
        _expert_kernel,
        out_shape=jax.ShapeDtypeStruct((PADDED_ROWS * ROW_TILE, LANES), F32),
        grid_spec=pltpu.PrefetchScalarGridSpec(
            num_scalar_prefetch=4,
            grid=(N_EXPERT_TILES,),
            in_specs=[
                pl.BlockSpec(memory_space=pltpu.SMEM),
                pl.BlockSpec(memory_space=pl.ANY),
                up_spec(sel_a), up_spec(sel_a), down_spec(sel_a),
                up_spec(sel_b), up_spec(sel_b), down_spec(sel_b),
                wr_spec(sel_a), wr_spec(sel_b),
            ],
            out_specs=pl.BlockSpec((tm * ROW_TILE, LANES), lambda j, *_: (j, 0)),
            scratch_shapes=[pltpu.VMEM((EXPERT_GATHER_BUFFERS, tm * ROW_TILE, LANES), F32),
                            pltpu.SemaphoreType.DMA((EXPERT_GATHER_BUFFERS,)),
                            up_scratch, up_scratch, down_scratch, up_scratch, up_scratch, down_scratch],
        ),
        compiler_params=_params("arbitrary"),
        name="grouped_experts",
    )(ea, eb, chg, n_used, src, u2, w_gate, w_up, w_down, w_gate, w_up, w_down, wr_rows, wr_rows)


def _final_kernel(dest_ref, x_ref, ys_ref, mod_ref, g_ref, o_ref, ybuf_ref, ysem_ref):
    y = _gathered_tile(dest_ref, ys_ref, ybuf_ref, ysem_ref, x_ref.shape[0], issue_first=True)
    x = x_ref[...] + mod_ref[0][5:6] * y
    o_ref[...] = _rms(x) * g_ref[...]
    _drain_row_gather(ys_ref, ybuf_ref, ysem_ref)


def _final(dest, x, ys, mod, g):
    t = x.shape[0]
    tm = TM_FINAL
    tiles_per_batch = SEQ // tm
    return pl.pallas_call(
        _final_kernel,
        out_shape=jax.ShapeDtypeStruct((t, D_MODEL), F32),
        grid_spec=pltpu.PrefetchScalarGridSpec(
            num_scalar_prefetch=1,
            grid=(t // tm,),
            in_specs=[
                pl.BlockSpec((tm, D_MODEL), lambda i, *_: (i, 0)),
                pl.BlockSpec(memory_space=pl.ANY),
                pl.BlockSpec((1, 6, D_MODEL), lambda i, *_: (i // tiles_per_batch, 0, 0)),
                _const_spec((1, D_MODEL)),
            ],
            out_specs=pl.BlockSpec((tm, D_MODEL), lambda i, *_: (i, 0)),
            scratch_shapes=[pltpu.VMEM((2, tm * ROW_TILE, LANES), F32), pltpu.SemaphoreType.DMA((2,))],
        ),
        compiler_params=_params("arbitrary"),
        name="final_norm",
    )(dest, x, ys, mod, g)


_PAIR_A = np.array([EXPERTS_PER_GROUP * (c // len(PAIRS)) + PAIRS[c % len(PAIRS)][0]
                    for c in range(N_CLASSES)], np.int32)
_PAIR_B = np.array([EXPERTS_PER_GROUP * (c // len(PAIRS)) + PAIRS[c % len(PAIRS)][1]
                    for c in range(N_CLASSES)], np.int32)


def _moe(layer, u2, cls, rank, cnt, w_gate, w_up, w_down, wr_rows):
    counts = cnt[:N_CLASSES, 0]
    tiles = (counts + TM_EXPERT - 1) // TM_EXPERT
    tile_end = jnp.cumsum(tiles)
    row_start = (tile_end - tiles) * TM_EXPERT
    n_used = tile_end[-1:]
    cls = cls.reshape(-1)
    dest = row_start[cls] + rank.reshape(-1)
    tile_cls = jnp.sum(jnp.arange(N_EXPERT_TILES)[:, None] >= tile_end[None, :], axis=1)
    tile_cls = jnp.minimum(tile_cls, tile_cls[jnp.maximum(n_used[0] - 1, 0)])
    ea = jnp.asarray(_PAIR_A)[tile_cls]
    eb = jnp.asarray(_PAIR_B)[tile_cls]
    chg = jnp.concatenate([jnp.ones((1,), jnp.int32),
                           (tile_cls[1:] != tile_cls[:-1]).astype(jnp.int32)])
    src = _invert_permutation(dest, PADDED_ROWS)
    ys = _experts(layer, ea, eb, chg, n_used.astype(jnp.int32), src, u2, w_gate, w_up, w_down, wr_rows)
    return dest, ys


def kernel(x, c, norm1_g, norm2_g, w_ada, b_ada, m_w_in, m_b_gates, m_norm_g, m_w_out,
           c_w_in, c_conv_w, c_conv_b, c_w_out, w_router, b_router,
           e_w_gate, e_w_up, e_w_down, final_g):
    xf = x.reshape(TOKENS, D_MODEL)
    mod = _ada(c, w_ada, b_ada)

    w_in = jnp.concatenate(
        [m_w_in[0].astype(BF16),
         jnp.zeros((D_MODEL, GATE_COLS - 2 * N_HEADS), BF16)], axis=1)
    bg = jnp.zeros((1, GATE_COLS), F32).at[0, :2 * N_HEADS].set(m_b_gates[0])
    wr = jnp.zeros((D_MODEL, ROUTER_COLS), BF16).at[:, :N_EXPERTS].set(w_router.astype(BF16))
    wr_rows = w_router.astype(BF16).astype(F32).T.reshape(N_EXPERTS, 1, D_MODEL)
    br = b_router.reshape(N_EXPERTS, 1)
    row = lambda v: v.reshape(1, -1)

    x1, u2, cls, rank, cnt = _mlstm_layer(xf, mod[0], row(norm1_g[0]), w_in, bg, row(m_norm_g[0]),
                                          m_w_out[0].astype(BF16), row(norm2_g[0]), wr, br)
    dest, ys = _moe(0, u2, cls, rank, cnt, e_w_gate, e_w_up, e_w_down, wr_rows)

    x3, u2, cls, rank, cnt = _conv_layer(
        dest, x1, ys, mod[0], mod[1], row(norm1_g[1]), c_w_in[0].astype(BF16), c_conv_w[0],
        row(c_conv_b[0]), c_w_out[0].astype(BF16), row(norm2_g[1]), wr, br)
    dest, ys = _moe(1, u2, cls, rank, cnt, e_w_gate, e_w_up, e_w_down, wr_rows)

    out = _final(dest, x3, ys, mod[1], row(final_g))
    return out.reshape(BATCH, SEQ, D_MODEL)
```

```python
import functools

import jax
import jax.numpy as jnp
import numpy as np
from jax import lax
from jax.experimental import pallas as pl
from jax.experimental.pallas import tpu as pltpu

F32 = jnp.float32
BF16 = jnp.bfloat16

D_MODEL = 1024
BATCH = 4
SEQ = 8192
TOKENS = BATCH * SEQ
N_HEADS = 4
DH_V = 256
DH_QK = 128
QK = N_HEADS * DH_QK
N_EXPERTS = 16
N_GROUPS = 4
EXPERTS_PER_GROUP = 4
D_EXPERT = 512
EPS = 1e-6

LANES = 128
SUBLANES = 8
VMEM_LIMIT_BYTES = 56 * 1024 * 1024

CHUNK = 128
TM_CONV = 512
TM_EXPERT = 512
GATE_COLS = LANES
GATE_WIDTH = 16
GATE_KINDS = 3
ROUTER_COLS = LANES
PAIRS = ((0, 1), (0, 2), (0, 3), (1, 2), (1, 3), (2, 3))
N_CLASSES = N_GROUPS * len(PAIRS)
CLASS_ROWS = 32
N_EXPERT_TILES = TOKENS // TM_EXPERT + N_CLASSES
PADDED_ROWS = N_EXPERT_TILES * TM_EXPERT
TM_FINAL = 512
EXPERT_GATHER_BUFFERS = 4
PROJ_COLS = 256
PROJ_EVERY = 1


def _params(*semantics):
    return pltpu.CompilerParams(dimension_semantics=semantics, vmem_limit_bytes=VMEM_LIMIT_BYTES)


def _dot(a, b):
    return jnp.dot(a, b, preferred_element_type=F32)


def _rms(x):
    return x * lax.rsqrt(jnp.mean(x * x, axis=-1, keepdims=True) + EPS)


def _sigmoid(x):
    return 1.0 / (1.0 + jnp.exp(-x))


ROW_TILE = D_MODEL // LANES


def _load_token_rows(ref, n):
    return jnp.concatenate([ref[pl.ds(c, n, stride=ROW_TILE), :] for c in range(ROW_TILE)], axis=1)


def _store_token_rows(ref, val, n):
    for c in range(ROW_TILE):
        ref[pl.ds(c, n, stride=ROW_TILE), :] = val[:, c * LANES:(c + 1) * LANES]


def _token_tile(ref, t):
    return ref.at[pl.ds(pl.multiple_of(t * ROW_TILE, ROW_TILE), ROW_TILE)]


def _start_row_gather(idx_ref, base, n, src_hbm, buf, sem):
    for r in range(n):
        pltpu.make_async_copy(_token_tile(src_hbm, idx_ref[base + r]), _token_tile(buf, r),
                              sem).start(priority=r % 2)


def _wait_row_gather(src_hbm, buf, sem):
    pltpu.make_async_copy(src_hbm.at[pl.ds(0, buf.shape[0])], buf, sem).wait()


def _gathered_tile(idx_ref, src_hbm, buf_ref, sem_ref, n, issue_first, n_tiles=None):
    i = pl.program_id(0)
    last = (pl.num_programs(0) if n_tiles is None else n_tiles) - 1
    nb = buf_ref.shape[0]
    ahead = nb - 1

    @pl.when(i == 0)
    def _():
        for k in range(ahead):
            _start_row_gather(idx_ref, jnp.minimum(k, last) * n, n, src_hbm, buf_ref.at[k], sem_ref.at[k])

    nxt = (i + ahead) % nb
    start_next = functools.partial(
        _start_row_gather, idx_ref, jnp.minimum(i + ahead, last) * n, n, src_hbm, buf_ref.at[nxt],
        sem_ref.at[nxt])
    if issue_first:
        start_next()
    slot = i % nb
    _wait_row_gather(src_hbm, buf_ref.at[slot], sem_ref.at[slot])
    rows = _load_token_rows(buf_ref.at[slot], n)
    if issue_first is None:
        return rows, start_next
    if not issue_first:
        start_next()
    return rows


def _drain_row_gather(src_hbm, buf_ref, sem_ref):
    i = pl.program_id(0)
    nb = buf_ref.shape[0]

    @pl.when(i == pl.num_programs(0) - 1)
    def _():
        for k in range(1, nb):
            _wait_row_gather(src_hbm, buf_ref.at[(i + k) % nb], sem_ref.at[(i + k) % nb])


def _ada_kernel(c_ref, w_ref, b_ref, o_ref):
    c = c_ref[...]
    cond = c * _sigmoid(c)
    o_ref[0] = jnp.dot(cond, w_ref[0], preferred_element_type=F32,
                       precision=lax.Precision.HIGHEST) + b_ref[0]


def _ada(c, w_ada, b_ada):
    depth, d, n = w_ada.shape
    tn = 1536
    c8 = jnp.zeros((SUBLANES, d), F32).at[:BATCH].set(c)
    out = pl.pallas_call(
        _ada_kernel,
        out_shape=jax.ShapeDtypeStruct((depth, SUBLANES, n), F32),
        grid=(depth, n // tn),
        in_specs=[
            pl.BlockSpec((SUBLANES, d), lambda l, j: (0, 0)),
            pl.BlockSpec((1, d, tn), lambda l, j: (l, 0, j)),
            pl.BlockSpec((1, 1, tn), lambda l, j: (l, 0, j)),
        ],
        out_specs=pl.BlockSpec((1, SUBLANES, tn), lambda l, j: (l, 0, j)),
        compiler_params=_params("arbitrary", "arbitrary"),
        name="ada_mod",
    )(c8, w_ada, b_ada.reshape(depth, 1, n))
    return out[:, :BATCH].reshape(depth, BATCH, 6, d)


def _mlstm_layer_kernel(xn_ref, x_ref, mod_ref, g1_ref, wi_ref, bg_ref, ng_ref, rep_ref,
                        wo_ref, g2_ref, wr_ref, br_ref, tri_ref,
                        x1_ref, u2_ref, cls_ref, rank_ref, cnt_ref,
                        q_ref, k_ref, v_ref, og_ref, gcol_ref, grow_ref, m_ref, h_ref, run_ref, *c_refs):
    step = pl.program_id(0)

    @pl.when(step == 0)
    def _():
        for ref in (q_ref, k_ref, v_ref, og_ref, gcol_ref, grow_ref, m_ref, h_ref, run_ref) + c_refs:
            ref[...] = jnp.zeros_like(ref)

    h_prev = h_ref[...]
    ln = CHUNK
    row = lax.broadcasted_iota(jnp.int32, (ln, ln), 0)
    col = lax.broadcasted_iota(jnp.int32, (ln, ln), 1)
    causal = col <= row
    ones = jnp.ones((ln, LANES), BF16)
    n_streams = BATCH * N_HEADS
    m_all = [m_ref[st] for st in range(n_streams)]
    m_out = [None] * n_streams

    reps = {}

    def replicated_gates(bi):
        if bi not in reps:
            g = gcol_ref[bi]
            g_hi = g.astype(BF16)
            g_r1 = g - g_hi.astype(F32)
            g_mid = g_r1.astype(BF16)
            g_lo = (g_r1 - g_mid.astype(F32)).astype(BF16)
            reps[bi] = _dot(g_hi, rep_ref[...]) + _dot(g_mid, rep_ref[...]) + _dot(g_lo, rep_ref[...])
        return reps[bi]

    def stream(st):
        bi, h = divmod(st, N_HEADS)
        rep = replicated_gates(bi)
        lanes = lambda kind: slice((kind * N_HEADS + h) * LANES, (kind * N_HEADS + h + 1) * LANES)
        cm, b, imb_col = rep[:, lanes(0)], rep[:, lanes(1)], rep[:, lanes(2)]
        imb_row = grow_ref[bi, h:h + 1, :]
        qh = q_ref[bi, :, h * DH_QK:(h + 1) * DH_QK]
        kh = k_ref[bi, :, h * DH_QK:(h + 1) * DH_QK]
        v_ext = jnp.concatenate([v_ref[bi, :, h * DH_V:(h + 1) * DH_V], ones], axis=1)
        c_ref = c_refs[st]
        c_prev = c_ref[...]
        m_prev = m_all[st]
        big_m = jnp.maximum(m_prev, cm)
        d_mat = jnp.exp(jnp.where(causal, imb_row - big_m, -jnp.inf))
        s_raw = lax.dot_general(qh, kh, (((1,), (1,)), ((), ())), preferred_element_type=F32)
        yield
        q_inter = (qh.astype(F32) * jnp.exp(m_prev - big_m)).astype(BF16)
        lhs = jnp.concatenate([q_inter, (s_raw * d_mat).astype(BF16)], axis=1)
        rhs = jnp.concatenate([c_prev.astype(BF16), v_ext], axis=0)
        nd = _dot(lhs, rhs)
        m_last = big_m[ln - 1:ln, :]
        kw = (kh.astype(F32) * jnp.exp(imb_col - m_last)).astype(BF16)
        update = lax.dot_general(kw, v_ext, (((0,), (0,)), ((), ())), preferred_element_type=F32)
        yield
        decay = jnp.exp(m_prev - m_last)
        c_ref[...] = jnp.concatenate([decay] * 3, axis=1) * c_prev + update
        m_out[st] = b[ln - 1:ln, :] + m_last
        den = nd[:, DH_V:]
        inv = 1.0 / jnp.maximum(jnp.abs(den), jnp.exp(-(b + big_m)))
        hh = nd[:, :DH_V] * jnp.concatenate([inv, inv], axis=1)
        sl = slice(h * DH_V, (h + 1) * DH_V)
        gate = _sigmoid(og_ref[bi, :, sl].astype(F32))
        h_ref[bi * ln:(bi + 1) * ln, sl] = (_rms(hh) * ng_ref[:, sl] * gate).astype(BF16)

    deferred_stores = []

    def projection():
        mod = mod_ref[...]
        u = jnp.concatenate(
            [_rms(xn_ref[bi]) * g1_ref[...] * (1.0 + mod[bi][1:2]) + mod[bi][0:1] for bi in range(BATCH)],
            axis=0).astype(BF16)
        gt = _dot(u, wi_ref[:, 2 * QK + 2 * D_MODEL:]) + bg_ref[...]
        cols, imb = _gate_columns(gt)
        imb_rows = imb.T[0:SUBLANES, :]
        for bi in range(BATCH):
            gcol_ref[bi] = cols[bi * ln:(bi + 1) * ln, 0:GATE_WIDTH]
            grow_ref[bi] = imb_rows[:, bi * ln:(bi + 1) * ln]
        yield
        lo = 0
        for ref, width, scale in ((q_ref, QK, DH_QK ** -0.5), (k_ref, QK, None), (v_ref, D_MODEL, None),
                                  (og_ref, D_MODEL, None)):
            for off in range(0, width, PROJ_COLS):
                part = _dot(u, wi_ref[:, lo + off:lo + off + PROJ_COLS])
                if scale is not None:
                    part = part * scale
                part = part.astype(BF16)

                def store(ref=ref, off=off, part=part):
                    for bi in range(BATCH):
                        ref[bi, :, off:off + PROJ_COLS] = part[bi * ln:(bi + 1) * ln]

                if ref is og_ref:
                    deferred_stores.append(store)
                else:
                    store()
                yield
            lo += width

    def previous_chunk():
        half = D_MODEL // 2
        mix0 = _dot(h_prev, wo_ref[:, 0:half])
        yield
        mix1 = _dot(h_prev, wo_ref[:, half:])
        yield
        mod = mod_ref[...]
        u2_rows = []
        for bi in range(BATCH):
            mb = mod[bi]
            rows = slice(bi * ln, (bi + 1) * ln)
            x1 = x_ref[bi] + mb[2:3] * jnp.concatenate([mix0[rows], mix1[rows]], axis=1)
            x1_ref[bi] = x1
            u2_b = _moe_input(x1, mb, g2_ref)
            _store_token_rows(u2_ref.at[bi], u2_b, ln)
            u2_rows.append(u2_b)
        valid = jnp.where(step > 1, 1.0, 0.0)
        yield from _route_stages(jnp.concatenate(u2_rows, axis=0), valid, wr_ref, br_ref, tri_ref,
                                 cls_ref, rank_ref, cnt_ref, run_ref)

    _round_robin([previous_chunk()] + [stream(st) for st in range(n_streams)],
                 background=projection(), every=PROJ_EVERY, start_after=n_streams + 1)
    for store in deferred_stores:
        store()
    for st in range(n_streams):
        m_ref[st] = m_out[st]


def _gate_columns(gt):
    lane = lax.broadcasted_iota(jnp.int32, gt.shape, 1)
    pos = lax.broadcasted_iota(jnp.int32, gt.shape, 0) % CHUNK
    b = jnp.minimum(gt, 0.0) - jnp.log(1.0 + jnp.exp(-jnp.abs(gt)))
    shift = 1
    while shift < CHUNK:
        b = b + jnp.where(pos >= shift, pltpu.roll(b, shift, axis=0), 0.0)
        shift *= 2
    imb = gt - pltpu.roll(b, LANES - N_HEADS, axis=1)
    cm = imb
    shift = 1
    while shift < CHUNK:
        cm = jnp.maximum(cm, jnp.where(pos >= shift, pltpu.roll(cm, shift, axis=0), -jnp.inf))
        shift *= 2
    cols = jnp.where(lane < N_HEADS, cm,
                     jnp.where(lane < 2 * N_HEADS, b,
                               jnp.where(lane < 3 * N_HEADS, pltpu.roll(imb, 2 * N_HEADS, axis=1), 0.0)))
    return cols, imb


def _mlstm_layer(x, mod, g1, w_in, bg, norm_g, w_out, g2, wr, br):
    nc = SEQ // CHUNK
    n_streams = BATCH * N_HEADS
    rows = BATCH * CHUNK
    n_rep = GATE_KINDS * N_HEADS
    replicate = (jnp.arange(GATE_WIDTH)[:, None] == jnp.arange(n_rep * LANES)[None, :] // LANES
                 ).astype(BF16)
    newest = lambda c: jnp.minimum(c, nc - 1)
    oldest = lambda c: jnp.maximum(c - 2, 0)
    old_rows = lambda height, width: pl.BlockSpec((BATCH, height, width), lambda c: (0, oldest(c), 0))
    x3d = x.reshape(BATCH, SEQ, D_MODEL)
    x1, u2, cls, rank, cnt = pl.pallas_call(
        _mlstm_layer_kernel,
        out_shape=(
            jax.ShapeDtypeStruct((BATCH, SEQ, D_MODEL), F32),
            jax.ShapeDtypeStruct((BATCH, SEQ * ROW_TILE, LANES), F32),
            jax.ShapeDtypeStruct((nc, 1, rows), jnp.int32),
            jax.ShapeDtypeStruct((nc, 1, rows), jnp.int32),
            jax.ShapeDtypeStruct((CLASS_ROWS, LANES), jnp.int32),
        ),
        grid=(nc + 2,),
        in_specs=[
            pl.BlockSpec((BATCH, CHUNK, D_MODEL), lambda c: (0, newest(c), 0)),
            old_rows(CHUNK, D_MODEL),
            _const_spec((BATCH, 6, D_MODEL)),
            _const_spec((1, D_MODEL)),
            _const_spec(w_in.shape),
            _const_spec((1, GATE_COLS)),
            _const_spec((1, D_MODEL)),
            _const_spec((GATE_WIDTH, n_rep * LANES)),
            _const_spec((D_MODEL, D_MODEL)),
            _const_spec((1, D_MODEL)),
            _const_spec((D_MODEL, ROUTER_COLS)),
            _const_spec((N_EXPERTS, 1)),
            _const_spec((rows, rows)),
        ],
        out_specs=(
            old_rows(CHUNK, D_MODEL),
            old_rows(CHUNK * ROW_TILE, LANES),
            pl.BlockSpec((1, 1, rows), lambda c: (oldest(c), 0, 0)),
            pl.BlockSpec((1, 1, rows), lambda c: (oldest(c), 0, 0)),
            _const_spec((CLASS_ROWS, LANES)),
        ),
        scratch_shapes=[pltpu.VMEM((BATCH, CHUNK, QK), BF16),
                        pltpu.VMEM((BATCH, CHUNK, QK), BF16),
                        pltpu.VMEM((BATCH, CHUNK, D_MODEL), BF16),
                        pltpu.VMEM((BATCH, CHUNK, D_MODEL), BF16),
                        pltpu.VMEM((BATCH, CHUNK, GATE_WIDTH), F32),
                        pltpu.VMEM((BATCH, SUBLANES, CHUNK), F32),
                        pltpu.VMEM((n_streams, 1, LANES), F32),
                        pltpu.VMEM((rows, D_MODEL), BF16),
                        pltpu.VMEM((CLASS_ROWS, LANES), F32)]
        + [pltpu.VMEM((DH_QK, DH_V + LANES), F32) for _ in range(n_streams)],
        compiler_params=_params("arbitrary"),
        name="mlstm_layer",
    )(x3d, x3d, mod, g1, w_in, bg, norm_g, replicate, w_out, g2, wr, br, _strict_upper(rows))
    token_order = lambda a: a.reshape(nc, BATCH, CHUNK).transpose(1, 0, 2).reshape(TOKENS)
    return (x1.reshape(TOKENS, D_MODEL), u2.reshape(TOKENS * ROW_TILE, LANES), token_order(cls),
            token_order(rank), cnt)


def _top2_sum(v0, v1, v2, v3):
    hi1, lo1 = jnp.maximum(v0, v1), jnp.minimum(v0, v1)
    hi2, lo2 = jnp.maximum(v2, v3), jnp.minimum(v2, v3)
    return jnp.maximum(hi1, hi2) + jnp.maximum(jnp.minimum(hi1, hi2), jnp.maximum(lo1, lo2))


def _route_tail(x_new, m, valid, g2_ref, wr_ref, br_ref, tri_ref, u2_ref, cls_ref, rank_ref, cnt_ref,
                run_ref, after_router=None):
    u2 = _moe_input(x_new, m, g2_ref)
    _store_token_rows(u2_ref, u2, u2.shape[0])
    _route(u2, valid, wr_ref, br_ref, tri_ref, cls_ref, rank_ref, cnt_ref, run_ref, after_router)


def _moe_input(x_new, m, g2_ref):
    return _rms(x_new) * g2_ref[...] * (1.0 + m[4:5]) + m[3:4]


def _round_robin(chains, background=None, every=1, start_after=0):
    chains = list(chains)
    advanced = 0
    while chains:
        alive = []
        for chain in chains:
            try:
                next(chain)
            except StopIteration:
                continue
            alive.append(chain)
            advanced += 1
            if background is not None and advanced >= start_after and advanced % every == 0:
                next(background, None)
        chains = alive
    if background is not None:
        for _ in background:
            pass


def _route(u2, valid, wr_ref, br_ref, tri_ref, cls_ref, rank_ref, cnt_ref, run_ref, after_router=None):
    stages = _route_stages(u2, valid, wr_ref, br_ref, tri_ref, cls_ref, rank_ref, cnt_ref, run_ref)
    next(stages)
    if after_router is not None:
        after_router()
    for _ in stages:
        pass


def _route_stages(u2, valid, wr_ref, br_ref, tri_ref, cls_ref, rank_ref, cnt_ref, run_ref):
    logits = _dot(u2.astype(BF16), wr_ref[...])
    yield
    lt = logits.T[0:N_EXPERTS, :]
    e = jnp.exp(lt - jnp.max(lt, axis=0, keepdims=True))
    probs = e / jnp.sum(e, axis=0, keepdims=True)
    sel = probs + br_ref[...]
    sel_rows = [sel[j:j + 1, :] for j in range(N_EXPERTS)]
    best = jnp.zeros_like(sel_rows[0], dtype=jnp.int32)
    best_score = _top2_sum(*sel_rows[0:EXPERTS_PER_GROUP])
    for g in range(1, N_GROUPS):
        score = _top2_sum(*sel_rows[g * EXPERTS_PER_GROUP:(g + 1) * EXPERTS_PER_GROUP])
        better = score > best_score
        best = jnp.where(better, g, best)
        best_score = jnp.where(better, score, best_score)
    s = []
    for j in range(EXPERTS_PER_GROUP):
        sj = sel_rows[j]
        for g in range(1, N_GROUPS):
            sj = jnp.where(best == g, sel_rows[g * EXPERTS_PER_GROUP + j], sj)
        s.append(sj)
    chosen = []
    for j in range(EXPERTS_PER_GROUP):
        beaten = jnp.zeros_like(best)
        for i in range(EXPERTS_PER_GROUP):
            if i == j:
                continue
            wins = (s[i] >= s[j]) if i < j else (s[i] > s[j])
            beaten = beaten + jnp.where(wins, 1, 0)
        chosen.append(beaten < 2)
    pair = jnp.full_like(best, len(PAIRS) - 1)
    for p in range(len(PAIRS) - 2, -1, -1):
        a, b = PAIRS[p]
        pair = jnp.where(jnp.logical_and(chosen[a], chosen[b]), p, pair)
    cls = best * len(PAIRS) + pair
    cls_ref[0] = cls

    class_id = lax.broadcasted_iota(jnp.int32, (CLASS_ROWS, cls.shape[1]), 0)
    onehot = class_id == cls
    before = _dot(jnp.where(onehot, 1.0, 0.0).astype(BF16), tri_ref[...])
    run = run_ref[...]
    rank = jnp.sum(jnp.where(onehot, before + run[:, 0:1], 0.0), axis=0, keepdims=True)
    rank_ref[0] = rank.astype(jnp.int32)
    run = run + valid * jnp.sum(jnp.where(onehot, 1.0, 0.0), axis=1, keepdims=True)
    run_ref[...] = run
    cnt_ref[...] = run.astype(jnp.int32)


def _strict_upper(n):
    return (jnp.arange(n)[:, None] < jnp.arange(n)[None, :]).astype(BF16)


def _route_out_shapes(t, tm):
    return (
        jax.ShapeDtypeStruct((t, D_MODEL), F32),
        jax.ShapeDtypeStruct((t * ROW_TILE, LANES), F32),
        jax.ShapeDtypeStruct((t // tm, 1, tm), jnp.int32),
        jax.ShapeDtypeStruct((t // tm, 1, tm), jnp.int32),
        jax.ShapeDtypeStruct((CLASS_ROWS, LANES), jnp.int32),
    )


def _cur_tile(n_tiles):
    return lambda i: jnp.minimum(i, n_tiles - 1)


def _prev_tile(i):
    return jnp.maximum(i - 1, 0)


def _route_out_specs(tm, n_tiles):
    cur = _cur_tile(n_tiles)
    return (
        pl.BlockSpec((tm, D_MODEL), lambda i, *_: (cur(i), 0)),
        pl.BlockSpec((tm * ROW_TILE, LANES), lambda i, *_: (_prev_tile(i), 0)),
        pl.BlockSpec((1, 1, tm), lambda i, *_: (_prev_tile(i), 0, 0)),
        pl.BlockSpec((1, 1, tm), lambda i, *_: (_prev_tile(i), 0, 0)),
        pl.BlockSpec((CLASS_ROWS, LANES), lambda i, *_: (0, 0)),
    )


def _init_deferred_tail(xprev_ref, run_ref):
    @pl.when(pl.program_id(0) == 0)
    def _():
        xprev_ref[...] = jnp.zeros_like(xprev_ref)
        run_ref[...] = jnp.zeros_like(run_ref)


def _deferred_tail(xprev_ref, modp_ref, *tail_refs, after_router=None):
    valid = jnp.where(pl.program_id(0) > 0, 1.0, 0.0)
    _route_tail(xprev_ref[...], modp_ref[0], valid, *tail_refs, after_router=after_router)


def _const_spec(shape):
    return pl.BlockSpec(shape, lambda i, *_: (0,) * len(shape))


def _conv_layer_kernel(dest_ref, x_ref, ys_ref, mod0_ref, mod_ref, modp_ref, g1_ref, wi_ref, cw_ref,
                       cb_ref, wo_ref, g2_ref, wr_ref, br_ref, tri_ref,
                       x3_ref, u2_ref, cls_ref, rank_ref, cnt_ref,
                       run_ref, carry_ref, ybuf_ref, ysem_ref, xprev_ref):
    tm = x_ref.shape[0]
    tiles_per_batch = SEQ // tm
    n_tiles = pl.num_programs(0) - 1
    i = pl.program_id(0)

    @pl.when(i % tiles_per_batch == 0)
    def _():
        carry_ref[...] = jnp.zeros_like(carry_ref)

    _init_deferred_tail(xprev_ref, run_ref)
    m = mod_ref[0]
    y = _gathered_tile(dest_ref, ys_ref, ybuf_ref, ysem_ref, tm, issue_first=False, n_tiles=n_tiles)
    x2 = x_ref[...] + mod0_ref[0][5:6] * y
    u = (_rms(x2) * g1_ref[...] * (1.0 + m[1:2]) + m[0:1]).astype(BF16)
    bgate = _dot(u, wi_ref[:, 0:D_MODEL])
    gated = []

    def project_gated():
        gated.append(_dot(u, wi_ref[:, D_MODEL:2 * D_MODEL]) * _dot(u, wi_ref[:, 2 * D_MODEL:]))

    _deferred_tail(xprev_ref, modp_ref, g2_ref, wr_ref, br_ref, tri_ref, u2_ref, cls_ref, rank_ref,
                   cnt_ref, run_ref, after_router=project_gated)
    z = gated[0]
    prev = carry_ref[...]
    row = lax.broadcasted_iota(jnp.int32, z.shape, 0)
    z1 = jnp.where(row == 0, prev[7:8], pltpu.roll(z, 1, axis=0))
    z2 = jnp.where(row == 0, prev[6:7], jnp.where(row == 1, prev[7:8], pltpu.roll(z, 2, axis=0)))
    carry_ref[...] = z[tm - SUBLANES:, :]
    cw = cw_ref[...]
    zc = cw[0:1] * z2 + cw[1:2] * z1 + cw[2:3] * z + cb_ref[...]
    x3 = x2 + m[2:3] * _dot((bgate * zc).astype(BF16), wo_ref[...])
    xprev_ref[...] = x3

    @pl.when(i < n_tiles)
    def _():
        x3_ref[...] = x3

    _drain_row_gather(ys_ref, ybuf_ref, ysem_ref)


def _conv_layer(dest, x, ys, mod0, mod, g1, w_in, conv_w, conv_b, w_out, g2, wr, br):
    t = x.shape[0]
    tm = TM_CONV
    tri = _strict_upper(tm)
    tiles_per_batch = SEQ // tm
    n_tiles = t // tm
    cur = _cur_tile(n_tiles)
    mod_spec = pl.BlockSpec((1, 6, D_MODEL), lambda i, *_: (cur(i) // tiles_per_batch, 0, 0))
    modp_spec = pl.BlockSpec((1, 6, D_MODEL), lambda i, *_: (_prev_tile(i) // tiles_per_batch, 0, 0))
    return pl.pallas_call(
        _conv_layer_kernel,
        out_shape=_route_out_shapes(t, tm),
        grid_spec=pltpu.PrefetchScalarGridSpec(
            num_scalar_prefetch=1,
            grid=(n_tiles + 1,),
            in_specs=[
                pl.BlockSpec((tm, D_MODEL), lambda i, *_: (cur(i), 0)),
                pl.BlockSpec(memory_space=pl.ANY),
                mod_spec,
                mod_spec,
                modp_spec,
                _const_spec((1, D_MODEL)),
                _const_spec((D_MODEL, 3 * D_MODEL)),
                _const_spec((3, D_MODEL)),
                _const_spec((1, D_MODEL)),
                _const_spec((D_MODEL, D_MODEL)),
                _const_spec((1, D_MODEL)),
                _const_spec((D_MODEL, ROUTER_COLS)),
                _const_spec((N_EXPERTS, 1)),
                _const_spec((tm, tm)),
            ],
            out_specs=_route_out_specs(tm, n_tiles),
            scratch_shapes=[
                pltpu.VMEM((CLASS_ROWS, LANES), F32),
                pltpu.VMEM((SUBLANES, D_MODEL), F32),
                pltpu.VMEM((2, tm * ROW_TILE, LANES), F32),
                pltpu.SemaphoreType.DMA((2,)),
                pltpu.VMEM((tm, D_MODEL), F32),
            ],
        ),
        compiler_params=_params("arbitrary"),
        name="conv_layer_route",
    )(dest, x, ys, mod0, mod, mod, g1, w_in, conv_w, conv_b, w_out, g2, wr, br, tri)


def _invert_kernel(dest_ref, fill_ref, src_ref, sem):
    fill = pltpu.make_async_copy(fill_ref, src_ref, sem)
    fill.start()
    fill.wait()
    unroll = 8

    def body(t8, carry):
        for k in range(unroll):
            t = t8 * unroll + k
            src_ref[dest_ref[t]] = t
        return carry

    lax.fori_loop(0, dest_ref.shape[0] // unroll, body, 0)


def _invert_permutation(dest, n_out):
    smem = pl.BlockSpec(memory_space=pltpu.SMEM)
    return pl.pallas_call(
        _invert_kernel,
        out_shape=jax.ShapeDtypeStruct((n_out,), jnp.int32),
        in_specs=[smem, pl.BlockSpec(memory_space=pl.ANY)],
        out_specs=smem,
        scratch_shapes=[pltpu.SemaphoreType.DMA(())],
        name="invert_permutation",
    )(dest, jnp.arange(n_out, dtype=jnp.int32) % dest.shape[0])


def _expert_kernel(ea_ref, eb_ref, chg_ref, nused_ref, src_ref, u2_ref, wga_ref, wua_ref, wda_ref,
                   wgb_ref, wub_ref, wdb_ref, wra_ref, wrb_ref, y_ref, xbuf_ref, xsem_ref, *wbf_refs):
    del ea_ref, eb_ref
    j = pl.program_id(0)
    tm = TM_EXPERT

    @pl.when(chg_ref[j] == 1)
    def _():
        for src, dst in zip((wga_ref, wua_ref, wda_ref, wgb_ref, wub_ref, wdb_ref), wbf_refs):
            dst[...] = src[0, 0].astype(BF16)

    rows, start_next = _gathered_tile(src_ref, u2_ref, xbuf_ref, xsem_ref, tm, issue_first=None)

    @pl.when(j < nused_ref[0])
    def _():
        start_next()
        xb = rows.astype(BF16)
        dl = jnp.sum(xb.astype(F32) * (wra_ref[0] - wrb_ref[0]), axis=-1, keepdims=True)
        w_a = _sigmoid(dl)
        w_b = _sigmoid(-dl)

        def ffn(wg_ref, wu_ref, wd_ref):
            gate = _dot(xb, wg_ref[...])
            hidden = gate * _sigmoid(gate) * _dot(xb, wu_ref[...])
            return _dot(hidden.astype(BF16), wd_ref[...])

        y = w_a * ffn(*wbf_refs[0:3]) + w_b * ffn(*wbf_refs[3:6])
        _store_token_rows(y_ref, y, tm)

    @pl.when(j >= nused_ref[0])
    def _():
        start_next()
        y_ref[...] = jnp.zeros_like(y_ref)

    _drain_row_gather(u2_ref, xbuf_ref, xsem_ref)


def _experts(layer, ea, eb, chg, n_used, src, u2, w_gate, w_up, w_down, wr_rows):
    tm = TM_EXPERT
    sel_a = lambda j, ea, eb, *_: (layer, ea[j], 0, 0)
    sel_b = lambda j, ea, eb, *_: (layer, eb[j], 0, 0)
    up_spec = lambda sel: pl.BlockSpec((1, 1, D_MODEL, D_EXPERT), sel)
    down_spec = lambda sel: pl.BlockSpec((1, 1, D_EXPERT, D_MODEL), sel)
    wr_spec = lambda sel: pl.BlockSpec((1, 1, D_MODEL), lambda *a: sel(*a)[1:])
    up_scratch = pltpu.VMEM((D_MODEL, D_EXPERT), BF16)
    down_scratch = pltpu.VMEM((D_EXPERT, D_MODEL), BF16)
    return pl.pallas_call(
        _expert_kernel,
        out_shape=jax.ShapeDtypeStruct((PADDED_ROWS * ROW_TILE, LANES), F32),
        grid_spec=pltpu.PrefetchScalarGridSpec(
            num_scalar_prefetch=4,
            grid=(N_EXPERT_TILES,),
            in_specs=[
                pl.BlockSpec(memory_space=pltpu.SMEM),
                pl.BlockSpec(memory_space=pl.ANY),
                up_spec(sel_a), up_spec(sel_a), down_spec(sel_a),
                up_spec(sel_b), up_spec(sel_b), down_spec(sel_b),
                wr_spec(sel_a), wr_spec(sel_b),
            ],
            out_specs=pl.BlockSpec((tm * ROW_TILE, LANES), lambda j, *_: (j, 0)),
            scratch_shapes=[pltpu.VMEM((EXPERT_GATHER_BUFFERS, tm * ROW_TILE, LANES), F32),
                            pltpu.SemaphoreType.DMA((EXPERT_GATHER_BUFFERS,)),
                            up_scratch, up_scratch, down_scratch, up_scratch, up_scratch, down_scratch],
        ),
        compiler_params=_params("arbitrary"),
        name="grouped_experts",
    )(ea, eb, chg, n_used, src, u2, w_gate, w_up, w_down, w_gate, w_up, w_down, wr_rows, wr_rows)


def _final_kernel(dest_ref, x_ref, ys_ref, mod_ref, g_ref, o_ref, ybuf_ref, ysem_ref):
    y = _gathered_tile(dest_ref, ys_ref, ybuf_ref, ysem_ref, x_ref.shape[0], issue_first=True)
    x = x_ref[...] + mod_ref[0][5:6] * y
    o_ref[...] = _rms(x) * g_ref[...]
    _drain_row_gather(ys_ref, ybuf_ref, ysem_ref)


def _final(dest, x, ys, mod, g):
    t = x.shape[0]
    tm = TM_FINAL
    tiles_per_batch = SEQ // tm
    return pl.pallas_call(
        _final_kernel,
        out_shape=jax.ShapeDtypeStruct((t, D_MODEL), F32),
        grid_spec=pltpu.PrefetchScalarGridSpec(
            num_scalar_prefetch=1,
            grid=(t // tm,),
            in_specs=[
                pl.BlockSpec((tm, D_MODEL), lambda i, *_: (i, 0)),
                pl.BlockSpec(memory_space=pl.ANY),
                pl.BlockSpec((1, 6, D_MODEL), lambda i, *_: (i // tiles_per_batch, 0, 0)),
                _const_spec((1, D_MODEL)),
            ],
            out_specs=pl.BlockSpec((tm, D_MODEL), lambda i, *_: (i, 0)),
            scratch_shapes=[pltpu.VMEM((2, tm * ROW_TILE, LANES), F32), pltpu.SemaphoreType.DMA((2,))],
        ),
        compiler_params=_params("arbitrary"),
        name="final_norm",
    )(dest, x, ys, mod, g)


_PAIR_A = np.array([EXPERTS_PER_GROUP * (c // len(PAIRS)) + PAIRS[c % len(PAIRS)][0]
                    for c in range(N_CLASSES)], np.int32)
_PAIR_B = np.array([EXPERTS_PER_GROUP * (c // len(PAIRS)) + PAIRS[c % len(PAIRS)][1]
                    for c in range(N_CLASSES)], np.int32)


def _moe(layer, u2, cls, rank, cnt, w_gate, w_up, w_down, wr_rows):
    counts = cnt[:N_CLASSES, 0]
    tiles = (counts + TM_EXPERT - 1) // TM_EXPERT
    tile_end = jnp.cumsum(tiles)
    row_start = (tile_end - tiles) * TM_EXPERT
    n_used = tile_end[-1:]
    cls = cls.reshape(-1, LANES)
    dest = rank.reshape(-1, LANES)
    for c in range(N_CLASSES):
        dest = dest + jnp.where(cls == c, row_start[c], 0)
    dest = dest.reshape(-1)
    tile_cls = jnp.sum(jnp.arange(N_EXPERT_TILES)[:, None] >= tile_end[None, :], axis=1)
    tile_cls = jnp.minimum(tile_cls, tile_cls[jnp.maximum(n_used[0] - 1, 0)])
    ea = jnp.asarray(_PAIR_A)[tile_cls]
    eb = jnp.asarray(_PAIR_B)[tile_cls]
    chg = jnp.concatenate([jnp.ones((1,), jnp.int32),
                           (tile_cls[1:] != tile_cls[:-1]).astype(jnp.int32)])
    src = _invert_permutation(dest, PADDED_ROWS)
    ys = _experts(layer, ea, eb, chg, n_used.astype(jnp.int32), src, u2, w_gate, w_up, w_down, wr_rows)
    return dest, ys


def kernel(x, c, norm1_g, norm2_g, w_ada, b_ada, m_w_in, m_b_gates, m_norm_g, m_w_out,
           c_w_in, c_conv_w, c_conv_b, c_w_out, w_router, b_router,
           e_w_gate, e_w_up, e_w_down, final_g):
    xf = x.reshape(TOKENS, D_MODEL)
    mod = _ada(c, w_ada, b_ada)

    w_in = jnp.concatenate(
        [m_w_in[0].astype(BF16),
         jnp.zeros((D_MODEL, GATE_COLS - 2 * N_HEADS), BF16)], axis=1)
    bg = jnp.zeros((1, GATE_COLS), F32).at[0, :2 * N_HEADS].set(m_b_gates[0])
    wr = jnp.zeros((D_MODEL, ROUTER_COLS), BF16).at[:, :N_EXPERTS].set(w_router.astype(BF16))
    wr_rows = w_router.astype(BF16).astype(F32).T.reshape(N_EXPERTS, 1, D_MODEL)
    br = b_router.reshape(N_EXPERTS, 1)
    row = lambda v: v.reshape(1, -1)

    x1, u2, cls, rank, cnt = _mlstm_layer(xf, mod[0], row(norm1_g[0]), w_in, bg, row(m_norm_g[0]),
                                          m_w_out[0].astype(BF16), row(norm2_g[0]), wr, br)
    dest, ys = _moe(0, u2, cls, rank, cnt, e_w_gate, e_w_up, e_w_down, wr_rows)

    x3, u2, cls, rank, cnt = _conv_layer(
        dest, x1, ys, mod[0], mod[1], row(norm1_g[1]), c_w_in[0].astype(BF16), c_conv_w[0],
        row(c_conv_b[0]), c_w_out[0].astype(BF16), row(norm2_g[1]), wr, br)
    dest, ys = _moe(1, u2, cls, rank, cnt, e_w_gate, e_w_up, e_w_down, wr_rows)

    out = _final(dest, x3, ys, mod[1], row(final_g))
    return out.reshape(BATCH, SEQ, D_MODEL)
```

```python
import functools

import jax
import jax.numpy as jnp
import numpy as np
from jax import lax
from jax.experimental import pallas as pl
from jax.experimental.pallas import tpu as pltpu

F32 = jnp.float32
BF16 = jnp.bfloat16

D_MODEL = 1024
BATCH = 4
SEQ = 8192
TOKENS = BATCH * SEQ
N_HEADS = 4
DH_V = 256
DH_QK = 128
QK = N_HEADS * DH_QK
N_EXPERTS = 16
N_GROUPS = 4
EXPERTS_PER_GROUP = 4
D_EXPERT = 512
EPS = 1e-6

LANES = 128
SUBLANES = 8
VMEM_LIMIT_BYTES = 56 * 1024 * 1024

CHUNK = 128
TM_CONV = 512
TM_EXPERT = 512
GATE_COLS = LANES
GATE_WIDTH = 16
ROUTER_COLS = LANES
PAIRS = ((0, 1), (0, 2), (0, 3), (1, 2), (1, 3), (2, 3))
N_CLASSES = N_GROUPS * len(PAIRS)
CLASS_ROWS = 32
N_EXPERT_TILES = TOKENS // TM_EXPERT + N_CLASSES
PADDED_ROWS = N_EXPERT_TILES * TM_EXPERT
TM_FINAL = 512
EXPERT_GATHER_BUFFERS = 4
PROJ_COLS = 256
PROJ_EVERY = 1


def _params(*semantics):
    return pltpu.CompilerParams(dimension_semantics=semantics, vmem_limit_bytes=VMEM_LIMIT_BYTES)


def _dot(a, b):
    return jnp.dot(a, b, preferred_element_type=F32)


def _rms(x):
    return x * lax.rsqrt(jnp.mean(x * x, axis=-1, keepdims=True) + EPS)


def _sigmoid(x):
    return 1.0 / (1.0 + jnp.exp(-x))


ROW_TILE = D_MODEL // LANES


def _load_token_rows(ref, n):
    return jnp.concatenate([ref[pl.ds(c, n, stride=ROW_TILE), :] for c in range(ROW_TILE)], axis=1)


def _store_token_rows(ref, val, n):
    for c in range(ROW_TILE):
        ref[pl.ds(c, n, stride=ROW_TILE), :] = val[:, c * LANES:(c + 1) * LANES]


def _token_tile(ref, t):
    return ref.at[pl.ds(pl.multiple_of(t * ROW_TILE, ROW_TILE), ROW_TILE)]


def _start_row_gather(idx_ref, base, n, src_hbm, buf, sem):
    for r in range(n):
        pltpu.make_async_copy(_token_tile(src_hbm, idx_ref[base + r]), _token_tile(buf, r),
                              sem).start(priority=r % 2)


def _wait_row_gather(src_hbm, buf, sem):
    pltpu.make_async_copy(src_hbm.at[pl.ds(0, buf.shape[0])], buf, sem).wait()


def _gathered_tile(idx_ref, src_hbm, buf_ref, sem_ref, n, issue_first, n_tiles=None):
    i = pl.program_id(0)
    last = (pl.num_programs(0) if n_tiles is None else n_tiles) - 1
    nb = buf_ref.shape[0]
    ahead = nb - 1

    @pl.when(i == 0)
    def _():
        for k in range(ahead):
            _start_row_gather(idx_ref, jnp.minimum(k, last) * n, n, src_hbm, buf_ref.at[k], sem_ref.at[k])

    nxt = (i + ahead) % nb
    start_next = functools.partial(
        _start_row_gather, idx_ref, jnp.minimum(i + ahead, last) * n, n, src_hbm, buf_ref.at[nxt],
        sem_ref.at[nxt])
    if issue_first:
        start_next()
    slot = i % nb
    _wait_row_gather(src_hbm, buf_ref.at[slot], sem_ref.at[slot])
    rows = _load_token_rows(buf_ref.at[slot], n)
    if issue_first is None:
        return rows, start_next
    if not issue_first:
        start_next()
    return rows


def _drain_row_gather(src_hbm, buf_ref, sem_ref):
    i = pl.program_id(0)
    nb = buf_ref.shape[0]

    @pl.when(i == pl.num_programs(0) - 1)
    def _():
        for k in range(1, nb):
            _wait_row_gather(src_hbm, buf_ref.at[(i + k) % nb], sem_ref.at[(i + k) % nb])


def _ada_kernel(c_ref, w_ref, b_ref, o_ref):
    c = c_ref[...]
    cond = c * _sigmoid(c)
    o_ref[0] = jnp.dot(cond, w_ref[0], preferred_element_type=F32,
                       precision=lax.Precision.HIGHEST) + b_ref[0]


def _ada(c, w_ada, b_ada):
    depth, d, n = w_ada.shape
    tn = 1536
    c8 = jnp.zeros((SUBLANES, d), F32).at[:BATCH].set(c)
    out = pl.pallas_call(
        _ada_kernel,
        out_shape=jax.ShapeDtypeStruct((depth, SUBLANES, n), F32),
        grid=(depth, n // tn),
        in_specs=[
            pl.BlockSpec((SUBLANES, d), lambda l, j: (0, 0)),
            pl.BlockSpec((1, d, tn), lambda l, j: (l, 0, j)),
            pl.BlockSpec((1, 1, tn), lambda l, j: (l, 0, j)),
        ],
        out_specs=pl.BlockSpec((1, SUBLANES, tn), lambda l, j: (l, 0, j)),
        compiler_params=_params("arbitrary", "arbitrary"),
        name="ada_mod",
    )(c8, w_ada, b_ada.reshape(depth, 1, n))
    return out[:, :BATCH].reshape(depth, BATCH, 6, d)


def _mlstm_layer_kernel(xn_ref, x_ref, mod_ref, g1_ref, wi_ref, bg_ref, ng_ref,
                        wo_ref, g2_ref, wr_ref, br_ref, tri_ref,
                        x1_ref, u2_ref, cls_ref, rank_ref, cnt_ref,
                        q_ref, k_ref, v_ref, og_ref, gcol_ref, grow_ref, m_ref, n_ref, h_ref, run_ref,
                        *c_refs):
    step = pl.program_id(0)

    @pl.when(step == 0)
    def _():
        for ref in (q_ref, k_ref, v_ref, og_ref, gcol_ref, grow_ref, m_ref, n_ref, h_ref, run_ref) + c_refs:
            ref[...] = jnp.zeros_like(ref)

    h_prev = h_ref[...]
    ln = CHUNK
    row = lax.broadcasted_iota(jnp.int32, (ln, ln), 0)
    col = lax.broadcasted_iota(jnp.int32, (ln, ln), 1)
    causal = col <= row
    n_streams = BATCH * N_HEADS
    m_all = [m_ref[st] for st in range(n_streams)]
    n_all = [n_ref[st] for st in range(n_streams)]
    m_out, n_out = [None] * n_streams, [None] * n_streams

    def stream(st):
        bi, h = divmod(st, N_HEADS)
        gates = gcol_ref[bi]
        column = lambda kind: jnp.broadcast_to(
            gates[:, kind * N_HEADS + h:kind * N_HEADS + h + 1], (ln, LANES))
        cm, b, imb_col = column(0), column(1), column(2)
        imb_row = grow_ref[bi, h:h + 1, :]
        qh = q_ref[bi, :, h * DH_QK:(h + 1) * DH_QK]
        kh = k_ref[bi, :, h * DH_QK:(h + 1) * DH_QK]
        vh = v_ref[bi, :, h * DH_V:(h + 1) * DH_V]
        c_ref = c_refs[st]
        c_prev = c_ref[...]
        m_prev = m_all[st]
        big_m = jnp.maximum(m_prev, cm)
        d_mat = jnp.exp(jnp.where(causal, imb_row - big_m, -jnp.inf))
        s_raw = lax.dot_general(qh, kh, (((1,), (1,)), ((), ())), preferred_element_type=F32)
        yield
        q_inter = qh.astype(F32) * jnp.exp(m_prev - big_m)
        s = s_raw * d_mat
        lhs = jnp.concatenate([q_inter.astype(BF16), s.astype(BF16)], axis=1)
        rhs = jnp.concatenate([c_prev.astype(BF16), vh], axis=0)
        num = _dot(lhs, rhs)
        den = jnp.sum(s + q_inter * n_all[st], axis=-1, keepdims=True)
        m_last = big_m[ln - 1:ln, :]
        kw = (kh.astype(F32) * jnp.exp(imb_col - m_last)).astype(BF16)
        update = lax.dot_general(kw, vh, (((0,), (0,)), ((), ())), preferred_element_type=F32)
        yield
        decay = jnp.exp(m_prev - m_last)
        c_ref[...] = jnp.concatenate([decay] * 2, axis=1) * c_prev + update
        n_out[st] = decay * n_all[st] + jnp.sum(kw.astype(F32), axis=0, keepdims=True)
        m_out[st] = b[ln - 1:ln, :] + m_last
        inv = 1.0 / jnp.maximum(jnp.abs(den), jnp.exp(-(b + big_m)))
        hh = num * jnp.concatenate([inv, inv], axis=1)
        sl = slice(h * DH_V, (h + 1) * DH_V)
        gate = _sigmoid(og_ref[bi, :, sl].astype(F32))
        h_ref[bi * ln:(bi + 1) * ln, sl] = (_rms(hh) * ng_ref[:, sl] * gate).astype(BF16)

    deferred_stores = []

    def projection():
        mod = mod_ref[...]
        u = jnp.concatenate(
            [_rms(xn_ref[bi]) * g1_ref[...] * (1.0 + mod[bi][1:2]) + mod[bi][0:1] for bi in range(BATCH)],
            axis=0).astype(BF16)
        gt = _dot(u, wi_ref[:, 2 * QK + 2 * D_MODEL:]) + bg_ref[...]
        cols, imb = _gate_columns(gt)
        imb_rows = imb.T[0:SUBLANES, :]
        for bi in range(BATCH):
            gcol_ref[bi] = cols[bi * ln:(bi + 1) * ln, 0:GATE_WIDTH]
            grow_ref[bi] = imb_rows[:, bi * ln:(bi + 1) * ln]
        yield
        lo = 0
        for ref, width, scale in ((q_ref, QK, DH_QK ** -0.5), (k_ref, QK, None), (v_ref, D_MODEL, None),
                                  (og_ref, D_MODEL, None)):
            for off in range(0, width, PROJ_COLS):
                part = _dot(u, wi_ref[:, lo + off:lo + off + PROJ_COLS])
                if scale is not None:
                    part = part * scale
                part = part.astype(BF16)

                def store(ref=ref, off=off, part=part):
                    for bi in range(BATCH):
                        ref[bi, :, off:off + PROJ_COLS] = part[bi * ln:(bi + 1) * ln]

                if ref is og_ref:
                    deferred_stores.append(store)
                else:
                    store()
                yield
            lo += width

    def previous_chunk():
        half = D_MODEL // 2
        mix0 = _dot(h_prev, wo_ref[:, 0:half])
        yield
        mix1 = _dot(h_prev, wo_ref[:, half:])
        yield
        mod = mod_ref[...]
        u2_rows = []
        for bi in range(BATCH):
            mb = mod[bi]
            rows = slice(bi * ln, (bi + 1) * ln)
            x1 = x_ref[bi] + mb[2:3] * jnp.concatenate([mix0[rows], mix1[rows]], axis=1)
            x1_ref[bi] = x1
            u2_b = _moe_input(x1, mb, g2_ref)
            _store_token_rows(u2_ref.at[bi], u2_b, ln)
            u2_rows.append(u2_b)
        valid = jnp.where(step > 1, 1.0, 0.0)
        yield from _route_stages(jnp.concatenate(u2_rows, axis=0), valid, wr_ref, br_ref, tri_ref,
                                 cls_ref, rank_ref, cnt_ref, run_ref)

    _round_robin([previous_chunk()] + [stream(st) for st in range(n_streams)],
                 background=projection(), every=PROJ_EVERY, start_after=n_streams + 1)
    for store in deferred_stores:
        store()
    for st in range(n_streams):
        m_ref[st] = m_out[st]
        n_ref[st] = n_out[st]


def _gate_columns(gt):
    lane = lax.broadcasted_iota(jnp.int32, gt.shape, 1)
    pos = lax.broadcasted_iota(jnp.int32, gt.shape, 0) % CHUNK
    b = jnp.minimum(gt, 0.0) - jnp.log(1.0 + jnp.exp(-jnp.abs(gt)))
    shift = 1
    while shift < CHUNK:
        b = b + jnp.where(pos >= shift, pltpu.roll(b, shift, axis=0), 0.0)
        shift *= 2
    imb = gt - pltpu.roll(b, LANES - N_HEADS, axis=1)
    cm = imb
    shift = 1
    while shift < CHUNK:
        cm = jnp.maximum(cm, jnp.where(pos >= shift, pltpu.roll(cm, shift, axis=0), -jnp.inf))
        shift *= 2
    cols = jnp.where(lane < N_HEADS, cm,
                     jnp.where(lane < 2 * N_HEADS, b,
                               jnp.where(lane < 3 * N_HEADS, pltpu.roll(imb, 2 * N_HEADS, axis=1), 0.0)))
    return cols, imb


def _mlstm_layer(x, mod, g1, w_in, bg, norm_g, w_out, g2, wr, br):
    nc = SEQ // CHUNK
    n_streams = BATCH * N_HEADS
    rows = BATCH * CHUNK
    newest =lambda c: jnp.minimum(c, nc - 1)
    oldest = lambda c: jnp.maximum(c - 2, 0)
    old_rows = lambda height, width: pl.BlockSpec((BATCH, height, width), lambda c: (0, oldest(c), 0))
    x3d = x.reshape(BATCH, SEQ, D_MODEL)
    x1, u2, cls, rank, cnt = pl.pallas_call(
        _mlstm_layer_kernel,
        out_shape=(
            jax.ShapeDtypeStruct((BATCH, SEQ, D_MODEL), F32),
            jax.ShapeDtypeStruct((BATCH, SEQ * ROW_TILE, LANES), F32),
            jax.ShapeDtypeStruct((nc, 1, rows), jnp.int32),
            jax.ShapeDtypeStruct((nc, 1, rows), jnp.int32),
            jax.ShapeDtypeStruct((CLASS_ROWS, LANES), jnp.int32),
        ),
        grid=(nc + 2,),
        in_specs=[
            pl.BlockSpec((BATCH, CHUNK, D_MODEL), lambda c: (0, newest(c), 0)),
            old_rows(CHUNK, D_MODEL),
            _const_spec((BATCH, 6, D_MODEL)),
            _const_spec((1, D_MODEL)),
            _const_spec(w_in.shape),
            _const_spec((1, GATE_COLS)),
            _const_spec((1, D_MODEL)),
            _const_spec((D_MODEL, D_MODEL)),
            _const_spec((1, D_MODEL)),
            _const_spec((D_MODEL, ROUTER_COLS)),
            _const_spec((N_EXPERTS, 1)),
            _const_spec((rows, rows)),
        ],
        out_specs=(
            old_rows(CHUNK, D_MODEL),
            old_rows(CHUNK * ROW_TILE, LANES),
            pl.BlockSpec((1, 1, rows), lambda c: (oldest(c), 0, 0)),
            pl.BlockSpec((1, 1, rows), lambda c: (oldest(c), 0, 0)),
            _const_spec((CLASS_ROWS, LANES)),
        ),
        scratch_shapes=[pltpu.VMEM((BATCH, CHUNK, QK), BF16),
                        pltpu.VMEM((BATCH, CHUNK, QK), BF16),
                        pltpu.VMEM((BATCH, CHUNK, D_MODEL), BF16),
                        pltpu.VMEM((BATCH, CHUNK, D_MODEL), BF16),
                        pltpu.VMEM((BATCH, CHUNK, GATE_WIDTH), F32),
                        pltpu.VMEM((BATCH, SUBLANES, CHUNK), F32),
                        pltpu.VMEM((n_streams, 1, LANES), F32),
                        pltpu.VMEM((n_streams, 1, DH_QK), F32),
                        pltpu.VMEM((rows, D_MODEL), BF16),
                        pltpu.VMEM((CLASS_ROWS, LANES), F32)]
        + [pltpu.VMEM((DH_QK, DH_V), F32) for _ in range(n_streams)],
        compiler_params=_params("arbitrary"),
        name="mlstm_layer",
    )(x3d, x3d, mod, g1, w_in, bg, norm_g, w_out, g2, wr, br, _strict_upper(rows))
    token_order = lambda a: a.reshape(nc, BATCH, CHUNK).transpose(1, 0, 2).reshape(TOKENS)
    return (x1.reshape(TOKENS, D_MODEL), u2.reshape(TOKENS * ROW_TILE, LANES), token_order(cls),
            token_order(rank), cnt)


def _top2_sum(v0, v1, v2, v3):
    hi1, lo1 = jnp.maximum(v0, v1), jnp.minimum(v0, v1)
    hi2, lo2 = jnp.maximum(v2, v3), jnp.minimum(v2, v3)
    return jnp.maximum(hi1, hi2) + jnp.maximum(jnp.minimum(hi1, hi2), jnp.maximum(lo1, lo2))


def _route_tail(x_new, m, valid, g2_ref, wr_ref, br_ref, tri_ref, u2_ref, cls_ref, rank_ref, cnt_ref,
                run_ref, after_router=None):
    u2 = _moe_input(x_new, m, g2_ref)
    _store_token_rows(u2_ref, u2, u2.shape[0])
    _route(u2, valid, wr_ref, br_ref, tri_ref, cls_ref, rank_ref, cnt_ref, run_ref, after_router)


def _moe_input(x_new, m, g2_ref):
    return _rms(x_new) * g2_ref[...] * (1.0 + m[4:5]) + m[3:4]


def _round_robin(chains, background=None, every=1, start_after=0):
    chains = list(chains)
    advanced = 0
    while chains:
        alive = []
        for chain in chains:
            try:
                next(chain)
            except StopIteration:
                continue
            alive.append(chain)
            advanced += 1
            if background is not None and advanced >= start_after and advanced % every == 0:
                next(background, None)
        chains = alive
    if background is not None:
        for _ in background:
            pass


def _route(u2, valid, wr_ref, br_ref, tri_ref, cls_ref, rank_ref, cnt_ref, run_ref, after_router=None):
    stages = _route_stages(u2, valid, wr_ref, br_ref, tri_ref, cls_ref, rank_ref, cnt_ref, run_ref)
    next(stages)
    if after_router is not None:
        after_router()
    for _ in stages:
        pass


def _route_stages(u2, valid, wr_ref, br_ref, tri_ref, cls_ref, rank_ref, cnt_ref, run_ref):
    logits = _dot(u2.astype(BF16), wr_ref[...])
    yield
    lt = logits.T[0:N_EXPERTS, :]
    e = jnp.exp(lt - jnp.max(lt, axis=0, keepdims=True))
    probs = e / jnp.sum(e, axis=0, keepdims=True)
    sel = probs + br_ref[...]
    sel_rows = [sel[j:j + 1, :] for j in range(N_EXPERTS)]
    best = jnp.zeros_like(sel_rows[0], dtype=jnp.int32)
    best_score = _top2_sum(*sel_rows[0:EXPERTS_PER_GROUP])
    for g in range(1, N_GROUPS):
        score = _top2_sum(*sel_rows[g * EXPERTS_PER_GROUP:(g + 1) * EXPERTS_PER_GROUP])
        better = score > best_score
        best = jnp.where(better, g, best)
        best_score = jnp.where(better, score, best_score)
    s = []
    for j in range(EXPERTS_PER_GROUP):
        sj = sel_rows[j]
        for g in range(1, N_GROUPS):
            sj = jnp.where(best == g, sel_rows[g * EXPERTS_PER_GROUP + j], sj)
        s.append(sj)
    chosen = []
    for j in range(EXPERTS_PER_GROUP):
        beaten = jnp.zeros_like(best)
        for i in range(EXPERTS_PER_GROUP):
            if i == j:
                continue
            wins = (s[i] >= s[j]) if i < j else (s[i] > s[j])
            beaten = beaten + jnp.where(wins, 1, 0)
        chosen.append(beaten < 2)
    pair = jnp.full_like(best, len(PAIRS) - 1)
    for p in range(len(PAIRS) - 2, -1, -1):
        a, b = PAIRS[p]
        pair = jnp.where(jnp.logical_and(chosen[a], chosen[b]), p, pair)
    cls = best * len(PAIRS) + pair
    cls_ref[0] = cls

    class_id = lax.broadcasted_iota(jnp.int32, (CLASS_ROWS, cls.shape[1]), 0)
    onehot = class_id == cls
    before = _dot(jnp.where(onehot, 1.0, 0.0).astype(BF16), tri_ref[...])
    run = run_ref[...]
    rank = jnp.sum(jnp.where(onehot, before + run[:, 0:1], 0.0), axis=0, keepdims=True)
    rank_ref[0] = rank.astype(jnp.int32)
    run = run + valid * jnp.sum(jnp.where(onehot, 1.0, 0.0), axis=1, keepdims=True)
    run_ref[...] = run
    cnt_ref[...] = run.astype(jnp.int32)


def _strict_upper(n):
    return (jnp.arange(n)[:, None] < jnp.arange(n)[None, :]).astype(BF16)


def _route_out_shapes(t, tm):
    return (
        jax.ShapeDtypeStruct((t, D_MODEL), F32),
        jax.ShapeDtypeStruct((t * ROW_TILE, LANES), F32),
        jax.ShapeDtypeStruct((t // tm, 1, tm), jnp.int32),
        jax.ShapeDtypeStruct((t // tm, 1, tm), jnp.int32),
        jax.ShapeDtypeStruct((CLASS_ROWS, LANES), jnp.int32),
    )


def _cur_tile(n_tiles):
    return lambda i: jnp.minimum(i, n_tiles - 1)


def _prev_tile(i):
    return jnp.maximum(i - 1, 0)


def _route_out_specs(tm, n_tiles):
    cur = _cur_tile(n_tiles)
    return (
        pl.BlockSpec((tm, D_MODEL), lambda i, *_: (cur(i), 0)),
        pl.BlockSpec((tm * ROW_TILE, LANES), lambda i, *_: (_prev_tile(i), 0)),
        pl.BlockSpec((1, 1, tm), lambda i, *_: (_prev_tile(i), 0, 0)),
        pl.BlockSpec((1, 1, tm), lambda i, *_: (_prev_tile(i), 0, 0)),
        pl.BlockSpec((CLASS_ROWS, LANES), lambda i, *_: (0, 0)),
    )


def _init_deferred_tail(xprev_ref, run_ref):
    @pl.when(pl.program_id(0) == 0)
    def _():
        xprev_ref[...] = jnp.zeros_like(xprev_ref)
        run_ref[...] = jnp.zeros_like(run_ref)


def _deferred_tail(xprev_ref, modp_ref, *tail_refs, after_router=None):
    valid = jnp.where(pl.program_id(0) > 0, 1.0, 0.0)
    _route_tail(xprev_ref[...], modp_ref[0], valid, *tail_refs, after_router=after_router)


def _const_spec(shape):
    return pl.BlockSpec(shape, lambda i, *_: (0,) * len(shape))


def _conv_layer_kernel(dest_ref, x_ref, ys_ref, mod0_ref, mod_ref, modp_ref, g1_ref, wi_ref, cw_ref,
                       cb_ref, wo_ref, g2_ref, wr_ref, br_ref, tri_ref,
                       x3_ref, u2_ref, cls_ref, rank_ref, cnt_ref,
                       run_ref, carry_ref, ybuf_ref, ysem_ref, xprev_ref):
    tm = x_ref.shape[0]
    tiles_per_batch = SEQ // tm
    n_tiles = pl.num_programs(0) - 1
    i = pl.program_id(0)

    @pl.when(i % tiles_per_batch == 0)
    def _():
        carry_ref[...] = jnp.zeros_like(carry_ref)

    _init_deferred_tail(xprev_ref, run_ref)
    m = mod_ref[0]
    y = _gathered_tile(dest_ref, ys_ref, ybuf_ref, ysem_ref, tm, issue_first=False, n_tiles=n_tiles)
    x2 = x_ref[...] + mod0_ref[0][5:6] * y
    u = (_rms(x2) * g1_ref[...] * (1.0 + m[1:2]) + m[0:1]).astype(BF16)
    bgate = _dot(u, wi_ref[:, 0:D_MODEL])
    gated = []

    def project_gated():
        gated.append(_dot(u, wi_ref[:, D_MODEL:2 * D_MODEL]) * _dot(u, wi_ref[:, 2 * D_MODEL:]))

    _deferred_tail(xprev_ref, modp_ref, g2_ref, wr_ref, br_ref, tri_ref, u2_ref, cls_ref, rank_ref,
                   cnt_ref, run_ref, after_router=project_gated)
    z = gated[0]
    prev = carry_ref[...]
    row = lax.broadcasted_iota(jnp.int32, z.shape, 0)
    z1 = jnp.where(row == 0, prev[7:8], pltpu.roll(z, 1, axis=0))
    z2 = jnp.where(row == 0, prev[6:7], jnp.where(row == 1, prev[7:8], pltpu.roll(z, 2, axis=0)))
    carry_ref[...] = z[tm - SUBLANES:, :]
    cw = cw_ref[...]
    zc = cw[0:1] * z2 + cw[1:2] * z1 + cw[2:3] * z + cb_ref[...]
    x3 = x2 + m[2:3] * _dot((bgate * zc).astype(BF16), wo_ref[...])
    xprev_ref[...] = x3

    @pl.when(i < n_tiles)
    def _():
        x3_ref[...] = x3

    _drain_row_gather(ys_ref, ybuf_ref, ysem_ref)


def _conv_layer(dest, x, ys, mod0, mod, g1, w_in, conv_w, conv_b, w_out, g2, wr, br):
    t = x.shape[0]
    tm = TM_CONV
    tri = _strict_upper(tm)
    tiles_per_batch = SEQ // tm
    n_tiles = t // tm
    cur = _cur_tile(n_tiles)
    mod_spec = pl.BlockSpec((1, 6, D_MODEL), lambda i, *_: (cur(i) // tiles_per_batch, 0, 0))
    modp_spec = pl.BlockSpec((1, 6, D_MODEL), lambda i, *_: (_prev_tile(i) // tiles_per_batch, 0, 0))
    return pl.pallas_call(
        _conv_layer_kernel,
        out_shape=_route_out_shapes(t, tm),
        grid_spec=pltpu.PrefetchScalarGridSpec(
            num_scalar_prefetch=1,
            grid=(n_tiles + 1,),
            in_specs=[
                pl.BlockSpec((tm, D_MODEL), lambda i, *_: (cur(i), 0)),
                pl.BlockSpec(memory_space=pl.ANY),
                mod_spec,
                mod_spec,
                modp_spec,
                _const_spec((1, D_MODEL)),
                _const_spec((D_MODEL, 3 * D_MODEL)),
                _const_spec((3, D_MODEL)),
                _const_spec((1, D_MODEL)),
                _const_spec((D_MODEL, D_MODEL)),
                _const_spec((1, D_MODEL)),
                _const_spec((D_MODEL, ROUTER_COLS)),
                _const_spec((N_EXPERTS, 1)),
                _const_spec((tm, tm)),
            ],
            out_specs=_route_out_specs(tm, n_tiles),
            scratch_shapes=[
                pltpu.VMEM((CLASS_ROWS, LANES), F32),
                pltpu.VMEM((SUBLANES, D_MODEL), F32),
                pltpu.VMEM((2, tm * ROW_TILE, LANES), F32),
                pltpu.SemaphoreType.DMA((2,)),
                pltpu.VMEM((tm, D_MODEL), F32),
            ],
        ),
        compiler_params=_params("arbitrary"),
        name="conv_layer_route",
    )(dest, x, ys, mod0, mod, mod, g1, w_in, conv_w, conv_b, w_out, g2, wr, br, tri)


def _invert_kernel(dest_ref, fill_ref, src_ref, sem):
    fill = pltpu.make_async_copy(fill_ref, src_ref, sem)
    fill.start()
    fill.wait()
    unroll = 8

    def body(t8, carry):
        for k in range(unroll):
            t = t8 * unroll + k
            src_ref[dest_ref[t]] = t
        return carry

    lax.fori_loop(0, dest_ref.shape[0] // unroll, body, 0)


def _invert_permutation(dest, n_out):
    smem = pl.BlockSpec(memory_space=pltpu.SMEM)
    return pl.pallas_call(
        _invert_kernel,
        out_shape=jax.ShapeDtypeStruct((n_out,), jnp.int32),
        in_specs=[smem, pl.BlockSpec(memory_space=pl.ANY)],
        out_specs=smem,
        scratch_shapes=[pltpu.SemaphoreType.DMA(())],
        name="invert_permutation",
    )(dest, jnp.arange(n_out, dtype=jnp.int32) % dest.shape[0])


def _expert_kernel(ea_ref, eb_ref, chg_ref, nused_ref, src_ref, u2_ref, wga_ref, wua_ref, wda_ref,
                   wgb_ref, wub_ref, wdb_ref, wra_ref, wrb_ref, y_ref, xbuf_ref, xsem_ref, *wbf_refs):
    del ea_ref, eb_ref
    j = pl.program_id(0)
    tm = TM_EXPERT

    @pl.when(chg_ref[j] == 1)
    def _():
        for src, dst in zip((wga_ref, wua_ref, wda_ref, wgb_ref, wub_ref, wdb_ref), wbf_refs):
            dst[...] = src[0, 0].astype(BF16)

    rows, start_next = _gathered_tile(src_ref, u2_ref, xbuf_ref, xsem_ref, tm, issue_first=None)

    @pl.when(j < nused_ref[0])
    def _():
        start_next()
        xb = rows.astype(BF16)
        dl = jnp.sum(xb.astype(F32) * (wra_ref[0] - wrb_ref[0]), axis=-1, keepdims=True)
        w_a = _sigmoid(dl)
        w_b = _sigmoid(-dl)

        def ffn(wg_ref, wu_ref, wd_ref):
            gate = _dot(xb, wg_ref[...])
            hidden = gate * _sigmoid(gate) * _dot(xb, wu_ref[...])
            return _dot(hidden.astype(BF16), wd_ref[...])

        y = w_a * ffn(*wbf_refs[0:3]) + w_b * ffn(*wbf_refs[3:6])
        _store_token_rows(y_ref, y, tm)

    @pl.when(j >= nused_ref[0])
    def _():
        start_next()
        y_ref[...] = jnp.zeros_like(y_ref)

    _drain_row_gather(u2_ref, xbuf_ref, xsem_ref)


def _experts(layer, ea, eb, chg, n_used, src, u2, w_gate, w_up, w_down, wr_rows):
    tm = TM_EXPERT
    sel_a = lambda j, ea, eb, *_: (layer, ea[j], 0, 0)
    sel_b = lambda j, ea, eb, *_: (layer, eb[j], 0, 0)
    up_spec = lambda sel: pl.BlockSpec((1, 1, D_MODEL, D_EXPERT), sel)
    down_spec = lambda sel: pl.BlockSpec((1, 1, D_EXPERT, D_MODEL), sel)
    wr_spec = lambda sel: pl.BlockSpec((1, 1, D_MODEL), lambda *a: sel(*a)[1:])
    up_scratch = pltpu.VMEM((D_MODEL, D_EXPERT), BF16)
    down_scratch = pltpu.VMEM((D_EXPERT, D_MODEL), BF16)
    return pl.pallas_call(
        _expert_kernel,
        out_shape=jax.ShapeDtypeStruct((PADDED_ROWS * ROW_TILE, LANES), F32),
        grid_spec=pltpu.PrefetchScalarGridSpec(
            num_scalar_prefetch=4,
            grid=(N_EXPERT_TILES,),
            in_specs=[
                pl.BlockSpec(memory_space=pltpu.SMEM),
                pl.BlockSpec(memory_space=pl.ANY),
                up_spec(sel_a), up_spec(sel_a), down_spec(sel_a),
                up_spec(sel_b), up_spec(sel_b), down_spec(sel_b),
                wr_spec(sel_a), wr_spec(sel_b),
            ],
            out_specs=pl.BlockSpec((tm * ROW_TILE, LANES), lambda j, *_: (j, 0)),
            scratch_shapes=[pltpu.VMEM((EXPERT_GATHER_BUFFERS, tm * ROW_TILE, LANES), F32),
                            pltpu.SemaphoreType.DMA((EXPERT_GATHER_BUFFERS,)),
                            up_scratch, up_scratch, down_scratch, up_scratch, up_scratch, down_scratch],
        ),
        compiler_params=_params("arbitrary"),
        name="grouped_experts",
    )(ea, eb, chg, n_used, src, u2, w_gate, w_up, w_down, w_gate, w_up, w_down, wr_rows, wr_rows)


def _final_kernel(dest_ref, x_ref, ys_ref, mod_ref, g_ref, o_ref, ybuf_ref, ysem_ref):
    y = _gathered_tile(dest_ref, ys_ref, ybuf_ref, ysem_ref, x_ref.shape[0], issue_first=True)
    x = x_ref[...] + mod_ref[0][5:6] * y
    o_ref[...] = _rms(x) * g_ref[...]
    _drain_row_gather(ys_ref, ybuf_ref, ysem_ref)


def _final(dest, x, ys, mod, g):
    t = x.shape[0]
    tm = TM_FINAL
    tiles_per_batch = SEQ // tm
    return pl.pallas_call(
        _final_kernel,
        out_shape=jax.ShapeDtypeStruct((t, D_MODEL), F32),
        grid_spec=pltpu.PrefetchScalarGridSpec(
            num_scalar_prefetch=1,
            grid=(t // tm,),
            in_specs=[
                pl.BlockSpec((tm, D_MODEL), lambda i, *_: (i, 0)),
                pl.BlockSpec(memory_space=pl.ANY),
                pl.BlockSpec((1, 6, D_MODEL), lambda i, *_: (i // tiles_per_batch, 0, 0)),
                _const_spec((1, D_MODEL)),
            ],
            out_specs=pl.BlockSpec((tm, D_MODEL), lambda i, *_: (i, 0)),
            scratch_shapes=[pltpu.VMEM((2, tm * ROW_TILE, LANES), F32), pltpu.SemaphoreType.DMA((2,))],
        ),
        compiler_params=_params("arbitrary"),
        name="final_norm",
    )(dest, x, ys, mod, g)


_PAIR_A = np.array([EXPERTS_PER_GROUP * (c // len(PAIRS)) + PAIRS[c % len(PAIRS)][0]
                    for c in range(N_CLASSES)], np.int32)
_PAIR_B = np.array([EXPERTS_PER_GROUP * (c // len(PAIRS)) + PAIRS[c % len(PAIRS)][1]
                    for c in range(N_CLASSES)], np.int32)


def _moe(layer, u2, cls, rank, cnt, w_gate, w_up, w_down, wr_rows):
    counts = cnt[:N_CLASSES, 0]
    tiles = (counts + TM_EXPERT - 1) // TM_EXPERT
    tile_end = jnp.cumsum(tiles)
    row_start = (tile_end - tiles) * TM_EXPERT
    n_used = tile_end[-1:]
    cls = cls.reshape(-1, LANES)
    dest = rank.reshape(-1, LANES)
    for c in range(N_CLASSES):
        dest = dest + jnp.where(cls == c, row_start[c], 0)
    dest = dest.reshape(-1)
    tile_cls = jnp.sum(jnp.arange(N_EXPERT_TILES)[:, None] >= tile_end[None, :], axis=1)
    tile_cls = jnp.minimum(tile_cls, tile_cls[jnp.maximum(n_used[0] - 1, 0)])
    ea = jnp.asarray(_PAIR_A)[tile_cls]
    eb = jnp.asarray(_PAIR_B)[tile_cls]
    chg = jnp.concatenate([jnp.ones((1,), jnp.int32),
                           (tile_cls[1:] != tile_cls[:-1]).astype(jnp.int32)])
    src = _invert_permutation(dest, PADDED_ROWS)
    ys = _experts(layer, ea, eb, chg, n_used.astype(jnp.int32), src, u2, w_gate, w_up, w_down, wr_rows)
    return dest, ys


def kernel(x, c, norm1_g, norm2_g, w_ada, b_ada, m_w_in, m_b_gates, m_norm_g, m_w_out,
           c_w_in, c_conv_w, c_conv_b, c_w_out, w_router, b_router,
           e_w_gate, e_w_up, e_w_down, final_g):
    xf = x.reshape(TOKENS, D_MODEL)
    mod = _ada(c, w_ada, b_ada)

    w_in = jnp.concatenate(
        [m_w_in[0].astype(BF16),
         jnp.zeros((D_MODEL, GATE_COLS - 2 * N_HEADS), BF16)], axis=1)
    bg = jnp.zeros((1, GATE_COLS), F32).at[0, :2 * N_HEADS].set(m_b_gates[0])
    wr = jnp.zeros((D_MODEL, ROUTER_COLS), BF16).at[:, :N_EXPERTS].set(w_router.astype(BF16))
    wr_rows = w_router.astype(BF16).astype(F32).T.reshape(N_EXPERTS, 1, D_MODEL)
    br = b_router.reshape(N_EXPERTS, 1)
    row = lambda v: v.reshape(1, -1)

    x1, u2, cls, rank, cnt = _mlstm_layer(xf, mod[0], row(norm1_g[0]), w_in, bg, row(m_norm_g[0]),
                                          m_w_out[0].astype(BF16), row(norm2_g[0]), wr, br)
    dest, ys = _moe(0, u2, cls, rank, cnt, e_w_gate, e_w_up, e_w_down, wr_rows)

    x3, u2, cls, rank, cnt = _conv_layer(
        dest, x1, ys, mod[0], mod[1], row(norm1_g[1]), c_w_in[0].astype(BF16), c_conv_w[0],
        row(c_conv_b[0]), c_w_out[0].astype(BF16), row(norm2_g[1]), wr, br)
    dest, ys = _moe(1, u2, cls, rank, cnt, e_w_gate, e_w_up, e_w_down, wr_rows)

    out = _final(dest, x3, ys, mod[1], row(final_g))
    return out.reshape(BATCH, SEQ, D_MODEL)
```

```python
import functools

import jax
import jax.numpy as jnp
import numpy as np
from jax import lax
from jax.experimental import pallas as pl
from jax.experimental.pallas import tpu as pltpu

F32 = jnp.float32
BF16 = jnp.bfloat16

D_MODEL = 1024
BATCH = 4
SEQ = 8192
TOKENS = BATCH * SEQ
N_HEADS = 4
DH_V = 256
DH_QK = 128
QK = N_HEADS * DH_QK
N_EXPERTS = 16
N_GROUPS = 4
EXPERTS_PER_GROUP = 4
D_EXPERT = 512
EPS = 1e-6

LANES = 128
SUBLANES = 8
VMEM_LIMIT_BYTES = 56 * 1024 * 1024

CHUNK = 128
TM_CONV = 512
TM_EXPERT = 512
GATE_COLS = LANES
GATE_WIDTH = 16
ROUTER_COLS = LANES
PAIRS = ((0, 1), (0, 2), (0, 3), (1, 2), (1, 3), (2, 3))
N_CLASSES = N_GROUPS * len(PAIRS)
CLASS_ROWS = 32
N_EXPERT_TILES = TOKENS // TM_EXPERT + N_CLASSES
PADDED_ROWS = N_EXPERT_TILES * TM_EXPERT
TM_FINAL = 512
EXPERT_GATHER_BUFFERS = 4
PROJ_COLS = 256
PROJ_EVERY = 1


def _params(*semantics):
    return pltpu.CompilerParams(dimension_semantics=semantics, vmem_limit_bytes=VMEM_LIMIT_BYTES)


def _dot(a, b):
    return jnp.dot(a, b, preferred_element_type=F32)


def _rms(x):
    return x * lax.rsqrt(jnp.mean(x * x, axis=-1, keepdims=True) + EPS)


def _sigmoid(x):
    return 1.0 / (1.0 + jnp.exp(-x))


ROW_TILE = D_MODEL // LANES


def _load_token_rows(ref, n):
    return jnp.concatenate([ref[pl.ds(c, n, stride=ROW_TILE), :] for c in range(ROW_TILE)], axis=1)


def _store_token_rows(ref, val, n):
    for c in range(ROW_TILE):
        ref[pl.ds(c, n, stride=ROW_TILE), :] = val[:, c * LANES:(c + 1) * LANES]


def _token_tile(ref, t):
    return ref.at[pl.ds(pl.multiple_of(t * ROW_TILE, ROW_TILE), ROW_TILE)]


def _start_row_gather(idx_ref, base, n, src_hbm, buf, sem):
    for r in range(n):
        pltpu.make_async_copy(_token_tile(src_hbm, idx_ref[base + r]), _token_tile(buf, r),
                              sem).start(priority=r % 2)


def _wait_row_gather(src_hbm, buf, sem):
    pltpu.make_async_copy(src_hbm.at[pl.ds(0, buf.shape[0])], buf, sem).wait()


def _gathered_tile(idx_ref, src_hbm, buf_ref, sem_ref, n, issue_first, n_tiles=None, n_active=None):
    i = pl.program_id(0)
    last = (pl.num_programs(0) if n_tiles is None else n_tiles) - 1
    nb = buf_ref.shape[0]
    ahead = nb - 1

    @pl.when(i == 0)
    def _():
        for k in range(ahead):
            _start_row_gather(idx_ref, jnp.minimum(k, last) * n, n, src_hbm, buf_ref.at[k], sem_ref.at[k])

    nxt = (i + ahead) % nb
    start_next = functools.partial(
        _start_row_gather, idx_ref, jnp.minimum(i + ahead, last) * n, n, src_hbm, buf_ref.at[nxt],
        sem_ref.at[nxt])
    if issue_first:
        start_next()
    slot = i % nb
    if n_active is None:
        _wait_row_gather(src_hbm, buf_ref.at[slot], sem_ref.at[slot])
    else:
        @pl.when(i < n_active + ahead)
        def _():
            _wait_row_gather(src_hbm, buf_ref.at[slot], sem_ref.at[slot])

    load_rows = functools.partial(_load_token_rows, buf_ref.at[slot], n)
    if issue_first is None:
        return load_rows, start_next
    rows = load_rows()
    if not issue_first:
        start_next()
    return rows


def _drain_row_gather(src_hbm, buf_ref, sem_ref, n_active=None):
    i = pl.program_id(0)
    nb = buf_ref.shape[0]

    @pl.when(i == pl.num_programs(0) - 1)
    def _():
        for k in range(1, nb):
            wait = functools.partial(_wait_row_gather, src_hbm, buf_ref.at[(i + k) % nb],
                                     sem_ref.at[(i + k) % nb])
            if n_active is None:
                wait()
            else:
                pl.when(i + k - (nb - 1) < n_active)(wait)


def _ada_kernel(c_ref, w_ref, b_ref, o_ref):
    c = c_ref[...]
    cond = c * _sigmoid(c)
    o_ref[0] = jnp.dot(cond, w_ref[0], preferred_element_type=F32,
                       precision=lax.Precision.HIGHEST) + b_ref[0]


def _ada(c, w_ada, b_ada):
    depth, d, n = w_ada.shape
    tn = 1536
    c8 = jnp.zeros((SUBLANES, d), F32).at[:BATCH].set(c)
    out = pl.pallas_call(
        _ada_kernel,
        out_shape=jax.ShapeDtypeStruct((depth, SUBLANES, n), F32),
        grid=(depth, n // tn),
        in_specs=[
            pl.BlockSpec((SUBLANES, d), lambda l, j: (0, 0)),
            pl.BlockSpec((1, d, tn), lambda l, j: (l, 0, j)),
            pl.BlockSpec((1, 1, tn), lambda l, j: (l, 0, j)),
        ],
        out_specs=pl.BlockSpec((1, SUBLANES, tn), lambda l, j: (l, 0, j)),
        compiler_params=_params("arbitrary", "arbitrary"),
        name="ada_mod",
    )(c8, w_ada, b_ada.reshape(depth, 1, n))
    return out[:, :BATCH].reshape(depth, BATCH, 6, d)


def _mlstm_layer_kernel(xn_ref, x_ref, mod_ref, g1_ref, wi_ref, bg_ref, ng_ref,
                        wo_ref, g2_ref, wr_ref, br_ref, tri_ref,
                        x1_ref, u2_ref, cls_ref, rank_ref, cnt_ref,
                        q_ref, k_ref, v_ref, og_ref, gcol_ref, grow_ref, m_ref, n_ref, h_ref, run_ref,
                        *c_refs):
    step = pl.program_id(0)

    @pl.when(step == 0)
    def _():
        for ref in (q_ref, k_ref, v_ref, og_ref, gcol_ref, grow_ref, m_ref, n_ref, h_ref, run_ref) + c_refs:
            ref[...] = jnp.zeros_like(ref)

    h_prev = h_ref[...]
    ln = CHUNK
    row = lax.broadcasted_iota(jnp.int32, (ln, ln), 0)
    col = lax.broadcasted_iota(jnp.int32, (ln, ln), 1)
    causal = col <= row
    n_streams = BATCH * N_HEADS
    m_all = [m_ref[st] for st in range(n_streams)]
    n_all = [n_ref[st] for st in range(n_streams)]
    m_out, n_out = [None] * n_streams, [None] * n_streams

    def stream(st):
        bi, h = divmod(st, N_HEADS)
        gates = gcol_ref[bi]
        column = lambda kind: jnp.broadcast_to(
            gates[:, kind * N_HEADS + h:kind * N_HEADS + h + 1], (ln, LANES))
        cm, b, imb_col = column(0), column(1), column(2)
        imb_row = grow_ref[bi, h:h + 1, :]
        qh = q_ref[bi, :, h * DH_QK:(h + 1) * DH_QK]
        kh = k_ref[bi, :, h * DH_QK:(h + 1) * DH_QK]
        vh = v_ref[bi, :, h * DH_V:(h + 1) * DH_V]
        c_ref = c_refs[st]
        c_prev = c_ref[...]
        m_prev = m_all[st]
        big_m = jnp.maximum(m_prev, cm)
        d_mat = jnp.exp(jnp.where(causal, imb_row - big_m, -jnp.inf))
        s_raw = lax.dot_general(qh, kh, (((1,), (1,)), ((), ())), preferred_element_type=F32)
        yield
        q_inter = qh.astype(F32) * jnp.exp(m_prev - big_m)
        s = s_raw * d_mat
        lhs = jnp.concatenate([q_inter.astype(BF16), s.astype(BF16)], axis=1)
        rhs = jnp.concatenate([c_prev.astype(BF16), vh], axis=0)
        num = _dot(lhs, rhs)
        den = jnp.sum(s + q_inter * n_all[st], axis=-1, keepdims=True)
        m_last = big_m[ln - 1:ln, :]
        kw = (kh.astype(F32) * jnp.exp(imb_col - m_last)).astype(BF16)
        update = lax.dot_general(kw, vh, (((0,), (0,)), ((), ())), preferred_element_type=F32)
        yield
        decay = jnp.exp(m_prev - m_last)
        c_ref[...] = jnp.concatenate([decay] * 2, axis=1) * c_prev + update
        n_out[st] = decay * n_all[st] + jnp.sum(kw.astype(F32), axis=0, keepdims=True)
        m_out[st] = b[ln - 1:ln, :] + m_last
        inv = 1.0 / jnp.maximum(jnp.abs(den), jnp.exp(-(b + big_m)))
        hh = num * jnp.concatenate([inv, inv], axis=1)
        sl = slice(h * DH_V, (h + 1) * DH_V)
        gate = _sigmoid(og_ref[bi, :, sl].astype(F32))
        h_ref[bi * ln:(bi + 1) * ln, sl] = (_rms(hh) * ng_ref[:, sl] * gate).astype(BF16)

    deferred_stores = []

    def projection():
        mod = mod_ref[...]
        u = jnp.concatenate(
            [_rms(xn_ref[bi]) * g1_ref[...] * (1.0 + mod[bi][1:2]) + mod[bi][0:1] for bi in range(BATCH)],
            axis=0).astype(BF16)
        gt = _dot(u, wi_ref[:, 2 * QK + 2 * D_MODEL:]) + bg_ref[...]
        cols, imb = _gate_columns(gt)
        imb_rows = imb.T[0:SUBLANES, :]
        for bi in range(BATCH):
            gcol_ref[bi] = cols[bi * ln:(bi + 1) * ln, 0:GATE_WIDTH]
            grow_ref[bi] = imb_rows[:, bi * ln:(bi + 1) * ln]
        yield
        lo = 0
        for ref, width, scale in ((q_ref, QK, DH_QK ** -0.5), (k_ref, QK, None), (v_ref, D_MODEL, None),
                                  (og_ref, D_MODEL, None)):
            for off in range(0, width, PROJ_COLS):
                part = _dot(u, wi_ref[:, lo + off:lo + off + PROJ_COLS])
                if scale is not None:
                    part = part * scale
                part = part.astype(BF16)

                def store(ref=ref, off=off, part=part):
                    for bi in range(BATCH):
                        ref[bi, :, off:off + PROJ_COLS] = part[bi * ln:(bi + 1) * ln]

                if ref is og_ref:
                    deferred_stores.append(store)
                else:
                    store()
                yield
            lo += width

    def previous_chunk():
        half = D_MODEL // 2
        mix0 = _dot(h_prev, wo_ref[:, 0:half])
        yield
        mix1 = _dot(h_prev, wo_ref[:, half:])
        yield
        mod = mod_ref[...]
        u2_rows = []
        for bi in range(BATCH):
            mb = mod[bi]
            rows = slice(bi * ln, (bi + 1) * ln)
            x1 = x_ref[bi] + mb[2:3] * jnp.concatenate([mix0[rows], mix1[rows]], axis=1)
            x1_ref[bi] = x1
            u2_b = _moe_input(x1, mb, g2_ref)
            _store_token_rows(u2_ref.at[bi], u2_b, ln)
            u2_rows.append(u2_b)
        valid = jnp.where(step > 1, 1.0, 0.0)
        yield from _route_stages(jnp.concatenate(u2_rows, axis=0), valid, wr_ref, br_ref, tri_ref,
                                 cls_ref, rank_ref, cnt_ref, run_ref)

    _round_robin([previous_chunk()] + [stream(st) for st in range(n_streams)],
                 background=projection(), every=PROJ_EVERY, start_after=n_streams + 1)
    for store in deferred_stores:
        store()
    for st in range(n_streams):
        m_ref[st] = m_out[st]
        n_ref[st] = n_out[st]


def _gate_columns(gt):
    lane = lax.broadcasted_iota(jnp.int32, gt.shape, 1)
    pos = lax.broadcasted_iota(jnp.int32, gt.shape, 0) % CHUNK
    b = jnp.minimum(gt, 0.0) - jnp.log(1.0 + jnp.exp(-jnp.abs(gt)))
    shift = 1
    while shift < CHUNK:
        b = b + jnp.where(pos >= shift, pltpu.roll(b, shift, axis=0), 0.0)
        shift *= 2
    imb = gt - pltpu.roll(b, LANES - N_HEADS, axis=1)
    cm = imb
    shift = 1
    while shift < CHUNK:
        cm = jnp.maximum(cm, jnp.where(pos >= shift, pltpu.roll(cm, shift, axis=0), -jnp.inf))
        shift *= 2
    cols = jnp.where(lane < N_HEADS, cm,
                     jnp.where(lane < 2 * N_HEADS, b,
                               jnp.where(lane < 3 * N_HEADS, pltpu.roll(imb, 2 * N_HEADS, axis=1), 0.0)))
    return cols, imb


def _mlstm_layer(x, mod, g1, w_in, bg, norm_g, w_out, g2, wr, br):
    nc = SEQ // CHUNK
    n_streams = BATCH * N_HEADS
    rows = BATCH * CHUNK
    newest =lambda c: jnp.minimum(c, nc - 1)
    oldest = lambda c: jnp.maximum(c - 2, 0)
    old_rows = lambda height, width: pl.BlockSpec((BATCH, height, width), lambda c: (0, oldest(c), 0))
    x3d = x.reshape(BATCH, SEQ, D_MODEL)
    x1, u2, cls, rank, cnt = pl.pallas_call(
        _mlstm_layer_kernel,
        out_shape=(
            jax.ShapeDtypeStruct((BATCH, SEQ, D_MODEL), F32),
            jax.ShapeDtypeStruct((BATCH, SEQ * ROW_TILE, LANES), F32),
            jax.ShapeDtypeStruct((nc, 1, rows), jnp.int32),
            jax.ShapeDtypeStruct((nc, 1, rows), jnp.int32),
            jax.ShapeDtypeStruct((CLASS_ROWS, LANES), jnp.int32),
        ),
        grid=(nc + 2,),
        in_specs=[
            pl.BlockSpec((BATCH, CHUNK, D_MODEL), lambda c: (0, newest(c), 0)),
            old_rows(CHUNK, D_MODEL),
            _const_spec((BATCH, 6, D_MODEL)),
            _const_spec((1, D_MODEL)),
            _const_spec(w_in.shape),
            _const_spec((1, GATE_COLS)),
            _const_spec((1, D_MODEL)),
            _const_spec((D_MODEL, D_MODEL)),
            _const_spec((1, D_MODEL)),
            _const_spec((D_MODEL, ROUTER_COLS)),
            _const_spec((N_EXPERTS, 1)),
            _const_spec((rows, rows)),
        ],
        out_specs=(
            old_rows(CHUNK, D_MODEL),
            old_rows(CHUNK * ROW_TILE, LANES),
            pl.BlockSpec((1, 1, rows), lambda c: (oldest(c), 0, 0)),
            pl.BlockSpec((1, 1, rows), lambda c: (oldest(c), 0, 0)),
            _const_spec((CLASS_ROWS, LANES)),
        ),
        scratch_shapes=[pltpu.VMEM((BATCH, CHUNK, QK), BF16),
                        pltpu.VMEM((BATCH, CHUNK, QK), BF16),
                        pltpu.VMEM((BATCH, CHUNK, D_MODEL), BF16),
                        pltpu.VMEM((BATCH, CHUNK, D_MODEL), BF16),
                        pltpu.VMEM((BATCH, CHUNK, GATE_WIDTH), F32),
                        pltpu.VMEM((BATCH, SUBLANES, CHUNK), F32),
                        pltpu.VMEM((n_streams, 1, LANES), F32),
                        pltpu.VMEM((n_streams, 1, DH_QK), F32),
                        pltpu.VMEM((rows, D_MODEL), BF16),
                        pltpu.VMEM((CLASS_ROWS, LANES), F32)]
        + [pltpu.VMEM((DH_QK, DH_V), F32) for _ in range(n_streams)],
        compiler_params=_params("arbitrary"),
        name="mlstm_layer",
    )(x3d, x3d, mod, g1, w_in, bg, norm_g, w_out, g2, wr, br, _strict_upper(rows))
    token_order = lambda a: a.reshape(nc, BATCH, CHUNK).transpose(1, 0, 2).reshape(TOKENS)
    return (x1.reshape(TOKENS, D_MODEL), u2.reshape(TOKENS * ROW_TILE, LANES), token_order(cls),
            token_order(rank), cnt)


def _top2_sum(v0, v1, v2, v3):
    hi1, lo1 = jnp.maximum(v0, v1), jnp.minimum(v0, v1)
    hi2, lo2 = jnp.maximum(v2, v3), jnp.minimum(v2, v3)
    return jnp.maximum(hi1, hi2) + jnp.maximum(jnp.minimum(hi1, hi2), jnp.maximum(lo1, lo2))


def _route_tail(x_new, m, valid, g2_ref, wr_ref, br_ref, tri_ref, u2_ref, cls_ref, rank_ref, cnt_ref,
                run_ref, after_router=None):
    u2 = _moe_input(x_new, m, g2_ref)
    _store_token_rows(u2_ref, u2, u2.shape[0])
    _route(u2, valid, wr_ref, br_ref, tri_ref, cls_ref, rank_ref, cnt_ref, run_ref, after_router)


def _moe_input(x_new, m, g2_ref):
    return _rms(x_new) * g2_ref[...] * (1.0 + m[4:5]) + m[3:4]


def _round_robin(chains, background=None, every=1, start_after=0):
    chains = list(chains)
    advanced = 0
    while chains:
        alive = []
        for chain in chains:
            try:
                next(chain)
            except StopIteration:
                continue
            alive.append(chain)
            advanced += 1
            if background is not None and advanced >= start_after and advanced % every == 0:
                next(background, None)
        chains = alive
    if background is not None:
        for _ in background:
            pass


def _route(u2, valid, wr_ref, br_ref, tri_ref, cls_ref, rank_ref, cnt_ref, run_ref, after_router=None):
    stages = _route_stages(u2, valid, wr_ref, br_ref, tri_ref, cls_ref, rank_ref, cnt_ref, run_ref)
    next(stages)
    if after_router is not None:
        after_router()
    for _ in stages:
        pass


def _route_stages(u2, valid, wr_ref, br_ref, tri_ref, cls_ref, rank_ref, cnt_ref, run_ref):
    logits = _dot(u2.astype(BF16), wr_ref[...])
    yield
    lt = logits.T[0:N_EXPERTS, :]
    e = jnp.exp(lt - jnp.max(lt, axis=0, keepdims=True))
    probs = e / jnp.sum(e, axis=0, keepdims=True)
    sel = probs + br_ref[...]
    sel_rows = [sel[j:j + 1, :] for j in range(N_EXPERTS)]
    best = jnp.zeros_like(sel_rows[0], dtype=jnp.int32)
    best_score = _top2_sum(*sel_rows[0:EXPERTS_PER_GROUP])
    for g in range(1, N_GROUPS):
        score = _top2_sum(*sel_rows[g * EXPERTS_PER_GROUP:(g + 1) * EXPERTS_PER_GROUP])
        better = score > best_score
        best = jnp.where(better, g, best)
        best_score = jnp.where(better, score, best_score)
    s = []
    for j in range(EXPERTS_PER_GROUP):
        sj = sel_rows[j]
        for g in range(1, N_GROUPS):
            sj = jnp.where(best == g, sel_rows[g * EXPERTS_PER_GROUP + j], sj)
        s.append(sj)
    chosen = []
    for j in range(EXPERTS_PER_GROUP):
        beaten = jnp.zeros_like(best)
        for i in range(EXPERTS_PER_GROUP):
            if i == j:
                continue
            wins = (s[i] >= s[j]) if i < j else (s[i] > s[j])
            beaten = beaten + jnp.where(wins, 1, 0)
        chosen.append(beaten < 2)
    pair = jnp.full_like(best, len(PAIRS) - 1)
    for p in range(len(PAIRS) - 2, -1, -1):
        a, b = PAIRS[p]
        pair = jnp.where(jnp.logical_and(chosen[a], chosen[b]), p, pair)
    cls = best * len(PAIRS) + pair
    cls_ref[0] = cls

    class_id = lax.broadcasted_iota(jnp.int32, (CLASS_ROWS, cls.shape[1]), 0)
    onehot = class_id == cls
    before = _dot(jnp.where(onehot, 1.0, 0.0).astype(BF16), tri_ref[...])
    run = run_ref[...]
    rank = jnp.sum(jnp.where(onehot, before + run[:, 0:1], 0.0), axis=0, keepdims=True)
    rank_ref[0] = rank.astype(jnp.int32)
    run = run + valid * jnp.sum(jnp.where(onehot, 1.0, 0.0), axis=1, keepdims=True)
    run_ref[...] = run
    cnt_ref[...] = run.astype(jnp.int32)


def _strict_upper(n):
    return (jnp.arange(n)[:, None] < jnp.arange(n)[None, :]).astype(BF16)


def _route_out_shapes(t, tm):
    return (
        jax.ShapeDtypeStruct((t, D_MODEL), F32),
        jax.ShapeDtypeStruct((t * ROW_TILE, LANES), F32),
        jax.ShapeDtypeStruct((t // tm, 1, tm), jnp.int32),
        jax.ShapeDtypeStruct((t // tm, 1, tm), jnp.int32),
        jax.ShapeDtypeStruct((CLASS_ROWS, LANES), jnp.int32),
    )


def _cur_tile(n_tiles):
    return lambda i: jnp.minimum(i, n_tiles - 1)


def _prev_tile(i):
    return jnp.maximum(i - 1, 0)


def _route_out_specs(tm, n_tiles):
    cur = _cur_tile(n_tiles)
    return (
        pl.BlockSpec((tm, D_MODEL), lambda i, *_: (cur(i), 0)),
        pl.BlockSpec((tm * ROW_TILE, LANES), lambda i, *_: (_prev_tile(i), 0)),
        pl.BlockSpec((1, 1, tm), lambda i, *_: (_prev_tile(i), 0, 0)),
        pl.BlockSpec((1, 1, tm), lambda i, *_: (_prev_tile(i), 0, 0)),
        pl.BlockSpec((CLASS_ROWS, LANES), lambda i, *_: (0, 0)),
    )


def _init_deferred_tail(xprev_ref, run_ref):
    @pl.when(pl.program_id(0) == 0)
    def _():
        xprev_ref[...] = jnp.zeros_like(xprev_ref)
        run_ref[...] = jnp.zeros_like(run_ref)


def _deferred_tail(xprev_ref, modp_ref, *tail_refs, after_router=None):
    valid = jnp.where(pl.program_id(0) > 0, 1.0, 0.0)
    _route_tail(xprev_ref[...], modp_ref[0], valid, *tail_refs, after_router=after_router)


def _const_spec(shape):
    return pl.BlockSpec(shape, lambda i, *_: (0,) * len(shape))


def _conv_layer_kernel(dest_ref, x_ref, ys_ref, mod0_ref, mod_ref, modp_ref, g1_ref, wi_ref, cw_ref,
                       cb_ref, wo_ref, g2_ref, wr_ref, br_ref, tri_ref,
                       x3_ref, u2_ref, cls_ref, rank_ref, cnt_ref,
                       run_ref, carry_ref, ybuf_ref, ysem_ref, xprev_ref):
    tm = x_ref.shape[0]
    tiles_per_batch = SEQ // tm
    n_tiles = pl.num_programs(0) - 1
    i = pl.program_id(0)

    @pl.when(i % tiles_per_batch == 0)
    def _():
        carry_ref[...] = jnp.zeros_like(carry_ref)

    _init_deferred_tail(xprev_ref, run_ref)
    m = mod_ref[0]
    y = _gathered_tile(dest_ref, ys_ref, ybuf_ref, ysem_ref, tm, issue_first=False, n_tiles=n_tiles)
    x2 = x_ref[...] + mod0_ref[0][5:6] * y
    u = (_rms(x2) * g1_ref[...] * (1.0 + m[1:2]) + m[0:1]).astype(BF16)
    bgate = _dot(u, wi_ref[:, 0:D_MODEL])
    gated = []

    def project_gated():
        gated.append(_dot(u, wi_ref[:, D_MODEL:2 * D_MODEL]) * _dot(u, wi_ref[:, 2 * D_MODEL:]))

    _deferred_tail(xprev_ref, modp_ref, g2_ref, wr_ref, br_ref, tri_ref, u2_ref, cls_ref, rank_ref,
                   cnt_ref, run_ref, after_router=project_gated)
    z = gated[0]
    prev = carry_ref[...]
    row = lax.broadcasted_iota(jnp.int32, z.shape, 0)
    z1 = jnp.where(row == 0, prev[7:8], pltpu.roll(z, 1, axis=0))
    z2 = jnp.where(row == 0, prev[6:7], jnp.where(row == 1, prev[7:8], pltpu.roll(z, 2, axis=0)))
    carry_ref[...] = z[tm - SUBLANES:, :]
    cw = cw_ref[...]
    zc = cw[0:1] * z2 + cw[1:2] * z1 + cw[2:3] * z + cb_ref[...]
    x3 = x2 + m[2:3] * _dot((bgate * zc).astype(BF16), wo_ref[...])
    xprev_ref[...] = x3

    @pl.when(i < n_tiles)
    def _():
        x3_ref[...] = x3

    _drain_row_gather(ys_ref, ybuf_ref, ysem_ref)


def _conv_layer(dest, x, ys, mod0, mod, g1, w_in, conv_w, conv_b, w_out, g2, wr, br):
    t = x.shape[0]
    tm = TM_CONV
    tri = _strict_upper(tm)
    tiles_per_batch = SEQ // tm
    n_tiles = t // tm
    cur = _cur_tile(n_tiles)
    mod_spec = pl.BlockSpec((1, 6, D_MODEL), lambda i, *_: (cur(i) // tiles_per_batch, 0, 0))
    modp_spec = pl.BlockSpec((1, 6, D_MODEL), lambda i, *_: (_prev_tile(i) // tiles_per_batch, 0, 0))
    return pl.pallas_call(
        _conv_layer_kernel,
        out_shape=_route_out_shapes(t, tm),
        grid_spec=pltpu.PrefetchScalarGridSpec(
            num_scalar_prefetch=1,
            grid=(n_tiles + 1,),
            in_specs=[
                pl.BlockSpec((tm, D_MODEL), lambda i, *_: (cur(i), 0)),
                pl.BlockSpec(memory_space=pl.ANY),
                mod_spec,
                mod_spec,
                modp_spec,
                _const_spec((1, D_MODEL)),
                _const_spec((D_MODEL, 3 * D_MODEL)),
                _const_spec((3, D_MODEL)),
                _const_spec((1, D_MODEL)),
                _const_spec((D_MODEL, D_MODEL)),
                _const_spec((1, D_MODEL)),
                _const_spec((D_MODEL, ROUTER_COLS)),
                _const_spec((N_EXPERTS, 1)),
                _const_spec((tm, tm)),
            ],
            out_specs=_route_out_specs(tm, n_tiles),
            scratch_shapes=[
                pltpu.VMEM((CLASS_ROWS, LANES), F32),
                pltpu.VMEM((SUBLANES, D_MODEL), F32),
                pltpu.VMEM((2, tm * ROW_TILE, LANES), F32),
                pltpu.SemaphoreType.DMA((2,)),
                pltpu.VMEM((tm, D_MODEL), F32),
            ],
        ),
        compiler_params=_params("arbitrary"),
        name="conv_layer_route",
    )(dest, x, ys, mod0, mod, mod, g1, w_in, conv_w, conv_b, w_out, g2, wr, br, tri)


def _invert_kernel(dest_ref, fill_ref, src_ref, sem):
    fill = pltpu.make_async_copy(fill_ref, src_ref, sem)
    fill.start()
    fill.wait()
    unroll = 8

    def body(t8, carry):
        for k in range(unroll):
            t = t8 * unroll + k
            src_ref[dest_ref[t]] = t
        return carry

    lax.fori_loop(0, dest_ref.shape[0] // unroll, body, 0)


def _invert_permutation(dest, n_out):
    smem = pl.BlockSpec(memory_space=pltpu.SMEM)
    return pl.pallas_call(
        _invert_kernel,
        out_shape=jax.ShapeDtypeStruct((n_out,), jnp.int32),
        in_specs=[smem, pl.BlockSpec(memory_space=pl.ANY)],
        out_specs=smem,
        scratch_shapes=[pltpu.SemaphoreType.DMA(())],
        name="invert_permutation",
    )(dest, jnp.arange(n_out, dtype=jnp.int32) % dest.shape[0])


def _expert_kernel(ea_ref, eb_ref, chg_ref, nused_ref, src_ref, u2_ref, wga_ref, wua_ref, wda_ref,
                   wgb_ref, wub_ref, wdb_ref, wra_ref, wrb_ref, y_ref, xbuf_ref, xsem_ref, *wbf_refs):
    del ea_ref, eb_ref
    j = pl.program_id(0)
    tm = TM_EXPERT

    @pl.when(chg_ref[j] == 1)
    def _():
        for src, dst in zip((wga_ref, wua_ref, wda_ref, wgb_ref, wub_ref, wdb_ref), wbf_refs):
            dst[...] = src[0, 0].astype(BF16)

    load_rows, start_next = _gathered_tile(src_ref, u2_ref, xbuf_ref, xsem_ref, tm, issue_first=None,
                                           n_active=nused_ref[0])

    @pl.when(j < nused_ref[0])
    def _():
        xb = load_rows().astype(BF16)
        start_next()
        dl = jnp.sum(xb.astype(F32) * (wra_ref[0] - wrb_ref[0]), axis=-1, keepdims=True)
        w_a = _sigmoid(dl)
        w_b = _sigmoid(-dl)

        def ffn(wg_ref, wu_ref, wd_ref):
            gate = _dot(xb, wg_ref[...])
            hidden = gate * _sigmoid(gate) * _dot(xb, wu_ref[...])
            return _dot(hidden.astype(BF16), wd_ref[...])

        y = w_a * ffn(*wbf_refs[0:3]) + w_b * ffn(*wbf_refs[3:6])
        _store_token_rows(y_ref, y, tm)

    @pl.when(j >= nused_ref[0])
    def _():
        y_ref[...] = jnp.zeros_like(y_ref)

    _drain_row_gather(u2_ref, xbuf_ref, xsem_ref, n_active=nused_ref[0])


def _experts(layer, ea, eb, chg, n_used, src, u2, w_gate, w_up, w_down, wr_rows):
    tm = TM_EXPERT
    sel_a = lambda j, ea, eb, *_: (layer, ea[j], 0, 0)
    sel_b = lambda j, ea, eb, *_: (layer, eb[j], 0, 0)
    up_spec = lambda sel: pl.BlockSpec((1, 1, D_MODEL, D_EXPERT), sel)
    down_spec = lambda sel: pl.BlockSpec((1, 1, D_EXPERT, D_MODEL), sel)
    wr_spec = lambda sel: pl.BlockSpec((1, 1, D_MODEL), lambda *a: sel(*a)[1:])
    up_scratch = pltpu.VMEM((D_MODEL, D_EXPERT), BF16)
    down_scratch = pltpu.VMEM((D_EXPERT, D_MODEL), BF16)
    return pl.pallas_call(
        _expert_kernel,
        out_shape=jax.ShapeDtypeStruct((PADDED_ROWS * ROW_TILE, LANES), F32),
        grid_spec=pltpu.PrefetchScalarGridSpec(
            num_scalar_prefetch=4,
            grid=(N_EXPERT_TILES,),
            in_specs=[
                pl.BlockSpec(memory_space=pltpu.SMEM),
                pl.BlockSpec(memory_space=pl.ANY),
                up_spec(sel_a), up_spec(sel_a), down_spec(sel_a),
                up_spec(sel_b), up_spec(sel_b), down_spec(sel_b),
                wr_spec(sel_a), wr_spec(sel_b),
            ],
            out_specs=pl.BlockSpec((tm * ROW_TILE, LANES), lambda j, *_: (j, 0)),
            scratch_shapes=[pltpu.VMEM((EXPERT_GATHER_BUFFERS, tm * ROW_TILE, LANES), F32),
                            pltpu.SemaphoreType.DMA((EXPERT_GATHER_BUFFERS,)),
                            up_scratch, up_scratch, down_scratch, up_scratch, up_scratch, down_scratch],
        ),
        compiler_params=_params("arbitrary"),
        name="grouped_experts",
    )(ea, eb, chg, n_used, src, u2, w_gate, w_up, w_down, w_gate, w_up, w_down, wr_rows, wr_rows)


def _final_kernel(dest_ref, x_ref, ys_ref, mod_ref, g_ref, o_ref, ybuf_ref, ysem_ref):
    y = _gathered_tile(dest_ref, ys_ref, ybuf_ref, ysem_ref, x_ref.shape[0], issue_first=True)
    x = x_ref[...] + mod_ref[0][5:6] * y
    o_ref[...] = _rms(x) * g_ref[...]
    _drain_row_gather(ys_ref, ybuf_ref, ysem_ref)


def _final(dest, x, ys, mod, g):
    t = x.shape[0]
    tm = TM_FINAL
    tiles_per_batch = SEQ // tm
    return pl.pallas_call(
        _final_kernel,
        out_shape=jax.ShapeDtypeStruct((t, D_MODEL), F32),
        grid_spec=pltpu.PrefetchScalarGridSpec(
            num_scalar_prefetch=1,
            grid=(t // tm,),
            in_specs=[
                pl.BlockSpec((tm, D_MODEL), lambda i, *_: (i, 0)),
                pl.BlockSpec(memory_space=pl.ANY),
                pl.BlockSpec((1, 6, D_MODEL), lambda i, *_: (i // tiles_per_batch, 0, 0)),
                _const_spec((1, D_MODEL)),
            ],
            out_specs=pl.BlockSpec((tm, D_MODEL), lambda i, *_: (i, 0)),
            scratch_shapes=[pltpu.VMEM((2, tm * ROW_TILE, LANES), F32), pltpu.SemaphoreType.DMA((2,))],
        ),
        compiler_params=_params("arbitrary"),
        name="final_norm",
    )(dest, x, ys, mod, g)


_PAIR_A = np.array([EXPERTS_PER_GROUP * (c // len(PAIRS)) + PAIRS[c % len(PAIRS)][0]
                    for c in range(N_CLASSES)], np.int32)
_PAIR_B = np.array([EXPERTS_PER_GROUP * (c // len(PAIRS)) + PAIRS[c % len(PAIRS)][1]
                    for c in range(N_CLASSES)], np.int32)


def _moe(layer, u2, cls, rank, cnt, w_gate, w_up, w_down, wr_rows):
    counts = cnt[:N_CLASSES, 0]
    tiles = (counts + TM_EXPERT - 1) // TM_EXPERT
    tile_end = jnp.cumsum(tiles)
    row_start = (tile_end - tiles) * TM_EXPERT
    n_used = tile_end[-1:]
    cls = cls.reshape(-1, LANES)
    dest = rank.reshape(-1, LANES)
    for c in range(N_CLASSES):
        dest = dest + jnp.where(cls == c, row_start[c], 0)
    dest = dest.reshape(-1)
    tile_cls = jnp.sum(jnp.arange(N_EXPERT_TILES)[:, None] >= tile_end[None, :], axis=1)
    tile_cls = jnp.minimum(tile_cls, tile_cls[jnp.maximum(n_used[0] - 1, 0)])
    ea = jnp.asarray(_PAIR_A)[tile_cls]
    eb = jnp.asarray(_PAIR_B)[tile_cls]
    chg = jnp.concatenate([jnp.ones((1,), jnp.int32),
                           (tile_cls[1:] != tile_cls[:-1]).astype(jnp.int32)])
    src = _invert_permutation(dest, PADDED_ROWS)
    ys = _experts(layer, ea, eb, chg, n_used.astype(jnp.int32), src, u2, w_gate, w_up, w_down, wr_rows)
    return dest, ys


def kernel(x, c, norm1_g, norm2_g, w_ada, b_ada, m_w_in, m_b_gates, m_norm_g, m_w_out,
           c_w_in, c_conv_w, c_conv_b, c_w_out, w_router, b_router,
           e_w_gate, e_w_up, e_w_down, final_g):
    xf = x.reshape(TOKENS, D_MODEL)
    mod = _ada(c, w_ada, b_ada)

    w_in = jnp.concatenate(
        [m_w_in[0].astype(BF16),
         jnp.zeros((D_MODEL, GATE_COLS - 2 * N_HEADS), BF16)], axis=1)
    bg = jnp.zeros((1, GATE_COLS), F32).at[0, :2 * N_HEADS].set(m_b_gates[0])
    wr = jnp.zeros((D_MODEL, ROUTER_COLS), BF16).at[:, :N_EXPERTS].set(w_router.astype(BF16))
    wr_rows = w_router.astype(BF16).astype(F32).T.reshape(N_EXPERTS, 1, D_MODEL)
    br = b_router.reshape(N_EXPERTS, 1)
    row = lambda v: v.reshape(1, -1)

    x1, u2, cls, rank, cnt = _mlstm_layer(xf, mod[0], row(norm1_g[0]), w_in, bg, row(m_norm_g[0]),
                                          m_w_out[0].astype(BF16), row(norm2_g[0]), wr, br)
    dest, ys = _moe(0, u2, cls, rank, cnt, e_w_gate, e_w_up, e_w_down, wr_rows)

    x3, u2, cls, rank, cnt = _conv_layer(
        dest, x1, ys, mod[0], mod[1], row(norm1_g[1]), c_w_in[0].astype(BF16), c_conv_w[0],
        row(c_conv_b[0]), c_w_out[0].astype(BF16), row(norm2_g[1]), wr, br)
    dest, ys = _moe(1, u2, cls, rank, cnt, e_w_gate, e_w_up, e_w_down, wr_rows)

    out = _final(dest, x3, ys, mod[1], row(final_g))
    return out.reshape(BATCH, SEQ, D_MODEL)
```

```python
import functools

import jax
import jax.numpy as jnp
import numpy as np
from jax import lax
from jax.experimental import pallas as pl
from jax.experimental.pallas import tpu as pltpu

F32 = jnp.float32
BF16 = jnp.bfloat16

D_MODEL = 1024
BATCH = 4
SEQ = 8192
TOKENS = BATCH * SEQ
N_HEADS = 4
DH_V = 256
DH_QK = 128
QK = N_HEADS * DH_QK
N_EXPERTS = 16
N_GROUPS = 4
EXPERTS_PER_GROUP = 4
D_EXPERT = 512
EPS = 1e-6

LANES = 128
SUBLANES = 8
VMEM_LIMIT_BYTES = 56 * 1024 * 1024

CHUNK = 128
TM_CONV = 512
TM_EXPERT = 256
GATE_COLS = LANES
GATE_WIDTH = 16
ROUTER_COLS = LANES
PAIRS = ((0, 1), (0, 2), (0, 3), (1, 2), (1, 3), (2, 3))
N_CLASSES = N_GROUPS * len(PAIRS)
CLASS_ROWS = 32
N_EXPERT_TILES = TOKENS // TM_EXPERT + N_CLASSES
PADDED_ROWS = N_EXPERT_TILES * TM_EXPERT
TM_FINAL = 512
EXPERT_GATHER_BUFFERS = 4
PROJ_COLS = 256
PROJ_EVERY = 1


def _params(*semantics):
    return pltpu.CompilerParams(dimension_semantics=semantics, vmem_limit_bytes=VMEM_LIMIT_BYTES)


def _dot(a, b):
    return jnp.dot(a, b, preferred_element_type=F32)


def _rms(x):
    return x * lax.rsqrt(jnp.mean(x * x, axis=-1, keepdims=True) + EPS)


def _sigmoid(x):
    return 1.0 / (1.0 + jnp.exp(-x))


ROW_TILE = D_MODEL // LANES


def _load_token_rows(ref, n):
    return jnp.concatenate([ref[pl.ds(c, n, stride=ROW_TILE), :] for c in range(ROW_TILE)], axis=1)


def _store_token_rows(ref, val, n):
    for c in range(ROW_TILE):
        ref[pl.ds(c, n, stride=ROW_TILE), :] = val[:, c * LANES:(c + 1) * LANES]


def _token_tile(ref, t):
    return ref.at[pl.ds(pl.multiple_of(t * ROW_TILE, ROW_TILE), ROW_TILE)]


def _start_row_gather(idx_ref, base, n, src_hbm, buf, sem):
    for r in range(n):
        pltpu.make_async_copy(_token_tile(src_hbm, idx_ref[base + r]), _token_tile(buf, r),
                              sem).start(priority=r % 2)


def _wait_row_gather(src_hbm, buf, sem):
    pltpu.make_async_copy(src_hbm.at[pl.ds(0, buf.shape[0])], buf, sem).wait()


def _gathered_tile(idx_ref, src_hbm, buf_ref, sem_ref, n, issue_first, n_tiles=None, n_active=None):
    i = pl.program_id(0)
    last = (pl.num_programs(0) if n_tiles is None else n_tiles) - 1
    nb = buf_ref.shape[0]
    ahead = nb - 1

    @pl.when(i == 0)
    def _():
        for k in range(ahead):
            _start_row_gather(idx_ref, jnp.minimum(k, last) * n, n, src_hbm, buf_ref.at[k], sem_ref.at[k])

    nxt = (i + ahead) % nb
    start_next = functools.partial(
        _start_row_gather, idx_ref, jnp.minimum(i + ahead, last) * n, n, src_hbm, buf_ref.at[nxt],
        sem_ref.at[nxt])
    if issue_first:
        start_next()
    slot = i % nb
    if n_active is None:
        _wait_row_gather(src_hbm, buf_ref.at[slot], sem_ref.at[slot])
    else:
        @pl.when(i < n_active + ahead)
        def _():
            _wait_row_gather(src_hbm, buf_ref.at[slot], sem_ref.at[slot])

    load_rows = functools.partial(_load_token_rows, buf_ref.at[slot], n)
    if issue_first is None:
        return load_rows, start_next
    rows = load_rows()
    if not issue_first:
        start_next()
    return rows


def _drain_row_gather(src_hbm, buf_ref, sem_ref, n_active=None):
    i = pl.program_id(0)
    nb = buf_ref.shape[0]

    @pl.when(i == pl.num_programs(0) - 1)
    def _():
        for k in range(1, nb):
            wait = functools.partial(_wait_row_gather, src_hbm, buf_ref.at[(i + k) % nb],
                                     sem_ref.at[(i + k) % nb])
            if n_active is None:
                wait()
            else:
                pl.when(i + k - (nb - 1) < n_active)(wait)


def _ada_kernel(c_ref, w_ref, b_ref, o_ref):
    c = c_ref[...]
    cond = c * _sigmoid(c)
    o_ref[0] = jnp.dot(cond, w_ref[0], preferred_element_type=F32,
                       precision=lax.Precision.HIGHEST) + b_ref[0]


def _ada(c, w_ada, b_ada):
    depth, d, n = w_ada.shape
    tn = 1536
    c8 = jnp.zeros((SUBLANES, d), F32).at[:BATCH].set(c)
    out = pl.pallas_call(
        _ada_kernel,
        out_shape=jax.ShapeDtypeStruct((depth, SUBLANES, n), F32),
        grid=(depth, n // tn),
        in_specs=[
            pl.BlockSpec((SUBLANES, d), lambda l, j: (0, 0)),
            pl.BlockSpec((1, d, tn), lambda l, j: (l, 0, j)),
            pl.BlockSpec((1, 1, tn), lambda l, j: (l, 0, j)),
        ],
        out_specs=pl.BlockSpec((1, SUBLANES, tn), lambda l, j: (l, 0, j)),
        compiler_params=_params("arbitrary", "arbitrary"),
        name="ada_mod",
    )(c8, w_ada, b_ada.reshape(depth, 1, n))
    return out[:, :BATCH].reshape(depth, BATCH, 6, d)


def _mlstm_layer_kernel(xn_ref, x_ref, mod_ref, g1_ref, wi_ref, bg_ref, ng_ref,
                        wo_ref, g2_ref, wr_ref, br_ref, tri_ref,
                        x1_ref, u2_ref, cls_ref, rank_ref, cnt_ref,
                        q_ref, k_ref, v_ref, og_ref, gcol_ref, grow_ref, m_ref, n_ref, h_ref, run_ref,
                        *c_refs):
    step = pl.program_id(0)

    @pl.when(step == 0)
    def _():
        for ref in (q_ref, k_ref, v_ref, og_ref, gcol_ref, grow_ref, m_ref, n_ref, h_ref, run_ref) + c_refs:
            ref[...] = jnp.zeros_like(ref)

    h_prev = h_ref[...]
    ln = CHUNK
    row = lax.broadcasted_iota(jnp.int32, (ln, ln), 0)
    col = lax.broadcasted_iota(jnp.int32, (ln, ln), 1)
    causal = col <= row
    n_streams = BATCH * N_HEADS
    m_all = [m_ref[st] for st in range(n_streams)]
    n_all = [n_ref[st] for st in range(n_streams)]
    m_out, n_out = [None] * n_streams, [None] * n_streams

    def stream(st):
        bi, h = divmod(st, N_HEADS)
        gates = gcol_ref[bi]
        column = lambda kind: jnp.broadcast_to(
            gates[:, kind * N_HEADS + h:kind * N_HEADS + h + 1], (ln, LANES))
        cm, b, imb_col = column(0), column(1), column(2)
        imb_row = grow_ref[bi, h:h + 1, :]
        qh = q_ref[bi, :, h * DH_QK:(h + 1) * DH_QK]
        kh = k_ref[bi, :, h * DH_QK:(h + 1) * DH_QK]
        vh = v_ref[bi, :, h * DH_V:(h + 1) * DH_V]
        c_ref = c_refs[st]
        c_prev = c_ref[...]
        m_prev = m_all[st]
        big_m = jnp.maximum(m_prev, cm)
        d_mat = jnp.exp(jnp.where(causal, imb_row - big_m, -jnp.inf))
        s_raw = lax.dot_general(qh, kh, (((1,), (1,)), ((), ())), preferred_element_type=F32)
        yield
        q_inter = qh.astype(F32) * jnp.exp(m_prev - big_m)
        s = s_raw * d_mat
        lhs = jnp.concatenate([q_inter.astype(BF16), s.astype(BF16)], axis=1)
        rhs = jnp.concatenate([c_prev.astype(BF16), vh], axis=0)
        num = _dot(lhs, rhs)
        den = jnp.sum(s + q_inter * n_all[st], axis=-1, keepdims=True)
        m_last = big_m[ln - 1:ln, :]
        kw = (kh.astype(F32) * jnp.exp(imb_col - m_last)).astype(BF16)
        update = lax.dot_general(kw, vh, (((0,), (0,)), ((), ())), preferred_element_type=F32)
        yield
        decay = jnp.exp(m_prev - m_last)
        c_ref[...] = jnp.concatenate([decay] * 2, axis=1) * c_prev + update
        n_out[st] = decay * n_all[st] + jnp.sum(kw.astype(F32), axis=0, keepdims=True)
        m_out[st] = b[ln - 1:ln, :] + m_last
        inv = 1.0 / jnp.maximum(jnp.abs(den), jnp.exp(-(b + big_m)))
        hh = num * jnp.concatenate([inv, inv], axis=1)
        sl = slice(h * DH_V, (h + 1) * DH_V)
        gate = _sigmoid(og_ref[bi, :, sl].astype(F32))
        h_ref[bi * ln:(bi + 1) * ln, sl] = (_rms(hh) * ng_ref[:, sl] * gate).astype(BF16)

    deferred_stores = []

    def projection():
        mod = mod_ref[...]
        u = jnp.concatenate(
            [_rms(xn_ref[bi]) * g1_ref[...] * (1.0 + mod[bi][1:2]) + mod[bi][0:1] for bi in range(BATCH)],
            axis=0).astype(BF16)
        gt = _dot(u, wi_ref[:, 2 * QK + 2 * D_MODEL:]) + bg_ref[...]
        cols, imb = _gate_columns(gt)
        imb_rows = imb.T[0:SUBLANES, :]
        for bi in range(BATCH):
            gcol_ref[bi] = cols[bi * ln:(bi + 1) * ln, 0:GATE_WIDTH]
            grow_ref[bi] = imb_rows[:, bi * ln:(bi + 1) * ln]
        yield
        lo = 0
        for ref, width, scale in ((q_ref, QK, DH_QK ** -0.5), (k_ref, QK, None), (v_ref, D_MODEL, None),
                                  (og_ref, D_MODEL, None)):
            for off in range(0, width, PROJ_COLS):
                part = _dot(u, wi_ref[:, lo + off:lo + off + PROJ_COLS])
                if scale is not None:
                    part = part * scale
                part = part.astype(BF16)

                def store(ref=ref, off=off, part=part):
                    for bi in range(BATCH):
                        ref[bi, :, off:off + PROJ_COLS] = part[bi * ln:(bi + 1) * ln]

                if ref is og_ref:
                    deferred_stores.append(store)
                else:
                    store()
                yield
            lo += width

    def previous_chunk():
        half = D_MODEL // 2
        mix0 = _dot(h_prev, wo_ref[:, 0:half])
        yield
        mix1 = _dot(h_prev, wo_ref[:, half:])
        yield
        mod = mod_ref[...]
        u2_rows = []
        for bi in range(BATCH):
            mb = mod[bi]
            rows = slice(bi * ln, (bi + 1) * ln)
            x1 = x_ref[bi] + mb[2:3] * jnp.concatenate([mix0[rows], mix1[rows]], axis=1)
            x1_ref[bi] = x1
            u2_b = _moe_input(x1, mb, g2_ref)
            _store_token_rows(u2_ref.at[bi], u2_b, ln)
            u2_rows.append(u2_b)
        valid = jnp.where(step > 1, 1.0, 0.0)
        yield from _route_stages(jnp.concatenate(u2_rows, axis=0), valid, wr_ref, br_ref, tri_ref,
                                 cls_ref, rank_ref, cnt_ref, run_ref)

    _round_robin([previous_chunk()] + [stream(st) for st in range(n_streams)],
                 background=projection(), every=PROJ_EVERY, start_after=n_streams + 1)
    for store in deferred_stores:
        store()
    for st in range(n_streams):
        m_ref[st] = m_out[st]
        n_ref[st] = n_out[st]


def _gate_columns(gt):
    lane = lax.broadcasted_iota(jnp.int32, gt.shape, 1)
    pos = lax.broadcasted_iota(jnp.int32, gt.shape, 0) % CHUNK
    b = jnp.minimum(gt, 0.0) - jnp.log(1.0 + jnp.exp(-jnp.abs(gt)))
    shift = 1
    while shift < CHUNK:
        b = b + jnp.where(pos >= shift, pltpu.roll(b, shift, axis=0), 0.0)
        shift *= 2
    imb = gt - pltpu.roll(b, LANES - N_HEADS, axis=1)
    cm = imb
    shift = 1
    while shift < CHUNK:
        cm = jnp.maximum(cm, jnp.where(pos >= shift, pltpu.roll(cm, shift, axis=0), -jnp.inf))
        shift *= 2
    cols = jnp.where(lane < N_HEADS, cm,
                     jnp.where(lane < 2 * N_HEADS, b,
                               jnp.where(lane < 3 * N_HEADS, pltpu.roll(imb, 2 * N_HEADS, axis=1), 0.0)))
    return cols, imb


def _mlstm_layer(x, mod, g1, w_in, bg, norm_g, w_out, g2, wr, br):
    nc = SEQ // CHUNK
    n_streams = BATCH * N_HEADS
    rows = BATCH * CHUNK
    newest =lambda c: jnp.minimum(c, nc - 1)
    oldest = lambda c: jnp.maximum(c - 2, 0)
    old_rows = lambda height, width: pl.BlockSpec((BATCH, height, width), lambda c: (0, oldest(c), 0))
    x3d = x.reshape(BATCH, SEQ, D_MODEL)
    x1, u2, cls, rank, cnt = pl.pallas_call(
        _mlstm_layer_kernel,
        out_shape=(
            jax.ShapeDtypeStruct((BATCH, SEQ, D_MODEL), F32),
            jax.ShapeDtypeStruct((BATCH, SEQ * ROW_TILE, LANES), F32),
            jax.ShapeDtypeStruct((nc, 1, rows), jnp.int32),
            jax.ShapeDtypeStruct((nc, 1, rows), jnp.int32),
            jax.ShapeDtypeStruct((CLASS_ROWS, LANES), jnp.int32),
        ),
        grid=(nc + 2,),
        in_specs=[
            pl.BlockSpec((BATCH, CHUNK, D_MODEL), lambda c: (0, newest(c), 0)),
            old_rows(CHUNK, D_MODEL),
            _const_spec((BATCH, 6, D_MODEL)),
            _const_spec((1, D_MODEL)),
            _const_spec(w_in.shape),
            _const_spec((1, GATE_COLS)),
            _const_spec((1, D_MODEL)),
            _const_spec((D_MODEL, D_MODEL)),
            _const_spec((1, D_MODEL)),
            _const_spec((D_MODEL, ROUTER_COLS)),
            _const_spec((N_EXPERTS, 1)),
            _const_spec((rows, rows)),
        ],
        out_specs=(
            old_rows(CHUNK, D_MODEL),
            old_rows(CHUNK * ROW_TILE, LANES),
            pl.BlockSpec((1, 1, rows), lambda c: (oldest(c), 0, 0)),
            pl.BlockSpec((1, 1, rows), lambda c: (oldest(c), 0, 0)),
            _const_spec((CLASS_ROWS, LANES)),
        ),
        scratch_shapes=[pltpu.VMEM((BATCH, CHUNK, QK), BF16),
                        pltpu.VMEM((BATCH, CHUNK, QK), BF16),
                        pltpu.VMEM((BATCH, CHUNK, D_MODEL), BF16),
                        pltpu.VMEM((BATCH, CHUNK, D_MODEL), BF16),
                        pltpu.VMEM((BATCH, CHUNK, GATE_WIDTH), F32),
                        pltpu.VMEM((BATCH, SUBLANES, CHUNK), F32),
                        pltpu.VMEM((n_streams, 1, LANES), F32),
                        pltpu.VMEM((n_streams, 1, DH_QK), F32),
                        pltpu.VMEM((rows, D_MODEL), BF16),
                        pltpu.VMEM((CLASS_ROWS, LANES), F32)]
        + [pltpu.VMEM((DH_QK, DH_V), F32) for _ in range(n_streams)],
        compiler_params=_params("arbitrary"),
        name="mlstm_layer",
    )(x3d, x3d, mod, g1, w_in, bg, norm_g, w_out, g2, wr, br, _strict_upper(rows))
    token_order = lambda a: a.reshape(nc, BATCH, CHUNK).transpose(1, 0, 2).reshape(TOKENS)
    return (x1.reshape(TOKENS, D_MODEL), u2.reshape(TOKENS * ROW_TILE, LANES), token_order(cls),
            token_order(rank), cnt)


def _top2_sum(v0, v1, v2, v3):
    hi1, lo1 = jnp.maximum(v0, v1), jnp.minimum(v0, v1)
    hi2, lo2 = jnp.maximum(v2, v3), jnp.minimum(v2, v3)
    return jnp.maximum(hi1, hi2) + jnp.maximum(jnp.minimum(hi1, hi2), jnp.maximum(lo1, lo2))


def _route_tail(x_new, m, valid, g2_ref, wr_ref, br_ref, tri_ref, u2_ref, cls_ref, rank_ref, cnt_ref,
                run_ref, after_router=None):
    u2 = _moe_input(x_new, m, g2_ref)
    _store_token_rows(u2_ref, u2, u2.shape[0])
    _route(u2, valid, wr_ref, br_ref, tri_ref, cls_ref, rank_ref, cnt_ref, run_ref, after_router)


def _moe_input(x_new, m, g2_ref):
    return _rms(x_new) * g2_ref[...] * (1.0 + m[4:5]) + m[3:4]


def _round_robin(chains, background=None, every=1, start_after=0):
    chains = list(chains)
    advanced = 0
    while chains:
        alive = []
        for chain in chains:
            try:
                next(chain)
            except StopIteration:
                continue
            alive.append(chain)
            advanced += 1
            if background is not None and advanced >= start_after and advanced % every == 0:
                next(background, None)
        chains = alive
    if background is not None:
        for _ in background:
            pass


def _route(u2, valid, wr_ref, br_ref, tri_ref, cls_ref, rank_ref, cnt_ref, run_ref, after_router=None):
    stages = _route_stages(u2, valid, wr_ref, br_ref, tri_ref, cls_ref, rank_ref, cnt_ref, run_ref)
    next(stages)
    if after_router is not None:
        after_router()
    for _ in stages:
        pass


def _route_stages(u2, valid, wr_ref, br_ref, tri_ref, cls_ref, rank_ref, cnt_ref, run_ref):
    logits = _dot(u2.astype(BF16), wr_ref[...])
    yield
    lt = logits.T[0:N_EXPERTS, :]
    e = jnp.exp(lt - jnp.max(lt, axis=0, keepdims=True))
    probs = e / jnp.sum(e, axis=0, keepdims=True)
    sel = probs + br_ref[...]
    sel_rows = [sel[j:j + 1, :] for j in range(N_EXPERTS)]
    best = jnp.zeros_like(sel_rows[0], dtype=jnp.int32)
    best_score = _top2_sum(*sel_rows[0:EXPERTS_PER_GROUP])
    for g in range(1, N_GROUPS):
        score = _top2_sum(*sel_rows[g * EXPERTS_PER_GROUP:(g + 1) * EXPERTS_PER_GROUP])
        better = score > best_score
        best = jnp.where(better, g, best)
        best_score = jnp.where(better, score, best_score)
    s = []
    for j in range(EXPERTS_PER_GROUP):
        sj = sel_rows[j]
        for g in range(1, N_GROUPS):
            sj = jnp.where(best == g, sel_rows[g * EXPERTS_PER_GROUP + j], sj)
        s.append(sj)
    chosen = []
    for j in range(EXPERTS_PER_GROUP):
        beaten = jnp.zeros_like(best)
        for i in range(EXPERTS_PER_GROUP):
            if i == j:
                continue
            wins = (s[i] >= s[j]) if i < j else (s[i] > s[j])
            beaten = beaten + jnp.where(wins, 1, 0)
        chosen.append(beaten < 2)
    pair = jnp.full_like(best, len(PAIRS) - 1)
    for p in range(len(PAIRS) - 2, -1, -1):
        a, b = PAIRS[p]
        pair = jnp.where(jnp.logical_and(chosen[a], chosen[b]), p, pair)
    cls = best * len(PAIRS) + pair
    cls_ref[0] = cls

    class_id = lax.broadcasted_iota(jnp.int32, (CLASS_ROWS, cls.shape[1]), 0)
    onehot = class_id == cls
    before = _dot(jnp.where(onehot, 1.0, 0.0).astype(BF16), tri_ref[...])
    run = run_ref[...]
    rank = jnp.sum(jnp.where(onehot, before + run[:, 0:1], 0.0), axis=0, keepdims=True)
    rank_ref[0] = rank.astype(jnp.int32)
    run = run + valid * jnp.sum(jnp.where(onehot, 1.0, 0.0), axis=1, keepdims=True)
    run_ref[...] = run
    cnt_ref[...] = run.astype(jnp.int32)


def _strict_upper(n):
    return (jnp.arange(n)[:, None] < jnp.arange(n)[None, :]).astype(BF16)


def _route_out_shapes(t, tm):
    return (
        jax.ShapeDtypeStruct((t, D_MODEL), F32),
        jax.ShapeDtypeStruct((t * ROW_TILE, LANES), F32),
        jax.ShapeDtypeStruct((t // tm, 1, tm), jnp.int32),
        jax.ShapeDtypeStruct((t // tm, 1, tm), jnp.int32),
        jax.ShapeDtypeStruct((CLASS_ROWS, LANES), jnp.int32),
    )


def _cur_tile(n_tiles):
    return lambda i: jnp.minimum(i, n_tiles - 1)


def _prev_tile(i):
    return jnp.maximum(i - 1, 0)


def _route_out_specs(tm, n_tiles):
    cur = _cur_tile(n_tiles)
    return (
        pl.BlockSpec((tm, D_MODEL), lambda i, *_: (cur(i), 0)),
        pl.BlockSpec((tm * ROW_TILE, LANES), lambda i, *_: (_prev_tile(i), 0)),
        pl.BlockSpec((1, 1, tm), lambda i, *_: (_prev_tile(i), 0, 0)),
        pl.BlockSpec((1, 1, tm), lambda i, *_: (_prev_tile(i), 0, 0)),
        pl.BlockSpec((CLASS_ROWS, LANES), lambda i, *_: (0, 0)),
    )


def _init_deferred_tail(xprev_ref, run_ref):
    @pl.when(pl.program_id(0) == 0)
    def _():
        xprev_ref[...] = jnp.zeros_like(xprev_ref)
        run_ref[...] = jnp.zeros_like(run_ref)


def _deferred_tail(xprev_ref, modp_ref, *tail_refs, after_router=None):
    valid = jnp.where(pl.program_id(0) > 0, 1.0, 0.0)
    _route_tail(xprev_ref[...], modp_ref[0], valid, *tail_refs, after_router=after_router)


def _const_spec(shape):
    return pl.BlockSpec(shape, lambda i, *_: (0,) * len(shape))


def _conv_layer_kernel(dest_ref, x_ref, ys_ref, mod0_ref, mod_ref, modp_ref, g1_ref, wi_ref, cw_ref,
                       cb_ref, wo_ref, g2_ref, wr_ref, br_ref, tri_ref,
                       x3_ref, u2_ref, cls_ref, rank_ref, cnt_ref,
                       run_ref, carry_ref, ybuf_ref, ysem_ref, xprev_ref):
    tm = x_ref.shape[0]
    tiles_per_batch = SEQ // tm
    n_tiles = pl.num_programs(0) - 1
    i = pl.program_id(0)

    @pl.when(i % tiles_per_batch == 0)
    def _():
        carry_ref[...] = jnp.zeros_like(carry_ref)

    _init_deferred_tail(xprev_ref, run_ref)
    m = mod_ref[0]
    y = _gathered_tile(dest_ref, ys_ref, ybuf_ref, ysem_ref, tm, issue_first=False, n_tiles=n_tiles)
    x2 = x_ref[...] + mod0_ref[0][5:6] * y
    u = (_rms(x2) * g1_ref[...] * (1.0 + m[1:2]) + m[0:1]).astype(BF16)
    bgate = _dot(u, wi_ref[:, 0:D_MODEL])
    gated = []

    def project_gated():
        gated.append(_dot(u, wi_ref[:, D_MODEL:2 * D_MODEL]) * _dot(u, wi_ref[:, 2 * D_MODEL:]))

    _deferred_tail(xprev_ref, modp_ref, g2_ref, wr_ref, br_ref, tri_ref, u2_ref, cls_ref, rank_ref,
                   cnt_ref, run_ref, after_router=project_gated)
    z = gated[0]
    prev = carry_ref[...]
    row = lax.broadcasted_iota(jnp.int32, z.shape, 0)
    z1 = jnp.where(row == 0, prev[7:8], pltpu.roll(z, 1, axis=0))
    z2 = jnp.where(row == 0, prev[6:7], jnp.where(row == 1, prev[7:8], pltpu.roll(z, 2, axis=0)))
    carry_ref[...] = z[tm - SUBLANES:, :]
    cw = cw_ref[...]
    zc = cw[0:1] * z2 + cw[1:2] * z1 + cw[2:3] * z + cb_ref[...]
    x3 = x2 + m[2:3] * _dot((bgate * zc).astype(BF16), wo_ref[...])
    xprev_ref[...] = x3

    @pl.when(i < n_tiles)
    def _():
        x3_ref[...] = x3

    _drain_row_gather(ys_ref, ybuf_ref, ysem_ref)


def _conv_layer(dest, x, ys, mod0, mod, g1, w_in, conv_w, conv_b, w_out, g2, wr, br):
    t = x.shape[0]
    tm = TM_CONV
    tri = _strict_upper(tm)
    tiles_per_batch = SEQ // tm
    n_tiles = t // tm
    cur = _cur_tile(n_tiles)
    mod_spec = pl.BlockSpec((1, 6, D_MODEL), lambda i, *_: (cur(i) // tiles_per_batch, 0, 0))
    modp_spec = pl.BlockSpec((1, 6, D_MODEL), lambda i, *_: (_prev_tile(i) // tiles_per_batch, 0, 0))
    return pl.pallas_call(
        _conv_layer_kernel,
        out_shape=_route_out_shapes(t, tm),
        grid_spec=pltpu.PrefetchScalarGridSpec(
            num_scalar_prefetch=1,
            grid=(n_tiles + 1,),
            in_specs=[
                pl.BlockSpec((tm, D_MODEL), lambda i, *_: (cur(i), 0)),
                pl.BlockSpec(memory_space=pl.ANY),
                mod_spec,
                mod_spec,
                modp_spec,
                _const_spec((1, D_MODEL)),
                _const_spec((D_MODEL, 3 * D_MODEL)),
                _const_spec((3, D_MODEL)),
                _const_spec((1, D_MODEL)),
                _const_spec((D_MODEL, D_MODEL)),
                _const_spec((1, D_MODEL)),
                _const_spec((D_MODEL, ROUTER_COLS)),
                _const_spec((N_EXPERTS, 1)),
                _const_spec((tm, tm)),
            ],
            out_specs=_route_out_specs(tm, n_tiles),
            scratch_shapes=[
                pltpu.VMEM((CLASS_ROWS, LANES), F32),
                pltpu.VMEM((SUBLANES, D_MODEL), F32),
                pltpu.VMEM((2, tm * ROW_TILE, LANES), F32),
                pltpu.SemaphoreType.DMA((2,)),
                pltpu.VMEM((tm, D_MODEL), F32),
            ],
        ),
        compiler_params=_params("arbitrary"),
        name="conv_layer_route",
    )(dest, x, ys, mod0, mod, mod, g1, w_in, conv_w, conv_b, w_out, g2, wr, br, tri)


def _invert_kernel(dest_ref, fill_ref, src_ref, sem):
    fill = pltpu.make_async_copy(fill_ref, src_ref, sem)
    fill.start()
    fill.wait()
    unroll = 8

    def body(t8, carry):
        for k in range(unroll):
            t = t8 * unroll + k
            src_ref[dest_ref[t]] = t
        return carry

    lax.fori_loop(0, dest_ref.shape[0] // unroll, body, 0)


def _invert_permutation(dest, n_out):
    smem = pl.BlockSpec(memory_space=pltpu.SMEM)
    return pl.pallas_call(
        _invert_kernel,
        out_shape=jax.ShapeDtypeStruct((n_out,), jnp.int32),
        in_specs=[smem, pl.BlockSpec(memory_space=pl.ANY)],
        out_specs=smem,
        scratch_shapes=[pltpu.SemaphoreType.DMA(())],
        name="invert_permutation",
    )(dest, jnp.arange(n_out, dtype=jnp.int32) % dest.shape[0])


def _expert_kernel(ea_ref, eb_ref, chg_ref, nused_ref, src_ref, u2_ref, wga_ref, wua_ref, wda_ref,
                   wgb_ref, wub_ref, wdb_ref, wra_ref, wrb_ref, y_ref, xbuf_ref, xsem_ref, *wbf_refs):
    del ea_ref, eb_ref
    j = pl.program_id(0)
    tm = TM_EXPERT

    @pl.when(chg_ref[j] == 1)
    def _():
        for src, dst in zip((wga_ref, wua_ref, wda_ref, wgb_ref, wub_ref, wdb_ref), wbf_refs):
            dst[...] = src[0, 0].astype(BF16)

    load_rows, start_next = _gathered_tile(src_ref, u2_ref, xbuf_ref, xsem_ref, tm, issue_first=None,
                                           n_active=nused_ref[0])

    @pl.when(j < nused_ref[0])
    def _():
        xb = load_rows().astype(BF16)
        start_next()
        dl = jnp.sum(xb.astype(F32) * (wra_ref[0] - wrb_ref[0]), axis=-1, keepdims=True)
        w_a = _sigmoid(dl)
        w_b = _sigmoid(-dl)

        def ffn(wg_ref, wu_ref, wd_ref):
            gate = _dot(xb, wg_ref[...])
            hidden = gate * _sigmoid(gate) * _dot(xb, wu_ref[...])
            return _dot(hidden.astype(BF16), wd_ref[...])

        y = w_a * ffn(*wbf_refs[0:3]) + w_b * ffn(*wbf_refs[3:6])
        _store_token_rows(y_ref, y, tm)

    @pl.when(j >= nused_ref[0])
    def _():
        y_ref[...] = jnp.zeros_like(y_ref)

    _drain_row_gather(u2_ref, xbuf_ref, xsem_ref, n_active=nused_ref[0])


def _experts(layer, ea, eb, chg, n_used, src, u2, w_gate, w_up, w_down, wr_rows):
    tm = TM_EXPERT
    sel_a = lambda j, ea, eb, *_: (layer, ea[j], 0, 0)
    sel_b = lambda j, ea, eb, *_: (layer, eb[j], 0, 0)
    up_spec = lambda sel: pl.BlockSpec((1, 1, D_MODEL, D_EXPERT), sel)
    down_spec = lambda sel: pl.BlockSpec((1, 1, D_EXPERT, D_MODEL), sel)
    wr_spec = lambda sel: pl.BlockSpec((1, 1, D_MODEL), lambda *a: sel(*a)[1:])
    up_scratch = pltpu.VMEM((D_MODEL, D_EXPERT), BF16)
    down_scratch = pltpu.VMEM((D_EXPERT, D_MODEL), BF16)
    return pl.pallas_call(
        _expert_kernel,
        out_shape=jax.ShapeDtypeStruct((PADDED_ROWS * ROW_TILE, LANES), F32),
        grid_spec=pltpu.PrefetchScalarGridSpec(
            num_scalar_prefetch=4,
            grid=(N_EXPERT_TILES,),
            in_specs=[
                pl.BlockSpec(memory_space=pltpu.SMEM),
                pl.BlockSpec(memory_space=pl.ANY),
                up_spec(sel_a), up_spec(sel_a), down_spec(sel_a),
                up_spec(sel_b), up_spec(sel_b), down_spec(sel_b),
                wr_spec(sel_a), wr_spec(sel_b),
            ],
            out_specs=pl.BlockSpec((tm * ROW_TILE, LANES), lambda j, *_: (j, 0)),
            scratch_shapes=[pltpu.VMEM((EXPERT_GATHER_BUFFERS, tm * ROW_TILE, LANES), F32),
                            pltpu.SemaphoreType.DMA((EXPERT_GATHER_BUFFERS,)),
                            up_scratch, up_scratch, down_scratch, up_scratch, up_scratch, down_scratch],
        ),
        compiler_params=_params("arbitrary"),
        name="grouped_experts",
    )(ea, eb, chg, n_used, src, u2, w_gate, w_up, w_down, w_gate, w_up, w_down, wr_rows, wr_rows)


def _final_kernel(dest_ref, x_ref, ys_ref, mod_ref, g_ref, o_ref, ybuf_ref, ysem_ref):
    y = _gathered_tile(dest_ref, ys_ref, ybuf_ref, ysem_ref, x_ref.shape[0], issue_first=True)
    x = x_ref[...] + mod_ref[0][5:6] * y
    o_ref[...] = _rms(x) * g_ref[...]
    _drain_row_gather(ys_ref, ybuf_ref, ysem_ref)


def _final(dest, x, ys, mod, g):
    t = x.shape[0]
    tm = TM_FINAL
    tiles_per_batch = SEQ // tm
    return pl.pallas_call(
        _final_kernel,
        out_shape=jax.ShapeDtypeStruct((t, D_MODEL), F32),
        grid_spec=pltpu.PrefetchScalarGridSpec(
            num_scalar_prefetch=1,
            grid=(t // tm,),
            in_specs=[
                pl.BlockSpec((tm, D_MODEL), lambda i, *_: (i, 0)),
                pl.BlockSpec(memory_space=pl.ANY),
                pl.BlockSpec((1, 6, D_MODEL), lambda i, *_: (i // tiles_per_batch, 0, 0)),
                _const_spec((1, D_MODEL)),
            ],
            out_specs=pl.BlockSpec((tm, D_MODEL), lambda i, *_: (i, 0)),
            scratch_shapes=[pltpu.VMEM((2, tm * ROW_TILE, LANES), F32), pltpu.SemaphoreType.DMA((2,))],
        ),
        compiler_params=_params("arbitrary"),
        name="final_norm",
    )(dest, x, ys, mod, g)


_PAIR_A = np.array([EXPERTS_PER_GROUP * (c // len(PAIRS)) + PAIRS[c % len(PAIRS)][0]
                    for c in range(N_CLASSES)], np.int32)
_PAIR_B = np.array([EXPERTS_PER_GROUP * (c // len(PAIRS)) + PAIRS[c % len(PAIRS)][1]
                    for c in range(N_CLASSES)], np.int32)


def _moe(layer, u2, cls, rank, cnt, w_gate, w_up, w_down, wr_rows):
    counts = cnt[:N_CLASSES, 0]
    tiles = (counts + TM_EXPERT - 1) // TM_EXPERT
    tile_end = jnp.cumsum(tiles)
    row_start = (tile_end - tiles) * TM_EXPERT
    n_used = tile_end[-1:]
    cls = cls.reshape(-1, LANES)
    dest = rank.reshape(-1, LANES)
    for c in range(N_CLASSES):
        dest = dest + jnp.where(cls == c, row_start[c], 0)
    dest = dest.reshape(-1)
    tile_cls = jnp.sum(jnp.arange(N_EXPERT_TILES)[:, None] >= tile_end[None, :], axis=1)
    tile_cls = jnp.minimum(tile_cls, tile_cls[jnp.maximum(n_used[0] - 1, 0)])
    ea = jnp.asarray(_PAIR_A)[tile_cls]
    eb = jnp.asarray(_PAIR_B)[tile_cls]
    chg = jnp.concatenate([jnp.ones((1,), jnp.int32),
                           (tile_cls[1:] != tile_cls[:-1]).astype(jnp.int32)])
    src = _invert_permutation(dest, PADDED_ROWS)
    ys = _experts(layer, ea, eb, chg, n_used.astype(jnp.int32), src, u2, w_gate, w_up, w_down, wr_rows)
    return dest, ys


def kernel(x, c, norm1_g, norm2_g, w_ada, b_ada, m_w_in, m_b_gates, m_norm_g, m_w_out,
           c_w_in, c_conv_w, c_conv_b, c_w_out, w_router, b_router,
           e_w_gate, e_w_up, e_w_down, final_g):
    xf = x.reshape(TOKENS, D_MODEL)
    mod = _ada(c, w_ada, b_ada)

    w_in = jnp.concatenate(
        [m_w_in[0].astype(BF16),
         jnp.zeros((D_MODEL, GATE_COLS - 2 * N_HEADS), BF16)], axis=1)
    bg = jnp.zeros((1, GATE_COLS), F32).at[0, :2 * N_HEADS].set(m_b_gates[0])
    wr = jnp.zeros((D_MODEL, ROUTER_COLS), BF16).at[:, :N_EXPERTS].set(w_router.astype(BF16))
    wr_rows = w_router.astype(BF16).astype(F32).T.reshape(N_EXPERTS, 1, D_MODEL)
    br = b_router.reshape(N_EXPERTS, 1)
    row = lambda v: v.reshape(1, -1)

    x1, u2, cls, rank, cnt = _mlstm_layer(xf, mod[0], row(norm1_g[0]), w_in, bg, row(m_norm_g[0]),
                                          m_w_out[0].astype(BF16), row(norm2_g[0]), wr, br)
    dest, ys = _moe(0, u2, cls, rank, cnt, e_w_gate, e_w_up, e_w_down, wr_rows)

    x3, u2, cls, rank, cnt = _conv_layer(
        dest, x1, ys, mod[0], mod[1], row(norm1_g[1]), c_w_in[0].astype(BF16), c_conv_w[0],
        row(c_conv_b[0]), c_w_out[0].astype(BF16), row(norm2_g[1]), wr, br)
    dest, ys = _moe(1, u2, cls, rank, cnt, e_w_gate, e_w_up, e_w_down, wr_rows)

    out = _final(dest, x3, ys, mod[1], row(final_g))
    return out.reshape(BATCH, SEQ, D_MODEL)
```

```python
import functools

import jax
import jax.numpy as jnp
import numpy as np
from jax import lax
from jax.experimental import pallas as pl
from jax.experimental.pallas import tpu as pltpu

F32 = jnp.float32
BF16 = jnp.bfloat16

D_MODEL = 1024
BATCH = 4
SEQ = 8192
TOKENS = BATCH * SEQ
N_HEADS = 4
DH_V = 256
DH_QK = 128
QK = N_HEADS * DH_QK
N_EXPERTS = 16
N_GROUPS = 4
EXPERTS_PER_GROUP = 4
D_EXPERT = 512
EPS = 1e-6

LANES = 128
SUBLANES = 8
VMEM_LIMIT_BYTES = 56 * 1024 * 1024

CHUNK = 128
TM_CONV = 512
TM_EXPERT = 512
GATE_COLS = LANES
GATE_WIDTH = 16
ROUTER_COLS = LANES
PAIRS = ((0, 1), (0, 2), (0, 3), (1, 2), (1, 3), (2, 3))
N_CLASSES = N_GROUPS * len(PAIRS)
CLASS_ROWS = 32
N_EXPERT_TILES = TOKENS // TM_EXPERT + N_CLASSES
PADDED_ROWS = N_EXPERT_TILES * TM_EXPERT
TM_FINAL = 512
EXPERT_GATHER_BUFFERS = 4
PROJ_COLS = 256
PROJ_EVERY = 1


def _params(*semantics):
    return pltpu.CompilerParams(dimension_semantics=semantics, vmem_limit_bytes=VMEM_LIMIT_BYTES)


def _dot(a, b):
    return jnp.dot(a, b, preferred_element_type=F32)


def _rms(x):
    return x * lax.rsqrt(jnp.mean(x * x, axis=-1, keepdims=True) + EPS)


def _sigmoid(x):
    return 1.0 / (1.0 + jnp.exp(-x))


ROW_TILE = D_MODEL // LANES


def _load_token_rows(ref, n):
    return jnp.concatenate([ref[pl.ds(c, n, stride=ROW_TILE), :] for c in range(ROW_TILE)], axis=1)


def _store_token_rows(ref, val, n):
    for c in range(ROW_TILE):
        ref[pl.ds(c, n, stride=ROW_TILE), :] = val[:, c * LANES:(c + 1) * LANES]


def _token_tile(ref, t):
    return ref.at[pl.ds(pl.multiple_of(t * ROW_TILE, ROW_TILE), ROW_TILE)]


def _start_row_gather(idx_ref, base, n, src_hbm, buf, sem):
    for r in range(n):
        pltpu.make_async_copy(_token_tile(src_hbm, idx_ref[base + r]), _token_tile(buf, r),
                              sem).start(priority=r % 2)


def _wait_row_gather(src_hbm, buf, sem):
    pltpu.make_async_copy(src_hbm.at[pl.ds(0, buf.shape[0])], buf, sem).wait()


def _gathered_tile(idx_ref, src_hbm, buf_ref, sem_ref, n, issue_first, n_tiles=None, n_active=None):
    i = pl.program_id(0)
    last = (pl.num_programs(0) if n_tiles is None else n_tiles) - 1
    nb = buf_ref.shape[0]
    ahead = nb - 1

    @pl.when(i == 0)
    def _():
        for k in range(ahead):
            _start_row_gather(idx_ref, jnp.minimum(k, last) * n, n, src_hbm, buf_ref.at[k], sem_ref.at[k])

    nxt = (i + ahead) % nb
    start_next = functools.partial(
        _start_row_gather, idx_ref, jnp.minimum(i + ahead, last) * n, n, src_hbm, buf_ref.at[nxt],
        sem_ref.at[nxt])
    if issue_first:
        start_next()
    slot = i % nb
    if n_active is None:
        _wait_row_gather(src_hbm, buf_ref.at[slot], sem_ref.at[slot])
    else:
        @pl.when(i < n_active + ahead)
        def _():
            _wait_row_gather(src_hbm, buf_ref.at[slot], sem_ref.at[slot])

    load_rows = functools.partial(_load_token_rows, buf_ref.at[slot], n)
    if issue_first is None:
        return load_rows, start_next
    rows = load_rows()
    if not issue_first:
        start_next()
    return rows


def _drain_row_gather(src_hbm, buf_ref, sem_ref, n_active=None):
    i = pl.program_id(0)
    nb = buf_ref.shape[0]

    @pl.when(i == pl.num_programs(0) - 1)
    def _():
        for k in range(1, nb):
            wait = functools.partial(_wait_row_gather, src_hbm, buf_ref.at[(i + k) % nb],
                                     sem_ref.at[(i + k) % nb])
            if n_active is None:
                wait()
            else:
                pl.when(i + k - (nb - 1) < n_active)(wait)


def _ada_kernel(c_ref, w_ref, b_ref, o_ref):
    c = c_ref[...]
    cond = c * _sigmoid(c)
    o_ref[0] = jnp.dot(cond, w_ref[0], preferred_element_type=F32,
                       precision=lax.Precision.HIGHEST) + b_ref[0]


def _ada(c, w_ada, b_ada):
    depth, d, n = w_ada.shape
    tn = 1536
    c8 = jnp.zeros((SUBLANES, d), F32).at[:BATCH].set(c)
    out = pl.pallas_call(
        _ada_kernel,
        out_shape=jax.ShapeDtypeStruct((depth, SUBLANES, n), F32),
        grid=(depth, n // tn),
        in_specs=[
            pl.BlockSpec((SUBLANES, d), lambda l, j: (0, 0)),
            pl.BlockSpec((1, d, tn), lambda l, j: (l, 0, j)),
            pl.BlockSpec((1, 1, tn), lambda l, j: (l, 0, j)),
        ],
        out_specs=pl.BlockSpec((1, SUBLANES, tn), lambda l, j: (l, 0, j)),
        compiler_params=_params("arbitrary", "arbitrary"),
        name="ada_mod",
    )(c8, w_ada, b_ada.reshape(depth, 1, n))
    return out[:, :BATCH].reshape(depth, BATCH, 6, d)


def _mlstm_layer_kernel(xn_ref, x_ref, mod_ref, g1_ref, wi_ref, bg_ref, ng_ref,
                        wo_ref, g2_ref, wr_ref, br_ref, tri_ref,
                        x1_ref, u2_ref, cls_ref, rank_ref, cnt_ref,
                        q_ref, k_ref, v_ref, og_ref, gcol_ref, grow_ref, m_ref, n_ref, h_ref, run_ref,
                        *c_refs):
    step = pl.program_id(0)

    @pl.when(step == 0)
    def _():
        for ref in (q_ref, k_ref, v_ref, og_ref, gcol_ref, grow_ref, m_ref, n_ref, h_ref, run_ref) + c_refs:
            ref[...] = jnp.zeros_like(ref)

    h_prev = h_ref[...]
    ln = CHUNK
    row = lax.broadcasted_iota(jnp.int32, (ln, ln), 0)
    col = lax.broadcasted_iota(jnp.int32, (ln, ln), 1)
    causal = col <= row
    n_streams = BATCH * N_HEADS
    m_all = [m_ref[st] for st in range(n_streams)]
    n_all = [n_ref[st] for st in range(n_streams)]
    m_out, n_out = [None] * n_streams, [None] * n_streams

    def stream(st):
        bi, h = divmod(st, N_HEADS)
        gates = gcol_ref[bi]
        column = lambda kind: jnp.broadcast_to(
            gates[:, kind * N_HEADS + h:kind * N_HEADS + h + 1], (ln, LANES))
        cm, b, imb_col = column(0), column(1), column(2)
        imb_row = grow_ref[bi, h:h + 1, :]
        qh = q_ref[bi, :, h * DH_QK:(h + 1) * DH_QK]
        kh = k_ref[bi, :, h * DH_QK:(h + 1) * DH_QK]
        vh = v_ref[bi, :, h * DH_V:(h + 1) * DH_V]
        c_ref = c_refs[st]
        c_prev = c_ref[...]
        m_prev = m_all[st]
        big_m = jnp.maximum(m_prev, cm)
        d_mat = jnp.exp(jnp.where(causal, imb_row - big_m, -jnp.inf))
        s_raw = lax.dot_general(qh, kh, (((1,), (1,)), ((), ())), preferred_element_type=F32)
        yield
        q_inter = qh.astype(F32) * jnp.exp(m_prev - big_m)
        s = s_raw * d_mat
        lhs = jnp.concatenate([q_inter.astype(BF16), s.astype(BF16)], axis=1)
        rhs = jnp.concatenate([c_prev.astype(BF16), vh], axis=0)
        num = _dot(lhs, rhs)
        den = jnp.sum(s + q_inter * n_all[st], axis=-1, keepdims=True)
        m_last = big_m[ln - 1:ln, :]
        kw = (kh.astype(F32) * jnp.exp(imb_col - m_last)).astype(BF16)
        update = lax.dot_general(kw, vh, (((0,), (0,)), ((), ())), preferred_element_type=F32)
        yield
        decay = jnp.exp(m_prev - m_last)
        c_ref[...] = jnp.concatenate([decay] * 2, axis=1) * c_prev + update
        n_out[st] = decay * n_all[st] + jnp.sum(kw.astype(F32), axis=0, keepdims=True)
        m_out[st] = b[ln - 1:ln, :] + m_last
        inv = 1.0 / jnp.maximum(jnp.abs(den), jnp.exp(-(b + big_m)))
        hh = num * jnp.concatenate([inv, inv], axis=1)
        sl = slice(h * DH_V, (h + 1) * DH_V)
        gate = _sigmoid(og_ref[bi, :, sl].astype(F32))
        h_ref[bi * ln:(bi + 1) * ln, sl] = (_rms(hh) * ng_ref[:, sl] * gate).astype(BF16)

    deferred_stores = []

    def projection():
        mod = mod_ref[...]
        u = jnp.concatenate(
            [_rms(xn_ref[bi]) * g1_ref[...] * (1.0 + mod[bi][1:2]) + mod[bi][0:1] for bi in range(BATCH)],
            axis=0).astype(BF16)
        gt = _dot(u, wi_ref[:, 2 * QK + 2 * D_MODEL:]) + bg_ref[...]
        cols, imb = _gate_columns(gt)
        imb_rows = imb.T[0:SUBLANES, :]
        for bi in range(BATCH):
            gcol_ref[bi] = cols[bi * ln:(bi + 1) * ln, 0:GATE_WIDTH]
            grow_ref[bi] = imb_rows[:, bi * ln:(bi + 1) * ln]
        yield
        lo = 0
        for ref, width, scale in ((q_ref, QK, DH_QK ** -0.5), (k_ref, QK, None), (v_ref, D_MODEL, None),
                                  (og_ref, D_MODEL, None)):
            for off in range(0, width, PROJ_COLS):
                part = _dot(u, wi_ref[:, lo + off:lo + off + PROJ_COLS])
                if scale is not None:
                    part = part * scale
                part = part.astype(BF16)

                def store(ref=ref, off=off, part=part):
                    for bi in range(BATCH):
                        ref[bi, :, off:off + PROJ_COLS] = part[bi * ln:(bi + 1) * ln]

                if ref is og_ref:
                    deferred_stores.append(store)
                else:
                    store()
                yield
            lo += width

    def previous_chunk():
        half = D_MODEL // 2
        mix0 = _dot(h_prev, wo_ref[:, 0:half])
        yield
        mix1 = _dot(h_prev, wo_ref[:, half:])
        yield
        mod = mod_ref[...]
        u2_rows = []
        for bi in range(BATCH):
            mb = mod[bi]
            rows = slice(bi * ln, (bi + 1) * ln)
            x1 = x_ref[bi] + mb[2:3] * jnp.concatenate([mix0[rows], mix1[rows]], axis=1)
            x1_ref[bi] = x1
            u2_b = _moe_input(x1, mb, g2_ref)
            _store_token_rows(u2_ref.at[bi], u2_b, ln)
            u2_rows.append(u2_b)
        valid = jnp.where(step > 1, 1.0, 0.0)
        yield from _route_stages(jnp.concatenate(u2_rows, axis=0), valid, wr_ref, br_ref, tri_ref,
                                 cls_ref, rank_ref, cnt_ref, run_ref)

    _round_robin([previous_chunk()] + [stream(st) for st in range(n_streams)],
                 background=projection(), every=PROJ_EVERY, start_after=n_streams + 1)
    for store in deferred_stores:
        store()
    for st in range(n_streams):
        m_ref[st] = m_out[st]
        n_ref[st] = n_out[st]


def _gate_columns(gt):
    lane = lax.broadcasted_iota(jnp.int32, gt.shape, 1)
    pos = lax.broadcasted_iota(jnp.int32, gt.shape, 0) % CHUNK
    b = jnp.minimum(gt, 0.0) - jnp.log(1.0 + jnp.exp(-jnp.abs(gt)))
    shift = 1
    while shift < CHUNK:
        b = b + jnp.where(pos >= shift, pltpu.roll(b, shift, axis=0), 0.0)
        shift *= 2
    imb = gt - pltpu.roll(b, LANES - N_HEADS, axis=1)
    cm = imb
    shift = 1
    while shift < CHUNK:
        cm = jnp.maximum(cm, jnp.where(pos >= shift, pltpu.roll(cm, shift, axis=0), -jnp.inf))
        shift *= 2
    cols = jnp.where(lane < N_HEADS, cm,
                     jnp.where(lane < 2 * N_HEADS, b,
                               jnp.where(lane < 3 * N_HEADS, pltpu.roll(imb, 2 * N_HEADS, axis=1), 0.0)))
    return cols, imb


def _mlstm_layer(x, mod, g1, w_in, bg, norm_g, w_out, g2, wr, br):
    nc = SEQ // CHUNK
    n_streams = BATCH * N_HEADS
    rows = BATCH * CHUNK
    newest =lambda c: jnp.minimum(c, nc - 1)
    oldest = lambda c: jnp.maximum(c - 2, 0)
    old_rows = lambda height, width: pl.BlockSpec((BATCH, height, width), lambda c: (0, oldest(c), 0))
    x3d = x.reshape(BATCH, SEQ, D_MODEL)
    x1, u2, cls, rank, cnt = pl.pallas_call(
        _mlstm_layer_kernel,
        out_shape=(
            jax.ShapeDtypeStruct((BATCH, SEQ, D_MODEL), F32),
            jax.ShapeDtypeStruct((BATCH, SEQ * ROW_TILE, LANES), F32),
            jax.ShapeDtypeStruct((nc, 1, rows), jnp.int32),
            jax.ShapeDtypeStruct((nc, 1, rows), jnp.int32),
            jax.ShapeDtypeStruct((CLASS_ROWS, LANES), jnp.int32),
        ),
        grid=(nc + 2,),
        in_specs=[
            pl.BlockSpec((BATCH, CHUNK, D_MODEL), lambda c: (0, newest(c), 0)),
            old_rows(CHUNK, D_MODEL),
            _const_spec((BATCH, 6, D_MODEL)),
            _const_spec((1, D_MODEL)),
            _const_spec(w_in.shape),
            _const_spec((1, GATE_COLS)),
            _const_spec((1, D_MODEL)),
            _const_spec((D_MODEL, D_MODEL)),
            _const_spec((1, D_MODEL)),
            _const_spec((D_MODEL, ROUTER_COLS)),
            _const_spec((N_EXPERTS, 1)),
            _const_spec((rows, rows)),
        ],
        out_specs=(
            old_rows(CHUNK, D_MODEL),
            old_rows(CHUNK * ROW_TILE, LANES),
            pl.BlockSpec((1, 1, rows), lambda c: (oldest(c), 0, 0)),
            pl.BlockSpec((1, 1, rows), lambda c: (oldest(c), 0, 0)),
            _const_spec((CLASS_ROWS, LANES)),
        ),
        scratch_shapes=[pltpu.VMEM((BATCH, CHUNK, QK), BF16),
                        pltpu.VMEM((BATCH, CHUNK, QK), BF16),
                        pltpu.VMEM((BATCH, CHUNK, D_MODEL), BF16),
                        pltpu.VMEM((BATCH, CHUNK, D_MODEL), BF16),
                        pltpu.VMEM((BATCH, CHUNK, GATE_WIDTH), F32),
                        pltpu.VMEM((BATCH, SUBLANES, CHUNK), F32),
                        pltpu.VMEM((n_streams, 1, LANES), F32),
                        pltpu.VMEM((n_streams, 1, DH_QK), F32),
                        pltpu.VMEM((rows, D_MODEL), BF16),
                        pltpu.VMEM((CLASS_ROWS, LANES), F32)]
        + [pltpu.VMEM((DH_QK, DH_V), F32) for _ in range(n_streams)],
        compiler_params=_params("arbitrary"),
        name="mlstm_layer",
    )(x3d, x3d, mod, g1, w_in, bg, norm_g, w_out, g2, wr, br, _strict_upper(rows))
    token_order = lambda a: a.reshape(nc, BATCH, CHUNK).transpose(1, 0, 2).reshape(TOKENS)
    return (x1.reshape(TOKENS, D_MODEL), u2.reshape(TOKENS * ROW_TILE, LANES), token_order(cls),
            token_order(rank), cnt)


def _top2_sum(v0, v1, v2, v3):
    hi1, lo1 = jnp.maximum(v0, v1), jnp.minimum(v0, v1)
    hi2, lo2 = jnp.maximum(v2, v3), jnp.minimum(v2, v3)
    return jnp.maximum(hi1, hi2) + jnp.maximum(jnp.minimum(hi1, hi2), jnp.maximum(lo1, lo2))


def _route_tail(x_new, m, valid, g2_ref, wr_ref, br_ref, tri_ref, u2_ref, cls_ref, rank_ref, cnt_ref,
                run_ref, after_router=None):
    u2 = _moe_input(x_new, m, g2_ref)
    _store_token_rows(u2_ref, u2, u2.shape[0])
    _route(u2, valid, wr_ref, br_ref, tri_ref, cls_ref, rank_ref, cnt_ref, run_ref, after_router)


def _moe_input(x_new, m, g2_ref):
    return _rms(x_new) * g2_ref[...] * (1.0 + m[4:5]) + m[3:4]


def _round_robin(chains, background=None, every=1, start_after=0):
    chains = list(chains)
    advanced = 0
    while chains:
        alive = []
        for chain in chains:
            try:
                next(chain)
            except StopIteration:
                continue
            alive.append(chain)
            advanced += 1
            if background is not None and advanced >= start_after and advanced % every == 0:
                next(background, None)
        chains = alive
    if background is not None:
        for _ in background:
            pass


def _route(u2, valid, wr_ref, br_ref, tri_ref, cls_ref, rank_ref, cnt_ref, run_ref, after_router=None):
    stages = _route_stages(u2, valid, wr_ref, br_ref, tri_ref, cls_ref, rank_ref, cnt_ref, run_ref)
    next(stages)
    if after_router is not None:
        after_router()
    for _ in stages:
        pass


def _route_stages(u2, valid, wr_ref, br_ref, tri_ref, cls_ref, rank_ref, cnt_ref, run_ref):
    logits = _dot(u2.astype(BF16), wr_ref[...])
    yield
    lt = logits.T[0:N_EXPERTS, :]
    e = jnp.exp(lt - jnp.max(lt, axis=0, keepdims=True))
    probs = e / jnp.sum(e, axis=0, keepdims=True)
    sel = probs + br_ref[...]
    sel_rows = [sel[j:j + 1, :] for j in range(N_EXPERTS)]
    best = jnp.zeros_like(sel_rows[0], dtype=jnp.int32)
    best_score = _top2_sum(*sel_rows[0:EXPERTS_PER_GROUP])
    for g in range(1, N_GROUPS):
        score = _top2_sum(*sel_rows[g * EXPERTS_PER_GROUP:(g + 1) * EXPERTS_PER_GROUP])
        better = score > best_score
        best = jnp.where(better, g, best)
        best_score = jnp.where(better, score, best_score)
    s = []
    for j in range(EXPERTS_PER_GROUP):
        sj = sel_rows[j]
        for g in range(1, N_GROUPS):
            sj = jnp.where(best == g, sel_rows[g * EXPERTS_PER_GROUP + j], sj)
        s.append(sj)
    chosen = []
    for j in range(EXPERTS_PER_GROUP):
        beaten = jnp.zeros_like(best)
        for i in range(EXPERTS_PER_GROUP):
            if i == j:
                continue
            wins = (s[i] >= s[j]) if i < j else (s[i] > s[j])
            beaten = beaten + jnp.where(wins, 1, 0)
        chosen.append(beaten < 2)
    pair = jnp.full_like(best, len(PAIRS) - 1)
    for p in range(len(PAIRS) - 2, -1, -1):
        a, b = PAIRS[p]
        pair = jnp.where(jnp.logical_and(chosen[a], chosen[b]), p, pair)
    cls = best * len(PAIRS) + pair
    cls_ref[0] = cls

    class_id = lax.broadcasted_iota(jnp.int32, (CLASS_ROWS, cls.shape[1]), 0)
    onehot = class_id == cls
    before = _dot(jnp.where(onehot, 1.0, 0.0).astype(BF16), tri_ref[...])
    run = run_ref[...]
    rank = jnp.sum(jnp.where(onehot, before + run[:, 0:1], 0.0), axis=0, keepdims=True)
    rank_ref[0] = rank.astype(jnp.int32)
    run = run + valid * jnp.sum(jnp.where(onehot, 1.0, 0.0), axis=1, keepdims=True)
    run_ref[...] = run
    cnt_ref[...] = run.astype(jnp.int32)


def _strict_upper(n):
    return (jnp.arange(n)[:, None] < jnp.arange(n)[None, :]).astype(BF16)


def _route_out_shapes(t, tm):
    return (
        jax.ShapeDtypeStruct((t, D_MODEL), F32),
        jax.ShapeDtypeStruct((t * ROW_TILE, LANES), F32),
        jax.ShapeDtypeStruct((t // tm, 1, tm), jnp.int32),
        jax.ShapeDtypeStruct((t // tm, 1, tm), jnp.int32),
        jax.ShapeDtypeStruct((CLASS_ROWS, LANES), jnp.int32),
    )


def _cur_tile(n_tiles):
    return lambda i: jnp.minimum(i, n_tiles - 1)


def _prev_tile(i):
    return jnp.maximum(i - 1, 0)


def _route_out_specs(tm, n_tiles):
    cur = _cur_tile(n_tiles)
    return (
        pl.BlockSpec((tm, D_MODEL), lambda i, *_: (cur(i), 0)),
        pl.BlockSpec((tm * ROW_TILE, LANES), lambda i, *_: (_prev_tile(i), 0)),
        pl.BlockSpec((1, 1, tm), lambda i, *_: (_prev_tile(i), 0, 0)),
        pl.BlockSpec((1, 1, tm), lambda i, *_: (_prev_tile(i), 0, 0)),
        pl.BlockSpec((CLASS_ROWS, LANES), lambda i, *_: (0, 0)),
    )


def _init_deferred_tail(xprev_ref, run_ref):
    @pl.when(pl.program_id(0) == 0)
    def _():
        xprev_ref[...] = jnp.zeros_like(xprev_ref)
        run_ref[...] = jnp.zeros_like(run_ref)


def _deferred_tail(xprev_ref, modp_ref, *tail_refs, after_router=None):
    valid = jnp.where(pl.program_id(0) > 0, 1.0, 0.0)
    _route_tail(xprev_ref[...], modp_ref[0], valid, *tail_refs, after_router=after_router)


def _const_spec(shape):
    return pl.BlockSpec(shape, lambda i, *_: (0,) * len(shape))


def _conv_layer_kernel(dest_ref, x_ref, ys_ref, mod0_ref, mod_ref, modp_ref, g1_ref, wi_ref, cw_ref,
                       cb_ref, wo_ref, g2_ref, wr_ref, br_ref, tri_ref,
                       x3_ref, u2_ref, cls_ref, rank_ref, cnt_ref,
                       run_ref, carry_ref, ybuf_ref, ysem_ref, xprev_ref):
    tm = x_ref.shape[0]
    tiles_per_batch = SEQ // tm
    n_tiles = pl.num_programs(0) - 1
    i = pl.program_id(0)

    @pl.when(i % tiles_per_batch == 0)
    def _():
        carry_ref[...] = jnp.zeros_like(carry_ref)

    _init_deferred_tail(xprev_ref, run_ref)
    m = mod_ref[0]
    y = _gathered_tile(dest_ref, ys_ref, ybuf_ref, ysem_ref, tm, issue_first=False, n_tiles=n_tiles)
    x2 = x_ref[...] + mod0_ref[0][5:6] * y
    u = (_rms(x2) * g1_ref[...] * (1.0 + m[1:2]) + m[0:1]).astype(BF16)
    bgate = _dot(u, wi_ref[:, 0:D_MODEL])
    gated = []

    def project_gated():
        gated.append(_dot(u, wi_ref[:, D_MODEL:2 * D_MODEL]) * _dot(u, wi_ref[:, 2 * D_MODEL:]))

    _deferred_tail(xprev_ref, modp_ref, g2_ref, wr_ref, br_ref, tri_ref, u2_ref, cls_ref, rank_ref,
                   cnt_ref, run_ref, after_router=project_gated)
    z = gated[0]
    prev = carry_ref[...]
    row = lax.broadcasted_iota(jnp.int32, z.shape, 0)
    z1 = jnp.where(row == 0, prev[7:8], pltpu.roll(z, 1, axis=0))
    z2 = jnp.where(row == 0, prev[6:7], jnp.where(row == 1, prev[7:8], pltpu.roll(z, 2, axis=0)))
    carry_ref[...] = z[tm - SUBLANES:, :]
    cw = cw_ref[...]
    zc = cw[0:1] * z2 + cw[1:2] * z1 + cw[2:3] * z + cb_ref[...]
    x3 = x2 + m[2:3] * _dot((bgate * zc).astype(BF16), wo_ref[...])
    xprev_ref[...] = x3

    @pl.when(i < n_tiles)
    def _():
        x3_ref[...] = x3

    _drain_row_gather(ys_ref, ybuf_ref, ysem_ref)


def _conv_layer(dest, x, ys, mod0, mod, g1, w_in, conv_w, conv_b, w_out, g2, wr, br):
    t = x.shape[0]
    tm = TM_CONV
    tri = _strict_upper(tm)
    tiles_per_batch = SEQ // tm
    n_tiles = t // tm
    cur = _cur_tile(n_tiles)
    mod_spec = pl.BlockSpec((1, 6, D_MODEL), lambda i, *_: (cur(i) // tiles_per_batch, 0, 0))
    modp_spec = pl.BlockSpec((1, 6, D_MODEL), lambda i, *_: (_prev_tile(i) // tiles_per_batch, 0, 0))
    return pl.pallas_call(
        _conv_layer_kernel,
        out_shape=_route_out_shapes(t, tm),
        grid_spec=pltpu.PrefetchScalarGridSpec(
            num_scalar_prefetch=1,
            grid=(n_tiles + 1,),
            in_specs=[
                pl.BlockSpec((tm, D_MODEL), lambda i, *_: (cur(i), 0)),
                pl.BlockSpec(memory_space=pl.ANY),
                mod_spec,
                mod_spec,
                modp_spec,
                _const_spec((1, D_MODEL)),
                _const_spec((D_MODEL, 3 * D_MODEL)),
                _const_spec((3, D_MODEL)),
                _const_spec((1, D_MODEL)),
                _const_spec((D_MODEL, D_MODEL)),
                _const_spec((1, D_MODEL)),
                _const_spec((D_MODEL, ROUTER_COLS)),
                _const_spec((N_EXPERTS, 1)),
                _const_spec((tm, tm)),
            ],
            out_specs=_route_out_specs(tm, n_tiles),
            scratch_shapes=[
                pltpu.VMEM((CLASS_ROWS, LANES), F32),
                pltpu.VMEM((SUBLANES, D_MODEL), F32),
                pltpu.VMEM((2, tm * ROW_TILE, LANES), F32),
                pltpu.SemaphoreType.DMA((2,)),
                pltpu.VMEM((tm, D_MODEL), F32),
            ],
        ),
        compiler_params=_params("arbitrary"),
        name="conv_layer_route",
    )(dest, x, ys, mod0, mod, mod, g1, w_in, conv_w, conv_b, w_out, g2, wr, br, tri)


def _invert_kernel(dest_ref, fill_ref, src_ref, sem):
    fill = pltpu.make_async_copy(fill_ref, src_ref, sem)
    fill.start()
    fill.wait()
    unroll = 8

    def body(t8, carry):
        for k in range(unroll):
            t = t8 * unroll + k
            src_ref[dest_ref[t]] = t
        return carry

    lax.fori_loop(0, dest_ref.shape[0] // unroll, body, 0)


def _invert_permutation(dest, n_out):
    smem = pl.BlockSpec(memory_space=pltpu.SMEM)
    return pl.pallas_call(
        _invert_kernel,
        out_shape=jax.ShapeDtypeStruct((n_out,), jnp.int32),
        in_specs=[smem, pl.BlockSpec(memory_space=pl.ANY)],
        out_specs=smem,
        scratch_shapes=[pltpu.SemaphoreType.DMA(())],
        name="invert_permutation",
    )(dest, jnp.arange(n_out, dtype=jnp.int32) % dest.shape[0])


def _expert_kernel(ea_ref, eb_ref, chg_ref, nused_ref, src_ref, u2_ref, wga_ref, wua_ref, wda_ref,
                   wgb_ref, wub_ref, wdb_ref, wra_ref, wrb_ref, y_ref, xbuf_ref, xsem_ref, *wbf_refs):
    del ea_ref, eb_ref
    j = pl.program_id(0)
    tm = TM_EXPERT

    @pl.when(chg_ref[j] == 1)
    def _():
        for src, dst in zip((wga_ref, wua_ref, wda_ref, wgb_ref, wub_ref, wdb_ref), wbf_refs):
            dst[...] = src[0, 0].astype(BF16)

    load_rows, start_next = _gathered_tile(src_ref, u2_ref, xbuf_ref, xsem_ref, tm, issue_first=None,
                                           n_active=nused_ref[0])

    @pl.when(j < nused_ref[0])
    def _():
        xb = load_rows().astype(BF16)
        start_next()
        dl = jnp.sum(xb.astype(F32) * (wra_ref[0] - wrb_ref[0]), axis=-1, keepdims=True)
        w_a = _sigmoid(dl)
        w_b = _sigmoid(-dl)

        def ffn(wg_ref, wu_ref, wd_ref):
            gate = _dot(xb, wg_ref[...])
            hidden = gate * _sigmoid(gate) * _dot(xb, wu_ref[...])
            return _dot(hidden.astype(BF16), wd_ref[...])

        y = w_a * ffn(*wbf_refs[0:3]) + w_b * ffn(*wbf_refs[3:6])
        _store_token_rows(y_ref, y, tm)

    @pl.when(j >= nused_ref[0])
    def _():
        y_ref[...] = jnp.zeros_like(y_ref)

    _drain_row_gather(u2_ref, xbuf_ref, xsem_ref, n_active=nused_ref[0])


def _experts(layer, ea, eb, chg, n_used, src, u2, w_gate, w_up, w_down, wr_rows):
    tm = TM_EXPERT
    sel_a = lambda j, ea, eb, *_: (layer, ea[j], 0, 0)
    sel_b = lambda j, ea, eb, *_: (layer, eb[j], 0, 0)
    up_spec = lambda sel: pl.BlockSpec((1, 1, D_MODEL, D_EXPERT), sel)
    down_spec = lambda sel: pl.BlockSpec((1, 1, D_EXPERT, D_MODEL), sel)
    wr_spec = lambda sel: pl.BlockSpec((1, 1, D_MODEL), lambda *a: sel(*a)[1:])
    up_scratch = pltpu.VMEM((D_MODEL, D_EXPERT), BF16)
    down_scratch = pltpu.VMEM((D_EXPERT, D_MODEL), BF16)
    return pl.pallas_call(
        _expert_kernel,
        out_shape=jax.ShapeDtypeStruct((PADDED_ROWS * ROW_TILE, LANES), F32),
        grid_spec=pltpu.PrefetchScalarGridSpec(
            num_scalar_prefetch=4,
            grid=(N_EXPERT_TILES,),
            in_specs=[
                pl.BlockSpec(memory_space=pltpu.SMEM),
                pl.BlockSpec(memory_space=pl.ANY),
                up_spec(sel_a), up_spec(sel_a), down_spec(sel_a),
                up_spec(sel_b), up_spec(sel_b), down_spec(sel_b),
                wr_spec(sel_a), wr_spec(sel_b),
            ],
            out_specs=pl.BlockSpec((tm * ROW_TILE, LANES), lambda j, *_: (j, 0)),
            scratch_shapes=[pltpu.VMEM((EXPERT_GATHER_BUFFERS, tm * ROW_TILE, LANES), F32),
                            pltpu.SemaphoreType.DMA((EXPERT_GATHER_BUFFERS,)),
                            up_scratch, up_scratch, down_scratch, up_scratch, up_scratch, down_scratch],
        ),
        compiler_params=_params("arbitrary"),
        name="grouped_experts",
    )(ea, eb, chg, n_used, src, u2, w_gate, w_up, w_down, w_gate, w_up, w_down, wr_rows, wr_rows)


def _final_kernel(dest_ref, x_ref, ys_ref, mod_ref, g_ref, o_ref, ybuf_ref, ysem_ref):
    y = _gathered_tile(dest_ref, ys_ref, ybuf_ref, ysem_ref, x_ref.shape[0], issue_first=True)
    x = x_ref[...] + mod_ref[0][5:6] * y
    o_ref[...] = _rms(x) * g_ref[...]
    _drain_row_gather(ys_ref, ybuf_ref, ysem_ref)


def _final(dest, x, ys, mod, g):
    t = x.shape[0]
    tm = TM_FINAL
    tiles_per_batch = SEQ // tm
    return pl.pallas_call(
        _final_kernel,
        out_shape=jax.ShapeDtypeStruct((t, D_MODEL), F32),
        grid_spec=pltpu.PrefetchScalarGridSpec(
            num_scalar_prefetch=1,
            grid=(t // tm,),
            in_specs=[
                pl.BlockSpec((tm, D_MODEL), lambda i, *_: (i, 0)),
                pl.BlockSpec(memory_space=pl.ANY),
                pl.BlockSpec((1, 6, D_MODEL), lambda i, *_: (i // tiles_per_batch, 0, 0)),
                _const_spec((1, D_MODEL)),
            ],
            out_specs=pl.BlockSpec((tm, D_MODEL), lambda i, *_: (i, 0)),
            scratch_shapes=[pltpu.VMEM((2, tm * ROW_TILE, LANES), F32), pltpu.SemaphoreType.DMA((2,))],
        ),
        compiler_params=_params("arbitrary"),
        name="final_norm",
    )(dest, x, ys, mod, g)


_PAIR_A = np.array([EXPERTS_PER_GROUP * (c // len(PAIRS)) + PAIRS[c % len(PAIRS)][0]
                    for c in range(N_CLASSES)], np.int32)
_PAIR_B = np.array([EXPERTS_PER_GROUP * (c // len(PAIRS)) + PAIRS[c % len(PAIRS)][1]
                    for c in range(N_CLASSES)], np.int32)


def _dest_kernel(start_ref, cls_ref, rank_ref, dest_ref):
    cls = cls_ref[...]
    dest = rank_ref[...]
    for c in range(N_CLASSES):
        dest = dest + jnp.where(cls == c, start_ref[c], 0)
    dest_ref[...] = dest


def _sorted_rows(row_start, cls, rank):
    shape = (TOKENS // LANES, LANES)
    dest = pl.pallas_call(
        _dest_kernel,
        out_shape=jax.ShapeDtypeStruct(shape, jnp.int32),
        in_specs=[pl.BlockSpec(memory_space=pltpu.SMEM), pl.BlockSpec(memory_space=pltpu.VMEM),
                  pl.BlockSpec(memory_space=pltpu.VMEM)],
        out_specs=pl.BlockSpec(memory_space=pltpu.VMEM),
        name="sorted_rows",
    )(row_start, cls.reshape(shape), rank.reshape(shape))
    return dest.reshape(-1)


def _moe(layer, u2, cls, rank, cnt, w_gate, w_up, w_down, wr_rows):
    counts = cnt[:N_CLASSES, 0]
    tiles = (counts + TM_EXPERT - 1) // TM_EXPERT
    tile_end = jnp.cumsum(tiles)
    row_start = (tile_end - tiles) * TM_EXPERT
    n_used = tile_end[-1:]
    dest = _sorted_rows(row_start, cls, rank)
    tile_cls = jnp.sum(jnp.arange(N_EXPERT_TILES)[:, None] >= tile_end[None, :], axis=1)
    tile_cls = jnp.minimum(tile_cls, tile_cls[jnp.maximum(n_used[0] - 1, 0)])
    ea = jnp.asarray(_PAIR_A)[tile_cls]
    eb = jnp.asarray(_PAIR_B)[tile_cls]
    chg = jnp.concatenate([jnp.ones((1,), jnp.int32),
                           (tile_cls[1:] != tile_cls[:-1]).astype(jnp.int32)])
    src = _invert_permutation(dest, PADDED_ROWS)
    ys = _experts(layer, ea, eb, chg, n_used.astype(jnp.int32), src, u2, w_gate, w_up, w_down, wr_rows)
    return dest, ys


def kernel(x, c, norm1_g, norm2_g, w_ada, b_ada, m_w_in, m_b_gates, m_norm_g, m_w_out,
           c_w_in, c_conv_w, c_conv_b, c_w_out, w_router, b_router,
           e_w_gate, e_w_up, e_w_down, final_g):
    xf = x.reshape(TOKENS, D_MODEL)
    mod = _ada(c, w_ada, b_ada)

    w_in = jnp.concatenate(
        [m_w_in[0].astype(BF16),
         jnp.zeros((D_MODEL, GATE_COLS - 2 * N_HEADS), BF16)], axis=1)
    bg = jnp.zeros((1, GATE_COLS), F32).at[0, :2 * N_HEADS].set(m_b_gates[0])
    wr = jnp.zeros((D_MODEL, ROUTER_COLS), BF16).at[:, :N_EXPERTS].set(w_router.astype(BF16))
    wr_rows = w_router.astype(BF16).astype(F32).T.reshape(N_EXPERTS, 1, D_MODEL)
    br = b_router.reshape(N_EXPERTS, 1)
    row = lambda v: v.reshape(1, -1)

    x1, u2, cls, rank, cnt = _mlstm_layer(xf, mod[0], row(norm1_g[0]), w_in, bg, row(m_norm_g[0]),
                                          m_w_out[0].astype(BF16), row(norm2_g[0]), wr, br)
    dest, ys = _moe(0, u2, cls, rank, cnt, e_w_gate, e_w_up, e_w_down, wr_rows)

    x3, u2, cls, rank, cnt = _conv_layer(
        dest, x1, ys, mod[0], mod[1], row(norm1_g[1]), c_w_in[0].astype(BF16), c_conv_w[0],
        row(c_conv_b[0]), c_w_out[0].astype(BF16), row(norm2_g[1]), wr, br)
    dest, ys = _moe(1, u2, cls, rank, cnt, e_w_gate, e_w_up, e_w_down, wr_rows)

    out = _final(dest, x3, ys, mod[1], row(final_g))
    return out.reshape(BATCH, SEQ, D_MODEL)
```

```python
import functools

import jax
import jax.numpy as jnp
import numpy as np
from jax import lax
from jax.experimental import pallas as pl
from jax.experimental.pallas import tpu as pltpu

F32 = jnp.float32
BF16 = jnp.bfloat16

D_MODEL = 1024
BATCH = 4
SEQ = 8192
TOKENS = BATCH * SEQ
N_HEADS = 4
DH_V = 256
DH_QK = 128
QK = N_HEADS * DH_QK
N_EXPERTS = 16
N_GROUPS = 4
EXPERTS_PER_GROUP = 4
D_EXPERT = 512
EPS = 1e-6

LANES = 128
SUBLANES = 8
VMEM_LIMIT_BYTES = 56 * 1024 * 1024

CHUNK = 128
TM_CONV = 512
TM_EXPERT = 512
GATE_COLS = LANES
GATE_WIDTH = 16
ROUTER_COLS = LANES
PAIRS = ((0, 1), (0, 2), (0, 3), (1, 2), (1, 3), (2, 3))
N_CLASSES = N_GROUPS * len(PAIRS)
CLASS_ROWS = 32
N_EXPERT_TILES = TOKENS // TM_EXPERT + N_CLASSES
PADDED_ROWS = N_EXPERT_TILES * TM_EXPERT
TM_FINAL = 1024
EXPERT_GATHER_BUFFERS = 4
PROJ_COLS = 256
PROJ_EVERY = 1


def _params(*semantics):
    return pltpu.CompilerParams(dimension_semantics=semantics, vmem_limit_bytes=VMEM_LIMIT_BYTES)


def _dot(a, b):
    return jnp.dot(a, b, preferred_element_type=F32)


def _rms(x):
    return x * lax.rsqrt(jnp.mean(x * x, axis=-1, keepdims=True) + EPS)


def _sigmoid(x):
    return 1.0 / (1.0 + jnp.exp(-x))


ROW_TILE = D_MODEL // LANES


def _load_token_rows(ref, n):
    return jnp.concatenate([ref[pl.ds(c, n, stride=ROW_TILE), :] for c in range(ROW_TILE)], axis=1)


def _store_token_rows(ref, val, n):
    for c in range(ROW_TILE):
        ref[pl.ds(c, n, stride=ROW_TILE), :] = val[:, c * LANES:(c + 1) * LANES]


def _token_tile(ref, t):
    return ref.at[pl.ds(pl.multiple_of(t * ROW_TILE, ROW_TILE), ROW_TILE)]


def _start_row_gather(idx_ref, base, n, src_hbm, buf, sem):
    for r in range(n):
        pltpu.make_async_copy(_token_tile(src_hbm, idx_ref[base + r]), _token_tile(buf, r),
                              sem).start(priority=r % 2)


def _wait_row_gather(src_hbm, buf, sem):
    pltpu.make_async_copy(src_hbm.at[pl.ds(0, buf.shape[0])], buf, sem).wait()


def _gathered_tile(idx_ref, src_hbm, buf_ref, sem_ref, n, issue_first, n_tiles=None, n_active=None):
    i = pl.program_id(0)
    last = (pl.num_programs(0) if n_tiles is None else n_tiles) - 1
    nb = buf_ref.shape[0]
    ahead = nb - 1

    @pl.when(i == 0)
    def _():
        for k in range(ahead):
            _start_row_gather(idx_ref, jnp.minimum(k, last) * n, n, src_hbm, buf_ref.at[k], sem_ref.at[k])

    nxt = (i + ahead) % nb
    start_next = functools.partial(
        _start_row_gather, idx_ref, jnp.minimum(i + ahead, last) * n, n, src_hbm, buf_ref.at[nxt],
        sem_ref.at[nxt])
    if issue_first:
        start_next()
    slot = i % nb
    if n_active is None:
        _wait_row_gather(src_hbm, buf_ref.at[slot], sem_ref.at[slot])
    else:
        @pl.when(i < n_active + ahead)
        def _():
            _wait_row_gather(src_hbm, buf_ref.at[slot], sem_ref.at[slot])

    load_rows = functools.partial(_load_token_rows, buf_ref.at[slot], n)
    if issue_first is None:
        return load_rows, start_next
    rows = load_rows()
    if not issue_first:
        start_next()
    return rows


def _drain_row_gather(src_hbm, buf_ref, sem_ref, n_active=None):
    i = pl.program_id(0)
    nb = buf_ref.shape[0]

    @pl.when(i == pl.num_programs(0) - 1)
    def _():
        for k in range(1, nb):
            wait = functools.partial(_wait_row_gather, src_hbm, buf_ref.at[(i + k) % nb],
                                     sem_ref.at[(i + k) % nb])
            if n_active is None:
                wait()
            else:
                pl.when(i + k - (nb - 1) < n_active)(wait)


def _ada_kernel(c_ref, w_ref, b_ref, o_ref):
    c = c_ref[...]
    cond = c * _sigmoid(c)
    o_ref[0] = jnp.dot(cond, w_ref[0], preferred_element_type=F32,
                       precision=lax.Precision.HIGHEST) + b_ref[0]


def _ada(c, w_ada, b_ada):
    depth, d, n = w_ada.shape
    tn = 3072
    c8 = jnp.zeros((SUBLANES, d), F32).at[:BATCH].set(c)
    out = pl.pallas_call(
        _ada_kernel,
        out_shape=jax.ShapeDtypeStruct((depth, SUBLANES, n), F32),
        grid=(depth, n // tn),
        in_specs=[
            pl.BlockSpec((SUBLANES, d), lambda l, j: (0, 0)),
            pl.BlockSpec((1, d, tn), lambda l, j: (l, 0, j)),
            pl.BlockSpec((1, 1, tn), lambda l, j: (l, 0, j)),
        ],
        out_specs=pl.BlockSpec((1, SUBLANES, tn), lambda l, j: (l, 0, j)),
        compiler_params=_params("arbitrary", "arbitrary"),
        name="ada_mod",
    )(c8, w_ada, b_ada.reshape(depth, 1, n))
    return out[:, :BATCH].reshape(depth, BATCH, 6, d)


def _mlstm_layer_kernel(xn_ref, x_ref, mod_ref, g1_ref, wi_ref, bg_ref, ng_ref,
                        wo_ref, g2_ref, wr_ref, br_ref, tri_ref,
                        x1_ref, u2_ref, cls_ref, rank_ref, cnt_ref,
                        q_ref, k_ref, v_ref, og_ref, gcol_ref, grow_ref, m_ref, n_ref, h_ref, run_ref,
                        *c_refs):
    step = pl.program_id(0)

    @pl.when(step == 0)
    def _():
        for ref in (q_ref, k_ref, v_ref, og_ref, gcol_ref, grow_ref, m_ref, n_ref, h_ref, run_ref) + c_refs:
            ref[...] = jnp.zeros_like(ref)

    h_prev = h_ref[...]
    ln = CHUNK
    row = lax.broadcasted_iota(jnp.int32, (ln, ln), 0)
    col = lax.broadcasted_iota(jnp.int32, (ln, ln), 1)
    causal = col <= row
    n_streams = BATCH * N_HEADS
    m_all = [m_ref[st] for st in range(n_streams)]
    n_all = [n_ref[st] for st in range(n_streams)]
    m_out, n_out = [None] * n_streams, [None] * n_streams

    def stream(st):
        bi, h = divmod(st, N_HEADS)
        gates = gcol_ref[bi]
        column = lambda kind: jnp.broadcast_to(
            gates[:, kind * N_HEADS + h:kind * N_HEADS + h + 1], (ln, LANES))
        cm, b, imb_col = column(0), column(1), column(2)
        imb_row = grow_ref[bi, h:h + 1, :]
        qh = q_ref[bi, :, h * DH_QK:(h + 1) * DH_QK]
        kh = k_ref[bi, :, h * DH_QK:(h + 1) * DH_QK]
        vh = v_ref[bi, :, h * DH_V:(h + 1) * DH_V]
        c_ref = c_refs[st]
        c_prev = c_ref[...]
        m_prev = m_all[st]
        big_m = jnp.maximum(m_prev, cm)
        d_mat = jnp.exp(jnp.where(causal, imb_row - big_m, -jnp.inf))
        s_raw = lax.dot_general(qh, kh, (((1,), (1,)), ((), ())), preferred_element_type=F32)
        yield
        q_inter = qh.astype(F32) * jnp.exp(m_prev - big_m)
        s = s_raw * d_mat
        lhs = jnp.concatenate([q_inter.astype(BF16), s.astype(BF16)], axis=1)
        rhs = jnp.concatenate([c_prev.astype(BF16), vh], axis=0)
        num = _dot(lhs, rhs)
        den = jnp.sum(s + q_inter * n_all[st], axis=-1, keepdims=True)
        m_last = big_m[ln - 1:ln, :]
        kw = (kh.astype(F32) * jnp.exp(imb_col - m_last)).astype(BF16)
        update = lax.dot_general(kw, vh, (((0,), (0,)), ((), ())), preferred_element_type=F32)
        yield
        decay = jnp.exp(m_prev - m_last)
        c_ref[...] = jnp.concatenate([decay] * 2, axis=1) * c_prev + update
        n_out[st] = decay * n_all[st] + jnp.sum(kw.astype(F32), axis=0, keepdims=True)
        m_out[st] = b[ln - 1:ln, :] + m_last
        inv = 1.0 / jnp.maximum(jnp.abs(den), jnp.exp(-(b + big_m)))
        hh = num * jnp.concatenate([inv, inv], axis=1)
        sl = slice(h * DH_V, (h + 1) * DH_V)
        gate = _sigmoid(og_ref[bi, :, sl].astype(F32))
        h_ref[bi * ln:(bi + 1) * ln, sl] = (_rms(hh) * ng_ref[:, sl] * gate).astype(BF16)

    deferred_stores = []

    def projection():
        mod = mod_ref[...]
        u = jnp.concatenate(
            [_rms(xn_ref[bi]) * g1_ref[...] * (1.0 + mod[bi][1:2]) + mod[bi][0:1] for bi in range(BATCH)],
            axis=0).astype(BF16)
        gt = _dot(u, wi_ref[:, 2 * QK + 2 * D_MODEL:]) + bg_ref[...]
        cols, imb = _gate_columns(gt)
        imb_rows = imb.T[0:SUBLANES, :]
        for bi in range(BATCH):
            gcol_ref[bi] = cols[bi * ln:(bi + 1) * ln, 0:GATE_WIDTH]
            grow_ref[bi] = imb_rows[:, bi * ln:(bi + 1) * ln]
        yield
        lo = 0
        for ref, width, scale in ((q_ref, QK, DH_QK ** -0.5), (k_ref, QK, None), (v_ref, D_MODEL, None),
                                  (og_ref, D_MODEL, None)):
            for off in range(0, width, PROJ_COLS):
                part = _dot(u, wi_ref[:, lo + off:lo + off + PROJ_COLS])
                if scale is not None:
                    part = part * scale
                part = part.astype(BF16)

                def store(ref=ref, off=off, part=part):
                    for bi in range(BATCH):
                        ref[bi, :, off:off + PROJ_COLS] = part[bi * ln:(bi + 1) * ln]

                if ref is og_ref:
                    deferred_stores.append(store)
                else:
                    store()
                yield
            lo += width

    def previous_chunk():
        half = D_MODEL // 2
        mix0 = _dot(h_prev, wo_ref[:, 0:half])
        yield
        mix1 = _dot(h_prev, wo_ref[:, half:])
        yield
        mod = mod_ref[...]
        u2_rows = []
        for bi in range(BATCH):
            mb = mod[bi]
            rows = slice(bi * ln, (bi + 1) * ln)
            x1 = x_ref[bi] + mb[2:3] * jnp.concatenate([mix0[rows], mix1[rows]], axis=1)
            x1_ref[bi] = x1
            u2_b = _moe_input(x1, mb, g2_ref)
            _store_token_rows(u2_ref.at[bi], u2_b, ln)
            u2_rows.append(u2_b)
        valid = jnp.where(step > 1, 1.0, 0.0)
        yield from _route_stages(jnp.concatenate(u2_rows, axis=0), valid, wr_ref, br_ref, tri_ref,
                                 cls_ref, rank_ref, cnt_ref, run_ref)

    _round_robin([previous_chunk()] + [stream(st) for st in range(n_streams)],
                 background=projection(), every=PROJ_EVERY, start_after=n_streams + 1)
    for store in deferred_stores:
        store()
    for st in range(n_streams):
        m_ref[st] = m_out[st]
        n_ref[st] = n_out[st]


def _gate_columns(gt):
    lane = lax.broadcasted_iota(jnp.int32, gt.shape, 1)
    pos = lax.broadcasted_iota(jnp.int32, gt.shape, 0) % CHUNK
    b = jnp.minimum(gt, 0.0) - jnp.log(1.0 + jnp.exp(-jnp.abs(gt)))
    shift = 1
    while shift < CHUNK:
        b = b + jnp.where(pos >= shift, pltpu.roll(b, shift, axis=0), 0.0)
        shift *= 2
    imb = gt - pltpu.roll(b, LANES - N_HEADS, axis=1)
    cm = imb
    shift = 1
    while shift < CHUNK:
        cm = jnp.maximum(cm, jnp.where(pos >= shift, pltpu.roll(cm, shift, axis=0), -jnp.inf))
        shift *= 2
    cols = jnp.where(lane < N_HEADS, cm,
                     jnp.where(lane < 2 * N_HEADS, b,
                               jnp.where(lane < 3 * N_HEADS, pltpu.roll(imb, 2 * N_HEADS, axis=1), 0.0)))
    return cols, imb


def _mlstm_layer(x, mod, g1, w_in, bg, norm_g, w_out, g2, wr, br):
    nc = SEQ // CHUNK
    n_streams = BATCH * N_HEADS
    rows = BATCH * CHUNK
    newest =lambda c: jnp.minimum(c, nc - 1)
    oldest = lambda c: jnp.maximum(c - 2, 0)
    old_rows = lambda height, width: pl.BlockSpec((BATCH, height, width), lambda c: (0, oldest(c), 0))
    x3d = x.reshape(BATCH, SEQ, D_MODEL)
    x1, u2, cls, rank, cnt = pl.pallas_call(
        _mlstm_layer_kernel,
        out_shape=(
            jax.ShapeDtypeStruct((BATCH, SEQ, D_MODEL), F32),
            jax.ShapeDtypeStruct((BATCH, SEQ * ROW_TILE, LANES), F32),
            jax.ShapeDtypeStruct((nc, 1, rows), jnp.int32),
            jax.ShapeDtypeStruct((nc, 1, rows), jnp.int32),
            jax.ShapeDtypeStruct((CLASS_ROWS, LANES), jnp.int32),
        ),
        grid=(nc + 2,),
        in_specs=[
            pl.BlockSpec((BATCH, CHUNK, D_MODEL), lambda c: (0, newest(c), 0)),
            old_rows(CHUNK, D_MODEL),
            _const_spec((BATCH, 6, D_MODEL)),
            _const_spec((1, D_MODEL)),
            _const_spec(w_in.shape),
            _const_spec((1, GATE_COLS)),
            _const_spec((1, D_MODEL)),
            _const_spec((D_MODEL, D_MODEL)),
            _const_spec((1, D_MODEL)),
            _const_spec((D_MODEL, ROUTER_COLS)),
            _const_spec((N_EXPERTS, 1)),
            _const_spec((rows, rows)),
        ],
        out_specs=(
            old_rows(CHUNK, D_MODEL),
            old_rows(CHUNK * ROW_TILE, LANES),
            pl.BlockSpec((1, 1, rows), lambda c: (oldest(c), 0, 0)),
            pl.BlockSpec((1, 1, rows), lambda c: (oldest(c), 0, 0)),
            _const_spec((CLASS_ROWS, LANES)),
        ),
        scratch_shapes=[pltpu.VMEM((BATCH, CHUNK, QK), BF16),
                        pltpu.VMEM((BATCH, CHUNK, QK), BF16),
                        pltpu.VMEM((BATCH, CHUNK, D_MODEL), BF16),
                        pltpu.VMEM((BATCH, CHUNK, D_MODEL), BF16),
                        pltpu.VMEM((BATCH, CHUNK, GATE_WIDTH), F32),
                        pltpu.VMEM((BATCH, SUBLANES, CHUNK), F32),
                        pltpu.VMEM((n_streams, 1, LANES), F32),
                        pltpu.VMEM((n_streams, 1, DH_QK), F32),
                        pltpu.VMEM((rows, D_MODEL), BF16),
                        pltpu.VMEM((CLASS_ROWS, LANES), F32)]
        + [pltpu.VMEM((DH_QK, DH_V), F32) for _ in range(n_streams)],
        compiler_params=_params("arbitrary"),
        name="mlstm_layer",
    )(x3d, x3d, mod, g1, w_in, bg, norm_g, w_out, g2, wr, br, _strict_upper(rows))
    token_order = lambda a: a.reshape(nc, BATCH, CHUNK).transpose(1, 0, 2).reshape(TOKENS)
    return (x1.reshape(TOKENS, D_MODEL), u2.reshape(TOKENS * ROW_TILE, LANES), token_order(cls),
            token_order(rank), cnt)


def _top2_sum(v0, v1, v2, v3):
    hi1, lo1 = jnp.maximum(v0, v1), jnp.minimum(v0, v1)
    hi2, lo2 = jnp.maximum(v2, v3), jnp.minimum(v2, v3)
    return jnp.maximum(hi1, hi2) + jnp.maximum(jnp.minimum(hi1, hi2), jnp.maximum(lo1, lo2))


def _route_tail(x_new, m, valid, g2_ref, wr_ref, br_ref, tri_ref, u2_ref, cls_ref, rank_ref, cnt_ref,
                run_ref, after_router=None):
    u2 = _moe_input(x_new, m, g2_ref)
    _store_token_rows(u2_ref, u2, u2.shape[0])
    _route(u2, valid, wr_ref, br_ref, tri_ref, cls_ref, rank_ref, cnt_ref, run_ref, after_router)


def _moe_input(x_new, m, g2_ref):
    return _rms(x_new) * g2_ref[...] * (1.0 + m[4:5]) + m[3:4]


def _round_robin(chains, background=None, every=1, start_after=0):
    chains = list(chains)
    advanced = 0
    while chains:
        alive = []
        for chain in chains:
            try:
                next(chain)
            except StopIteration:
                continue
            alive.append(chain)
            advanced += 1
            if background is not None and advanced >= start_after and advanced % every == 0:
                next(background, None)
        chains = alive
    if background is not None:
        for _ in background:
            pass


def _route(u2, valid, wr_ref, br_ref, tri_ref, cls_ref, rank_ref, cnt_ref, run_ref, after_router=None):
    stages = _route_stages(u2, valid, wr_ref, br_ref, tri_ref, cls_ref, rank_ref, cnt_ref, run_ref)
    next(stages)
    if after_router is not None:
        after_router()
    for _ in stages:
        pass


def _route_stages(u2, valid, wr_ref, br_ref, tri_ref, cls_ref, rank_ref, cnt_ref, run_ref):
    logits = _dot(u2.astype(BF16), wr_ref[...])
    yield
    lt = logits.T[0:N_EXPERTS, :]
    e = jnp.exp(lt - jnp.max(lt, axis=0, keepdims=True))
    probs = e / jnp.sum(e, axis=0, keepdims=True)
    sel = probs + br_ref[...]
    sel_rows = [sel[j:j + 1, :] for j in range(N_EXPERTS)]
    best = jnp.zeros_like(sel_rows[0], dtype=jnp.int32)
    best_score = _top2_sum(*sel_rows[0:EXPERTS_PER_GROUP])
    for g in range(1, N_GROUPS):
        score = _top2_sum(*sel_rows[g * EXPERTS_PER_GROUP:(g + 1) * EXPERTS_PER_GROUP])
        better = score > best_score
        best = jnp.where(better, g, best)
        best_score = jnp.where(better, score, best_score)
    s = []
    for j in range(EXPERTS_PER_GROUP):
        sj = sel_rows[j]
        for g in range(1, N_GROUPS):
            sj = jnp.where(best == g, sel_rows[g * EXPERTS_PER_GROUP + j], sj)
        s.append(sj)
    chosen = []
    for j in range(EXPERTS_PER_GROUP):
        beaten = jnp.zeros_like(best)
        for i in range(EXPERTS_PER_GROUP):
            if i == j:
                continue
            wins = (s[i] >= s[j]) if i < j else (s[i] > s[j])
            beaten = beaten + jnp.where(wins, 1, 0)
        chosen.append(beaten < 2)
    pair = jnp.full_like(best, len(PAIRS) - 1)
    for p in range(len(PAIRS) - 2, -1, -1):
        a, b = PAIRS[p]
        pair = jnp.where(jnp.logical_and(chosen[a], chosen[b]), p, pair)
    cls = best * len(PAIRS) + pair
    cls_ref[0] = cls

    class_id = lax.broadcasted_iota(jnp.int32, (CLASS_ROWS, cls.shape[1]), 0)
    onehot = class_id == cls
    before = _dot(jnp.where(onehot, 1.0, 0.0).astype(BF16), tri_ref[...])
    run = run_ref[...]
    rank = jnp.sum(jnp.where(onehot, before + run[:, 0:1], 0.0), axis=0, keepdims=True)
    rank_ref[0] = rank.astype(jnp.int32)
    run = run + valid * jnp.sum(jnp.where(onehot, 1.0, 0.0), axis=1, keepdims=True)
    run_ref[...] = run
    cnt_ref[...] = run.astype(jnp.int32)


def _strict_upper(n):
    return (jnp.arange(n)[:, None] < jnp.arange(n)[None, :]).astype(BF16)


def _route_out_shapes(t, tm):
    return (
        jax.ShapeDtypeStruct((t, D_MODEL), F32),
        jax.ShapeDtypeStruct((t * ROW_TILE, LANES), F32),
        jax.ShapeDtypeStruct((t // tm, 1, tm), jnp.int32),
        jax.ShapeDtypeStruct((t // tm, 1, tm), jnp.int32),
        jax.ShapeDtypeStruct((CLASS_ROWS, LANES), jnp.int32),
    )


def _cur_tile(n_tiles):
    return lambda i: jnp.minimum(i, n_tiles - 1)


def _prev_tile(i):
    return jnp.maximum(i - 1, 0)


def _route_out_specs(tm, n_tiles):
    cur = _cur_tile(n_tiles)
    return (
        pl.BlockSpec((tm, D_MODEL), lambda i, *_: (cur(i), 0)),
        pl.BlockSpec((tm * ROW_TILE, LANES), lambda i, *_: (_prev_tile(i), 0)),
        pl.BlockSpec((1, 1, tm), lambda i, *_: (_prev_tile(i), 0, 0)),
        pl.BlockSpec((1, 1, tm), lambda i, *_: (_prev_tile(i), 0, 0)),
        pl.BlockSpec((CLASS_ROWS, LANES), lambda i, *_: (0, 0)),
    )


def _init_deferred_tail(xprev_ref, run_ref):
    @pl.when(pl.program_id(0) == 0)
    def _():
        xprev_ref[...] = jnp.zeros_like(xprev_ref)
        run_ref[...] = jnp.zeros_like(run_ref)


def _deferred_tail(xprev_ref, modp_ref, *tail_refs, after_router=None):
    valid = jnp.where(pl.program_id(0) > 0, 1.0, 0.0)
    _route_tail(xprev_ref[...], modp_ref[0], valid, *tail_refs, after_router=after_router)


def _const_spec(shape):
    return pl.BlockSpec(shape, lambda i, *_: (0,) * len(shape))


def _conv_layer_kernel(dest_ref, x_ref, ys_ref, mod0_ref, mod_ref, modp_ref, g1_ref, wi_ref, cw_ref,
                       cb_ref, wo_ref, g2_ref, wr_ref, br_ref, tri_ref,
                       x3_ref, u2_ref, cls_ref, rank_ref, cnt_ref,
                       run_ref, carry_ref, ybuf_ref, ysem_ref, xprev_ref):
    tm = x_ref.shape[0]
    tiles_per_batch = SEQ // tm
    n_tiles = pl.num_programs(0) - 1
    i = pl.program_id(0)

    @pl.when(i % tiles_per_batch == 0)
    def _():
        carry_ref[...] = jnp.zeros_like(carry_ref)

    _init_deferred_tail(xprev_ref, run_ref)
    m = mod_ref[0]
    y = _gathered_tile(dest_ref, ys_ref, ybuf_ref, ysem_ref, tm, issue_first=False, n_tiles=n_tiles)
    x2 = x_ref[...] + mod0_ref[0][5:6] * y
    u = (_rms(x2) * g1_ref[...] * (1.0 + m[1:2]) + m[0:1]).astype(BF16)
    bgate = _dot(u, wi_ref[:, 0:D_MODEL])
    gated = []

    def project_gated():
        gated.append(_dot(u, wi_ref[:, D_MODEL:2 * D_MODEL]) * _dot(u, wi_ref[:, 2 * D_MODEL:]))

    _deferred_tail(xprev_ref, modp_ref, g2_ref, wr_ref, br_ref, tri_ref, u2_ref, cls_ref, rank_ref,
                   cnt_ref, run_ref, after_router=project_gated)
    z = gated[0]
    prev = carry_ref[...]
    row = lax.broadcasted_iota(jnp.int32, z.shape, 0)
    z1 = jnp.where(row == 0, prev[7:8], pltpu.roll(z, 1, axis=0))
    z2 = jnp.where(row == 0, prev[6:7], jnp.where(row == 1, prev[7:8], pltpu.roll(z, 2, axis=0)))
    carry_ref[...] = z[tm - SUBLANES:, :]
    cw = cw_ref[...]
    zc = cw[0:1] * z2 + cw[1:2] * z1 + cw[2:3] * z + cb_ref[...]
    x3 = x2 + m[2:3] * _dot((bgate * zc).astype(BF16), wo_ref[...])
    xprev_ref[...] = x3

    @pl.when(i < n_tiles)
    def _():
        x3_ref[...] = x3

    _drain_row_gather(ys_ref, ybuf_ref, ysem_ref)


def _conv_layer(dest, x, ys, mod0, mod, g1, w_in, conv_w, conv_b, w_out, g2, wr, br):
    t = x.shape[0]
    tm = TM_CONV
    tri = _strict_upper(tm)
    tiles_per_batch = SEQ // tm
    n_tiles = t // tm
    cur = _cur_tile(n_tiles)
    mod_spec = pl.BlockSpec((1, 6, D_MODEL), lambda i, *_: (cur(i) // tiles_per_batch, 0, 0))
    modp_spec = pl.BlockSpec((1, 6, D_MODEL), lambda i, *_: (_prev_tile(i) // tiles_per_batch, 0, 0))
    return pl.pallas_call(
        _conv_layer_kernel,
        out_shape=_route_out_shapes(t, tm),
        grid_spec=pltpu.PrefetchScalarGridSpec(
            num_scalar_prefetch=1,
            grid=(n_tiles + 1,),
            in_specs=[
                pl.BlockSpec((tm, D_MODEL), lambda i, *_: (cur(i), 0)),
                pl.BlockSpec(memory_space=pl.ANY),
                mod_spec,
                mod_spec,
                modp_spec,
                _const_spec((1, D_MODEL)),
                _const_spec((D_MODEL, 3 * D_MODEL)),
                _const_spec((3, D_MODEL)),
                _const_spec((1, D_MODEL)),
                _const_spec((D_MODEL, D_MODEL)),
                _const_spec((1, D_MODEL)),
                _const_spec((D_MODEL, ROUTER_COLS)),
                _const_spec((N_EXPERTS, 1)),
                _const_spec((tm, tm)),
            ],
            out_specs=_route_out_specs(tm, n_tiles),
            scratch_shapes=[
                pltpu.VMEM((CLASS_ROWS, LANES), F32),
                pltpu.VMEM((SUBLANES, D_MODEL), F32),
                pltpu.VMEM((2, tm * ROW_TILE, LANES), F32),
                pltpu.SemaphoreType.DMA((2,)),
                pltpu.VMEM((tm, D_MODEL), F32),
            ],
        ),
        compiler_params=_params("arbitrary"),
        name="conv_layer_route",
    )(dest, x, ys, mod0, mod, mod, g1, w_in, conv_w, conv_b, w_out, g2, wr, br, tri)


def _invert_kernel(dest_ref, fill_ref, src_ref, sem):
    fill = pltpu.make_async_copy(fill_ref, src_ref, sem)
    fill.start()
    fill.wait()
    unroll = 16

    def body(t8, carry):
        for k in range(unroll):
            t = t8 * unroll + k
            src_ref[dest_ref[t]] = t
        return carry

    lax.fori_loop(0, dest_ref.shape[0] // unroll, body, 0)


def _invert_permutation(dest, n_out):
    smem = pl.BlockSpec(memory_space=pltpu.SMEM)
    return pl.pallas_call(
        _invert_kernel,
        out_shape=jax.ShapeDtypeStruct((n_out,), jnp.int32),
        in_specs=[smem, pl.BlockSpec(memory_space=pl.ANY)],
        out_specs=smem,
        scratch_shapes=[pltpu.SemaphoreType.DMA(())],
        name="invert_permutation",
    )(dest, jnp.arange(n_out, dtype=jnp.int32) % dest.shape[0])


def _expert_kernel(ea_ref, eb_ref, chg_ref, nused_ref, src_ref, u2_ref, wga_ref, wua_ref, wda_ref,
                   wgb_ref, wub_ref, wdb_ref, wra_ref, wrb_ref, y_ref, xbuf_ref, xsem_ref, *wbf_refs):
    del ea_ref, eb_ref
    j = pl.program_id(0)
    tm = TM_EXPERT

    @pl.when(chg_ref[j] == 1)
    def _():
        for src, dst in zip((wga_ref, wua_ref, wda_ref, wgb_ref, wub_ref, wdb_ref), wbf_refs):
            dst[...] = src[0, 0].astype(BF16)

    load_rows, start_next = _gathered_tile(src_ref, u2_ref, xbuf_ref, xsem_ref, tm, issue_first=None,
                                           n_active=nused_ref[0])

    @pl.when(j < nused_ref[0])
    def _():
        xb = load_rows().astype(BF16)
        start_next()
        dl = jnp.sum(xb.astype(F32) * (wra_ref[0] - wrb_ref[0]), axis=-1, keepdims=True)
        w_a = _sigmoid(dl)
        w_b = _sigmoid(-dl)

        def ffn(wg_ref, wu_ref, wd_ref):
            gate = _dot(xb, wg_ref[...])
            hidden = gate * _sigmoid(gate) * _dot(xb, wu_ref[...])
            return _dot(hidden.astype(BF16), wd_ref[...])

        y = w_a * ffn(*wbf_refs[0:3]) + w_b * ffn(*wbf_refs[3:6])
        _store_token_rows(y_ref, y, tm)

    @pl.when(j >= nused_ref[0])
    def _():
        y_ref[...] = jnp.zeros_like(y_ref)

    _drain_row_gather(u2_ref, xbuf_ref, xsem_ref, n_active=nused_ref[0])


def _experts(layer, ea, eb, chg, n_used, src, u2, w_gate, w_up, w_down, wr_rows):
    tm = TM_EXPERT
    sel_a = lambda j, ea, eb, *_: (layer, ea[j], 0, 0)
    sel_b = lambda j, ea, eb, *_: (layer, eb[j], 0, 0)
    up_spec = lambda sel: pl.BlockSpec((1, 1, D_MODEL, D_EXPERT), sel)
    down_spec = lambda sel: pl.BlockSpec((1, 1, D_EXPERT, D_MODEL), sel)
    wr_spec = lambda sel: pl.BlockSpec((1, 1, D_MODEL), lambda *a: sel(*a)[1:])
    up_scratch = pltpu.VMEM((D_MODEL, D_EXPERT), BF16)
    down_scratch = pltpu.VMEM((D_EXPERT, D_MODEL), BF16)
    return pl.pallas_call(
        _expert_kernel,
        out_shape=jax.ShapeDtypeStruct((PADDED_ROWS * ROW_TILE, LANES), F32),
        grid_spec=pltpu.PrefetchScalarGridSpec(
            num_scalar_prefetch=4,
            grid=(N_EXPERT_TILES,),
            in_specs=[
                pl.BlockSpec(memory_space=pltpu.SMEM),
                pl.BlockSpec(memory_space=pl.ANY),
                up_spec(sel_a), up_spec(sel_a), down_spec(sel_a),
                up_spec(sel_b), up_spec(sel_b), down_spec(sel_b),
                wr_spec(sel_a), wr_spec(sel_b),
            ],
            out_specs=pl.BlockSpec((tm * ROW_TILE, LANES), lambda j, *_: (j, 0)),
            scratch_shapes=[pltpu.VMEM((EXPERT_GATHER_BUFFERS, tm * ROW_TILE, LANES), F32),
                            pltpu.SemaphoreType.DMA((EXPERT_GATHER_BUFFERS,)),
                            up_scratch, up_scratch, down_scratch, up_scratch, up_scratch, down_scratch],
        ),
        compiler_params=_params("arbitrary"),
        name="grouped_experts",
    )(ea, eb, chg, n_used, src, u2, w_gate, w_up, w_down, w_gate, w_up, w_down, wr_rows, wr_rows)


def _final_kernel(dest_ref, x_ref, ys_ref, mod_ref, g_ref, o_ref, ybuf_ref, ysem_ref):
    y = _gathered_tile(dest_ref, ys_ref, ybuf_ref, ysem_ref, x_ref.shape[0], issue_first=True)
    x = x_ref[...] + mod_ref[0][5:6] * y
    o_ref[...] = _rms(x) * g_ref[...]
    _drain_row_gather(ys_ref, ybuf_ref, ysem_ref)


def _final(dest, x, ys, mod, g):
    t = x.shape[0]
    tm = TM_FINAL
    tiles_per_batch = SEQ // tm
    return pl.pallas_call(
        _final_kernel,
        out_shape=jax.ShapeDtypeStruct((t, D_MODEL), F32),
        grid_spec=pltpu.PrefetchScalarGridSpec(
            num_scalar_prefetch=1,
            grid=(t // tm,),
            in_specs=[
                pl.BlockSpec((tm, D_MODEL), lambda i, *_: (i, 0)),
                pl.BlockSpec(memory_space=pl.ANY),
                pl.BlockSpec((1, 6, D_MODEL), lambda i, *_: (i // tiles_per_batch, 0, 0)),
                _const_spec((1, D_MODEL)),
            ],
            out_specs=pl.BlockSpec((tm, D_MODEL), lambda i, *_: (i, 0)),
            scratch_shapes=[pltpu.VMEM((2, tm * ROW_TILE, LANES), F32), pltpu.SemaphoreType.DMA((2,))],
        ),
        compiler_params=_params("arbitrary"),
        name="final_norm",
    )(dest, x, ys, mod, g)


_PAIR_A = np.array([EXPERTS_PER_GROUP * (c // len(PAIRS)) + PAIRS[c % len(PAIRS)][0]
                    for c in range(N_CLASSES)], np.int32)
_PAIR_B = np.array([EXPERTS_PER_GROUP * (c // len(PAIRS)) + PAIRS[c % len(PAIRS)][1]
                    for c in range(N_CLASSES)], np.int32)


def _dest_kernel(start_ref, cls_ref, rank_ref, dest_ref):
    cls = cls_ref[...]
    dest = rank_ref[...]
    for c in range(N_CLASSES):
        dest = dest + jnp.where(cls == c, start_ref[c], 0)
    dest_ref[...] = dest


def _sorted_rows(row_start, cls, rank):
    shape = (TOKENS // LANES, LANES)
    dest = pl.pallas_call(
        _dest_kernel,
        out_shape=jax.ShapeDtypeStruct(shape, jnp.int32),
        in_specs=[pl.BlockSpec(memory_space=pltpu.SMEM), pl.BlockSpec(memory_space=pltpu.VMEM),
                  pl.BlockSpec(memory_space=pltpu.VMEM)],
        out_specs=pl.BlockSpec(memory_space=pltpu.VMEM),
        name="sorted_rows",
    )(row_start, cls.reshape(shape), rank.reshape(shape))
    return dest.reshape(-1)


def _moe(layer, u2, cls, rank, cnt, w_gate, w_up, w_down, wr_rows):
    counts = cnt[:N_CLASSES, 0]
    tiles = (counts + TM_EXPERT - 1) // TM_EXPERT
    tile_end = jnp.cumsum(tiles)
    row_start = (tile_end - tiles) * TM_EXPERT
    n_used = tile_end[-1:]
    dest = _sorted_rows(row_start, cls, rank)
    tile_cls = jnp.sum(jnp.arange(N_EXPERT_TILES)[:, None] >= tile_end[None, :], axis=1)
    tile_cls = jnp.minimum(tile_cls, tile_cls[jnp.maximum(n_used[0] - 1, 0)])
    ea = jnp.asarray(_PAIR_A)[tile_cls]
    eb = jnp.asarray(_PAIR_B)[tile_cls]
    chg = jnp.concatenate([jnp.ones((1,), jnp.int32),
                           (tile_cls[1:] != tile_cls[:-1]).astype(jnp.int32)])
    src = _invert_permutation(dest, PADDED_ROWS)
    ys = _experts(layer, ea, eb, chg, n_used.astype(jnp.int32), src, u2, w_gate, w_up, w_down, wr_rows)
    return dest, ys


def kernel(x, c, norm1_g, norm2_g, w_ada, b_ada, m_w_in, m_b_gates, m_norm_g, m_w_out,
           c_w_in, c_conv_w, c_conv_b, c_w_out, w_router, b_router,
           e_w_gate, e_w_up, e_w_down, final_g):
    xf = x.reshape(TOKENS, D_MODEL)
    mod = _ada(c, w_ada, b_ada)

    w_in = jnp.concatenate(
        [m_w_in[0].astype(BF16),
         jnp.zeros((D_MODEL, GATE_COLS - 2 * N_HEADS), BF16)], axis=1)
    bg = jnp.zeros((1, GATE_COLS), F32).at[0, :2 * N_HEADS].set(m_b_gates[0])
    wr = jnp.zeros((D_MODEL, ROUTER_COLS), BF16).at[:, :N_EXPERTS].set(w_router.astype(BF16))
    wr_rows = w_router.astype(BF16).astype(F32).T.reshape(N_EXPERTS, 1, D_MODEL)
    br = b_router.reshape(N_EXPERTS, 1)
    row = lambda v: v.reshape(1, -1)

    x1, u2, cls, rank, cnt = _mlstm_layer(xf, mod[0], row(norm1_g[0]), w_in, bg, row(m_norm_g[0]),
                                          m_w_out[0].astype(BF16), row(norm2_g[0]), wr, br)
    dest, ys = _moe(0, u2, cls, rank, cnt, e_w_gate, e_w_up, e_w_down, wr_rows)

    x3, u2, cls, rank, cnt = _conv_layer(
        dest, x1, ys, mod[0], mod[1], row(norm1_g[1]), c_w_in[0].astype(BF16), c_conv_w[0],
        row(c_conv_b[0]), c_w_out[0].astype(BF16), row(norm2_g[1]), wr, br)
    dest, ys = _moe(1, u2, cls, rank, cnt, e_w_gate, e_w_up, e_w_down, wr_rows)

    out = _final(dest, x3, ys, mod[1], row(final_g))
    return out.reshape(BATCH, SEQ, D_MODEL)
```

```python
import functools

import jax
import jax.numpy as jnp
import numpy as np
from jax import lax
from jax.experimental import pallas as pl
from jax.experimental.pallas import tpu as pltpu

F32 = jnp.float32
BF16 = jnp.bfloat16

D_MODEL = 1024
BATCH = 4
SEQ = 8192
TOKENS = BATCH * SEQ
N_HEADS = 4
DH_V = 256
DH_QK = 128
QK = N_HEADS * DH_QK
N_EXPERTS = 16
N_GROUPS = 4
EXPERTS_PER_GROUP = 4
D_EXPERT = 512
EPS = 1e-6

LANES = 128
SUBLANES = 8
VMEM_LIMIT_BYTES = 56 * 1024 * 1024

CHUNK = 128
TM_CONV = 512
TM_EXPERT = 512
GATE_COLS = LANES
ROUTER_COLS = LANES
GATE_WIDTH = 16
PAIRS = ((0, 1), (0, 2), (0, 3), (1, 2), (1, 3), (2, 3))
N_CLASSES = N_GROUPS * len(PAIRS)
CLASS_ROWS = 32
N_EXPERT_TILES = TOKENS // TM_EXPERT + N_CLASSES
PADDED_ROWS = N_EXPERT_TILES * TM_EXPERT
TM_FINAL = 512
EXPERT_GATHER_BUFFERS = 4
PROJ_COLS = 256
PROJ_EVERY = 1


def _params(*semantics):
    return pltpu.CompilerParams(dimension_semantics=semantics, vmem_limit_bytes=VMEM_LIMIT_BYTES)


def _dot(a, b):
    return jnp.dot(a, b, preferred_element_type=F32)


def _rms(x):
    return x * lax.rsqrt(jnp.mean(x * x, axis=-1, keepdims=True) + EPS)


def _sigmoid(x):
    return 1.0 / (1.0 + jnp.exp(-x))


ROW_TILE = D_MODEL // LANES


def _load_token_rows(ref, n):
    return jnp.concatenate([ref[pl.ds(c, n, stride=ROW_TILE), :] for c in range(ROW_TILE)], axis=1)


def _store_token_rows(ref, val, n):
    for c in range(ROW_TILE):
        ref[pl.ds(c, n, stride=ROW_TILE), :] = val[:, c * LANES:(c + 1) * LANES]


def _token_tile(ref, t):
    return ref.at[pl.ds(pl.multiple_of(t * ROW_TILE, ROW_TILE), ROW_TILE)]


def _start_row_gather(idx_ref, base, n, src_hbm, buf, sem):
    for r in range(n):
        pltpu.make_async_copy(_token_tile(src_hbm, idx_ref[base + r]), _token_tile(buf, r),
                              sem).start(priority=r % 2)


def _wait_row_gather(src_hbm, buf, sem):
    pltpu.make_async_copy(src_hbm.at[pl.ds(0, buf.shape[0])], buf, sem).wait()


def _gathered_tile(idx_ref, src_hbm, buf_ref, sem_ref, n, issue_first, n_tiles=None, n_active=None):
    i = pl.program_id(0)
    last = (pl.num_programs(0) if n_tiles is None else n_tiles) - 1
    nb = buf_ref.shape[0]
    ahead = nb - 1

    @pl.when(i == 0)
    def _():
        for k in range(ahead):
            _start_row_gather(idx_ref, jnp.minimum(k, last) * n, n, src_hbm, buf_ref.at[k], sem_ref.at[k])

    nxt = (i + ahead) % nb
    start_next = functools.partial(
        _start_row_gather, idx_ref, jnp.minimum(i + ahead, last) * n, n, src_hbm, buf_ref.at[nxt],
        sem_ref.at[nxt])
    if issue_first:
        start_next()
    slot = i % nb
    if n_active is None:
        _wait_row_gather(src_hbm, buf_ref.at[slot], sem_ref.at[slot])
    else:
        @pl.when(i < n_active + ahead)
        def _():
            _wait_row_gather(src_hbm, buf_ref.at[slot], sem_ref.at[slot])

    load_rows = functools.partial(_load_token_rows, buf_ref.at[slot], n)
    if issue_first is None:
        return load_rows, start_next
    rows = load_rows()
    if not issue_first:
        start_next()
    return rows


def _drain_row_gather(src_hbm, buf_ref, sem_ref, n_active=None):
    i = pl.program_id(0)
    nb = buf_ref.shape[0]

    @pl.when(i == pl.num_programs(0) - 1)
    def _():
        for k in range(1, nb):
            wait = functools.partial(_wait_row_gather, src_hbm, buf_ref.at[(i + k) % nb],
                                     sem_ref.at[(i + k) % nb])
            if n_active is None:
                wait()
            else:
                pl.when(i + k - (nb - 1) < n_active)(wait)


def _ada_kernel(c_ref, w_ref, b_ref, o_ref):
    c = c_ref[...]
    cond = c * _sigmoid(c)
    o_ref[0] = jnp.dot(cond, w_ref[0], preferred_element_type=F32,
                       precision=lax.Precision.HIGHEST) + b_ref[0]


def _ada(c, w_ada, b_ada):
    depth, d, n = w_ada.shape
    tn = 1536
    c8 = jnp.zeros((SUBLANES, d), F32).at[:BATCH].set(c)
    out = pl.pallas_call(
        _ada_kernel,
        out_shape=jax.ShapeDtypeStruct((depth, SUBLANES, n), F32),
        grid=(depth, n // tn),
        in_specs=[
            pl.BlockSpec((SUBLANES, d), lambda l, j: (0, 0)),
            pl.BlockSpec((1, d, tn), lambda l, j: (l, 0, j)),
            pl.BlockSpec((1, 1, tn), lambda l, j: (l, 0, j)),
        ],
        out_specs=pl.BlockSpec((1, SUBLANES, tn), lambda l, j: (l, 0, j)),
        compiler_params=_params("arbitrary", "arbitrary"),
        name="ada_mod",
    )(c8, w_ada, b_ada.reshape(depth, 1, n))
    return out[:, :BATCH].reshape(depth, BATCH, 6, d)


def _mlstm_layer_kernel(xn_ref, x_ref, mod_ref, g1_ref, wi_ref, bg_ref, ng_ref,
                        wo_ref, g2_ref, wr_ref, br_ref, tri_ref,
                        x1_ref, u2_ref, cls_ref, rank_ref, cnt_ref,
                        q_ref, k_ref, v_ref, og_ref, gcol_ref, grow_ref, m_ref, n_ref, h_ref, run_ref,
                        *c_refs):
    step = pl.program_id(0)

    @pl.when(step == 0)
    def _():
        for ref in (q_ref, k_ref, v_ref, og_ref, gcol_ref, grow_ref, m_ref, n_ref, h_ref, run_ref) + c_refs:
            ref[...] = jnp.zeros_like(ref)

    h_prev = h_ref[...]
    ln = CHUNK
    row = lax.broadcasted_iota(jnp.int32, (ln, ln), 0)
    col = lax.broadcasted_iota(jnp.int32, (ln, ln), 1)
    causal = col <= row
    n_streams = BATCH * N_HEADS
    m_all = [m_ref[st] for st in range(n_streams)]
    n_all = [n_ref[st] for st in range(n_streams)]
    m_out, n_out = [None] * n_streams, [None] * n_streams

    def stream(st):
        bi, h = divmod(st, N_HEADS)
        gates = gcol_ref[bi]
        column = lambda kind: jnp.broadcast_to(
            gates[:, kind * N_HEADS + h:kind * N_HEADS + h + 1], (ln, LANES))
        cm, b, imb_col = column(0), column(1), column(2)
        imb_row = grow_ref[bi, h:h + 1, :]
        qh = q_ref[bi, :, h * DH_QK:(h + 1) * DH_QK]
        kh = k_ref[bi, :, h * DH_QK:(h + 1) * DH_QK]
        vh = v_ref[bi, :, h * DH_V:(h + 1) * DH_V]
        c_ref = c_refs[st]
        c_prev = c_ref[...]
        m_prev = m_all[st]
        s_raw = lax.dot_general(qh, kh, (((1,), (1,)), ((), ())), preferred_element_type=F32)
        yield
        big_m = jnp.maximum(m_prev, cm)
        d_mat = jnp.exp(jnp.where(causal, imb_row - big_m, -jnp.inf))
        q_inter = qh.astype(F32) * jnp.exp(m_prev - big_m)
        s = s_raw * d_mat
        lhs = jnp.concatenate([q_inter.astype(BF16), s.astype(BF16)], axis=1)
        rhs = jnp.concatenate([c_prev.astype(BF16), vh], axis=0)
        num = _dot(lhs, rhs)
        den = jnp.sum(s + q_inter * n_all[st], axis=-1, keepdims=True)
        yield
        m_last = big_m[ln - 1:ln, :]
        kw = (kh.astype(F32) * jnp.exp(imb_col - m_last)).astype(BF16)
        update = lax.dot_general(kw, vh, (((0,), (0,)), ((), ())), preferred_element_type=F32)
        yield
        decay = jnp.exp(m_prev - m_last)
        c_ref[...] = jnp.concatenate([decay] * 2, axis=1) * c_prev + update
        n_out[st] = decay * n_all[st] + jnp.sum(kw.astype(F32), axis=0, keepdims=True)
        m_out[st] = b[ln - 1:ln, :] + m_last
        inv = 1.0 / jnp.maximum(jnp.abs(den), jnp.exp(-(b + big_m)))
        hh = num * jnp.concatenate([inv, inv], axis=1)
        sl = slice(h * DH_V, (h + 1) * DH_V)
        gate = _sigmoid(og_ref[bi, :, sl].astype(F32))
        h_ref[bi * ln:(bi + 1) * ln, sl] = (_rms(hh) * ng_ref[:, sl] * gate).astype(BF16)

    deferred_stores = []

    def projection():
        mod = mod_ref[...]
        u = jnp.concatenate(
            [_rms(xn_ref[bi]) * g1_ref[...] * (1.0 + mod[bi][1:2]) + mod[bi][0:1] for bi in range(BATCH)],
            axis=0).astype(BF16)
        gt = _dot(u, wi_ref[:, 2 * QK + 2 * D_MODEL:]) + bg_ref[...]
        cols, imb = _gate_columns(gt)
        imb_rows = imb.T[0:SUBLANES, :]
        for bi in range(BATCH):
            gcol_ref[bi] = cols[bi * ln:(bi + 1) * ln, 0:GATE_WIDTH]
            grow_ref[bi] = imb_rows[:, bi * ln:(bi + 1) * ln]
        yield
        lo = 0
        for ref, width, scale in ((q_ref, QK, DH_QK ** -0.5), (k_ref, QK, None), (v_ref, D_MODEL, None),
                                  (og_ref, D_MODEL, None)):
            for off in range(0, width, PROJ_COLS):
                part = _dot(u, wi_ref[:, lo + off:lo + off + PROJ_COLS])
                if scale is not None:
                    part = part * scale
                part = part.astype(BF16)

                def store(ref=ref, off=off, part=part):
                    for bi in range(BATCH):
                        ref[bi, :, off:off + PROJ_COLS] = part[bi * ln:(bi + 1) * ln]

                if ref is og_ref:
                    deferred_stores.append(store)
                else:
                    store()
                yield
            lo += width

    def previous_chunk():
        half = D_MODEL // 2
        mix0 = _dot(h_prev, wo_ref[:, 0:half])
        yield
        mix1 = _dot(h_prev, wo_ref[:, half:])
        yield
        mod = mod_ref[...]
        u2_rows = []
        for bi in range(BATCH):
            mb = mod[bi]
            rows = slice(bi * ln, (bi + 1) * ln)
            x1 = x_ref[bi] + mb[2:3] * jnp.concatenate([mix0[rows], mix1[rows]], axis=1)
            x1_ref[bi] = x1
            u2_b = _moe_input(x1, mb, g2_ref)
            _store_token_rows(u2_ref.at[bi], u2_b, ln)
            u2_rows.append(u2_b)
        valid = jnp.where(step > 1, 1.0, 0.0)
        yield from _route_stages(jnp.concatenate(u2_rows, axis=0), valid, wr_ref, br_ref, tri_ref,
                                 cls_ref, rank_ref, cnt_ref, run_ref)

    _round_robin([previous_chunk()] + [stream(st) for st in range(n_streams)],
                 background=projection(), every=PROJ_EVERY, start_after=n_streams + 1)
    for store in deferred_stores:
        store()
    for st in range(n_streams):
        m_ref[st] = m_out[st]
        n_ref[st] = n_out[st]


def _gate_columns(gt):
    lane = lax.broadcasted_iota(jnp.int32, gt.shape, 1)
    pos = lax.broadcasted_iota(jnp.int32, gt.shape, 0) % CHUNK
    b = jnp.minimum(gt, 0.0) - jnp.log(1.0 + jnp.exp(-jnp.abs(gt)))
    shift = 1
    while shift < CHUNK:
        b = b + jnp.where(pos >= shift, pltpu.roll(b, shift, axis=0), 0.0)
        shift *= 2
    imb = gt - pltpu.roll(b, LANES - N_HEADS, axis=1)
    cm = imb
    shift = 1
    while shift < CHUNK:
        cm = jnp.maximum(cm, jnp.where(pos >= shift, pltpu.roll(cm, shift, axis=0), -jnp.inf))
        shift *= 2
    cols = jnp.where(lane < N_HEADS, cm,
                     jnp.where(lane < 2 * N_HEADS, b,
                               jnp.where(lane < 3 * N_HEADS, pltpu.roll(imb, 2 * N_HEADS, axis=1), 0.0)))
    return cols, imb


def _mlstm_layer(x, mod, g1, w_in, bg, norm_g, w_out, g2, wr, br):
    nc = SEQ // CHUNK
    n_streams = BATCH * N_HEADS
    rows = BATCH * CHUNK
    newest =lambda c: jnp.minimum(c, nc - 1)
    oldest = lambda c: jnp.maximum(c - 2, 0)
    old_rows = lambda height, width: pl.BlockSpec((BATCH, height, width), lambda c: (0, oldest(c), 0))
    x3d = x.reshape(BATCH, SEQ, D_MODEL)
    x1, u2, cls, rank, cnt = pl.pallas_call(
        _mlstm_layer_kernel,
        out_shape=(
            jax.ShapeDtypeStruct((BATCH, SEQ, D_MODEL), F32),
            jax.ShapeDtypeStruct((BATCH, SEQ * ROW_TILE, LANES), F32),
            jax.ShapeDtypeStruct((nc, 1, rows), jnp.int32),
            jax.ShapeDtypeStruct((nc, 1, rows), jnp.int32),
            jax.ShapeDtypeStruct((CLASS_ROWS, LANES), jnp.int32),
        ),
        grid=(nc + 2,),
        in_specs=[
            pl.BlockSpec((BATCH, CHUNK, D_MODEL), lambda c: (0, newest(c), 0)),
            old_rows(CHUNK, D_MODEL),
            _const_spec((BATCH, 6, D_MODEL)),
            _const_spec((1, D_MODEL)),
            _const_spec(w_in.shape),
            _const_spec((1, GATE_COLS)),
            _const_spec((1, D_MODEL)),
            _const_spec((D_MODEL, D_MODEL)),
            _const_spec((1, D_MODEL)),
            _const_spec((D_MODEL, ROUTER_COLS)),
            _const_spec((N_EXPERTS, 1)),
            _const_spec((rows, rows)),
        ],
        out_specs=(
            old_rows(CHUNK, D_MODEL),
            old_rows(CHUNK * ROW_TILE, LANES),
            pl.BlockSpec((1, 1, rows), lambda c: (oldest(c), 0, 0)),
            pl.BlockSpec((1, 1, rows), lambda c: (oldest(c), 0, 0)),
            _const_spec((CLASS_ROWS, LANES)),
        ),
        scratch_shapes=[pltpu.VMEM((BATCH, CHUNK, QK), BF16),
                        pltpu.VMEM((BATCH, CHUNK, QK), BF16),
                        pltpu.VMEM((BATCH, CHUNK, D_MODEL), BF16),
                        pltpu.VMEM((BATCH, CHUNK, D_MODEL), BF16),
                        pltpu.VMEM((BATCH, CHUNK, GATE_WIDTH), F32),
                        pltpu.VMEM((BATCH, SUBLANES, CHUNK), F32),
                        pltpu.VMEM((n_streams, 1, LANES), F32),
                        pltpu.VMEM((n_streams, 1, DH_QK), F32),
                        pltpu.VMEM((rows, D_MODEL), BF16),
                        pltpu.VMEM((CLASS_ROWS, LANES), F32)]
        + [pltpu.VMEM((DH_QK, DH_V), F32) for _ in range(n_streams)],
        compiler_params=_params("arbitrary"),
        name="mlstm_layer",
    )(x3d, x3d, mod, g1, w_in, bg, norm_g, w_out, g2, wr, br, _strict_upper(rows))
    token_order = lambda a: a.reshape(nc, BATCH, CHUNK).transpose(1, 0, 2).reshape(TOKENS)
    return (x1.reshape(TOKENS, D_MODEL), u2.reshape(TOKENS * ROW_TILE, LANES), token_order(cls),
            token_order(rank), cnt)


def _top2_sum(v0, v1, v2, v3):
    hi1, lo1 = jnp.maximum(v0, v1), jnp.minimum(v0, v1)
    hi2, lo2 = jnp.maximum(v2, v3), jnp.minimum(v2, v3)
    return jnp.maximum(hi1, hi2) + jnp.maximum(jnp.minimum(hi1, hi2), jnp.maximum(lo1, lo2))


def _route_tail(x_new, m, valid, g2_ref, wr_ref, br_ref, tri_ref, u2_ref, cls_ref, rank_ref, cnt_ref,
                run_ref, after_router=None):
    u2 = _moe_input(x_new, m, g2_ref)
    _store_token_rows(u2_ref, u2, u2.shape[0])
    _route(u2, valid, wr_ref, br_ref, tri_ref, cls_ref, rank_ref, cnt_ref, run_ref, after_router)


def _moe_input(x_new, m, g2_ref):
    return _rms(x_new) * g2_ref[...] * (1.0 + m[4:5]) + m[3:4]


def _round_robin(chains, background=None, every=1, start_after=0):
    chains = list(chains)
    advanced = 0
    while chains:
        alive = []
        for chain in chains:
            try:
                next(chain)
            except StopIteration:
                continue
            alive.append(chain)
            advanced += 1
            if background is not None and advanced >= start_after and advanced % every == 0:
                next(background, None)
        chains = alive
    if background is not None:
        for _ in background:
            pass


def _route(u2, valid, wr_ref, br_ref, tri_ref, cls_ref, rank_ref, cnt_ref, run_ref, after_router=None):
    stages = _route_stages(u2, valid, wr_ref, br_ref, tri_ref, cls_ref, rank_ref, cnt_ref, run_ref)
    next(stages)
    if after_router is not None:
        after_router()
    for _ in stages:
        pass


def _route_stages(u2, valid, wr_ref, br_ref, tri_ref, cls_ref, rank_ref, cnt_ref, run_ref):
    logits = _dot(u2.astype(BF16), wr_ref[...])
    yield
    lt = logits.T[0:N_EXPERTS, :]
    e = jnp.exp(lt - jnp.max(lt, axis=0, keepdims=True))
    probs = e / jnp.sum(e, axis=0, keepdims=True)
    sel = probs + br_ref[...]
    sel_rows = [sel[j:j + 1, :] for j in range(N_EXPERTS)]
    best = jnp.zeros_like(sel_rows[0], dtype=jnp.int32)
    best_score = _top2_sum(*sel_rows[0:EXPERTS_PER_GROUP])
    for g in range(1, N_GROUPS):
        score = _top2_sum(*sel_rows[g * EXPERTS_PER_GROUP:(g + 1) * EXPERTS_PER_GROUP])
        better = score > best_score
        best = jnp.where(better, g, best)
        best_score = jnp.where(better, score, best_score)
    s = []
    for j in range(EXPERTS_PER_GROUP):
        sj = sel_rows[j]
        for g in range(1, N_GROUPS):
            sj = jnp.where(best == g, sel_rows[g * EXPERTS_PER_GROUP + j], sj)
        s.append(sj)
    chosen = []
    for j in range(EXPERTS_PER_GROUP):
        beaten = jnp.zeros_like(best)
        for i in range(EXPERTS_PER_GROUP):
            if i == j:
                continue
            wins = (s[i] >= s[j]) if i < j else (s[i] > s[j])
            beaten = beaten + jnp.where(wins, 1, 0)
        chosen.append(beaten < 2)
    pair = jnp.full_like(best, len(PAIRS) - 1)
    for p in range(len(PAIRS) - 2, -1, -1):
        a, b = PAIRS[p]
        pair = jnp.where(jnp.logical_and(chosen[a], chosen[b]), p, pair)
    cls = best * len(PAIRS) + pair
    cls_ref[0] = cls

    class_id = lax.broadcasted_iota(jnp.int32, (CLASS_ROWS, cls.shape[1]), 0)
    onehot = class_id == cls
    before = _dot(jnp.where(onehot, 1.0, 0.0).astype(BF16), tri_ref[...])
    run = run_ref[...]
    rank = jnp.sum(jnp.where(onehot, before + run[:, 0:1], 0.0), axis=0, keepdims=True)
    rank_ref[0] = rank.astype(jnp.int32)
    run = run + valid * jnp.sum(jnp.where(onehot, 1.0, 0.0), axis=1, keepdims=True)
    run_ref[...] = run
    cnt_ref[...] = run.astype(jnp.int32)


def _strict_upper(n):
    return (jnp.arange(n)[:, None] < jnp.arange(n)[None, :]).astype(BF16)


def _route_out_shapes(t, tm):
    return (
        jax.ShapeDtypeStruct((t, D_MODEL), F32),
        jax.ShapeDtypeStruct((t * ROW_TILE, LANES), F32),
        jax.ShapeDtypeStruct((t // tm, 1, tm), jnp.int32),
        jax.ShapeDtypeStruct((t // tm, 1, tm), jnp.int32),
        jax.ShapeDtypeStruct((CLASS_ROWS, LANES), jnp.int32),
    )


def _cur_tile(n_tiles):
    return lambda i: jnp.minimum(i, n_tiles - 1)


def _prev_tile(i):
    return jnp.maximum(i - 1, 0)


def _route_out_specs(tm, n_tiles):
    cur = _cur_tile(n_tiles)
    return (
        pl.BlockSpec((tm, D_MODEL), lambda i, *_: (cur(i), 0)),
        pl.BlockSpec((tm * ROW_TILE, LANES), lambda i, *_: (_prev_tile(i), 0)),
        pl.BlockSpec((1, 1, tm), lambda i, *_: (_prev_tile(i), 0, 0)),
        pl.BlockSpec((1, 1, tm), lambda i, *_: (_prev_tile(i), 0, 0)),
        pl.BlockSpec((CLASS_ROWS, LANES), lambda i, *_: (0, 0)),
    )


def _init_deferred_tail(xprev_ref, run_ref):
    @pl.when(pl.program_id(0) == 0)
    def _():
        xprev_ref[...] = jnp.zeros_like(xprev_ref)
        run_ref[...] = jnp.zeros_like(run_ref)


def _deferred_tail(xprev_ref, modp_ref, *tail_refs, after_router=None):
    valid = jnp.where(pl.program_id(0) > 0, 1.0, 0.0)
    _route_tail(xprev_ref[...], modp_ref[0], valid, *tail_refs, after_router=after_router)


def _const_spec(shape):
    return pl.BlockSpec(shape, lambda i, *_: (0,) * len(shape))


def _conv_layer_kernel(dest_ref, x_ref, ys_ref, mod0_ref, mod_ref, modp_ref, g1_ref, wi_ref, cw_ref,
                       cb_ref, wo_ref, g2_ref, wr_ref, br_ref, tri_ref,
                       x3_ref, u2_ref, cls_ref, rank_ref, cnt_ref,
                       run_ref, carry_ref, ybuf_ref, ysem_ref, xprev_ref):
    tm = x_ref.shape[0]
    tiles_per_batch = SEQ // tm
    n_tiles = pl.num_programs(0) - 1
    i = pl.program_id(0)

    @pl.when(i % tiles_per_batch == 0)
    def _():
        carry_ref[...] = jnp.zeros_like(carry_ref)

    _init_deferred_tail(xprev_ref, run_ref)
    m = mod_ref[0]
    y = _gathered_tile(dest_ref, ys_ref, ybuf_ref, ysem_ref, tm, issue_first=False, n_tiles=n_tiles)
    x2 = x_ref[...] + mod0_ref[0][5:6] * y
    u = (_rms(x2) * g1_ref[...] * (1.0 + m[1:2]) + m[0:1]).astype(BF16)
    bgate = _dot(u, wi_ref[:, 0:D_MODEL])
    gated = []

    def project_gated():
        gated.append(_dot(u, wi_ref[:, D_MODEL:2 * D_MODEL]) * _dot(u, wi_ref[:, 2 * D_MODEL:]))

    _deferred_tail(xprev_ref, modp_ref, g2_ref, wr_ref, br_ref, tri_ref, u2_ref, cls_ref, rank_ref,
                   cnt_ref, run_ref, after_router=project_gated)
    z = gated[0]
    prev = carry_ref[...]
    row = lax.broadcasted_iota(jnp.int32, z.shape, 0)
    z1 = jnp.where(row == 0, prev[7:8], pltpu.roll(z, 1, axis=0))
    z2 = jnp.where(row == 0, prev[6:7], jnp.where(row == 1, prev[7:8], pltpu.roll(z, 2, axis=0)))
    carry_ref[...] = z[tm - SUBLANES:, :]
    cw = cw_ref[...]
    zc = cw[0:1] * z2 + cw[1:2] * z1 + cw[2:3] * z + cb_ref[...]
    x3 = x2 + m[2:3] * _dot((bgate * zc).astype(BF16), wo_ref[...])
    xprev_ref[...] = x3

    @pl.when(i < n_tiles)
    def _():
        x3_ref[...] = x3

    _drain_row_gather(ys_ref, ybuf_ref, ysem_ref)


def _conv_layer(dest, x, ys, mod0, mod, g1, w_in, conv_w, conv_b, w_out, g2, wr, br):
    t = x.shape[0]
    tm = TM_CONV
    tri = _strict_upper(tm)
    tiles_per_batch = SEQ // tm
    n_tiles = t // tm
    cur = _cur_tile(n_tiles)
    mod_spec = pl.BlockSpec((1, 6, D_MODEL), lambda i, *_: (cur(i) // tiles_per_batch, 0, 0))
    modp_spec = pl.BlockSpec((1, 6, D_MODEL), lambda i, *_: (_prev_tile(i) // tiles_per_batch, 0, 0))
    return pl.pallas_call(
        _conv_layer_kernel,
        out_shape=_route_out_shapes(t, tm),
        grid_spec=pltpu.PrefetchScalarGridSpec(
            num_scalar_prefetch=1,
            grid=(n_tiles + 1,),
            in_specs=[
                pl.BlockSpec((tm, D_MODEL), lambda i, *_: (cur(i), 0)),
                pl.BlockSpec(memory_space=pl.ANY),
                mod_spec,
                mod_spec,
                modp_spec,
                _const_spec((1, D_MODEL)),
                _const_spec((D_MODEL, 3 * D_MODEL)),
                _const_spec((3, D_MODEL)),
                _const_spec((1, D_MODEL)),
                _const_spec((D_MODEL, D_MODEL)),
                _const_spec((1, D_MODEL)),
                _const_spec((D_MODEL, ROUTER_COLS)),
                _const_spec((N_EXPERTS, 1)),
                _const_spec((tm, tm)),
            ],
            out_specs=_route_out_specs(tm, n_tiles),
            scratch_shapes=[
                pltpu.VMEM((CLASS_ROWS, LANES), F32),
                pltpu.VMEM((SUBLANES, D_MODEL), F32),
                pltpu.VMEM((2, tm * ROW_TILE, LANES), F32),
                pltpu.SemaphoreType.DMA((2,)),
                pltpu.VMEM((tm, D_MODEL), F32),
            ],
        ),
        compiler_params=_params("arbitrary"),
        name="conv_layer_route",
    )(dest, x, ys, mod0, mod, mod, g1, w_in, conv_w, conv_b, w_out, g2, wr, br, tri)


def _invert_kernel(dest_ref, fill_ref, src_ref, sem):
    fill = pltpu.make_async_copy(fill_ref, src_ref, sem)
    fill.start()
    fill.wait()
    unroll = 32

    def body(t8, carry):
        for k in range(unroll):
            t = t8 * unroll + k
            src_ref[dest_ref[t]] = t
        return carry

    lax.fori_loop(0, dest_ref.shape[0] // unroll, body, 0)


def _invert_permutation(dest, n_out):
    smem = pl.BlockSpec(memory_space=pltpu.SMEM)
    return pl.pallas_call(
        _invert_kernel,
        out_shape=jax.ShapeDtypeStruct((n_out,), jnp.int32),
        in_specs=[smem, pl.BlockSpec(memory_space=pl.ANY)],
        out_specs=smem,
        scratch_shapes=[pltpu.SemaphoreType.DMA(())],
        name="invert_permutation",
    )(dest, jnp.arange(n_out, dtype=jnp.int32) % dest.shape[0])


def _expert_kernel(ea_ref, eb_ref, chg_ref, nused_ref, src_ref, u2_ref, wga_ref, wua_ref, wda_ref,
                   wgb_ref, wub_ref, wdb_ref, wra_ref, wrb_ref, y_ref, xbuf_ref, xsem_ref, *wbf_refs):
    del ea_ref, eb_ref
    j = pl.program_id(0)
    tm = TM_EXPERT

    @pl.when(chg_ref[j] == 1)
    def _():
        for src, dst in zip((wga_ref, wua_ref, wda_ref, wgb_ref, wub_ref, wdb_ref), wbf_refs):
            dst[...] = src[0, 0].astype(BF16)

    load_rows, start_next = _gathered_tile(src_ref, u2_ref, xbuf_ref, xsem_ref, tm, issue_first=None,
                                           n_active=nused_ref[0])

    @pl.when(j < nused_ref[0])
    def _():
        xb = load_rows().astype(BF16)
        start_next()
        dl = jnp.sum(xb.astype(F32) * (wra_ref[0] - wrb_ref[0]), axis=-1, keepdims=True)
        w_a = _sigmoid(dl)
        w_b = _sigmoid(-dl)

        def ffn(wg_ref, wu_ref, wd_ref):
            gate = _dot(xb, wg_ref[...])
            hidden = gate * _sigmoid(gate) * _dot(xb, wu_ref[...])
            return _dot(hidden.astype(BF16), wd_ref[...])

        y = w_a * ffn(*wbf_refs[0:3]) + w_b * ffn(*wbf_refs[3:6])
        _store_token_rows(y_ref, y, tm)

    @pl.when(j >= nused_ref[0])
    def _():
        y_ref[...] = jnp.zeros_like(y_ref)

    _drain_row_gather(u2_ref, xbuf_ref, xsem_ref, n_active=nused_ref[0])


def _experts(layer, ea, eb, chg, n_used, src, u2, w_gate, w_up, w_down, wr_rows):
    tm = TM_EXPERT
    sel_a = lambda j, ea, eb, *_: (layer, ea[j], 0, 0)
    sel_b = lambda j, ea, eb, *_: (layer, eb[j], 0, 0)
    up_spec = lambda sel: pl.BlockSpec((1, 1, D_MODEL, D_EXPERT), sel)
    down_spec = lambda sel: pl.BlockSpec((1, 1, D_EXPERT, D_MODEL), sel)
    wr_spec = lambda sel: pl.BlockSpec((1, 1, D_MODEL), lambda *a: sel(*a)[1:])
    up_scratch = pltpu.VMEM((D_MODEL, D_EXPERT), BF16)
    down_scratch = pltpu.VMEM((D_EXPERT, D_MODEL), BF16)
    return pl.pallas_call(
        _expert_kernel,
        out_shape=jax.ShapeDtypeStruct((PADDED_ROWS * ROW_TILE, LANES), F32),
        grid_spec=pltpu.PrefetchScalarGridSpec(
            num_scalar_prefetch=4,
            grid=(N_EXPERT_TILES,),
            in_specs=[
                pl.BlockSpec(memory_space=pltpu.SMEM),
                pl.BlockSpec(memory_space=pl.ANY),
                up_spec(sel_a), up_spec(sel_a), down_spec(sel_a),
                up_spec(sel_b), up_spec(sel_b), down_spec(sel_b),
                wr_spec(sel_a), wr_spec(sel_b),
            ],
            out_specs=pl.BlockSpec((tm * ROW_TILE, LANES), lambda j, *_: (j, 0)),
            scratch_shapes=[pltpu.VMEM((EXPERT_GATHER_BUFFERS, tm * ROW_TILE, LANES), F32),
                            pltpu.SemaphoreType.DMA((EXPERT_GATHER_BUFFERS,)),
                            up_scratch, up_scratch, down_scratch, up_scratch, up_scratch, down_scratch],
        ),
        compiler_params=_params("arbitrary"),
        name="grouped_experts",
    )(ea, eb, chg, n_used, src, u2, w_gate, w_up, w_down, w_gate, w_up, w_down, wr_rows, wr_rows)


def _final_kernel(dest_ref, x_ref, ys_ref, mod_ref, g_ref, o_ref, ybuf_ref, ysem_ref):
    y = _gathered_tile(dest_ref, ys_ref, ybuf_ref, ysem_ref, x_ref.shape[0], issue_first=True)
    x = x_ref[...] + mod_ref[0][5:6] * y
    o_ref[...] = _rms(x) * g_ref[...]
    _drain_row_gather(ys_ref, ybuf_ref, ysem_ref)


def _final(dest, x, ys, mod, g):
    t = x.shape[0]
    tm = TM_FINAL
    tiles_per_batch = SEQ // tm
    return pl.pallas_call(
        _final_kernel,
        out_shape=jax.ShapeDtypeStruct((t, D_MODEL), F32),
        grid_spec=pltpu.PrefetchScalarGridSpec(
            num_scalar_prefetch=1,
            grid=(t // tm,),
            in_specs=[
                pl.BlockSpec((tm, D_MODEL), lambda i, *_: (i, 0)),
                pl.BlockSpec(memory_space=pl.ANY),
                pl.BlockSpec((1, 6, D_MODEL), lambda i, *_: (i // tiles_per_batch, 0, 0)),
                _const_spec((1, D_MODEL)),
            ],
            out_specs=pl.BlockSpec((tm, D_MODEL), lambda i, *_: (i, 0)),
            scratch_shapes=[pltpu.VMEM((2, tm * ROW_TILE, LANES), F32), pltpu.SemaphoreType.DMA((2,))],
        ),
        compiler_params=_params("arbitrary"),
        name="final_norm",
    )(dest, x, ys, mod, g)


_PAIR_A = np.array([EXPERTS_PER_GROUP * (c // len(PAIRS)) + PAIRS[c % len(PAIRS)][0]
                    for c in range(N_CLASSES)], np.int32)
_PAIR_B = np.array([EXPERTS_PER_GROUP * (c // len(PAIRS)) + PAIRS[c % len(PAIRS)][1]
                    for c in range(N_CLASSES)], np.int32)


def _dest_kernel(start_ref, cls_ref, rank_ref, dest_ref):
    cls = cls_ref[...]
    dest = rank_ref[...]
    for c in range(N_CLASSES):
        dest = dest + jnp.where(cls == c, start_ref[c], 0)
    dest_ref[...] = dest


def _sorted_rows(row_start, cls, rank):
    shape = (TOKENS // LANES, LANES)
    dest = pl.pallas_call(
        _dest_kernel,
        out_shape=jax.ShapeDtypeStruct(shape, jnp.int32),
        in_specs=[pl.BlockSpec(memory_space=pltpu.SMEM), pl.BlockSpec(memory_space=pltpu.VMEM),
                  pl.BlockSpec(memory_space=pltpu.VMEM)],
        out_specs=pl.BlockSpec(memory_space=pltpu.VMEM),
        name="sorted_rows",
    )(row_start, cls.reshape(shape), rank.reshape(shape))
    return dest.reshape(-1)


def _moe(layer, u2, cls, rank, cnt, w_gate, w_up, w_down, wr_rows):
    counts = cnt[:N_CLASSES, 0]
    tiles = (counts + TM_EXPERT - 1) // TM_EXPERT
    tile_end = jnp.cumsum(tiles)
    row_start = (tile_end - tiles) * TM_EXPERT
    n_used = tile_end[-1:]
    dest = _sorted_rows(row_start, cls, rank)
    tile_cls = jnp.sum(jnp.arange(N_EXPERT_TILES)[:, None] >= tile_end[None, :], axis=1)
    tile_cls = jnp.minimum(tile_cls, tile_cls[jnp.maximum(n_used[0] - 1, 0)])
    ea = jnp.asarray(_PAIR_A)[tile_cls]
    eb = jnp.asarray(_PAIR_B)[tile_cls]
    chg = jnp.concatenate([jnp.ones((1,), jnp.int32),
                           (tile_cls[1:] != tile_cls[:-1]).astype(jnp.int32)])
    src = _invert_permutation(dest, PADDED_ROWS)
    ys = _experts(layer, ea, eb, chg, n_used.astype(jnp.int32), src, u2, w_gate, w_up, w_down, wr_rows)
    return dest, ys


def kernel(x, c, norm1_g, norm2_g, w_ada, b_ada, m_w_in, m_b_gates, m_norm_g, m_w_out,
           c_w_in, c_conv_w, c_conv_b, c_w_out, w_router, b_router,
           e_w_gate, e_w_up, e_w_down, final_g):
    xf = x.reshape(TOKENS, D_MODEL)
    mod = _ada(c, w_ada, b_ada)

    w_in = jnp.concatenate(
        [m_w_in[0].astype(BF16),
         jnp.zeros((D_MODEL, GATE_COLS - 2 * N_HEADS), BF16)], axis=1)
    bg = jnp.zeros((1, GATE_COLS), F32).at[0, :2 * N_HEADS].set(m_b_gates[0])
    wr = jnp.zeros((D_MODEL, ROUTER_COLS), BF16).at[:, :N_EXPERTS].set(w_router.astype(BF16))
    wr_rows = w_router.astype(BF16).astype(F32).T.reshape(N_EXPERTS, 1, D_MODEL)
    br = b_router.reshape(N_EXPERTS, 1)
    row = lambda v: v.reshape(1, -1)

    x1, u2, cls, rank, cnt = _mlstm_layer(xf, mod[0], row(norm1_g[0]), w_in, bg, row(m_norm_g[0]),
                                          m_w_out[0].astype(BF16), row(norm2_g[0]), wr, br)
    dest, ys = _moe(0, u2, cls, rank, cnt, e_w_gate, e_w_up, e_w_down, wr_rows)

    x3, u2, cls, rank, cnt = _conv_layer(
        dest, x1, ys, mod[0], mod[1], row(norm1_g[1]), c_w_in[0].astype(BF16), c_conv_w[0],
        row(c_conv_b[0]), c_w_out[0].astype(BF16), row(norm2_g[1]), wr, br)
    dest, ys = _moe(1, u2, cls, rank, cnt, e_w_gate, e_w_up, e_w_down, wr_rows)

    out = _final(dest, x3, ys, mod[1], row(final_g))
    return out.reshape(BATCH, SEQ, D_MODEL)
```

```python
import functools

import jax
import jax.numpy as jnp
import numpy as np
from jax import lax
from jax.experimental import pallas as pl
from jax.experimental.pallas import tpu as pltpu

F32 = jnp.float32
BF16 = jnp.bfloat16

D_MODEL = 1024
BATCH = 4
SEQ = 8192
TOKENS = BATCH * SEQ
N_HEADS = 4
DH_V = 256
DH_QK = 128
QK = N_HEADS * DH_QK
N_EXPERTS = 16
N_GROUPS = 4
EXPERTS_PER_GROUP = 4
D_EXPERT = 512
EPS = 1e-6

LANES = 128
SUBLANES = 8
VMEM_LIMIT_BYTES = 56 * 1024 * 1024

CHUNK = 128
TM_CONV = 512
TM_EXPERT = 512
EXPERT_ROW_STEPS = (TM_EXPERT // 2, 3 * TM_EXPERT // 4, TM_EXPERT)
GATE_COLS = LANES
ROUTER_COLS = LANES
GATE_WIDTH = 16
PAIRS = ((0, 1), (0, 2), (0, 3), (1, 2), (1, 3), (2, 3))
N_CLASSES = N_GROUPS * len(PAIRS)
CLASS_ROWS = 32
N_EXPERT_TILES = TOKENS // TM_EXPERT + N_CLASSES
PADDED_ROWS = N_EXPERT_TILES * TM_EXPERT
TM_FINAL = 512
EXPERT_GATHER_BUFFERS = 4
PROJ_COLS = 256
PROJ_EVERY = 1


def _params(*semantics):
    return pltpu.CompilerParams(dimension_semantics=semantics, vmem_limit_bytes=VMEM_LIMIT_BYTES)


def _dot(a, b):
    return jnp.dot(a, b, preferred_element_type=F32)


def _rms(x):
    return x * lax.rsqrt(jnp.mean(x * x, axis=-1, keepdims=True) + EPS)


def _sigmoid(x):
    return 1.0 / (1.0 + jnp.exp(-x))


ROW_TILE = D_MODEL // LANES


def _load_token_rows(ref, n):
    return jnp.concatenate([ref[pl.ds(c, n, stride=ROW_TILE), :] for c in range(ROW_TILE)], axis=1)


def _store_token_rows(ref, val, n):
    for c in range(ROW_TILE):
        ref[pl.ds(c, n, stride=ROW_TILE), :] = val[:, c * LANES:(c + 1) * LANES]


def _token_tile(ref, t):
    return ref.at[pl.ds(pl.multiple_of(t * ROW_TILE, ROW_TILE), ROW_TILE)]


def _start_row_gather(idx_ref, base, n, src_hbm, buf, sem):
    for r in range(n):
        pltpu.make_async_copy(_token_tile(src_hbm, idx_ref[base + r]), _token_tile(buf, r),
                              sem).start(priority=r % 2)


def _wait_row_gather(src_hbm, buf, sem):
    pltpu.make_async_copy(src_hbm.at[pl.ds(0, buf.shape[0])], buf, sem).wait()


def _gathered_tile(idx_ref, src_hbm, buf_ref, sem_ref, n, issue_first, n_tiles=None, n_active=None):
    i = pl.program_id(0)
    last = (pl.num_programs(0) if n_tiles is None else n_tiles) - 1
    nb = buf_ref.shape[0]
    ahead = nb - 1

    @pl.when(i == 0)
    def _():
        for k in range(ahead):
            _start_row_gather(idx_ref, jnp.minimum(k, last) * n, n, src_hbm, buf_ref.at[k], sem_ref.at[k])

    nxt = (i + ahead) % nb
    start_next = functools.partial(
        _start_row_gather, idx_ref, jnp.minimum(i + ahead, last) * n, n, src_hbm, buf_ref.at[nxt],
        sem_ref.at[nxt])
    if issue_first:
        start_next()
    slot = i % nb
    if n_active is None:
        _wait_row_gather(src_hbm, buf_ref.at[slot], sem_ref.at[slot])
    else:
        @pl.when(i < n_active + ahead)
        def _():
            _wait_row_gather(src_hbm, buf_ref.at[slot], sem_ref.at[slot])

    load_rows = functools.partial(_load_token_rows, buf_ref.at[slot])
    if issue_first is None:
        return load_rows, start_next
    rows = load_rows(n)
    if not issue_first:
        start_next()
    return rows


def _drain_row_gather(src_hbm, buf_ref, sem_ref, n_active=None):
    i = pl.program_id(0)
    nb = buf_ref.shape[0]

    @pl.when(i == pl.num_programs(0) - 1)
    def _():
        for k in range(1, nb):
            wait = functools.partial(_wait_row_gather, src_hbm, buf_ref.at[(i + k) % nb],
                                     sem_ref.at[(i + k) % nb])
            if n_active is None:
                wait()
            else:
                pl.when(i + k - (nb - 1) < n_active)(wait)


def _ada_kernel(c_ref, w_ref, b_ref, o_ref):
    c = c_ref[...]
    cond = c * _sigmoid(c)
    o_ref[0] = jnp.dot(cond, w_ref[0], preferred_element_type=F32,
                       precision=lax.Precision.HIGHEST) + b_ref[0]


def _ada(c, w_ada, b_ada):
    depth, d, n = w_ada.shape
    tn = 1536
    c8 = jnp.zeros((SUBLANES, d), F32).at[:BATCH].set(c)
    out = pl.pallas_call(
        _ada_kernel,
        out_shape=jax.ShapeDtypeStruct((depth, SUBLANES, n), F32),
        grid=(depth, n // tn),
        in_specs=[
            pl.BlockSpec((SUBLANES, d), lambda l, j: (0, 0)),
            pl.BlockSpec((1, d, tn), lambda l, j: (l, 0, j)),
            pl.BlockSpec((1, 1, tn), lambda l, j: (l, 0, j)),
        ],
        out_specs=pl.BlockSpec((1, SUBLANES, tn), lambda l, j: (l, 0, j)),
        compiler_params=_params("arbitrary", "arbitrary"),
        name="ada_mod",
    )(c8, w_ada, b_ada.reshape(depth, 1, n))
    return out[:, :BATCH].reshape(depth, BATCH, 6, d)


def _mlstm_layer_kernel(xn_ref, x_ref, mod_ref, g1_ref, w32_ref, bg_ref, ng_ref,
                        wo_ref, g2_ref, wr_ref, br_ref, tri_ref,
                        x1_ref, u2_ref, cls_ref, rank_ref, cnt_ref,
                        wi_ref, q_ref, k_ref, v_ref, og_ref, gcol_ref, grow_ref, m_ref, n_ref, h_ref, run_ref,
                        *c_refs):
    step = pl.program_id(0)

    @pl.when(step == 0)
    def _():
        for ref in (q_ref, k_ref, v_ref, og_ref, gcol_ref, grow_ref, m_ref, n_ref, h_ref, run_ref) + c_refs:
            ref[...] = jnp.zeros_like(ref)
        n_main = 2 * QK + 2 * D_MODEL
        for lo in range(0, n_main, PROJ_COLS):
            wi_ref[:, lo:lo + PROJ_COLS] = w32_ref[0, :, lo:lo + PROJ_COLS].astype(BF16)
        gates = w32_ref[0, :, n_main:n_main + 2 * N_HEADS].astype(BF16)
        wi_ref[:, n_main:] = jnp.concatenate(
            [gates, jnp.zeros((D_MODEL, GATE_COLS - 2 * N_HEADS), BF16)], axis=1)

    h_prev = h_ref[...]
    ln = CHUNK
    row = lax.broadcasted_iota(jnp.int32, (ln, ln), 0)
    col = lax.broadcasted_iota(jnp.int32, (ln, ln), 1)
    causal = col <= row
    n_streams = BATCH * N_HEADS
    m_all = [m_ref[st] for st in range(n_streams)]
    n_all = [n_ref[st] for st in range(n_streams)]
    m_out, n_out = [None] * n_streams, [None] * n_streams

    def stream(st):
        bi, h = divmod(st, N_HEADS)
        gates = gcol_ref[bi]
        column = lambda kind: jnp.broadcast_to(
            gates[:, kind * N_HEADS + h:kind * N_HEADS + h + 1], (ln, LANES))
        cm, b, imb_col = column(0), column(1), column(2)
        imb_row = grow_ref[bi, h:h + 1, :]
        qh = q_ref[bi, :, h * DH_QK:(h + 1) * DH_QK]
        kh = k_ref[bi, :, h * DH_QK:(h + 1) * DH_QK]
        vh = v_ref[bi, :, h * DH_V:(h + 1) * DH_V]
        c_ref = c_refs[st]
        c_prev = c_ref[...]
        m_prev = m_all[st]
        s_raw = lax.dot_general(qh, kh, (((1,), (1,)), ((), ())), preferred_element_type=F32)
        yield
        big_m = jnp.maximum(m_prev, cm)
        d_mat = jnp.exp(jnp.where(causal, imb_row - big_m, -jnp.inf))
        q_inter = qh.astype(F32) * jnp.exp(m_prev - big_m)
        s = s_raw * d_mat
        lhs = jnp.concatenate([q_inter.astype(BF16), s.astype(BF16)], axis=1)
        rhs = jnp.concatenate([c_prev.astype(BF16), vh], axis=0)
        num = _dot(lhs, rhs)
        den = jnp.sum(s + q_inter * n_all[st], axis=-1, keepdims=True)
        yield
        m_last = big_m[ln - 1:ln, :]
        kw = (kh.astype(F32) * jnp.exp(imb_col - m_last)).astype(BF16)
        update = lax.dot_general(kw, vh, (((0,), (0,)), ((), ())), preferred_element_type=F32)
        yield
        decay = jnp.exp(m_prev - m_last)
        c_ref[...] = jnp.concatenate([decay] * 2, axis=1) * c_prev + update
        n_out[st] = decay * n_all[st] + jnp.sum(kw.astype(F32), axis=0, keepdims=True)
        m_out[st] = b[ln - 1:ln, :] + m_last
        inv = 1.0 / jnp.maximum(jnp.abs(den), jnp.exp(-(b + big_m)))
        hh = num * jnp.concatenate([inv, inv], axis=1)
        sl = slice(h * DH_V, (h + 1) * DH_V)
        gate = _sigmoid(og_ref[bi, :, sl].astype(F32))
        h_ref[bi * ln:(bi + 1) * ln, sl] = (_rms(hh) * ng_ref[:, sl] * gate).astype(BF16)

    deferred_stores = []

    def projection():
        mod = mod_ref[...]
        u = jnp.concatenate(
            [_rms(xn_ref[bi]) * g1_ref[...] * (1.0 + mod[bi][1:2]) + mod[bi][0:1] for bi in range(BATCH)],
            axis=0).astype(BF16)
        gt = _dot(u, wi_ref[:, 2 * QK + 2 * D_MODEL:]) + bg_ref[...]
        cols, imb = _gate_columns(gt)
        imb_rows = imb.T[0:SUBLANES, :]
        for bi in range(BATCH):
            gcol_ref[bi] = cols[bi * ln:(bi + 1) * ln, 0:GATE_WIDTH]
            grow_ref[bi] = imb_rows[:, bi * ln:(bi + 1) * ln]
        yield
        lo = 0
        for ref, width, scale in ((q_ref, QK, DH_QK ** -0.5), (k_ref, QK, None), (v_ref, D_MODEL, None),
                                  (og_ref, D_MODEL, None)):
            for off in range(0, width, PROJ_COLS):
                part = _dot(u, wi_ref[:, lo + off:lo + off + PROJ_COLS])
                if scale is not None:
                    part = part * scale
                part = part.astype(BF16)

                def store(ref=ref, off=off, part=part):
                    for bi in range(BATCH):
                        ref[bi, :, off:off + PROJ_COLS] = part[bi * ln:(bi + 1) * ln]

                if ref is og_ref:
                    deferred_stores.append(store)
                else:
                    store()
                yield
            lo += width

    def previous_chunk():
        half = D_MODEL // 2
        mix0 = _dot(h_prev, wo_ref[:, 0:half])
        yield
        mix1 = _dot(h_prev, wo_ref[:, half:])
        yield
        mod = mod_ref[...]
        u2_rows = []
        for bi in range(BATCH):
            mb = mod[bi]
            rows = slice(bi * ln, (bi + 1) * ln)
            x1 = x_ref[bi] + mb[2:3] * jnp.concatenate([mix0[rows], mix1[rows]], axis=1)
            x1_ref[bi] = x1
            u2_b = _moe_input(x1, mb, g2_ref)
            _store_token_rows(u2_ref.at[bi], u2_b, ln)
            u2_rows.append(u2_b)
        valid = jnp.where(step > 1, 1.0, 0.0)
        yield from _route_stages(jnp.concatenate(u2_rows, axis=0), valid, wr_ref, br_ref, tri_ref,
                                 cls_ref, rank_ref, cnt_ref, run_ref)

    _round_robin([previous_chunk()] + [stream(st) for st in range(n_streams)],
                 background=projection(), every=PROJ_EVERY, start_after=n_streams + 1)
    for store in deferred_stores:
        store()
    for st in range(n_streams):
        m_ref[st] = m_out[st]
        n_ref[st] = n_out[st]


def _gate_columns(gt):
    lane = lax.broadcasted_iota(jnp.int32, gt.shape, 1)
    pos = lax.broadcasted_iota(jnp.int32, gt.shape, 0) % CHUNK
    b = jnp.minimum(gt, 0.0) - jnp.log(1.0 + jnp.exp(-jnp.abs(gt)))
    shift = 1
    while shift < CHUNK:
        b = b + jnp.where(pos >= shift, pltpu.roll(b, shift, axis=0), 0.0)
        shift *= 2
    imb = gt - pltpu.roll(b, LANES - N_HEADS, axis=1)
    cm = imb
    shift = 1
    while shift < CHUNK:
        cm = jnp.maximum(cm, jnp.where(pos >= shift, pltpu.roll(cm, shift, axis=0), -jnp.inf))
        shift *= 2
    cols = jnp.where(lane < N_HEADS, cm,
                     jnp.where(lane < 2 * N_HEADS, b,
                               jnp.where(lane < 3 * N_HEADS, pltpu.roll(imb, 2 * N_HEADS, axis=1), 0.0)))
    return cols, imb


def _mlstm_layer(x, mod, g1, w_in, bg, norm_g, w_out, g2, wr, br):
    nc = SEQ // CHUNK
    n_streams = BATCH * N_HEADS
    rows = BATCH * CHUNK
    newest = lambda c: jnp.minimum(c, nc - 1)
    oldest = lambda c: jnp.maximum(c - 2, 0)
    old_rows = lambda height, width: pl.BlockSpec((BATCH, height, width), lambda c: (0, oldest(c), 0))
    x3d = x.reshape(BATCH, SEQ, D_MODEL)
    x1, u2, cls, rank, cnt = pl.pallas_call(
        _mlstm_layer_kernel,
        out_shape=(
            jax.ShapeDtypeStruct((BATCH, SEQ, D_MODEL), F32),
            jax.ShapeDtypeStruct((BATCH, SEQ * ROW_TILE, LANES), F32),
            jax.ShapeDtypeStruct((nc, 1, rows), jnp.int32),
            jax.ShapeDtypeStruct((nc, 1, rows), jnp.int32),
            jax.ShapeDtypeStruct((CLASS_ROWS, LANES), jnp.int32),
        ),
        grid=(nc + 2,),
        in_specs=[
            pl.BlockSpec((BATCH, CHUNK, D_MODEL), lambda c: (0, newest(c), 0)),
            old_rows(CHUNK, D_MODEL),
            _const_spec((BATCH, 6, D_MODEL)),
            _const_spec((1, D_MODEL)),
            _const_spec(w_in.shape),
            _const_spec((1, GATE_COLS)),
            _const_spec((1, D_MODEL)),
            _const_spec((D_MODEL, D_MODEL)),
            _const_spec((1, D_MODEL)),
            _const_spec((D_MODEL, ROUTER_COLS)),
            _const_spec((N_EXPERTS, 1)),
            _const_spec((rows, rows)),
        ],
        out_specs=(
            old_rows(CHUNK, D_MODEL),
            old_rows(CHUNK * ROW_TILE, LANES),
            pl.BlockSpec((1, 1, rows), lambda c: (oldest(c), 0, 0)),
            pl.BlockSpec((1, 1, rows), lambda c: (oldest(c), 0, 0)),
            _const_spec((CLASS_ROWS, LANES)),
        ),
        scratch_shapes=[pltpu.VMEM((D_MODEL, 2 * QK + 2 * D_MODEL + GATE_COLS), BF16),
                        pltpu.VMEM((BATCH, CHUNK, QK), BF16),
                        pltpu.VMEM((BATCH, CHUNK, QK), BF16),
                        pltpu.VMEM((BATCH, CHUNK, D_MODEL), BF16),
                        pltpu.VMEM((BATCH, CHUNK, D_MODEL), BF16),
                        pltpu.VMEM((BATCH, CHUNK, GATE_WIDTH), F32),
                        pltpu.VMEM((BATCH, SUBLANES, CHUNK), F32),
                        pltpu.VMEM((n_streams, 1, LANES), F32),
                        pltpu.VMEM((n_streams, 1, DH_QK), F32),
                        pltpu.VMEM((rows, D_MODEL), BF16),
                        pltpu.VMEM((CLASS_ROWS, LANES), F32)]
        + [pltpu.VMEM((DH_QK, DH_V), F32) for _ in range(n_streams)],
        compiler_params=_params("arbitrary"),
        name="mlstm_layer",
    )(x3d, x3d, mod, g1, w_in, bg, norm_g, w_out, g2, wr, br, _strict_upper(rows))
    token_order = lambda a: a.reshape(nc, BATCH, CHUNK).transpose(1, 0, 2).reshape(TOKENS)
    return (x1.reshape(TOKENS, D_MODEL), u2.reshape(TOKENS * ROW_TILE, LANES), token_order(cls),
            token_order(rank), cnt)


def _top2_sum(v0, v1, v2, v3):
    hi1, lo1 = jnp.maximum(v0, v1), jnp.minimum(v0, v1)
    hi2, lo2 = jnp.maximum(v2, v3), jnp.minimum(v2, v3)
    return jnp.maximum(hi1, hi2) + jnp.maximum(jnp.minimum(hi1, hi2), jnp.maximum(lo1, lo2))


def _route_tail(x_new, m, valid, g2_ref, wr_ref, br_ref, tri_ref, u2_ref, cls_ref, rank_ref, cnt_ref,
                run_ref, after_router=None):
    u2 = _moe_input(x_new, m, g2_ref)
    _store_token_rows(u2_ref, u2, u2.shape[0])
    _route(u2, valid, wr_ref, br_ref, tri_ref, cls_ref, rank_ref, cnt_ref, run_ref, after_router)


def _moe_input(x_new, m, g2_ref):
    return _rms(x_new) * g2_ref[...] * (1.0 + m[4:5]) + m[3:4]


def _round_robin(chains, background=None, every=1, start_after=0):
    chains = list(chains)
    advanced = 0
    while chains:
        alive = []
        for chain in chains:
            try:
                next(chain)
            except StopIteration:
                continue
            alive.append(chain)
            advanced += 1
            if background is not None and advanced >= start_after and advanced % every == 0:
                next(background, None)
        chains = alive
    if background is not None:
        for _ in background:
            pass


def _route(u2, valid, wr_ref, br_ref, tri_ref, cls_ref, rank_ref, cnt_ref, run_ref, after_router=None):
    stages = _route_stages(u2, valid, wr_ref, br_ref, tri_ref, cls_ref, rank_ref, cnt_ref, run_ref)
    next(stages)
    if after_router is not None:
        after_router()
    for _ in stages:
        pass


def _route_stages(u2, valid, wr_ref, br_ref, tri_ref, cls_ref, rank_ref, cnt_ref, run_ref):
    logits = _dot(u2.astype(BF16), wr_ref[...])
    yield
    lt = logits.T[0:N_EXPERTS, :]
    e = jnp.exp(lt - jnp.max(lt, axis=0, keepdims=True))
    probs = e / jnp.sum(e, axis=0, keepdims=True)
    sel = probs + br_ref[...]
    sel_rows = [sel[j:j + 1, :] for j in range(N_EXPERTS)]
    best = jnp.zeros_like(sel_rows[0], dtype=jnp.int32)
    best_score = _top2_sum(*sel_rows[0:EXPERTS_PER_GROUP])
    for g in range(1, N_GROUPS):
        score = _top2_sum(*sel_rows[g * EXPERTS_PER_GROUP:(g + 1) * EXPERTS_PER_GROUP])
        better = score > best_score
        best = jnp.where(better, g, best)
        best_score = jnp.where(better, score, best_score)
    s = []
    for j in range(EXPERTS_PER_GROUP):
        sj = sel_rows[j]
        for g in range(1, N_GROUPS):
            sj = jnp.where(best == g, sel_rows[g * EXPERTS_PER_GROUP + j], sj)
        s.append(sj)
    chosen = []
    for j in range(EXPERTS_PER_GROUP):
        beaten = jnp.zeros_like(best)
        for i in range(EXPERTS_PER_GROUP):
            if i == j:
                continue
            wins = (s[i] >= s[j]) if i < j else (s[i] > s[j])
            beaten = beaten + jnp.where(wins, 1, 0)
        chosen.append(beaten < 2)
    pair = jnp.full_like(best, len(PAIRS) - 1)
    for p in range(len(PAIRS) - 2, -1, -1):
        a, b = PAIRS[p]
        pair = jnp.where(jnp.logical_and(chosen[a], chosen[b]), p, pair)
    cls = best * len(PAIRS) + pair
    cls_ref[0] = cls

    class_id = lax.broadcasted_iota(jnp.int32, (CLASS_ROWS, cls.shape[1]), 0)
    onehot = class_id == cls
    before = _dot(jnp.where(onehot, 1.0, 0.0).astype(BF16), tri_ref[...])
    run = run_ref[...]
    rank = jnp.sum(jnp.where(onehot, before + run[:, 0:1], 0.0), axis=0, keepdims=True)
    rank_ref[0] = rank.astype(jnp.int32)
    run = run + valid * jnp.sum(jnp.where(onehot, 1.0, 0.0), axis=1, keepdims=True)
    run_ref[...] = run
    cnt_ref[...] = run.astype(jnp.int32)


def _strict_upper(n):
    return (jnp.arange(n)[:, None] < jnp.arange(n)[None, :]).astype(BF16)


def _route_out_shapes(t, tm):
    return (
        jax.ShapeDtypeStruct((t, D_MODEL), F32),
        jax.ShapeDtypeStruct((t * ROW_TILE, LANES), F32),
        jax.ShapeDtypeStruct((t // tm, 1, tm), jnp.int32),
        jax.ShapeDtypeStruct((t // tm, 1, tm), jnp.int32),
        jax.ShapeDtypeStruct((CLASS_ROWS, LANES), jnp.int32),
    )


def _cur_tile(n_tiles):
    return lambda i: jnp.minimum(i, n_tiles - 1)


def _prev_tile(i):
    return jnp.maximum(i - 1, 0)


def _route_out_specs(tm, n_tiles):
    cur = _cur_tile(n_tiles)
    return (
        pl.BlockSpec((tm, D_MODEL), lambda i, *_: (cur(i), 0)),
        pl.BlockSpec((tm * ROW_TILE, LANES), lambda i, *_: (_prev_tile(i), 0)),
        pl.BlockSpec((1, 1, tm), lambda i, *_: (_prev_tile(i), 0, 0)),
        pl.BlockSpec((1, 1, tm), lambda i, *_: (_prev_tile(i), 0, 0)),
        pl.BlockSpec((CLASS_ROWS, LANES), lambda i, *_: (0, 0)),
    )


def _init_deferred_tail(xprev_ref, run_ref):
    @pl.when(pl.program_id(0) == 0)
    def _():
        xprev_ref[...] = jnp.zeros_like(xprev_ref)
        run_ref[...] = jnp.zeros_like(run_ref)


def _deferred_tail(xprev_ref, modp_ref, *tail_refs, after_router=None):
    valid = jnp.where(pl.program_id(0) > 0, 1.0, 0.0)
    _route_tail(xprev_ref[...], modp_ref[0], valid, *tail_refs, after_router=after_router)


def _const_spec(shape):
    return pl.BlockSpec(shape, lambda i, *_: (0,) * len(shape))


def _conv_layer_kernel(dest_ref, x_ref, ys_ref, mod0_ref, mod_ref, modp_ref, g1_ref, wi_ref, cw_ref,
                       cb_ref, wo_ref, g2_ref, wr_ref, br_ref, tri_ref,
                       x3_ref, u2_ref, cls_ref, rank_ref, cnt_ref,
                       run_ref, carry_ref, ybuf_ref, ysem_ref, xprev_ref):
    tm = x_ref.shape[0]
    tiles_per_batch = SEQ // tm
    n_tiles = pl.num_programs(0) - 1
    i = pl.program_id(0)

    @pl.when(i % tiles_per_batch == 0)
    def _():
        carry_ref[...] = jnp.zeros_like(carry_ref)

    _init_deferred_tail(xprev_ref, run_ref)
    m = mod_ref[0]
    y = _gathered_tile(dest_ref, ys_ref, ybuf_ref, ysem_ref, tm, issue_first=False, n_tiles=n_tiles)
    x2 = x_ref[...] + mod0_ref[0][5:6] * y
    u = (_rms(x2) * g1_ref[...] * (1.0 + m[1:2]) + m[0:1]).astype(BF16)
    bgate = _dot(u, wi_ref[:, 0:D_MODEL])
    gated = []

    def project_gated():
        gated.append(_dot(u, wi_ref[:, D_MODEL:2 * D_MODEL]) * _dot(u, wi_ref[:, 2 * D_MODEL:]))

    _deferred_tail(xprev_ref, modp_ref, g2_ref, wr_ref, br_ref, tri_ref, u2_ref, cls_ref, rank_ref,
                   cnt_ref, run_ref, after_router=project_gated)
    z = gated[0]
    prev = carry_ref[...]
    row = lax.broadcasted_iota(jnp.int32, z.shape, 0)
    z1 = jnp.where(row == 0, prev[7:8], pltpu.roll(z, 1, axis=0))
    z2 = jnp.where(row == 0, prev[6:7], jnp.where(row == 1, prev[7:8], pltpu.roll(z, 2, axis=0)))
    carry_ref[...] = z[tm - SUBLANES:, :]
    cw = cw_ref[...]
    zc = cw[0:1] * z2 + cw[1:2] * z1 + cw[2:3] * z + cb_ref[...]
    x3 = x2 + m[2:3] * _dot((bgate * zc).astype(BF16), wo_ref[...])
    xprev_ref[...] = x3

    @pl.when(i < n_tiles)
    def _():
        x3_ref[...] = x3

    _drain_row_gather(ys_ref, ybuf_ref, ysem_ref)


def _conv_layer(dest, x, ys, mod0, mod, g1, w_in, conv_w, conv_b, w_out, g2, wr, br):
    t = x.shape[0]
    tm = TM_CONV
    tri = _strict_upper(tm)
    tiles_per_batch = SEQ // tm
    n_tiles = t // tm
    cur = _cur_tile(n_tiles)
    mod_spec = pl.BlockSpec((1, 6, D_MODEL), lambda i, *_: (cur(i) // tiles_per_batch, 0, 0))
    modp_spec = pl.BlockSpec((1, 6, D_MODEL), lambda i, *_: (_prev_tile(i) // tiles_per_batch, 0, 0))
    return pl.pallas_call(
        _conv_layer_kernel,
        out_shape=_route_out_shapes(t, tm),
        grid_spec=pltpu.PrefetchScalarGridSpec(
            num_scalar_prefetch=1,
            grid=(n_tiles + 1,),
            in_specs=[
                pl.BlockSpec((tm, D_MODEL), lambda i, *_: (cur(i), 0)),
                pl.BlockSpec(memory_space=pl.ANY),
                mod_spec,
                mod_spec,
                modp_spec,
                _const_spec((1, D_MODEL)),
                _const_spec((D_MODEL, 3 * D_MODEL)),
                _const_spec((3, D_MODEL)),
                _const_spec((1, D_MODEL)),
                _const_spec((D_MODEL, D_MODEL)),
                _const_spec((1, D_MODEL)),
                _const_spec((D_MODEL, ROUTER_COLS)),
                _const_spec((N_EXPERTS, 1)),
                _const_spec((tm, tm)),
            ],
            out_specs=_route_out_specs(tm, n_tiles),
            scratch_shapes=[
                pltpu.VMEM((CLASS_ROWS, LANES), F32),
                pltpu.VMEM((SUBLANES, D_MODEL), F32),
                pltpu.VMEM((2, tm * ROW_TILE, LANES), F32),
                pltpu.SemaphoreType.DMA((2,)),
                pltpu.VMEM((tm, D_MODEL), F32),
            ],
        ),
        compiler_params=_params("arbitrary"),
        name="conv_layer_route",
    )(dest, x, ys, mod0, mod, mod, g1, w_in, conv_w, conv_b, w_out, g2, wr, br, tri)


def _invert_kernel(dest_ref, fill_ref, src_ref, sem):
    fill = pltpu.make_async_copy(fill_ref, src_ref, sem)
    fill.start()
    fill.wait()
    unroll = 32

    def body(t8, carry):
        for k in range(unroll):
            t = t8 * unroll + k
            src_ref[dest_ref[t]] = t
        return carry

    lax.fori_loop(0, dest_ref.shape[0] // unroll, body, 0)


def _invert_permutation(dest, n_out):
    smem = pl.BlockSpec(memory_space=pltpu.SMEM)
    return pl.pallas_call(
        _invert_kernel,
        out_shape=jax.ShapeDtypeStruct((n_out,), jnp.int32),
        in_specs=[smem, pl.BlockSpec(memory_space=pl.ANY)],
        out_specs=smem,
        scratch_shapes=[pltpu.SemaphoreType.DMA(())],
        name="invert_permutation",
    )(dest, jnp.arange(n_out, dtype=jnp.int32) % dest.shape[0])


def _expert_kernel(ea_ref, eb_ref, chg_ref, nused_ref, short_ref, src_ref, u2_ref, wga_ref, wua_ref,
                   wda_ref, wgb_ref, wub_ref, wdb_ref, wra_ref, wrb_ref, y_ref, xbuf_ref, xsem_ref,
                   *wbf_refs):
    del ea_ref, eb_ref
    j = pl.program_id(0)
    tm = TM_EXPERT

    @pl.when(chg_ref[j] == 1)
    def _():
        for src, dst in zip((wga_ref, wua_ref, wda_ref, wgb_ref, wub_ref, wdb_ref), wbf_refs):
            dst[...] = src[0, 0].astype(BF16)

    load_rows, start_next = _gathered_tile(src_ref, u2_ref, xbuf_ref, xsem_ref, tm, issue_first=None,
                                           n_active=nused_ref[0])

    def leading_rows(rows):
        xb = load_rows(rows).astype(BF16)
        start_next()
        dl = jnp.sum(xb.astype(F32) * (wra_ref[0] - wrb_ref[0]), axis=-1, keepdims=True)
        w_a = _sigmoid(dl)
        w_b = _sigmoid(-dl)

        def ffn(wg_ref, wu_ref, wd_ref):
            gate = _dot(xb, wg_ref[...])
            hidden = gate * _sigmoid(gate) * _dot(xb, wu_ref[...])
            return _dot(hidden.astype(BF16), wd_ref[...])

        y = w_a * ffn(*wbf_refs[0:3]) + w_b * ffn(*wbf_refs[3:6])
        _store_token_rows(y_ref, y, rows)
        if rows < tm:
            y_ref[rows * ROW_TILE:, :] = jnp.zeros(((tm - rows) * ROW_TILE, LANES), F32)

    used = j < nused_ref[0]
    for k, rows in enumerate(EXPERT_ROW_STEPS):
        pl.when(used & (short_ref[j] == k))(functools.partial(leading_rows, rows))

    @pl.when(j >= nused_ref[0])
    def _():
        y_ref[...] = jnp.zeros_like(y_ref)

    _drain_row_gather(u2_ref, xbuf_ref, xsem_ref, n_active=nused_ref[0])


def _experts(layer, ea, eb, chg, n_used, short, src, u2, w_gate, w_up, w_down, wr_rows):
    tm = TM_EXPERT
    sel_a = lambda j, ea, eb, *_: (layer, ea[j], 0, 0)
    sel_b = lambda j, ea, eb, *_: (layer, eb[j], 0, 0)
    up_spec = lambda sel: pl.BlockSpec((1, 1, D_MODEL, D_EXPERT), sel)
    down_spec = lambda sel: pl.BlockSpec((1, 1, D_EXPERT, D_MODEL), sel)
    wr_spec = lambda sel: pl.BlockSpec((1, 1, D_MODEL), lambda *a: sel(*a)[1:])
    up_scratch = pltpu.VMEM((D_MODEL, D_EXPERT), BF16)
    down_scratch = pltpu.VMEM((D_EXPERT, D_MODEL), BF16)
    return pl.pallas_call(
        _expert_kernel,
        out_shape=jax.ShapeDtypeStruct((PADDED_ROWS * ROW_TILE, LANES), F32),
        grid_spec=pltpu.PrefetchScalarGridSpec(
            num_scalar_prefetch=5,
            grid=(N_EXPERT_TILES,),
            in_specs=[
                pl.BlockSpec(memory_space=pltpu.SMEM),
                pl.BlockSpec(memory_space=pl.ANY),
                up_spec(sel_a), up_spec(sel_a), down_spec(sel_a),
                up_spec(sel_b), up_spec(sel_b), down_spec(sel_b),
                wr_spec(sel_a), wr_spec(sel_b),
            ],
            out_specs=pl.BlockSpec((tm * ROW_TILE, LANES), lambda j, *_: (j, 0)),
            scratch_shapes=[pltpu.VMEM((EXPERT_GATHER_BUFFERS, tm * ROW_TILE, LANES), F32),
                            pltpu.SemaphoreType.DMA((EXPERT_GATHER_BUFFERS,)),
                            up_scratch, up_scratch, down_scratch, up_scratch, up_scratch, down_scratch],
        ),
        compiler_params=_params("arbitrary"),
        name="grouped_experts",
    )(ea, eb, chg, n_used, short, src, u2, w_gate, w_up, w_down, w_gate, w_up, w_down, wr_rows, wr_rows)


def _final_kernel(dest_ref, x_ref, ys_ref, mod_ref, g_ref, o_ref, ybuf_ref, ysem_ref):
    y = _gathered_tile(dest_ref, ys_ref, ybuf_ref, ysem_ref, x_ref.shape[0], issue_first=True)
    x = x_ref[...] + mod_ref[0][5:6] * y
    o_ref[...] = _rms(x) * g_ref[...]
    _drain_row_gather(ys_ref, ybuf_ref, ysem_ref)


def _final(dest, x, ys, mod, g):
    t = x.shape[0]
    tm = TM_FINAL
    tiles_per_batch = SEQ // tm
    return pl.pallas_call(
        _final_kernel,
        out_shape=jax.ShapeDtypeStruct((t, D_MODEL), F32),
        grid_spec=pltpu.PrefetchScalarGridSpec(
            num_scalar_prefetch=1,
            grid=(t // tm,),
            in_specs=[
                pl.BlockSpec((tm, D_MODEL), lambda i, *_: (i, 0)),
                pl.BlockSpec(memory_space=pl.ANY),
                pl.BlockSpec((1, 6, D_MODEL), lambda i, *_: (i // tiles_per_batch, 0, 0)),
                _const_spec((1, D_MODEL)),
            ],
            out_specs=pl.BlockSpec((tm, D_MODEL), lambda i, *_: (i, 0)),
            scratch_shapes=[pltpu.VMEM((2, tm * ROW_TILE, LANES), F32), pltpu.SemaphoreType.DMA((2,))],
        ),
        compiler_params=_params("arbitrary"),
        name="final_norm",
    )(dest, x, ys, mod, g)


_PAIR_A = np.array([EXPERTS_PER_GROUP * (c // len(PAIRS)) + PAIRS[c % len(PAIRS)][0]
                    for c in range(N_CLASSES)], np.int32)
_PAIR_B = np.array([EXPERTS_PER_GROUP * (c // len(PAIRS)) + PAIRS[c % len(PAIRS)][1]
                    for c in range(N_CLASSES)], np.int32)


def _dest_kernel(start_ref, cls_ref, rank_ref, dest_ref):
    cls = cls_ref[...]
    dest = rank_ref[...]
    for c in range(N_CLASSES):
        dest = dest + jnp.where(cls == c, start_ref[c], 0)
    dest_ref[...] = dest


def _sorted_rows(row_start, cls, rank):
    shape = (TOKENS // LANES, LANES)
    dest = pl.pallas_call(
        _dest_kernel,
        out_shape=jax.ShapeDtypeStruct(shape, jnp.int32),
        in_specs=[pl.BlockSpec(memory_space=pltpu.SMEM), pl.BlockSpec(memory_space=pltpu.VMEM),
                  pl.BlockSpec(memory_space=pltpu.VMEM)],
        out_specs=pl.BlockSpec(memory_space=pltpu.VMEM),
        name="sorted_rows",
    )(row_start, cls.reshape(shape), rank.reshape(shape))
    return dest.reshape(-1)


def _moe(layer, u2, cls, rank, cnt, w_gate, w_up, w_down, wr_rows):
    counts = cnt[:N_CLASSES, 0]
    tiles = (counts + TM_EXPERT - 1) // TM_EXPERT
    tile_end = jnp.cumsum(tiles)
    row_start = (tile_end - tiles) * TM_EXPERT
    n_used = tile_end[-1:]
    dest = _sorted_rows(row_start, cls, rank)
    tile_cls = jnp.sum(jnp.arange(N_EXPERT_TILES)[:, None] >= tile_end[None, :], axis=1)
    tile_cls = jnp.minimum(tile_cls, tile_cls[jnp.maximum(n_used[0] - 1, 0)])
    ea = jnp.asarray(_PAIR_A)[tile_cls]
    eb = jnp.asarray(_PAIR_B)[tile_cls]
    chg = jnp.concatenate([jnp.ones((1,), jnp.int32),
                           (tile_cls[1:] != tile_cls[:-1]).astype(jnp.int32)])
    tiles_before = jnp.arange(N_EXPERT_TILES) - (tile_end - tiles)[tile_cls]
    tile_rows = counts[tile_cls] - tiles_before * TM_EXPERT
    smaller_steps = jnp.asarray(EXPERT_ROW_STEPS[:-1], jnp.int32)
    short = jnp.sum(tile_rows[:, None] > smaller_steps[None, :], axis=1).astype(jnp.int32)
    src = _invert_permutation(dest, PADDED_ROWS)
    ys = _experts(layer, ea, eb, chg, n_used.astype(jnp.int32), short, src, u2, w_gate, w_up, w_down,
                  wr_rows)
    return dest, ys


def kernel(x, c, norm1_g, norm2_g, w_ada, b_ada, m_w_in, m_b_gates, m_norm_g, m_w_out,
           c_w_in, c_conv_w, c_conv_b, c_w_out, w_router, b_router,
           e_w_gate, e_w_up, e_w_down, final_g):
    xf = x.reshape(TOKENS, D_MODEL)
    mod = _ada(c, w_ada, b_ada)

    bg = jnp.zeros((1, GATE_COLS), F32).at[0, :2 * N_HEADS].set(m_b_gates[0])
    wr = jnp.zeros((D_MODEL, ROUTER_COLS), BF16).at[:, :N_EXPERTS].set(w_router.astype(BF16))
    wr_rows = w_router.astype(BF16).astype(F32).T.reshape(N_EXPERTS, 1, D_MODEL)
    br = b_router.reshape(N_EXPERTS, 1)
    row = lambda v: v.reshape(1, -1)

    x1, u2, cls, rank, cnt = _mlstm_layer(xf, mod[0], row(norm1_g[0]), m_w_in, bg, row(m_norm_g[0]),
                                          m_w_out[0].astype(BF16), row(norm2_g[0]), wr, br)
    dest, ys = _moe(0, u2, cls, rank, cnt, e_w_gate, e_w_up, e_w_down, wr_rows)

    x3, u2, cls, rank, cnt = _conv_layer(
        dest, x1, ys, mod[0], mod[1], row(norm1_g[1]), c_w_in[0].astype(BF16), c_conv_w[0],
        row(c_conv_b[0]), c_w_out[0].astype(BF16), row(norm2_g[1]), wr, br)
    dest, ys = _moe(1, u2, cls, rank, cnt, e_w_gate, e_w_up, e_w_down, wr_rows)

    out = _final(dest, x3, ys, mod[1], row(final_g))
    return out.reshape(BATCH, SEQ, D_MODEL)
```

```python
import functools

import jax
import jax.numpy as jnp
import numpy as np
from jax import lax
from jax.experimental import pallas as pl
from jax.experimental.pallas import tpu as pltpu

F32 = jnp.float32
BF16 = jnp.bfloat16

D_MODEL = 1024
BATCH = 4
SEQ = 8192
TOKENS = BATCH * SEQ
N_HEADS = 4
DH_V = 256
DH_QK = 128
QK = N_HEADS * DH_QK
N_EXPERTS = 16
N_GROUPS = 4
EXPERTS_PER_GROUP = 4
D_EXPERT = 512
EPS = 1e-6

LANES = 128
SUBLANES = 8
VMEM_LIMIT_BYTES = 56 * 1024 * 1024

CHUNK = 128
TM_CONV = 512
TM_EXPERT = 512
EXPERT_ROW_STEPS = (TM_EXPERT // 2, 3 * TM_EXPERT // 4, TM_EXPERT)
GATE_COLS = LANES
ROUTER_COLS = LANES
GATE_WIDTH = 16
PAIRS = ((0, 1), (0, 2), (0, 3), (1, 2), (1, 3), (2, 3))
N_CLASSES = N_GROUPS * len(PAIRS)
CLASS_ROWS = 32
N_EXPERT_TILES = TOKENS // TM_EXPERT + N_CLASSES
PADDED_ROWS = N_EXPERT_TILES * TM_EXPERT
TM_FINAL = 512
EXPERT_GATHER_BUFFERS = 4
PROJ_COLS = 256
PROJ_EVERY = 1


def _params(*semantics):
    return pltpu.CompilerParams(dimension_semantics=semantics, vmem_limit_bytes=VMEM_LIMIT_BYTES)


def _dot(a, b):
    return jnp.dot(a, b, preferred_element_type=F32)


def _rms(x):
    return x * lax.rsqrt(jnp.mean(x * x, axis=-1, keepdims=True) + EPS)


def _sigmoid(x):
    return 1.0 / (1.0 + jnp.exp(-x))


ROW_TILE = D_MODEL // LANES


def _load_token_rows(ref, n):
    return jnp.concatenate([ref[pl.ds(c, n, stride=ROW_TILE), :] for c in range(ROW_TILE)], axis=1)


def _store_token_rows(ref, val, n):
    for c in range(ROW_TILE):
        ref[pl.ds(c, n, stride=ROW_TILE), :] = val[:, c * LANES:(c + 1) * LANES]


def _token_tile(ref, t):
    return ref.at[pl.ds(pl.multiple_of(t * ROW_TILE, ROW_TILE), ROW_TILE)]


def _start_row_gather(idx_ref, base, n, src_hbm, buf, sem):
    for r in range(n):
        pltpu.make_async_copy(_token_tile(src_hbm, idx_ref[base + r]), _token_tile(buf, r),
                              sem).start(priority=r % 2)


def _wait_row_gather(src_hbm, buf, sem):
    pltpu.make_async_copy(src_hbm.at[pl.ds(0, buf.shape[0])], buf, sem).wait()


def _gathered_tile(idx_ref, src_hbm, buf_ref, sem_ref, n, issue_first, n_tiles=None, n_active=None):
    i = pl.program_id(0)
    last = (pl.num_programs(0) if n_tiles is None else n_tiles) - 1
    nb = buf_ref.shape[0]
    ahead = nb - 1

    @pl.when(i == 0)
    def _():
        for k in range(ahead):
            _start_row_gather(idx_ref, jnp.minimum(k, last) * n, n, src_hbm, buf_ref.at[k], sem_ref.at[k])

    nxt = (i + ahead) % nb
    start_next = functools.partial(
        _start_row_gather, idx_ref, jnp.minimum(i + ahead, last) * n, n, src_hbm, buf_ref.at[nxt],
        sem_ref.at[nxt])
    if issue_first:
        start_next()
    slot = i % nb
    if n_active is None:
        _wait_row_gather(src_hbm, buf_ref.at[slot], sem_ref.at[slot])
    else:
        @pl.when(i < n_active + ahead)
        def _():
            _wait_row_gather(src_hbm, buf_ref.at[slot], sem_ref.at[slot])

    load_rows = functools.partial(_load_token_rows, buf_ref.at[slot])
    if issue_first is None:
        return load_rows, start_next
    rows = load_rows(n)
    if not issue_first:
        start_next()
    return rows


def _drain_row_gather(src_hbm, buf_ref, sem_ref, n_active=None):
    i = pl.program_id(0)
    nb = buf_ref.shape[0]

    @pl.when(i == pl.num_programs(0) - 1)
    def _():
        for k in range(1, nb):
            wait = functools.partial(_wait_row_gather, src_hbm, buf_ref.at[(i + k) % nb],
                                     sem_ref.at[(i + k) % nb])
            if n_active is None:
                wait()
            else:
                pl.when(i + k - (nb - 1) < n_active)(wait)


def _ada_kernel(c_ref, w_ref, b_ref, o_ref):
    c = c_ref[...]
    cond = c * _sigmoid(c)
    o_ref[0] = _dot(cond.astype(BF16), w_ref[0].astype(BF16)) + b_ref[0]


def _ada(c, w_ada, b_ada):
    depth, d, n = w_ada.shape
    tn = 1536
    c8 = jnp.zeros((SUBLANES, d), F32).at[:BATCH].set(c)
    out = pl.pallas_call(
        _ada_kernel,
        out_shape=jax.ShapeDtypeStruct((depth, SUBLANES, n), F32),
        grid=(depth, n // tn),
        in_specs=[
            pl.BlockSpec((SUBLANES, d), lambda l, j: (0, 0)),
            pl.BlockSpec((1, d, tn), lambda l, j: (l, 0, j)),
            pl.BlockSpec((1, 1, tn), lambda l, j: (l, 0, j)),
        ],
        out_specs=pl.BlockSpec((1, SUBLANES, tn), lambda l, j: (l, 0, j)),
        compiler_params=_params("arbitrary", "arbitrary"),
        name="ada_mod",
    )(c8, w_ada, b_ada.reshape(depth, 1, n))
    return out[:, :BATCH].reshape(depth, BATCH, 6, d)


def _mlstm_layer_kernel(xn_ref, x_ref, mod_ref, g1_ref, w32_ref, bg_ref, ng_ref,
                        wo_ref, g2_ref, wr_ref, br_ref, tri_ref,
                        x1_ref, u2_ref, cls_ref, rank_ref, cnt_ref,
                        wi_ref, q_ref, k_ref, v_ref, og_ref, gcol_ref, grow_ref, m_ref, n_ref, h_ref, run_ref,
                        *c_refs):
    step = pl.program_id(0)

    @pl.when(step == 0)
    def _():
        for ref in (q_ref, k_ref, v_ref, og_ref, gcol_ref, grow_ref, m_ref, n_ref, h_ref, run_ref) + c_refs:
            ref[...] = jnp.zeros_like(ref)
        n_main = 2 * QK + 2 * D_MODEL
        for lo in range(0, n_main, PROJ_COLS):
            wi_ref[:, lo:lo + PROJ_COLS] = w32_ref[0, :, lo:lo + PROJ_COLS].astype(BF16)
        gates = w32_ref[0, :, n_main:n_main + 2 * N_HEADS].astype(BF16)
        wi_ref[:, n_main:] = jnp.concatenate(
            [gates, jnp.zeros((D_MODEL, GATE_COLS - 2 * N_HEADS), BF16)], axis=1)

    h_prev = h_ref[...]
    ln = CHUNK
    row = lax.broadcasted_iota(jnp.int32, (ln, ln), 0)
    col = lax.broadcasted_iota(jnp.int32, (ln, ln), 1)
    causal = col <= row
    n_streams = BATCH * N_HEADS
    m_all = [m_ref[st] for st in range(n_streams)]
    n_all = [n_ref[st] for st in range(n_streams)]
    m_out, n_out = [None] * n_streams, [None] * n_streams

    def stream(st):
        bi, h = divmod(st, N_HEADS)
        gates = gcol_ref[bi]
        column = lambda kind: jnp.broadcast_to(
            gates[:, kind * N_HEADS + h:kind * N_HEADS + h + 1], (ln, LANES))
        cm, b, imb_col = column(0), column(1), column(2)
        imb_row = grow_ref[bi, h:h + 1, :]
        qh = q_ref[bi, :, h * DH_QK:(h + 1) * DH_QK]
        kh = k_ref[bi, :, h * DH_QK:(h + 1) * DH_QK]
        vh = v_ref[bi, :, h * DH_V:(h + 1) * DH_V]
        c_ref = c_refs[st]
        c_prev = c_ref[...]
        m_prev = m_all[st]
        s_raw = lax.dot_general(qh, kh, (((1,), (1,)), ((), ())), preferred_element_type=F32)
        yield
        big_m = jnp.maximum(m_prev, cm)
        d_mat = jnp.exp(jnp.where(causal, imb_row - big_m, -jnp.inf))
        q_inter = qh.astype(F32) * jnp.exp(m_prev - big_m)
        s = s_raw * d_mat
        lhs = jnp.concatenate([q_inter.astype(BF16), s.astype(BF16)], axis=1)
        rhs = jnp.concatenate([c_prev.astype(BF16), vh], axis=0)
        num = _dot(lhs, rhs)
        den = jnp.sum(s + q_inter * n_all[st], axis=-1, keepdims=True)
        yield
        m_last = big_m[ln - 1:ln, :]
        kw = (kh.astype(F32) * jnp.exp(imb_col - m_last)).astype(BF16)
        update = lax.dot_general(kw, vh, (((0,), (0,)), ((), ())), preferred_element_type=F32)
        yield
        decay = jnp.exp(m_prev - m_last)
        c_ref[...] = jnp.concatenate([decay] * 2, axis=1) * c_prev + update
        n_out[st] = decay * n_all[st] + jnp.sum(kw.astype(F32), axis=0, keepdims=True)
        m_out[st] = b[ln - 1:ln, :] + m_last
        inv = 1.0 / jnp.maximum(jnp.abs(den), jnp.exp(-(b + big_m)))
        hh = num * jnp.concatenate([inv, inv], axis=1)
        sl = slice(h * DH_V, (h + 1) * DH_V)
        gate = _sigmoid(og_ref[bi, :, sl].astype(F32))
        h_ref[bi * ln:(bi + 1) * ln, sl] = (_rms(hh) * ng_ref[:, sl] * gate).astype(BF16)

    deferred_stores = []

    def projection():
        mod = mod_ref[...]
        u = jnp.concatenate(
            [_rms(xn_ref[bi]) * g1_ref[...] * (1.0 + mod[bi][1:2]) + mod[bi][0:1] for bi in range(BATCH)],
            axis=0).astype(BF16)
        gt = _dot(u, wi_ref[:, 2 * QK + 2 * D_MODEL:]) + bg_ref[...]
        cols, imb = _gate_columns(gt)
        imb_rows = imb.T[0:SUBLANES, :]
        for bi in range(BATCH):
            gcol_ref[bi] = cols[bi * ln:(bi + 1) * ln, 0:GATE_WIDTH]
            grow_ref[bi] = imb_rows[:, bi * ln:(bi + 1) * ln]
        yield
        lo = 0
        for ref, width, scale in ((q_ref, QK, DH_QK ** -0.5), (k_ref, QK, None), (v_ref, D_MODEL, None),
                                  (og_ref, D_MODEL, None)):
            for off in range(0, width, PROJ_COLS):
                part = _dot(u, wi_ref[:, lo + off:lo + off + PROJ_COLS])
                if scale is not None:
                    part = part * scale
                part = part.astype(BF16)

                def store(ref=ref, off=off, part=part):
                    for bi in range(BATCH):
                        ref[bi, :, off:off + PROJ_COLS] = part[bi * ln:(bi + 1) * ln]

                if ref is og_ref:
                    deferred_stores.append(store)
                else:
                    store()
                yield
            lo += width

    def previous_chunk():
        half = D_MODEL // 2
        mix0 = _dot(h_prev, wo_ref[:, 0:half])
        yield
        mix1 = _dot(h_prev, wo_ref[:, half:])
        yield
        mod = mod_ref[...]
        u2_rows = []
        for bi in range(BATCH):
            mb = mod[bi]
            rows = slice(bi * ln, (bi + 1) * ln)
            x1 = x_ref[bi] + mb[2:3] * jnp.concatenate([mix0[rows], mix1[rows]], axis=1)
            x1_ref[bi] = x1
            u2_b = _moe_input(x1, mb, g2_ref)
            _store_token_rows(u2_ref.at[bi], u2_b, ln)
            u2_rows.append(u2_b)
        valid = jnp.where(step > 1, 1.0, 0.0)
        yield from _route_stages(jnp.concatenate(u2_rows, axis=0), valid, wr_ref, br_ref, tri_ref,
                                 cls_ref, rank_ref, cnt_ref, run_ref)

    _round_robin([previous_chunk()] + [stream(st) for st in range(n_streams)],
                 background=projection(), every=PROJ_EVERY, start_after=n_streams + 1)
    for store in deferred_stores:
        store()
    for st in range(n_streams):
        m_ref[st] = m_out[st]
        n_ref[st] = n_out[st]


def _gate_columns(gt):
    lane = lax.broadcasted_iota(jnp.int32, gt.shape, 1)
    pos = lax.broadcasted_iota(jnp.int32, gt.shape, 0) % CHUNK
    b = jnp.minimum(gt, 0.0) - jnp.log(1.0 + jnp.exp(-jnp.abs(gt)))
    shift = 1
    while shift < CHUNK:
        b = b + jnp.where(pos >= shift, pltpu.roll(b, shift, axis=0), 0.0)
        shift *= 2
    imb = gt - pltpu.roll(b, LANES - N_HEADS, axis=1)
    cm = imb
    shift = 1
    while shift < CHUNK:
        cm = jnp.maximum(cm, jnp.where(pos >= shift, pltpu.roll(cm, shift, axis=0), -jnp.inf))
        shift *= 2
    cols = jnp.where(lane < N_HEADS, cm,
                     jnp.where(lane < 2 * N_HEADS, b,
                               jnp.where(lane < 3 * N_HEADS, pltpu.roll(imb, 2 * N_HEADS, axis=1), 0.0)))
    return cols, imb


def _mlstm_layer(x, mod, g1, w_in, bg, norm_g, w_out, g2, wr, br):
    nc = SEQ // CHUNK
    n_streams = BATCH * N_HEADS
    rows = BATCH * CHUNK
    newest = lambda c: jnp.minimum(c, nc - 1)
    oldest = lambda c: jnp.maximum(c - 2, 0)
    old_rows = lambda height, width: pl.BlockSpec((BATCH, height, width), lambda c: (0, oldest(c), 0))
    x3d = x.reshape(BATCH, SEQ, D_MODEL)
    x1, u2, cls, rank, cnt = pl.pallas_call(
        _mlstm_layer_kernel,
        out_shape=(
            jax.ShapeDtypeStruct((BATCH, SEQ, D_MODEL), F32),
            jax.ShapeDtypeStruct((BATCH, SEQ * ROW_TILE, LANES), F32),
            jax.ShapeDtypeStruct((nc, 1, rows), jnp.int32),
            jax.ShapeDtypeStruct((nc, 1, rows), jnp.int32),
            jax.ShapeDtypeStruct((CLASS_ROWS, LANES), jnp.int32),
        ),
        grid=(nc + 2,),
        in_specs=[
            pl.BlockSpec((BATCH, CHUNK, D_MODEL), lambda c: (0, newest(c), 0)),
            old_rows(CHUNK, D_MODEL),
            _const_spec((BATCH, 6, D_MODEL)),
            _const_spec((1, D_MODEL)),
            _const_spec(w_in.shape),
            _const_spec((1, GATE_COLS)),
            _const_spec((1, D_MODEL)),
            _const_spec((D_MODEL, D_MODEL)),
            _const_spec((1, D_MODEL)),
            _const_spec((D_MODEL, ROUTER_COLS)),
            _const_spec((N_EXPERTS, 1)),
            _const_spec((rows, rows)),
        ],
        out_specs=(
            old_rows(CHUNK, D_MODEL),
            old_rows(CHUNK * ROW_TILE, LANES),
            pl.BlockSpec((1, 1, rows), lambda c: (oldest(c), 0, 0)),
            pl.BlockSpec((1, 1, rows), lambda c: (oldest(c), 0, 0)),
            _const_spec((CLASS_ROWS, LANES)),
        ),
        scratch_shapes=[pltpu.VMEM((D_MODEL, 2 * QK + 2 * D_MODEL + GATE_COLS), BF16),
                        pltpu.VMEM((BATCH, CHUNK, QK), BF16),
                        pltpu.VMEM((BATCH, CHUNK, QK), BF16),
                        pltpu.VMEM((BATCH, CHUNK, D_MODEL), BF16),
                        pltpu.VMEM((BATCH, CHUNK, D_MODEL), BF16),
                        pltpu.VMEM((BATCH, CHUNK, GATE_WIDTH), F32),
                        pltpu.VMEM((BATCH, SUBLANES, CHUNK), F32),
                        pltpu.VMEM((n_streams, 1, LANES), F32),
                        pltpu.VMEM((n_streams, 1, DH_QK), F32),
                        pltpu.VMEM((rows, D_MODEL), BF16),
                        pltpu.VMEM((CLASS_ROWS, LANES), F32)]
        + [pltpu.VMEM((DH_QK, DH_V), F32) for _ in range(n_streams)],
        compiler_params=_params("arbitrary"),
        name="mlstm_layer",
    )(x3d, x3d, mod, g1, w_in, bg, norm_g, w_out, g2, wr, br, _strict_upper(rows))
    token_order = lambda a: a.reshape(nc, BATCH, CHUNK).transpose(1, 0, 2).reshape(TOKENS)
    return (x1.reshape(TOKENS, D_MODEL), u2.reshape(TOKENS * ROW_TILE, LANES), token_order(cls),
            token_order(rank), cnt)


def _top2_sum(v0, v1, v2, v3):
    hi1, lo1 = jnp.maximum(v0, v1), jnp.minimum(v0, v1)
    hi2, lo2 = jnp.maximum(v2, v3), jnp.minimum(v2, v3)
    return jnp.maximum(hi1, hi2) + jnp.maximum(jnp.minimum(hi1, hi2), jnp.maximum(lo1, lo2))


def _route_tail(x_new, m, valid, g2_ref, wr_ref, br_ref, tri_ref, u2_ref, cls_ref, rank_ref, cnt_ref,
                run_ref, after_router=None):
    u2 = _moe_input(x_new, m, g2_ref)
    _store_token_rows(u2_ref, u2, u2.shape[0])
    _route(u2, valid, wr_ref, br_ref, tri_ref, cls_ref, rank_ref, cnt_ref, run_ref, after_router)


def _moe_input(x_new, m, g2_ref):
    return _rms(x_new) * g2_ref[...] * (1.0 + m[4:5]) + m[3:4]


def _round_robin(chains, background=None, every=1, start_after=0):
    chains = list(chains)
    advanced = 0
    while chains:
        alive = []
        for chain in chains:
            try:
                next(chain)
            except StopIteration:
                continue
            alive.append(chain)
            advanced += 1
            if background is not None and advanced >= start_after and advanced % every == 0:
                next(background, None)
        chains = alive
    if background is not None:
        for _ in background:
            pass


def _route(u2, valid, wr_ref, br_ref, tri_ref, cls_ref, rank_ref, cnt_ref, run_ref, after_router=None):
    stages = _route_stages(u2, valid, wr_ref, br_ref, tri_ref, cls_ref, rank_ref, cnt_ref, run_ref)
    next(stages)
    if after_router is not None:
        after_router()
    for _ in stages:
        pass


def _route_stages(u2, valid, wr_ref, br_ref, tri_ref, cls_ref, rank_ref, cnt_ref, run_ref):
    logits = _dot(u2.astype(BF16), wr_ref[...])
    yield
    lt = logits.T[0:N_EXPERTS, :]
    e = jnp.exp(lt - jnp.max(lt, axis=0, keepdims=True))
    probs = e / jnp.sum(e, axis=0, keepdims=True)
    sel = probs + br_ref[...]
    sel_rows = [sel[j:j + 1, :] for j in range(N_EXPERTS)]
    best = jnp.zeros_like(sel_rows[0], dtype=jnp.int32)
    best_score = _top2_sum(*sel_rows[0:EXPERTS_PER_GROUP])
    for g in range(1, N_GROUPS):
        score = _top2_sum(*sel_rows[g * EXPERTS_PER_GROUP:(g + 1) * EXPERTS_PER_GROUP])
        better = score > best_score
        best = jnp.where(better, g, best)
        best_score = jnp.where(better, score, best_score)
    s = []
    for j in range(EXPERTS_PER_GROUP):
        sj = sel_rows[j]
        for g in range(1, N_GROUPS):
            sj = jnp.where(best == g, sel_rows[g * EXPERTS_PER_GROUP + j], sj)
        s.append(sj)
    chosen = []
    for j in range(EXPERTS_PER_GROUP):
        beaten = jnp.zeros_like(best)
        for i in range(EXPERTS_PER_GROUP):
            if i == j:
                continue
            wins = (s[i] >= s[j]) if i < j else (s[i] > s[j])
            beaten = beaten + jnp.where(wins, 1, 0)
        chosen.append(beaten < 2)
    pair = jnp.full_like(best, len(PAIRS) - 1)
    for p in range(len(PAIRS) - 2, -1, -1):
        a, b = PAIRS[p]
        pair = jnp.where(jnp.logical_and(chosen[a], chosen[b]), p, pair)
    cls = best * len(PAIRS) + pair
    cls_ref[0] = cls

    class_id = lax.broadcasted_iota(jnp.int32, (CLASS_ROWS, cls.shape[1]), 0)
    onehot = class_id == cls
    before = _dot(jnp.where(onehot, 1.0, 0.0).astype(BF16), tri_ref[...])
    run = run_ref[...]
    rank = jnp.sum(jnp.where(onehot, before + run[:, 0:1], 0.0), axis=0, keepdims=True)
    rank_ref[0] = rank.astype(jnp.int32)
    run = run + valid * jnp.sum(jnp.where(onehot, 1.0, 0.0), axis=1, keepdims=True)
    run_ref[...] = run
    cnt_ref[...] = run.astype(jnp.int32)


def _strict_upper(n):
    return (jnp.arange(n)[:, None] < jnp.arange(n)[None, :]).astype(BF16)


def _route_out_shapes(t, tm):
    return (
        jax.ShapeDtypeStruct((t, D_MODEL), F32),
        jax.ShapeDtypeStruct((t * ROW_TILE, LANES), F32),
        jax.ShapeDtypeStruct((t // tm, 1, tm), jnp.int32),
        jax.ShapeDtypeStruct((t // tm, 1, tm), jnp.int32),
        jax.ShapeDtypeStruct((CLASS_ROWS, LANES), jnp.int32),
    )


def _cur_tile(n_tiles):
    return lambda i: jnp.minimum(i, n_tiles - 1)


def _prev_tile(i):
    return jnp.maximum(i - 1, 0)


def _route_out_specs(tm, n_tiles):
    cur = _cur_tile(n_tiles)
    return (
        pl.BlockSpec((tm, D_MODEL), lambda i, *_: (cur(i), 0)),
        pl.BlockSpec((tm * ROW_TILE, LANES), lambda i, *_: (_prev_tile(i), 0)),
        pl.BlockSpec((1, 1, tm), lambda i, *_: (_prev_tile(i), 0, 0)),
        pl.BlockSpec((1, 1, tm), lambda i, *_: (_prev_tile(i), 0, 0)),
        pl.BlockSpec((CLASS_ROWS, LANES), lambda i, *_: (0, 0)),
    )


def _init_deferred_tail(xprev_ref, run_ref):
    @pl.when(pl.program_id(0) == 0)
    def _():
        xprev_ref[...] = jnp.zeros_like(xprev_ref)
        run_ref[...] = jnp.zeros_like(run_ref)


def _deferred_tail(xprev_ref, modp_ref, *tail_refs, after_router=None):
    valid = jnp.where(pl.program_id(0) > 0, 1.0, 0.0)
    _route_tail(xprev_ref[...], modp_ref[0], valid, *tail_refs, after_router=after_router)


def _const_spec(shape):
    return pl.BlockSpec(shape, lambda i, *_: (0,) * len(shape))


def _conv_layer_kernel(dest_ref, x_ref, ys_ref, mod0_ref, mod_ref, modp_ref, g1_ref, wi_ref, cw_ref,
                       cb_ref, wo_ref, g2_ref, wr_ref, br_ref, tri_ref,
                       x3_ref, u2_ref, cls_ref, rank_ref, cnt_ref,
                       run_ref, carry_ref, ybuf_ref, ysem_ref, xprev_ref):
    tm = x_ref.shape[0]
    tiles_per_batch = SEQ // tm
    n_tiles = pl.num_programs(0) - 1
    i = pl.program_id(0)

    @pl.when(i % tiles_per_batch == 0)
    def _():
        carry_ref[...] = jnp.zeros_like(carry_ref)

    _init_deferred_tail(xprev_ref, run_ref)
    m = mod_ref[0]
    y = _gathered_tile(dest_ref, ys_ref, ybuf_ref, ysem_ref, tm, issue_first=False, n_tiles=n_tiles)
    x2 = x_ref[...] + mod0_ref[0][5:6] * y
    u = (_rms(x2) * g1_ref[...] * (1.0 + m[1:2]) + m[0:1]).astype(BF16)
    bgate = _dot(u, wi_ref[:, 0:D_MODEL])
    gated = []

    def project_gated():
        gated.append(_dot(u, wi_ref[:, D_MODEL:2 * D_MODEL]) * _dot(u, wi_ref[:, 2 * D_MODEL:]))

    _deferred_tail(xprev_ref, modp_ref, g2_ref, wr_ref, br_ref, tri_ref, u2_ref, cls_ref, rank_ref,
                   cnt_ref, run_ref, after_router=project_gated)
    z = gated[0]
    prev = carry_ref[...]
    row = lax.broadcasted_iota(jnp.int32, z.shape, 0)
    z1 = jnp.where(row == 0, prev[7:8], pltpu.roll(z, 1, axis=0))
    z2 = jnp.where(row == 0, prev[6:7], jnp.where(row == 1, prev[7:8], pltpu.roll(z, 2, axis=0)))
    carry_ref[...] = z[tm - SUBLANES:, :]
    cw = cw_ref[...]
    zc = cw[0:1] * z2 + cw[1:2] * z1 + cw[2:3] * z + cb_ref[...]
    x3 = x2 + m[2:3] * _dot((bgate * zc).astype(BF16), wo_ref[...])
    xprev_ref[...] = x3

    @pl.when(i < n_tiles)
    def _():
        x3_ref[...] = x3

    _drain_row_gather(ys_ref, ybuf_ref, ysem_ref)


def _conv_layer(dest, x, ys, mod0, mod, g1, w_in, conv_w, conv_b, w_out, g2, wr, br):
    t = x.shape[0]
    tm = TM_CONV
    tri = _strict_upper(tm)
    tiles_per_batch = SEQ // tm
    n_tiles = t // tm
    cur = _cur_tile(n_tiles)
    mod_spec = pl.BlockSpec((1, 6, D_MODEL), lambda i, *_: (cur(i) // tiles_per_batch, 0, 0))
    modp_spec = pl.BlockSpec((1, 6, D_MODEL), lambda i, *_: (_prev_tile(i) // tiles_per_batch, 0, 0))
    return pl.pallas_call(
        _conv_layer_kernel,
        out_shape=_route_out_shapes(t, tm),
        grid_spec=pltpu.PrefetchScalarGridSpec(
            num_scalar_prefetch=1,
            grid=(n_tiles + 1,),
            in_specs=[
                pl.BlockSpec((tm, D_MODEL), lambda i, *_: (cur(i), 0)),
                pl.BlockSpec(memory_space=pl.ANY),
                mod_spec,
                mod_spec,
                modp_spec,
                _const_spec((1, D_MODEL)),
                _const_spec((D_MODEL, 3 * D_MODEL)),
                _const_spec((3, D_MODEL)),
                _const_spec((1, D_MODEL)),
                _const_spec((D_MODEL, D_MODEL)),
                _const_spec((1, D_MODEL)),
                _const_spec((D_MODEL, ROUTER_COLS)),
                _const_spec((N_EXPERTS, 1)),
                _const_spec((tm, tm)),
            ],
            out_specs=_route_out_specs(tm, n_tiles),
            scratch_shapes=[
                pltpu.VMEM((CLASS_ROWS, LANES), F32),
                pltpu.VMEM((SUBLANES, D_MODEL), F32),
                pltpu.VMEM((2, tm * ROW_TILE, LANES), F32),
                pltpu.SemaphoreType.DMA((2,)),
                pltpu.VMEM((tm, D_MODEL), F32),
            ],
        ),
        compiler_params=_params("arbitrary"),
        name="conv_layer_route",
    )(dest, x, ys, mod0, mod, mod, g1, w_in, conv_w, conv_b, w_out, g2, wr, br, tri)


def _invert_kernel(dest_ref, fill_ref, src_ref, sem):
    fill = pltpu.make_async_copy(fill_ref, src_ref, sem)
    fill.start()
    fill.wait()
    unroll = 32

    def body(t8, carry):
        for k in range(unroll):
            t = t8 * unroll + k
            src_ref[dest_ref[t]] = t
        return carry

    lax.fori_loop(0, dest_ref.shape[0] // unroll, body, 0)


def _invert_permutation(dest, n_out):
    smem = pl.BlockSpec(memory_space=pltpu.SMEM)
    return pl.pallas_call(
        _invert_kernel,
        out_shape=jax.ShapeDtypeStruct((n_out,), jnp.int32),
        in_specs=[smem, pl.BlockSpec(memory_space=pl.ANY)],
        out_specs=smem,
        scratch_shapes=[pltpu.SemaphoreType.DMA(())],
        name="invert_permutation",
    )(dest, jnp.arange(n_out, dtype=jnp.int32) % dest.shape[0])


def _expert_kernel(ea_ref, eb_ref, chg_ref, nused_ref, short_ref, src_ref, u2_ref, wga_ref, wua_ref,
                   wda_ref, wgb_ref, wub_ref, wdb_ref, wra_ref, wrb_ref, y_ref, xbuf_ref, xsem_ref,
                   *wbf_refs):
    del ea_ref, eb_ref
    j = pl.program_id(0)
    tm = TM_EXPERT

    @pl.when(chg_ref[j] == 1)
    def _():
        for src, dst in zip((wga_ref, wua_ref, wda_ref, wgb_ref, wub_ref, wdb_ref), wbf_refs):
            dst[...] = src[0, 0].astype(BF16)

    load_rows, start_next = _gathered_tile(src_ref, u2_ref, xbuf_ref, xsem_ref, tm, issue_first=None,
                                           n_active=nused_ref[0])

    def leading_rows(rows):
        xb = load_rows(rows).astype(BF16)
        start_next()
        dl = jnp.sum(xb.astype(F32) * (wra_ref[0] - wrb_ref[0]), axis=-1, keepdims=True)
        w_a = _sigmoid(dl)
        w_b = _sigmoid(-dl)

        def ffn(wg_ref, wu_ref, wd_ref):
            gate = _dot(xb, wg_ref[...])
            hidden = gate * _sigmoid(gate) * _dot(xb, wu_ref[...])
            return _dot(hidden.astype(BF16), wd_ref[...])

        y = w_a * ffn(*wbf_refs[0:3]) + w_b * ffn(*wbf_refs[3:6])
        _store_token_rows(y_ref, y, rows)
        if rows < tm:
            y_ref[rows * ROW_TILE:, :] = jnp.zeros(((tm - rows) * ROW_TILE, LANES), F32)

    used = j < nused_ref[0]
    for k, rows in enumerate(EXPERT_ROW_STEPS):
        pl.when(used & (short_ref[j] == k))(functools.partial(leading_rows, rows))

    @pl.when(j >= nused_ref[0])
    def _():
        y_ref[...] = jnp.zeros_like(y_ref)

    _drain_row_gather(u2_ref, xbuf_ref, xsem_ref, n_active=nused_ref[0])


def _experts(layer, ea, eb, chg, n_used, short, src, u2, w_gate, w_up, w_down, wr_rows):
    tm = TM_EXPERT
    sel_a = lambda j, ea, eb, *_: (layer, ea[j], 0, 0)
    sel_b = lambda j, ea, eb, *_: (layer, eb[j], 0, 0)
    up_spec = lambda sel: pl.BlockSpec((1, 1, D_MODEL, D_EXPERT), sel)
    down_spec = lambda sel: pl.BlockSpec((1, 1, D_EXPERT, D_MODEL), sel)
    wr_spec = lambda sel: pl.BlockSpec((1, 1, D_MODEL), lambda *a: sel(*a)[1:])
    up_scratch = pltpu.VMEM((D_MODEL, D_EXPERT), BF16)
    down_scratch = pltpu.VMEM((D_EXPERT, D_MODEL), BF16)
    return pl.pallas_call(
        _expert_kernel,
        out_shape=jax.ShapeDtypeStruct((PADDED_ROWS * ROW_TILE, LANES), F32),
        grid_spec=pltpu.PrefetchScalarGridSpec(
            num_scalar_prefetch=5,
            grid=(N_EXPERT_TILES,),
            in_specs=[
                pl.BlockSpec(memory_space=pltpu.SMEM),
                pl.BlockSpec(memory_space=pl.ANY),
                up_spec(sel_a), up_spec(sel_a), down_spec(sel_a),
                up_spec(sel_b), up_spec(sel_b), down_spec(sel_b),
                wr_spec(sel_a), wr_spec(sel_b),
            ],
            out_specs=pl.BlockSpec((tm * ROW_TILE, LANES), lambda j, *_: (j, 0)),
            scratch_shapes=[pltpu.VMEM((EXPERT_GATHER_BUFFERS, tm * ROW_TILE, LANES), F32),
                            pltpu.SemaphoreType.DMA((EXPERT_GATHER_BUFFERS,)),
                            up_scratch, up_scratch, down_scratch, up_scratch, up_scratch, down_scratch],
        ),
        compiler_params=_params("arbitrary"),
        name="grouped_experts",
    )(ea, eb, chg, n_used, short, src, u2, w_gate, w_up, w_down, w_gate, w_up, w_down, wr_rows, wr_rows)


def _final_kernel(dest_ref, x_ref, ys_ref, mod_ref, g_ref, o_ref, ybuf_ref, ysem_ref):
    y = _gathered_tile(dest_ref, ys_ref, ybuf_ref, ysem_ref, x_ref.shape[0], issue_first=True)
    x = x_ref[...] + mod_ref[0][5:6] * y
    o_ref[...] = _rms(x) * g_ref[...]
    _drain_row_gather(ys_ref, ybuf_ref, ysem_ref)


def _final(dest, x, ys, mod, g):
    t = x.shape[0]
    tm = TM_FINAL
    tiles_per_batch = SEQ // tm
    return pl.pallas_call(
        _final_kernel,
        out_shape=jax.ShapeDtypeStruct((t, D_MODEL), F32),
        grid_spec=pltpu.PrefetchScalarGridSpec(
            num_scalar_prefetch=1,
            grid=(t // tm,),
            in_specs=[
                pl.BlockSpec((tm, D_MODEL), lambda i, *_: (i, 0)),
                pl.BlockSpec(memory_space=pl.ANY),
                pl.BlockSpec((1, 6, D_MODEL), lambda i, *_: (i // tiles_per_batch, 0, 0)),
                _const_spec((1, D_MODEL)),
            ],
            out_specs=pl.BlockSpec((tm, D_MODEL), lambda i, *_: (i, 0)),
            scratch_shapes=[pltpu.VMEM((2, tm * ROW_TILE, LANES), F32), pltpu.SemaphoreType.DMA((2,))],
        ),
        compiler_params=_params("arbitrary"),
        name="final_norm",
    )(dest, x, ys, mod, g)


_PAIR_A = np.array([EXPERTS_PER_GROUP * (c // len(PAIRS)) + PAIRS[c % len(PAIRS)][0]
                    for c in range(N_CLASSES)], np.int32)
_PAIR_B = np.array([EXPERTS_PER_GROUP * (c // len(PAIRS)) + PAIRS[c % len(PAIRS)][1]
                    for c in range(N_CLASSES)], np.int32)


def _dest_kernel(start_ref, cls_ref, rank_ref, dest_ref):
    cls = cls_ref[...]
    dest = rank_ref[...]
    for c in range(N_CLASSES):
        dest = dest + jnp.where(cls == c, start_ref[c], 0)
    dest_ref[...] = dest


def _sorted_rows(row_start, cls, rank):
    shape = (TOKENS // LANES, LANES)
    dest = pl.pallas_call(
        _dest_kernel,
        out_shape=jax.ShapeDtypeStruct(shape, jnp.int32),
        in_specs=[pl.BlockSpec(memory_space=pltpu.SMEM), pl.BlockSpec(memory_space=pltpu.VMEM),
                  pl.BlockSpec(memory_space=pltpu.VMEM)],
        out_specs=pl.BlockSpec(memory_space=pltpu.VMEM),
        name="sorted_rows",
    )(row_start, cls.reshape(shape), rank.reshape(shape))
    return dest.reshape(-1)


def _moe(layer, u2, cls, rank, cnt, w_gate, w_up, w_down, wr_rows):
    counts = cnt[:N_CLASSES, 0]
    tiles = (counts + TM_EXPERT - 1) // TM_EXPERT
    tile_end = jnp.cumsum(tiles)
    row_start = (tile_end - tiles) * TM_EXPERT
    n_used = tile_end[-1:]
    dest = _sorted_rows(row_start, cls, rank)
    tile_cls = jnp.sum(jnp.arange(N_EXPERT_TILES)[:, None] >= tile_end[None, :], axis=1)
    tile_cls = jnp.minimum(tile_cls, tile_cls[jnp.maximum(n_used[0] - 1, 0)])
    ea = jnp.asarray(_PAIR_A)[tile_cls]
    eb = jnp.asarray(_PAIR_B)[tile_cls]
    chg = jnp.concatenate([jnp.ones((1,), jnp.int32),
                           (tile_cls[1:] != tile_cls[:-1]).astype(jnp.int32)])
    tiles_before = jnp.arange(N_EXPERT_TILES) - (tile_end - tiles)[tile_cls]
    tile_rows = counts[tile_cls] - tiles_before * TM_EXPERT
    smaller_steps = jnp.asarray(EXPERT_ROW_STEPS[:-1], jnp.int32)
    short = jnp.sum(tile_rows[:, None] > smaller_steps[None, :], axis=1).astype(jnp.int32)
    src = _invert_permutation(dest, PADDED_ROWS)
    ys = _experts(layer, ea, eb, chg, n_used.astype(jnp.int32), short, src, u2, w_gate, w_up, w_down,
                  wr_rows)
    return dest, ys


def kernel(x, c, norm1_g, norm2_g, w_ada, b_ada, m_w_in, m_b_gates, m_norm_g, m_w_out,
           c_w_in, c_conv_w, c_conv_b, c_w_out, w_router, b_router,
           e_w_gate, e_w_up, e_w_down, final_g):
    xf = x.reshape(TOKENS, D_MODEL)
    mod = _ada(c, w_ada, b_ada)

    bg = jnp.zeros((1, GATE_COLS), F32).at[0, :2 * N_HEADS].set(m_b_gates[0])
    wr = jnp.zeros((D_MODEL, ROUTER_COLS), BF16).at[:, :N_EXPERTS].set(w_router.astype(BF16))
    wr_rows = w_router.astype(BF16).astype(F32).T.reshape(N_EXPERTS, 1, D_MODEL)
    br = b_router.reshape(N_EXPERTS, 1)
    row = lambda v: v.reshape(1, -1)

    x1, u2, cls, rank, cnt = _mlstm_layer(xf, mod[0], row(norm1_g[0]), m_w_in, bg, row(m_norm_g[0]),
                                          m_w_out[0].astype(BF16), row(norm2_g[0]), wr, br)
    dest, ys = _moe(0, u2, cls, rank, cnt, e_w_gate, e_w_up, e_w_down, wr_rows)

    x3, u2, cls, rank, cnt = _conv_layer(
        dest, x1, ys, mod[0], mod[1], row(norm1_g[1]), c_w_in[0].astype(BF16), c_conv_w[0],
        row(c_conv_b[0]), c_w_out[0].astype(BF16), row(norm2_g[1]), wr, br)
    dest, ys = _moe(1, u2, cls, rank, cnt, e_w_gate, e_w_up, e_w_down, wr_rows)

    out = _final(dest, x3, ys, mod[1], row(final_g))
    return out.reshape(BATCH, SEQ, D_MODEL)
```

```python
import functools

import jax
import jax.numpy as jnp
import numpy as np
from jax import lax
from jax.experimental import pallas as pl
from jax.experimental.pallas import tpu as pltpu

F32 = jnp.float32
BF16 = jnp.bfloat16

D_MODEL = 1024
BATCH = 4
SEQ = 8192
TOKENS = BATCH * SEQ
N_HEADS = 4
DH_V = 256
DH_QK = 128
QK = N_HEADS * DH_QK
N_EXPERTS = 16
N_GROUPS = 4
EXPERTS_PER_GROUP = 4
D_EXPERT = 512
EPS = 1e-6

LANES = 128
SUBLANES = 8
VMEM_LIMIT_BYTES = 56 * 1024 * 1024

CHUNK = 128
TM_CONV = 512
TM_EXPERT = 512
EXPERT_ROW_STEPS = (TM_EXPERT // 2, 3 * TM_EXPERT // 4, TM_EXPERT)
GATE_COLS = LANES
ROUTER_COLS = LANES
GATE_WIDTH = 16
PAIRS = ((0, 1), (0, 2), (0, 3), (1, 2), (1, 3), (2, 3))
N_CLASSES = N_GROUPS * len(PAIRS)
CLASS_ROWS = 32
N_EXPERT_TILES = TOKENS // TM_EXPERT + N_CLASSES
PADDED_ROWS = N_EXPERT_TILES * TM_EXPERT
TM_FINAL = 512
EXPERT_GATHER_BUFFERS = 4
PROJ_COLS = 256
PROJ_EVERY = 1


def _params(*semantics):
    return pltpu.CompilerParams(dimension_semantics=semantics, vmem_limit_bytes=VMEM_LIMIT_BYTES)


def _dot(a, b):
    return jnp.dot(a, b, preferred_element_type=F32)


def _rms(x):
    return x * lax.rsqrt(jnp.mean(x * x, axis=-1, keepdims=True) + EPS)


def _sigmoid(x):
    return 1.0 / (1.0 + jnp.exp(-x))


ROW_TILE = D_MODEL // LANES


def _load_token_rows(ref, n):
    return jnp.concatenate([ref[pl.ds(c, n, stride=ROW_TILE), :] for c in range(ROW_TILE)], axis=1)


def _store_token_rows(ref, val, n):
    for c in range(ROW_TILE):
        ref[pl.ds(c, n, stride=ROW_TILE), :] = val[:, c * LANES:(c + 1) * LANES]


def _token_tile(ref, t):
    return ref.at[pl.ds(pl.multiple_of(t * ROW_TILE, ROW_TILE), ROW_TILE)]


def _start_row_gather(idx_ref, base, n, src_hbm, buf, sem):
    for r in range(n):
        pltpu.make_async_copy(_token_tile(src_hbm, idx_ref[base + r]), _token_tile(buf, r),
                              sem).start(priority=r % 2)


def _wait_row_gather(src_hbm, buf, sem):
    pltpu.make_async_copy(src_hbm.at[pl.ds(0, buf.shape[0])], buf, sem).wait()


def _gathered_tile(idx_ref, src_hbm, buf_ref, sem_ref, n, issue_first, n_tiles=None, n_active=None):
    i = pl.program_id(0)
    last = (pl.num_programs(0) if n_tiles is None else n_tiles) - 1
    nb = buf_ref.shape[0]
    ahead = nb - 1

    @pl.when(i == 0)
    def _():
        for k in range(ahead):
            _start_row_gather(idx_ref, jnp.minimum(k, last) * n, n, src_hbm, buf_ref.at[k], sem_ref.at[k])

    nxt = (i + ahead) % nb
    start_next = functools.partial(
        _start_row_gather, idx_ref, jnp.minimum(i + ahead, last) * n, n, src_hbm, buf_ref.at[nxt],
        sem_ref.at[nxt])
    if issue_first:
        start_next()
    slot = i % nb
    if n_active is None:
        _wait_row_gather(src_hbm, buf_ref.at[slot], sem_ref.at[slot])
    else:
        @pl.when(i < n_active + ahead)
        def _():
            _wait_row_gather(src_hbm, buf_ref.at[slot], sem_ref.at[slot])

    load_rows = functools.partial(_load_token_rows, buf_ref.at[slot])
    if issue_first is None:
        return load_rows, start_next
    rows = load_rows(n)
    if not issue_first:
        start_next()
    return rows


def _drain_row_gather(src_hbm, buf_ref, sem_ref, n_active=None):
    i = pl.program_id(0)
    nb = buf_ref.shape[0]

    @pl.when(i == pl.num_programs(0) - 1)
    def _():
        for k in range(1, nb):
            wait = functools.partial(_wait_row_gather, src_hbm, buf_ref.at[(i + k) % nb],
                                     sem_ref.at[(i + k) % nb])
            if n_active is None:
                wait()
            else:
                pl.when(i + k - (nb - 1) < n_active)(wait)


def _ada_kernel(c_ref, w_ref, b_ref, o_ref):
    c = c_ref[...]
    cond = c * _sigmoid(c)
    o_ref[0] = _dot(cond.astype(BF16), w_ref[0].astype(BF16)) + b_ref[0]


def _ada(c, w_ada, b_ada):
    depth, d, n = w_ada.shape
    tn = 1536
    c8 = jnp.zeros((SUBLANES, d), F32).at[:BATCH].set(c)
    out = pl.pallas_call(
        _ada_kernel,
        out_shape=jax.ShapeDtypeStruct((depth, SUBLANES, n), F32),
        grid=(depth, n // tn),
        in_specs=[
            pl.BlockSpec((SUBLANES, d), lambda l, j: (0, 0)),
            pl.BlockSpec((1, d, tn), lambda l, j: (l, 0, j)),
            pl.BlockSpec((1, 1, tn), lambda l, j: (l, 0, j)),
        ],
        out_specs=pl.BlockSpec((1, SUBLANES, tn), lambda l, j: (l, 0, j)),
        compiler_params=_params("arbitrary", "arbitrary"),
        name="ada_mod",
    )(c8, w_ada, b_ada.reshape(depth, 1, n))
    return out[:, :BATCH].reshape(depth, BATCH, 6, d)


def _mlstm_layer_kernel(xn_ref, x_ref, mod_ref, g1_ref, w32_ref, bg_ref, ng_ref,
                        wo_ref, g2_ref, wr_ref, br_ref, tri_ref,
                        x1_ref, u2_ref, cls_ref, rank_ref, cnt_ref,
                        wi_ref, q_ref, k_ref, v_ref, og_ref, gcol_ref, grow_ref, m_ref, n_ref, h_ref, run_ref,
                        *c_refs):
    step = pl.program_id(0)

    @pl.when(step == 0)
    def _():
        for ref in (q_ref, k_ref, v_ref, og_ref, gcol_ref, grow_ref, m_ref, n_ref, h_ref, run_ref) + c_refs:
            ref[...] = jnp.zeros_like(ref)
        n_main = 2 * QK + 2 * D_MODEL
        for lo in range(0, n_main, PROJ_COLS):
            wi_ref[:, lo:lo + PROJ_COLS] = w32_ref[0, lo:lo + PROJ_COLS, :].T.astype(BF16)
        gates = jnp.concatenate([w32_ref[0, n_main:n_main + 2 * N_HEADS, :],
                                 jnp.zeros((GATE_COLS - 2 * N_HEADS, D_MODEL), F32)], axis=0)
        wi_ref[:, n_main:] = gates.T.astype(BF16)

    h_prev = h_ref[...]
    ln = CHUNK
    row = lax.broadcasted_iota(jnp.int32, (ln, ln), 0)
    col = lax.broadcasted_iota(jnp.int32, (ln, ln), 1)
    causal = col <= row
    n_streams = BATCH * N_HEADS
    m_all = [m_ref[st] for st in range(n_streams)]
    n_all = [n_ref[st] for st in range(n_streams)]
    m_out, n_out = [None] * n_streams, [None] * n_streams

    def stream(st):
        bi, h = divmod(st, N_HEADS)
        gates = gcol_ref[bi]
        column = lambda kind: jnp.broadcast_to(
            gates[:, kind * N_HEADS + h:kind * N_HEADS + h + 1], (ln, LANES))
        cm, b, imb_col = column(0), column(1), column(2)
        imb_row = grow_ref[bi, h:h + 1, :]
        qh = q_ref[bi, :, h * DH_QK:(h + 1) * DH_QK]
        kh = k_ref[bi, :, h * DH_QK:(h + 1) * DH_QK]
        vh = v_ref[bi, :, h * DH_V:(h + 1) * DH_V]
        c_ref = c_refs[st]
        c_prev = c_ref[...]
        m_prev = m_all[st]
        s_raw = lax.dot_general(qh, kh, (((1,), (1,)), ((), ())), preferred_element_type=F32)
        yield
        big_m = jnp.maximum(m_prev, cm)
        d_mat = jnp.exp(jnp.where(causal, imb_row - big_m, -jnp.inf))
        q_inter = qh.astype(F32) * jnp.exp(m_prev - big_m)
        s = s_raw * d_mat
        lhs = jnp.concatenate([q_inter.astype(BF16), s.astype(BF16)], axis=1)
        rhs = jnp.concatenate([c_prev.astype(BF16), vh], axis=0)
        num = _dot(lhs, rhs)
        den = jnp.sum(s + q_inter * n_all[st], axis=-1, keepdims=True)
        yield
        m_last = big_m[ln - 1:ln, :]
        kw = (kh.astype(F32) * jnp.exp(imb_col - m_last)).astype(BF16)
        update = lax.dot_general(kw, vh, (((0,), (0,)), ((), ())), preferred_element_type=F32)
        yield
        decay = jnp.exp(m_prev - m_last)
        c_ref[...] = jnp.concatenate([decay] * 2, axis=1) * c_prev + update
        n_out[st] = decay * n_all[st] + jnp.sum(kw.astype(F32), axis=0, keepdims=True)
        m_out[st] = b[ln - 1:ln, :] + m_last
        inv = 1.0 / jnp.maximum(jnp.abs(den), jnp.exp(-(b + big_m)))
        hh = num * jnp.concatenate([inv, inv], axis=1)
        sl = slice(h * DH_V, (h + 1) * DH_V)
        gate = _sigmoid(og_ref[bi, :, sl].astype(F32))
        h_ref[bi * ln:(bi + 1) * ln, sl] = (_rms(hh) * ng_ref[:, sl] * gate).astype(BF16)

    deferred_stores = []

    def projection():
        mod = mod_ref[...]
        u = jnp.concatenate(
            [_rms(xn_ref[bi]) * g1_ref[...] * (1.0 + mod[bi][1:2]) + mod[bi][0:1] for bi in range(BATCH)],
            axis=0).astype(BF16)
        gt = _dot(u, wi_ref[:, 2 * QK + 2 * D_MODEL:]) + bg_ref[...]
        cols, imb = _gate_columns(gt)
        imb_rows = imb.T[0:SUBLANES, :]
        for bi in range(BATCH):
            gcol_ref[bi] = cols[bi * ln:(bi + 1) * ln, 0:GATE_WIDTH]
            grow_ref[bi] = imb_rows[:, bi * ln:(bi + 1) * ln]
        yield
        lo = 0
        for ref, width, scale in ((q_ref, QK, DH_QK ** -0.5), (k_ref, QK, None), (v_ref, D_MODEL, None),
                                  (og_ref, D_MODEL, None)):
            for off in range(0, width, PROJ_COLS):
                part = _dot(u, wi_ref[:, lo + off:lo + off + PROJ_COLS])
                if scale is not None:
                    part = part * scale
                part = part.astype(BF16)

                def store(ref=ref, off=off, part=part):
                    for bi in range(BATCH):
                        ref[bi, :, off:off + PROJ_COLS] = part[bi * ln:(bi + 1) * ln]

                if ref is og_ref:
                    deferred_stores.append(store)
                else:
                    store()
                yield
            lo += width

    def previous_chunk():
        half = D_MODEL // 2
        mix0 = _dot(h_prev, wo_ref[:, 0:half])
        yield
        mix1 = _dot(h_prev, wo_ref[:, half:])
        yield
        mod = mod_ref[...]
        u2_rows = []
        for bi in range(BATCH):
            mb = mod[bi]
            rows = slice(bi * ln, (bi + 1) * ln)
            x1 = x_ref[bi] + mb[2:3] * jnp.concatenate([mix0[rows], mix1[rows]], axis=1)
            x1_ref[bi] = x1
            u2_b = _moe_input(x1, mb, g2_ref)
            _store_token_rows(u2_ref.at[bi], u2_b, ln)
            u2_rows.append(u2_b)
        valid = jnp.where(step > 1, 1.0, 0.0)
        yield from _route_stages(jnp.concatenate(u2_rows, axis=0), valid, wr_ref, br_ref, tri_ref,
                                 cls_ref, rank_ref, cnt_ref, run_ref)

    _round_robin([previous_chunk()] + [stream(st) for st in range(n_streams)],
                 background=projection(), every=PROJ_EVERY, start_after=n_streams + 1)
    for store in deferred_stores:
        store()
    for st in range(n_streams):
        m_ref[st] = m_out[st]
        n_ref[st] = n_out[st]


def _gate_columns(gt):
    lane = lax.broadcasted_iota(jnp.int32, gt.shape, 1)
    pos = lax.broadcasted_iota(jnp.int32, gt.shape, 0) % CHUNK
    b = jnp.minimum(gt, 0.0) - jnp.log(1.0 + jnp.exp(-jnp.abs(gt)))
    shift = 1
    while shift < CHUNK:
        b = b + jnp.where(pos >= shift, pltpu.roll(b, shift, axis=0), 0.0)
        shift *= 2
    imb = gt - pltpu.roll(b, LANES - N_HEADS, axis=1)
    cm = imb
    shift = 1
    while shift < CHUNK:
        cm = jnp.maximum(cm, jnp.where(pos >= shift, pltpu.roll(cm, shift, axis=0), -jnp.inf))
        shift *= 2
    cols = jnp.where(lane < N_HEADS, cm,
                     jnp.where(lane < 2 * N_HEADS, b,
                               jnp.where(lane < 3 * N_HEADS, pltpu.roll(imb, 2 * N_HEADS, axis=1), 0.0)))
    return cols, imb


def _mlstm_layer(x, mod, g1, w_in, bg, norm_g, w_out, g2, wr, br):
    nc = SEQ // CHUNK
    n_streams = BATCH * N_HEADS
    rows = BATCH * CHUNK
    newest = lambda c: jnp.minimum(c, nc - 1)
    oldest = lambda c: jnp.maximum(c - 2, 0)
    old_rows = lambda height, width: pl.BlockSpec((BATCH, height, width), lambda c: (0, oldest(c), 0))
    x3d = x.reshape(BATCH, SEQ, D_MODEL)
    x1, u2, cls, rank, cnt = pl.pallas_call(
        _mlstm_layer_kernel,
        out_shape=(
            jax.ShapeDtypeStruct((BATCH, SEQ, D_MODEL), F32),
            jax.ShapeDtypeStruct((BATCH, SEQ * ROW_TILE, LANES), F32),
            jax.ShapeDtypeStruct((nc, 1, rows), jnp.int32),
            jax.ShapeDtypeStruct((nc, 1, rows), jnp.int32),
            jax.ShapeDtypeStruct((CLASS_ROWS, LANES), jnp.int32),
        ),
        grid=(nc + 2,),
        in_specs=[
            pl.BlockSpec((BATCH, CHUNK, D_MODEL), lambda c: (0, newest(c), 0)),
            old_rows(CHUNK, D_MODEL),
            _const_spec((BATCH, 6, D_MODEL)),
            _const_spec((1, D_MODEL)),
            _const_spec(w_in.shape),
            _const_spec((1, GATE_COLS)),
            _const_spec((1, D_MODEL)),
            _const_spec((D_MODEL, D_MODEL)),
            _const_spec((1, D_MODEL)),
            _const_spec((D_MODEL, ROUTER_COLS)),
            _const_spec((N_EXPERTS, 1)),
            _const_spec((rows, rows)),
        ],
        out_specs=(
            old_rows(CHUNK, D_MODEL),
            old_rows(CHUNK * ROW_TILE, LANES),
            pl.BlockSpec((1, 1, rows), lambda c: (oldest(c), 0, 0)),
            pl.BlockSpec((1, 1, rows), lambda c: (oldest(c), 0, 0)),
            _const_spec((CLASS_ROWS, LANES)),
        ),
        scratch_shapes=[pltpu.VMEM((D_MODEL, 2 * QK + 2 * D_MODEL + GATE_COLS), BF16),
                        pltpu.VMEM((BATCH, CHUNK, QK), BF16),
                        pltpu.VMEM((BATCH, CHUNK, QK), BF16),
                        pltpu.VMEM((BATCH, CHUNK, D_MODEL), BF16),
                        pltpu.VMEM((BATCH, CHUNK, D_MODEL), BF16),
                        pltpu.VMEM((BATCH, CHUNK, GATE_WIDTH), F32),
                        pltpu.VMEM((BATCH, SUBLANES, CHUNK), F32),
                        pltpu.VMEM((n_streams, 1, LANES), F32),
                        pltpu.VMEM((n_streams, 1, DH_QK), F32),
                        pltpu.VMEM((rows, D_MODEL), BF16),
                        pltpu.VMEM((CLASS_ROWS, LANES), F32)]
        + [pltpu.VMEM((DH_QK, DH_V), F32) for _ in range(n_streams)],
        compiler_params=_params("arbitrary"),
        name="mlstm_layer",
    )(x3d, x3d, mod, g1, w_in, bg, norm_g, w_out, g2, wr, br, _strict_upper(rows))
    token_order = lambda a: a.reshape(nc, BATCH, CHUNK).transpose(1, 0, 2).reshape(TOKENS)
    return (x1.reshape(TOKENS, D_MODEL), u2.reshape(TOKENS * ROW_TILE, LANES), token_order(cls),
            token_order(rank), cnt)


def _top2_sum(v0, v1, v2, v3):
    hi1, lo1 = jnp.maximum(v0, v1), jnp.minimum(v0, v1)
    hi2, lo2 = jnp.maximum(v2, v3), jnp.minimum(v2, v3)
    return jnp.maximum(hi1, hi2) + jnp.maximum(jnp.minimum(hi1, hi2), jnp.maximum(lo1, lo2))


def _route_tail(x_new, m, valid, g2_ref, wr_ref, br_ref, tri_ref, u2_ref, cls_ref, rank_ref, cnt_ref,
                run_ref, after_router=None):
    u2 = _moe_input(x_new, m, g2_ref)
    _store_token_rows(u2_ref, u2, u2.shape[0])
    _route(u2, valid, wr_ref, br_ref, tri_ref, cls_ref, rank_ref, cnt_ref, run_ref, after_router)


def _moe_input(x_new, m, g2_ref):
    return _rms(x_new) * g2_ref[...] * (1.0 + m[4:5]) + m[3:4]


def _round_robin(chains, background=None, every=1, start_after=0):
    chains = list(chains)
    advanced = 0
    while chains:
        alive = []
        for chain in chains:
            try:
                next(chain)
            except StopIteration:
                continue
            alive.append(chain)
            advanced += 1
            if background is not None and advanced >= start_after and advanced % every == 0:
                next(background, None)
        chains = alive
    if background is not None:
        for _ in background:
            pass


def _route(u2, valid, wr_ref, br_ref, tri_ref, cls_ref, rank_ref, cnt_ref, run_ref, after_router=None):
    stages = _route_stages(u2, valid, wr_ref, br_ref, tri_ref, cls_ref, rank_ref, cnt_ref, run_ref)
    next(stages)
    if after_router is not None:
        after_router()
    for _ in stages:
        pass


def _route_stages(u2, valid, wr_ref, br_ref, tri_ref, cls_ref, rank_ref, cnt_ref, run_ref):
    logits = _dot(u2.astype(BF16), wr_ref[...])
    yield
    lt = logits.T[0:N_EXPERTS, :]
    e = jnp.exp(lt - jnp.max(lt, axis=0, keepdims=True))
    probs = e / jnp.sum(e, axis=0, keepdims=True)
    sel = probs + br_ref[...]
    sel_rows = [sel[j:j + 1, :] for j in range(N_EXPERTS)]
    best = jnp.zeros_like(sel_rows[0], dtype=jnp.int32)
    best_score = _top2_sum(*sel_rows[0:EXPERTS_PER_GROUP])
    for g in range(1, N_GROUPS):
        score = _top2_sum(*sel_rows[g * EXPERTS_PER_GROUP:(g + 1) * EXPERTS_PER_GROUP])
        better = score > best_score
        best = jnp.where(better, g, best)
        best_score = jnp.where(better, score, best_score)
    s = []
    for j in range(EXPERTS_PER_GROUP):
        sj = sel_rows[j]
        for g in range(1, N_GROUPS):
            sj = jnp.where(best == g, sel_rows[g * EXPERTS_PER_GROUP + j], sj)
        s.append(sj)
    chosen = []
    for j in range(EXPERTS_PER_GROUP):
        beaten = jnp.zeros_like(best)
        for i in range(EXPERTS_PER_GROUP):
            if i == j:
                continue
            wins = (s[i] >= s[j]) if i < j else (s[i] > s[j])
            beaten = beaten + jnp.where(wins, 1, 0)
        chosen.append(beaten < 2)
    pair = jnp.full_like(best, len(PAIRS) - 1)
    for p in range(len(PAIRS) - 2, -1, -1):
        a, b = PAIRS[p]
        pair = jnp.where(jnp.logical_and(chosen[a], chosen[b]), p, pair)
    cls = best * len(PAIRS) + pair
    cls_ref[0] = cls

    class_id = lax.broadcasted_iota(jnp.int32, (CLASS_ROWS, cls.shape[1]), 0)
    onehot = class_id == cls
    before = _dot(jnp.where(onehot, 1.0, 0.0).astype(BF16), tri_ref[...])
    run = run_ref[...]
    rank = jnp.sum(jnp.where(onehot, before + run[:, 0:1], 0.0), axis=0, keepdims=True)
    rank_ref[0] = rank.astype(jnp.int32)
    run = run + valid * jnp.sum(jnp.where(onehot, 1.0, 0.0), axis=1, keepdims=True)
    run_ref[...] = run
    cnt_ref[...] = run.astype(jnp.int32)


def _strict_upper(n):
    return (jnp.arange(n)[:, None] < jnp.arange(n)[None, :]).astype(BF16)


def _route_out_shapes(t, tm):
    return (
        jax.ShapeDtypeStruct((t, D_MODEL), F32),
        jax.ShapeDtypeStruct((t * ROW_TILE, LANES), F32),
        jax.ShapeDtypeStruct((t // tm, 1, tm), jnp.int32),
        jax.ShapeDtypeStruct((t // tm, 1, tm), jnp.int32),
        jax.ShapeDtypeStruct((CLASS_ROWS, LANES), jnp.int32),
    )


def _cur_tile(n_tiles):
    return lambda i: jnp.minimum(i, n_tiles - 1)


def _prev_tile(i):
    return jnp.maximum(i - 1, 0)


def _route_out_specs(tm, n_tiles):
    cur = _cur_tile(n_tiles)
    return (
        pl.BlockSpec((tm, D_MODEL), lambda i, *_: (cur(i), 0)),
        pl.BlockSpec((tm * ROW_TILE, LANES), lambda i, *_: (_prev_tile(i), 0)),
        pl.BlockSpec((1, 1, tm), lambda i, *_: (_prev_tile(i), 0, 0)),
        pl.BlockSpec((1, 1, tm), lambda i, *_: (_prev_tile(i), 0, 0)),
        pl.BlockSpec((CLASS_ROWS, LANES), lambda i, *_: (0, 0)),
    )


def _init_deferred_tail(xprev_ref, run_ref):
    @pl.when(pl.program_id(0) == 0)
    def _():
        xprev_ref[...] = jnp.zeros_like(xprev_ref)
        run_ref[...] = jnp.zeros_like(run_ref)


def _deferred_tail(xprev_ref, modp_ref, *tail_refs, after_router=None):
    valid = jnp.where(pl.program_id(0) > 0, 1.0, 0.0)
    _route_tail(xprev_ref[...], modp_ref[0], valid, *tail_refs, after_router=after_router)


def _const_spec(shape):
    return pl.BlockSpec(shape, lambda i, *_: (0,) * len(shape))


def _conv_layer_kernel(dest_ref, x_ref, ys_ref, mod0_ref, mod_ref, modp_ref, g1_ref, wi_ref, cw_ref,
                       cb_ref, wo_ref, g2_ref, wr_ref, br_ref, tri_ref,
                       x3_ref, u2_ref, cls_ref, rank_ref, cnt_ref,
                       run_ref, carry_ref, ybuf_ref, ysem_ref, xprev_ref):
    tm = x_ref.shape[0]
    tiles_per_batch = SEQ // tm
    n_tiles = pl.num_programs(0) - 1
    i = pl.program_id(0)

    @pl.when(i % tiles_per_batch == 0)
    def _():
        carry_ref[...] = jnp.zeros_like(carry_ref)

    _init_deferred_tail(xprev_ref, run_ref)
    m = mod_ref[0]
    y = _gathered_tile(dest_ref, ys_ref, ybuf_ref, ysem_ref, tm, issue_first=False, n_tiles=n_tiles)
    x2 = x_ref[...] + mod0_ref[0][5:6] * y
    u = (_rms(x2) * g1_ref[...] * (1.0 + m[1:2]) + m[0:1]).astype(BF16)
    bgate = _dot(u, wi_ref[:, 0:D_MODEL])
    gated = []

    def project_gated():
        gated.append(_dot(u, wi_ref[:, D_MODEL:2 * D_MODEL]) * _dot(u, wi_ref[:, 2 * D_MODEL:]))

    _deferred_tail(xprev_ref, modp_ref, g2_ref, wr_ref, br_ref, tri_ref, u2_ref, cls_ref, rank_ref,
                   cnt_ref, run_ref, after_router=project_gated)
    z = gated[0]
    prev = carry_ref[...]
    row = lax.broadcasted_iota(jnp.int32, z.shape, 0)
    z1 = jnp.where(row == 0, prev[7:8], pltpu.roll(z, 1, axis=0))
    z2 = jnp.where(row == 0, prev[6:7], jnp.where(row == 1, prev[7:8], pltpu.roll(z, 2, axis=0)))
    carry_ref[...] = z[tm - SUBLANES:, :]
    cw = cw_ref[...]
    zc = cw[0:1] * z2 + cw[1:2] * z1 + cw[2:3] * z + cb_ref[...]
    x3 = x2 + m[2:3] * _dot((bgate * zc).astype(BF16), wo_ref[...])
    xprev_ref[...] = x3

    @pl.when(i < n_tiles)
    def _():
        x3_ref[...] = x3

    _drain_row_gather(ys_ref, ybuf_ref, ysem_ref)


def _conv_layer(dest, x, ys, mod0, mod, g1, w_in, conv_w, conv_b, w_out, g2, wr, br):
    t = x.shape[0]
    tm = TM_CONV
    tri = _strict_upper(tm)
    tiles_per_batch = SEQ // tm
    n_tiles = t // tm
    cur = _cur_tile(n_tiles)
    mod_spec = pl.BlockSpec((1, 6, D_MODEL), lambda i, *_: (cur(i) // tiles_per_batch, 0, 0))
    modp_spec = pl.BlockSpec((1, 6, D_MODEL), lambda i, *_: (_prev_tile(i) // tiles_per_batch, 0, 0))
    return pl.pallas_call(
        _conv_layer_kernel,
        out_shape=_route_out_shapes(t, tm),
        grid_spec=pltpu.PrefetchScalarGridSpec(
            num_scalar_prefetch=1,
            grid=(n_tiles + 1,),
            in_specs=[
                pl.BlockSpec((tm, D_MODEL), lambda i, *_: (cur(i), 0)),
                pl.BlockSpec(memory_space=pl.ANY),
                mod_spec,
                mod_spec,
                modp_spec,
                _const_spec((1, D_MODEL)),
                _const_spec((D_MODEL, 3 * D_MODEL)),
                _const_spec((3, D_MODEL)),
                _const_spec((1, D_MODEL)),
                _const_spec((D_MODEL, D_MODEL)),
                _const_spec((1, D_MODEL)),
                _const_spec((D_MODEL, ROUTER_COLS)),
                _const_spec((N_EXPERTS, 1)),
                _const_spec((tm, tm)),
            ],
            out_specs=_route_out_specs(tm, n_tiles),
            scratch_shapes=[
                pltpu.VMEM((CLASS_ROWS, LANES), F32),
                pltpu.VMEM((SUBLANES, D_MODEL), F32),
                pltpu.VMEM((2, tm * ROW_TILE, LANES), F32),
                pltpu.SemaphoreType.DMA((2,)),
                pltpu.VMEM((tm, D_MODEL), F32),
            ],
        ),
        compiler_params=_params("arbitrary"),
        name="conv_layer_route",
    )(dest, x, ys, mod0, mod, mod, g1, w_in, conv_w, conv_b, w_out, g2, wr, br, tri)


def _invert_kernel(dest_ref, fill_ref, src_ref, sem):
    fill = pltpu.make_async_copy(fill_ref, src_ref, sem)
    fill.start()
    fill.wait()
    unroll = 32

    def body(t8, carry):
        for k in range(unroll):
            t = t8 * unroll + k
            src_ref[dest_ref[t]] = t
        return carry

    lax.fori_loop(0, dest_ref.shape[0] // unroll, body, 0)


def _invert_permutation(dest, n_out):
    smem = pl.BlockSpec(memory_space=pltpu.SMEM)
    return pl.pallas_call(
        _invert_kernel,
        out_shape=jax.ShapeDtypeStruct((n_out,), jnp.int32),
        in_specs=[smem, pl.BlockSpec(memory_space=pl.ANY)],
        out_specs=smem,
        scratch_shapes=[pltpu.SemaphoreType.DMA(())],
        name="invert_permutation",
    )(dest, jnp.arange(n_out, dtype=jnp.int32) % dest.shape[0])


def _expert_kernel(ea_ref, eb_ref, chg_ref, nused_ref, short_ref, src_ref, u2_ref, wga_ref, wua_ref,
                   wda_ref, wgb_ref, wub_ref, wdb_ref, wra_ref, wrb_ref, y_ref, xbuf_ref, xsem_ref,
                   *wbf_refs):
    del ea_ref, eb_ref
    j = pl.program_id(0)
    tm = TM_EXPERT

    @pl.when(chg_ref[j] == 1)
    def _():
        for src, dst in zip((wga_ref, wua_ref, wda_ref, wgb_ref, wub_ref, wdb_ref), wbf_refs):
            dst[...] = src[0, 0].astype(BF16)

    load_rows, start_next = _gathered_tile(src_ref, u2_ref, xbuf_ref, xsem_ref, tm, issue_first=None,
                                           n_active=nused_ref[0])

    def leading_rows(rows):
        xb = load_rows(rows).astype(BF16)
        start_next()
        dl = jnp.sum(xb.astype(F32) * (wra_ref[0] - wrb_ref[0]), axis=-1, keepdims=True)
        w_a = _sigmoid(dl)
        w_b = _sigmoid(-dl)

        def ffn(wg_ref, wu_ref, wd_ref):
            gate = _dot(xb, wg_ref[...])
            hidden = gate * _sigmoid(gate) * _dot(xb, wu_ref[...])
            return _dot(hidden.astype(BF16), wd_ref[...])

        y = w_a * ffn(*wbf_refs[0:3]) + w_b * ffn(*wbf_refs[3:6])
        _store_token_rows(y_ref, y, rows)
        if rows < tm:
            y_ref[rows * ROW_TILE:, :] = jnp.zeros(((tm - rows) * ROW_TILE, LANES), F32)

    used = j < nused_ref[0]
    for k, rows in enumerate(EXPERT_ROW_STEPS):
        pl.when(used & (short_ref[j] == k))(functools.partial(leading_rows, rows))

    @pl.when(j >= nused_ref[0])
    def _():
        y_ref[...] = jnp.zeros_like(y_ref)

    _drain_row_gather(u2_ref, xbuf_ref, xsem_ref, n_active=nused_ref[0])


def _experts(layer, ea, eb, chg, n_used, short, src, u2, w_gate, w_up, w_down, wr_rows):
    tm = TM_EXPERT
    sel_a = lambda j, ea, eb, *_: (layer, ea[j], 0, 0)
    sel_b = lambda j, ea, eb, *_: (layer, eb[j], 0, 0)
    up_spec = lambda sel: pl.BlockSpec((1, 1, D_MODEL, D_EXPERT), sel)
    down_spec = lambda sel: pl.BlockSpec((1, 1, D_EXPERT, D_MODEL), sel)
    wr_spec = lambda sel: pl.BlockSpec((1, 1, D_MODEL), lambda *a: sel(*a)[1:])
    up_scratch = pltpu.VMEM((D_MODEL, D_EXPERT), BF16)
    down_scratch = pltpu.VMEM((D_EXPERT, D_MODEL), BF16)
    return pl.pallas_call(
        _expert_kernel,
        out_shape=jax.ShapeDtypeStruct((PADDED_ROWS * ROW_TILE, LANES), F32),
        grid_spec=pltpu.PrefetchScalarGridSpec(
            num_scalar_prefetch=5,
            grid=(N_EXPERT_TILES,),
            in_specs=[
                pl.BlockSpec(memory_space=pltpu.SMEM),
                pl.BlockSpec(memory_space=pl.ANY),
                up_spec(sel_a), up_spec(sel_a), down_spec(sel_a),
                up_spec(sel_b), up_spec(sel_b), down_spec(sel_b),
                wr_spec(sel_a), wr_spec(sel_b),
            ],
            out_specs=pl.BlockSpec((tm * ROW_TILE, LANES), lambda j, *_: (j, 0)),
            scratch_shapes=[pltpu.VMEM((EXPERT_GATHER_BUFFERS, tm * ROW_TILE, LANES), F32),
                            pltpu.SemaphoreType.DMA((EXPERT_GATHER_BUFFERS,)),
                            up_scratch, up_scratch, down_scratch, up_scratch, up_scratch, down_scratch],
        ),
        compiler_params=_params("arbitrary"),
        name="grouped_experts",
    )(ea, eb, chg, n_used, short, src, u2, w_gate, w_up, w_down, w_gate, w_up, w_down, wr_rows, wr_rows)


def _final_kernel(dest_ref, x_ref, ys_ref, mod_ref, g_ref, o_ref, ybuf_ref, ysem_ref):
    y = _gathered_tile(dest_ref, ys_ref, ybuf_ref, ysem_ref, x_ref.shape[0], issue_first=True)
    x = x_ref[...] + mod_ref[0][5:6] * y
    o_ref[...] = _rms(x) * g_ref[...]
    _drain_row_gather(ys_ref, ybuf_ref, ysem_ref)


def _final(dest, x, ys, mod, g):
    t = x.shape[0]
    tm = TM_FINAL
    tiles_per_batch = SEQ // tm
    return pl.pallas_call(
        _final_kernel,
        out_shape=jax.ShapeDtypeStruct((t, D_MODEL), F32),
        grid_spec=pltpu.PrefetchScalarGridSpec(
            num_scalar_prefetch=1,
            grid=(t // tm,),
            in_specs=[
                pl.BlockSpec((tm, D_MODEL), lambda i, *_: (i, 0)),
                pl.BlockSpec(memory_space=pl.ANY),
                pl.BlockSpec((1, 6, D_MODEL), lambda i, *_: (i // tiles_per_batch, 0, 0)),
                _const_spec((1, D_MODEL)),
            ],
            out_specs=pl.BlockSpec((tm, D_MODEL), lambda i, *_: (i, 0)),
            scratch_shapes=[pltpu.VMEM((2, tm * ROW_TILE, LANES), F32), pltpu.SemaphoreType.DMA((2,))],
        ),
        compiler_params=_params("arbitrary"),
        name="final_norm",
    )(dest, x, ys, mod, g)


_PAIR_A = np.array([EXPERTS_PER_GROUP * (c // len(PAIRS)) + PAIRS[c % len(PAIRS)][0]
                    for c in range(N_CLASSES)], np.int32)
_PAIR_B = np.array([EXPERTS_PER_GROUP * (c // len(PAIRS)) + PAIRS[c % len(PAIRS)][1]
                    for c in range(N_CLASSES)], np.int32)


def _dest_kernel(start_ref, cls_ref, rank_ref, dest_ref):
    cls = cls_ref[...]
    dest = rank_ref[...]
    for c in range(N_CLASSES):
        dest = dest + jnp.where(cls == c, start_ref[c], 0)
    dest_ref[...] = dest


def _sorted_rows(row_start, cls, rank):
    shape = (TOKENS // LANES, LANES)
    dest = pl.pallas_call(
        _dest_kernel,
        out_shape=jax.ShapeDtypeStruct(shape, jnp.int32),
        in_specs=[pl.BlockSpec(memory_space=pltpu.SMEM), pl.BlockSpec(memory_space=pltpu.VMEM),
                  pl.BlockSpec(memory_space=pltpu.VMEM)],
        out_specs=pl.BlockSpec(memory_space=pltpu.VMEM),
        name="sorted_rows",
    )(row_start, cls.reshape(shape), rank.reshape(shape))
    return dest.reshape(-1)


def _moe(layer, u2, cls, rank, cnt, w_gate, w_up, w_down, wr_rows):
    counts = cnt[:N_CLASSES, 0]
    tiles = (counts + TM_EXPERT - 1) // TM_EXPERT
    tile_end = jnp.cumsum(tiles)
    row_start = (tile_end - tiles) * TM_EXPERT
    n_used = tile_end[-1:]
    dest = _sorted_rows(row_start, cls, rank)
    tile_cls = jnp.sum(jnp.arange(N_EXPERT_TILES)[:, None] >= tile_end[None, :], axis=1)
    tile_cls = jnp.minimum(tile_cls, tile_cls[jnp.maximum(n_used[0] - 1, 0)])
    ea = jnp.asarray(_PAIR_A)[tile_cls]
    eb = jnp.asarray(_PAIR_B)[tile_cls]
    chg = jnp.concatenate([jnp.ones((1,), jnp.int32),
                           (tile_cls[1:] != tile_cls[:-1]).astype(jnp.int32)])
    tiles_before = jnp.arange(N_EXPERT_TILES) - (tile_end - tiles)[tile_cls]
    tile_rows = counts[tile_cls] - tiles_before * TM_EXPERT
    smaller_steps = jnp.asarray(EXPERT_ROW_STEPS[:-1], jnp.int32)
    short = jnp.sum(tile_rows[:, None] > smaller_steps[None, :], axis=1).astype(jnp.int32)
    src = _invert_permutation(dest, PADDED_ROWS)
    ys = _experts(layer, ea, eb, chg, n_used.astype(jnp.int32), short, src, u2, w_gate, w_up, w_down,
                  wr_rows)
    return dest, ys


def kernel(x, c, norm1_g, norm2_g, w_ada, b_ada, m_w_in, m_b_gates, m_norm_g, m_w_out,
           c_w_in, c_conv_w, c_conv_b, c_w_out, w_router, b_router,
           e_w_gate, e_w_up, e_w_down, final_g):
    xf = x.reshape(TOKENS, D_MODEL)
    mod = _ada(c, w_ada, b_ada)

    bg = jnp.zeros((1, GATE_COLS), F32).at[0, :2 * N_HEADS].set(m_b_gates[0])
    wr = jnp.zeros((D_MODEL, ROUTER_COLS), BF16).at[:, :N_EXPERTS].set(w_router.astype(BF16))
    wr_rows = w_router.astype(BF16).astype(F32).T.reshape(N_EXPERTS, 1, D_MODEL)
    br = b_router.reshape(N_EXPERTS, 1)
    row = lambda v: v.reshape(1, -1)

    x1, u2, cls, rank, cnt = _mlstm_layer(xf, mod[0], row(norm1_g[0]), m_w_in.swapaxes(1, 2), bg,
                                          row(m_norm_g[0]),
                                          m_w_out[0].astype(BF16), row(norm2_g[0]), wr, br)
    dest, ys = _moe(0, u2, cls, rank, cnt, e_w_gate, e_w_up, e_w_down, wr_rows)

    x3, u2, cls, rank, cnt = _conv_layer(
        dest, x1, ys, mod[0], mod[1], row(norm1_g[1]), c_w_in[0].astype(BF16), c_conv_w[0],
        row(c_conv_b[0]), c_w_out[0].astype(BF16), row(norm2_g[1]), wr, br)
    dest, ys = _moe(1, u2, cls, rank, cnt, e_w_gate, e_w_up, e_w_down, wr_rows)

    out = _final(dest, x3, ys, mod[1], row(final_g))
    return out.reshape(BATCH, SEQ, D_MODEL)
```

```python
import functools

import jax
import jax.numpy as jnp
import numpy as np
from jax import lax
from jax.experimental import pallas as pl
from jax.experimental.pallas import tpu as pltpu

F32 = jnp.float32
BF16 = jnp.bfloat16

D_MODEL = 1024
BATCH = 4
SEQ = 8192
TOKENS = BATCH * SEQ
N_HEADS = 4
DH_V = 256
DH_QK = 128
QK = N_HEADS * DH_QK
N_EXPERTS = 16
N_GROUPS = 4
EXPERTS_PER_GROUP = 4
D_EXPERT = 512
EPS = 1e-6

LANES = 128
SUBLANES = 8
VMEM_LIMIT_BYTES = 56 * 1024 * 1024

CHUNK = 128
TM_CONV = 512
TM_EXPERT = 512
EXPERT_ROW_STEPS = (TM_EXPERT // 2, 3 * TM_EXPERT // 4, TM_EXPERT)
GATE_COLS = LANES
ROUTER_COLS = LANES
GATE_WIDTH = 16
PAIRS = ((0, 1), (0, 2), (0, 3), (1, 2), (1, 3), (2, 3))
N_CLASSES = N_GROUPS * len(PAIRS)
CLASS_ROWS = 32
N_EXPERT_TILES = TOKENS // TM_EXPERT + N_CLASSES
PADDED_ROWS = N_EXPERT_TILES * TM_EXPERT
TM_FINAL = 512
EXPERT_GATHER_BUFFERS = 4
PROJ_COLS = 256
PROJ_EVERY = 1


def _params(*semantics):
    return pltpu.CompilerParams(dimension_semantics=semantics, vmem_limit_bytes=VMEM_LIMIT_BYTES)


def _dot(a, b):
    return jnp.dot(a, b, preferred_element_type=F32)


def _rms(x):
    return x * lax.rsqrt(jnp.mean(x * x, axis=-1, keepdims=True) + EPS)


def _sigmoid(x):
    return 1.0 / (1.0 + jnp.exp(-x))


ROW_TILE = D_MODEL // LANES


def _load_token_rows(ref, n):
    return jnp.concatenate([ref[pl.ds(c, n, stride=ROW_TILE), :] for c in range(ROW_TILE)], axis=1)


def _store_token_rows(ref, val, n):
    for c in range(ROW_TILE):
        ref[pl.ds(c, n, stride=ROW_TILE), :] = val[:, c * LANES:(c + 1) * LANES]


def _token_tile(ref, t):
    return ref.at[pl.ds(pl.multiple_of(t * ROW_TILE, ROW_TILE), ROW_TILE)]


def _start_row_gather(idx_ref, base, n, src_hbm, buf, sem):
    for r in range(n):
        pltpu.make_async_copy(_token_tile(src_hbm, idx_ref[base + r]), _token_tile(buf, r),
                              sem).start(priority=r % 2)


def _wait_row_gather(src_hbm, buf, sem):
    pltpu.make_async_copy(src_hbm.at[pl.ds(0, buf.shape[0])], buf, sem).wait()


def _gathered_tile(idx_ref, src_hbm, buf_ref, sem_ref, n, issue_first, n_tiles=None, n_active=None):
    i = pl.program_id(0)
    last = (pl.num_programs(0) if n_tiles is None else n_tiles) - 1
    nb = buf_ref.shape[0]
    ahead = nb - 1

    @pl.when(i == 0)
    def _():
        for k in range(ahead):
            _start_row_gather(idx_ref, jnp.minimum(k, last) * n, n, src_hbm, buf_ref.at[k], sem_ref.at[k])

    nxt = (i + ahead) % nb
    start_next = functools.partial(
        _start_row_gather, idx_ref, jnp.minimum(i + ahead, last) * n, n, src_hbm, buf_ref.at[nxt],
        sem_ref.at[nxt])
    if issue_first:
        start_next()
    slot = i % nb
    if n_active is None:
        _wait_row_gather(src_hbm, buf_ref.at[slot], sem_ref.at[slot])
    else:
        @pl.when(i < n_active + ahead)
        def _():
            _wait_row_gather(src_hbm, buf_ref.at[slot], sem_ref.at[slot])

    load_rows = functools.partial(_load_token_rows, buf_ref.at[slot])
    if issue_first is None:
        return load_rows, start_next
    rows = load_rows(n)
    if not issue_first:
        start_next()
    return rows


def _drain_row_gather(src_hbm, buf_ref, sem_ref, n_active=None):
    i = pl.program_id(0)
    nb = buf_ref.shape[0]

    @pl.when(i == pl.num_programs(0) - 1)
    def _():
        for k in range(1, nb):
            wait = functools.partial(_wait_row_gather, src_hbm, buf_ref.at[(i + k) % nb],
                                     sem_ref.at[(i + k) % nb])
            if n_active is None:
                wait()
            else:
                pl.when(i + k - (nb - 1) < n_active)(wait)


def _ada_kernel(c_ref, w_ref, b_ref, o_ref):
    c = c_ref[...]
    cond = c * _sigmoid(c)
    o_ref[0] = _dot(cond.astype(BF16), w_ref[0].astype(BF16)) + b_ref[0]


def _ada(c, w_ada, b_ada):
    depth, d, n = w_ada.shape
    tn = 1536
    c8 = jnp.zeros((SUBLANES, d), F32).at[:BATCH].set(c)
    out = pl.pallas_call(
        _ada_kernel,
        out_shape=jax.ShapeDtypeStruct((depth, SUBLANES, n), F32),
        grid=(depth, n // tn),
        in_specs=[
            pl.BlockSpec((SUBLANES, d), lambda l, j: (0, 0)),
            pl.BlockSpec((1, d, tn), lambda l, j: (l, 0, j)),
            pl.BlockSpec((1, 1, tn), lambda l, j: (l, 0, j)),
        ],
        out_specs=pl.BlockSpec((1, SUBLANES, tn), lambda l, j: (l, 0, j)),
        compiler_params=_params("arbitrary", "arbitrary"),
        name="ada_mod",
    )(c8, w_ada, b_ada.reshape(depth, 1, n))
    return out[:, :BATCH].reshape(depth, BATCH, 6, d)


def _mlstm_layer_kernel(xn_ref, x_ref, mod_ref, g1_ref, w32_ref, bg_ref, ng_ref,
                        wo_ref, g2_ref, wr_ref, br_ref, tri_ref,
                        x1_ref, u2_ref, cls_ref, rank_ref, cnt_ref,
                        wi_ref, q_ref, k_ref, v_ref, og_ref, gcol_ref, grow_ref, m_ref, n_ref, h_ref, run_ref,
                        *c_refs):
    step = pl.program_id(0)

    @pl.when(step == 0)
    def _():
        for ref in (q_ref, k_ref, v_ref, og_ref, gcol_ref, grow_ref, m_ref, n_ref, h_ref, run_ref) + c_refs:
            ref[...] = jnp.zeros_like(ref)
        n_main = 2 * QK + 2 * D_MODEL
        for lo in range(0, n_main, PROJ_COLS):
            wi_ref[:, lo:lo + PROJ_COLS] = w32_ref[0, lo:lo + PROJ_COLS, :].T.astype(BF16)
        gates = jnp.concatenate([w32_ref[0, n_main:n_main + 2 * N_HEADS, :],
                                 jnp.zeros((GATE_COLS - 2 * N_HEADS, D_MODEL), F32)], axis=0)
        wi_ref[:, n_main:] = gates.T.astype(BF16)

    h_prev = h_ref[...]
    ln = CHUNK
    row = lax.broadcasted_iota(jnp.int32, (ln, ln), 0)
    col = lax.broadcasted_iota(jnp.int32, (ln, ln), 1)
    causal = col <= row
    n_streams = BATCH * N_HEADS
    m_all = [m_ref[st] for st in range(n_streams)]
    n_all = [n_ref[st] for st in range(n_streams)]
    m_out, n_out = [None] * n_streams, [None] * n_streams

    def stream(st):
        bi, h = divmod(st, N_HEADS)
        gates = gcol_ref[bi]
        column = lambda kind: jnp.broadcast_to(
            gates[:, kind * N_HEADS + h:kind * N_HEADS + h + 1], (ln, LANES))
        cm, b, imb_col = column(0), column(1), column(2)
        imb_row = grow_ref[bi, h:h + 1, :]
        qh = q_ref[bi, :, h * DH_QK:(h + 1) * DH_QK]
        kh = k_ref[bi, :, h * DH_QK:(h + 1) * DH_QK]
        vh = v_ref[bi, :, h * DH_V:(h + 1) * DH_V]
        c_ref = c_refs[st]
        c_prev = c_ref[...]
        m_prev = m_all[st]
        s_raw = lax.dot_general(qh, kh, (((1,), (1,)), ((), ())), preferred_element_type=F32)
        yield
        big_m = jnp.maximum(m_prev, cm)
        d_mat = jnp.exp(jnp.where(causal, imb_row - big_m, -jnp.inf))
        q_inter = qh.astype(F32) * jnp.exp(m_prev - big_m)
        s = s_raw * d_mat
        lhs = jnp.concatenate([q_inter.astype(BF16), s.astype(BF16)], axis=1)
        rhs = jnp.concatenate([c_prev.astype(BF16), vh], axis=0)
        num = _dot(lhs, rhs)
        den = jnp.sum(s + q_inter * n_all[st], axis=-1, keepdims=True)
        yield
        m_last = big_m[ln - 1:ln, :]
        kw = (kh.astype(F32) * jnp.exp(imb_col - m_last)).astype(BF16)
        update = lax.dot_general(kw, vh, (((0,), (0,)), ((), ())), preferred_element_type=F32)
        yield
        decay = jnp.exp(m_prev - m_last)
        c_ref[...] = jnp.concatenate([decay] * 2, axis=1) * c_prev + update
        n_out[st] = decay * n_all[st] + jnp.sum(kw.astype(F32), axis=0, keepdims=True)
        m_out[st] = b[ln - 1:ln, :] + m_last
        inv = 1.0 / jnp.maximum(jnp.abs(den), jnp.exp(-(b + big_m)))
        hh = num * jnp.concatenate([inv, inv], axis=1)
        sl = slice(h * DH_V, (h + 1) * DH_V)
        gate = _sigmoid(og_ref[bi, :, sl].astype(F32))
        h_ref[bi * ln:(bi + 1) * ln, sl] = (_rms(hh) * ng_ref[:, sl] * gate).astype(BF16)

    deferred_stores = []

    def projection():
        mod = mod_ref[...]
        u = jnp.concatenate(
            [_rms(xn_ref[bi]) * g1_ref[...] * (1.0 + mod[bi][1:2]) + mod[bi][0:1] for bi in range(BATCH)],
            axis=0).astype(BF16)
        gt = _dot(u, wi_ref[:, 2 * QK + 2 * D_MODEL:]) + bg_ref[...]
        cols, imb = _gate_columns(gt)
        imb_rows = imb.T[0:SUBLANES, :]
        for bi in range(BATCH):
            gcol_ref[bi] = cols[bi * ln:(bi + 1) * ln, 0:GATE_WIDTH]
            grow_ref[bi] = imb_rows[:, bi * ln:(bi + 1) * ln]
        yield
        lo = 0
        for ref, width, scale in ((q_ref, QK, DH_QK ** -0.5), (k_ref, QK, None), (v_ref, D_MODEL, None),
                                  (og_ref, D_MODEL, None)):
            for off in range(0, width, PROJ_COLS):
                part = _dot(u, wi_ref[:, lo + off:lo + off + PROJ_COLS])
                if scale is not None:
                    part = part * scale
                part = part.astype(BF16)

                def store(ref=ref, off=off, part=part):
                    for bi in range(BATCH):
                        ref[bi, :, off:off + PROJ_COLS] = part[bi * ln:(bi + 1) * ln]

                if ref is og_ref:
                    deferred_stores.append(store)
                else:
                    store()
                yield
            lo += width

    def previous_chunk():
        half = D_MODEL // 2
        mix0 = _dot(h_prev, wo_ref[:, 0:half])
        yield
        mix1 = _dot(h_prev, wo_ref[:, half:])
        yield
        mod = mod_ref[...]
        u2_rows = []
        for bi in range(BATCH):
            mb = mod[bi]
            rows = slice(bi * ln, (bi + 1) * ln)
            x1 = x_ref[bi] + mb[2:3] * jnp.concatenate([mix0[rows], mix1[rows]], axis=1)
            x1_ref[bi] = x1
            u2_b = _moe_input(x1, mb, g2_ref)
            _store_token_rows(u2_ref.at[bi], u2_b, ln)
            u2_rows.append(u2_b)
        valid = jnp.where(step > 1, 1.0, 0.0)
        yield from _route_stages(jnp.concatenate(u2_rows, axis=0), valid, wr_ref, br_ref, tri_ref,
                                 cls_ref, rank_ref, cnt_ref, run_ref)

    _round_robin([previous_chunk()] + [stream(st) for st in range(n_streams)],
                 background=projection(), every=PROJ_EVERY, start_after=n_streams + 1)
    for store in deferred_stores:
        store()
    for st in range(n_streams):
        m_ref[st] = m_out[st]
        n_ref[st] = n_out[st]


def _gate_columns(gt):
    lane = lax.broadcasted_iota(jnp.int32, gt.shape, 1)
    pos = lax.broadcasted_iota(jnp.int32, gt.shape, 0) % CHUNK
    b = jnp.minimum(gt, 0.0) - jnp.log(1.0 + jnp.exp(-jnp.abs(gt)))
    shift = 1
    while shift < CHUNK:
        b = b + jnp.where(pos >= shift, pltpu.roll(b, shift, axis=0), 0.0)
        shift *= 2
    imb = gt - pltpu.roll(b, LANES - N_HEADS, axis=1)
    cm = imb
    shift = 1
    while shift < CHUNK:
        cm = jnp.maximum(cm, jnp.where(pos >= shift, pltpu.roll(cm, shift, axis=0), -jnp.inf))
        shift *= 2
    cols = jnp.where(lane < N_HEADS, cm,
                     jnp.where(lane < 2 * N_HEADS, b,
                               jnp.where(lane < 3 * N_HEADS, pltpu.roll(imb, 2 * N_HEADS, axis=1), 0.0)))
    return cols, imb


def _mlstm_layer(x, mod, g1, w_in, bg, norm_g, w_out, g2, wr, br):
    nc = SEQ // CHUNK
    n_streams = BATCH * N_HEADS
    rows = BATCH * CHUNK
    newest = lambda c: jnp.minimum(c, nc - 1)
    oldest = lambda c: jnp.maximum(c - 2, 0)
    old_rows = lambda height, width: pl.BlockSpec((BATCH, height, width), lambda c: (0, oldest(c), 0))
    x3d = x.reshape(BATCH, SEQ, D_MODEL)
    x1, u2, cls, rank, cnt = pl.pallas_call(
        _mlstm_layer_kernel,
        out_shape=(
            jax.ShapeDtypeStruct((BATCH, SEQ, D_MODEL), F32),
            jax.ShapeDtypeStruct((BATCH, SEQ * ROW_TILE, LANES), F32),
            jax.ShapeDtypeStruct((nc, 1, rows), jnp.int32),
            jax.ShapeDtypeStruct((nc, 1, rows), jnp.int32),
            jax.ShapeDtypeStruct((CLASS_ROWS, LANES), jnp.int32),
        ),
        grid=(nc + 2,),
        in_specs=[
            pl.BlockSpec((BATCH, CHUNK, D_MODEL), lambda c: (0, newest(c), 0)),
            old_rows(CHUNK, D_MODEL),
            _const_spec((BATCH, 6, D_MODEL)),
            _const_spec((1, D_MODEL)),
            _const_spec(w_in.shape),
            _const_spec((1, GATE_COLS)),
            _const_spec((1, D_MODEL)),
            _const_spec((D_MODEL, D_MODEL)),
            _const_spec((1, D_MODEL)),
            _const_spec((D_MODEL, ROUTER_COLS)),
            _const_spec((N_EXPERTS, 1)),
            _const_spec((rows, rows)),
        ],
        out_specs=(
            old_rows(CHUNK, D_MODEL),
            old_rows(CHUNK * ROW_TILE, LANES),
            pl.BlockSpec((1, 1, rows), lambda c: (oldest(c), 0, 0)),
            pl.BlockSpec((1, 1, rows), lambda c: (oldest(c), 0, 0)),
            _const_spec((CLASS_ROWS, LANES)),
        ),
        scratch_shapes=[pltpu.VMEM((D_MODEL, 2 * QK + 2 * D_MODEL + GATE_COLS), BF16),
                        pltpu.VMEM((BATCH, CHUNK, QK), BF16),
                        pltpu.VMEM((BATCH, CHUNK, QK), BF16),
                        pltpu.VMEM((BATCH, CHUNK, D_MODEL), BF16),
                        pltpu.VMEM((BATCH, CHUNK, D_MODEL), BF16),
                        pltpu.VMEM((BATCH, CHUNK, GATE_WIDTH), F32),
                        pltpu.VMEM((BATCH, SUBLANES, CHUNK), F32),
                        pltpu.VMEM((n_streams, 1, LANES), F32),
                        pltpu.VMEM((n_streams, 1, DH_QK), F32),
                        pltpu.VMEM((rows, D_MODEL), BF16),
                        pltpu.VMEM((CLASS_ROWS, LANES), F32)]
        + [pltpu.VMEM((DH_QK, DH_V), F32) for _ in range(n_streams)],
        compiler_params=_params("arbitrary"),
        name="mlstm_layer",
    )(x3d, x3d, mod, g1, w_in, bg, norm_g, w_out, g2, wr, br, _strict_upper(rows))
    token_order = lambda a: a.reshape(nc, BATCH, CHUNK).transpose(1, 0, 2).reshape(TOKENS)
    return (x1.reshape(TOKENS, D_MODEL), u2.reshape(TOKENS * ROW_TILE, LANES), token_order(cls),
            token_order(rank), cnt)


def _top2_sum(v0, v1, v2, v3):
    hi1, lo1 = jnp.maximum(v0, v1), jnp.minimum(v0, v1)
    hi2, lo2 = jnp.maximum(v2, v3), jnp.minimum(v2, v3)
    return jnp.maximum(hi1, hi2) + jnp.maximum(jnp.minimum(hi1, hi2), jnp.maximum(lo1, lo2))


def _route_tail(x_new, m, valid, g2_ref, wr_ref, br_ref, tri_ref, u2_ref, cls_ref, rank_ref, cnt_ref,
                run_ref, after_router=None):
    u2 = _moe_input(x_new, m, g2_ref)
    _store_token_rows(u2_ref, u2, u2.shape[0])
    _route(u2, valid, wr_ref, br_ref, tri_ref, cls_ref, rank_ref, cnt_ref, run_ref, after_router)


def _moe_input(x_new, m, g2_ref):
    return _rms(x_new) * g2_ref[...] * (1.0 + m[4:5]) + m[3:4]


def _round_robin(chains, background=None, every=1, start_after=0):
    chains = list(chains)
    advanced = 0
    while chains:
        alive = []
        for chain in chains:
            try:
                next(chain)
            except StopIteration:
                continue
            alive.append(chain)
            advanced += 1
            if background is not None and advanced >= start_after and advanced % every == 0:
                next(background, None)
        chains = alive
    if background is not None:
        for _ in background:
            pass


def _route(u2, valid, wr_ref, br_ref, tri_ref, cls_ref, rank_ref, cnt_ref, run_ref, after_router=None):
    stages = _route_stages(u2, valid, wr_ref, br_ref, tri_ref, cls_ref, rank_ref, cnt_ref, run_ref)
    next(stages)
    if after_router is not None:
        after_router()
    for _ in stages:
        pass


def _route_stages(u2, valid, wr_ref, br_ref, tri_ref, cls_ref, rank_ref, cnt_ref, run_ref):
    logits = _dot(u2.astype(BF16), wr_ref[...])
    yield
    lt = logits.T[0:N_EXPERTS, :]
    e = jnp.exp(lt - jnp.max(lt, axis=0, keepdims=True))
    probs = e / jnp.sum(e, axis=0, keepdims=True)
    sel = probs + br_ref[...]
    sel_rows = [sel[j:j + 1, :] for j in range(N_EXPERTS)]
    best = jnp.zeros_like(sel_rows[0], dtype=jnp.int32)
    best_score = _top2_sum(*sel_rows[0:EXPERTS_PER_GROUP])
    for g in range(1, N_GROUPS):
        score = _top2_sum(*sel_rows[g * EXPERTS_PER_GROUP:(g + 1) * EXPERTS_PER_GROUP])
        better = score > best_score
        best = jnp.where(better, g, best)
        best_score = jnp.where(better, score, best_score)
    s = []
    for j in range(EXPERTS_PER_GROUP):
        sj = sel_rows[j]
        for g in range(1, N_GROUPS):
            sj = jnp.where(best == g, sel_rows[g * EXPERTS_PER_GROUP + j], sj)
        s.append(sj)
    chosen = []
    for j in range(EXPERTS_PER_GROUP):
        beaten = jnp.zeros_like(best)
        for i in range(EXPERTS_PER_GROUP):
            if i == j:
                continue
            wins = (s[i] >= s[j]) if i < j else (s[i] > s[j])
            beaten = beaten + jnp.where(wins, 1, 0)
        chosen.append(beaten < 2)
    pair = jnp.full_like(best, len(PAIRS) - 1)
    for p in range(len(PAIRS) - 2, -1, -1):
        a, b = PAIRS[p]
        pair = jnp.where(jnp.logical_and(chosen[a], chosen[b]), p, pair)
    cls = best * len(PAIRS) + pair
    cls_ref[0] = cls

    class_id = lax.broadcasted_iota(jnp.int32, (CLASS_ROWS, cls.shape[1]), 0)
    onehot = class_id == cls
    before = _dot(jnp.where(onehot, 1.0, 0.0).astype(BF16), tri_ref[...])
    run = run_ref[...]
    rank = jnp.sum(jnp.where(onehot, before + run[:, 0:1], 0.0), axis=0, keepdims=True)
    rank_ref[0] = rank.astype(jnp.int32)
    run = run + valid * jnp.sum(jnp.where(onehot, 1.0, 0.0), axis=1, keepdims=True)
    run_ref[...] = run
    cnt_ref[...] = run.astype(jnp.int32)


def _strict_upper(n):
    return (jnp.arange(n)[:, None] < jnp.arange(n)[None, :]).astype(BF16)


def _route_out_shapes(t, tm):
    return (
        jax.ShapeDtypeStruct((t, D_MODEL), F32),
        jax.ShapeDtypeStruct((t * ROW_TILE, LANES), F32),
        jax.ShapeDtypeStruct((t // tm, 1, tm), jnp.int32),
        jax.ShapeDtypeStruct((t // tm, 1, tm), jnp.int32),
        jax.ShapeDtypeStruct((CLASS_ROWS, LANES), jnp.int32),
    )


def _cur_tile(n_tiles):
    return lambda i: jnp.minimum(i, n_tiles - 1)


def _prev_tile(i):
    return jnp.maximum(i - 1, 0)


def _route_out_specs(tm, n_tiles):
    cur = _cur_tile(n_tiles)
    return (
        pl.BlockSpec((tm, D_MODEL), lambda i, *_: (cur(i), 0)),
        pl.BlockSpec((tm * ROW_TILE, LANES), lambda i, *_: (_prev_tile(i), 0)),
        pl.BlockSpec((1, 1, tm), lambda i, *_: (_prev_tile(i), 0, 0)),
        pl.BlockSpec((1, 1, tm), lambda i, *_: (_prev_tile(i), 0, 0)),
        pl.BlockSpec((CLASS_ROWS, LANES), lambda i, *_: (0, 0)),
    )


def _init_deferred_tail(xprev_ref, run_ref):
    @pl.when(pl.program_id(0) == 0)
    def _():
        xprev_ref[...] = jnp.zeros_like(xprev_ref)
        run_ref[...] = jnp.zeros_like(run_ref)


def _deferred_tail(xprev_ref, modp_ref, *tail_refs, after_router=None):
    valid = jnp.where(pl.program_id(0) > 0, 1.0, 0.0)
    _route_tail(xprev_ref[...], modp_ref[0], valid, *tail_refs, after_router=after_router)


def _const_spec(shape):
    return pl.BlockSpec(shape, lambda i, *_: (0,) * len(shape))


def _conv_layer_kernel(dest_ref, x_ref, ys_ref, mod0_ref, mod_ref, modp_ref, g1_ref, wi32_ref, cw_ref,
                       cb_ref, wo32_ref, g2_ref, wr_ref, br_ref, tri_ref,
                       x3_ref, u2_ref, cls_ref, rank_ref, cnt_ref,
                       run_ref, carry_ref, ybuf_ref, ysem_ref, xprev_ref, wi_ref, wo_ref):
    tm = x_ref.shape[0]
    tiles_per_batch = SEQ // tm
    n_tiles = pl.num_programs(0) - 1
    i = pl.program_id(0)

    @pl.when(i % tiles_per_batch == 0)
    def _():
        carry_ref[...] = jnp.zeros_like(carry_ref)

    @pl.when(i == 0)
    def _():
        for lo in range(0, 3 * D_MODEL, D_MODEL):
            wi_ref[:, lo:lo + D_MODEL] = wi32_ref[0, :, lo:lo + D_MODEL].astype(BF16)
        wo_ref[...] = wo32_ref[0].astype(BF16)

    _init_deferred_tail(xprev_ref, run_ref)
    m = mod_ref[0]
    y = _gathered_tile(dest_ref, ys_ref, ybuf_ref, ysem_ref, tm, issue_first=False, n_tiles=n_tiles)
    x2 = x_ref[...] + mod0_ref[0][5:6] * y
    u = (_rms(x2) * g1_ref[...] * (1.0 + m[1:2]) + m[0:1]).astype(BF16)
    bgate = _dot(u, wi_ref[:, 0:D_MODEL])
    gated = []

    def project_gated():
        gated.append(_dot(u, wi_ref[:, D_MODEL:2 * D_MODEL]) * _dot(u, wi_ref[:, 2 * D_MODEL:]))

    _deferred_tail(xprev_ref, modp_ref, g2_ref, wr_ref, br_ref, tri_ref, u2_ref, cls_ref, rank_ref,
                   cnt_ref, run_ref, after_router=project_gated)
    z = gated[0]
    prev = carry_ref[...]
    row = lax.broadcasted_iota(jnp.int32, z.shape, 0)
    z1 = jnp.where(row == 0, prev[7:8], pltpu.roll(z, 1, axis=0))
    z2 = jnp.where(row == 0, prev[6:7], jnp.where(row == 1, prev[7:8], pltpu.roll(z, 2, axis=0)))
    carry_ref[...] = z[tm - SUBLANES:, :]
    cw = cw_ref[...]
    zc = cw[0:1] * z2 + cw[1:2] * z1 + cw[2:3] * z + cb_ref[...]
    x3 = x2 + m[2:3] * _dot((bgate * zc).astype(BF16), wo_ref[...])
    xprev_ref[...] = x3

    @pl.when(i < n_tiles)
    def _():
        x3_ref[...] = x3

    _drain_row_gather(ys_ref, ybuf_ref, ysem_ref)


def _conv_layer(dest, x, ys, mod0, mod, g1, w_in, conv_w, conv_b, w_out, g2, wr, br):
    t = x.shape[0]
    tm = TM_CONV
    tri = _strict_upper(tm)
    tiles_per_batch = SEQ // tm
    n_tiles = t // tm
    cur = _cur_tile(n_tiles)
    mod_spec = pl.BlockSpec((1, 6, D_MODEL), lambda i, *_: (cur(i) // tiles_per_batch, 0, 0))
    modp_spec = pl.BlockSpec((1, 6, D_MODEL), lambda i, *_: (_prev_tile(i) // tiles_per_batch, 0, 0))
    return pl.pallas_call(
        _conv_layer_kernel,
        out_shape=_route_out_shapes(t, tm),
        grid_spec=pltpu.PrefetchScalarGridSpec(
            num_scalar_prefetch=1,
            grid=(n_tiles + 1,),
            in_specs=[
                pl.BlockSpec((tm, D_MODEL), lambda i, *_: (cur(i), 0)),
                pl.BlockSpec(memory_space=pl.ANY),
                mod_spec,
                mod_spec,
                modp_spec,
                _const_spec((1, D_MODEL)),
                _const_spec((1, D_MODEL, 3 * D_MODEL)),
                _const_spec((3, D_MODEL)),
                _const_spec((1, D_MODEL)),
                _const_spec((1, D_MODEL, D_MODEL)),
                _const_spec((1, D_MODEL)),
                _const_spec((D_MODEL, ROUTER_COLS)),
                _const_spec((N_EXPERTS, 1)),
                _const_spec((tm, tm)),
            ],
            out_specs=_route_out_specs(tm, n_tiles),
            scratch_shapes=[
                pltpu.VMEM((CLASS_ROWS, LANES), F32),
                pltpu.VMEM((SUBLANES, D_MODEL), F32),
                pltpu.VMEM((2, tm * ROW_TILE, LANES), F32),
                pltpu.SemaphoreType.DMA((2,)),
                pltpu.VMEM((tm, D_MODEL), F32),
                pltpu.VMEM((D_MODEL, 3 * D_MODEL), BF16),
                pltpu.VMEM((D_MODEL, D_MODEL), BF16),
            ],
        ),
        compiler_params=_params("arbitrary"),
        name="conv_layer_route",
    )(dest, x, ys, mod0, mod, mod, g1, w_in, conv_w, conv_b, w_out, g2, wr, br, tri)


def _invert_kernel(dest_ref, fill_ref, src_ref, sem):
    fill = pltpu.make_async_copy(fill_ref, src_ref, sem)
    fill.start()
    fill.wait()
    unroll = 32

    def body(t8, carry):
        for k in range(unroll):
            t = t8 * unroll + k
            src_ref[dest_ref[t]] = t
        return carry

    lax.fori_loop(0, dest_ref.shape[0] // unroll, body, 0)


def _invert_permutation(dest, n_out):
    smem = pl.BlockSpec(memory_space=pltpu.SMEM)
    return pl.pallas_call(
        _invert_kernel,
        out_shape=jax.ShapeDtypeStruct((n_out,), jnp.int32),
        in_specs=[smem, pl.BlockSpec(memory_space=pl.ANY)],
        out_specs=smem,
        scratch_shapes=[pltpu.SemaphoreType.DMA(())],
        name="invert_permutation",
    )(dest, jnp.arange(n_out, dtype=jnp.int32) % dest.shape[0])


def _expert_kernel(ea_ref, eb_ref, chg_ref, nused_ref, short_ref, src_ref, u2_ref, wga_ref, wua_ref,
                   wda_ref, wgb_ref, wub_ref, wdb_ref, wra_ref, wrb_ref, y_ref, xbuf_ref, xsem_ref,
                   *wbf_refs):
    del ea_ref, eb_ref
    j = pl.program_id(0)
    tm = TM_EXPERT

    @pl.when(chg_ref[j] == 1)
    def _():
        for src, dst in zip((wga_ref, wua_ref, wda_ref, wgb_ref, wub_ref, wdb_ref), wbf_refs):
            dst[...] = src[0, 0].astype(BF16)

    load_rows, start_next = _gathered_tile(src_ref, u2_ref, xbuf_ref, xsem_ref, tm, issue_first=None,
                                           n_active=nused_ref[0])

    def leading_rows(rows):
        xb = load_rows(rows).astype(BF16)
        start_next()
        dl = jnp.sum(xb.astype(F32) * (wra_ref[0] - wrb_ref[0]), axis=-1, keepdims=True)
        w_a = _sigmoid(dl)
        w_b = _sigmoid(-dl)

        def ffn(wg_ref, wu_ref, wd_ref):
            gate = _dot(xb, wg_ref[...])
            hidden = gate * _sigmoid(gate) * _dot(xb, wu_ref[...])
            return _dot(hidden.astype(BF16), wd_ref[...])

        y = w_a * ffn(*wbf_refs[0:3]) + w_b * ffn(*wbf_refs[3:6])
        _store_token_rows(y_ref, y, rows)
        if rows < tm:
            y_ref[rows * ROW_TILE:, :] = jnp.zeros(((tm - rows) * ROW_TILE, LANES), F32)

    used = j < nused_ref[0]
    for k, rows in enumerate(EXPERT_ROW_STEPS):
        pl.when(used & (short_ref[j] == k))(functools.partial(leading_rows, rows))

    @pl.when(j >= nused_ref[0])
    def _():
        y_ref[...] = jnp.zeros_like(y_ref)

    _drain_row_gather(u2_ref, xbuf_ref, xsem_ref, n_active=nused_ref[0])


def _experts(layer, ea, eb, chg, n_used, short, src, u2, w_gate, w_up, w_down, wr_rows):
    tm = TM_EXPERT
    sel_a = lambda j, ea, eb, *_: (layer, ea[j], 0, 0)
    sel_b = lambda j, ea, eb, *_: (layer, eb[j], 0, 0)
    up_spec = lambda sel: pl.BlockSpec((1, 1, D_MODEL, D_EXPERT), sel)
    down_spec = lambda sel: pl.BlockSpec((1, 1, D_EXPERT, D_MODEL), sel)
    wr_spec = lambda sel: pl.BlockSpec((1, 1, D_MODEL), lambda *a: sel(*a)[1:])
    up_scratch = pltpu.VMEM((D_MODEL, D_EXPERT), BF16)
    down_scratch = pltpu.VMEM((D_EXPERT, D_MODEL), BF16)
    return pl.pallas_call(
        _expert_kernel,
        out_shape=jax.ShapeDtypeStruct((PADDED_ROWS * ROW_TILE, LANES), F32),
        grid_spec=pltpu.PrefetchScalarGridSpec(
            num_scalar_prefetch=5,
            grid=(N_EXPERT_TILES,),
            in_specs=[
                pl.BlockSpec(memory_space=pltpu.SMEM),
                pl.BlockSpec(memory_space=pl.ANY),
                up_spec(sel_a), up_spec(sel_a), down_spec(sel_a),
                up_spec(sel_b), up_spec(sel_b), down_spec(sel_b),
                wr_spec(sel_a), wr_spec(sel_b),
            ],
            out_specs=pl.BlockSpec((tm * ROW_TILE, LANES), lambda j, *_: (j, 0)),
            scratch_shapes=[pltpu.VMEM((EXPERT_GATHER_BUFFERS, tm * ROW_TILE, LANES), F32),
                            pltpu.SemaphoreType.DMA((EXPERT_GATHER_BUFFERS,)),
                            up_scratch, up_scratch, down_scratch, up_scratch, up_scratch, down_scratch],
        ),
        compiler_params=_params("arbitrary"),
        name="grouped_experts",
    )(ea, eb, chg, n_used, short, src, u2, w_gate, w_up, w_down, w_gate, w_up, w_down, wr_rows, wr_rows)


def _final_kernel(dest_ref, x_ref, ys_ref, mod_ref, g_ref, o_ref, ybuf_ref, ysem_ref):
    y = _gathered_tile(dest_ref, ys_ref, ybuf_ref, ysem_ref, x_ref.shape[0], issue_first=True)
    x = x_ref[...] + mod_ref[0][5:6] * y
    o_ref[...] = _rms(x) * g_ref[...]
    _drain_row_gather(ys_ref, ybuf_ref, ysem_ref)


def _final(dest, x, ys, mod, g):
    t = x.shape[0]
    tm = TM_FINAL
    tiles_per_batch = SEQ // tm
    return pl.pallas_call(
        _final_kernel,
        out_shape=jax.ShapeDtypeStruct((t, D_MODEL), F32),
        grid_spec=pltpu.PrefetchScalarGridSpec(
            num_scalar_prefetch=1,
            grid=(t // tm,),
            in_specs=[
                pl.BlockSpec((tm, D_MODEL), lambda i, *_: (i, 0)),
                pl.BlockSpec(memory_space=pl.ANY),
                pl.BlockSpec((1, 6, D_MODEL), lambda i, *_: (i // tiles_per_batch, 0, 0)),
                _const_spec((1, D_MODEL)),
            ],
            out_specs=pl.BlockSpec((tm, D_MODEL), lambda i, *_: (i, 0)),
            scratch_shapes=[pltpu.VMEM((2, tm * ROW_TILE, LANES), F32), pltpu.SemaphoreType.DMA((2,))],
        ),
        compiler_params=_params("arbitrary"),
        name="final_norm",
    )(dest, x, ys, mod, g)


_PAIR_A = np.array([EXPERTS_PER_GROUP * (c // len(PAIRS)) + PAIRS[c % len(PAIRS)][0]
                    for c in range(N_CLASSES)], np.int32)
_PAIR_B = np.array([EXPERTS_PER_GROUP * (c // len(PAIRS)) + PAIRS[c % len(PAIRS)][1]
                    for c in range(N_CLASSES)], np.int32)


def _dest_kernel(start_ref, cls_ref, rank_ref, dest_ref):
    cls = cls_ref[...]
    dest = rank_ref[...]
    for c in range(N_CLASSES):
        dest = dest + jnp.where(cls == c, start_ref[c], 0)
    dest_ref[...] = dest


def _sorted_rows(row_start, cls, rank):
    shape = (TOKENS // LANES, LANES)
    dest = pl.pallas_call(
        _dest_kernel,
        out_shape=jax.ShapeDtypeStruct(shape, jnp.int32),
        in_specs=[pl.BlockSpec(memory_space=pltpu.SMEM), pl.BlockSpec(memory_space=pltpu.VMEM),
                  pl.BlockSpec(memory_space=pltpu.VMEM)],
        out_specs=pl.BlockSpec(memory_space=pltpu.VMEM),
        name="sorted_rows",
    )(row_start, cls.reshape(shape), rank.reshape(shape))
    return dest.reshape(-1)


def _moe(layer, u2, cls, rank, cnt, w_gate, w_up, w_down, wr_rows):
    counts = cnt[:N_CLASSES, 0]
    tiles = (counts + TM_EXPERT - 1) // TM_EXPERT
    tile_end = jnp.cumsum(tiles)
    row_start = (tile_end - tiles) * TM_EXPERT
    n_used = tile_end[-1:]
    dest = _sorted_rows(row_start, cls, rank)
    tile_cls = jnp.sum(jnp.arange(N_EXPERT_TILES)[:, None] >= tile_end[None, :], axis=1)
    tile_cls = jnp.minimum(tile_cls, tile_cls[jnp.maximum(n_used[0] - 1, 0)])
    ea = jnp.asarray(_PAIR_A)[tile_cls]
    eb = jnp.asarray(_PAIR_B)[tile_cls]
    chg = jnp.concatenate([jnp.ones((1,), jnp.int32),
                           (tile_cls[1:] != tile_cls[:-1]).astype(jnp.int32)])
    tiles_before = jnp.arange(N_EXPERT_TILES) - (tile_end - tiles)[tile_cls]
    tile_rows = counts[tile_cls] - tiles_before * TM_EXPERT
    smaller_steps = jnp.asarray(EXPERT_ROW_STEPS[:-1], jnp.int32)
    short = jnp.sum(tile_rows[:, None] > smaller_steps[None, :], axis=1).astype(jnp.int32)
    src = _invert_permutation(dest, PADDED_ROWS)
    ys = _experts(layer, ea, eb, chg, n_used.astype(jnp.int32), short, src, u2, w_gate, w_up, w_down,
                  wr_rows)
    return dest, ys


def kernel(x, c, norm1_g, norm2_g, w_ada, b_ada, m_w_in, m_b_gates, m_norm_g, m_w_out,
           c_w_in, c_conv_w, c_conv_b, c_w_out, w_router, b_router,
           e_w_gate, e_w_up, e_w_down, final_g):
    xf = x.reshape(TOKENS, D_MODEL)
    mod = _ada(c, w_ada, b_ada)

    bg = jnp.zeros((1, GATE_COLS), F32).at[0, :2 * N_HEADS].set(m_b_gates[0])
    wr = jnp.zeros((D_MODEL, ROUTER_COLS), BF16).at[:, :N_EXPERTS].set(w_router.astype(BF16))
    wr_rows = w_router.astype(BF16).astype(F32).T.reshape(N_EXPERTS, 1, D_MODEL)
    br = b_router.reshape(N_EXPERTS, 1)
    row = lambda v: v.reshape(1, -1)

    x1, u2, cls, rank, cnt = _mlstm_layer(xf, mod[0], row(norm1_g[0]), m_w_in.swapaxes(1, 2), bg,
                                          row(m_norm_g[0]),
                                          m_w_out[0].astype(BF16), row(norm2_g[0]), wr, br)
    dest, ys = _moe(0, u2, cls, rank, cnt, e_w_gate, e_w_up, e_w_down, wr_rows)

    x3, u2, cls, rank, cnt = _conv_layer(
        dest, x1, ys, mod[0], mod[1], row(norm1_g[1]), c_w_in, c_conv_w[0],
        row(c_conv_b[0]), c_w_out, row(norm2_g[1]), wr, br)
    dest, ys = _moe(1, u2, cls, rank, cnt, e_w_gate, e_w_up, e_w_down, wr_rows)

    out = _final(dest, x3, ys, mod[1], row(final_g))
    return out.reshape(BATCH, SEQ, D_MODEL)
```

```python
import functools

import jax
import jax.numpy as jnp
import numpy as np
from jax import lax
from jax.experimental import pallas as pl
from jax.experimental.pallas import tpu as pltpu

F32 = jnp.float32
BF16 = jnp.bfloat16

D_MODEL = 1024
BATCH = 4
SEQ = 8192
TOKENS = BATCH * SEQ
N_HEADS = 4
DH_V = 256
DH_QK = 128
QK = N_HEADS * DH_QK
N_EXPERTS = 16
N_GROUPS = 4
EXPERTS_PER_GROUP = 4
D_EXPERT = 512
EPS = 1e-6

LANES = 128
SUBLANES = 8
VMEM_LIMIT_BYTES = 56 * 1024 * 1024

CHUNK = 128
TM_CONV = 512
TM_EXPERT = 512
EXPERT_ROW_STEPS = (TM_EXPERT // 2, 3 * TM_EXPERT // 4, TM_EXPERT)
GATE_COLS = LANES
ROUTER_COLS = LANES
GATE_WIDTH = 16
PAIRS = ((0, 1), (0, 2), (0, 3), (1, 2), (1, 3), (2, 3))
N_CLASSES = N_GROUPS * len(PAIRS)
CLASS_ROWS = 32
N_EXPERT_TILES = TOKENS // TM_EXPERT + N_CLASSES
PADDED_ROWS = N_EXPERT_TILES * TM_EXPERT
TM_FINAL = 512
EXPERT_GATHER_BUFFERS = 4
PROJ_COLS = 256
PROJ_EVERY = 1


def _params(*semantics):
    return pltpu.CompilerParams(dimension_semantics=semantics, vmem_limit_bytes=VMEM_LIMIT_BYTES)


def _dot(a, b):
    return jnp.dot(a, b, preferred_element_type=F32)


def _rms(x):
    return x * lax.rsqrt(jnp.mean(x * x, axis=-1, keepdims=True) + EPS)


def _sigmoid(x):
    return 1.0 / (1.0 + jnp.exp(-x))


ROW_TILE = D_MODEL // LANES


def _load_token_rows(ref, n):
    return jnp.concatenate([ref[pl.ds(c, n, stride=ROW_TILE), :] for c in range(ROW_TILE)], axis=1)


def _store_token_rows(ref, val, n):
    for c in range(ROW_TILE):
        ref[pl.ds(c, n, stride=ROW_TILE), :] = val[:, c * LANES:(c + 1) * LANES]


def _token_tile(ref, t):
    return ref.at[pl.ds(pl.multiple_of(t * ROW_TILE, ROW_TILE), ROW_TILE)]


def _start_row_gather(idx_ref, base, n, src_hbm, buf, sem):
    for r in range(n):
        pltpu.make_async_copy(_token_tile(src_hbm, idx_ref[base + r]), _token_tile(buf, r),
                              sem).start(priority=r % 2)


def _wait_row_gather(src_hbm, buf, sem):
    pltpu.make_async_copy(src_hbm.at[pl.ds(0, buf.shape[0])], buf, sem).wait()


def _gathered_tile(idx_ref, src_hbm, buf_ref, sem_ref, n, issue_first, n_tiles=None, n_active=None):
    i = pl.program_id(0)
    last = (pl.num_programs(0) if n_tiles is None else n_tiles) - 1
    nb = buf_ref.shape[0]
    ahead = nb - 1

    @pl.when(i == 0)
    def _():
        for k in range(ahead):
            _start_row_gather(idx_ref, jnp.minimum(k, last) * n, n, src_hbm, buf_ref.at[k], sem_ref.at[k])

    nxt = (i + ahead) % nb
    start_next = functools.partial(
        _start_row_gather, idx_ref, jnp.minimum(i + ahead, last) * n, n, src_hbm, buf_ref.at[nxt],
        sem_ref.at[nxt])
    if issue_first:
        start_next()
    slot = i % nb
    if n_active is None:
        _wait_row_gather(src_hbm, buf_ref.at[slot], sem_ref.at[slot])
    else:
        @pl.when(i < n_active + ahead)
        def _():
            _wait_row_gather(src_hbm, buf_ref.at[slot], sem_ref.at[slot])

    load_rows = functools.partial(_load_token_rows, buf_ref.at[slot])
    if issue_first is None:
        return load_rows, start_next
    rows = load_rows(n)
    if not issue_first:
        start_next()
    return rows


def _drain_row_gather(src_hbm, buf_ref, sem_ref, n_active=None):
    i = pl.program_id(0)
    nb = buf_ref.shape[0]

    @pl.when(i == pl.num_programs(0) - 1)
    def _():
        for k in range(1, nb):
            wait = functools.partial(_wait_row_gather, src_hbm, buf_ref.at[(i + k) % nb],
                                     sem_ref.at[(i + k) % nb])
            if n_active is None:
                wait()
            else:
                pl.when(i + k - (nb - 1) < n_active)(wait)


def _ada_kernel(c_ref, w_ref, b_ref, o_ref):
    c = c_ref[...]
    cond = c * _sigmoid(c)
    o_ref[0] = _dot(cond.astype(BF16), w_ref[0].astype(BF16)) + b_ref[0]


def _ada(c, w_ada, b_ada):
    depth, d, n = w_ada.shape
    tn = 1536
    c8 = jnp.zeros((SUBLANES, d), F32).at[:BATCH].set(c)
    out = pl.pallas_call(
        _ada_kernel,
        out_shape=jax.ShapeDtypeStruct((depth, SUBLANES, n), F32),
        grid=(depth, n // tn),
        in_specs=[
            pl.BlockSpec((SUBLANES, d), lambda l, j: (0, 0)),
            pl.BlockSpec((1, d, tn), lambda l, j: (l, 0, j)),
            pl.BlockSpec((1, 1, tn), lambda l, j: (l, 0, j)),
        ],
        out_specs=pl.BlockSpec((1, SUBLANES, tn), lambda l, j: (l, 0, j)),
        compiler_params=_params("arbitrary", "arbitrary"),
        name="ada_mod",
    )(c8, w_ada, b_ada.reshape(depth, 1, n))
    return out[:, :BATCH].reshape(depth, BATCH, 6, d)


def _mlstm_layer_kernel(xn_ref, x_ref, mod_ref, g1_ref, w32_ref, bg_ref, ng_ref,
                        wo_ref, g2_ref, wr_ref, br_ref, tri_ref,
                        x1_ref, u2_ref, cls_ref, rank_ref, cnt_ref,
                        wi_ref, q_ref, k_ref, v_ref, og_ref, gcol_ref, grow_ref, m_ref, n_ref, h_ref, run_ref,
                        *c_refs):
    step = pl.program_id(0)

    @pl.when(step == 0)
    def _():
        for ref in (q_ref, k_ref, v_ref, og_ref, gcol_ref, grow_ref, m_ref, n_ref, h_ref, run_ref) + c_refs:
            ref[...] = jnp.zeros_like(ref)
        n_main = 2 * QK + 2 * D_MODEL
        for lo in range(0, n_main, PROJ_COLS):
            wi_ref[:, lo:lo + PROJ_COLS] = w32_ref[0, lo:lo + PROJ_COLS, :].T.astype(BF16)
        gates = jnp.concatenate([w32_ref[0, n_main:n_main + 2 * N_HEADS, :],
                                 jnp.zeros((GATE_COLS - 2 * N_HEADS, D_MODEL), F32)], axis=0)
        wi_ref[:, n_main:] = gates.T.astype(BF16)

    h_prev = h_ref[...]
    ln = CHUNK
    row = lax.broadcasted_iota(jnp.int32, (ln, ln), 0)
    col = lax.broadcasted_iota(jnp.int32, (ln, ln), 1)
    causal = col <= row
    n_streams = BATCH * N_HEADS
    m_all = [m_ref[st] for st in range(n_streams)]
    n_all = [n_ref[st] for st in range(n_streams)]
    m_out, n_out = [None] * n_streams, [None] * n_streams

    def stream(st):
        bi, h = divmod(st, N_HEADS)
        gates = gcol_ref[bi]
        column = lambda kind: jnp.broadcast_to(
            gates[:, kind * N_HEADS + h:kind * N_HEADS + h + 1], (ln, LANES))
        cm, b, imb_col = column(0), column(1), column(2)
        imb_row = grow_ref[bi, h:h + 1, :]
        qh = q_ref[bi, :, h * DH_QK:(h + 1) * DH_QK]
        kh = k_ref[bi, :, h * DH_QK:(h + 1) * DH_QK]
        vh = v_ref[bi, :, h * DH_V:(h + 1) * DH_V]
        c_ref = c_refs[st]
        c_prev = c_ref[...]
        m_prev = m_all[st]
        s_raw = lax.dot_general(qh, kh, (((1,), (1,)), ((), ())), preferred_element_type=F32)
        yield
        big_m = jnp.maximum(m_prev, cm)
        d_mat = jnp.exp(jnp.where(causal, imb_row - big_m, -jnp.inf))
        q_inter = qh.astype(F32) * jnp.exp(m_prev - big_m)
        s = s_raw * d_mat
        lhs = jnp.concatenate([q_inter.astype(BF16), s.astype(BF16)], axis=1)
        rhs = jnp.concatenate([c_prev.astype(BF16), vh], axis=0)
        num = _dot(lhs, rhs)
        den = jnp.sum(s + q_inter * n_all[st], axis=-1, keepdims=True)
        yield
        m_last = big_m[ln - 1:ln, :]
        kw = (kh.astype(F32) * jnp.exp(imb_col - m_last)).astype(BF16)
        update = lax.dot_general(kw, vh, (((0,), (0,)), ((), ())), preferred_element_type=F32)
        yield
        decay = jnp.exp(m_prev - m_last)
        c_ref[...] = jnp.concatenate([decay] * 2, axis=1) * c_prev + update
        n_out[st] = decay * n_all[st] + jnp.sum(kw.astype(F32), axis=0, keepdims=True)
        m_out[st] = b[ln - 1:ln, :] + m_last
        inv = 1.0 / jnp.maximum(jnp.abs(den), jnp.exp(-(b + big_m)))
        hh = num * jnp.concatenate([inv, inv], axis=1)
        sl = slice(h * DH_V, (h + 1) * DH_V)
        gate = _sigmoid(og_ref[bi, :, sl].astype(F32))
        h_ref[bi * ln:(bi + 1) * ln, sl] = (_rms(hh) * ng_ref[:, sl] * gate).astype(BF16)

    deferred_stores = []

    def projection():
        mod = mod_ref[...]
        u = jnp.concatenate(
            [_rms(xn_ref[bi]) * g1_ref[...] * (1.0 + mod[bi][1:2]) + mod[bi][0:1] for bi in range(BATCH)],
            axis=0).astype(BF16)
        gt = _dot(u, wi_ref[:, 2 * QK + 2 * D_MODEL:]) + bg_ref[...]
        cols, imb = _gate_columns(gt)
        imb_rows = imb.T[0:SUBLANES, :]
        for bi in range(BATCH):
            gcol_ref[bi] = cols[bi * ln:(bi + 1) * ln, 0:GATE_WIDTH]
            grow_ref[bi] = imb_rows[:, bi * ln:(bi + 1) * ln]
        yield
        lo = 0
        for ref, width, scale in ((q_ref, QK, DH_QK ** -0.5), (k_ref, QK, None), (v_ref, D_MODEL, None),
                                  (og_ref, D_MODEL, None)):
            for off in range(0, width, PROJ_COLS):
                part = _dot(u, wi_ref[:, lo + off:lo + off + PROJ_COLS])
                if scale is not None:
                    part = part * scale
                part = part.astype(BF16)

                def store(ref=ref, off=off, part=part):
                    for bi in range(BATCH):
                        ref[bi, :, off:off + PROJ_COLS] = part[bi * ln:(bi + 1) * ln]

                if ref is og_ref:
                    deferred_stores.append(store)
                else:
                    store()
                yield
            lo += width

    def previous_chunk():
        half = D_MODEL // 2
        mix0 = _dot(h_prev, wo_ref[:, 0:half])
        yield
        mix1 = _dot(h_prev, wo_ref[:, half:])
        yield
        mod = mod_ref[...]
        u2_rows = []
        for bi in range(BATCH):
            mb = mod[bi]
            rows = slice(bi * ln, (bi + 1) * ln)
            x1 = x_ref[bi] + mb[2:3] * jnp.concatenate([mix0[rows], mix1[rows]], axis=1)
            x1_ref[bi] = x1
            u2_b = _moe_input(x1, mb, g2_ref)
            _store_token_rows(u2_ref.at[bi], u2_b, ln)
            u2_rows.append(u2_b)
        valid = jnp.where(step > 1, 1.0, 0.0)
        yield from _route_stages(jnp.concatenate(u2_rows, axis=0), valid, wr_ref, br_ref, tri_ref,
                                 cls_ref, rank_ref, cnt_ref, run_ref)

    _round_robin([previous_chunk()] + [stream(st) for st in range(n_streams)],
                 background=projection(), every=PROJ_EVERY, start_after=n_streams + 1)
    for store in deferred_stores:
        store()
    for st in range(n_streams):
        m_ref[st] = m_out[st]
        n_ref[st] = n_out[st]


def _gate_columns(gt):
    lane = lax.broadcasted_iota(jnp.int32, gt.shape, 1)
    pos = lax.broadcasted_iota(jnp.int32, gt.shape, 0) % CHUNK
    b = jnp.minimum(gt, 0.0) - jnp.log(1.0 + jnp.exp(-jnp.abs(gt)))
    shift = 1
    while shift < CHUNK:
        b = b + jnp.where(pos >= shift, pltpu.roll(b, shift, axis=0), 0.0)
        shift *= 2
    imb = gt - pltpu.roll(b, LANES - N_HEADS, axis=1)
    cm = imb
    shift = 1
    while shift < CHUNK:
        cm = jnp.maximum(cm, jnp.where(pos >= shift, pltpu.roll(cm, shift, axis=0), -jnp.inf))
        shift *= 2
    cols = jnp.where(lane < N_HEADS, cm,
                     jnp.where(lane < 2 * N_HEADS, b,
                               jnp.where(lane < 3 * N_HEADS, pltpu.roll(imb, 2 * N_HEADS, axis=1), 0.0)))
    return cols, imb


def _mlstm_layer(x, mod, g1, w_in, bg, norm_g, w_out, g2, wr, br):
    nc = SEQ // CHUNK
    n_streams = BATCH * N_HEADS
    rows = BATCH * CHUNK
    newest = lambda c: jnp.minimum(c, nc - 1)
    oldest = lambda c: jnp.maximum(c - 2, 0)
    old_rows = lambda height, width: pl.BlockSpec((BATCH, height, width), lambda c: (0, oldest(c), 0))
    x3d = x.reshape(BATCH, SEQ, D_MODEL)
    x1, u2, cls, rank, cnt = pl.pallas_call(
        _mlstm_layer_kernel,
        out_shape=(
            jax.ShapeDtypeStruct((BATCH, SEQ, D_MODEL), F32),
            jax.ShapeDtypeStruct((BATCH, SEQ * ROW_TILE, LANES), F32),
            jax.ShapeDtypeStruct((nc, 1, rows), jnp.int32),
            jax.ShapeDtypeStruct((nc, 1, rows), jnp.int32),
            jax.ShapeDtypeStruct((CLASS_ROWS, LANES), jnp.int32),
        ),
        grid=(nc + 2,),
        in_specs=[
            pl.BlockSpec((BATCH, CHUNK, D_MODEL), lambda c: (0, newest(c), 0)),
            old_rows(CHUNK, D_MODEL),
            _const_spec((BATCH, 6, D_MODEL)),
            _const_spec((1, D_MODEL)),
            _const_spec(w_in.shape),
            _const_spec((1, GATE_COLS)),
            _const_spec((1, D_MODEL)),
            _const_spec((D_MODEL, D_MODEL)),
            _const_spec((1, D_MODEL)),
            _const_spec((D_MODEL, ROUTER_COLS)),
            _const_spec((N_EXPERTS, 1)),
            _const_spec((rows, rows)),
        ],
        out_specs=(
            old_rows(CHUNK, D_MODEL),
            old_rows(CHUNK * ROW_TILE, LANES),
            pl.BlockSpec((1, 1, rows), lambda c: (oldest(c), 0, 0)),
            pl.BlockSpec((1, 1, rows), lambda c: (oldest(c), 0, 0)),
            _const_spec((CLASS_ROWS, LANES)),
        ),
        scratch_shapes=[pltpu.VMEM((D_MODEL, 2 * QK + 2 * D_MODEL + GATE_COLS), BF16),
                        pltpu.VMEM((BATCH, CHUNK, QK), BF16),
                        pltpu.VMEM((BATCH, CHUNK, QK), BF16),
                        pltpu.VMEM((BATCH, CHUNK, D_MODEL), BF16),
                        pltpu.VMEM((BATCH, CHUNK, D_MODEL), BF16),
                        pltpu.VMEM((BATCH, CHUNK, GATE_WIDTH), F32),
                        pltpu.VMEM((BATCH, SUBLANES, CHUNK), F32),
                        pltpu.VMEM((n_streams, 1, LANES), F32),
                        pltpu.VMEM((n_streams, 1, DH_QK), F32),
                        pltpu.VMEM((rows, D_MODEL), BF16),
                        pltpu.VMEM((CLASS_ROWS, LANES), F32)]
        + [pltpu.VMEM((DH_QK, DH_V), F32) for _ in range(n_streams)],
        compiler_params=_params("arbitrary"),
        name="mlstm_layer",
    )(x3d, x3d, mod, g1, w_in, bg, norm_g, w_out, g2, wr, br, _strict_upper(rows))
    token_order = lambda a: a.reshape(nc, BATCH, CHUNK).transpose(1, 0, 2).reshape(TOKENS)
    return (x1.reshape(TOKENS, D_MODEL), u2.reshape(TOKENS * ROW_TILE, LANES), token_order(cls),
            token_order(rank), cnt)


def _top2_sum(v0, v1, v2, v3):
    hi1, lo1 = jnp.maximum(v0, v1), jnp.minimum(v0, v1)
    hi2, lo2 = jnp.maximum(v2, v3), jnp.minimum(v2, v3)
    return jnp.maximum(hi1, hi2) + jnp.maximum(jnp.minimum(hi1, hi2), jnp.maximum(lo1, lo2))


def _route_tail(x_new, m, valid, g2_ref, wr_ref, br_ref, tri_ref, u2_ref, cls_ref, rank_ref, cnt_ref,
                run_ref, after_router=None):
    u2 = _moe_input(x_new, m, g2_ref)
    _store_token_rows(u2_ref, u2, u2.shape[0])
    _route(u2, valid, wr_ref, br_ref, tri_ref, cls_ref, rank_ref, cnt_ref, run_ref, after_router)


def _moe_input(x_new, m, g2_ref):
    return _rms(x_new) * g2_ref[...] * (1.0 + m[4:5]) + m[3:4]


def _round_robin(chains, background=None, every=1, start_after=0):
    chains = list(chains)
    advanced = 0
    while chains:
        alive = []
        for chain in chains:
            try:
                next(chain)
            except StopIteration:
                continue
            alive.append(chain)
            advanced += 1
            if background is not None and advanced >= start_after and advanced % every == 0:
                next(background, None)
        chains = alive
    if background is not None:
        for _ in background:
            pass


def _route(u2, valid, wr_ref, br_ref, tri_ref, cls_ref, rank_ref, cnt_ref, run_ref, after_router=None):
    stages = _route_stages(u2, valid, wr_ref, br_ref, tri_ref, cls_ref, rank_ref, cnt_ref, run_ref)
    next(stages)
    if after_router is not None:
        after_router()
    for _ in stages:
        pass


def _route_stages(u2, valid, wr_ref, br_ref, tri_ref, cls_ref, rank_ref, cnt_ref, run_ref):
    logits = _dot(u2.astype(BF16), wr_ref[...])
    yield
    lt = logits.T[0:N_EXPERTS, :]
    e = jnp.exp(lt - jnp.max(lt, axis=0, keepdims=True))
    probs = e / jnp.sum(e, axis=0, keepdims=True)
    sel = probs + br_ref[...]
    sel_rows = [sel[j:j + 1, :] for j in range(N_EXPERTS)]
    best = jnp.zeros_like(sel_rows[0], dtype=jnp.int32)
    best_score = _top2_sum(*sel_rows[0:EXPERTS_PER_GROUP])
    for g in range(1, N_GROUPS):
        score = _top2_sum(*sel_rows[g * EXPERTS_PER_GROUP:(g + 1) * EXPERTS_PER_GROUP])
        better = score > best_score
        best = jnp.where(better, g, best)
        best_score = jnp.where(better, score, best_score)
    s = []
    for j in range(EXPERTS_PER_GROUP):
        sj = sel_rows[j]
        for g in range(1, N_GROUPS):
            sj = jnp.where(best == g, sel_rows[g * EXPERTS_PER_GROUP + j], sj)
        s.append(sj)
    chosen = []
    for j in range(EXPERTS_PER_GROUP):
        beaten = jnp.zeros_like(best)
        for i in range(EXPERTS_PER_GROUP):
            if i == j:
                continue
            wins = (s[i] >= s[j]) if i < j else (s[i] > s[j])
            beaten = beaten + jnp.where(wins, 1, 0)
        chosen.append(beaten < 2)
    pair = jnp.full_like(best, len(PAIRS) - 1)
    for p in range(len(PAIRS) - 2, -1, -1):
        a, b = PAIRS[p]
        pair = jnp.where(jnp.logical_and(chosen[a], chosen[b]), p, pair)
    cls = best * len(PAIRS) + pair
    cls_ref[0] = cls

    class_id = lax.broadcasted_iota(jnp.int32, (CLASS_ROWS, cls.shape[1]), 0)
    onehot = class_id == cls
    before = _dot(jnp.where(onehot, 1.0, 0.0).astype(BF16), tri_ref[...])
    run = run_ref[...]
    rank = jnp.sum(jnp.where(onehot, before + run[:, 0:1], 0.0), axis=0, keepdims=True)
    rank_ref[0] = rank.astype(jnp.int32)
    run = run + valid * jnp.sum(jnp.where(onehot, 1.0, 0.0), axis=1, keepdims=True)
    run_ref[...] = run
    cnt_ref[...] = run.astype(jnp.int32)


def _strict_upper(n):
    return (jnp.arange(n)[:, None] < jnp.arange(n)[None, :]).astype(BF16)


def _route_out_shapes(t, tm):
    return (
        jax.ShapeDtypeStruct((t, D_MODEL), F32),
        jax.ShapeDtypeStruct((t * ROW_TILE, LANES), F32),
        jax.ShapeDtypeStruct((t // tm, 1, tm), jnp.int32),
        jax.ShapeDtypeStruct((t // tm, 1, tm), jnp.int32),
        jax.ShapeDtypeStruct((CLASS_ROWS, LANES), jnp.int32),
    )


def _cur_tile(n_tiles):
    return lambda i: jnp.minimum(i, n_tiles - 1)


def _prev_tile(i):
    return jnp.maximum(i - 1, 0)


def _route_out_specs(tm, n_tiles):
    cur = _cur_tile(n_tiles)
    return (
        pl.BlockSpec((tm, D_MODEL), lambda i, *_: (cur(i), 0)),
        pl.BlockSpec((tm * ROW_TILE, LANES), lambda i, *_: (_prev_tile(i), 0)),
        pl.BlockSpec((1, 1, tm), lambda i, *_: (_prev_tile(i), 0, 0)),
        pl.BlockSpec((1, 1, tm), lambda i, *_: (_prev_tile(i), 0, 0)),
        pl.BlockSpec((CLASS_ROWS, LANES), lambda i, *_: (0, 0)),
    )


def _init_deferred_tail(xprev_ref, run_ref):
    @pl.when(pl.program_id(0) == 0)
    def _():
        xprev_ref[...] = jnp.zeros_like(xprev_ref)
        run_ref[...] = jnp.zeros_like(run_ref)


def _deferred_tail(xprev_ref, modp_ref, *tail_refs, after_router=None):
    valid = jnp.where(pl.program_id(0) > 0, 1.0, 0.0)
    _route_tail(xprev_ref[...], modp_ref[0], valid, *tail_refs, after_router=after_router)


def _const_spec(shape):
    return pl.BlockSpec(shape, lambda i, *_: (0,) * len(shape))


def _conv_layer_kernel(dest_ref, x_ref, ys_ref, mod0_ref, mod_ref, modp_ref, g1_ref, wi_ref, cw_ref,
                       cb_ref, wo_ref, g2_ref, wr_ref, br_ref, tri_ref,
                       x3_ref, u2_ref, cls_ref, rank_ref, cnt_ref,
                       run_ref, carry_ref, ybuf_ref, ysem_ref, xprev_ref):
    tm = x_ref.shape[0]
    tiles_per_batch = SEQ // tm
    n_tiles = pl.num_programs(0) - 1
    i = pl.program_id(0)

    @pl.when(i % tiles_per_batch == 0)
    def _():
        carry_ref[...] = jnp.zeros_like(carry_ref)

    _init_deferred_tail(xprev_ref, run_ref)
    m = mod_ref[0]
    y = _gathered_tile(dest_ref, ys_ref, ybuf_ref, ysem_ref, tm, issue_first=False, n_tiles=n_tiles)
    x2 = x_ref[...] + mod0_ref[0][5:6] * y
    u = (_rms(x2) * g1_ref[...] * (1.0 + m[1:2]) + m[0:1]).astype(BF16)
    bgate = _dot(u, wi_ref[:, 0:D_MODEL])
    gated = []

    def project_gated():
        gated.append(_dot(u, wi_ref[:, D_MODEL:2 * D_MODEL]) * _dot(u, wi_ref[:, 2 * D_MODEL:]))

    _deferred_tail(xprev_ref, modp_ref, g2_ref, wr_ref, br_ref, tri_ref, u2_ref, cls_ref, rank_ref,
                   cnt_ref, run_ref, after_router=project_gated)
    z = gated[0]
    prev = carry_ref[...]
    row = lax.broadcasted_iota(jnp.int32, z.shape, 0)
    z1 = jnp.where(row == 0, prev[7:8], pltpu.roll(z, 1, axis=0))
    z2 = jnp.where(row == 0, prev[6:7], jnp.where(row == 1, prev[7:8], pltpu.roll(z, 2, axis=0)))
    carry_ref[...] = z[tm - SUBLANES:, :]
    cw = cw_ref[...]
    zc = cw[0:1] * z2 + cw[1:2] * z1 + cw[2:3] * z + cb_ref[...]
    x3 = x2 + m[2:3] * _dot((bgate * zc).astype(BF16), wo_ref[...])
    xprev_ref[...] = x3

    @pl.when(i < n_tiles)
    def _():
        x3_ref[...] = x3

    _drain_row_gather(ys_ref, ybuf_ref, ysem_ref)


def _conv_layer(dest, x, ys, mod0, mod, g1, w_in, conv_w, conv_b, w_out, g2, wr, br):
    t = x.shape[0]
    tm = TM_CONV
    tri = _strict_upper(tm)
    tiles_per_batch = SEQ // tm
    n_tiles = t // tm
    cur = _cur_tile(n_tiles)
    mod_spec = pl.BlockSpec((1, 6, D_MODEL), lambda i, *_: (cur(i) // tiles_per_batch, 0, 0))
    modp_spec = pl.BlockSpec((1, 6, D_MODEL), lambda i, *_: (_prev_tile(i) // tiles_per_batch, 0, 0))
    return pl.pallas_call(
        _conv_layer_kernel,
        out_shape=_route_out_shapes(t, tm),
        grid_spec=pltpu.PrefetchScalarGridSpec(
            num_scalar_prefetch=1,
            grid=(n_tiles + 1,),
            in_specs=[
                pl.BlockSpec((tm, D_MODEL), lambda i, *_: (cur(i), 0)),
                pl.BlockSpec(memory_space=pl.ANY),
                mod_spec,
                mod_spec,
                modp_spec,
                _const_spec((1, D_MODEL)),
                _const_spec((D_MODEL, 3 * D_MODEL)),
                _const_spec((3, D_MODEL)),
                _const_spec((1, D_MODEL)),
                _const_spec((D_MODEL, D_MODEL)),
                _const_spec((1, D_MODEL)),
                _const_spec((D_MODEL, ROUTER_COLS)),
                _const_spec((N_EXPERTS, 1)),
                _const_spec((tm, tm)),
            ],
            out_specs=_route_out_specs(tm, n_tiles),
            scratch_shapes=[
                pltpu.VMEM((CLASS_ROWS, LANES), F32),
                pltpu.VMEM((SUBLANES, D_MODEL), F32),
                pltpu.VMEM((2, tm * ROW_TILE, LANES), F32),
                pltpu.SemaphoreType.DMA((2,)),
                pltpu.VMEM((tm, D_MODEL), F32),
            ],
        ),
        compiler_params=_params("arbitrary"),
        name="conv_layer_route",
    )(dest, x, ys, mod0, mod, mod, g1, w_in, conv_w, conv_b, w_out, g2, wr, br, tri)


def _invert_kernel(dest_ref, fill_ref, src_ref, sem):
    fill = pltpu.make_async_copy(fill_ref, src_ref, sem)
    fill.start()
    fill.wait()
    unroll = 32

    def body(t8, carry):
        for k in range(unroll):
            t = t8 * unroll + k
            src_ref[dest_ref[t]] = t
        return carry

    lax.fori_loop(0, dest_ref.shape[0] // unroll, body, 0)


def _invert_permutation(dest, n_out):
    smem = pl.BlockSpec(memory_space=pltpu.SMEM)
    return pl.pallas_call(
        _invert_kernel,
        out_shape=jax.ShapeDtypeStruct((n_out,), jnp.int32),
        in_specs=[smem, pl.BlockSpec(memory_space=pl.ANY)],
        out_specs=smem,
        scratch_shapes=[pltpu.SemaphoreType.DMA(())],
        name="invert_permutation",
    )(dest, jnp.arange(n_out, dtype=jnp.int32) % dest.shape[0])


def _expert_kernel(ea_ref, eb_ref, chg_ref, nused_ref, short_ref, src_ref, u2_ref, wga_ref, wua_ref,
                   wda_ref, wgb_ref, wub_ref, wdb_ref, wra_ref, wrb_ref, y_ref, xbuf_ref, xsem_ref,
                   *wbf_refs):
    del ea_ref, eb_ref
    j = pl.program_id(0)
    tm = TM_EXPERT

    @pl.when(chg_ref[j] == 1)
    def _():
        for src, dst in zip((wga_ref, wua_ref, wda_ref, wgb_ref, wub_ref, wdb_ref), wbf_refs):
            dst[...] = src[0, 0].astype(BF16)

    load_rows, start_next = _gathered_tile(src_ref, u2_ref, xbuf_ref, xsem_ref, tm, issue_first=None,
                                           n_active=nused_ref[0])

    def leading_rows(rows):
        xb = load_rows(rows).astype(BF16)
        start_next()
        dl = jnp.sum(xb.astype(F32) * (wra_ref[0] - wrb_ref[0]), axis=-1, keepdims=True)
        w_a = _sigmoid(dl)
        w_b = _sigmoid(-dl)

        def ffn(wg_ref, wu_ref, wd_ref):
            gate = _dot(xb, wg_ref[...])
            hidden = gate * _sigmoid(gate) * _dot(xb, wu_ref[...])
            return _dot(hidden.astype(BF16), wd_ref[...])

        y = w_a * ffn(*wbf_refs[0:3]) + w_b * ffn(*wbf_refs[3:6])
        _store_token_rows(y_ref, y, rows)
        if rows < tm:
            y_ref[rows * ROW_TILE:, :] = jnp.zeros(((tm - rows) * ROW_TILE, LANES), F32)

    used = j < nused_ref[0]
    for k, rows in enumerate(EXPERT_ROW_STEPS):
        pl.when(used & (short_ref[j] == k))(functools.partial(leading_rows, rows))

    @pl.when(j >= nused_ref[0])
    def _():
        y_ref[...] = jnp.zeros_like(y_ref)

    _drain_row_gather(u2_ref, xbuf_ref, xsem_ref, n_active=nused_ref[0])


def _experts(layer, ea, eb, chg, n_used, short, src, u2, w_gate, w_up, w_down, wr_rows):
    tm = TM_EXPERT
    sel_a = lambda j, ea, eb, *_: (layer, ea[j], 0, 0)
    sel_b = lambda j, ea, eb, *_: (layer, eb[j], 0, 0)
    up_spec = lambda sel: pl.BlockSpec((1, 1, D_MODEL, D_EXPERT), sel)
    down_spec = lambda sel: pl.BlockSpec((1, 1, D_EXPERT, D_MODEL), sel)
    wr_spec = lambda sel: pl.BlockSpec((1, 1, D_MODEL), lambda *a: sel(*a)[1:])
    up_scratch = pltpu.VMEM((D_MODEL, D_EXPERT), BF16)
    down_scratch = pltpu.VMEM((D_EXPERT, D_MODEL), BF16)
    return pl.pallas_call(
        _expert_kernel,
        out_shape=jax.ShapeDtypeStruct((PADDED_ROWS * ROW_TILE, LANES), F32),
        grid_spec=pltpu.PrefetchScalarGridSpec(
            num_scalar_prefetch=5,
            grid=(N_EXPERT_TILES,),
            in_specs=[
                pl.BlockSpec(memory_space=pltpu.SMEM),
                pl.BlockSpec(memory_space=pl.ANY),
                up_spec(sel_a), up_spec(sel_a), down_spec(sel_a),
                up_spec(sel_b), up_spec(sel_b), down_spec(sel_b),
                wr_spec(sel_a), wr_spec(sel_b),
            ],
            out_specs=pl.BlockSpec((tm * ROW_TILE, LANES), lambda j, *_: (j, 0)),
            scratch_shapes=[pltpu.VMEM((EXPERT_GATHER_BUFFERS, tm * ROW_TILE, LANES), F32),
                            pltpu.SemaphoreType.DMA((EXPERT_GATHER_BUFFERS,)),
                            up_scratch, up_scratch, down_scratch, up_scratch, up_scratch, down_scratch],
        ),
        compiler_params=_params("arbitrary"),
        name="grouped_experts",
    )(ea, eb, chg, n_used, short, src, u2, w_gate, w_up, w_down, w_gate, w_up, w_down, wr_rows, wr_rows)


def _final_kernel(dest_ref, x_ref, ys_ref, mod_ref, g_ref, o_ref, ybuf_ref, ysem_ref):
    y = _gathered_tile(dest_ref, ys_ref, ybuf_ref, ysem_ref, x_ref.shape[0], issue_first=True)
    x = x_ref[...] + mod_ref[0][5:6] * y
    o_ref[...] = _rms(x) * g_ref[...]
    _drain_row_gather(ys_ref, ybuf_ref, ysem_ref)


def _final(dest, x, ys, mod, g):
    t = x.shape[0]
    tm = TM_FINAL
    tiles_per_batch = SEQ // tm
    return pl.pallas_call(
        _final_kernel,
        out_shape=jax.ShapeDtypeStruct((t, D_MODEL), F32),
        grid_spec=pltpu.PrefetchScalarGridSpec(
            num_scalar_prefetch=1,
            grid=(t // tm,),
            in_specs=[
                pl.BlockSpec((tm, D_MODEL), lambda i, *_: (i, 0)),
                pl.BlockSpec(memory_space=pl.ANY),
                pl.BlockSpec((1, 6, D_MODEL), lambda i, *_: (i // tiles_per_batch, 0, 0)),
                _const_spec((1, D_MODEL)),
            ],
            out_specs=pl.BlockSpec((tm, D_MODEL), lambda i, *_: (i, 0)),
            scratch_shapes=[pltpu.VMEM((2, tm * ROW_TILE, LANES), F32), pltpu.SemaphoreType.DMA((2,))],
        ),
        compiler_params=_params("arbitrary"),
        name="final_norm",
    )(dest, x, ys, mod, g)


_PAIR_A = np.array([EXPERTS_PER_GROUP * (c // len(PAIRS)) + PAIRS[c % len(PAIRS)][0]
                    for c in range(N_CLASSES)], np.int32)
_PAIR_B = np.array([EXPERTS_PER_GROUP * (c // len(PAIRS)) + PAIRS[c % len(PAIRS)][1]
                    for c in range(N_CLASSES)], np.int32)


def _plan_kernel(cnt_ref, cls_ref, rank_ref, dest_ref, ea_ref, eb_ref, chg_ref, step_ref, nused_ref,
                 start_ref):
    tile = jnp.int32(0)
    for c in range(N_CLASSES):
        count = cnt_ref[c, 0]
        start_ref[c] = tile * TM_EXPERT

        def class_tile(k, carry, c=c, count=count, first=tile):
            j = first + k
            ea_ref[j] = jnp.int32(int(_PAIR_A[c]))
            eb_ref[j] = jnp.int32(int(_PAIR_B[c]))
            chg_ref[j] = jnp.where(k == 0, 1, 0)
            rows = count - k * TM_EXPERT
            step = jnp.int32(0)
            for smaller in EXPERT_ROW_STEPS[:-1]:
                step = step + jnp.where(rows > smaller, 1, 0)
            step_ref[j] = step
            return carry

        class_tiles = (count + TM_EXPERT - 1) // TM_EXPERT
        lax.fori_loop(0, class_tiles, class_tile, 0)
        tile = tile + class_tiles
    nused_ref[0] = tile

    last = jnp.maximum(tile - 1, 0)
    last_a = ea_ref[last]
    last_b = eb_ref[last]

    def unused_tile(j, carry):
        ea_ref[j] = last_a
        eb_ref[j] = last_b
        chg_ref[j] = 0
        step_ref[j] = 0
        return carry

    lax.fori_loop(tile, N_EXPERT_TILES, unused_tile, 0)

    cls = cls_ref[...]
    dest = rank_ref[...]
    for c in range(N_CLASSES):
        dest = dest + jnp.where(cls == c, start_ref[c], 0)
    dest_ref[...] = dest


def _routing_plan(cnt, cls, rank):
    shape = (TOKENS // LANES, LANES)
    smem = pl.BlockSpec(memory_space=pltpu.SMEM)
    vmem = pl.BlockSpec(memory_space=pltpu.VMEM)
    per_tile = jax.ShapeDtypeStruct((N_EXPERT_TILES,), jnp.int32)
    dest, ea, eb, chg, step, n_used = pl.pallas_call(
        _plan_kernel,
        out_shape=(jax.ShapeDtypeStruct(shape, jnp.int32), per_tile, per_tile, per_tile, per_tile,
                   jax.ShapeDtypeStruct((1,), jnp.int32)),
        in_specs=[smem, vmem, vmem],
        out_specs=(vmem, smem, smem, smem, smem, smem),
        scratch_shapes=[pltpu.SMEM((N_CLASSES,), jnp.int32)],
        name="routing_plan",
    )(cnt, cls.reshape(shape), rank.reshape(shape))
    return dest.reshape(-1), ea, eb, chg, step, n_used


def _moe(layer, u2, cls, rank, cnt, w_gate, w_up, w_down, wr_rows):
    dest, ea, eb, chg, step, n_used = _routing_plan(cnt, cls, rank)
    src = _invert_permutation(dest, PADDED_ROWS)
    ys = _experts(layer, ea, eb, chg, n_used, step, src, u2, w_gate, w_up, w_down, wr_rows)
    return dest, ys


def kernel(x, c, norm1_g, norm2_g, w_ada, b_ada, m_w_in, m_b_gates, m_norm_g, m_w_out,
           c_w_in, c_conv_w, c_conv_b, c_w_out, w_router, b_router,
           e_w_gate, e_w_up, e_w_down, final_g):
    xf = x.reshape(TOKENS, D_MODEL)
    mod = _ada(c, w_ada, b_ada)

    bg = jnp.zeros((1, GATE_COLS), F32).at[0, :2 * N_HEADS].set(m_b_gates[0])
    wr = jnp.zeros((D_MODEL, ROUTER_COLS), BF16).at[:, :N_EXPERTS].set(w_router.astype(BF16))
    wr_rows = w_router.astype(BF16).astype(F32).T.reshape(N_EXPERTS, 1, D_MODEL)
    br = b_router.reshape(N_EXPERTS, 1)
    row = lambda v: v.reshape(1, -1)

    x1, u2, cls, rank, cnt = _mlstm_layer(xf, mod[0], row(norm1_g[0]), m_w_in.swapaxes(1, 2), bg,
                                          row(m_norm_g[0]),
                                          m_w_out[0].astype(BF16), row(norm2_g[0]), wr, br)
    dest, ys = _moe(0, u2, cls, rank, cnt, e_w_gate, e_w_up, e_w_down, wr_rows)

    x3, u2, cls, rank, cnt = _conv_layer(
        dest, x1, ys, mod[0], mod[1], row(norm1_g[1]), c_w_in[0].astype(BF16), c_conv_w[0],
        row(c_conv_b[0]), c_w_out[0].astype(BF16), row(norm2_g[1]), wr, br)
    dest, ys = _moe(1, u2, cls, rank, cnt, e_w_gate, e_w_up, e_w_down, wr_rows)

    out = _final(dest, x3, ys, mod[1], row(final_g))
    return out.reshape(BATCH, SEQ, D_MODEL)
```

```python
import functools

import jax
import jax.numpy as jnp
import numpy as np
from jax import lax
from jax.experimental import pallas as pl
from jax.experimental.pallas import tpu as pltpu

F32 = jnp.float32
BF16 = jnp.bfloat16

D_MODEL = 1024
BATCH = 4
SEQ = 8192
TOKENS = BATCH * SEQ
N_HEADS = 4
DH_V = 256
DH_QK = 128
QK = N_HEADS * DH_QK
N_EXPERTS = 16
N_GROUPS = 4
EXPERTS_PER_GROUP = 4
D_EXPERT = 512
EPS = 1e-6

LANES = 128
SUBLANES = 8
VMEM_LIMIT_BYTES = 56 * 1024 * 1024

CHUNK = 128
TM_CONV = 512
TM_EXPERT = 512
EXPERT_ROW_STEPS = (TM_EXPERT // 2, 3 * TM_EXPERT // 4, TM_EXPERT)
GATE_COLS = LANES
ROUTER_COLS = LANES
GATE_WIDTH = 16
PAIRS = ((0, 1), (0, 2), (0, 3), (1, 2), (1, 3), (2, 3))
N_CLASSES = N_GROUPS * len(PAIRS)
CLASS_ROWS = 32
N_EXPERT_TILES = TOKENS // TM_EXPERT + N_CLASSES
PADDED_ROWS = N_EXPERT_TILES * TM_EXPERT
TM_FINAL = 512
EXPERT_GATHER_BUFFERS = 4
PROJ_COLS = 256
PROJ_EVERY = 1


def _params(*semantics):
    return pltpu.CompilerParams(dimension_semantics=semantics, vmem_limit_bytes=VMEM_LIMIT_BYTES)


def _dot(a, b):
    return jnp.dot(a, b, preferred_element_type=F32)


def _rms(x):
    return x * lax.rsqrt(jnp.mean(x * x, axis=-1, keepdims=True) + EPS)


def _sigmoid(x):
    return 1.0 / (1.0 + jnp.exp(-x))


ROW_TILE = D_MODEL // LANES


def _load_token_rows(ref, n):
    return jnp.concatenate([ref[pl.ds(c, n, stride=ROW_TILE), :] for c in range(ROW_TILE)], axis=1)


def _store_token_rows(ref, val, n):
    for c in range(ROW_TILE):
        ref[pl.ds(c, n, stride=ROW_TILE), :] = val[:, c * LANES:(c + 1) * LANES]


def _token_tile(ref, t):
    return ref.at[pl.ds(pl.multiple_of(t * ROW_TILE, ROW_TILE), ROW_TILE)]


def _start_row_gather(idx_ref, base, n, src_hbm, buf, sem):
    for r in range(n):
        pltpu.make_async_copy(_token_tile(src_hbm, idx_ref[base + r]), _token_tile(buf, r),
                              sem).start(priority=r % 2)


def _wait_row_gather(src_hbm, buf, sem):
    pltpu.make_async_copy(src_hbm.at[pl.ds(0, buf.shape[0])], buf, sem).wait()


def _gathered_tile(idx_ref, src_hbm, buf_ref, sem_ref, n, issue_first, n_tiles=None, n_active=None):
    i = pl.program_id(0)
    last = (pl.num_programs(0) if n_tiles is None else n_tiles) - 1
    nb = buf_ref.shape[0]
    ahead = nb - 1

    @pl.when(i == 0)
    def _():
        for k in range(ahead):
            _start_row_gather(idx_ref, jnp.minimum(k, last) * n, n, src_hbm, buf_ref.at[k], sem_ref.at[k])

    nxt = (i + ahead) % nb
    start_next = functools.partial(
        _start_row_gather, idx_ref, jnp.minimum(i + ahead, last) * n, n, src_hbm, buf_ref.at[nxt],
        sem_ref.at[nxt])
    if issue_first:
        start_next()
    slot = i % nb
    if n_active is None:
        _wait_row_gather(src_hbm, buf_ref.at[slot], sem_ref.at[slot])
    else:
        @pl.when(i < n_active + ahead)
        def _():
            _wait_row_gather(src_hbm, buf_ref.at[slot], sem_ref.at[slot])

    load_rows = functools.partial(_load_token_rows, buf_ref.at[slot])
    if issue_first is None:
        return load_rows, start_next
    rows = load_rows(n)
    if not issue_first:
        start_next()
    return rows


def _drain_row_gather(src_hbm, buf_ref, sem_ref, n_active=None):
    i = pl.program_id(0)
    nb = buf_ref.shape[0]

    @pl.when(i == pl.num_programs(0) - 1)
    def _():
        for k in range(1, nb):
            wait = functools.partial(_wait_row_gather, src_hbm, buf_ref.at[(i + k) % nb],
                                     sem_ref.at[(i + k) % nb])
            if n_active is None:
                wait()
            else:
                pl.when(i + k - (nb - 1) < n_active)(wait)


def _ada_kernel(c_ref, w_ref, b_ref, o_ref):
    c = c_ref[...]
    cond = c * _sigmoid(c)
    o_ref[0] = _dot(cond.astype(BF16), w_ref[0].astype(BF16)) + b_ref[0]


def _ada(c, w_ada, b_ada):
    depth, d, n = w_ada.shape
    tn = 1536
    c8 = jnp.zeros((SUBLANES, d), F32).at[:BATCH].set(c)
    out = pl.pallas_call(
        _ada_kernel,
        out_shape=jax.ShapeDtypeStruct((depth, SUBLANES, n), F32),
        grid=(depth, n // tn),
        in_specs=[
            pl.BlockSpec((SUBLANES, d), lambda l, j: (0, 0)),
            pl.BlockSpec((1, d, tn), lambda l, j: (l, 0, j)),
            pl.BlockSpec((1, 1, tn), lambda l, j: (l, 0, j)),
        ],
        out_specs=pl.BlockSpec((1, SUBLANES, tn), lambda l, j: (l, 0, j)),
        compiler_params=_params("arbitrary", "arbitrary"),
        name="ada_mod",
    )(c8, w_ada, b_ada.reshape(depth, 1, n))
    return out[:, :BATCH].reshape(depth, BATCH, 6, d)


def _mlstm_layer_kernel(xn_ref, x_ref, mod_ref, g1_ref, w32_ref, bg_ref, ng_ref,
                        wo_ref, g2_ref, wr_ref, br_ref, tri_ref,
                        x1_ref, u2_ref, cls_ref, rank_ref, cnt_ref,
                        wi_ref, q_ref, k_ref, v_ref, og_ref, gcol_ref, grow_ref, m_ref, n_ref, h_ref, run_ref,
                        *c_refs):
    step = pl.program_id(0)

    @pl.when(step == 0)
    def _():
        for ref in (q_ref, k_ref, v_ref, og_ref, gcol_ref, grow_ref, m_ref, n_ref, h_ref, run_ref) + c_refs:
            ref[...] = jnp.zeros_like(ref)
        n_main = 2 * QK + 2 * D_MODEL
        for lo in range(0, n_main, PROJ_COLS):
            wi_ref[:, lo:lo + PROJ_COLS] = w32_ref[0, lo:lo + PROJ_COLS, :].T.astype(BF16)
        gates = jnp.concatenate([w32_ref[0, n_main:n_main + 2 * N_HEADS, :],
                                 jnp.zeros((GATE_COLS - 2 * N_HEADS, D_MODEL), F32)], axis=0)
        wi_ref[:, n_main:] = gates.T.astype(BF16)

    h_prev = h_ref[...]
    ln = CHUNK
    row = lax.broadcasted_iota(jnp.int32, (ln, ln), 0)
    col = lax.broadcasted_iota(jnp.int32, (ln, ln), 1)
    causal = col <= row
    n_streams = BATCH * N_HEADS
    m_all = [m_ref[st] for st in range(n_streams)]
    n_all = [n_ref[st] for st in range(n_streams)]
    m_out, n_out = [None] * n_streams, [None] * n_streams

    def stream(st):
        bi, h = divmod(st, N_HEADS)
        gates = gcol_ref[bi]
        column = lambda kind: jnp.broadcast_to(
            gates[:, kind * N_HEADS + h:kind * N_HEADS + h + 1], (ln, LANES))
        cm, b, imb_col = column(0), column(1), column(2)
        imb_row = grow_ref[bi, h:h + 1, :]
        qh = q_ref[bi, :, h * DH_QK:(h + 1) * DH_QK]
        kh = k_ref[bi, :, h * DH_QK:(h + 1) * DH_QK]
        vh = v_ref[bi, :, h * DH_V:(h + 1) * DH_V]
        c_ref = c_refs[st]
        c_prev = c_ref[...]
        m_prev = m_all[st]
        s_raw = lax.dot_general(qh, kh, (((1,), (1,)), ((), ())), preferred_element_type=F32)
        yield
        big_m = jnp.maximum(m_prev, cm)
        d_mat = jnp.exp(jnp.where(causal, imb_row - big_m, -jnp.inf))
        q_inter = qh.astype(F32) * jnp.exp(m_prev - big_m)
        s = s_raw * d_mat
        lhs = jnp.concatenate([q_inter.astype(BF16), s.astype(BF16)], axis=1)
        rhs = jnp.concatenate([c_prev.astype(BF16), vh], axis=0)
        num = _dot(lhs, rhs)
        den = jnp.sum(s + q_inter * n_all[st], axis=-1, keepdims=True)
        yield
        m_last = big_m[ln - 1:ln, :]
        kw = (kh.astype(F32) * jnp.exp(imb_col - m_last)).astype(BF16)
        update = lax.dot_general(kw, vh, (((0,), (0,)), ((), ())), preferred_element_type=F32)
        yield
        decay = jnp.exp(m_prev - m_last)
        c_ref[...] = jnp.concatenate([decay] * 2, axis=1) * c_prev + update
        n_out[st] = decay * n_all[st] + jnp.sum(kw.astype(F32), axis=0, keepdims=True)
        m_out[st] = b[ln - 1:ln, :] + m_last
        inv = 1.0 / jnp.maximum(jnp.abs(den), jnp.exp(-(b + big_m)))
        hh = num * jnp.concatenate([inv, inv], axis=1)
        sl = slice(h * DH_V, (h + 1) * DH_V)
        gate = _sigmoid(og_ref[bi, :, sl].astype(F32))
        h_ref[bi * ln:(bi + 1) * ln, sl] = (_rms(hh) * ng_ref[:, sl] * gate).astype(BF16)

    deferred_stores = []

    def projection():
        mod = mod_ref[...]
        u = jnp.concatenate(
            [_rms(xn_ref[bi]) * g1_ref[...] * (1.0 + mod[bi][1:2]) + mod[bi][0:1] for bi in range(BATCH)],
            axis=0).astype(BF16)
        gt = _dot(u, wi_ref[:, 2 * QK + 2 * D_MODEL:]) + bg_ref[...]
        cols, imb = _gate_columns(gt)
        imb_rows = imb.T[0:SUBLANES, :]
        for bi in range(BATCH):
            gcol_ref[bi] = cols[bi * ln:(bi + 1) * ln, 0:GATE_WIDTH]
            grow_ref[bi] = imb_rows[:, bi * ln:(bi + 1) * ln]
        yield
        lo = 0
        for ref, width, scale in ((q_ref, QK, DH_QK ** -0.5), (k_ref, QK, None), (v_ref, D_MODEL, None),
                                  (og_ref, D_MODEL, None)):
            for off in range(0, width, PROJ_COLS):
                part = _dot(u, wi_ref[:, lo + off:lo + off + PROJ_COLS])
                if scale is not None:
                    part = part * scale
                part = part.astype(BF16)

                def store(ref=ref, off=off, part=part):
                    for bi in range(BATCH):
                        ref[bi, :, off:off + PROJ_COLS] = part[bi * ln:(bi + 1) * ln]

                if ref is og_ref:
                    deferred_stores.append(store)
                else:
                    store()
                yield
            lo += width

    def previous_chunk():
        half = D_MODEL // 2
        mix0 = _dot(h_prev, wo_ref[:, 0:half])
        yield
        mix1 = _dot(h_prev, wo_ref[:, half:])
        yield
        mod = mod_ref[...]
        u2_rows = []
        for bi in range(BATCH):
            mb = mod[bi]
            rows = slice(bi * ln, (bi + 1) * ln)
            x1 = x_ref[bi] + mb[2:3] * jnp.concatenate([mix0[rows], mix1[rows]], axis=1)
            x1_ref[bi] = x1
            u2_b = _moe_input(x1, mb, g2_ref)
            _store_token_rows(u2_ref.at[bi], u2_b, ln)
            u2_rows.append(u2_b)
        valid = jnp.where(step > 1, 1.0, 0.0)
        yield from _route_stages(jnp.concatenate(u2_rows, axis=0), valid, wr_ref, br_ref, tri_ref,
                                 cls_ref, rank_ref, cnt_ref, run_ref)

    _round_robin([previous_chunk()] + [stream(st) for st in range(n_streams)],
                 background=projection(), every=PROJ_EVERY, start_after=n_streams + 1)
    for store in deferred_stores:
        store()
    for st in range(n_streams):
        m_ref[st] = m_out[st]
        n_ref[st] = n_out[st]


def _gate_columns(gt):
    lane = lax.broadcasted_iota(jnp.int32, gt.shape, 1)
    pos = lax.broadcasted_iota(jnp.int32, gt.shape, 0) % CHUNK
    b = jnp.minimum(gt, 0.0) - jnp.log(1.0 + jnp.exp(-jnp.abs(gt)))
    shift = 1
    while shift < CHUNK:
        b = b + jnp.where(pos >= shift, pltpu.roll(b, shift, axis=0), 0.0)
        shift *= 2
    imb = gt - pltpu.roll(b, LANES - N_HEADS, axis=1)
    cm = imb
    shift = 1
    while shift < CHUNK:
        cm = jnp.maximum(cm, jnp.where(pos >= shift, pltpu.roll(cm, shift, axis=0), -jnp.inf))
        shift *= 2
    cols = jnp.where(lane < N_HEADS, cm,
                     jnp.where(lane < 2 * N_HEADS, b,
                               jnp.where(lane < 3 * N_HEADS, pltpu.roll(imb, 2 * N_HEADS, axis=1), 0.0)))
    return cols, imb


def _mlstm_layer(x, mod, g1, w_in, bg, norm_g, w_out, g2, wr, br):
    nc = SEQ // CHUNK
    n_streams = BATCH * N_HEADS
    rows = BATCH * CHUNK
    newest = lambda c: jnp.minimum(c, nc - 1)
    oldest = lambda c: jnp.maximum(c - 2, 0)
    old_rows = lambda height, width: pl.BlockSpec((BATCH, height, width), lambda c: (0, oldest(c), 0))
    x3d = x.reshape(BATCH, SEQ, D_MODEL)
    x1, u2, cls, rank, cnt = pl.pallas_call(
        _mlstm_layer_kernel,
        out_shape=(
            jax.ShapeDtypeStruct((BATCH, SEQ, D_MODEL), F32),
            jax.ShapeDtypeStruct((BATCH, SEQ * ROW_TILE, LANES), F32),
            jax.ShapeDtypeStruct((nc, 1, rows), jnp.int32),
            jax.ShapeDtypeStruct((nc, 1, rows), jnp.int32),
            jax.ShapeDtypeStruct((CLASS_ROWS, LANES), jnp.int32),
        ),
        grid=(nc + 2,),
        in_specs=[
            pl.BlockSpec((BATCH, CHUNK, D_MODEL), lambda c: (0, newest(c), 0)),
            old_rows(CHUNK, D_MODEL),
            _const_spec((BATCH, 6, D_MODEL)),
            _const_spec((1, D_MODEL)),
            _const_spec(w_in.shape),
            _const_spec((1, GATE_COLS)),
            _const_spec((1, D_MODEL)),
            _const_spec((D_MODEL, D_MODEL)),
            _const_spec((1, D_MODEL)),
            _const_spec((D_MODEL, ROUTER_COLS)),
            _const_spec((N_EXPERTS, 1)),
            _const_spec((rows, rows)),
        ],
        out_specs=(
            old_rows(CHUNK, D_MODEL),
            old_rows(CHUNK * ROW_TILE, LANES),
            pl.BlockSpec((1, 1, rows), lambda c: (oldest(c), 0, 0)),
            pl.BlockSpec((1, 1, rows), lambda c: (oldest(c), 0, 0)),
            _const_spec((CLASS_ROWS, LANES)),
        ),
        scratch_shapes=[pltpu.VMEM((D_MODEL, 2 * QK + 2 * D_MODEL + GATE_COLS), BF16),
                        pltpu.VMEM((BATCH, CHUNK, QK), BF16),
                        pltpu.VMEM((BATCH, CHUNK, QK), BF16),
                        pltpu.VMEM((BATCH, CHUNK, D_MODEL), BF16),
                        pltpu.VMEM((BATCH, CHUNK, D_MODEL), BF16),
                        pltpu.VMEM((BATCH, CHUNK, GATE_WIDTH), F32),
                        pltpu.VMEM((BATCH, SUBLANES, CHUNK), F32),
                        pltpu.VMEM((n_streams, 1, LANES), F32),
                        pltpu.VMEM((n_streams, 1, DH_QK), F32),
                        pltpu.VMEM((rows, D_MODEL), BF16),
                        pltpu.VMEM((CLASS_ROWS, LANES), F32)]
        + [pltpu.VMEM((DH_QK, DH_V), F32) for _ in range(n_streams)],
        compiler_params=_params("arbitrary"),
        name="mlstm_layer",
    )(x3d, x3d, mod, g1, w_in, bg, norm_g, w_out, g2, wr, br, _strict_upper(rows))
    token_order = lambda a: a.reshape(nc, BATCH, CHUNK).transpose(1, 0, 2).reshape(TOKENS)
    return (x1.reshape(TOKENS, D_MODEL), u2.reshape(TOKENS * ROW_TILE, LANES), token_order(cls),
            token_order(rank), cnt)


def _top2_sum(v0, v1, v2, v3):
    hi1, lo1 = jnp.maximum(v0, v1), jnp.minimum(v0, v1)
    hi2, lo2 = jnp.maximum(v2, v3), jnp.minimum(v2, v3)
    return jnp.maximum(hi1, hi2) + jnp.maximum(jnp.minimum(hi1, hi2), jnp.maximum(lo1, lo2))


def _route_tail(x_new, m, valid, g2_ref, wr_ref, br_ref, tri_ref, u2_ref, cls_ref, rank_ref, cnt_ref,
                run_ref, after_router=None):
    u2 = _moe_input(x_new, m, g2_ref)
    _store_token_rows(u2_ref, u2, u2.shape[0])
    _route(u2, valid, wr_ref, br_ref, tri_ref, cls_ref, rank_ref, cnt_ref, run_ref, after_router)


def _moe_input(x_new, m, g2_ref):
    return _rms(x_new) * g2_ref[...] * (1.0 + m[4:5]) + m[3:4]


def _round_robin(chains, background=None, every=1, start_after=0):
    chains = list(chains)
    advanced = 0
    while chains:
        alive = []
        for chain in chains:
            try:
                next(chain)
            except StopIteration:
                continue
            alive.append(chain)
            advanced += 1
            if background is not None and advanced >= start_after and advanced % every == 0:
                next(background, None)
        chains = alive
    if background is not None:
        for _ in background:
            pass


def _route(u2, valid, wr_ref, br_ref, tri_ref, cls_ref, rank_ref, cnt_ref, run_ref, after_router=None):
    stages = _route_stages(u2, valid, wr_ref, br_ref, tri_ref, cls_ref, rank_ref, cnt_ref, run_ref)
    next(stages)
    if after_router is not None:
        after_router()
    for _ in stages:
        pass


def _route_stages(u2, valid, wr_ref, br_ref, tri_ref, cls_ref, rank_ref, cnt_ref, run_ref):
    logits = _dot(u2.astype(BF16), wr_ref[...])
    yield
    lt = logits.T[0:N_EXPERTS, :]
    e = jnp.exp(lt - jnp.max(lt, axis=0, keepdims=True))
    probs = e / jnp.sum(e, axis=0, keepdims=True)
    sel = probs + br_ref[...]
    sel_rows = [sel[j:j + 1, :] for j in range(N_EXPERTS)]
    best = jnp.zeros_like(sel_rows[0], dtype=jnp.int32)
    best_score = _top2_sum(*sel_rows[0:EXPERTS_PER_GROUP])
    for g in range(1, N_GROUPS):
        score = _top2_sum(*sel_rows[g * EXPERTS_PER_GROUP:(g + 1) * EXPERTS_PER_GROUP])
        better = score > best_score
        best = jnp.where(better, g, best)
        best_score = jnp.where(better, score, best_score)
    s = []
    for j in range(EXPERTS_PER_GROUP):
        sj = sel_rows[j]
        for g in range(1, N_GROUPS):
            sj = jnp.where(best == g, sel_rows[g * EXPERTS_PER_GROUP + j], sj)
        s.append(sj)
    chosen = []
    for j in range(EXPERTS_PER_GROUP):
        beaten = jnp.zeros_like(best)
        for i in range(EXPERTS_PER_GROUP):
            if i == j:
                continue
            wins = (s[i] >= s[j]) if i < j else (s[i] > s[j])
            beaten = beaten + jnp.where(wins, 1, 0)
        chosen.append(beaten < 2)
    pair = jnp.full_like(best, len(PAIRS) - 1)
    for p in range(len(PAIRS) - 2, -1, -1):
        a, b = PAIRS[p]
        pair = jnp.where(jnp.logical_and(chosen[a], chosen[b]), p, pair)
    cls = best * len(PAIRS) + pair
    cls_ref[0] = cls

    class_id = lax.broadcasted_iota(jnp.int32, (CLASS_ROWS, cls.shape[1]), 0)
    onehot = class_id == cls
    before = _dot(jnp.where(onehot, 1.0, 0.0).astype(BF16), tri_ref[...])
    run = run_ref[...]
    rank = jnp.sum(jnp.where(onehot, before + run[:, 0:1], 0.0), axis=0, keepdims=True)
    rank_ref[0] = rank.astype(jnp.int32)
    run = run + valid * jnp.sum(jnp.where(onehot, 1.0, 0.0), axis=1, keepdims=True)
    run_ref[...] = run
    cnt_ref[...] = run.astype(jnp.int32)


def _strict_upper(n):
    return (jnp.arange(n)[:, None] < jnp.arange(n)[None, :]).astype(BF16)


def _route_out_shapes(t, tm):
    return (
        jax.ShapeDtypeStruct((t, D_MODEL), F32),
        jax.ShapeDtypeStruct((t * ROW_TILE, LANES), F32),
        jax.ShapeDtypeStruct((t // tm, 1, tm), jnp.int32),
        jax.ShapeDtypeStruct((t // tm, 1, tm), jnp.int32),
        jax.ShapeDtypeStruct((CLASS_ROWS, LANES), jnp.int32),
    )


def _cur_tile(n_tiles):
    return lambda i: jnp.minimum(i, n_tiles - 1)


def _prev_tile(i):
    return jnp.maximum(i - 1, 0)


def _route_out_specs(tm, n_tiles):
    cur = _cur_tile(n_tiles)
    return (
        pl.BlockSpec((tm, D_MODEL), lambda i, *_: (cur(i), 0)),
        pl.BlockSpec((tm * ROW_TILE, LANES), lambda i, *_: (_prev_tile(i), 0)),
        pl.BlockSpec((1, 1, tm), lambda i, *_: (_prev_tile(i), 0, 0)),
        pl.BlockSpec((1, 1, tm), lambda i, *_: (_prev_tile(i), 0, 0)),
        pl.BlockSpec((CLASS_ROWS, LANES), lambda i, *_: (0, 0)),
    )


def _init_deferred_tail(xprev_ref, run_ref):
    @pl.when(pl.program_id(0) == 0)
    def _():
        xprev_ref[...] = jnp.zeros_like(xprev_ref)
        run_ref[...] = jnp.zeros_like(run_ref)


def _deferred_tail(xprev_ref, modp_ref, *tail_refs, after_router=None):
    valid = jnp.where(pl.program_id(0) > 0, 1.0, 0.0)
    _route_tail(xprev_ref[...], modp_ref[0], valid, *tail_refs, after_router=after_router)


def _const_spec(shape):
    return pl.BlockSpec(shape, lambda i, *_: (0,) * len(shape))


def _conv_layer_kernel(dest_ref, x_ref, ys_ref, mod0_ref, mod_ref, modp_ref, g1_ref, wi_ref, cw_ref,
                       cb_ref, wo_ref, g2_ref, wr_ref, br_ref, tri_ref,
                       x3_ref, u2_ref, cls_ref, rank_ref, cnt_ref,
                       run_ref, carry_ref, ybuf_ref, ysem_ref, xprev_ref):
    tm = x_ref.shape[0]
    tiles_per_batch = SEQ // tm
    n_tiles = pl.num_programs(0) - 1
    i = pl.program_id(0)

    @pl.when(i % tiles_per_batch == 0)
    def _():
        carry_ref[...] = jnp.zeros_like(carry_ref)

    _init_deferred_tail(xprev_ref, run_ref)
    m = mod_ref[0]
    y = _gathered_tile(dest_ref, ys_ref, ybuf_ref, ysem_ref, tm, issue_first=False, n_tiles=n_tiles)
    x2 = x_ref[...] + mod0_ref[0][5:6] * y
    u = (_rms(x2) * g1_ref[...] * (1.0 + m[1:2]) + m[0:1]).astype(BF16)
    bgate = _dot(u, wi_ref[:, 0:D_MODEL])
    gated = []

    def project_gated():
        gated.append(_dot(u, wi_ref[:, D_MODEL:2 * D_MODEL]) * _dot(u, wi_ref[:, 2 * D_MODEL:]))

    _deferred_tail(xprev_ref, modp_ref, g2_ref, wr_ref, br_ref, tri_ref, u2_ref, cls_ref, rank_ref,
                   cnt_ref, run_ref, after_router=project_gated)
    z = gated[0]
    prev = carry_ref[...]
    row = lax.broadcasted_iota(jnp.int32, z.shape, 0)
    z1 = jnp.where(row == 0, prev[7:8], pltpu.roll(z, 1, axis=0))
    z2 = jnp.where(row == 0, prev[6:7], jnp.where(row == 1, prev[7:8], pltpu.roll(z, 2, axis=0)))
    carry_ref[...] = z[tm - SUBLANES:, :]
    cw = cw_ref[...]
    zc = cw[0:1] * z2 + cw[1:2] * z1 + cw[2:3] * z + cb_ref[...]
    x3 = x2 + m[2:3] * _dot((bgate * zc).astype(BF16), wo_ref[...])
    xprev_ref[...] = x3

    @pl.when(i < n_tiles)
    def _():
        x3_ref[...] = x3

    _drain_row_gather(ys_ref, ybuf_ref, ysem_ref)


def _conv_layer(dest, x, ys, mod0, mod, g1, w_in, conv_w, conv_b, w_out, g2, wr, br):
    t = x.shape[0]
    tm = TM_CONV
    tri = _strict_upper(tm)
    tiles_per_batch = SEQ // tm
    n_tiles = t // tm
    cur = _cur_tile(n_tiles)
    mod_spec = pl.BlockSpec((1, 6, D_MODEL), lambda i, *_: (cur(i) // tiles_per_batch, 0, 0))
    modp_spec = pl.BlockSpec((1, 6, D_MODEL), lambda i, *_: (_prev_tile(i) // tiles_per_batch, 0, 0))
    return pl.pallas_call(
        _conv_layer_kernel,
        out_shape=_route_out_shapes(t, tm),
        grid_spec=pltpu.PrefetchScalarGridSpec(
            num_scalar_prefetch=1,
            grid=(n_tiles + 1,),
            in_specs=[
                pl.BlockSpec((tm, D_MODEL), lambda i, *_: (cur(i), 0)),
                pl.BlockSpec(memory_space=pl.ANY),
                mod_spec,
                mod_spec,
                modp_spec,
                _const_spec((1, D_MODEL)),
                _const_spec((D_MODEL, 3 * D_MODEL)),
                _const_spec((3, D_MODEL)),
                _const_spec((1, D_MODEL)),
                _const_spec((D_MODEL, D_MODEL)),
                _const_spec((1, D_MODEL)),
                _const_spec((D_MODEL, ROUTER_COLS)),
                _const_spec((N_EXPERTS, 1)),
                _const_spec((tm, tm)),
            ],
            out_specs=_route_out_specs(tm, n_tiles),
            scratch_shapes=[
                pltpu.VMEM((CLASS_ROWS, LANES), F32),
                pltpu.VMEM((SUBLANES, D_MODEL), F32),
                pltpu.VMEM((2, tm * ROW_TILE, LANES), F32),
                pltpu.SemaphoreType.DMA((2,)),
                pltpu.VMEM((tm, D_MODEL), F32),
            ],
        ),
        compiler_params=_params("arbitrary"),
        name="conv_layer_route",
    )(dest, x, ys, mod0, mod, mod, g1, w_in, conv_w, conv_b, w_out, g2, wr, br, tri)


def _invert_kernel(dest_ref, fill_ref, src_ref, sem):
    fill = pltpu.make_async_copy(fill_ref, src_ref, sem)
    fill.start()
    fill.wait()
    unroll = 32

    def body(t8, carry):
        for k in range(unroll):
            t = t8 * unroll + k
            src_ref[dest_ref[t]] = t
        return carry

    lax.fori_loop(0, dest_ref.shape[0] // unroll, body, 0)


def _invert_permutation(dest, n_out):
    smem = pl.BlockSpec(memory_space=pltpu.SMEM)
    return pl.pallas_call(
        _invert_kernel,
        out_shape=jax.ShapeDtypeStruct((n_out,), jnp.int32),
        in_specs=[smem, pl.BlockSpec(memory_space=pl.ANY)],
        out_specs=smem,
        scratch_shapes=[pltpu.SemaphoreType.DMA(())],
        name="invert_permutation",
    )(dest, jnp.arange(n_out, dtype=jnp.int32) % dest.shape[0])


def _expert_kernel(ea_ref, eb_ref, chg_ref, nused_ref, short_ref, src_ref, u2_ref, wga_ref, wua_ref,
                   wda_ref, wgb_ref, wub_ref, wdb_ref, wra_ref, wrb_ref, y_ref, xbuf_ref, xsem_ref,
                   *wbf_refs):
    del ea_ref, eb_ref
    j = pl.program_id(0)
    tm = TM_EXPERT

    @pl.when(chg_ref[j] == 1)
    def _():
        for src, dst in zip((wga_ref, wua_ref, wda_ref, wgb_ref, wub_ref, wdb_ref), wbf_refs):
            dst[...] = src[0, 0].astype(BF16)

    load_rows, start_next = _gathered_tile(src_ref, u2_ref, xbuf_ref, xsem_ref, tm, issue_first=None,
                                           n_active=nused_ref[0])

    def leading_rows(rows):
        xb = load_rows(rows).astype(BF16)
        start_next()
        dl = jnp.sum(xb.astype(F32) * (wra_ref[0] - wrb_ref[0]), axis=-1, keepdims=True)
        w_a = _sigmoid(dl)
        w_b = _sigmoid(-dl)

        def ffn(wg_ref, wu_ref, wd_ref):
            out = None
            for lo in range(0, D_EXPERT, D_EXPERT // 2):
                cols = slice(lo, lo + D_EXPERT // 2)
                gate = _dot(xb, wg_ref[:, cols])
                hidden = gate * _sigmoid(gate) * _dot(xb, wu_ref[:, cols])
                part = _dot(hidden.astype(BF16), wd_ref[cols, :])
                out = part if out is None else out + part
            return out

        y = w_a * ffn(*wbf_refs[0:3]) + w_b * ffn(*wbf_refs[3:6])
        _store_token_rows(y_ref, y, rows)
        if rows < tm:
            y_ref[rows * ROW_TILE:, :] = jnp.zeros(((tm - rows) * ROW_TILE, LANES), F32)

    used = j < nused_ref[0]
    for k, rows in enumerate(EXPERT_ROW_STEPS):
        pl.when(used & (short_ref[j] == k))(functools.partial(leading_rows, rows))

    @pl.when(j >= nused_ref[0])
    def _():
        y_ref[...] = jnp.zeros_like(y_ref)

    _drain_row_gather(u2_ref, xbuf_ref, xsem_ref, n_active=nused_ref[0])


def _experts(layer, ea, eb, chg, n_used, short, src, u2, w_gate, w_up, w_down, wr_rows):
    tm = TM_EXPERT
    sel_a = lambda j, ea, eb, *_: (layer, ea[j], 0, 0)
    sel_b = lambda j, ea, eb, *_: (layer, eb[j], 0, 0)
    up_spec = lambda sel: pl.BlockSpec((1, 1, D_MODEL, D_EXPERT), sel)
    down_spec = lambda sel: pl.BlockSpec((1, 1, D_EXPERT, D_MODEL), sel)
    wr_spec = lambda sel: pl.BlockSpec((1, 1, D_MODEL), lambda *a: sel(*a)[1:])
    up_scratch = pltpu.VMEM((D_MODEL, D_EXPERT), BF16)
    down_scratch = pltpu.VMEM((D_EXPERT, D_MODEL), BF16)
    return pl.pallas_call(
        _expert_kernel,
        out_shape=jax.ShapeDtypeStruct((PADDED_ROWS * ROW_TILE, LANES), F32),
        grid_spec=pltpu.PrefetchScalarGridSpec(
            num_scalar_prefetch=5,
            grid=(N_EXPERT_TILES,),
            in_specs=[
                pl.BlockSpec(memory_space=pltpu.SMEM),
                pl.BlockSpec(memory_space=pl.ANY),
                up_spec(sel_a), up_spec(sel_a), down_spec(sel_a),
                up_spec(sel_b), up_spec(sel_b), down_spec(sel_b),
                wr_spec(sel_a), wr_spec(sel_b),
            ],
            out_specs=pl.BlockSpec((tm * ROW_TILE, LANES), lambda j, *_: (j, 0)),
            scratch_shapes=[pltpu.VMEM((EXPERT_GATHER_BUFFERS, tm * ROW_TILE, LANES), F32),
                            pltpu.SemaphoreType.DMA((EXPERT_GATHER_BUFFERS,)),
                            up_scratch, up_scratch, down_scratch, up_scratch, up_scratch, down_scratch],
        ),
        compiler_params=_params("arbitrary"),
        name="grouped_experts",
    )(ea, eb, chg, n_used, short, src, u2, w_gate, w_up, w_down, w_gate, w_up, w_down, wr_rows, wr_rows)


def _final_kernel(dest_ref, x_ref, ys_ref, mod_ref, g_ref, o_ref, ybuf_ref, ysem_ref):
    y = _gathered_tile(dest_ref, ys_ref, ybuf_ref, ysem_ref, x_ref.shape[0], issue_first=True)
    x = x_ref[...] + mod_ref[0][5:6] * y
    o_ref[...] = _rms(x) * g_ref[...]
    _drain_row_gather(ys_ref, ybuf_ref, ysem_ref)


def _final(dest, x, ys, mod, g):
    t = x.shape[0]
    tm = TM_FINAL
    tiles_per_batch = SEQ // tm
    return pl.pallas_call(
        _final_kernel,
        out_shape=jax.ShapeDtypeStruct((t, D_MODEL), F32),
        grid_spec=pltpu.PrefetchScalarGridSpec(
            num_scalar_prefetch=1,
            grid=(t // tm,),
            in_specs=[
                pl.BlockSpec((tm, D_MODEL), lambda i, *_: (i, 0)),
                pl.BlockSpec(memory_space=pl.ANY),
                pl.BlockSpec((1, 6, D_MODEL), lambda i, *_: (i // tiles_per_batch, 0, 0)),
                _const_spec((1, D_MODEL)),
            ],
            out_specs=pl.BlockSpec((tm, D_MODEL), lambda i, *_: (i, 0)),
            scratch_shapes=[pltpu.VMEM((2, tm * ROW_TILE, LANES), F32), pltpu.SemaphoreType.DMA((2,))],
        ),
        compiler_params=_params("arbitrary"),
        name="final_norm",
    )(dest, x, ys, mod, g)


_PAIR_A = np.array([EXPERTS_PER_GROUP * (c // len(PAIRS)) + PAIRS[c % len(PAIRS)][0]
                    for c in range(N_CLASSES)], np.int32)
_PAIR_B = np.array([EXPERTS_PER_GROUP * (c // len(PAIRS)) + PAIRS[c % len(PAIRS)][1]
                    for c in range(N_CLASSES)], np.int32)


def _plan_kernel(cnt_ref, cls_ref, rank_ref, dest_ref, ea_ref, eb_ref, chg_ref, step_ref, nused_ref,
                 start_ref):
    tile = jnp.int32(0)
    for c in range(N_CLASSES):
        count = cnt_ref[c, 0]
        start_ref[c] = tile * TM_EXPERT

        def class_tile(k, carry, c=c, count=count, first=tile):
            j = first + k
            ea_ref[j] = jnp.int32(int(_PAIR_A[c]))
            eb_ref[j] = jnp.int32(int(_PAIR_B[c]))
            chg_ref[j] = jnp.where(k == 0, 1, 0)
            rows = count - k * TM_EXPERT
            step = jnp.int32(0)
            for smaller in EXPERT_ROW_STEPS[:-1]:
                step = step + jnp.where(rows > smaller, 1, 0)
            step_ref[j] = step
            return carry

        class_tiles = (count + TM_EXPERT - 1) // TM_EXPERT
        lax.fori_loop(0, class_tiles, class_tile, 0)
        tile = tile + class_tiles
    nused_ref[0] = tile

    last = jnp.maximum(tile - 1, 0)
    last_a = ea_ref[last]
    last_b = eb_ref[last]

    def unused_tile(j, carry):
        ea_ref[j] = last_a
        eb_ref[j] = last_b
        chg_ref[j] = 0
        step_ref[j] = 0
        return carry

    lax.fori_loop(tile, N_EXPERT_TILES, unused_tile, 0)

    cls = cls_ref[...]
    dest = rank_ref[...]
    for c in range(N_CLASSES):
        dest = dest + jnp.where(cls == c, start_ref[c], 0)
    dest_ref[...] = dest


def _routing_plan(cnt, cls, rank):
    shape = (TOKENS // LANES, LANES)
    smem = pl.BlockSpec(memory_space=pltpu.SMEM)
    vmem = pl.BlockSpec(memory_space=pltpu.VMEM)
    per_tile = jax.ShapeDtypeStruct((N_EXPERT_TILES,), jnp.int32)
    dest, ea, eb, chg, step, n_used = pl.pallas_call(
        _plan_kernel,
        out_shape=(jax.ShapeDtypeStruct(shape, jnp.int32), per_tile, per_tile, per_tile, per_tile,
                   jax.ShapeDtypeStruct((1,), jnp.int32)),
        in_specs=[smem, vmem, vmem],
        out_specs=(vmem, smem, smem, smem, smem, smem),
        scratch_shapes=[pltpu.SMEM((N_CLASSES,), jnp.int32)],
        name="routing_plan",
    )(cnt, cls.reshape(shape), rank.reshape(shape))
    return dest.reshape(-1), ea, eb, chg, step, n_used


def _moe(layer, u2, cls, rank, cnt, w_gate, w_up, w_down, wr_rows):
    dest, ea, eb, chg, step, n_used = _routing_plan(cnt, cls, rank)
    src = _invert_permutation(dest, PADDED_ROWS)
    ys = _experts(layer, ea, eb, chg, n_used, step, src, u2, w_gate, w_up, w_down, wr_rows)
    return dest, ys


def kernel(x, c, norm1_g, norm2_g, w_ada, b_ada, m_w_in, m_b_gates, m_norm_g, m_w_out,
           c_w_in, c_conv_w, c_conv_b, c_w_out, w_router, b_router,
           e_w_gate, e_w_up, e_w_down, final_g):
    xf = x.reshape(TOKENS, D_MODEL)
    mod = _ada(c, w_ada, b_ada)

    bg = jnp.zeros((1, GATE_COLS), F32).at[0, :2 * N_HEADS].set(m_b_gates[0])
    wr = jnp.zeros((D_MODEL, ROUTER_COLS), BF16).at[:, :N_EXPERTS].set(w_router.astype(BF16))
    wr_rows = w_router.astype(BF16).astype(F32).T.reshape(N_EXPERTS, 1, D_MODEL)
    br = b_router.reshape(N_EXPERTS, 1)
    row = lambda v: v.reshape(1, -1)

    x1, u2, cls, rank, cnt = _mlstm_layer(xf, mod[0], row(norm1_g[0]), m_w_in.swapaxes(1, 2), bg,
                                          row(m_norm_g[0]),
                                          m_w_out[0].astype(BF16), row(norm2_g[0]), wr, br)
    dest, ys = _moe(0, u2, cls, rank, cnt, e_w_gate, e_w_up, e_w_down, wr_rows)

    x3, u2, cls, rank, cnt = _conv_layer(
        dest, x1, ys, mod[0], mod[1], row(norm1_g[1]), c_w_in[0].astype(BF16), c_conv_w[0],
        row(c_conv_b[0]), c_w_out[0].astype(BF16), row(norm2_g[1]), wr, br)
    dest, ys = _moe(1, u2, cls, rank, cnt, e_w_gate, e_w_up, e_w_down, wr_rows)

    out = _final(dest, x3, ys, mod[1], row(final_g))
    return out.reshape(BATCH, SEQ, D_MODEL)
```
